```python
import jax, jax.numpy as jnp
from jax import lax
import numpy as np

D_MODEL = 1024
BATCH = 16
SEQ = 2048
DEPTH = 2

GRID_W = 64
D_ATTN = D_MODEL // 2
D_HGRN = D_MODEL // 4
D_CONV = D_MODEL - D_ATTN - D_HGRN
HEAD_DIM = 64
N_HEADS = D_ATTN // HEAD_DIM
N_KV_HEADS = 2
KV_GROUP = N_HEADS // N_KV_HEADS
ROPE_THETA = 10000.0
Q_BLOCK = 128
HGRN_HEAD_DIM = 64
HGRN_HEADS = D_HGRN // HGRN_HEAD_DIM
HGRN_CHUNK = 64
F_MIN = 1e-6
CONV_WIDTH = 31
CONV_PAD = (CONV_WIDTH - 1) // 2
D_FF = ((8 * D_MODEL // 3 + 255) // 256) * 256
EPS = 1e-6
LN_EPS = 1e-5
SPLITS = (D_ATTN, N_KV_HEADS * HEAD_DIM, N_KV_HEADS * HEAD_DIM,
          D_HGRN, D_HGRN, D_HGRN, D_HGRN, D_HGRN, D_CONV, D_CONV)
D_IN_PROJ = int(sum(SPLITS))
SPLIT_IDX = tuple(int(c) for c in np.cumsum(SPLITS)[:-1])

kernel_name = "hybrid_attn_hgrn2_conformer_encoder"


def _rms_norm(x, w, eps=EPS):
    xf = x.astype(jnp.float32)
    y = xf * lax.rsqrt(jnp.mean(xf * xf, axis=-1, keepdims=True) + eps)
    return (y * w.astype(jnp.float32)).astype(x.dtype)


def _layer_norm(x, w, b, eps=LN_EPS):
    xf = x.astype(jnp.float32)
    mu = jnp.mean(xf, axis=-1, keepdims=True)
    xc = xf - mu
    var = jnp.mean(xc * xc, axis=-1, keepdims=True)
    y = xc * lax.rsqrt(var + eps) * w.astype(jnp.float32) + b.astype(jnp.float32)
    return y.astype(x.dtype)


def _axial_rope_tables(seq_len, dtype):
    rows = seq_len // GRID_W
    row_id = jnp.repeat(jnp.arange(rows, dtype=jnp.float32), GRID_W)
    col_id = jnp.tile(jnp.arange(GRID_W, dtype=jnp.float32), rows)
    half = HEAD_DIM // 2
    inv_freq = ROPE_THETA ** (-jnp.arange(0, half, 2, dtype=jnp.float32) / half)
    ang_r = row_id[:, None] * inv_freq[None, :]
    ang_c = col_id[:, None] * inv_freq[None, :]
    ang = jnp.concatenate([ang_r, ang_r, ang_c, ang_c], axis=-1)
    return jnp.cos(ang).astype(dtype), jnp.sin(ang).astype(dtype)


def _axial_rope(x, cos, sin):
    x_r1, x_r2, x_c1, x_c2 = jnp.split(x, 4, axis=-1)
    rot = jnp.concatenate([-x_r2, x_r1, -x_c2, x_c1], axis=-1)
    return x * cos[None, :, None, :] + rot * sin[None, :, None, :]


def _gqa_attention(q, k, v, cos, sin):
    B, S = q.shape[0], q.shape[1]
    q = _axial_rope(q, cos, sin) * (HEAD_DIM ** -0.5)
    k = _axial_rope(k, cos, sin)
    qb = q.reshape(B, S // Q_BLOCK, Q_BLOCK, N_KV_HEADS, KV_GROUP, HEAD_DIM)
    qb = qb.transpose(1, 0, 2, 3, 4, 5)

    def one_block(q_blk):
        s = jnp.einsum('bqkgd,bskd->bkgqs', q_blk, k, preferred_element_type=jnp.float32)
        p = jax.nn.softmax(s, axis=-1).astype(v.dtype)
        return jnp.einsum('bkgqs,bskd->bqkgd', p, v)

    o = lax.map(one_block, qb)
    return o.transpose(1, 0, 2, 3, 4, 5).reshape(B, S, D_ATTN)


def _hgrn2_scan(q, k, v, log_f):
    B, S = q.shape[0], q.shape[1]
    n_chunks = S // HGRN_CHUNK

    def to_chunks(a):
        return a.reshape(B, n_chunks, HGRN_CHUNK, HGRN_HEADS, a.shape[-1]).transpose(1, 0, 3, 2, 4)

    in_chunk_mask = jnp.tril(jnp.ones((HGRN_CHUNK, HGRN_CHUNK), dtype=bool))
    mask5 = in_chunk_mask[None, None, :, :, None]

    def step(state, xs):
        qc, kc, vc, gc = xs
        b = jnp.cumsum(gc, axis=2)
        diff = b[:, :, :, None, :] - b[:, :, None, :, :]
        decay = jnp.where(mask5, jnp.exp(jnp.where(mask5, diff, 0.0)), 0.0)
        scores = jnp.einsum('bhtd,bhsd,bhtsd->bhts', qc, kc, decay)
        o = (jnp.einsum('bhts,bhse->bhte', scores, vc)
             + jnp.einsum('bhtd,bhde->bhte', qc * jnp.exp(b), state))
        b_last = b[:, :, -1:, :]
        new_state = (jnp.exp(b_last[:, :, 0, :])[..., None] * state
                     + jnp.einsum('bhsd,bhse->bhde', kc * jnp.exp(b_last - b), vc))
        return new_state, o

    state0 = jnp.zeros((B, HGRN_HEADS, q.shape[-1], v.shape[-1]), jnp.float32)
    _, o = lax.scan(step, state0, (to_chunks(q), to_chunks(k), to_chunks(v), to_chunks(log_f)))
    return o.transpose(1, 0, 3, 2, 4).reshape(B, S, HGRN_HEADS, v.shape[-1])


def _hgrn2_gates(z, lb):
    z = z.astype(jnp.float32)
    lb = lb.reshape(1, 1, HGRN_HEADS, HGRN_HEAD_DIM)
    f = lb + (1.0 - lb) * jax.nn.sigmoid(z)
    log_f = jnp.log(jnp.maximum(f, F_MIN))
    k = (1.0 - lb) * jax.nn.sigmoid(-z)
    return log_f, k


def _hgrn2_mixer(hq, hf_fwd, hf_bwd, hi, hg, lb_fwd, lb_bwd, gnorm_w):
    B, S = hq.shape[0], hq.shape[1]
    shp = (B, S, HGRN_HEADS, HGRN_HEAD_DIM)
    q = hq.reshape(shp).astype(jnp.float32)
    v = hi.reshape(shp).astype(jnp.float32)
    log_f_fw, k_fw = _hgrn2_gates(hf_fwd.reshape(shp), lb_fwd)
    log_f_bw, k_bw = _hgrn2_gates(hf_bwd.reshape(shp), lb_bwd)
    o_fw = _hgrn2_scan(q, k_fw, v, log_f_fw)
    o_bw = _hgrn2_scan(q[:, ::-1], k_bw[:, ::-1], v[:, ::-1], log_f_bw[:, ::-1])[:, ::-1]
    o = (o_fw + o_bw).astype(hq.dtype)
    g = hg.reshape(shp)
    o = _rms_norm(o, gnorm_w) * jax.nn.silu(g)
    return o.reshape(B, S, D_HGRN)


def _conformer_conv(a, b, dw_w, dw_b, ln_w, ln_b, pw_w, pw_b):
    u = a * jax.nn.sigmoid(b)
    u = lax.conv_general_dilated(
        u, dw_w[:, None, :].astype(u.dtype), window_strides=(1,),
        padding=[(CONV_PAD, CONV_PAD)], dimension_numbers=('NWC', 'WIO', 'NWC'),
        feature_group_count=D_CONV) + dw_b
    u = jax.nn.silu(_layer_norm(u, ln_w, ln_b))
    return u @ pw_w + pw_b


def _fwd_setup_inputs(seed: int = 0) -> dict:
    key = jax.random.key(seed)
    ks = jax.random.split(key, 24)
    f32 = jnp.float32

    def gain(k, shape):
        return 1.0 + 0.02 * jax.random.normal(k, shape, f32)

    def bias(k, shape):
        return 0.02 * jax.random.normal(k, shape, f32)

    def dense(k, shape, fan_in):
        return jax.random.normal(k, shape, f32) * fan_in ** -0.5

    return {
        "x": jax.random.normal(ks[0], (BATCH, SEQ, D_MODEL), f32),
        "mix_norm_w": gain(ks[1], (DEPTH, D_MODEL)),
        "w_in": dense(ks[2], (DEPTH, D_MODEL, D_IN_PROJ), D_MODEL),
        "q_norm_w": gain(ks[3], (DEPTH, HEAD_DIM)),
        "k_norm_w": gain(ks[4], (DEPTH, HEAD_DIM)),
        "hgrn_lb_logits": 0.5 * jax.random.normal(ks[5], (DEPTH, 2, D_HGRN), f32),
        "hgrn_gnorm_w": gain(ks[6], (DEPTH, HGRN_HEAD_DIM)),
        "conv_dw_w": dense(ks[7], (DEPTH, CONV_WIDTH, D_CONV), CONV_WIDTH),
        "conv_dw_b": bias(ks[8], (DEPTH, D_CONV)),
        "conv_ln_w": gain(ks[9], (DEPTH, D_CONV)),
        "conv_ln_b": bias(ks[10], (DEPTH, D_CONV)),
        "conv_pw_w": dense(ks[11], (DEPTH, D_CONV, D_CONV), D_CONV),
        "conv_pw_b": bias(ks[12], (DEPTH, D_CONV)),
        "attn_out_norm_w": gain(ks[13], (DEPTH, D_ATTN)),
        "conv_out_norm_w": gain(ks[14], (DEPTH, D_CONV)),
        "w_out": dense(ks[15], (DEPTH, D_MODEL, D_MODEL), D_MODEL),
        "ffn_norm_w": gain(ks[16], (DEPTH, D_MODEL)),
        "w_gate": dense(ks[17], (DEPTH, D_MODEL, D_FF), D_MODEL),
        "w_up": dense(ks[18], (DEPTH, D_MODEL, D_FF), D_MODEL),
        "w_down": dense(ks[19], (DEPTH, D_FF, D_MODEL), D_FF),
    }


def _fwd_reference(x, mix_norm_w, w_in, q_norm_w, k_norm_w, hgrn_lb_logits, hgrn_gnorm_w,
              conv_dw_w, conv_dw_b, conv_ln_w, conv_ln_b, conv_pw_w, conv_pw_b,
              attn_out_norm_w, conv_out_norm_w, w_out, ffn_norm_w, w_gate, w_up, w_down):
    B, S = x.shape[0], x.shape[1]
    cos, sin = _axial_rope_tables(S, x.dtype)
    sm = jax.nn.softmax(hgrn_lb_logits.astype(jnp.float32), axis=0)
    lower_bounds = jnp.cumsum(sm, axis=0) - sm[0:1]

    for l in range(DEPTH):
        h = _rms_norm(x, mix_norm_w[l])
        proj = h @ w_in[l]
        q, k, v, hq, hf_fw, hf_bw, hi, hg, ca, cb = jnp.split(proj, SPLIT_IDX, axis=-1)

        q = _rms_norm(q.reshape(B, S, N_HEADS, HEAD_DIM), q_norm_w[l])
        k = _rms_norm(k.reshape(B, S, N_KV_HEADS, HEAD_DIM), k_norm_w[l])
        v = v.reshape(B, S, N_KV_HEADS, HEAD_DIM)
        y_attn = _gqa_attention(q, k, v, cos, sin)

        y_hgrn = _hgrn2_mixer(hq, hf_fw, hf_bw, hi, hg,
                              lower_bounds[l, 0], lower_bounds[l, 1], hgrn_gnorm_w[l])

        y_conv = _conformer_conv(ca, cb, conv_dw_w[l], conv_dw_b[l], conv_ln_w[l],
                                 conv_ln_b[l], conv_pw_w[l], conv_pw_b[l])

        mixed = jnp.concatenate([_rms_norm(y_attn, attn_out_norm_w[l]),
                                 y_hgrn,
                                 _rms_norm(y_conv, conv_out_norm_w[l])], axis=-1)
        x = x + mixed @ w_out[l]

        h = _rms_norm(x, ffn_norm_w[l])
        x = x + (jax.nn.silu(h @ w_gate[l]) * (h @ w_up[l])) @ w_down[l]
    return x


import jax as _jax
import jax.numpy as _jnp

TWIN_FORMAT = 'train_step'
FWD_PARAMS = ['x', 'mix_norm_w', 'w_in', 'q_norm_w', 'k_norm_w', 'hgrn_lb_logits', 'hgrn_gnorm_w', 'conv_dw_w', 'conv_dw_b', 'conv_ln_w', 'conv_ln_b', 'conv_pw_w', 'conv_pw_b', 'attn_out_norm_w', 'conv_out_norm_w', 'w_out', 'ffn_norm_w', 'w_gate', 'w_up', 'w_down']
TWIN_WEIGHTS = ['mix_norm_w', 'w_in', 'q_norm_w', 'k_norm_w', 'hgrn_lb_logits', 'hgrn_gnorm_w', 'conv_dw_w', 'conv_dw_b', 'conv_ln_w', 'conv_ln_b', 'conv_pw_w', 'conv_pw_b', 'attn_out_norm_w', 'conv_out_norm_w', 'w_out', 'ffn_norm_w', 'w_gate', 'w_up', 'w_down']
TWIN_DIFF_INPUT = 'x'
TWIN_INPUTS = ['x', 'mix_norm_w', 'w_in', 'q_norm_w', 'k_norm_w', 'hgrn_lb_logits', 'hgrn_gnorm_w', 'conv_dw_w', 'conv_dw_b', 'conv_ln_w', 'conv_ln_b', 'conv_pw_w', 'conv_pw_b', 'attn_out_norm_w', 'conv_out_norm_w', 'w_out', 'ffn_norm_w', 'w_gate', 'w_up', 'w_down', 'loss_target', 'm_mix_norm_w', 'm_w_in', 'm_q_norm_w', 'm_k_norm_w', 'm_hgrn_lb_logits', 'm_hgrn_gnorm_w', 'm_conv_dw_w', 'm_conv_dw_b', 'm_conv_ln_w', 'm_conv_ln_b', 'm_conv_pw_w', 'm_conv_pw_b', 'm_attn_out_norm_w', 'm_conv_out_norm_w', 'm_w_out', 'm_ffn_norm_w', 'm_w_gate', 'm_w_up', 'm_w_down', 'v_mix_norm_w', 'v_w_in', 'v_q_norm_w', 'v_k_norm_w', 'v_hgrn_lb_logits', 'v_hgrn_gnorm_w', 'v_conv_dw_w', 'v_conv_dw_b', 'v_conv_ln_w', 'v_conv_ln_b', 'v_conv_pw_w', 'v_conv_pw_b', 'v_attn_out_norm_w', 'v_conv_out_norm_w', 'v_w_out', 'v_ffn_norm_w', 'v_w_gate', 'v_w_up', 'v_w_down']
TWIN_OUTPUTS = ['loss', 'grad_x', 'grad_mix_norm_w', 'grad_w_in', 'grad_q_norm_w', 'grad_k_norm_w', 'grad_hgrn_lb_logits', 'grad_hgrn_gnorm_w', 'grad_conv_dw_w', 'grad_conv_dw_b', 'grad_conv_ln_w', 'grad_conv_ln_b', 'grad_conv_pw_w', 'grad_conv_pw_b', 'grad_attn_out_norm_w', 'grad_conv_out_norm_w', 'grad_w_out', 'grad_ffn_norm_w', 'grad_w_gate', 'grad_w_up', 'grad_w_down', 'delta_mix_norm_w', 'delta_w_in', 'delta_q_norm_w', 'delta_k_norm_w', 'delta_hgrn_lb_logits', 'delta_hgrn_gnorm_w', 'delta_conv_dw_w', 'delta_conv_dw_b', 'delta_conv_ln_w', 'delta_conv_ln_b', 'delta_conv_pw_w', 'delta_conv_pw_b', 'delta_attn_out_norm_w', 'delta_conv_out_norm_w', 'delta_w_out', 'delta_ffn_norm_w', 'delta_w_gate', 'delta_w_up', 'delta_w_down', 'new_m_mix_norm_w', 'new_m_w_in', 'new_m_q_norm_w', 'new_m_k_norm_w', 'new_m_hgrn_lb_logits', 'new_m_hgrn_gnorm_w', 'new_m_conv_dw_w', 'new_m_conv_dw_b', 'new_m_conv_ln_w', 'new_m_conv_ln_b', 'new_m_conv_pw_w', 'new_m_conv_pw_b', 'new_m_attn_out_norm_w', 'new_m_conv_out_norm_w', 'new_m_w_out', 'new_m_ffn_norm_w', 'new_m_w_gate', 'new_m_w_up', 'new_m_w_down', 'new_v_mix_norm_w', 'new_v_w_in', 'new_v_q_norm_w', 'new_v_k_norm_w', 'new_v_hgrn_lb_logits', 'new_v_hgrn_gnorm_w', 'new_v_conv_dw_w', 'new_v_conv_dw_b', 'new_v_conv_ln_w', 'new_v_conv_ln_b', 'new_v_conv_pw_w', 'new_v_conv_pw_b', 'new_v_attn_out_norm_w', 'new_v_conv_out_norm_w', 'new_v_w_out', 'new_v_ffn_norm_w', 'new_v_w_gate', 'new_v_w_up', 'new_v_w_down']
TWIN_LEAF_KINDS = {'loss': 'loss', 'grad_x': 'grad_x', 'grad_mix_norm_w': 'grad_w', 'grad_w_in': 'grad_w', 'grad_q_norm_w': 'grad_w', 'grad_k_norm_w': 'grad_w', 'grad_hgrn_lb_logits': 'grad_w', 'grad_hgrn_gnorm_w': 'grad_w', 'grad_conv_dw_w': 'grad_w', 'grad_conv_dw_b': 'grad_w', 'grad_conv_ln_w': 'grad_w', 'grad_conv_ln_b': 'grad_w', 'grad_conv_pw_w': 'grad_w', 'grad_conv_pw_b': 'grad_w', 'grad_attn_out_norm_w': 'grad_w', 'grad_conv_out_norm_w': 'grad_w', 'grad_w_out': 'grad_w', 'grad_ffn_norm_w': 'grad_w', 'grad_w_gate': 'grad_w', 'grad_w_up': 'grad_w', 'grad_w_down': 'grad_w', 'delta_mix_norm_w': 'delta_w', 'delta_w_in': 'delta_w', 'delta_q_norm_w': 'delta_w', 'delta_k_norm_w': 'delta_w', 'delta_hgrn_lb_logits': 'delta_w', 'delta_hgrn_gnorm_w': 'delta_w', 'delta_conv_dw_w': 'delta_w', 'delta_conv_dw_b': 'delta_w', 'delta_conv_ln_w': 'delta_w', 'delta_conv_ln_b': 'delta_w', 'delta_conv_pw_w': 'delta_w', 'delta_conv_pw_b': 'delta_w', 'delta_attn_out_norm_w': 'delta_w', 'delta_conv_out_norm_w': 'delta_w', 'delta_w_out': 'delta_w', 'delta_ffn_norm_w': 'delta_w', 'delta_w_gate': 'delta_w', 'delta_w_up': 'delta_w', 'delta_w_down': 'delta_w', 'new_m_mix_norm_w': 'new_m', 'new_m_w_in': 'new_m', 'new_m_q_norm_w': 'new_m', 'new_m_k_norm_w': 'new_m', 'new_m_hgrn_lb_logits': 'new_m', 'new_m_hgrn_gnorm_w': 'new_m', 'new_m_conv_dw_w': 'new_m', 'new_m_conv_dw_b': 'new_m', 'new_m_conv_ln_w': 'new_m', 'new_m_conv_ln_b': 'new_m', 'new_m_conv_pw_w': 'new_m', 'new_m_conv_pw_b': 'new_m', 'new_m_attn_out_norm_w': 'new_m', 'new_m_conv_out_norm_w': 'new_m', 'new_m_w_out': 'new_m', 'new_m_ffn_norm_w': 'new_m', 'new_m_w_gate': 'new_m', 'new_m_w_up': 'new_m', 'new_m_w_down': 'new_m', 'new_v_mix_norm_w': 'new_v', 'new_v_w_in': 'new_v', 'new_v_q_norm_w': 'new_v', 'new_v_k_norm_w': 'new_v', 'new_v_hgrn_lb_logits': 'new_v', 'new_v_hgrn_gnorm_w': 'new_v', 'new_v_conv_dw_w': 'new_v', 'new_v_conv_dw_b': 'new_v', 'new_v_conv_ln_w': 'new_v', 'new_v_conv_ln_b': 'new_v', 'new_v_conv_pw_w': 'new_v', 'new_v_conv_pw_b': 'new_v', 'new_v_attn_out_norm_w': 'new_v', 'new_v_conv_out_norm_w': 'new_v', 'new_v_w_out': 'new_v', 'new_v_ffn_norm_w': 'new_v', 'new_v_w_gate': 'new_v', 'new_v_w_up': 'new_v', 'new_v_w_down': 'new_v'}


def _forward(args):
    return _fwd_reference(*[args[k] for k in FWD_PARAMS])


def _output_shape():
    out = _jax.eval_shape(lambda: _forward(_fwd_setup_inputs(0)))
    return out.shape, out.dtype

N_MICROBATCH = 1
ADAM_LR = 0.001
ADAM_B1 = 0.9
ADAM_B2 = 0.999
ADAM_EPS = 1e-08
ADAM_WD = 0.01
ADAM_STEP = 10
PER_EXAMPLE_BATCH_AXIS = {'x': 0, 'loss_target': 0}
SHARED_INPUTS = []
_WEIGHT_DTYPES = {'mix_norm_w': _jnp.float32, 'w_in': _jnp.float32, 'q_norm_w': _jnp.float32, 'k_norm_w': _jnp.float32, 'hgrn_lb_logits': _jnp.float32, 'hgrn_gnorm_w': _jnp.float32, 'conv_dw_w': _jnp.float32, 'conv_dw_b': _jnp.float32, 'conv_ln_w': _jnp.float32, 'conv_ln_b': _jnp.float32, 'conv_pw_w': _jnp.float32, 'conv_pw_b': _jnp.float32, 'attn_out_norm_w': _jnp.float32, 'conv_out_norm_w': _jnp.float32, 'w_out': _jnp.float32, 'ffn_norm_w': _jnp.float32, 'w_gate': _jnp.float32, 'w_up': _jnp.float32, 'w_down': _jnp.float32}
MOMENT_SCALE = {'mix_norm_w': 1.052646e+01, 'w_in': 5.934892e+00, 'q_norm_w': 5.139027e+00, 'k_norm_w': 5.919691e+00, 'hgrn_lb_logits': 1.078433e-01, 'hgrn_gnorm_w': 4.587032e+01, 'conv_dw_w': 2.633006e+00, 'conv_dw_b': 3.838037e+01, 'conv_ln_w': 1.452286e+01, 'conv_ln_b': 2.007988e+01, 'conv_pw_w': 9.731278e+00, 'conv_pw_b': 4.166102e+01, 'attn_out_norm_w': 3.826758e+01, 'conv_out_norm_w': 3.302794e+01, 'w_out': 1.075763e+01, 'ffn_norm_w': 2.502165e+01, 'w_gate': 1.146686e+00, 'w_up': 1.267306e+00, 'w_down': 1.958058e+00}


def _to_microbatches(a, axis):
    t = _jnp.moveaxis(a, axis, 0)
    t = t.reshape((N_MICROBATCH, t.shape[0] // N_MICROBATCH) + t.shape[1:])
    return _jnp.moveaxis(t, 1, axis + 1)


def setup_inputs(seed: int = 0) -> dict:
    inp = _fwd_setup_inputs(seed)
    key = _jax.random.fold_in(_jax.random.key(seed), 7919)
    shape, _ = _output_shape()
    out = dict(inp)
    out["loss_target"] = _jax.random.normal(_jax.random.fold_in(key, 0), shape, _jnp.float32)
    for i, name in enumerate(TWIN_WEIGHTS):
        w = inp[name].astype(_jnp.float32)
        if MOMENT_SCALE is None:
            s = _jnp.sqrt(_jnp.mean(_jnp.square(w)) + 1e-30)
        else:
            s = MOMENT_SCALE[name]
        km, kv = _jax.random.split(_jax.random.fold_in(key, i + 1))
        out[name] = w
        out["m_" + name] = s * _jax.random.normal(km, w.shape, _jnp.float32)
        out["v_" + name] = (s * s) * _jax.random.uniform(kv, w.shape, _jnp.float32, 0.5, 1.5)
    if N_MICROBATCH > 1:
        for name, axis in PER_EXAMPLE_BATCH_AXIS.items():
            out[name] = _to_microbatches(out[name], axis)
    return {'x': out['x'], 'mix_norm_w': out['mix_norm_w'], 'w_in': out['w_in'], 'q_norm_w': out['q_norm_w'], 'k_norm_w': out['k_norm_w'], 'hgrn_lb_logits': out['hgrn_lb_logits'], 'hgrn_gnorm_w': out['hgrn_gnorm_w'], 'conv_dw_w': out['conv_dw_w'], 'conv_dw_b': out['conv_dw_b'], 'conv_ln_w': out['conv_ln_w'], 'conv_ln_b': out['conv_ln_b'], 'conv_pw_w': out['conv_pw_w'], 'conv_pw_b': out['conv_pw_b'], 'attn_out_norm_w': out['attn_out_norm_w'], 'conv_out_norm_w': out['conv_out_norm_w'], 'w_out': out['w_out'], 'ffn_norm_w': out['ffn_norm_w'], 'w_gate': out['w_gate'], 'w_up': out['w_up'], 'w_down': out['w_down'], 'loss_target': out['loss_target'], 'm_mix_norm_w': out['m_mix_norm_w'], 'm_w_in': out['m_w_in'], 'm_q_norm_w': out['m_q_norm_w'], 'm_k_norm_w': out['m_k_norm_w'], 'm_hgrn_lb_logits': out['m_hgrn_lb_logits'], 'm_hgrn_gnorm_w': out['m_hgrn_gnorm_w'], 'm_conv_dw_w': out['m_conv_dw_w'], 'm_conv_dw_b': out['m_conv_dw_b'], 'm_conv_ln_w': out['m_conv_ln_w'], 'm_conv_ln_b': out['m_conv_ln_b'], 'm_conv_pw_w': out['m_conv_pw_w'], 'm_conv_pw_b': out['m_conv_pw_b'], 'm_attn_out_norm_w': out['m_attn_out_norm_w'], 'm_conv_out_norm_w': out['m_conv_out_norm_w'], 'm_w_out': out['m_w_out'], 'm_ffn_norm_w': out['m_ffn_norm_w'], 'm_w_gate': out['m_w_gate'], 'm_w_up': out['m_w_up'], 'm_w_down': out['m_w_down'], 'v_mix_norm_w': out['v_mix_norm_w'], 'v_w_in': out['v_w_in'], 'v_q_norm_w': out['v_q_norm_w'], 'v_k_norm_w': out['v_k_norm_w'], 'v_hgrn_lb_logits': out['v_hgrn_lb_logits'], 'v_hgrn_gnorm_w': out['v_hgrn_gnorm_w'], 'v_conv_dw_w': out['v_conv_dw_w'], 'v_conv_dw_b': out['v_conv_dw_b'], 'v_conv_ln_w': out['v_conv_ln_w'], 'v_conv_ln_b': out['v_conv_ln_b'], 'v_conv_pw_w': out['v_conv_pw_w'], 'v_conv_pw_b': out['v_conv_pw_b'], 'v_attn_out_norm_w': out['v_attn_out_norm_w'], 'v_conv_out_norm_w': out['v_conv_out_norm_w'], 'v_w_out': out['v_w_out'], 'v_ffn_norm_w': out['v_ffn_norm_w'], 'v_w_gate': out['v_w_gate'], 'v_w_up': out['v_w_up'], 'v_w_down': out['v_w_down']}


def _loss(weights, diff, rest, loss_target):
    with _jax.named_scope("forward"):
        args = {**rest, TWIN_DIFF_INPUT: diff, **{k: w.astype(_WEIGHT_DTYPES[k]) for k, w in weights.items()}}
        y = _forward(args)
    with _jax.named_scope("loss_head"):
        err = _jnp.square(y.astype(_jnp.float32) - loss_target)
        return 0.5 * _jnp.sum(_jnp.mean(err, axis=-1)) if err.ndim else 0.5 * err


def _adamw(w, g, m, v):
    m = ADAM_B1 * m + (1.0 - ADAM_B1) * g
    v = ADAM_B2 * v + (1.0 - ADAM_B2) * _jnp.square(g)
    m_hat = m / (1.0 - ADAM_B1 ** ADAM_STEP)
    v_hat = v / (1.0 - ADAM_B2 ** ADAM_STEP)
    delta = -ADAM_LR * (m_hat / (_jnp.sqrt(v_hat) + ADAM_EPS) + ADAM_WD * w)
    return delta, m, v


def reference(x, mix_norm_w, w_in, q_norm_w, k_norm_w, hgrn_lb_logits, hgrn_gnorm_w, conv_dw_w, conv_dw_b, conv_ln_w, conv_ln_b, conv_pw_w, conv_pw_b, attn_out_norm_w, conv_out_norm_w, w_out, ffn_norm_w, w_gate, w_up, w_down, loss_target, m_mix_norm_w, m_w_in, m_q_norm_w, m_k_norm_w, m_hgrn_lb_logits, m_hgrn_gnorm_w, m_conv_dw_w, m_conv_dw_b, m_conv_ln_w, m_conv_ln_b, m_conv_pw_w, m_conv_pw_b, m_attn_out_norm_w, m_conv_out_norm_w, m_w_out, m_ffn_norm_w, m_w_gate, m_w_up, m_w_down, v_mix_norm_w, v_w_in, v_q_norm_w, v_k_norm_w, v_hgrn_lb_logits, v_hgrn_gnorm_w, v_conv_dw_w, v_conv_dw_b, v_conv_ln_w, v_conv_ln_b, v_conv_pw_w, v_conv_pw_b, v_attn_out_norm_w, v_conv_out_norm_w, v_w_out, v_ffn_norm_w, v_w_gate, v_w_up, v_w_down):
    given = dict(x=x, mix_norm_w=mix_norm_w, w_in=w_in, q_norm_w=q_norm_w, k_norm_w=k_norm_w, hgrn_lb_logits=hgrn_lb_logits, hgrn_gnorm_w=hgrn_gnorm_w, conv_dw_w=conv_dw_w, conv_dw_b=conv_dw_b, conv_ln_w=conv_ln_w, conv_ln_b=conv_ln_b, conv_pw_w=conv_pw_w, conv_pw_b=conv_pw_b, attn_out_norm_w=attn_out_norm_w, conv_out_norm_w=conv_out_norm_w, w_out=w_out, ffn_norm_w=ffn_norm_w, w_gate=w_gate, w_up=w_up, w_down=w_down, loss_target=loss_target, m_mix_norm_w=m_mix_norm_w, m_w_in=m_w_in, m_q_norm_w=m_q_norm_w, m_k_norm_w=m_k_norm_w, m_hgrn_lb_logits=m_hgrn_lb_logits, m_hgrn_gnorm_w=m_hgrn_gnorm_w, m_conv_dw_w=m_conv_dw_w, m_conv_dw_b=m_conv_dw_b, m_conv_ln_w=m_conv_ln_w, m_conv_ln_b=m_conv_ln_b, m_conv_pw_w=m_conv_pw_w, m_conv_pw_b=m_conv_pw_b, m_attn_out_norm_w=m_attn_out_norm_w, m_conv_out_norm_w=m_conv_out_norm_w, m_w_out=m_w_out, m_ffn_norm_w=m_ffn_norm_w, m_w_gate=m_w_gate, m_w_up=m_w_up, m_w_down=m_w_down, v_mix_norm_w=v_mix_norm_w, v_w_in=v_w_in, v_q_norm_w=v_q_norm_w, v_k_norm_w=v_k_norm_w, v_hgrn_lb_logits=v_hgrn_lb_logits, v_hgrn_gnorm_w=v_hgrn_gnorm_w, v_conv_dw_w=v_conv_dw_w, v_conv_dw_b=v_conv_dw_b, v_conv_ln_w=v_conv_ln_w, v_conv_ln_b=v_conv_ln_b, v_conv_pw_w=v_conv_pw_w, v_conv_pw_b=v_conv_pw_b, v_attn_out_norm_w=v_attn_out_norm_w, v_conv_out_norm_w=v_conv_out_norm_w, v_w_out=v_w_out, v_ffn_norm_w=v_ffn_norm_w, v_w_gate=v_w_gate, v_w_up=v_w_up, v_w_down=v_w_down)
    weights = {n: given[n] for n in TWIN_WEIGHTS}
    shared = {n: given[n] for n in SHARED_INPUTS}
    per_example = {n: given[n] for n in ['x']}
    grad_fn = _jax.value_and_grad(_loss, argnums=(0, 1))

    def one_microbatch(ex, loss_target):
        ex = dict(ex)
        diff = ex.pop(TWIN_DIFF_INPUT)
        return grad_fn(weights, diff, {**shared, **ex}, loss_target)

    if N_MICROBATCH == 1:
        loss, (grad_w, grad_x) = one_microbatch(per_example, given["loss_target"])
    else:
        def body(carry, xs):
            loss_sum, grad_sum = carry
            l_k, (gw_k, gx_k) = one_microbatch(xs[0], xs[1])
            with _jax.named_scope("update"):
                return (loss_sum + l_k, _jax.tree.map(_jnp.add, grad_sum, gw_k)), gx_k

        init = (_jnp.zeros((), _jnp.float32), _jax.tree.map(_jnp.zeros_like, weights))
        (loss, grad_w), grad_x = _jax.lax.scan(body, init, (per_example, given["loss_target"]))
    with _jax.named_scope("update"):
        delta_w, new_m, new_v = {}, {}, {}
        for n in TWIN_WEIGHTS:
            delta_w[n], new_m[n], new_v[n] = _adamw(weights[n], grad_w[n], given["m_" + n], given["v_" + n])
    return (loss, grad_x, *[grad_w[n] for n in TWIN_WEIGHTS], *[delta_w[n] for n in TWIN_WEIGHTS],
            *[new_m[n] for n in TWIN_WEIGHTS], *[new_v[n] for n in TWIN_WEIGHTS])
```

```python
import functools

import jax
import jax.numpy as jnp
import numpy as np
from jax import lax
from jax.experimental import pallas as pl
from jax.experimental.pallas import tpu as pltpu

F32 = jnp.float32
BF16 = jnp.bfloat16

D_MODEL = 1024
D_ATTN = 512
D_HGRN = 256
D_CONV = 256
HEAD_DIM = 64
N_HEADS = 8
N_KV = 2
GRID_W = 64
ROPE_THETA = 10000.0
CHUNK = 64
F_MIN = 1e-6
CONV_W = 31
CONV_PAD = 15
D_FF = 2816
D_PROJ = 2560
EPS = 1e-6
LN_EPS = 1e-5
DEPTH = 2
ADAM_LR = 0.001
ADAM_B1 = 0.9
ADAM_B2 = 0.999
ADAM_EPS = 1e-08
ADAM_WD = 0.01
ADAM_STEP = 10
N_DEV = 8
MESH_AXES = ("x", "y", "c")

COL_HQ, COL_ZFW, COL_ZBW, COL_HI, COL_HG = 6, 8, 10, 12, 14
COL_CA, COL_CB = 8, 9

LANES = 128
VMEM_LIMIT_MB = 56


def _cparams(dims=None):
    return pltpu.CompilerParams(dimension_semantics=dims, vmem_limit_bytes=VMEM_LIMIT_MB * 2 ** 20)


def _dot(a, b):
    return jnp.dot(a, b, preferred_element_type=F32)


def _dot_nt(a, b):
    return lax.dot_general(a, b, (((1,), (1,)), ((), ())), preferred_element_type=F32)


def _dot_tn(a, b):
    return lax.dot_general(a, b, (((0,), (0,)), ((), ())), preferred_element_type=F32)


def _split_bf16(x, parts):
    out = []
    r = x
    for _ in range(parts):
        p = r.astype(BF16)
        out.append(p)
        r = r - p.astype(F32)
    return out


def _dot_precise(x, m_bf16, parts=3):
    acc = None
    for p in _split_bf16(x, parts):
        t = _dot(p, m_bf16)
        acc = t if acc is None else acc + t
    return acc


def _block_ones(width, group):
    i = np.arange(width)
    return jnp.asarray((i[:, None] // group) == (i[None, :] // group), dtype=BF16)


def _sigmoid(x):
    return 1.0 / (1.0 + jnp.exp(-x))


def _rot(x):
    w = x.shape[1]
    lane = lax.broadcasted_iota(jnp.int32, x.shape, 1)
    first = (lane % 32) < 16
    return jnp.where(first, -pltpu.roll(x, w - 16, 1), pltpu.roll(x, 16, 1))


def _rope(x, cos, sin):
    return x * cos + _rot(x) * sin


def _rope_t(dy, cos, sin):
    return dy * cos - _rot(dy * sin)


def _row(v):
    return v.reshape(1, -1)


def _rms_proj(x, wn, w, tm):
    t, d = x.shape
    n = w.shape[1]

    def body(x_ref, wn_ref, w_ref, h_ref, y_ref):
        xv = x_ref[...]
        r = lax.rsqrt(jnp.mean(xv * xv, axis=-1, keepdims=True) + EPS)
        h = (xv * r * wn_ref[...]).astype(BF16)
        h_ref[...] = h
        y_ref[...] = _dot(h, w_ref[...])

    return pl.pallas_call(
        body, name="rms_proj", grid=(t // tm,),
        in_specs=[pl.BlockSpec((tm, d), lambda i: (i, 0)),
                  pl.BlockSpec((1, d), lambda i: (0, 0)),
                  pl.BlockSpec((d, n), lambda i: (0, 0))],
        out_specs=[pl.BlockSpec((tm, d), lambda i: (i, 0)),
                   pl.BlockSpec((tm, n), lambda i: (i, 0))],
        out_shape=[jax.ShapeDtypeStruct((t, d), BF16), jax.ShapeDtypeStruct((t, n), F32)],
        compiler_params=_cparams(("parallel",)),
    )(x, wn, w)


def _rms_bwd(dh, x, wn):
    r = lax.rsqrt(jnp.mean(x * x, axis=-1, keepdims=True) + EPS)
    g = dh * wn
    dx = r * (g - x * (r * r) * jnp.mean(g * x, axis=-1, keepdims=True))
    return dx, dh * x * r


def _proj_bwd(dproj, w, x, wn, dres, tm):
    t, n = dproj.shape
    d = x.shape[1]

    def body(dp_ref, w_ref, x_ref, wn_ref, dr_ref, dx_ref, dwn_ref):
        dh = _dot_nt(dp_ref[...], w_ref[...])
        dx, dwn = _rms_bwd(dh, x_ref[...], wn_ref[...])
        dx_ref[...] = dr_ref[...] + dx

        @pl.when(pl.program_id(0) == 0)
        def _():
            dwn_ref[...] = jnp.zeros_like(dwn_ref)

        dwn_ref[...] += jnp.sum(dwn, axis=0, keepdims=True)

    return pl.pallas_call(
        body, name="proj_bwd", grid=(t // tm,),
        in_specs=[pl.BlockSpec((tm, n), lambda i: (i, 0)),
                  pl.BlockSpec((d, n), lambda i: (0, 0)),
                  pl.BlockSpec((tm, d), lambda i: (i, 0)),
                  pl.BlockSpec((1, d), lambda i: (0, 0)),
                  pl.BlockSpec((tm, d), lambda i: (i, 0))],
        out_specs=[pl.BlockSpec((tm, d), lambda i: (i, 0)),
                   pl.BlockSpec((1, d), lambda i: (0, 0))],
        out_shape=[jax.ShapeDtypeStruct((t, d), F32), jax.ShapeDtypeStruct((1, d), F32)],
        compiler_params=_cparams(("arbitrary",)),
    )(dproj, w, x, wn, dres)


def _mm_tn(a, b, tn, name, tm):
    t, k = a.shape
    n = b.shape[1]

    def body(a_ref, b_ref, o_ref):
        @pl.when(pl.program_id(1) == 0)
        def _():
            o_ref[...] = jnp.zeros_like(o_ref)

        o_ref[...] += _dot_tn(a_ref[...].astype(BF16), b_ref[...].astype(BF16))

    return pl.pallas_call(
        body, name=name, grid=(n // tn, t // tm),
        in_specs=[pl.BlockSpec((tm, k), lambda j, i: (i, 0)),
                  pl.BlockSpec((tm, tn), lambda j, i: (i, j))],
        out_specs=pl.BlockSpec((k, tn), lambda j, i: (0, j)),
        out_shape=jax.ShapeDtypeStruct((k, n), F32),
        compiler_params=_cparams(("parallel", "arbitrary")),
    )(a, b)


def _rope_tables(s):
    rows = s // GRID_W
    row_id = jnp.repeat(jnp.arange(rows, dtype=F32), GRID_W)
    col_id = jnp.tile(jnp.arange(GRID_W, dtype=F32), rows)
    half = HEAD_DIM // 2
    inv_freq = ROPE_THETA ** (-jnp.arange(0, half, 2, dtype=F32) / half)
    ang_r = row_id[:, None] * inv_freq[None, :]
    ang_c = col_id[:, None] * inv_freq[None, :]
    ang = jnp.concatenate([ang_r, ang_r, ang_c, ang_c], axis=-1)
    cos, sin = jnp.cos(ang), jnp.sin(ang)
    return jnp.tile(cos, (1, N_HEADS)), jnp.tile(sin, (1, N_HEADS))


def _head_rms(x, w, ones):
    r = lax.rsqrt(_dot_precise(x * x, ones, 2) * (1.0 / HEAD_DIM) + EPS)
    return x * r * w, r


def _dup_half(x, kv):
    lane = lax.broadcasted_iota(jnp.int32, x.shape, 1)
    sel = (lane < 64) if kv == 0 else (lane >= 64)
    return jnp.where(sel, x, pltpu.roll(x, 64, 1))


def _qkv_prep(proj, cosq, sinq, qw, kw, ones, s, tm):
    t = proj.shape[0]
    ns = s // tm

    def body(p_ref, cos_ref, sin_ref, qw_ref, kw_ref, ones_ref, q_out, kd_out, vd_out, kdt_out, vdt_out):
        cos = cos_ref[...]
        sin = sin_ref[...]
        ones_m = ones_ref[...]
        qn, _ = _head_rms(p_ref[:, 0:512], qw_ref[...], ones_m)
        q_out[...] = (_rope(qn, cos, sin) * (HEAD_DIM ** -0.5)).astype(BF16)
        kn, _ = _head_rms(p_ref[:, 512:640], kw_ref[...], ones_m[0:128, 0:128])
        kr = _rope(kn, cos[:, 0:128], sin[:, 0:128])
        v = p_ref[:, 640:768]
        for kv in range(N_KV):
            kd = _dup_half(kr, kv)
            vd = _dup_half(v, kv)
            kd_out[kv] = kd.astype(BF16)
            vd_out[kv] = vd.astype(BF16)
            kdt_out[kv] = kd.T.astype(BF16)
            vdt_out[kv] = vd.T.astype(BF16)

    return pl.pallas_call(
        body, name="qkv_prep", grid=(t // tm,),
        in_specs=[pl.BlockSpec((tm, 768), lambda i: (i, 0)),
                  pl.BlockSpec((tm, 512), lambda i: (i % ns, 0)),
                  pl.BlockSpec((tm, 512), lambda i: (i % ns, 0)),
                  pl.BlockSpec((1, 512), lambda i: (0, 0)),
                  pl.BlockSpec((1, 128), lambda i: (0, 0)),
                  pl.BlockSpec((512, 512), lambda i: (0, 0))],
        out_specs=[pl.BlockSpec((tm, 512), lambda i: (i, 0)),
                   pl.BlockSpec((N_KV, tm, 128), lambda i: (0, i, 0)),
                   pl.BlockSpec((N_KV, tm, 128), lambda i: (0, i, 0)),
                   pl.BlockSpec((N_KV, 128, tm), lambda i: (0, 0, i)),
                   pl.BlockSpec((N_KV, 128, tm), lambda i: (0, 0, i))],
        out_shape=[jax.ShapeDtypeStruct((t, 512), BF16),
                   jax.ShapeDtypeStruct((N_KV, t, 128), BF16),
                   jax.ShapeDtypeStruct((N_KV, t, 128), BF16),
                   jax.ShapeDtypeStruct((N_KV, 128, t), BF16),
                   jax.ShapeDtypeStruct((N_KV, 128, t), BF16)],
        compiler_params=_cparams(("parallel",)),
    )(proj, cosq, sinq, qw, kw, ones)


def _qkv_bwd(proj, dq, dkd, dvd, cosq, sinq, qw, kw, ones, s, tm):
    t = proj.shape[0]
    ns = s // tm

    def body(p_ref, dq_ref, dkd_ref, dvd_ref, cos_ref, sin_ref, qw_ref, kw_ref, ones_ref,
             out_ref, dqw_ref, dkw_ref):
        cos = cos_ref[...]
        sin = sin_ref[...]
        ones_m = ones_ref[...]
        ones_k = ones_m[0:128, 0:128]

        def norm_bwd(x, w, dn, om):
            r = lax.rsqrt(_dot_precise(x * x, om, 2) * (1.0 / HEAD_DIM) + EPS)
            g = dn * w
            dx = r * (g - x * (r * r) * (_dot_precise(g * x, om, 2) * (1.0 / HEAD_DIM)))
            return dx, jnp.sum(dn * x * r, axis=0, keepdims=True)

        q = p_ref[:, 0:512]
        dqn = _rope_t(dq_ref[...], cos, sin) * (HEAD_DIM ** -0.5)
        dq_raw, dqw = norm_bwd(q, qw_ref[...], dqn, ones_m)
        out_ref[:, 0:512] = dq_raw.astype(BF16)

        lane = lax.broadcasted_iota(jnp.int32, (tm, 128), 1)

        def fold(ref):
            a0 = ref[0]
            a1 = ref[1]
            f0 = a0 + pltpu.roll(a0, 64, 1)
            f1 = a1 + pltpu.roll(a1, 64, 1)
            return jnp.where(lane < 64, f0, f1)

        k = p_ref[:, 512:640]
        dkn = _rope_t(fold(dkd_ref), cos[:, 0:128], sin[:, 0:128])
        dk_raw, dkw = norm_bwd(k, kw_ref[...], dkn, ones_k)
        out_ref[:, 512:640] = dk_raw.astype(BF16)
        out_ref[:, 640:768] = fold(dvd_ref).astype(BF16)

        @pl.when(pl.program_id(0) == 0)
        def _():
            dqw_ref[...] = jnp.zeros_like(dqw_ref)
            dkw_ref[...] = jnp.zeros_like(dkw_ref)

        dqw_ref[...] += dqw
        dkw_ref[...] += dkw

    return pl.pallas_call(
        body, name="qkv_bwd", grid=(t // tm,),
        in_specs=[pl.BlockSpec((tm, 768), lambda i: (i, 0)),
                  pl.BlockSpec((tm, 512), lambda i: (i, 0)),
                  pl.BlockSpec((N_KV, tm, 128), lambda i: (0, i, 0)),
                  pl.BlockSpec((N_KV, tm, 128), lambda i: (0, i, 0)),
                  pl.BlockSpec((tm, 512), lambda i: (i % ns, 0)),
                  pl.BlockSpec((tm, 512), lambda i: (i % ns, 0)),
                  pl.BlockSpec((1, 512), lambda i: (0, 0)),
                  pl.BlockSpec((1, 128), lambda i: (0, 0)),
                  pl.BlockSpec((512, 512), lambda i: (0, 0))],
        out_specs=[pl.BlockSpec((tm, 768), lambda i: (i, 0)),
                   pl.BlockSpec((1, 512), lambda i: (0, 0)),
                   pl.BlockSpec((1, 128), lambda i: (0, 0))],
        out_shape=[jax.ShapeDtypeStruct((t, 768), BF16),
                   jax.ShapeDtypeStruct((1, 512), F32),
                   jax.ShapeDtypeStruct((1, 128), F32)],
        compiler_params=_cparams(("arbitrary",)),
    )(proj, dq, dkd, dvd, cosq, sinq, qw, kw, ones)


def _attn_fwd(q, kd, vdt, nb, s, tq):
    t = q.shape[0]
    nq = s // tq

    def body(q_ref, k_ref, vt_ref, o_ref, lse_ref):
        qv = q_ref[...].astype(F32)
        lane = lax.broadcasted_iota(jnp.int32, qv.shape, 1)
        k = k_ref[0]
        vt = vt_ref[0]
        outs = []
        for half in range(2):
            qh = jnp.where((lane < 64) if half == 0 else (lane >= 64), qv, 0.0).astype(BF16)
            st = _dot_nt(k, qh)
            m = jnp.max(st, axis=0, keepdims=True)
            p = jnp.exp(st - m)
            l = jnp.sum(p, axis=0, keepdims=True)
            ot = _dot(vt, p.astype(BF16)) / l
            lse_ref[0, half] = m + jnp.log(l)
            outs.append(ot)
        row = lax.broadcasted_iota(jnp.int32, outs[0].shape, 0)
        o_ref[...] = jnp.where(row < 64, outs[0], outs[1]).T

    return pl.pallas_call(
        body, name="attn_fwd", grid=(nb, N_HEADS // 2, nq),
        in_specs=[pl.BlockSpec((tq, 128), lambda b, p, i: (b * nq + i, p)),
                  pl.BlockSpec((1, s, 128), lambda b, p, i: (p // 2, b, 0)),
                  pl.BlockSpec((1, 128, s), lambda b, p, i: (p // 2, 0, b))],
        out_specs=[pl.BlockSpec((tq, 128), lambda b, p, i: (b * nq + i, p)),
                   pl.BlockSpec((1, 2, 1, tq), lambda b, p, i: (b, p, 0, i))],
        out_shape=[jax.ShapeDtypeStruct((t, D_ATTN), F32),
                   jax.ShapeDtypeStruct((nb, N_HEADS, 1, s), F32)],
        compiler_params=_cparams(("parallel", "parallel", "parallel")),
    )(q, kd, vdt)


def _attn_bwd(q, kd, vd, kdt, o, lse, do, nb, s, tq):
    t = q.shape[0]
    nq = s // tq
    ones8 = jnp.ones((8, 128), BF16)

    def body(q_ref, k_ref, v_ref, kt_ref, o_ref, lse_ref, do_ref, ones_ref, dq_ref, dk_ref, dv_ref):
        @pl.when((pl.program_id(2) == 0) & (pl.program_id(3) == 0))
        def _():
            dk_ref[...] = jnp.zeros_like(dk_ref)
            dv_ref[...] = jnp.zeros_like(dv_ref)

        qv = q_ref[...].astype(F32)
        dov = do_ref[...]
        ov = o_ref[...]
        lane = lax.broadcasted_iota(jnp.int32, qv.shape, 1)
        k = k_ref[0]
        v = v_ref[0]
        kt = kt_ref[0]
        dqs = []
        dk_acc = None
        dv_acc = None
        for half in range(2):
            sel = (lane < 64) if half == 0 else (lane >= 64)
            qh = jnp.where(sel, qv, 0.0).astype(BF16)
            doh = jnp.where(sel, dov, 0.0)
            dob = doh.astype(BF16)
            delta = None
            for part in _split_bf16(doh * ov, 3):
                d8 = _dot_nt(ones_ref[...], part)
                delta = d8 if delta is None else delta + d8
            delta = delta[0:1, :]
            st = _dot_nt(k, qh)
            pt = jnp.exp(st - lse_ref[0, half])
            dpt = _dot_nt(v, dob)
            dst = (pt * (dpt - delta)).astype(BF16)
            dkh = _dot(dst, qh)
            dvh = _dot(pt.astype(BF16), dob)
            dk_acc = dkh if dk_acc is None else dk_acc + dkh
            dv_acc = dvh if dv_acc is None else dv_acc + dvh
            dqs.append(_dot(kt, dst))
        dk_ref[0] += dk_acc
        dv_ref[0] += dv_acc
        row = lax.broadcasted_iota(jnp.int32, dqs[0].shape, 0)
        dq_ref[...] = jnp.where(row < 64, dqs[0], dqs[1]).T

    qmap = lambda b, g, p, i: (b * nq + i, g * 2 + p)
    kvmap = lambda b, g, p, i: (g, b, 0)
    return pl.pallas_call(
        body, name="attn_bwd", grid=(nb, N_KV, 2, nq),
        in_specs=[pl.BlockSpec((tq, 128), qmap),
                  pl.BlockSpec((1, s, 128), kvmap),
                  pl.BlockSpec((1, s, 128), kvmap),
                  pl.BlockSpec((1, 128, s), lambda b, g, p, i: (g, 0, b)),
                  pl.BlockSpec((tq, 128), qmap),
                  pl.BlockSpec((1, 2, 1, tq), lambda b, g, p, i: (b, g * 2 + p, 0, i)),
                  pl.BlockSpec((tq, 128), qmap),
                  pl.BlockSpec((8, 128), lambda b, g, p, i: (0, 0))],
        out_specs=[pl.BlockSpec((tq, 128), qmap),
                   pl.BlockSpec((1, s, 128), kvmap),
                   pl.BlockSpec((1, s, 128), kvmap)],
        out_shape=[jax.ShapeDtypeStruct((t, D_ATTN), F32),
                   jax.ShapeDtypeStruct((N_KV, t, 128), F32),
                   jax.ShapeDtypeStruct((N_KV, t, 128), F32)],
        compiler_params=_cparams(("parallel", "parallel", "arbitrary", "arbitrary")),
    )(q, kd, vd, kdt, o, lse, do, ones8)


def _tri_mats():
    i = np.arange(CHUNK)
    lower = jnp.asarray(i[:, None] >= i[None, :], dtype=BF16)
    upper = jnp.asarray(i[:, None] <= i[None, :], dtype=BF16)
    return jnp.stack([lower, upper])


def _running_sum(tri, x):
    acc = None
    for part in _split_bf16(x, 3):
        t = _dot(tri, part)
        acc = t if acc is None else acc + t
    return acc


def _gates(z, lb):
    sig = _sigmoid(z)
    f = lb + (1.0 - lb) * sig
    logf = jnp.log(jnp.maximum(f, F_MIN))
    k = (1.0 - lb) * (1.0 - sig)
    return sig, f, logf, k


def _row_group(jg, anti):
    if anti:
        return 0, 8 * jg + 8
    return 8 * jg, CHUNK


def _chunk_fwd(q, k, v, b, st, bones, bmask, anti):
    b_last = b[0:1] if anti else b[CHUNK - 1:CHUNK]
    qb = q * jnp.exp(b)
    o_inter = _dot_nt(qb.astype(BF16), st.astype(BF16))
    kt = k * jnp.exp(b_last - b)
    st_new = st * jnp.exp(b_last) + _dot_tn(v.astype(BF16), kt.astype(BF16)) * bmask
    tt = lax.broadcasted_iota(jnp.int32, (CHUNK, LANES), 0)
    blocks = []
    for jg in range(8):
        r0, r1 = _row_group(jg, anti)
        qr = q[r0:r1]
        br = b[r0:r1]
        tr = tt[r0:r1]
        acc = None
        for i in range(8):
            sc = 8 * jg + i
            mask = (tr <= sc) if anti else (tr >= sc)
            e = jnp.where(mask, jnp.exp(jnp.minimum(br - b[sc:sc + 1], 0.0)), 0.0)
            w = qr * e * k[sc:sc + 1]
            pb = _dot(w.astype(BF16), bones)
            term = pb * v[sc:sc + 1]
            acc = term if acc is None else acc + term
        blocks.append((r0, r1, acc))
    o = o_inter
    pieces = []
    for g in range(8):
        tot = o[8 * g:8 * g + 8]
        for (r0, r1, acc) in blocks:
            if r0 <= 8 * g and 8 * g + 8 <= r1:
                tot = tot + acc[8 * g - r0:8 * g - r0 + 8]
        pieces.append(tot)
    return jnp.concatenate(pieces, axis=0), st_new


def _hgrn_fwd(proj, lb, gw, nb, s):
    t = proj.shape[0]
    nc = s // CHUNK
    tri = _tri_mats()
    bones = _block_ones(LANES, HEAD_DIM)

    def body(q_ref, zf_ref, zb_ref, v_ref, g_ref, lb_ref, gw_ref, tri_ref, bones_ref, y_ref, os_ref, st_ref):
        bones_m = bones_ref[...]
        bmask = bones_m.astype(F32)
        for anti in (False, True):
            z_ref = zb_ref if anti else zf_ref
            lbv = lb_ref[1:2] if anti else lb_ref[0:1]
            trim = tri_ref[1] if anti else tri_ref[0]
            st_ref[...] = jnp.zeros_like(st_ref)

            def step(n, carry, anti=anti, z_ref=z_ref, lbv=lbv, trim=trim):
                cn = (nc - 1 - n) if anti else n
                rows = pl.ds(pl.multiple_of(cn * CHUNK, CHUNK), CHUNK)
                q = q_ref[rows, :]
                v = v_ref[rows, :]
                _, _, logf, k = _gates(z_ref[rows, :], lbv)
                b = _running_sum(trim, logf)
                o, st_new = _chunk_fwd(q, k, v, b, st_ref[...], bones_m, bmask, anti)
                st_ref[...] = st_new
                if anti:
                    osum = os_ref[rows, :] + o
                    os_ref[rows, :] = osum
                    r = lax.rsqrt(_dot_precise(osum * osum, bones_m, 2) * (1.0 / HEAD_DIM) + EPS)
                    hg = g_ref[rows, :]
                    y_ref[rows, :] = osum * r * gw_ref[...] * (hg * _sigmoid(hg))
                else:
                    os_ref[rows, :] = o
                return carry

            lax.fori_loop(0, nc, step, 0)

    def col(c):
        return pl.BlockSpec((s, LANES), lambda b, p, c=c: (b, c + p))

    return pl.pallas_call(
        body, name="hgrn_fwd", grid=(nb, 2),
        in_specs=[col(COL_HQ), col(COL_ZFW), col(COL_ZBW), col(COL_HI), col(COL_HG),
                  pl.BlockSpec((2, LANES), lambda b, p: (0, p)),
                  pl.BlockSpec((1, LANES), lambda b, p: (0, 0)),
                  pl.BlockSpec((2, CHUNK, CHUNK), lambda b, p: (0, 0, 0)),
                  pl.BlockSpec((LANES, LANES), lambda b, p: (0, 0))],
        out_specs=[pl.BlockSpec((s, LANES), lambda b, p: (b, p)),
                   pl.BlockSpec((s, LANES), lambda b, p: (b, p))],
        out_shape=[jax.ShapeDtypeStruct((t, D_HGRN), F32), jax.ShapeDtypeStruct((t, D_HGRN), F32)],
        scratch_shapes=[pltpu.VMEM((LANES, LANES), F32)],
        compiler_params=_cparams(("parallel", "parallel")),
    )(proj, proj, proj, proj, proj, lb, gw, tri, bones)


def _chunk_bwd(q, k, v, b, do, st_in, rt, bones, bmask, anti):
    b_last = b[0:1] if anti else b[CHUNK - 1:CHUNK]
    eb = jnp.exp(b)
    ebl = jnp.exp(b_last - b)
    dob = do.astype(BF16)
    rtb = rt.astype(BF16)
    dq_inter = eb * _dot(dob, st_in.astype(BF16))
    dk_inter = ebl * _dot(v.astype(BF16), rtb)
    dv_inter = _dot_nt((k * ebl).astype(BF16), rtb)
    rt_new = rt * jnp.exp(b_last) + _dot_tn(dob, (q * eb).astype(BF16)) * bmask
    tt = lax.broadcasted_iota(jnp.int32, (CHUNK, LANES), 0)
    r8 = lax.broadcasted_iota(jnp.int32, (8, LANES), 0)
    blocks = []
    dk_pieces = []
    dv_pieces = []
    for jg in range(8):
        r0, r1 = _row_group(jg, anti)
        qr = q[r0:r1]
        br = b[r0:r1]
        tr = tt[r0:r1]
        dor = do[r0:r1]
        acc = None
        dk_blk = jnp.zeros((8, LANES), F32)
        dv_blk = jnp.zeros((8, LANES), F32)
        for i in range(8):
            sc = 8 * jg + i
            mask = (tr <= sc) if anti else (tr >= sc)
            e = jnp.where(mask, jnp.exp(jnp.minimum(br - b[sc:sc + 1], 0.0)), 0.0)
            qe = qr * e
            ke = e * k[sc:sc + 1]
            pb = _dot((qe * k[sc:sc + 1]).astype(BF16), bones)
            dpb = _dot((dor * v[sc:sc + 1]).astype(BF16), bones)
            term = dpb * ke
            acc = term if acc is None else acc + term
            dk_s = jnp.sum(dpb * qe, axis=0, keepdims=True)
            dv_s = jnp.sum(pb * dor, axis=0, keepdims=True)
            dk_blk = jnp.where(r8 == i, dk_s, dk_blk)
            dv_blk = jnp.where(r8 == i, dv_s, dv_blk)
        blocks.append((r0, r1, acc))
        dk_pieces.append(dk_blk)
        dv_pieces.append(dv_blk)
    dq_pieces = []
    for g in range(8):
        tot = dq_inter[8 * g:8 * g + 8]
        for (r0, r1, acc) in blocks:
            if r0 <= 8 * g and 8 * g + 8 <= r1:
                tot = tot + acc[8 * g - r0:8 * g - r0 + 8]
        dq_pieces.append(tot)
    dq = jnp.concatenate(dq_pieces, axis=0)
    dk = dk_inter + jnp.concatenate(dk_pieces, axis=0)
    dv = dv_inter + jnp.concatenate(dv_pieces, axis=0)
    db_last = (jnp.sum(k * dk_inter, axis=0, keepdims=True)
               + jnp.exp(b_last) * jnp.sum(rt * st_in, axis=0, keepdims=True))
    return dq, dk, dv, rt_new, db_last


def _hgrn_bwd(proj, lb, gw, osum, dy, nb, s):
    t = proj.shape[0]
    nc = s // CHUNK
    tri = _tri_mats()
    bones = _block_ones(LANES, HEAD_DIM)

    def body(q_ref, zf_ref, zb_ref, v_ref, g_ref, lb_ref, gw_ref, os_ref, dy_ref, tri_ref, bones_ref,
             dq_ref, dzf_ref, dzb_ref, dv_ref, dg_ref, dgw_ref, dlb_ref,
             do_sc, dq_sc, dv_sc, st_sc, st_cur, rt_cur):
        bones_m = bones_ref[...]
        bmask = bones_m.astype(F32)
        gwv = gw_ref[...]

        def head(n, acc):
            rows = pl.ds(pl.multiple_of(n * CHUNK, CHUNK), CHUNK)
            o = os_ref[rows, :]
            hg = g_ref[rows, :]
            dyv = dy_ref[rows, :]
            sg = _sigmoid(hg)
            r = lax.rsqrt(_dot_precise(o * o, bones_m, 2) * (1.0 / HEAD_DIM) + EPS)
            nrm = o * r * gwv
            dn = dyv * (hg * sg)
            dg_ref[rows, :] = (dyv * nrm * (sg * (1.0 + hg * (1.0 - sg)))).astype(BF16)
            g = dn * gwv
            mean_go = _dot_precise(g * o, bones_m, 2) * (1.0 / HEAD_DIM)
            do_sc[rows, :] = r * (g - o * (r * r) * mean_go)
            return acc + jnp.sum(dn * o * r, axis=0, keepdims=True)

        dgw_ref[0] = lax.fori_loop(0, nc, head, jnp.zeros((1, LANES), F32))
        dq_sc[...] = jnp.zeros_like(dq_sc)
        dv_sc[...] = jnp.zeros_like(dv_sc)

        for anti in (False, True):
            z_ref = zb_ref if anti else zf_ref
            dz_ref = dzb_ref if anti else dzf_ref
            lbv = lb_ref[1:2] if anti else lb_ref[0:1]
            trim = tri_ref[1] if anti else tri_ref[0]
            trim_r = tri_ref[0] if anti else tri_ref[1]

            def load(cn, z_ref=z_ref, lbv=lbv, trim=trim):
                rows = pl.ds(pl.multiple_of(cn * CHUNK, CHUNK), CHUNK)
                q = q_ref[rows, :]
                v = v_ref[rows, :]
                sig, f, logf, k = _gates(z_ref[rows, :], lbv)
                b = _running_sum(trim, logf)
                return rows, q, v, sig, f, k, b

            st_cur[...] = jnp.zeros_like(st_cur)

            def sweep(n, carry, anti=anti, load=load):
                cn = (nc - 1 - n) if anti else n
                _, q, v, _, _, k, b = load(cn)
                st_sc[cn] = st_cur[...]
                b_last = b[0:1] if anti else b[CHUNK - 1:CHUNK]
                kt = k * jnp.exp(b_last - b)
                st_cur[...] = st_cur[...] * jnp.exp(b_last) + _dot_tn(v.astype(BF16), kt.astype(BF16)) * bmask
                return carry

            lax.fori_loop(0, nc, sweep, 0)

            rt_cur[...] = jnp.zeros_like(rt_cur)

            def back(n, dlb, anti=anti, load=load, lbv=lbv, trim_r=trim_r, dz_ref=dz_ref):
                cn = n if anti else (nc - 1 - n)
                rows, q, v, sig, f, k, b = load(cn)
                do = do_sc[rows, :]
                dq, dk, dv, rt_new, db_last = _chunk_bwd(q, k, v, b, do, st_sc[cn], rt_cur[...], bones_m,
                                                         bmask, anti)
                rt_cur[...] = rt_new
                dq_sc[rows, :] += dq
                dv_sc[rows, :] += dv
                dlogf = _running_sum(trim_r, q * dq - k * dk) + db_last
                dfl = jnp.where(f > F_MIN, dlogf / f, 0.0)
                dz_ref[rows, :] = ((dfl - dk) * (1.0 - lbv) * sig * (1.0 - sig)).astype(BF16)
                return dlb + jnp.sum((dfl - dk) * (1.0 - sig), axis=0, keepdims=True)

            dlb = lax.fori_loop(0, nc, back, jnp.zeros((1, LANES), F32))
            side = 1 if anti else 0
            dlb_ref[0, side:side + 1, :] = dlb

        dq_ref[...] = dq_sc[...].astype(BF16)
        dv_ref[...] = dv_sc[...].astype(BF16)

    def col(c):
        return pl.BlockSpec((s, LANES), lambda b, p, c=c: (b, c + p))

    sl = pl.BlockSpec((s, LANES), lambda b, p: (b, p))
    out_t = jax.ShapeDtypeStruct((t, D_HGRN), BF16)
    return pl.pallas_call(
        body, name="hgrn_bwd", grid=(nb, 2),
        in_specs=[col(COL_HQ), col(COL_ZFW), col(COL_ZBW), col(COL_HI), col(COL_HG),
                  pl.BlockSpec((2, LANES), lambda b, p: (0, p)),
                  pl.BlockSpec((1, LANES), lambda b, p: (0, 0)),
                  sl, sl,
                  pl.BlockSpec((2, CHUNK, CHUNK), lambda b, p: (0, 0, 0)),
                  pl.BlockSpec((LANES, LANES), lambda b, p: (0, 0))],
        out_specs=[sl, sl, sl, sl, sl,
                   pl.BlockSpec((1, 1, LANES), lambda b, p: (b, 0, p)),
                   pl.BlockSpec((1, 2, LANES), lambda b, p: (b, 0, p))],
        out_shape=[out_t, out_t, out_t, out_t, out_t,
                   jax.ShapeDtypeStruct((nb, 1, D_HGRN), F32),
                   jax.ShapeDtypeStruct((nb, 2, D_HGRN), F32)],
        scratch_shapes=[pltpu.VMEM((s, LANES), F32), pltpu.VMEM((s, LANES), F32), pltpu.VMEM((s, LANES), F32),
                        pltpu.VMEM((nc, LANES, LANES), F32), pltpu.VMEM((LANES, LANES), F32),
                        pltpu.VMEM((LANES, LANES), F32)],
        compiler_params=_cparams(("parallel", "parallel")),
    )(proj, proj, proj, proj, proj, lb, gw, osum, dy, tri, bones)


def _lower_bounds(logits):
    def body(lg_ref, lb_ref):
        rows = [lg_ref[l:l + 1, :] for l in range(DEPTH)]
        m = functools.reduce(jnp.maximum, rows)
        ex = [jnp.exp(r - m) for r in rows]
        den = functools.reduce(jnp.add, ex)
        run = jnp.zeros_like(m)
        for l in range(DEPTH):
            if l > 0:
                run = run + ex[l] / den
            lb_ref[l:l + 1, :] = run

    return pl.pallas_call(body, name="lower_bounds", out_shape=jax.ShapeDtypeStruct(logits.shape, F32))(logits)


def _lower_bounds_bwd(logits, dlb):
    def body(lg_ref, dlb_ref, dlg_ref):
        rows = [lg_ref[l:l + 1, :] for l in range(DEPTH)]
        m = functools.reduce(jnp.maximum, rows)
        ex = [jnp.exp(r - m) for r in rows]
        den = functools.reduce(jnp.add, ex)
        sm = [e / den for e in ex]
        dsm = [jnp.zeros_like(m) for _ in range(DEPTH)]
        for i in range(1, DEPTH):
            for l in range(i, DEPTH):
                dsm[i] = dsm[i] + dlb_ref[l:l + 1, :]
        dot = functools.reduce(jnp.add, [sm[i] * dsm[i] for i in range(DEPTH)])
        for i in range(DEPTH):
            dlg_ref[i:i + 1, :] = sm[i] * (dsm[i] - dot)

    return pl.pallas_call(body, name="lower_bounds_bwd", out_shape=jax.ShapeDtypeStruct(logits.shape, F32))(logits, dlb)


CONV_ROWS = 128


def _conv_core(a, bg, dww, dwb, lnw, lnb, upad_ref, s):
    sb = _sigmoid(bg)
    u = a * sb
    upad_ref[0:16, :] = jnp.zeros((16, D_CONV), F32)
    upad_ref[16:16 + s, :] = u
    upad_ref[16 + s:32 + s, :] = jnp.zeros((16, D_CONV), F32)
    rows = min(s, CONV_ROWS)
    pieces = []
    for r0 in range(0, s, rows):
        acc = None
        for j in range(CONV_W):
            term = upad_ref[r0 + 1 + j:r0 + 1 + j + rows, :] * dww[j:j + 1, :]
            acc = term if acc is None else acc + term
        pieces.append(acc)
    c = jnp.concatenate(pieces, axis=0) + dwb
    mu = jnp.mean(c, axis=-1, keepdims=True)
    xc = c - mu
    rstd = lax.rsqrt(jnp.mean(xc * xc, axis=-1, keepdims=True) + LN_EPS)
    nh = xc * rstd
    l = nh * lnw + lnb
    sl = _sigmoid(l)
    return sb, nh, rstd, l, sl


def _conv_fwd(proj, dww, dwb, lnw, lnb, pww, pwb, nb, s):
    t = proj.shape[0]
    assert s % min(s, CONV_ROWS) == 0

    def body(a_ref, b_ref, dww_ref, dwb_ref, lnw_ref, lnb_ref, pww_ref, pwb_ref, y_ref, upad_ref):
        _, _, _, l, sl = _conv_core(a_ref[...], b_ref[...], dww_ref[...], dwb_ref[...], lnw_ref[...],
                                    lnb_ref[...], upad_ref, s)
        y_ref[...] = _dot((l * sl).astype(BF16), pww_ref[...]) + pwb_ref[...]

    vec = pl.BlockSpec((1, D_CONV), lambda b: (0, 0))
    return pl.pallas_call(
        body, name="conv_fwd", grid=(nb,),
        in_specs=[pl.BlockSpec((s, D_CONV), lambda b: (b, COL_CA)),
                  pl.BlockSpec((s, D_CONV), lambda b: (b, COL_CB)),
                  pl.BlockSpec((32, D_CONV), lambda b: (0, 0)), vec, vec, vec,
                  pl.BlockSpec((D_CONV, D_CONV), lambda b: (0, 0)), vec],
        out_specs=pl.BlockSpec((s, D_CONV), lambda b: (b, 0)),
        out_shape=jax.ShapeDtypeStruct((t, D_CONV), F32),
        scratch_shapes=[pltpu.VMEM((s + 32, D_CONV), F32)],
        compiler_params=_cparams(("parallel",)),
    )(proj, proj, dww, dwb, lnw, lnb, pww, pwb)


def _conv_bwd(proj, dy, dww, dwb, lnw, lnb, pww, nb, s):
    t = proj.shape[0]

    def body(a_ref, b_ref, dy_ref, dww_ref, dwb_ref, lnw_ref, lnb_ref, pww_ref,
             dab_ref, ddww_ref, ddwb_ref, dlnw_ref, dlnb_ref, dpww_ref, dpwb_ref, upad_ref, dcpad_ref):
        a = a_ref[...]
        dww = dww_ref[...]
        sb, nh, rstd, l, sl = _conv_core(a, b_ref[...], dww, dwb_ref[...], lnw_ref[...], lnb_ref[...],
                                         upad_ref, s)
        dyv = dy_ref[...]
        dyb = dyv.astype(BF16)
        ds = _dot_nt(dyb, pww_ref[...])
        dl = ds * (sl * (1.0 + l * (1.0 - sl)))
        dn = dl * lnw_ref[...]
        dc = rstd * (dn - jnp.mean(dn, axis=-1, keepdims=True)
                     - nh * jnp.mean(dn * nh, axis=-1, keepdims=True))

        @pl.when(pl.program_id(0) == 0)
        def _():
            for r in (ddww_ref, ddwb_ref, dlnw_ref, dlnb_ref, dpww_ref, dpwb_ref):
                r[...] = jnp.zeros_like(r)

        dpww_ref[...] += _dot_tn((l * sl).astype(BF16), dyb)
        dpwb_ref[...] += jnp.sum(dyv, axis=0, keepdims=True)
        dlnw_ref[...] += jnp.sum(dl * nh, axis=0, keepdims=True)
        dlnb_ref[...] += jnp.sum(dl, axis=0, keepdims=True)
        ddwb_ref[...] += jnp.sum(dc, axis=0, keepdims=True)

        dcpad_ref[0:16, :] = jnp.zeros((16, D_CONV), F32)
        dcpad_ref[16:16 + s, :] = dc
        dcpad_ref[16 + s:32 + s, :] = jnp.zeros((16, D_CONV), F32)
        rows = min(s, CONV_ROWS)
        r8 = lax.broadcasted_iota(jnp.int32, (32, D_CONV), 0)
        ddww = jnp.zeros((32, D_CONV), F32)
        pieces = []
        for r0 in range(0, s, rows):
            acc = None
            dcr = dcpad_ref[16 + r0:16 + r0 + rows, :]
            for j in range(CONV_W):
                term = dcpad_ref[r0 + 31 - j:r0 + 31 - j + rows, :] * dww[j:j + 1, :]
                acc = term if acc is None else acc + term
                wj = jnp.sum(dcr * upad_ref[r0 + 1 + j:r0 + 1 + j + rows, :], axis=0, keepdims=True)
                ddww = ddww + jnp.where(r8 == j, wj, 0.0)
            pieces.append(acc)
        du = jnp.concatenate(pieces, axis=0)
        ddww_ref[...] += ddww
        dab_ref[:, 0:D_CONV] = (du * sb).astype(BF16)
        dab_ref[:, D_CONV:2 * D_CONV] = (du * a * sb * (1.0 - sb)).astype(BF16)

    vec = pl.BlockSpec((1, D_CONV), lambda b: (0, 0))
    mat = pl.BlockSpec((D_CONV, D_CONV), lambda b: (0, 0))
    w32 = pl.BlockSpec((32, D_CONV), lambda b: (0, 0))
    vshape = jax.ShapeDtypeStruct((1, D_CONV), F32)
    return pl.pallas_call(
        body, name="conv_bwd", grid=(nb,),
        in_specs=[pl.BlockSpec((s, D_CONV), lambda b: (b, COL_CA)),
                  pl.BlockSpec((s, D_CONV), lambda b: (b, COL_CB)),
                  pl.BlockSpec((s, D_CONV), lambda b: (b, 0)),
                  w32, vec, vec, vec, mat],
        out_specs=[pl.BlockSpec((s, 2 * D_CONV), lambda b: (b, 0)), w32, vec, vec, vec, mat, vec],
        out_shape=[jax.ShapeDtypeStruct((t, 2 * D_CONV), BF16),
                   jax.ShapeDtypeStruct((32, D_CONV), F32), vshape, vshape, vshape,
                   jax.ShapeDtypeStruct((D_CONV, D_CONV), F32), vshape],
        scratch_shapes=[pltpu.VMEM((s + 32, D_CONV), F32), pltpu.VMEM((s + 32, D_CONV), F32)],
        compiler_params=_cparams(("arbitrary",)),
    )(proj, proj, dy, dww, dwb, lnw, lnb, pww)


def _mix_out(o_attn, y_hgrn, y_conv, x, aw, cw, w_out, tm):
    t = x.shape[0]

    def body(o_ref, h_ref, c_ref, x_ref, aw_ref, cw_ref, w_ref, mixed_ref, x1_ref):
        o = o_ref[...]
        a = o * lax.rsqrt(jnp.mean(o * o, axis=-1, keepdims=True) + EPS) * aw_ref[...]
        yc = c_ref[...]
        c = yc * lax.rsqrt(jnp.mean(yc * yc, axis=-1, keepdims=True) + EPS) * cw_ref[...]
        ab, hb, cb = a.astype(BF16), h_ref[...].astype(BF16), c.astype(BF16)
        mixed_ref[:, 0:512] = ab
        mixed_ref[:, 512:768] = hb
        mixed_ref[:, 768:1024] = cb
        x1_ref[...] = (x_ref[...] + _dot(ab, w_ref[0:512, :]) + _dot(hb, w_ref[512:768, :])
                       + _dot(cb, w_ref[768:1024, :]))

    def tok(w):
        return pl.BlockSpec((tm, w), lambda i: (i, 0))

    return pl.pallas_call(
        body, name="mix_out", grid=(t // tm,),
        in_specs=[tok(512), tok(256), tok(256), tok(D_MODEL),
                  pl.BlockSpec((1, 512), lambda i: (0, 0)), pl.BlockSpec((1, 256), lambda i: (0, 0)),
                  pl.BlockSpec((D_MODEL, D_MODEL), lambda i: (0, 0))],
        out_specs=[tok(D_MODEL), tok(D_MODEL)],
        out_shape=[jax.ShapeDtypeStruct((t, D_MODEL), BF16), jax.ShapeDtypeStruct((t, D_MODEL), F32)],
        compiler_params=_cparams(("parallel",)),
    )(o_attn, y_hgrn, y_conv, x, aw, cw, w_out)


def _mix_out_bwd(dx1, w_out, o_attn, y_conv, aw, cw, tm):
    t = dx1.shape[0]

    def body(dx_ref, w_ref, o_ref, c_ref, aw_ref, cw_ref, do_ref, dh_ref, dc_ref, daw_ref, dcw_ref):
        dm = _dot_nt(dx_ref[...].astype(BF16), w_ref[...])
        do, daw = _rms_bwd(dm[:, 0:512], o_ref[...], aw_ref[...])
        dc, dcw = _rms_bwd(dm[:, 768:1024], c_ref[...], cw_ref[...])
        do_ref[...] = do
        dh_ref[...] = dm[:, 512:768]
        dc_ref[...] = dc

        @pl.when(pl.program_id(0) == 0)
        def _():
            daw_ref[...] = jnp.zeros_like(daw_ref)
            dcw_ref[...] = jnp.zeros_like(dcw_ref)

        daw_ref[...] += jnp.sum(daw, axis=0, keepdims=True)
        dcw_ref[...] += jnp.sum(dcw, axis=0, keepdims=True)

    def tok(w):
        return pl.BlockSpec((tm, w), lambda i: (i, 0))

    v512 = pl.BlockSpec((1, 512), lambda i: (0, 0))
    v256 = pl.BlockSpec((1, 256), lambda i: (0, 0))
    return pl.pallas_call(
        body, name="mix_out_bwd", grid=(t // tm,),
        in_specs=[tok(D_MODEL), pl.BlockSpec((D_MODEL, D_MODEL), lambda i: (0, 0)), tok(512), tok(256),
                  v512, v256],
        out_specs=[tok(512), tok(256), tok(256), v512, v256],
        out_shape=[jax.ShapeDtypeStruct((t, 512), F32), jax.ShapeDtypeStruct((t, 256), F32),
                   jax.ShapeDtypeStruct((t, 256), F32), jax.ShapeDtypeStruct((1, 512), F32),
                   jax.ShapeDtypeStruct((1, 256), F32)],
        compiler_params=_cparams(("arbitrary",)),
    )(dx1, w_out, o_attn, y_conv, aw, cw)


FF_TILE = 1408


def _ffn_fwd(x1, fw, wg, wu, wd, tm):
    t = x1.shape[0]
    nf = D_FF // FF_TILE

    def body(x_ref, fw_ref, wg_ref, wu_ref, wd_ref, h_ref, g_ref, u_ref, a_ref, x2_ref, acc_ref):
        j = pl.program_id(1)

        @pl.when(j == 0)
        def _():
            xv = x_ref[...]
            r = lax.rsqrt(jnp.mean(xv * xv, axis=-1, keepdims=True) + EPS)
            h_ref[...] = (xv * r * fw_ref[...]).astype(BF16)
            acc_ref[...] = xv

        h = h_ref[...]
        g = _dot(h, wg_ref[...])
        u = _dot(h, wu_ref[...])
        a = (g * _sigmoid(g) * u).astype(BF16)
        g_ref[...] = g.astype(BF16)
        u_ref[...] = u.astype(BF16)
        a_ref[...] = a
        acc_ref[...] += _dot(a, wd_ref[...])

        @pl.when(j == nf - 1)
        def _():
            x2_ref[...] = acc_ref[...]

    tok = pl.BlockSpec((tm, D_MODEL), lambda i, j: (i, 0))
    ffb = pl.BlockSpec((tm, FF_TILE), lambda i, j: (i, j))
    ffs = jax.ShapeDtypeStruct((t, D_FF), BF16)
    return pl.pallas_call(
        body, name="ffn_fwd", grid=(t // tm, nf),
        in_specs=[tok, pl.BlockSpec((1, D_MODEL), lambda i, j: (0, 0)),
                  pl.BlockSpec((D_MODEL, FF_TILE), lambda i, j: (0, j)),
                  pl.BlockSpec((D_MODEL, FF_TILE), lambda i, j: (0, j)),
                  pl.BlockSpec((FF_TILE, D_MODEL), lambda i, j: (j, 0))],
        out_specs=[tok, ffb, ffb, ffb, tok],
        out_shape=[jax.ShapeDtypeStruct((t, D_MODEL), BF16), ffs, ffs, ffs,
                   jax.ShapeDtypeStruct((t, D_MODEL), F32)],
        scratch_shapes=[pltpu.VMEM((tm, D_MODEL), F32)],
        compiler_params=_cparams(("parallel", "arbitrary")),
    )(x1, fw, wg, wu, wd)


def _ffn_bwd(dx2, g, u, wg, wu, wd, x1, fw, tm):
    t = dx2.shape[0]
    nf = D_FF // FF_TILE

    def body(dx_ref, g_ref, u_ref, wg_ref, wu_ref, wd_ref, x_ref, fw_ref,
             dg_ref, du_ref, dx1_ref, dfw_ref, acc_ref):
        i = pl.program_id(0)
        j = pl.program_id(1)
        da = _dot_nt(dx_ref[...].astype(BF16), wd_ref[...])
        gv = g_ref[...].astype(F32)
        uv = u_ref[...].astype(F32)
        sg = _sigmoid(gv)
        dg = (da * uv * (sg * (1.0 + gv * (1.0 - sg)))).astype(BF16)
        du = (da * gv * sg).astype(BF16)
        dg_ref[...] = dg
        du_ref[...] = du
        dh = _dot_nt(dg, wg_ref[...]) + _dot_nt(du, wu_ref[...])

        @pl.when(j == 0)
        def _():
            acc_ref[...] = dh

        @pl.when(j > 0)
        def _():
            acc_ref[...] += dh

        @pl.when((i == 0) & (j == 0))
        def _():
            dfw_ref[...] = jnp.zeros_like(dfw_ref)

        @pl.when(j == nf - 1)
        def _():
            dx, dfw = _rms_bwd(acc_ref[...], x_ref[...], fw_ref[...])
            dx1_ref[...] = dx_ref[...] + dx
            dfw_ref[...] += jnp.sum(dfw, axis=0, keepdims=True)

    tok = pl.BlockSpec((tm, D_MODEL), lambda i, j: (i, 0))
    ffb = pl.BlockSpec((tm, FF_TILE), lambda i, j: (i, j))
    ffs = jax.ShapeDtypeStruct((t, D_FF), BF16)
    vec = pl.BlockSpec((1, D_MODEL), lambda i, j: (0, 0))
    return pl.pallas_call(
        body, name="ffn_bwd", grid=(t // tm, nf),
        in_specs=[tok, ffb, ffb,
                  pl.BlockSpec((D_MODEL, FF_TILE), lambda i, j: (0, j)),
                  pl.BlockSpec((D_MODEL, FF_TILE), lambda i, j: (0, j)),
                  pl.BlockSpec((FF_TILE, D_MODEL), lambda i, j: (j, 0)),
                  tok, vec],
        out_specs=[ffb, ffb, tok, vec],
        out_shape=[ffs, ffs, jax.ShapeDtypeStruct((t, D_MODEL), F32), jax.ShapeDtypeStruct((1, D_MODEL), F32)],
        scratch_shapes=[pltpu.VMEM((tm, D_MODEL), F32)],
        compiler_params=_cparams(("arbitrary", "arbitrary")),
    )(dx2, g, u, wg, wu, wd, x1, fw)


def _loss_grad(y, target, tm):
    t, d = y.shape

    def body(y_ref, t_ref, dy_ref, loss_ref):
        err = y_ref[...] - t_ref[...]
        dy_ref[...] = err * (1.0 / d)

        @pl.when(pl.program_id(0) == 0)
        def _():
            loss_ref[...] = jnp.zeros_like(loss_ref)

        part = jnp.sum(jnp.sum(err * err, axis=-1, keepdims=True), axis=0, keepdims=True)
        loss_ref[...] += part * (0.5 / d)

    tok = pl.BlockSpec((tm, d), lambda i: (i, 0))
    return pl.pallas_call(
        body, name="loss_grad", grid=(t // tm,),
        in_specs=[tok, tok],
        out_specs=[tok, pl.BlockSpec((1, 1), lambda i: (0, 0))],
        out_shape=[jax.ShapeDtypeStruct((t, d), F32), jax.ShapeDtypeStruct((1, 1), F32)],
        compiler_params=_cparams(("arbitrary",)),
    )(y, target)


def _tile(v, reps):
    return jnp.tile(v.reshape(1, -1), (1, reps))


def _local_step(x, target, p, wb):
    nb, s, d = x.shape
    t = nb * s
    tm = min(512, s)
    tq = min(512, s)
    xf = x.reshape(t, d)
    cosq, sinq = _rope_tables(s)
    ones512 = _block_ones(512, HEAD_DIM)
    lbs = _lower_bounds(p["hgrn_lb_logits"].reshape(DEPTH, 2 * D_HGRN)).reshape(DEPTH, 2, D_HGRN)

    saved = []
    cur = xf
    for l in range(DEPTH):
        qw = _tile(p["q_norm_w"][l], N_HEADS)
        kw = _tile(p["k_norm_w"][l], N_KV)
        gw = _tile(p["hgrn_gnorm_w"][l], 2)
        dww = jnp.pad(p["conv_dw_w"][l], ((0, 1), (0, 0)))
        pww = p["conv_pw_w"][l].astype(BF16)
        h0, proj = _rms_proj(cur, _row(p["mix_norm_w"][l]), wb["w_in"][l], tm)
        qr, kd, vd, kdt, vdt = _qkv_prep(proj, cosq, sinq, qw, kw, ones512, s, tm)
        o_attn, lse = _attn_fwd(qr, kd, vdt, nb, s, tq)
        y_hgrn, osum = _hgrn_fwd(proj, lbs[l], gw, nb, s)
        y_conv = _conv_fwd(proj, dww, _row(p["conv_dw_b"][l]), _row(p["conv_ln_w"][l]),
                           _row(p["conv_ln_b"][l]), pww, _row(p["conv_pw_b"][l]), nb, s)
        mixed, x1 = _mix_out(o_attn, y_hgrn, y_conv, cur, _row(p["attn_out_norm_w"][l]),
                             _row(p["conv_out_norm_w"][l]), wb["w_out"][l], tm)
        hf, g, u, a, x2 = _ffn_fwd(x1, _row(p["ffn_norm_w"][l]), wb["w_gate"][l], wb["w_up"][l],
                                   wb["w_down"][l], tm)
        saved.append(dict(x=cur, h0=h0, proj=proj, qr=qr, kd=kd, vd=vd, kdt=kdt, o_attn=o_attn, lse=lse,
                          osum=osum, y_conv=y_conv, mixed=mixed, x1=x1, hf=hf, g=g, u=u, a=a,
                          qw=qw, kw=kw, gw=gw, dww=dww, pww=pww))
        cur = x2

    dcur, loss = _loss_grad(cur, target.reshape(t, d), tm)

    grads = {k: [None] * DEPTH for k in WEIGHTS}
    dlb = [None] * DEPTH
    for l in reversed(range(DEPTH)):
        sv = saved[l]
        dg, du, dx1, dfw = _ffn_bwd(dcur, sv["g"], sv["u"], wb["w_gate"][l], wb["w_up"][l], wb["w_down"][l],
                                    sv["x1"], _row(p["ffn_norm_w"][l]), tm)
        grads["ffn_norm_w"][l] = dfw[0]
        grads["w_gate"][l] = _mm_tn(sv["hf"], dg, FF_TILE, "dw_gate", tm)
        grads["w_up"][l] = _mm_tn(sv["hf"], du, FF_TILE, "dw_up", tm)
        grads["w_down"][l] = _mm_tn(sv["a"], dcur, 512, "dw_down", tm)
        do_attn, dy_hgrn, dy_conv, daw, dcw = _mix_out_bwd(
            dx1, wb["w_out"][l], sv["o_attn"], sv["y_conv"], _row(p["attn_out_norm_w"][l]),
            _row(p["conv_out_norm_w"][l]), tm)
        grads["attn_out_norm_w"][l] = daw[0]
        grads["conv_out_norm_w"][l] = dcw[0]
        grads["w_out"][l] = _mm_tn(sv["mixed"], dx1, D_MODEL, "dw_out", tm)
        dq, dkd, dvd = _attn_bwd(sv["qr"], sv["kd"], sv["vd"], sv["kdt"], sv["o_attn"], sv["lse"], do_attn,
                                 nb, s, tq)
        dqkv, dqw, dkw = _qkv_bwd(sv["proj"], dq, dkd, dvd, cosq, sinq, sv["qw"], sv["kw"], ones512, s, tm)
        grads["q_norm_w"][l] = dqw.reshape(N_HEADS, HEAD_DIM).sum(0)
        grads["k_norm_w"][l] = dkw.reshape(N_KV, HEAD_DIM).sum(0)
        dhq, dzf, dzb, dhi, dhg, dgw, dlb_l = _hgrn_bwd(sv["proj"], lbs[l], sv["gw"], sv["osum"], dy_hgrn, nb, s)
        grads["hgrn_gnorm_w"][l] = dgw.reshape(nb * D_HGRN // HEAD_DIM, HEAD_DIM).sum(0)
        dlb[l] = dlb_l.sum(0)
        dab, ddww, ddwb, dlnw, dlnb, dpww, dpwb = _conv_bwd(
            sv["proj"], dy_conv, sv["dww"], _row(p["conv_dw_b"][l]), _row(p["conv_ln_w"][l]),
            _row(p["conv_ln_b"][l]), sv["pww"], nb, s)
        grads["conv_dw_w"][l] = ddww[:CONV_W]
        grads["conv_dw_b"][l] = ddwb[0]
        grads["conv_ln_w"][l] = dlnw[0]
        grads["conv_ln_b"][l] = dlnb[0]
        grads["conv_pw_w"][l] = dpww
        grads["conv_pw_b"][l] = dpwb[0]
        dproj = jnp.concatenate([dqkv, dhq, dzf, dzb, dhi, dhg, dab], axis=1)
        grads["w_in"][l] = _mm_tn(sv["h0"], dproj, D_PROJ // 2, "dw_in", tm)
        dcur, dnw = _proj_bwd(dproj, wb["w_in"][l], sv["x"], _row(p["mix_norm_w"][l]), dx1, tm)
        grads["mix_norm_w"][l] = dnw[0]

    dlog = _lower_bounds_bwd(p["hgrn_lb_logits"].reshape(DEPTH, 2 * D_HGRN),
                             jnp.stack(dlb).reshape(DEPTH, 2 * D_HGRN))
    out = {k: jnp.stack(v) for k, v in grads.items() if k != "hgrn_lb_logits"}
    out["hgrn_lb_logits"] = dlog.reshape(DEPTH, 2, D_HGRN)
    return loss, dcur.reshape(nb, s, d), out


BIG_AXIS = {"w_in": 2, "w_out": 1, "w_gate": 2, "w_up": 2, "w_down": 1}
SMALL_SHARD_AXIS = {"hgrn_lb_logits": 2, "conv_dw_w": 2, "conv_pw_w": 1}
WEIGHTS = ("mix_norm_w", "w_in", "q_norm_w", "k_norm_w", "hgrn_lb_logits", "hgrn_gnorm_w", "conv_dw_w",
           "conv_dw_b", "conv_ln_w", "conv_ln_b", "conv_pw_w", "conv_pw_b", "attn_out_norm_w",
           "conv_out_norm_w", "w_out", "ffn_norm_w", "w_gate", "w_up", "w_down")
SMALL = tuple(n for n in WEIGHTS if n not in BIG_AXIS)


def _my_index():
    return 4 * lax.axis_index("x") + 2 * lax.axis_index("y") + lax.axis_index("c")


def _exchange(src, gather, name):
    blk = src.shape[-2:]

    def body(src_ref, out_ref, send_sems, recv_sems, local_sem):
        x, y, c = lax.axis_index("x"), lax.axis_index("y"), lax.axis_index("c")
        me = 4 * x + 2 * y + c

        def rows_for(j):
            return src_ref if gather else src_ref.at[j]

        local = pltpu.make_async_copy(rows_for(me), out_ref.at[me], local_sem)
        local.start()
        sends, recvs = [], []
        for k in range(1, N_DEV):
            px = (1 - x) if (k & 4) else x
            py = (1 - y) if (k & 2) else y
            pc = (1 - c) if (k & 1) else c
            pidx = 4 * px + 2 * py + pc
            common = dict(send_sem=send_sems.at[k - 1], recv_sem=recv_sems.at[k - 1],
                          device_id=(px, py, pc), device_id_type=pl.DeviceIdType.MESH)
            sends.append(pltpu.make_async_remote_copy(src_ref=rows_for(pidx), dst_ref=out_ref.at[me], **common))
            recvs.append(pltpu.make_async_remote_copy(src_ref=rows_for(pidx), dst_ref=out_ref.at[pidx], **common))
        for cp in sends:
            cp.start()
        for cp in recvs:
            cp.wait_recv()
        for cp in sends:
            cp.wait_send()
        local.wait()

    return pl.pallas_call(
        body, name=name,
        in_specs=[pl.BlockSpec(memory_space=pl.ANY)],
        out_specs=pl.BlockSpec(memory_space=pl.ANY),
        out_shape=jax.ShapeDtypeStruct((N_DEV,) + blk, src.dtype),
        scratch_shapes=[pltpu.SemaphoreType.DMA((N_DEV - 1,)), pltpu.SemaphoreType.DMA((N_DEV - 1,)),
                        pltpu.SemaphoreType.DMA(())],
    )(src)


def _adamw_math(w, g, m, v):
    m = ADAM_B1 * m + (1.0 - ADAM_B1) * g
    v = ADAM_B2 * v + (1.0 - ADAM_B2) * (g * g)
    m_hat = m / (1.0 - ADAM_B1 ** ADAM_STEP)
    v_hat = v / (1.0 - ADAM_B2 ** ADAM_STEP)
    delta = -ADAM_LR * (m_hat / (jnp.sqrt(v_hat) + ADAM_EPS) + ADAM_WD * w)
    return delta, m, v


def _sum_adamw(parts, w, m, v, name):
    r = w.shape[0]
    tr = r
    for cand in (512, 256, 128, 64, 32, 16):
        if r % cand == 0:
            tr = cand
            break

    def body(p_ref, w_ref, m_ref, v_ref, g_ref, d_ref, mo_ref, vo_ref):
        g = p_ref[0].astype(F32)
        for i in range(1, N_DEV):
            g = g + p_ref[i].astype(F32)
        g_ref[...] = g
        d_ref[...], mo_ref[...], vo_ref[...] = _adamw_math(w_ref[...], g, m_ref[...], v_ref[...])

    row = pl.BlockSpec((tr, LANES), lambda i: (i, 0))
    shp = jax.ShapeDtypeStruct((r, LANES), F32)
    return pl.pallas_call(
        body, name=name, grid=(r // tr,),
        in_specs=[pl.BlockSpec((N_DEV, tr, LANES), lambda i: (0, i, 0)), row, row, row],
        out_specs=[row, row, row, row],
        out_shape=[shp, shp, shp, shp],
        compiler_params=_cparams(("parallel",)),
    )(parts, w, m, v)


def _sum8(parts):
    r = parts.shape[1]

    def body(p_ref, g_ref):
        g = p_ref[0]
        for i in range(1, N_DEV):
            g = g + p_ref[i]
        g_ref[...] = g

    return pl.pallas_call(body, name="sum_small_grads", out_shape=jax.ShapeDtypeStruct((r, LANES), F32))(parts)


def _adamw(w, g, m, v):
    def body(w_ref, g_ref, m_ref, v_ref, d_ref, mo_ref, vo_ref):
        d_ref[...], mo_ref[...], vo_ref[...] = _adamw_math(w_ref[...], g_ref[...], m_ref[...], v_ref[...])

    shp = jax.ShapeDtypeStruct(w.shape, F32)
    return pl.pallas_call(body, name="adamw_small", out_shape=[shp, shp, shp])(w, g, m, v)


def _pack(arrays, dtype, row_multiple):
    flat = jnp.concatenate([a.reshape(-1).astype(dtype) for a in arrays])
    n = flat.shape[0]
    unit = row_multiple * LANES
    total = -(-n // unit) * unit
    return jnp.pad(flat, (0, total - n)).reshape(total // LANES, LANES)


def _unpack(flat2d, shapes, lead=()):
    flat = flat2d.reshape(lead + (-1,))
    out, off = [], 0
    for shp in shapes:
        n = int(np.prod(shp))
        out.append(flat[..., off:off + n].reshape(lead + tuple(shp)))
        off += n
    return out


def _shard_to_rows(full, axis):
    shp = full.shape
    k = shp[axis] // N_DEV
    r = full.reshape(shp[:axis] + (N_DEV, k) + shp[axis + 1:])
    return jnp.moveaxis(r, axis, 0)


def _rows_to_full(rows, axis):
    r = jnp.moveaxis(rows, 0, axis)
    shp = r.shape
    return r.reshape(shp[:axis] + (shp[axis] * shp[axis + 1],) + shp[axis + 2:])


def kernel(x, mix_norm_w, w_in, q_norm_w, k_norm_w, hgrn_lb_logits, hgrn_gnorm_w, conv_dw_w, conv_dw_b, conv_ln_w, conv_ln_b, conv_pw_w, conv_pw_b, attn_out_norm_w, conv_out_norm_w, w_out, ffn_norm_w, w_gate, w_up, w_down, loss_target, m_mix_norm_w, m_w_in, m_q_norm_w, m_k_norm_w, m_hgrn_lb_logits, m_hgrn_gnorm_w, m_conv_dw_w, m_conv_dw_b, m_conv_ln_w, m_conv_ln_b, m_conv_pw_w, m_conv_pw_b, m_attn_out_norm_w, m_conv_out_norm_w, m_w_out, m_ffn_norm_w, m_w_gate, m_w_up, m_w_down, v_mix_norm_w, v_w_in, v_q_norm_w, v_k_norm_w, v_hgrn_lb_logits, v_hgrn_gnorm_w, v_conv_dw_w, v_conv_dw_b, v_conv_ln_w, v_conv_ln_b, v_conv_pw_w, v_conv_pw_b, v_attn_out_norm_w, v_conv_out_norm_w, v_w_out, v_ffn_norm_w, v_w_gate, v_w_up, v_w_down):
    w_loc = dict(zip(WEIGHTS, (mix_norm_w, w_in, q_norm_w, k_norm_w, hgrn_lb_logits, hgrn_gnorm_w, conv_dw_w,
                               conv_dw_b, conv_ln_w, conv_ln_b, conv_pw_w, conv_pw_b, attn_out_norm_w,
                               conv_out_norm_w, w_out, ffn_norm_w, w_gate, w_up, w_down)))
    m_loc = dict(zip(WEIGHTS, (m_mix_norm_w, m_w_in, m_q_norm_w, m_k_norm_w, m_hgrn_lb_logits, m_hgrn_gnorm_w,
                               m_conv_dw_w, m_conv_dw_b, m_conv_ln_w, m_conv_ln_b, m_conv_pw_w, m_conv_pw_b,
                               m_attn_out_norm_w, m_conv_out_norm_w, m_w_out, m_ffn_norm_w, m_w_gate, m_w_up,
                               m_w_down)))
    v_loc = dict(zip(WEIGHTS, (v_mix_norm_w, v_w_in, v_q_norm_w, v_k_norm_w, v_hgrn_lb_logits, v_hgrn_gnorm_w,
                               v_conv_dw_w, v_conv_dw_b, v_conv_ln_w, v_conv_ln_b, v_conv_pw_w, v_conv_pw_b,
                               v_attn_out_norm_w, v_conv_out_norm_w, v_w_out, v_ffn_norm_w, v_w_gate, v_w_up,
                               v_w_down)))
    me = _my_index()
    big = tuple(BIG_AXIS)
    sms = tuple(SMALL_SHARD_AXIS)

    big_shapes = [w_loc[n].shape for n in big]
    got = _exchange(_pack([w_loc[n] for n in big], BF16, 16), True, "gather_weights")
    wb = {n: _rows_to_full(a, BIG_AXIS[n] + 0)
          for n, a in zip(big, _unpack(got, big_shapes, (N_DEV,)))}
    sm_shapes = [w_loc[n].shape for n in sms]
    got = _exchange(_pack([w_loc[n] for n in sms], F32, 8), True, "gather_small_params")
    p_full = {n: w_loc[n] for n in SMALL if n not in SMALL_SHARD_AXIS}
    for n, a in zip(sms, _unpack(got, sm_shapes, (N_DEV,))):
        p_full[n] = _rows_to_full(a, SMALL_SHARD_AXIS[n])

    loss_part, grad_x, g = _local_step(x, loss_target, p_full, wb)
    loss = lax.psum(loss_part[0, 0], MESH_AXES)

    rows = [_shard_to_rows(g[n], BIG_AXIS[n]) for n in big]
    send = jnp.concatenate([r.reshape(N_DEV, -1) for r in rows], axis=1).astype(BF16)
    send = send.reshape(N_DEV, -1, LANES)
    parts = _exchange(send, False, "exchange_weight_grads")
    packed = [_pack([d[n] for n in big], F32, 16) for d in (w_loc, m_loc, v_loc)]
    res = _sum_adamw(parts, *packed, "sum_adamw_weights")
    big_out = [dict(zip(big, _unpack(r, big_shapes))) for r in res]

    small_shapes = [g[n].shape for n in SMALL]
    parts = _exchange(_pack([g[n] for n in SMALL], F32, 8), True, "gather_small_grads")
    g_small = dict(zip(SMALL, _unpack(_sum8(parts), small_shapes)))
    for n in sms:
        ax = SMALL_SHARD_AXIS[n]
        k = w_loc[n].shape[ax]
        g_small[n] = lax.dynamic_slice_in_dim(g_small[n], me * k, k, axis=ax)
    loc_shapes = [w_loc[n].shape for n in SMALL]
    packed = [_pack([d[n] for n in SMALL], F32, 8) for d in (w_loc, g_small, m_loc, v_loc)]
    res = _adamw(*packed)
    small_out = [g_small] + [dict(zip(SMALL, _unpack(r, loc_shapes))) for r in res]

    def pick(i, n):
        return big_out[i][n] if n in BIG_AXIS else small_out[i][n]

    return (loss, grad_x) + tuple(pick(i, n) for i in range(4) for n in WEIGHTS)
```

```python
import functools

import jax
import jax.numpy as jnp
import numpy as np
from jax import lax
from jax.experimental import pallas as pl
from jax.experimental.pallas import tpu as pltpu

F32 = jnp.float32
BF16 = jnp.bfloat16

D_MODEL = 1024
D_ATTN = 512
D_HGRN = 256
D_CONV = 256
HEAD_DIM = 64
N_HEADS = 8
N_KV = 2
GRID_W = 64
ROPE_THETA = 10000.0
CHUNK = 64
F_MIN = 1e-6
CONV_W = 31
CONV_PAD = 15
D_FF = 2816
D_PROJ = 2560
EPS = 1e-6
LN_EPS = 1e-5
DEPTH = 2
ADAM_LR = 0.001
ADAM_B1 = 0.9
ADAM_B2 = 0.999
ADAM_EPS = 1e-08
ADAM_WD = 0.01
ADAM_STEP = 10
N_DEV = 8
MESH_AXES = ("x", "y", "c")

COL_HQ, COL_ZFW, COL_ZBW, COL_HI, COL_HG = 6, 8, 10, 12, 14
COL_CA, COL_CB = 8, 9

LANES = 128
VMEM_LIMIT_MB = 56


def _cparams(dims=None):
    return pltpu.CompilerParams(dimension_semantics=dims, vmem_limit_bytes=VMEM_LIMIT_MB * 2 ** 20)


def _dot(a, b):
    return jnp.dot(a, b, preferred_element_type=F32)


def _dot_nt(a, b):
    return lax.dot_general(a, b, (((1,), (1,)), ((), ())), preferred_element_type=F32)


def _dot_tn(a, b):
    return lax.dot_general(a, b, (((0,), (0,)), ((), ())), preferred_element_type=F32)


def _split_bf16(x, parts):
    out = []
    r = x
    for _ in range(parts):
        p = r.astype(BF16)
        out.append(p)
        r = r - p.astype(F32)
    return out


def _dot_precise(x, m_bf16, parts=3):
    acc = None
    for p in _split_bf16(x, parts):
        t = _dot(p, m_bf16)
        acc = t if acc is None else acc + t
    return acc


def _block_ones(width, group):
    i = np.arange(width)
    return jnp.asarray((i[:, None] // group) == (i[None, :] // group), dtype=BF16)


def _sigmoid(x):
    return 1.0 / (1.0 + jnp.exp(-x))


def _rot(x):
    w = x.shape[1]
    lane = lax.broadcasted_iota(jnp.int32, x.shape, 1)
    first = (lane % 32) < 16
    return jnp.where(first, -pltpu.roll(x, w - 16, 1), pltpu.roll(x, 16, 1))


def _rope(x, cos, sin):
    return x * cos + _rot(x) * sin


def _rope_t(dy, cos, sin):
    return dy * cos - _rot(dy * sin)


def _row(v):
    return v.reshape(1, -1)


def _rms_proj(x, wn, w, tm):
    t, d = x.shape
    n = w.shape[1]

    def body(x_ref, wn_ref, w_ref, h_ref, y_ref):
        xv = x_ref[...]
        r = lax.rsqrt(jnp.mean(xv * xv, axis=-1, keepdims=True) + EPS)
        h = (xv * r * wn_ref[...]).astype(BF16)
        h_ref[...] = h
        y_ref[...] = _dot(h, w_ref[...])

    return pl.pallas_call(
        body, name="rms_proj", grid=(t // tm,),
        in_specs=[pl.BlockSpec((tm, d), lambda i: (i, 0)),
                  pl.BlockSpec((1, d), lambda i: (0, 0)),
                  pl.BlockSpec((d, n), lambda i: (0, 0))],
        out_specs=[pl.BlockSpec((tm, d), lambda i: (i, 0)),
                   pl.BlockSpec((tm, n), lambda i: (i, 0))],
        out_shape=[jax.ShapeDtypeStruct((t, d), BF16), jax.ShapeDtypeStruct((t, n), F32)],
        compiler_params=_cparams(("parallel",)),
    )(x, wn, w)


def _rms_bwd(dh, x, wn):
    r = lax.rsqrt(jnp.mean(x * x, axis=-1, keepdims=True) + EPS)
    g = dh * wn
    dx = r * (g - x * (r * r) * jnp.mean(g * x, axis=-1, keepdims=True))
    return dx, dh * x * r


def _proj_bwd(pieces, w, x, wn, dres, tm):
    t = x.shape[0]
    d = x.shape[1]
    n = w.shape[1]
    widths = [p.shape[1] for p in pieces]
    offs = [sum(widths[:i]) for i in range(len(widths))]
    assert sum(widths) == n
    npc = len(pieces)

    def body(*refs):
        p_refs = refs[:npc]
        w_ref, x_ref, wn_ref, dr_ref, dx_ref, dwn_ref = refs[npc:]
        dh = None
        for p_ref, o, wd in zip(p_refs, offs, widths):
            part = _dot_nt(p_ref[...], w_ref[:, o:o + wd])
            dh = part if dh is None else dh + part
        dx, dwn = _rms_bwd(dh, x_ref[...], wn_ref[...])
        dx_ref[...] = dr_ref[...] + dx

        @pl.when(pl.program_id(0) == 0)
        def _():
            dwn_ref[...] = jnp.zeros_like(dwn_ref)

        dwn_ref[...] += jnp.sum(dwn, axis=0, keepdims=True)

    return pl.pallas_call(
        body, name="proj_bwd", grid=(t // tm,),
        in_specs=[pl.BlockSpec((tm, wd), lambda i: (i, 0)) for wd in widths]
        + [pl.BlockSpec((d, n), lambda i: (0, 0)),
           pl.BlockSpec((tm, d), lambda i: (i, 0)),
           pl.BlockSpec((1, d), lambda i: (0, 0)),
           pl.BlockSpec((tm, d), lambda i: (i, 0))],
        out_specs=[pl.BlockSpec((tm, d), lambda i: (i, 0)),
                   pl.BlockSpec((1, d), lambda i: (0, 0))],
        out_shape=[jax.ShapeDtypeStruct((t, d), F32), jax.ShapeDtypeStruct((1, d), F32)],
        compiler_params=_cparams(("arbitrary",)),
    )(*pieces, w, x, wn, dres)


def _dw_in(h0, pieces, tm):
    t, k = h0.shape
    widths = [p.shape[1] for p in pieces]
    offs = [sum(widths[:i]) for i in range(len(widths))]
    n = sum(widths)
    npc = len(pieces)

    def body(*refs):
        h_ref = refs[0]
        p_refs = refs[1:1 + npc]
        o_ref = refs[1 + npc]

        @pl.when(pl.program_id(0) == 0)
        def _():
            o_ref[...] = jnp.zeros_like(o_ref)

        ht = h_ref[...].astype(F32).T.astype(BF16)
        for p_ref, o, wd in zip(p_refs, offs, widths):
            o_ref[:, o:o + wd] += _dot(ht, p_ref[...])

    return pl.pallas_call(
        body, name="dw_in", grid=(t // tm,),
        in_specs=[pl.BlockSpec((tm, k), lambda i: (i, 0))]
        + [pl.BlockSpec((tm, wd), lambda i: (i, 0)) for wd in widths],
        out_specs=pl.BlockSpec((k, n), lambda i: (0, 0)),
        out_shape=jax.ShapeDtypeStruct((k, n), F32),
        compiler_params=_cparams(("arbitrary",)),
    )(h0, *pieces)


def _mm_tn(a, b, tn, name, tm):
    t, k = a.shape
    n = b.shape[1]

    def body(a_ref, b_ref, o_ref):
        @pl.when(pl.program_id(1) == 0)
        def _():
            o_ref[...] = jnp.zeros_like(o_ref)

        o_ref[...] += _dot_tn(a_ref[...].astype(BF16), b_ref[...].astype(BF16))

    return pl.pallas_call(
        body, name=name, grid=(n // tn, t // tm),
        in_specs=[pl.BlockSpec((tm, k), lambda j, i: (i, 0)),
                  pl.BlockSpec((tm, tn), lambda j, i: (i, j))],
        out_specs=pl.BlockSpec((k, tn), lambda j, i: (0, j)),
        out_shape=jax.ShapeDtypeStruct((k, n), F32),
        compiler_params=_cparams(("parallel", "arbitrary")),
    )(a, b)


def _rope_tables(s):
    rows = s // GRID_W
    row_id = jnp.repeat(jnp.arange(rows, dtype=F32), GRID_W)
    col_id = jnp.tile(jnp.arange(GRID_W, dtype=F32), rows)
    half = HEAD_DIM // 2
    inv_freq = ROPE_THETA ** (-jnp.arange(0, half, 2, dtype=F32) / half)
    ang_r = row_id[:, None] * inv_freq[None, :]
    ang_c = col_id[:, None] * inv_freq[None, :]
    ang = jnp.concatenate([ang_r, ang_r, ang_c, ang_c], axis=-1)
    cos, sin = jnp.cos(ang), jnp.sin(ang)
    return jnp.tile(cos, (1, N_HEADS)), jnp.tile(sin, (1, N_HEADS))


def _head_rms(x, w, ones):
    r = lax.rsqrt(_dot_precise(x * x, ones, 2) * (1.0 / HEAD_DIM) + EPS)
    return x * r * w, r


def _dup_half(x, kv):
    lane = lax.broadcasted_iota(jnp.int32, x.shape, 1)
    sel = (lane < 64) if kv == 0 else (lane >= 64)
    return jnp.where(sel, x, pltpu.roll(x, 64, 1))


def _qkv_prep(proj, cosq, sinq, qw, kw, ones, s, tm):
    t = proj.shape[0]
    ns = s // tm

    def body(p_ref, cos_ref, sin_ref, qw_ref, kw_ref, ones_ref, q_out, kd_out, vd_out, kdt_out, vdt_out):
        cos = cos_ref[...]
        sin = sin_ref[...]
        ones_m = ones_ref[...]
        qn, _ = _head_rms(p_ref[:, 0:512], qw_ref[...], ones_m)
        q_out[...] = (_rope(qn, cos, sin) * (HEAD_DIM ** -0.5)).astype(BF16)
        kn, _ = _head_rms(p_ref[:, 512:640], kw_ref[...], ones_m[0:128, 0:128])
        kr = _rope(kn, cos[:, 0:128], sin[:, 0:128])
        v = p_ref[:, 640:768]
        for kv in range(N_KV):
            kd = _dup_half(kr, kv)
            vd = _dup_half(v, kv)
            kd_out[kv] = kd.astype(BF16)
            vd_out[kv] = vd.astype(BF16)
            kdt_out[kv] = kd.T.astype(BF16)
            vdt_out[kv] = vd.T.astype(BF16)

    return pl.pallas_call(
        body, name="qkv_prep", grid=(t // tm,),
        in_specs=[pl.BlockSpec((tm, 768), lambda i: (i, 0)),
                  pl.BlockSpec((tm, 512), lambda i: (i % ns, 0)),
                  pl.BlockSpec((tm, 512), lambda i: (i % ns, 0)),
                  pl.BlockSpec((1, 512), lambda i: (0, 0)),
                  pl.BlockSpec((1, 128), lambda i: (0, 0)),
                  pl.BlockSpec((512, 512), lambda i: (0, 0))],
        out_specs=[pl.BlockSpec((tm, 512), lambda i: (i, 0)),
                   pl.BlockSpec((N_KV, tm, 128), lambda i: (0, i, 0)),
                   pl.BlockSpec((N_KV, tm, 128), lambda i: (0, i, 0)),
                   pl.BlockSpec((N_KV, 128, tm), lambda i: (0, 0, i)),
                   pl.BlockSpec((N_KV, 128, tm), lambda i: (0, 0, i))],
        out_shape=[jax.ShapeDtypeStruct((t, 512), BF16),
                   jax.ShapeDtypeStruct((N_KV, t, 128), BF16),
                   jax.ShapeDtypeStruct((N_KV, t, 128), BF16),
                   jax.ShapeDtypeStruct((N_KV, 128, t), BF16),
                   jax.ShapeDtypeStruct((N_KV, 128, t), BF16)],
        compiler_params=_cparams(("parallel",)),
    )(proj, cosq, sinq, qw, kw, ones)


def _qkv_bwd(proj, dq, dkd, dvd, cosq, sinq, qw, kw, ones, s, tm):
    t = proj.shape[0]
    ns = s // tm

    def body(p_ref, dq_ref, dkd_ref, dvd_ref, cos_ref, sin_ref, qw_ref, kw_ref, ones_ref,
             out_ref, dqw_ref, dkw_ref):
        cos = cos_ref[...]
        sin = sin_ref[...]
        ones_m = ones_ref[...]
        ones_k = ones_m[0:128, 0:128]

        def norm_bwd(x, w, dn, om):
            r = lax.rsqrt(_dot_precise(x * x, om, 2) * (1.0 / HEAD_DIM) + EPS)
            g = dn * w
            dx = r * (g - x * (r * r) * (_dot_precise(g * x, om, 2) * (1.0 / HEAD_DIM)))
            return dx, jnp.sum(dn * x * r, axis=0, keepdims=True)

        q = p_ref[:, 0:512]
        dqn = _rope_t(dq_ref[...], cos, sin) * (HEAD_DIM ** -0.5)
        dq_raw, dqw = norm_bwd(q, qw_ref[...], dqn, ones_m)
        out_ref[:, 0:512] = dq_raw.astype(BF16)

        lane = lax.broadcasted_iota(jnp.int32, (tm, 128), 1)

        def fold(ref):
            a0 = ref[0]
            a1 = ref[1]
            f0 = a0 + pltpu.roll(a0, 64, 1)
            f1 = a1 + pltpu.roll(a1, 64, 1)
            return jnp.where(lane < 64, f0, f1)

        k = p_ref[:, 512:640]
        dkn = _rope_t(fold(dkd_ref), cos[:, 0:128], sin[:, 0:128])
        dk_raw, dkw = norm_bwd(k, kw_ref[...], dkn, ones_k)
        out_ref[:, 512:640] = dk_raw.astype(BF16)
        out_ref[:, 640:768] = fold(dvd_ref).astype(BF16)

        @pl.when(pl.program_id(0) == 0)
        def _():
            dqw_ref[...] = jnp.zeros_like(dqw_ref)
            dkw_ref[...] = jnp.zeros_like(dkw_ref)

        dqw_ref[...] += dqw
        dkw_ref[...] += dkw

    return pl.pallas_call(
        body, name="qkv_bwd", grid=(t // tm,),
        in_specs=[pl.BlockSpec((tm, 768), lambda i: (i, 0)),
                  pl.BlockSpec((tm, 512), lambda i: (i, 0)),
                  pl.BlockSpec((N_KV, tm, 128), lambda i: (0, i, 0)),
                  pl.BlockSpec((N_KV, tm, 128), lambda i: (0, i, 0)),
                  pl.BlockSpec((tm, 512), lambda i: (i % ns, 0)),
                  pl.BlockSpec((tm, 512), lambda i: (i % ns, 0)),
                  pl.BlockSpec((1, 512), lambda i: (0, 0)),
                  pl.BlockSpec((1, 128), lambda i: (0, 0)),
                  pl.BlockSpec((512, 512), lambda i: (0, 0))],
        out_specs=[pl.BlockSpec((tm, 768), lambda i: (i, 0)),
                   pl.BlockSpec((1, 512), lambda i: (0, 0)),
                   pl.BlockSpec((1, 128), lambda i: (0, 0))],
        out_shape=[jax.ShapeDtypeStruct((t, 768), BF16),
                   jax.ShapeDtypeStruct((1, 512), F32),
                   jax.ShapeDtypeStruct((1, 128), F32)],
        compiler_params=_cparams(("arbitrary",)),
    )(proj, dq, dkd, dvd, cosq, sinq, qw, kw, ones)


def _attn_fwd(q, kd, vdt, nb, s, tq):
    t = q.shape[0]
    nq = s // tq

    def body(q_ref, k_ref, vt_ref, o_ref, lse_ref):
        qv = q_ref[...].astype(F32)
        lane = lax.broadcasted_iota(jnp.int32, qv.shape, 1)
        k = k_ref[0]
        vt = vt_ref[0]
        outs = []
        for half in range(2):
            qh = jnp.where((lane < 64) if half == 0 else (lane >= 64), qv, 0.0).astype(BF16)
            st = _dot_nt(k, qh)
            m = jnp.max(st, axis=0, keepdims=True)
            p = jnp.exp(st - m)
            l = jnp.sum(p, axis=0, keepdims=True)
            ot = _dot(vt, p.astype(BF16)) / l
            lse_ref[0, half] = m + jnp.log(l)
            outs.append(ot)
        row = lax.broadcasted_iota(jnp.int32, outs[0].shape, 0)
        o_ref[...] = jnp.where(row < 64, outs[0], outs[1]).T

    return pl.pallas_call(
        body, name="attn_fwd", grid=(nb, N_HEADS // 2, nq),
        in_specs=[pl.BlockSpec((tq, 128), lambda b, p, i: (b * nq + i, p)),
                  pl.BlockSpec((1, s, 128), lambda b, p, i: (p // 2, b, 0)),
                  pl.BlockSpec((1, 128, s), lambda b, p, i: (p // 2, 0, b))],
        out_specs=[pl.BlockSpec((tq, 128), lambda b, p, i: (b * nq + i, p)),
                   pl.BlockSpec((1, 2, 1, tq), lambda b, p, i: (b, p, 0, i))],
        out_shape=[jax.ShapeDtypeStruct((t, D_ATTN), F32),
                   jax.ShapeDtypeStruct((nb, N_HEADS, 1, s), F32)],
        compiler_params=_cparams(("parallel", "parallel", "parallel")),
    )(q, kd, vdt)


def _attn_bwd(q, kd, vd, kdt, o, lse, do, nb, s, tq):
    t = q.shape[0]
    nq = s // tq
    ones8 = jnp.ones((8, 128), BF16)

    def body(q_ref, k_ref, v_ref, kt_ref, o_ref, lse_ref, do_ref, ones_ref, dq_ref, dk_ref, dv_ref):
        @pl.when((pl.program_id(2) == 0) & (pl.program_id(3) == 0))
        def _():
            dk_ref[...] = jnp.zeros_like(dk_ref)
            dv_ref[...] = jnp.zeros_like(dv_ref)

        qv = q_ref[...].astype(F32)
        dov = do_ref[...]
        ov = o_ref[...]
        lane = lax.broadcasted_iota(jnp.int32, qv.shape, 1)
        k = k_ref[0]
        v = v_ref[0]
        kt = kt_ref[0]
        dqs = []
        dk_acc = None
        dv_acc = None
        for half in range(2):
            sel = (lane < 64) if half == 0 else (lane >= 64)
            qh = jnp.where(sel, qv, 0.0).astype(BF16)
            doh = jnp.where(sel, dov, 0.0)
            dob = doh.astype(BF16)
            delta = None
            for part in _split_bf16(doh * ov, 3):
                d8 = _dot_nt(ones_ref[...], part)
                delta = d8 if delta is None else delta + d8
            delta = delta[0:1, :]
            st = _dot_nt(k, qh)
            pt = jnp.exp(st - lse_ref[0, half])
            dpt = _dot_nt(v, dob)
            dst = (pt * (dpt - delta)).astype(BF16)
            dkh = _dot(dst, qh)
            dvh = _dot(pt.astype(BF16), dob)
            dk_acc = dkh if dk_acc is None else dk_acc + dkh
            dv_acc = dvh if dv_acc is None else dv_acc + dvh
            dqs.append(_dot(kt, dst))
        dk_ref[0] += dk_acc
        dv_ref[0] += dv_acc
        row = lax.broadcasted_iota(jnp.int32, dqs[0].shape, 0)
        dq_ref[...] = jnp.where(row < 64, dqs[0], dqs[1]).T

    qmap = lambda b, g, p, i: (b * nq + i, g * 2 + p)
    kvmap = lambda b, g, p, i: (g, b, 0)
    return pl.pallas_call(
        body, name="attn_bwd", grid=(nb, N_KV, 2, nq),
        in_specs=[pl.BlockSpec((tq, 128), qmap),
                  pl.BlockSpec((1, s, 128), kvmap),
                  pl.BlockSpec((1, s, 128), kvmap),
                  pl.BlockSpec((1, 128, s), lambda b, g, p, i: (g, 0, b)),
                  pl.BlockSpec((tq, 128), qmap),
                  pl.BlockSpec((1, 2, 1, tq), lambda b, g, p, i: (b, g * 2 + p, 0, i)),
                  pl.BlockSpec((tq, 128), qmap),
                  pl.BlockSpec((8, 128), lambda b, g, p, i: (0, 0))],
        out_specs=[pl.BlockSpec((tq, 128), qmap),
                   pl.BlockSpec((1, s, 128), kvmap),
                   pl.BlockSpec((1, s, 128), kvmap)],
        out_shape=[jax.ShapeDtypeStruct((t, D_ATTN), F32),
                   jax.ShapeDtypeStruct((N_KV, t, 128), F32),
                   jax.ShapeDtypeStruct((N_KV, t, 128), F32)],
        compiler_params=_cparams(("parallel", "parallel", "arbitrary", "arbitrary")),
    )(q, kd, vd, kdt, o, lse, do, ones8)


def _tri_mats():
    i = np.arange(CHUNK)
    lower = jnp.asarray(i[:, None] >= i[None, :], dtype=BF16)
    upper = jnp.asarray(i[:, None] <= i[None, :], dtype=BF16)
    return jnp.stack([lower, upper])


def _running_sum(tri, x):
    acc = None
    for part in _split_bf16(x, 3):
        t = _dot(tri, part)
        acc = t if acc is None else acc + t
    return acc


def _gates(z, lb):
    sig = _sigmoid(z)
    f = lb + (1.0 - lb) * sig
    logf = jnp.log(jnp.maximum(f, F_MIN))
    k = (1.0 - lb) * (1.0 - sig)
    return sig, f, logf, k


def _row_group(jg, anti):
    if anti:
        return 0, 8 * jg + 8
    return 8 * jg, CHUNK


def _chunk_fwd(q, k, v, b, st, bones, bmask, anti):
    b_last = b[0:1] if anti else b[CHUNK - 1:CHUNK]
    qb = q * jnp.exp(b)
    o_inter = _dot_nt(qb.astype(BF16), st.astype(BF16))
    kt = k * jnp.exp(b_last - b)
    st_new = st * jnp.exp(b_last) + _dot_tn(v.astype(BF16), kt.astype(BF16)) * bmask
    tt = lax.broadcasted_iota(jnp.int32, (CHUNK, LANES), 0)
    blocks = []
    for jg in range(8):
        r0, r1 = _row_group(jg, anti)
        qr = q[r0:r1]
        br = b[r0:r1]
        tr = tt[r0:r1]
        acc = None
        for i in range(8):
            sc = 8 * jg + i
            mask = (tr <= sc) if anti else (tr >= sc)
            e = jnp.where(mask, jnp.exp(jnp.minimum(br - b[sc:sc + 1], 0.0)), 0.0)
            w = qr * e * k[sc:sc + 1]
            pb = _dot(w.astype(BF16), bones)
            term = pb * v[sc:sc + 1]
            acc = term if acc is None else acc + term
        blocks.append((r0, r1, acc))
    o = o_inter
    pieces = []
    for g in range(8):
        tot = o[8 * g:8 * g + 8]
        for (r0, r1, acc) in blocks:
            if r0 <= 8 * g and 8 * g + 8 <= r1:
                tot = tot + acc[8 * g - r0:8 * g - r0 + 8]
        pieces.append(tot)
    return jnp.concatenate(pieces, axis=0), st_new


def _hgrn_fwd(proj, lb, gw, nb, s):
    t = proj.shape[0]
    nc = s // CHUNK
    tri = _tri_mats()
    bones = _block_ones(LANES, HEAD_DIM)

    def body(q_ref, zf_ref, zb_ref, v_ref, g_ref, lb_ref, gw_ref, tri_ref, bones_ref, y_ref, os_ref, st_ref):
        bones_m = bones_ref[...]
        bmask = bones_m.astype(F32)
        for anti in (False, True):
            z_ref = zb_ref if anti else zf_ref
            lbv = lb_ref[1:2] if anti else lb_ref[0:1]
            trim = tri_ref[1] if anti else tri_ref[0]
            st_ref[...] = jnp.zeros_like(st_ref)

            def step(n, carry, anti=anti, z_ref=z_ref, lbv=lbv, trim=trim):
                cn = (nc - 1 - n) if anti else n
                rows = pl.ds(pl.multiple_of(cn * CHUNK, CHUNK), CHUNK)
                q = q_ref[rows, :]
                v = v_ref[rows, :]
                _, _, logf, k = _gates(z_ref[rows, :], lbv)
                b = _running_sum(trim, logf)
                o, st_new = _chunk_fwd(q, k, v, b, st_ref[...], bones_m, bmask, anti)
                st_ref[...] = st_new
                if anti:
                    osum = os_ref[rows, :] + o
                    os_ref[rows, :] = osum
                    r = lax.rsqrt(_dot_precise(osum * osum, bones_m, 2) * (1.0 / HEAD_DIM) + EPS)
                    hg = g_ref[rows, :]
                    y_ref[rows, :] = osum * r * gw_ref[...] * (hg * _sigmoid(hg))
                else:
                    os_ref[rows, :] = o
                return carry

            lax.fori_loop(0, nc, step, 0)

    def col(c):
        return pl.BlockSpec((s, LANES), lambda b, p, c=c: (b, c + p))

    return pl.pallas_call(
        body, name="hgrn_fwd", grid=(nb, 2),
        in_specs=[col(COL_HQ), col(COL_ZFW), col(COL_ZBW), col(COL_HI), col(COL_HG),
                  pl.BlockSpec((2, LANES), lambda b, p: (0, p)),
                  pl.BlockSpec((1, LANES), lambda b, p: (0, 0)),
                  pl.BlockSpec((2, CHUNK, CHUNK), lambda b, p: (0, 0, 0)),
                  pl.BlockSpec((LANES, LANES), lambda b, p: (0, 0))],
        out_specs=[pl.BlockSpec((s, LANES), lambda b, p: (b, p)),
                   pl.BlockSpec((s, LANES), lambda b, p: (b, p))],
        out_shape=[jax.ShapeDtypeStruct((t, D_HGRN), F32), jax.ShapeDtypeStruct((t, D_HGRN), F32)],
        scratch_shapes=[pltpu.VMEM((LANES, LANES), F32)],
        compiler_params=_cparams(("parallel", "parallel")),
    )(proj, proj, proj, proj, proj, lb, gw, tri, bones)


def _chunk_bwd(q, k, v, b, do, st_in, rt, bones, bmask, anti):
    b_last = b[0:1] if anti else b[CHUNK - 1:CHUNK]
    eb = jnp.exp(b)
    ebl = jnp.exp(b_last - b)
    dob = do.astype(BF16)
    rtb = rt.astype(BF16)
    dq_inter = eb * _dot(dob, st_in.astype(BF16))
    dk_inter = ebl * _dot(v.astype(BF16), rtb)
    dv_inter = _dot_nt((k * ebl).astype(BF16), rtb)
    rt_new = rt * jnp.exp(b_last) + _dot_tn(dob, (q * eb).astype(BF16)) * bmask
    tt = lax.broadcasted_iota(jnp.int32, (CHUNK, LANES), 0)
    r8 = lax.broadcasted_iota(jnp.int32, (8, LANES), 0)
    blocks = []
    dk_pieces = []
    dv_pieces = []
    for jg in range(8):
        r0, r1 = _row_group(jg, anti)
        qr = q[r0:r1]
        br = b[r0:r1]
        tr = tt[r0:r1]
        dor = do[r0:r1]
        acc = None
        dk_blk = jnp.zeros((8, LANES), F32)
        dv_blk = jnp.zeros((8, LANES), F32)
        for i in range(8):
            sc = 8 * jg + i
            mask = (tr <= sc) if anti else (tr >= sc)
            e = jnp.where(mask, jnp.exp(jnp.minimum(br - b[sc:sc + 1], 0.0)), 0.0)
            qe = qr * e
            ke = e * k[sc:sc + 1]
            pb = _dot((qe * k[sc:sc + 1]).astype(BF16), bones)
            dpb = _dot((dor * v[sc:sc + 1]).astype(BF16), bones)
            term = dpb * ke
            acc = term if acc is None else acc + term
            dk_s = jnp.sum(dpb * qe, axis=0, keepdims=True)
            dv_s = jnp.sum(pb * dor, axis=0, keepdims=True)
            dk_blk = jnp.where(r8 == i, dk_s, dk_blk)
            dv_blk = jnp.where(r8 == i, dv_s, dv_blk)
        blocks.append((r0, r1, acc))
        dk_pieces.append(dk_blk)
        dv_pieces.append(dv_blk)
    dq_pieces = []
    for g in range(8):
        tot = dq_inter[8 * g:8 * g + 8]
        for (r0, r1, acc) in blocks:
            if r0 <= 8 * g and 8 * g + 8 <= r1:
                tot = tot + acc[8 * g - r0:8 * g - r0 + 8]
        dq_pieces.append(tot)
    dq = jnp.concatenate(dq_pieces, axis=0)
    dk = dk_inter + jnp.concatenate(dk_pieces, axis=0)
    dv = dv_inter + jnp.concatenate(dv_pieces, axis=0)
    db_last = (jnp.sum(k * dk_inter, axis=0, keepdims=True)
               + jnp.exp(b_last) * jnp.sum(rt * st_in, axis=0, keepdims=True))
    return dq, dk, dv, rt_new, db_last


def _hgrn_bwd(proj, lb, gw, osum, dy, nb, s):
    t = proj.shape[0]
    nc = s // CHUNK
    tri = _tri_mats()
    bones = _block_ones(LANES, HEAD_DIM)

    def body(q_ref, zf_ref, zb_ref, v_ref, g_ref, lb_ref, gw_ref, os_ref, dy_ref, tri_ref, bones_ref,
             dq_ref, dzf_ref, dzb_ref, dv_ref, dg_ref, dgw_ref, dlb_ref,
             do_sc, dq_sc, dv_sc, st_sc, st_cur, rt_cur):
        bones_m = bones_ref[...]
        bmask = bones_m.astype(F32)
        gwv = gw_ref[...]

        def head(n, acc):
            rows = pl.ds(pl.multiple_of(n * CHUNK, CHUNK), CHUNK)
            o = os_ref[rows, :]
            hg = g_ref[rows, :]
            dyv = dy_ref[rows, :]
            sg = _sigmoid(hg)
            r = lax.rsqrt(_dot_precise(o * o, bones_m, 2) * (1.0 / HEAD_DIM) + EPS)
            nrm = o * r * gwv
            dn = dyv * (hg * sg)
            dg_ref[rows, :] = (dyv * nrm * (sg * (1.0 + hg * (1.0 - sg)))).astype(BF16)
            g = dn * gwv
            mean_go = _dot_precise(g * o, bones_m, 2) * (1.0 / HEAD_DIM)
            do_sc[rows, :] = r * (g - o * (r * r) * mean_go)
            return acc + jnp.sum(dn * o * r, axis=0, keepdims=True)

        dgw_ref[0] = lax.fori_loop(0, nc, head, jnp.zeros((1, LANES), F32))
        dq_sc[...] = jnp.zeros_like(dq_sc)
        dv_sc[...] = jnp.zeros_like(dv_sc)

        for anti in (False, True):
            z_ref = zb_ref if anti else zf_ref
            dz_ref = dzb_ref if anti else dzf_ref
            lbv = lb_ref[1:2] if anti else lb_ref[0:1]
            trim = tri_ref[1] if anti else tri_ref[0]
            trim_r = tri_ref[0] if anti else tri_ref[1]

            def load(cn, z_ref=z_ref, lbv=lbv, trim=trim):
                rows = pl.ds(pl.multiple_of(cn * CHUNK, CHUNK), CHUNK)
                q = q_ref[rows, :]
                v = v_ref[rows, :]
                sig, f, logf, k = _gates(z_ref[rows, :], lbv)
                b = _running_sum(trim, logf)
                return rows, q, v, sig, f, k, b

            st_cur[...] = jnp.zeros_like(st_cur)

            def sweep(n, carry, anti=anti, load=load):
                cn = (nc - 1 - n) if anti else n
                _, q, v, _, _, k, b = load(cn)
                st_sc[cn] = st_cur[...]
                b_last = b[0:1] if anti else b[CHUNK - 1:CHUNK]
                kt = k * jnp.exp(b_last - b)
                st_cur[...] = st_cur[...] * jnp.exp(b_last) + _dot_tn(v.astype(BF16), kt.astype(BF16)) * bmask
                return carry

            lax.fori_loop(0, nc, sweep, 0)

            rt_cur[...] = jnp.zeros_like(rt_cur)

            def back(n, dlb, anti=anti, load=load, lbv=lbv, trim_r=trim_r, dz_ref=dz_ref):
                cn = n if anti else (nc - 1 - n)
                rows, q, v, sig, f, k, b = load(cn)
                do = do_sc[rows, :]
                dq, dk, dv, rt_new, db_last = _chunk_bwd(q, k, v, b, do, st_sc[cn], rt_cur[...], bones_m,
                                                         bmask, anti)
                rt_cur[...] = rt_new
                dq_sc[rows, :] += dq
                dv_sc[rows, :] += dv
                dlogf = _running_sum(trim_r, q * dq - k * dk) + db_last
                dfl = jnp.where(f > F_MIN, dlogf / f, 0.0)
                dz_ref[rows, :] = ((dfl - dk) * (1.0 - lbv) * sig * (1.0 - sig)).astype(BF16)
                return dlb + jnp.sum((dfl - dk) * (1.0 - sig), axis=0, keepdims=True)

            dlb = lax.fori_loop(0, nc, back, jnp.zeros((1, LANES), F32))
            side = 1 if anti else 0
            dlb_ref[0, side:side + 1, :] = dlb

        dq_ref[...] = dq_sc[...].astype(BF16)
        dv_ref[...] = dv_sc[...].astype(BF16)

    def col(c):
        return pl.BlockSpec((s, LANES), lambda b, p, c=c: (b, c + p))

    sl = pl.BlockSpec((s, LANES), lambda b, p: (b, p))
    out_t = jax.ShapeDtypeStruct((t, D_HGRN), BF16)
    return pl.pallas_call(
        body, name="hgrn_bwd", grid=(nb, 2),
        in_specs=[col(COL_HQ), col(COL_ZFW), col(COL_ZBW), col(COL_HI), col(COL_HG),
                  pl.BlockSpec((2, LANES), lambda b, p: (0, p)),
                  pl.BlockSpec((1, LANES), lambda b, p: (0, 0)),
                  sl, sl,
                  pl.BlockSpec((2, CHUNK, CHUNK), lambda b, p: (0, 0, 0)),
                  pl.BlockSpec((LANES, LANES), lambda b, p: (0, 0))],
        out_specs=[sl, sl, sl, sl, sl,
                   pl.BlockSpec((1, 1, LANES), lambda b, p: (b, 0, p)),
                   pl.BlockSpec((1, 2, LANES), lambda b, p: (b, 0, p))],
        out_shape=[out_t, out_t, out_t, out_t, out_t,
                   jax.ShapeDtypeStruct((nb, 1, D_HGRN), F32),
                   jax.ShapeDtypeStruct((nb, 2, D_HGRN), F32)],
        scratch_shapes=[pltpu.VMEM((s, LANES), F32), pltpu.VMEM((s, LANES), F32), pltpu.VMEM((s, LANES), F32),
                        pltpu.VMEM((nc, LANES, LANES), F32), pltpu.VMEM((LANES, LANES), F32),
                        pltpu.VMEM((LANES, LANES), F32)],
        compiler_params=_cparams(("parallel", "parallel")),
    )(proj, proj, proj, proj, proj, lb, gw, osum, dy, tri, bones)


def _lower_bounds(logits):
    def body(lg_ref, lb_ref):
        rows = [lg_ref[l:l + 1, :] for l in range(DEPTH)]
        m = functools.reduce(jnp.maximum, rows)
        ex = [jnp.exp(r - m) for r in rows]
        den = functools.reduce(jnp.add, ex)
        run = jnp.zeros_like(m)
        for l in range(DEPTH):
            if l > 0:
                run = run + ex[l] / den
            lb_ref[l:l + 1, :] = run

    return pl.pallas_call(body, name="lower_bounds", out_shape=jax.ShapeDtypeStruct(logits.shape, F32))(logits)


def _lower_bounds_bwd(logits, dlb):
    def body(lg_ref, dlb_ref, dlg_ref):
        rows = [lg_ref[l:l + 1, :] for l in range(DEPTH)]
        m = functools.reduce(jnp.maximum, rows)
        ex = [jnp.exp(r - m) for r in rows]
        den = functools.reduce(jnp.add, ex)
        sm = [e / den for e in ex]
        dsm = [jnp.zeros_like(m) for _ in range(DEPTH)]
        for i in range(1, DEPTH):
            for l in range(i, DEPTH):
                dsm[i] = dsm[i] + dlb_ref[l:l + 1, :]
        dot = functools.reduce(jnp.add, [sm[i] * dsm[i] for i in range(DEPTH)])
        for i in range(DEPTH):
            dlg_ref[i:i + 1, :] = sm[i] * (dsm[i] - dot)

    return pl.pallas_call(body, name="lower_bounds_bwd", out_shape=jax.ShapeDtypeStruct(logits.shape, F32))(logits, dlb)


CONV_ROWS = 128


def _conv_core(a, bg, dww, dwb, lnw, lnb, upad_ref, s):
    sb = _sigmoid(bg)
    u = a * sb
    upad_ref[0:16, :] = jnp.zeros((16, D_CONV), F32)
    upad_ref[16:16 + s, :] = u
    upad_ref[16 + s:32 + s, :] = jnp.zeros((16, D_CONV), F32)
    rows = min(s, CONV_ROWS)
    pieces = []
    for r0 in range(0, s, rows):
        acc = None
        for j in range(CONV_W):
            term = upad_ref[r0 + 1 + j:r0 + 1 + j + rows, :] * dww[j:j + 1, :]
            acc = term if acc is None else acc + term
        pieces.append(acc)
    c = jnp.concatenate(pieces, axis=0) + dwb
    mu = jnp.mean(c, axis=-1, keepdims=True)
    xc = c - mu
    rstd = lax.rsqrt(jnp.mean(xc * xc, axis=-1, keepdims=True) + LN_EPS)
    nh = xc * rstd
    l = nh * lnw + lnb
    sl = _sigmoid(l)
    return sb, nh, rstd, l, sl


def _conv_fwd(proj, dww, dwb, lnw, lnb, pww, pwb, nb, s):
    t = proj.shape[0]
    assert s % min(s, CONV_ROWS) == 0

    def body(a_ref, b_ref, dww_ref, dwb_ref, lnw_ref, lnb_ref, pww_ref, pwb_ref, y_ref, upad_ref):
        _, _, _, l, sl = _conv_core(a_ref[...], b_ref[...], dww_ref[...], dwb_ref[...], lnw_ref[...],
                                    lnb_ref[...], upad_ref, s)
        y_ref[...] = _dot((l * sl).astype(BF16), pww_ref[...]) + pwb_ref[...]

    vec = pl.BlockSpec((1, D_CONV), lambda b: (0, 0))
    return pl.pallas_call(
        body, name="conv_fwd", grid=(nb,),
        in_specs=[pl.BlockSpec((s, D_CONV), lambda b: (b, COL_CA)),
                  pl.BlockSpec((s, D_CONV), lambda b: (b, COL_CB)),
                  pl.BlockSpec((32, D_CONV), lambda b: (0, 0)), vec, vec, vec,
                  pl.BlockSpec((D_CONV, D_CONV), lambda b: (0, 0)), vec],
        out_specs=pl.BlockSpec((s, D_CONV), lambda b: (b, 0)),
        out_shape=jax.ShapeDtypeStruct((t, D_CONV), F32),
        scratch_shapes=[pltpu.VMEM((s + 32, D_CONV), F32)],
        compiler_params=_cparams(("parallel",)),
    )(proj, proj, dww, dwb, lnw, lnb, pww, pwb)


def _conv_bwd(proj, dy, dww, dwb, lnw, lnb, pww, nb, s):
    t = proj.shape[0]

    def body(a_ref, b_ref, dy_ref, dww_ref, dwb_ref, lnw_ref, lnb_ref, pww_ref,
             dab_ref, ddww_ref, ddwb_ref, dlnw_ref, dlnb_ref, dpww_ref, dpwb_ref, upad_ref, dcpad_ref):
        a = a_ref[...]
        dww = dww_ref[...]
        sb, nh, rstd, l, sl = _conv_core(a, b_ref[...], dww, dwb_ref[...], lnw_ref[...], lnb_ref[...],
                                         upad_ref, s)
        dyv = dy_ref[...]
        dyb = dyv.astype(BF16)
        ds = _dot_nt(dyb, pww_ref[...])
        dl = ds * (sl * (1.0 + l * (1.0 - sl)))
        dn = dl * lnw_ref[...]
        dc = rstd * (dn - jnp.mean(dn, axis=-1, keepdims=True)
                     - nh * jnp.mean(dn * nh, axis=-1, keepdims=True))

        @pl.when(pl.program_id(0) == 0)
        def _():
            for r in (ddww_ref, ddwb_ref, dlnw_ref, dlnb_ref, dpww_ref, dpwb_ref):
                r[...] = jnp.zeros_like(r)

        dpww_ref[...] += _dot_tn((l * sl).astype(BF16), dyb)
        dpwb_ref[...] += jnp.sum(dyv, axis=0, keepdims=True)
        dlnw_ref[...] += jnp.sum(dl * nh, axis=0, keepdims=True)
        dlnb_ref[...] += jnp.sum(dl, axis=0, keepdims=True)
        ddwb_ref[...] += jnp.sum(dc, axis=0, keepdims=True)

        dcpad_ref[0:16, :] = jnp.zeros((16, D_CONV), F32)
        dcpad_ref[16:16 + s, :] = dc
        dcpad_ref[16 + s:32 + s, :] = jnp.zeros((16, D_CONV), F32)
        rows = min(s, CONV_ROWS)
        r8 = lax.broadcasted_iota(jnp.int32, (32, D_CONV), 0)
        ddww = jnp.zeros((32, D_CONV), F32)
        pieces = []
        for r0 in range(0, s, rows):
            acc = None
            dcr = dcpad_ref[16 + r0:16 + r0 + rows, :]
            for j in range(CONV_W):
                term = dcpad_ref[r0 + 31 - j:r0 + 31 - j + rows, :] * dww[j:j + 1, :]
                acc = term if acc is None else acc + term
                wj = jnp.sum(dcr * upad_ref[r0 + 1 + j:r0 + 1 + j + rows, :], axis=0, keepdims=True)
                ddww = ddww + jnp.where(r8 == j, wj, 0.0)
            pieces.append(acc)
        du = jnp.concatenate(pieces, axis=0)
        ddww_ref[...] += ddww
        dab_ref[:, 0:D_CONV] = (du * sb).astype(BF16)
        dab_ref[:, D_CONV:2 * D_CONV] = (du * a * sb * (1.0 - sb)).astype(BF16)

    vec = pl.BlockSpec((1, D_CONV), lambda b: (0, 0))
    mat = pl.BlockSpec((D_CONV, D_CONV), lambda b: (0, 0))
    w32 = pl.BlockSpec((32, D_CONV), lambda b: (0, 0))
    vshape = jax.ShapeDtypeStruct((1, D_CONV), F32)
    return pl.pallas_call(
        body, name="conv_bwd", grid=(nb,),
        in_specs=[pl.BlockSpec((s, D_CONV), lambda b: (b, COL_CA)),
                  pl.BlockSpec((s, D_CONV), lambda b: (b, COL_CB)),
                  pl.BlockSpec((s, D_CONV), lambda b: (b, 0)),
                  w32, vec, vec, vec, mat],
        out_specs=[pl.BlockSpec((s, 2 * D_CONV), lambda b: (b, 0)), w32, vec, vec, vec, mat, vec],
        out_shape=[jax.ShapeDtypeStruct((t, 2 * D_CONV), BF16),
                   jax.ShapeDtypeStruct((32, D_CONV), F32), vshape, vshape, vshape,
                   jax.ShapeDtypeStruct((D_CONV, D_CONV), F32), vshape],
        scratch_shapes=[pltpu.VMEM((s + 32, D_CONV), F32), pltpu.VMEM((s + 32, D_CONV), F32)],
        compiler_params=_cparams(("arbitrary",)),
    )(proj, proj, dy, dww, dwb, lnw, lnb, pww)


def _mix_out(o_attn, y_hgrn, y_conv, x, aw, cw, w_out, tm):
    t = x.shape[0]

    def body(o_ref, h_ref, c_ref, x_ref, aw_ref, cw_ref, w_ref, mixed_ref, x1_ref):
        o = o_ref[...]
        a = o * lax.rsqrt(jnp.mean(o * o, axis=-1, keepdims=True) + EPS) * aw_ref[...]
        yc = c_ref[...]
        c = yc * lax.rsqrt(jnp.mean(yc * yc, axis=-1, keepdims=True) + EPS) * cw_ref[...]
        ab, hb, cb = a.astype(BF16), h_ref[...].astype(BF16), c.astype(BF16)
        mixed_ref[:, 0:512] = ab
        mixed_ref[:, 512:768] = hb
        mixed_ref[:, 768:1024] = cb
        x1_ref[...] = (x_ref[...] + _dot(ab, w_ref[0:512, :]) + _dot(hb, w_ref[512:768, :])
                       + _dot(cb, w_ref[768:1024, :]))

    def tok(w):
        return pl.BlockSpec((tm, w), lambda i: (i, 0))

    return pl.pallas_call(
        body, name="mix_out", grid=(t // tm,),
        in_specs=[tok(512), tok(256), tok(256), tok(D_MODEL),
                  pl.BlockSpec((1, 512), lambda i: (0, 0)), pl.BlockSpec((1, 256), lambda i: (0, 0)),
                  pl.BlockSpec((D_MODEL, D_MODEL), lambda i: (0, 0))],
        out_specs=[tok(D_MODEL), tok(D_MODEL)],
        out_shape=[jax.ShapeDtypeStruct((t, D_MODEL), BF16), jax.ShapeDtypeStruct((t, D_MODEL), F32)],
        compiler_params=_cparams(("parallel",)),
    )(o_attn, y_hgrn, y_conv, x, aw, cw, w_out)


def _mix_out_bwd(dx1, w_out, o_attn, y_conv, aw, cw, tm):
    t = dx1.shape[0]

    def body(dx_ref, w_ref, o_ref, c_ref, aw_ref, cw_ref, do_ref, dh_ref, dc_ref, daw_ref, dcw_ref):
        dm = _dot_nt(dx_ref[...].astype(BF16), w_ref[...])
        do, daw = _rms_bwd(dm[:, 0:512], o_ref[...], aw_ref[...])
        dc, dcw = _rms_bwd(dm[:, 768:1024], c_ref[...], cw_ref[...])
        do_ref[...] = do
        dh_ref[...] = dm[:, 512:768]
        dc_ref[...] = dc

        @pl.when(pl.program_id(0) == 0)
        def _():
            daw_ref[...] = jnp.zeros_like(daw_ref)
            dcw_ref[...] = jnp.zeros_like(dcw_ref)

        daw_ref[...] += jnp.sum(daw, axis=0, keepdims=True)
        dcw_ref[...] += jnp.sum(dcw, axis=0, keepdims=True)

    def tok(w):
        return pl.BlockSpec((tm, w), lambda i: (i, 0))

    v512 = pl.BlockSpec((1, 512), lambda i: (0, 0))
    v256 = pl.BlockSpec((1, 256), lambda i: (0, 0))
    return pl.pallas_call(
        body, name="mix_out_bwd", grid=(t // tm,),
        in_specs=[tok(D_MODEL), pl.BlockSpec((D_MODEL, D_MODEL), lambda i: (0, 0)), tok(512), tok(256),
                  v512, v256],
        out_specs=[tok(512), tok(256), tok(256), v512, v256],
        out_shape=[jax.ShapeDtypeStruct((t, 512), F32), jax.ShapeDtypeStruct((t, 256), F32),
                   jax.ShapeDtypeStruct((t, 256), F32), jax.ShapeDtypeStruct((1, 512), F32),
                   jax.ShapeDtypeStruct((1, 256), F32)],
        compiler_params=_cparams(("arbitrary",)),
    )(dx1, w_out, o_attn, y_conv, aw, cw)


FF_TILE = 1408


def _ffn_fwd(x1, fw, wg, wu, wd, tm):
    t = x1.shape[0]
    nf = D_FF // FF_TILE

    def body(x_ref, fw_ref, wg_ref, wu_ref, wd_ref, h_ref, g_ref, u_ref, a_ref, x2_ref, acc_ref):
        j = pl.program_id(1)

        @pl.when(j == 0)
        def _():
            xv = x_ref[...]
            r = lax.rsqrt(jnp.mean(xv * xv, axis=-1, keepdims=True) + EPS)
            h_ref[...] = (xv * r * fw_ref[...]).astype(BF16)
            acc_ref[...] = xv

        h = h_ref[...]
        g = _dot(h, wg_ref[...])
        u = _dot(h, wu_ref[...])
        a = (g * _sigmoid(g) * u).astype(BF16)
        g_ref[...] = g.astype(BF16)
        u_ref[...] = u.astype(BF16)
        a_ref[...] = a
        acc_ref[...] += _dot(a, wd_ref[...])

        @pl.when(j == nf - 1)
        def _():
            x2_ref[...] = acc_ref[...]

    tok = pl.BlockSpec((tm, D_MODEL), lambda i, j: (i, 0))
    ffb = pl.BlockSpec((tm, FF_TILE), lambda i, j: (i, j))
    ffs = jax.ShapeDtypeStruct((t, D_FF), BF16)
    return pl.pallas_call(
        body, name="ffn_fwd", grid=(t // tm, nf),
        in_specs=[tok, pl.BlockSpec((1, D_MODEL), lambda i, j: (0, 0)),
                  pl.BlockSpec((D_MODEL, FF_TILE), lambda i, j: (0, j)),
                  pl.BlockSpec((D_MODEL, FF_TILE), lambda i, j: (0, j)),
                  pl.BlockSpec((FF_TILE, D_MODEL), lambda i, j: (j, 0))],
        out_specs=[tok, ffb, ffb, ffb, tok],
        out_shape=[jax.ShapeDtypeStruct((t, D_MODEL), BF16), ffs, ffs, ffs,
                   jax.ShapeDtypeStruct((t, D_MODEL), F32)],
        scratch_shapes=[pltpu.VMEM((tm, D_MODEL), F32)],
        compiler_params=_cparams(("parallel", "arbitrary")),
    )(x1, fw, wg, wu, wd)


def _ffn_bwd(dx2, g, u, wg, wu, wd, x1, fw, tm):
    t = dx2.shape[0]
    nf = D_FF // FF_TILE

    def body(dx_ref, g_ref, u_ref, wg_ref, wu_ref, wd_ref, x_ref, fw_ref,
             dg_ref, du_ref, dx1_ref, dfw_ref, acc_ref):
        i = pl.program_id(0)
        j = pl.program_id(1)
        da = _dot_nt(dx_ref[...].astype(BF16), wd_ref[...])
        gv = g_ref[...].astype(F32)
        uv = u_ref[...].astype(F32)
        sg = _sigmoid(gv)
        dg = (da * uv * (sg * (1.0 + gv * (1.0 - sg)))).astype(BF16)
        du = (da * gv * sg).astype(BF16)
        dg_ref[...] = dg
        du_ref[...] = du
        dh = _dot_nt(dg, wg_ref[...]) + _dot_nt(du, wu_ref[...])

        @pl.when(j == 0)
        def _():
            acc_ref[...] = dh

        @pl.when(j > 0)
        def _():
            acc_ref[...] += dh

        @pl.when((i == 0) & (j == 0))
        def _():
            dfw_ref[...] = jnp.zeros_like(dfw_ref)

        @pl.when(j == nf - 1)
        def _():
            dx, dfw = _rms_bwd(acc_ref[...], x_ref[...], fw_ref[...])
            dx1_ref[...] = dx_ref[...] + dx
            dfw_ref[...] += jnp.sum(dfw, axis=0, keepdims=True)

    tok = pl.BlockSpec((tm, D_MODEL), lambda i, j: (i, 0))
    ffb = pl.BlockSpec((tm, FF_TILE), lambda i, j: (i, j))
    ffs = jax.ShapeDtypeStruct((t, D_FF), BF16)
    vec = pl.BlockSpec((1, D_MODEL), lambda i, j: (0, 0))
    return pl.pallas_call(
        body, name="ffn_bwd", grid=(t // tm, nf),
        in_specs=[tok, ffb, ffb,
                  pl.BlockSpec((D_MODEL, FF_TILE), lambda i, j: (0, j)),
                  pl.BlockSpec((D_MODEL, FF_TILE), lambda i, j: (0, j)),
                  pl.BlockSpec((FF_TILE, D_MODEL), lambda i, j: (j, 0)),
                  tok, vec],
        out_specs=[ffb, ffb, tok, vec],
        out_shape=[ffs, ffs, jax.ShapeDtypeStruct((t, D_MODEL), F32), jax.ShapeDtypeStruct((1, D_MODEL), F32)],
        scratch_shapes=[pltpu.VMEM((tm, D_MODEL), F32)],
        compiler_params=_cparams(("arbitrary", "arbitrary")),
    )(dx2, g, u, wg, wu, wd, x1, fw)


def _loss_grad(y, target, tm):
    t, d = y.shape

    def body(y_ref, t_ref, dy_ref, loss_ref):
        err = y_ref[...] - t_ref[...]
        dy_ref[...] = err * (1.0 / d)

        @pl.when(pl.program_id(0) == 0)
        def _():
            loss_ref[...] = jnp.zeros_like(loss_ref)

        part = jnp.sum(jnp.sum(err * err, axis=-1, keepdims=True), axis=0, keepdims=True)
        loss_ref[...] += part * (0.5 / d)

    tok = pl.BlockSpec((tm, d), lambda i: (i, 0))
    return pl.pallas_call(
        body, name="loss_grad", grid=(t // tm,),
        in_specs=[tok, tok],
        out_specs=[tok, pl.BlockSpec((1, 1), lambda i: (0, 0))],
        out_shape=[jax.ShapeDtypeStruct((t, d), F32), jax.ShapeDtypeStruct((1, 1), F32)],
        compiler_params=_cparams(("arbitrary",)),
    )(y, target)


def _tile(v, reps):
    return jnp.tile(v.reshape(1, -1), (1, reps))


def _local_step(x, target, p, wb):
    nb, s, d = x.shape
    t = nb * s
    tm = min(512, s)
    tq = min(512, s)
    xf = x.reshape(t, d)
    cosq, sinq = _rope_tables(s)
    ones512 = _block_ones(512, HEAD_DIM)
    lbs = _lower_bounds(p["hgrn_lb_logits"].reshape(DEPTH, 2 * D_HGRN)).reshape(DEPTH, 2, D_HGRN)

    saved = []
    cur = xf
    for l in range(DEPTH):
        qw = _tile(p["q_norm_w"][l], N_HEADS)
        kw = _tile(p["k_norm_w"][l], N_KV)
        gw = _tile(p["hgrn_gnorm_w"][l], 2)
        dww = jnp.pad(p["conv_dw_w"][l], ((0, 1), (0, 0)))
        pww = p["conv_pw_w"][l].astype(BF16)
        h0, proj = _rms_proj(cur, _row(p["mix_norm_w"][l]), wb["w_in"][l], tm)
        qr, kd, vd, kdt, vdt = _qkv_prep(proj, cosq, sinq, qw, kw, ones512, s, tm)
        o_attn, lse = _attn_fwd(qr, kd, vdt, nb, s, tq)
        y_hgrn, osum = _hgrn_fwd(proj, lbs[l], gw, nb, s)
        y_conv = _conv_fwd(proj, dww, _row(p["conv_dw_b"][l]), _row(p["conv_ln_w"][l]),
                           _row(p["conv_ln_b"][l]), pww, _row(p["conv_pw_b"][l]), nb, s)
        mixed, x1 = _mix_out(o_attn, y_hgrn, y_conv, cur, _row(p["attn_out_norm_w"][l]),
                             _row(p["conv_out_norm_w"][l]), wb["w_out"][l], tm)
        hf, g, u, a, x2 = _ffn_fwd(x1, _row(p["ffn_norm_w"][l]), wb["w_gate"][l], wb["w_up"][l],
                                   wb["w_down"][l], tm)
        saved.append(dict(x=cur, h0=h0, proj=proj, qr=qr, kd=kd, vd=vd, kdt=kdt, o_attn=o_attn, lse=lse,
                          osum=osum, y_conv=y_conv, mixed=mixed, x1=x1, hf=hf, g=g, u=u, a=a,
                          qw=qw, kw=kw, gw=gw, dww=dww, pww=pww))
        cur = x2

    dcur, loss = _loss_grad(cur, target.reshape(t, d), tm)

    grads = {k: [None] * DEPTH for k in WEIGHTS}
    dlb = [None] * DEPTH
    for l in reversed(range(DEPTH)):
        sv = saved[l]
        dg, du, dx1, dfw = _ffn_bwd(dcur, sv["g"], sv["u"], wb["w_gate"][l], wb["w_up"][l], wb["w_down"][l],
                                    sv["x1"], _row(p["ffn_norm_w"][l]), tm)
        grads["ffn_norm_w"][l] = dfw[0]
        grads["w_gate"][l] = _mm_tn(sv["hf"], dg, FF_TILE, "dw_gate", tm)
        grads["w_up"][l] = _mm_tn(sv["hf"], du, FF_TILE, "dw_up", tm)
        grads["w_down"][l] = _mm_tn(sv["a"], dcur, 512, "dw_down", tm)
        do_attn, dy_hgrn, dy_conv, daw, dcw = _mix_out_bwd(
            dx1, wb["w_out"][l], sv["o_attn"], sv["y_conv"], _row(p["attn_out_norm_w"][l]),
            _row(p["conv_out_norm_w"][l]), tm)
        grads["attn_out_norm_w"][l] = daw[0]
        grads["conv_out_norm_w"][l] = dcw[0]
        grads["w_out"][l] = _mm_tn(sv["mixed"], dx1, D_MODEL, "dw_out", tm)
        dq, dkd, dvd = _attn_bwd(sv["qr"], sv["kd"], sv["vd"], sv["kdt"], sv["o_attn"], sv["lse"], do_attn,
                                 nb, s, tq)
        dqkv, dqw, dkw = _qkv_bwd(sv["proj"], dq, dkd, dvd, cosq, sinq, sv["qw"], sv["kw"], ones512, s, tm)
        grads["q_norm_w"][l] = dqw.reshape(N_HEADS, HEAD_DIM).sum(0)
        grads["k_norm_w"][l] = dkw.reshape(N_KV, HEAD_DIM).sum(0)
        dhq, dzf, dzb, dhi, dhg, dgw, dlb_l = _hgrn_bwd(sv["proj"], lbs[l], sv["gw"], sv["osum"], dy_hgrn, nb, s)
        grads["hgrn_gnorm_w"][l] = dgw.reshape(nb * D_HGRN // HEAD_DIM, HEAD_DIM).sum(0)
        dlb[l] = dlb_l.sum(0)
        dab, ddww, ddwb, dlnw, dlnb, dpww, dpwb = _conv_bwd(
            sv["proj"], dy_conv, sv["dww"], _row(p["conv_dw_b"][l]), _row(p["conv_ln_w"][l]),
            _row(p["conv_ln_b"][l]), sv["pww"], nb, s)
        grads["conv_dw_w"][l] = ddww[:CONV_W]
        grads["conv_dw_b"][l] = ddwb[0]
        grads["conv_ln_w"][l] = dlnw[0]
        grads["conv_ln_b"][l] = dlnb[0]
        grads["conv_pw_w"][l] = dpww
        grads["conv_pw_b"][l] = dpwb[0]
        pieces = [dqkv, dhq, dzf, dzb, dhi, dhg, dab]
        grads["w_in"][l] = _dw_in(sv["h0"], pieces, tm)
        dcur, dnw = _proj_bwd(pieces, wb["w_in"][l], sv["x"], _row(p["mix_norm_w"][l]), dx1, tm)
        grads["mix_norm_w"][l] = dnw[0]

    dlog = _lower_bounds_bwd(p["hgrn_lb_logits"].reshape(DEPTH, 2 * D_HGRN),
                             jnp.stack(dlb).reshape(DEPTH, 2 * D_HGRN))
    out = {k: (v if k in BIG_AXIS else jnp.stack(v)) for k, v in grads.items() if k != "hgrn_lb_logits"}
    out["hgrn_lb_logits"] = dlog.reshape(DEPTH, 2, D_HGRN)
    return loss, dcur.reshape(nb, s, d), out


BIG_AXIS = {"w_in": 2, "w_out": 1, "w_gate": 2, "w_up": 2, "w_down": 1}
SMALL_SHARD_AXIS = {"hgrn_lb_logits": 2, "conv_dw_w": 2, "conv_pw_w": 1}
WEIGHTS = ("mix_norm_w", "w_in", "q_norm_w", "k_norm_w", "hgrn_lb_logits", "hgrn_gnorm_w", "conv_dw_w",
           "conv_dw_b", "conv_ln_w", "conv_ln_b", "conv_pw_w", "conv_pw_b", "attn_out_norm_w",
           "conv_out_norm_w", "w_out", "ffn_norm_w", "w_gate", "w_up", "w_down")
SMALL = tuple(n for n in WEIGHTS if n not in BIG_AXIS)


def _my_index():
    return 4 * lax.axis_index("x") + 2 * lax.axis_index("y") + lax.axis_index("c")


def _exchange(srcs, gather, name):
    na = len(srcs)
    npeer = N_DEV - 1

    def body(*refs):
        src_refs = refs[:na]
        out_refs = refs[na:2 * na]
        send_sems, recv_sems, local_sems = refs[2 * na:]
        x, y, c = lax.axis_index("x"), lax.axis_index("y"), lax.axis_index("c")
        me = 4 * x + 2 * y + c
        locals_, sends, recvs = [], [], []
        for a in range(na):
            src_ref, out_ref = src_refs[a], out_refs[a]

            def rows_for(j, src_ref=src_ref):
                return src_ref if gather else src_ref.at[j]

            locals_.append(pltpu.make_async_copy(rows_for(me), out_ref.at[me], local_sems.at[a]))
            for k in range(1, N_DEV):
                px = (1 - x) if (k & 4) else x
                py = (1 - y) if (k & 2) else y
                pc = (1 - c) if (k & 1) else c
                pidx = 4 * px + 2 * py + pc
                common = dict(send_sem=send_sems.at[a * npeer + k - 1], recv_sem=recv_sems.at[a * npeer + k - 1],
                              device_id=(px, py, pc), device_id_type=pl.DeviceIdType.MESH)
                sends.append(pltpu.make_async_remote_copy(src_ref=rows_for(pidx), dst_ref=out_ref.at[me], **common))
                recvs.append(pltpu.make_async_remote_copy(src_ref=rows_for(pidx), dst_ref=out_ref.at[pidx],
                                                          **common))
        for cp in locals_ + sends:
            cp.start()
        for cp in recvs:
            cp.wait_recv()
        for cp in sends:
            cp.wait_send()
        for cp in locals_:
            cp.wait()

    return pl.pallas_call(
        body, name=name,
        in_specs=[pl.BlockSpec(memory_space=pl.ANY)] * na,
        out_specs=[pl.BlockSpec(memory_space=pl.ANY)] * na,
        out_shape=[jax.ShapeDtypeStruct(((N_DEV,) + s.shape) if gather else s.shape, s.dtype) for s in srcs],
        scratch_shapes=[pltpu.SemaphoreType.DMA((na * npeer,)), pltpu.SemaphoreType.DMA((na * npeer,)),
                        pltpu.SemaphoreType.DMA((na,))],
    )(*srcs)


def _lane_group(n):
    g = 1
    while (g * n) % LANES:
        g += 1
    return g


def _cols_to_natural(gathered, layer, name):
    _, _, k, n = gathered.shape
    grp = _lane_group(n)
    place = jnp.stack([jnp.asarray(np.eye(n, grp * n, k=i * n), BF16) for i in range(grp)])

    def body(g_ref, p_ref, o_ref):
        acc = None
        for i in range(grp):
            part = _dot(g_ref[i], p_ref[i])
            acc = part if acc is None else acc + part
        o_ref[...] = acc.astype(BF16)

    return pl.pallas_call(
        body, name=name, grid=(N_DEV // grp,),
        in_specs=[pl.BlockSpec((grp, None, k, n), lambda j: (j, layer, 0, 0)),
                  pl.BlockSpec((grp, n, grp * n), lambda j: (0, 0, 0))],
        out_specs=pl.BlockSpec((k, grp * n), lambda j: (0, j)),
        out_shape=jax.ShapeDtypeStruct((k, N_DEV * n), BF16),
        compiler_params=_cparams(("parallel",)),
    )(gathered, place)


def _natural_to_cols(dws, name):
    k, n8 = dws[0].shape
    n = n8 // N_DEV
    grp = _lane_group(n)
    pick = jnp.stack([jnp.asarray(np.eye(grp * n, n, k=-i * n), BF16) for i in range(grp)])

    def body(*refs):
        d_refs = refs[:DEPTH]
        p_ref, o_ref = refs[DEPTH:]
        for l in range(DEPTH):
            @pl.when(pl.program_id(0) == l)
            def _(l=l):
                xb = d_refs[l][...].astype(BF16)
                for i in range(grp):
                    o_ref[i] = _dot(xb, p_ref[i]).astype(BF16)

    return pl.pallas_call(
        body, name=name, grid=(DEPTH, N_DEV // grp),
        in_specs=[pl.BlockSpec((k, grp * n), lambda l, j: (0, j))] * DEPTH
        + [pl.BlockSpec((grp, grp * n, n), lambda l, j: (0, 0, 0))],
        out_specs=pl.BlockSpec((grp, None, k, n), lambda l, j: (j, l, 0, 0)),
        out_shape=jax.ShapeDtypeStruct((N_DEV, DEPTH, k, n), BF16),
        compiler_params=_cparams(("arbitrary", "arbitrary")),
    )(*dws, pick)


def _adamw_math(w, g, m, v):
    m = ADAM_B1 * m + (1.0 - ADAM_B1) * g
    v = ADAM_B2 * v + (1.0 - ADAM_B2) * (g * g)
    m_hat = m / (1.0 - ADAM_B1 ** ADAM_STEP)
    v_hat = v / (1.0 - ADAM_B2 ** ADAM_STEP)
    delta = -ADAM_LR * (m_hat / (jnp.sqrt(v_hat) + ADAM_EPS) + ADAM_WD * w)
    return delta, m, v


def _sum_adamw(parts, w, m, v, name):
    _, k, n = w.shape
    tk = k
    for cand in (256, 176, 128):
        if k % cand == 0:
            tk = cand
            break

    def body(p_ref, w_ref, m_ref, v_ref, g_ref, d_ref, mo_ref, vo_ref):
        g = p_ref[0].astype(F32)
        for i in range(1, N_DEV):
            g = g + p_ref[i].astype(F32)
        g_ref[...] = g
        d_ref[...], mo_ref[...], vo_ref[...] = _adamw_math(w_ref[...], g, m_ref[...], v_ref[...])

    row = pl.BlockSpec((None, tk, n), lambda l, i: (l, i, 0))
    shp = jax.ShapeDtypeStruct(w.shape, F32)
    return pl.pallas_call(
        body, name=name, grid=(DEPTH, k // tk),
        in_specs=[pl.BlockSpec((N_DEV, None, tk, n), lambda l, i: (0, l, i, 0)), row, row, row],
        out_specs=[row, row, row, row],
        out_shape=[shp, shp, shp, shp],
        compiler_params=_cparams(("parallel", "parallel")),
    )(parts, w, m, v)


def _sum8(parts):
    r = parts.shape[1]

    def body(p_ref, g_ref):
        g = p_ref[0]
        for i in range(1, N_DEV):
            g = g + p_ref[i]
        g_ref[...] = g

    return pl.pallas_call(body, name="sum_small_grads", out_shape=jax.ShapeDtypeStruct((r, LANES), F32))(parts)


def _adamw(w, g, m, v):
    def body(w_ref, g_ref, m_ref, v_ref, d_ref, mo_ref, vo_ref):
        d_ref[...], mo_ref[...], vo_ref[...] = _adamw_math(w_ref[...], g_ref[...], m_ref[...], v_ref[...])

    shp = jax.ShapeDtypeStruct(w.shape, F32)
    return pl.pallas_call(body, name="adamw_small", out_shape=[shp, shp, shp])(w, g, m, v)


def _pack(arrays, dtype, row_multiple):
    flat = jnp.concatenate([a.reshape(-1).astype(dtype) for a in arrays])
    n = flat.shape[0]
    unit = row_multiple * LANES
    total = -(-n // unit) * unit
    return jnp.pad(flat, (0, total - n)).reshape(total // LANES, LANES)


def _unpack(flat2d, shapes, lead=()):
    flat = flat2d.reshape(lead + (-1,))
    out, off = [], 0
    for shp in shapes:
        n = int(np.prod(shp))
        out.append(flat[..., off:off + n].reshape(lead + tuple(shp)))
        off += n
    return out


def _shard_to_rows(full, axis):
    shp = full.shape
    k = shp[axis] // N_DEV
    r = full.reshape(shp[:axis] + (N_DEV, k) + shp[axis + 1:])
    return jnp.moveaxis(r, axis, 0)


def _rows_to_full(rows, axis):
    r = jnp.moveaxis(rows, 0, axis)
    shp = r.shape
    return r.reshape(shp[:axis] + (shp[axis] * shp[axis + 1],) + shp[axis + 2:])


def kernel(x, mix_norm_w, w_in, q_norm_w, k_norm_w, hgrn_lb_logits, hgrn_gnorm_w, conv_dw_w, conv_dw_b, conv_ln_w, conv_ln_b, conv_pw_w, conv_pw_b, attn_out_norm_w, conv_out_norm_w, w_out, ffn_norm_w, w_gate, w_up, w_down, loss_target, m_mix_norm_w, m_w_in, m_q_norm_w, m_k_norm_w, m_hgrn_lb_logits, m_hgrn_gnorm_w, m_conv_dw_w, m_conv_dw_b, m_conv_ln_w, m_conv_ln_b, m_conv_pw_w, m_conv_pw_b, m_attn_out_norm_w, m_conv_out_norm_w, m_w_out, m_ffn_norm_w, m_w_gate, m_w_up, m_w_down, v_mix_norm_w, v_w_in, v_q_norm_w, v_k_norm_w, v_hgrn_lb_logits, v_hgrn_gnorm_w, v_conv_dw_w, v_conv_dw_b, v_conv_ln_w, v_conv_ln_b, v_conv_pw_w, v_conv_pw_b, v_attn_out_norm_w, v_conv_out_norm_w, v_w_out, v_ffn_norm_w, v_w_gate, v_w_up, v_w_down):
    w_loc = dict(zip(WEIGHTS, (mix_norm_w, w_in, q_norm_w, k_norm_w, hgrn_lb_logits, hgrn_gnorm_w, conv_dw_w,
                               conv_dw_b, conv_ln_w, conv_ln_b, conv_pw_w, conv_pw_b, attn_out_norm_w,
                               conv_out_norm_w, w_out, ffn_norm_w, w_gate, w_up, w_down)))
    m_loc = dict(zip(WEIGHTS, (m_mix_norm_w, m_w_in, m_q_norm_w, m_k_norm_w, m_hgrn_lb_logits, m_hgrn_gnorm_w,
                               m_conv_dw_w, m_conv_dw_b, m_conv_ln_w, m_conv_ln_b, m_conv_pw_w, m_conv_pw_b,
                               m_attn_out_norm_w, m_conv_out_norm_w, m_w_out, m_ffn_norm_w, m_w_gate, m_w_up,
                               m_w_down)))
    v_loc = dict(zip(WEIGHTS, (v_mix_norm_w, v_w_in, v_q_norm_w, v_k_norm_w, v_hgrn_lb_logits, v_hgrn_gnorm_w,
                               v_conv_dw_w, v_conv_dw_b, v_conv_ln_w, v_conv_ln_b, v_conv_pw_w, v_conv_pw_b,
                               v_attn_out_norm_w, v_conv_out_norm_w, v_w_out, v_ffn_norm_w, v_w_gate, v_w_up,
                               v_w_down)))
    me = _my_index()
    big = tuple(BIG_AXIS)
    sms = tuple(SMALL_SHARD_AXIS)

    got = dict(zip(big, _exchange([w_loc[n].astype(BF16) for n in big], True, "gather_weights")))
    wb = {}
    for n in big:
        if BIG_AXIS[n] == 2:
            wb[n] = [_cols_to_natural(got[n], l, "relayout_" + n) for l in range(DEPTH)]
        else:
            wb[n] = [got[n][:, l].reshape(-1, got[n].shape[-1]) for l in range(DEPTH)]
    sm_shapes = [w_loc[n].shape for n in sms]
    got_s = _exchange([_pack([w_loc[n] for n in sms], F32, 8)], True, "gather_small_params")[0]
    p_full = {n: w_loc[n] for n in SMALL if n not in SMALL_SHARD_AXIS}
    for n, a in zip(sms, _unpack(got_s, sm_shapes, (N_DEV,))):
        p_full[n] = _rows_to_full(a, SMALL_SHARD_AXIS[n])

    loss_part, grad_x, g = _local_step(x, loss_target, p_full, wb)
    loss = lax.psum(loss_part[0, 0], MESH_AXES)

    send = []
    for n in big:
        if BIG_AXIS[n] == 2:
            send.append(_natural_to_cols(g[n], "split_d" + n))
        else:
            k = w_loc[n].shape[1]
            send.append(jnp.stack([gl.reshape(N_DEV, k, gl.shape[-1]) for gl in g[n]], axis=1).astype(BF16))
    parts = _exchange(send, False, "exchange_weight_grads")
    big_out = {n: _sum_adamw(pt, w_loc[n], m_loc[n], v_loc[n], "sum_adamw_" + n) for n, pt in zip(big, parts)}

    small_shapes = [g[n].shape for n in SMALL]
    parts = _exchange([_pack([g[n] for n in SMALL], F32, 8)], True, "gather_small_grads")[0]
    g_small = dict(zip(SMALL, _unpack(_sum8(parts), small_shapes)))
    for n in sms:
        ax = SMALL_SHARD_AXIS[n]
        k = w_loc[n].shape[ax]
        g_small[n] = lax.dynamic_slice_in_dim(g_small[n], me * k, k, axis=ax)
    loc_shapes = [w_loc[n].shape for n in SMALL]
    packed = [_pack([d[n] for n in SMALL], F32, 8) for d in (w_loc, g_small, m_loc, v_loc)]
    res = _adamw(*packed)
    small_out = [g_small] + [dict(zip(SMALL, _unpack(r, loc_shapes))) for r in res]

    def pick(i, n):
        return big_out[n][i] if n in BIG_AXIS else small_out[i][n]

    return (loss, grad_x) + tuple(pick(i, n) for i in range(4) for n in WEIGHTS)
```

```python
import functools

import jax
import jax.numpy as jnp
import numpy as np
from jax import lax
from jax.experimental import pallas as pl
from jax.experimental.pallas import tpu as pltpu

F32 = jnp.float32
BF16 = jnp.bfloat16

D_MODEL = 1024
D_ATTN = 512
D_HGRN = 256
D_CONV = 256
HEAD_DIM = 64
N_HEADS = 8
N_KV = 2
GRID_W = 64
ROPE_THETA = 10000.0
CHUNK = 64
F_MIN = 1e-6
CONV_W = 31
CONV_PAD = 15
D_FF = 2816
D_PROJ = 2560
EPS = 1e-6
LN_EPS = 1e-5
DEPTH = 2
ADAM_LR = 0.001
ADAM_B1 = 0.9
ADAM_B2 = 0.999
ADAM_EPS = 1e-08
ADAM_WD = 0.01
ADAM_STEP = 10
N_DEV = 8
MESH_AXES = ("x", "y", "c")

COL_HQ, COL_ZFW, COL_ZBW, COL_HI, COL_HG = 6, 8, 10, 12, 14
COL_CA, COL_CB = 8, 9

LANES = 128
VMEM_LIMIT_MB = 56


def _cparams(dims=None):
    return pltpu.CompilerParams(dimension_semantics=dims, vmem_limit_bytes=VMEM_LIMIT_MB * 2 ** 20)


def _dot(a, b):
    return jnp.dot(a, b, preferred_element_type=F32)


def _dot_nt(a, b):
    return lax.dot_general(a, b, (((1,), (1,)), ((), ())), preferred_element_type=F32)


def _dot_tn(a, b):
    return lax.dot_general(a, b, (((0,), (0,)), ((), ())), preferred_element_type=F32)


def _split_bf16(x, parts):
    out = []
    r = x
    for _ in range(parts):
        p = r.astype(BF16)
        out.append(p)
        r = r - p.astype(F32)
    return out


def _dot_precise(x, m_bf16, parts=3):
    acc = None
    for p in _split_bf16(x, parts):
        t = _dot(p, m_bf16)
        acc = t if acc is None else acc + t
    return acc


def _block_ones(width, group):
    i = np.arange(width)
    return jnp.asarray((i[:, None] // group) == (i[None, :] // group), dtype=BF16)


def _sigmoid(x):
    return 1.0 / (1.0 + jnp.exp(-x))


def _rot(x):
    w = x.shape[1]
    lane = lax.broadcasted_iota(jnp.int32, x.shape, 1)
    first = (lane % 32) < 16
    return jnp.where(first, -pltpu.roll(x, w - 16, 1), pltpu.roll(x, 16, 1))


def _rope(x, cos, sin):
    return x * cos + _rot(x) * sin


def _rope_t(dy, cos, sin):
    return dy * cos - _rot(dy * sin)


def _row(v):
    return v.reshape(1, -1)


def _rms_proj(x, wn, w, tm):
    t, d = x.shape
    n = w.shape[1]

    def body(x_ref, wn_ref, w_ref, h_ref, y_ref):
        xv = x_ref[...]
        r = lax.rsqrt(jnp.mean(xv * xv, axis=-1, keepdims=True) + EPS)
        h = (xv * r * wn_ref[...]).astype(BF16)
        h_ref[...] = h
        y_ref[...] = _dot(h, w_ref[...])

    return pl.pallas_call(
        body, name="rms_proj", grid=(t // tm,),
        in_specs=[pl.BlockSpec((tm, d), lambda i: (i, 0)),
                  pl.BlockSpec((1, d), lambda i: (0, 0)),
                  pl.BlockSpec((d, n), lambda i: (0, 0))],
        out_specs=[pl.BlockSpec((tm, d), lambda i: (i, 0)),
                   pl.BlockSpec((tm, n), lambda i: (i, 0))],
        out_shape=[jax.ShapeDtypeStruct((t, d), BF16), jax.ShapeDtypeStruct((t, n), F32)],
        compiler_params=_cparams(("parallel",)),
    )(x, wn, w)


def _rms_bwd(dh, x, wn):
    r = lax.rsqrt(jnp.mean(x * x, axis=-1, keepdims=True) + EPS)
    g = dh * wn
    dx = r * (g - x * (r * r) * jnp.mean(g * x, axis=-1, keepdims=True))
    return dx, dh * x * r


def _proj_bwd(pieces, w, x, wn, dres, tm):
    t = x.shape[0]
    d = x.shape[1]
    n = w.shape[1]
    widths = [p.shape[1] for p in pieces]
    offs = [sum(widths[:i]) for i in range(len(widths))]
    assert sum(widths) == n
    npc = len(pieces)

    def body(*refs):
        p_refs = refs[:npc]
        w_ref, x_ref, wn_ref, dr_ref, dx_ref, dwn_ref = refs[npc:]
        dh = None
        for p_ref, o, wd in zip(p_refs, offs, widths):
            part = _dot_nt(p_ref[...], w_ref[:, o:o + wd])
            dh = part if dh is None else dh + part
        dx, dwn = _rms_bwd(dh, x_ref[...], wn_ref[...])
        dx_ref[...] = dr_ref[...] + dx

        @pl.when(pl.program_id(0) == 0)
        def _():
            dwn_ref[...] = jnp.zeros_like(dwn_ref)

        dwn_ref[...] += jnp.sum(dwn, axis=0, keepdims=True)

    return pl.pallas_call(
        body, name="proj_bwd", grid=(t // tm,),
        in_specs=[pl.BlockSpec((tm, wd), lambda i: (i, 0)) for wd in widths]
        + [pl.BlockSpec((d, n), lambda i: (0, 0)),
           pl.BlockSpec((tm, d), lambda i: (i, 0)),
           pl.BlockSpec((1, d), lambda i: (0, 0)),
           pl.BlockSpec((tm, d), lambda i: (i, 0))],
        out_specs=[pl.BlockSpec((tm, d), lambda i: (i, 0)),
                   pl.BlockSpec((1, d), lambda i: (0, 0))],
        out_shape=[jax.ShapeDtypeStruct((t, d), F32), jax.ShapeDtypeStruct((1, d), F32)],
        compiler_params=_cparams(("arbitrary",)),
    )(*pieces, w, x, wn, dres)


def _dw_in(h0, pieces, tm):
    t, k = h0.shape
    widths = [p.shape[1] for p in pieces]
    offs = [sum(widths[:i]) for i in range(len(widths))]
    n = sum(widths)
    npc = len(pieces)

    def body(*refs):
        h_ref = refs[0]
        p_refs = refs[1:1 + npc]
        o_ref = refs[1 + npc]

        @pl.when(pl.program_id(0) == 0)
        def _():
            o_ref[...] = jnp.zeros_like(o_ref)

        ht = h_ref[...].astype(F32).T.astype(BF16)
        for p_ref, o, wd in zip(p_refs, offs, widths):
            o_ref[:, o:o + wd] += _dot(ht, p_ref[...])

    return pl.pallas_call(
        body, name="dw_in", grid=(t // tm,),
        in_specs=[pl.BlockSpec((tm, k), lambda i: (i, 0))]
        + [pl.BlockSpec((tm, wd), lambda i: (i, 0)) for wd in widths],
        out_specs=pl.BlockSpec((k, n), lambda i: (0, 0)),
        out_shape=jax.ShapeDtypeStruct((k, n), F32),
        compiler_params=_cparams(("arbitrary",)),
    )(h0, *pieces)


def _mm_tn(a, b, tn, name, tm):
    t, k = a.shape
    n = b.shape[1]

    def body(a_ref, b_ref, o_ref):
        @pl.when(pl.program_id(1) == 0)
        def _():
            o_ref[...] = jnp.zeros_like(o_ref)

        o_ref[...] += _dot_tn(a_ref[...].astype(BF16), b_ref[...].astype(BF16))

    return pl.pallas_call(
        body, name=name, grid=(n // tn, t // tm),
        in_specs=[pl.BlockSpec((tm, k), lambda j, i: (i, 0)),
                  pl.BlockSpec((tm, tn), lambda j, i: (i, j))],
        out_specs=pl.BlockSpec((k, tn), lambda j, i: (0, j)),
        out_shape=jax.ShapeDtypeStruct((k, n), F32),
        compiler_params=_cparams(("parallel", "arbitrary")),
    )(a, b)


def _rope_tables(s):
    rows = s // GRID_W
    row_id = jnp.repeat(jnp.arange(rows, dtype=F32), GRID_W)
    col_id = jnp.tile(jnp.arange(GRID_W, dtype=F32), rows)
    half = HEAD_DIM // 2
    inv_freq = ROPE_THETA ** (-jnp.arange(0, half, 2, dtype=F32) / half)
    ang_r = row_id[:, None] * inv_freq[None, :]
    ang_c = col_id[:, None] * inv_freq[None, :]
    ang = jnp.concatenate([ang_r, ang_r, ang_c, ang_c], axis=-1)
    cos, sin = jnp.cos(ang), jnp.sin(ang)
    return jnp.tile(cos, (1, N_HEADS)), jnp.tile(sin, (1, N_HEADS))


def _head_rms(x, w, ones):
    r = lax.rsqrt(_dot_precise(x * x, ones, 2) * (1.0 / HEAD_DIM) + EPS)
    return x * r * w, r


def _dup_half(x, kv):
    lane = lax.broadcasted_iota(jnp.int32, x.shape, 1)
    sel = (lane < 64) if kv == 0 else (lane >= 64)
    return jnp.where(sel, x, pltpu.roll(x, 64, 1))


def _qkv_prep(proj, cosq, sinq, qw, kw, ones, s, tm):
    t = proj.shape[0]
    ns = s // tm

    def body(p_ref, cos_ref, sin_ref, qw_ref, kw_ref, ones_ref, q_out, kd_out, vd_out, kdt_out, vdt_out):
        cos = cos_ref[...]
        sin = sin_ref[...]
        ones_m = ones_ref[...]
        qn, _ = _head_rms(p_ref[:, 0:512], qw_ref[...], ones_m)
        q_out[...] = (_rope(qn, cos, sin) * (HEAD_DIM ** -0.5)).astype(BF16)
        kn, _ = _head_rms(p_ref[:, 512:640], kw_ref[...], ones_m[0:128, 0:128])
        kr = _rope(kn, cos[:, 0:128], sin[:, 0:128])
        v = p_ref[:, 640:768]
        for kv in range(N_KV):
            kd = _dup_half(kr, kv)
            vd = _dup_half(v, kv)
            kd_out[kv] = kd.astype(BF16)
            vd_out[kv] = vd.astype(BF16)
            kdt_out[kv] = kd.T.astype(BF16)
            vdt_out[kv] = vd.T.astype(BF16)

    return pl.pallas_call(
        body, name="qkv_prep", grid=(t // tm,),
        in_specs=[pl.BlockSpec((tm, 768), lambda i: (i, 0)),
                  pl.BlockSpec((tm, 512), lambda i: (i % ns, 0)),
                  pl.BlockSpec((tm, 512), lambda i: (i % ns, 0)),
                  pl.BlockSpec((1, 512), lambda i: (0, 0)),
                  pl.BlockSpec((1, 128), lambda i: (0, 0)),
                  pl.BlockSpec((512, 512), lambda i: (0, 0))],
        out_specs=[pl.BlockSpec((tm, 512), lambda i: (i, 0)),
                   pl.BlockSpec((N_KV, tm, 128), lambda i: (0, i, 0)),
                   pl.BlockSpec((N_KV, tm, 128), lambda i: (0, i, 0)),
                   pl.BlockSpec((N_KV, 128, tm), lambda i: (0, 0, i)),
                   pl.BlockSpec((N_KV, 128, tm), lambda i: (0, 0, i))],
        out_shape=[jax.ShapeDtypeStruct((t, 512), BF16),
                   jax.ShapeDtypeStruct((N_KV, t, 128), BF16),
                   jax.ShapeDtypeStruct((N_KV, t, 128), BF16),
                   jax.ShapeDtypeStruct((N_KV, 128, t), BF16),
                   jax.ShapeDtypeStruct((N_KV, 128, t), BF16)],
        compiler_params=_cparams(("parallel",)),
    )(proj, cosq, sinq, qw, kw, ones)


def _qkv_bwd(proj, dq, dkd, dvd, cosq, sinq, qw, kw, ones, s, tm):
    t = proj.shape[0]
    ns = s // tm

    def body(p_ref, dq_ref, dkd_ref, dvd_ref, cos_ref, sin_ref, qw_ref, kw_ref, ones_ref,
             out_ref, dqw_ref, dkw_ref):
        cos = cos_ref[...]
        sin = sin_ref[...]
        ones_m = ones_ref[...]
        ones_k = ones_m[0:128, 0:128]

        def norm_bwd(x, w, dn, om):
            r = lax.rsqrt(_dot_precise(x * x, om, 2) * (1.0 / HEAD_DIM) + EPS)
            g = dn * w
            dx = r * (g - x * (r * r) * (_dot_precise(g * x, om, 2) * (1.0 / HEAD_DIM)))
            return dx, jnp.sum(dn * x * r, axis=0, keepdims=True)

        q = p_ref[:, 0:512]
        dqn = _rope_t(dq_ref[...], cos, sin) * (HEAD_DIM ** -0.5)
        dq_raw, dqw = norm_bwd(q, qw_ref[...], dqn, ones_m)
        out_ref[:, 0:512] = dq_raw.astype(BF16)

        lane = lax.broadcasted_iota(jnp.int32, (tm, 128), 1)

        def fold(ref):
            a0 = ref[0]
            a1 = ref[1]
            f0 = a0 + pltpu.roll(a0, 64, 1)
            f1 = a1 + pltpu.roll(a1, 64, 1)
            return jnp.where(lane < 64, f0, f1)

        k = p_ref[:, 512:640]
        dkn = _rope_t(fold(dkd_ref), cos[:, 0:128], sin[:, 0:128])
        dk_raw, dkw = norm_bwd(k, kw_ref[...], dkn, ones_k)
        out_ref[:, 512:640] = dk_raw.astype(BF16)
        out_ref[:, 640:768] = fold(dvd_ref).astype(BF16)

        @pl.when(pl.program_id(0) == 0)
        def _():
            dqw_ref[...] = jnp.zeros_like(dqw_ref)
            dkw_ref[...] = jnp.zeros_like(dkw_ref)

        dqw_ref[...] += dqw
        dkw_ref[...] += dkw

    return pl.pallas_call(
        body, name="qkv_bwd", grid=(t // tm,),
        in_specs=[pl.BlockSpec((tm, 768), lambda i: (i, 0)),
                  pl.BlockSpec((tm, 512), lambda i: (i, 0)),
                  pl.BlockSpec((N_KV, tm, 128), lambda i: (0, i, 0)),
                  pl.BlockSpec((N_KV, tm, 128), lambda i: (0, i, 0)),
                  pl.BlockSpec((tm, 512), lambda i: (i % ns, 0)),
                  pl.BlockSpec((tm, 512), lambda i: (i % ns, 0)),
                  pl.BlockSpec((1, 512), lambda i: (0, 0)),
                  pl.BlockSpec((1, 128), lambda i: (0, 0)),
                  pl.BlockSpec((512, 512), lambda i: (0, 0))],
        out_specs=[pl.BlockSpec((tm, 768), lambda i: (i, 0)),
                   pl.BlockSpec((1, 512), lambda i: (0, 0)),
                   pl.BlockSpec((1, 128), lambda i: (0, 0))],
        out_shape=[jax.ShapeDtypeStruct((t, 768), BF16),
                   jax.ShapeDtypeStruct((1, 512), F32),
                   jax.ShapeDtypeStruct((1, 128), F32)],
        compiler_params=_cparams(("arbitrary",)),
    )(proj, dq, dkd, dvd, cosq, sinq, qw, kw, ones)


def _attn_fwd(q, kd, vdt, nb, s, tq):
    t = q.shape[0]
    nq = s // tq

    def body(q_ref, k_ref, vt_ref, o_ref, lse_ref):
        qv = q_ref[...].astype(F32)
        lane = lax.broadcasted_iota(jnp.int32, qv.shape, 1)
        k = k_ref[0]
        vt = vt_ref[0]
        outs = []
        for half in range(2):
            qh = jnp.where((lane < 64) if half == 0 else (lane >= 64), qv, 0.0).astype(BF16)
            st = _dot_nt(k, qh)
            m = jnp.max(st, axis=0, keepdims=True)
            p = jnp.exp(st - m)
            l = jnp.sum(p, axis=0, keepdims=True)
            ot = _dot(vt, p.astype(BF16)) / l
            lse_ref[0, half] = m + jnp.log(l)
            outs.append(ot)
        row = lax.broadcasted_iota(jnp.int32, outs[0].shape, 0)
        o_ref[...] = jnp.where(row < 64, outs[0], outs[1]).T

    return pl.pallas_call(
        body, name="attn_fwd", grid=(nb, N_HEADS // 2, nq),
        in_specs=[pl.BlockSpec((tq, 128), lambda b, p, i: (b * nq + i, p)),
                  pl.BlockSpec((1, s, 128), lambda b, p, i: (p // 2, b, 0)),
                  pl.BlockSpec((1, 128, s), lambda b, p, i: (p // 2, 0, b))],
        out_specs=[pl.BlockSpec((tq, 128), lambda b, p, i: (b * nq + i, p)),
                   pl.BlockSpec((1, 2, 1, tq), lambda b, p, i: (b, p, 0, i))],
        out_shape=[jax.ShapeDtypeStruct((t, D_ATTN), F32),
                   jax.ShapeDtypeStruct((nb, N_HEADS, 1, s), F32)],
        compiler_params=_cparams(("parallel", "parallel", "parallel")),
    )(q, kd, vdt)


def _attn_bwd(q, kd, vd, kdt, o, lse, do, nb, s, tq):
    t = q.shape[0]
    nq = s // tq
    ones8 = jnp.ones((8, 128), BF16)

    def body(q_ref, k_ref, v_ref, kt_ref, o_ref, lse_ref, do_ref, ones_ref, dq_ref, dk_ref, dv_ref):
        @pl.when((pl.program_id(2) == 0) & (pl.program_id(3) == 0))
        def _():
            dk_ref[...] = jnp.zeros_like(dk_ref)
            dv_ref[...] = jnp.zeros_like(dv_ref)

        qv = q_ref[...].astype(F32)
        dov = do_ref[...]
        ov = o_ref[...]
        lane = lax.broadcasted_iota(jnp.int32, qv.shape, 1)
        k = k_ref[0]
        v = v_ref[0]
        kt = kt_ref[0]
        dqs = []
        dk_acc = None
        dv_acc = None
        for half in range(2):
            sel = (lane < 64) if half == 0 else (lane >= 64)
            qh = jnp.where(sel, qv, 0.0).astype(BF16)
            doh = jnp.where(sel, dov, 0.0)
            dob = doh.astype(BF16)
            delta = None
            for part in _split_bf16(doh * ov, 3):
                d8 = _dot_nt(ones_ref[...], part)
                delta = d8 if delta is None else delta + d8
            delta = delta[0:1, :]
            st = _dot_nt(k, qh)
            pt = jnp.exp(st - lse_ref[0, half])
            dpt = _dot_nt(v, dob)
            dst = (pt * (dpt - delta)).astype(BF16)
            dkh = _dot(dst, qh)
            dvh = _dot(pt.astype(BF16), dob)
            dk_acc = dkh if dk_acc is None else dk_acc + dkh
            dv_acc = dvh if dv_acc is None else dv_acc + dvh
            dqs.append(_dot(kt, dst))
        dk_ref[0] += dk_acc
        dv_ref[0] += dv_acc
        row = lax.broadcasted_iota(jnp.int32, dqs[0].shape, 0)
        dq_ref[...] = jnp.where(row < 64, dqs[0], dqs[1]).T

    qmap = lambda b, g, p, i: (b * nq + i, g * 2 + p)
    kvmap = lambda b, g, p, i: (g, b, 0)
    return pl.pallas_call(
        body, name="attn_bwd", grid=(nb, N_KV, 2, nq),
        in_specs=[pl.BlockSpec((tq, 128), qmap),
                  pl.BlockSpec((1, s, 128), kvmap),
                  pl.BlockSpec((1, s, 128), kvmap),
                  pl.BlockSpec((1, 128, s), lambda b, g, p, i: (g, 0, b)),
                  pl.BlockSpec((tq, 128), qmap),
                  pl.BlockSpec((1, 2, 1, tq), lambda b, g, p, i: (b, g * 2 + p, 0, i)),
                  pl.BlockSpec((tq, 128), qmap),
                  pl.BlockSpec((8, 128), lambda b, g, p, i: (0, 0))],
        out_specs=[pl.BlockSpec((tq, 128), qmap),
                   pl.BlockSpec((1, s, 128), kvmap),
                   pl.BlockSpec((1, s, 128), kvmap)],
        out_shape=[jax.ShapeDtypeStruct((t, D_ATTN), F32),
                   jax.ShapeDtypeStruct((N_KV, t, 128), F32),
                   jax.ShapeDtypeStruct((N_KV, t, 128), F32)],
        compiler_params=_cparams(("parallel", "parallel", "arbitrary", "arbitrary")),
    )(q, kd, vd, kdt, o, lse, do, ones8)


def _tri_mats():
    i = np.arange(CHUNK)
    lower = jnp.asarray(i[:, None] >= i[None, :], dtype=BF16)
    upper = jnp.asarray(i[:, None] <= i[None, :], dtype=BF16)
    return jnp.stack([lower, upper])


def _running_sum(tri, x):
    acc = None
    for part in _split_bf16(x, 3):
        t = _dot(tri, part)
        acc = t if acc is None else acc + t
    return acc


def _gates(z, lb):
    sig = _sigmoid(z)
    f = lb + (1.0 - lb) * sig
    logf = jnp.log(jnp.maximum(f, F_MIN))
    k = (1.0 - lb) * (1.0 - sig)
    return sig, f, logf, k


def _row_group(jg, anti):
    if anti:
        return 0, 8 * jg + 8
    return 8 * jg, CHUNK


def _chunk_fwd(q, k, v, b, st, bones, bmask, anti):
    b_last = b[0:1] if anti else b[CHUNK - 1:CHUNK]
    qb = q * jnp.exp(b)
    o_inter = _dot_nt(qb.astype(BF16), st.astype(BF16))
    kt = k * jnp.exp(b_last - b)
    st_new = st * jnp.exp(b_last) + _dot_tn(v.astype(BF16), kt.astype(BF16)) * bmask
    tt = lax.broadcasted_iota(jnp.int32, (CHUNK, LANES), 0)
    blocks = []
    for jg in range(8):
        r0, r1 = _row_group(jg, anti)
        qr = q[r0:r1]
        br = b[r0:r1]
        tr = tt[r0:r1]
        acc = None
        for i in range(8):
            sc = 8 * jg + i
            mask = (tr <= sc) if anti else (tr >= sc)
            e = jnp.where(mask, jnp.exp(jnp.minimum(br - b[sc:sc + 1], 0.0)), 0.0)
            w = qr * e * k[sc:sc + 1]
            pb = _dot(w.astype(BF16), bones)
            term = pb * v[sc:sc + 1]
            acc = term if acc is None else acc + term
        blocks.append((r0, r1, acc))
    o = o_inter
    pieces = []
    for g in range(8):
        tot = o[8 * g:8 * g + 8]
        for (r0, r1, acc) in blocks:
            if r0 <= 8 * g and 8 * g + 8 <= r1:
                tot = tot + acc[8 * g - r0:8 * g - r0 + 8]
        pieces.append(tot)
    return jnp.concatenate(pieces, axis=0), st_new


def _hgrn_fwd(proj, lb, gw, nb, s, ride=None):
    t = proj.shape[0]
    nc = s // CHUNK
    tri = _tri_mats()
    bones = _block_ones(LANES, HEAD_DIM)
    rd = _ride_plan(ride)

    def body(*refs):
        q_ref, zf_ref, zb_ref, v_ref, g_ref, lb_ref, gw_ref, tri_ref, bones_ref = refs[:9]
        y_ref, os_ref = refs[9 + rd.n:11 + rd.n]
        st_ref = refs[11 + 2 * rd.n]
        copies = rd.copies(refs[9:9 + rd.n], refs[11 + rd.n:11 + 2 * rd.n], refs[12 + 2 * rd.n:])
        step_id = pl.program_id(0) * 2 + pl.program_id(1)
        _ride_start(copies, step_id == 0)
        bones_m = bones_ref[...]
        bmask = bones_m.astype(F32)
        for anti in (False, True):
            z_ref = zb_ref if anti else zf_ref
            lbv = lb_ref[1:2] if anti else lb_ref[0:1]
            trim = tri_ref[1] if anti else tri_ref[0]
            st_ref[...] = jnp.zeros_like(st_ref)

            def step(n, carry, anti=anti, z_ref=z_ref, lbv=lbv, trim=trim):
                cn = (nc - 1 - n) if anti else n
                rows = pl.ds(pl.multiple_of(cn * CHUNK, CHUNK), CHUNK)
                q = q_ref[rows, :]
                v = v_ref[rows, :]
                _, _, logf, k = _gates(z_ref[rows, :], lbv)
                b = _running_sum(trim, logf)
                o, st_new = _chunk_fwd(q, k, v, b, st_ref[...], bones_m, bmask, anti)
                st_ref[...] = st_new
                if anti:
                    osum = os_ref[rows, :] + o
                    os_ref[rows, :] = osum
                    r = lax.rsqrt(_dot_precise(osum * osum, bones_m, 2) * (1.0 / HEAD_DIM) + EPS)
                    hg = g_ref[rows, :]
                    y_ref[rows, :] = osum * r * gw_ref[...] * (hg * _sigmoid(hg))
                else:
                    os_ref[rows, :] = o
                return carry

            lax.fori_loop(0, nc, step, 0)
        _ride_wait(copies, step_id == nb * 2 - 1)

    def col(c):
        return pl.BlockSpec((s, LANES), lambda b, p, c=c: (b, c + p))

    return pl.pallas_call(
        body, name="hgrn_fwd", grid=(nb, 2),
        in_specs=[col(COL_HQ), col(COL_ZFW), col(COL_ZBW), col(COL_HI), col(COL_HG),
                  pl.BlockSpec((2, LANES), lambda b, p: (0, p)),
                  pl.BlockSpec((1, LANES), lambda b, p: (0, 0)),
                  pl.BlockSpec((2, CHUNK, CHUNK), lambda b, p: (0, 0, 0)),
                  pl.BlockSpec((LANES, LANES), lambda b, p: (0, 0))] + rd.in_specs,
        out_specs=[pl.BlockSpec((s, LANES), lambda b, p: (b, p)),
                   pl.BlockSpec((s, LANES), lambda b, p: (b, p))] + rd.out_specs,
        out_shape=[jax.ShapeDtypeStruct((t, D_HGRN), F32), jax.ShapeDtypeStruct((t, D_HGRN), F32)] + rd.out_shape,
        scratch_shapes=[pltpu.VMEM((LANES, LANES), F32)] + rd.scratch,
        compiler_params=_cparams(("arbitrary", "arbitrary")),
    )(proj, proj, proj, proj, proj, lb, gw, tri, bones, *rd.srcs)


def _chunk_bwd(q, k, v, b, do, st_in, rt, bones, bmask, anti):
    b_last = b[0:1] if anti else b[CHUNK - 1:CHUNK]
    eb = jnp.exp(b)
    ebl = jnp.exp(b_last - b)
    dob = do.astype(BF16)
    rtb = rt.astype(BF16)
    dq_inter = eb * _dot(dob, st_in.astype(BF16))
    dk_inter = ebl * _dot(v.astype(BF16), rtb)
    dv_inter = _dot_nt((k * ebl).astype(BF16), rtb)
    rt_new = rt * jnp.exp(b_last) + _dot_tn(dob, (q * eb).astype(BF16)) * bmask
    tt = lax.broadcasted_iota(jnp.int32, (CHUNK, LANES), 0)
    r8 = lax.broadcasted_iota(jnp.int32, (8, LANES), 0)
    blocks = []
    dk_pieces = []
    dv_pieces = []
    for jg in range(8):
        r0, r1 = _row_group(jg, anti)
        qr = q[r0:r1]
        br = b[r0:r1]
        tr = tt[r0:r1]
        dor = do[r0:r1]
        acc = None
        dk_blk = jnp.zeros((8, LANES), F32)
        dv_blk = jnp.zeros((8, LANES), F32)
        for i in range(8):
            sc = 8 * jg + i
            mask = (tr <= sc) if anti else (tr >= sc)
            e = jnp.where(mask, jnp.exp(jnp.minimum(br - b[sc:sc + 1], 0.0)), 0.0)
            qe = qr * e
            ke = e * k[sc:sc + 1]
            pb = _dot((qe * k[sc:sc + 1]).astype(BF16), bones)
            dpb = _dot((dor * v[sc:sc + 1]).astype(BF16), bones)
            term = dpb * ke
            acc = term if acc is None else acc + term
            dk_s = jnp.sum(dpb * qe, axis=0, keepdims=True)
            dv_s = jnp.sum(pb * dor, axis=0, keepdims=True)
            dk_blk = jnp.where(r8 == i, dk_s, dk_blk)
            dv_blk = jnp.where(r8 == i, dv_s, dv_blk)
        blocks.append((r0, r1, acc))
        dk_pieces.append(dk_blk)
        dv_pieces.append(dv_blk)
    dq_pieces = []
    for g in range(8):
        tot = dq_inter[8 * g:8 * g + 8]
        for (r0, r1, acc) in blocks:
            if r0 <= 8 * g and 8 * g + 8 <= r1:
                tot = tot + acc[8 * g - r0:8 * g - r0 + 8]
        dq_pieces.append(tot)
    dq = jnp.concatenate(dq_pieces, axis=0)
    dk = dk_inter + jnp.concatenate(dk_pieces, axis=0)
    dv = dv_inter + jnp.concatenate(dv_pieces, axis=0)
    db_last = (jnp.sum(k * dk_inter, axis=0, keepdims=True)
               + jnp.exp(b_last) * jnp.sum(rt * st_in, axis=0, keepdims=True))
    return dq, dk, dv, rt_new, db_last


def _hgrn_bwd(proj, lb, gw, osum, dy, nb, s, ride=None):
    t = proj.shape[0]
    nc = s // CHUNK
    tri = _tri_mats()
    bones = _block_ones(LANES, HEAD_DIM)
    rd = _ride_plan(ride)

    def body(*refs):
        q_ref, zf_ref, zb_ref, v_ref, g_ref, lb_ref, gw_ref, os_ref, dy_ref, tri_ref, bones_ref = refs[:11]
        dq_ref, dzf_ref, dzb_ref, dv_ref, dg_ref, dgw_ref, dlb_ref = refs[11 + rd.n:18 + rd.n]
        do_sc, dq_sc, dv_sc, st_sc, st_cur, rt_cur = refs[18 + 2 * rd.n:24 + 2 * rd.n]
        copies = rd.copies(refs[11:11 + rd.n], refs[18 + rd.n:18 + 2 * rd.n], refs[24 + 2 * rd.n:])
        step_id = pl.program_id(0) * 2 + pl.program_id(1)
        _ride_start(copies, step_id == 0)
        bones_m = bones_ref[...]
        bmask = bones_m.astype(F32)
        gwv = gw_ref[...]

        def head(n, acc):
            rows = pl.ds(pl.multiple_of(n * CHUNK, CHUNK), CHUNK)
            o = os_ref[rows, :]
            hg = g_ref[rows, :]
            dyv = dy_ref[rows, :]
            sg = _sigmoid(hg)
            r = lax.rsqrt(_dot_precise(o * o, bones_m, 2) * (1.0 / HEAD_DIM) + EPS)
            nrm = o * r * gwv
            dn = dyv * (hg * sg)
            dg_ref[rows, :] = (dyv * nrm * (sg * (1.0 + hg * (1.0 - sg)))).astype(BF16)
            g = dn * gwv
            mean_go = _dot_precise(g * o, bones_m, 2) * (1.0 / HEAD_DIM)
            do_sc[rows, :] = r * (g - o * (r * r) * mean_go)
            return acc + jnp.sum(dn * o * r, axis=0, keepdims=True)

        dgw_ref[0] = lax.fori_loop(0, nc, head, jnp.zeros((1, LANES), F32))
        dq_sc[...] = jnp.zeros_like(dq_sc)
        dv_sc[...] = jnp.zeros_like(dv_sc)

        for anti in (False, True):
            z_ref = zb_ref if anti else zf_ref
            dz_ref = dzb_ref if anti else dzf_ref
            lbv = lb_ref[1:2] if anti else lb_ref[0:1]
            trim = tri_ref[1] if anti else tri_ref[0]
            trim_r = tri_ref[0] if anti else tri_ref[1]

            def load(cn, z_ref=z_ref, lbv=lbv, trim=trim):
                rows = pl.ds(pl.multiple_of(cn * CHUNK, CHUNK), CHUNK)
                q = q_ref[rows, :]
                v = v_ref[rows, :]
                sig, f, logf, k = _gates(z_ref[rows, :], lbv)
                b = _running_sum(trim, logf)
                return rows, q, v, sig, f, k, b

            st_cur[...] = jnp.zeros_like(st_cur)

            def sweep(n, carry, anti=anti, load=load):
                cn = (nc - 1 - n) if anti else n
                _, q, v, _, _, k, b = load(cn)
                st_sc[cn] = st_cur[...]
                b_last = b[0:1] if anti else b[CHUNK - 1:CHUNK]
                kt = k * jnp.exp(b_last - b)
                st_cur[...] = st_cur[...] * jnp.exp(b_last) + _dot_tn(v.astype(BF16), kt.astype(BF16)) * bmask
                return carry

            lax.fori_loop(0, nc, sweep, 0)

            rt_cur[...] = jnp.zeros_like(rt_cur)

            def back(n, dlb, anti=anti, load=load, lbv=lbv, trim_r=trim_r, dz_ref=dz_ref):
                cn = n if anti else (nc - 1 - n)
                rows, q, v, sig, f, k, b = load(cn)
                do = do_sc[rows, :]
                dq, dk, dv, rt_new, db_last = _chunk_bwd(q, k, v, b, do, st_sc[cn], rt_cur[...], bones_m,
                                                         bmask, anti)
                rt_cur[...] = rt_new
                dq_sc[rows, :] += dq
                dv_sc[rows, :] += dv
                dlogf = _running_sum(trim_r, q * dq - k * dk) + db_last
                dfl = jnp.where(f > F_MIN, dlogf / f, 0.0)
                dz_ref[rows, :] = ((dfl - dk) * (1.0 - lbv) * sig * (1.0 - sig)).astype(BF16)
                return dlb + jnp.sum((dfl - dk) * (1.0 - sig), axis=0, keepdims=True)

            dlb = lax.fori_loop(0, nc, back, jnp.zeros((1, LANES), F32))
            side = 1 if anti else 0
            dlb_ref[0, side:side + 1, :] = dlb

        dq_ref[...] = dq_sc[...].astype(BF16)
        dv_ref[...] = dv_sc[...].astype(BF16)
        _ride_wait(copies, step_id == nb * 2 - 1)

    def col(c):
        return pl.BlockSpec((s, LANES), lambda b, p, c=c: (b, c + p))

    sl = pl.BlockSpec((s, LANES), lambda b, p: (b, p))
    out_t = jax.ShapeDtypeStruct((t, D_HGRN), BF16)
    return pl.pallas_call(
        body, name="hgrn_bwd", grid=(nb, 2),
        in_specs=[col(COL_HQ), col(COL_ZFW), col(COL_ZBW), col(COL_HI), col(COL_HG),
                  pl.BlockSpec((2, LANES), lambda b, p: (0, p)),
                  pl.BlockSpec((1, LANES), lambda b, p: (0, 0)),
                  sl, sl,
                  pl.BlockSpec((2, CHUNK, CHUNK), lambda b, p: (0, 0, 0)),
                  pl.BlockSpec((LANES, LANES), lambda b, p: (0, 0))] + rd.in_specs,
        out_specs=[sl, sl, sl, sl, sl,
                   pl.BlockSpec((1, 1, LANES), lambda b, p: (b, 0, p)),
                   pl.BlockSpec((1, 2, LANES), lambda b, p: (b, 0, p))] + rd.out_specs,
        out_shape=[out_t, out_t, out_t, out_t, out_t,
                   jax.ShapeDtypeStruct((nb, 1, D_HGRN), F32),
                   jax.ShapeDtypeStruct((nb, 2, D_HGRN), F32)] + rd.out_shape,
        scratch_shapes=[pltpu.VMEM((s, LANES), F32), pltpu.VMEM((s, LANES), F32), pltpu.VMEM((s, LANES), F32),
                        pltpu.VMEM((nc, LANES, LANES), F32), pltpu.VMEM((LANES, LANES), F32),
                        pltpu.VMEM((LANES, LANES), F32)] + rd.scratch,
        compiler_params=_cparams(("arbitrary", "arbitrary")),
    )(proj, proj, proj, proj, proj, lb, gw, osum, dy, tri, bones, *rd.srcs)


def _lower_bounds(logits):
    def body(lg_ref, lb_ref):
        rows = [lg_ref[l:l + 1, :] for l in range(DEPTH)]
        m = functools.reduce(jnp.maximum, rows)
        ex = [jnp.exp(r - m) for r in rows]
        den = functools.reduce(jnp.add, ex)
        run = jnp.zeros_like(m)
        for l in range(DEPTH):
            if l > 0:
                run = run + ex[l] / den
            lb_ref[l:l + 1, :] = run

    return pl.pallas_call(body, name="lower_bounds", out_shape=jax.ShapeDtypeStruct(logits.shape, F32))(logits)


def _lower_bounds_bwd(logits, dlb):
    def body(lg_ref, dlb_ref, dlg_ref):
        rows = [lg_ref[l:l + 1, :] for l in range(DEPTH)]
        m = functools.reduce(jnp.maximum, rows)
        ex = [jnp.exp(r - m) for r in rows]
        den = functools.reduce(jnp.add, ex)
        sm = [e / den for e in ex]
        dsm = [jnp.zeros_like(m) for _ in range(DEPTH)]
        for i in range(1, DEPTH):
            for l in range(i, DEPTH):
                dsm[i] = dsm[i] + dlb_ref[l:l + 1, :]
        dot = functools.reduce(jnp.add, [sm[i] * dsm[i] for i in range(DEPTH)])
        for i in range(DEPTH):
            dlg_ref[i:i + 1, :] = sm[i] * (dsm[i] - dot)

    return pl.pallas_call(body, name="lower_bounds_bwd", out_shape=jax.ShapeDtypeStruct(logits.shape, F32))(logits, dlb)


CONV_ROWS = 128


def _conv_core(a, bg, dww, dwb, lnw, lnb, upad_ref, s):
    sb = _sigmoid(bg)
    u = a * sb
    upad_ref[0:16, :] = jnp.zeros((16, D_CONV), F32)
    upad_ref[16:16 + s, :] = u
    upad_ref[16 + s:32 + s, :] = jnp.zeros((16, D_CONV), F32)
    rows = min(s, CONV_ROWS)
    pieces = []
    for r0 in range(0, s, rows):
        acc = None
        for j in range(CONV_W):
            term = upad_ref[r0 + 1 + j:r0 + 1 + j + rows, :] * dww[j:j + 1, :]
            acc = term if acc is None else acc + term
        pieces.append(acc)
    c = jnp.concatenate(pieces, axis=0) + dwb
    mu = jnp.mean(c, axis=-1, keepdims=True)
    xc = c - mu
    rstd = lax.rsqrt(jnp.mean(xc * xc, axis=-1, keepdims=True) + LN_EPS)
    nh = xc * rstd
    l = nh * lnw + lnb
    sl = _sigmoid(l)
    return sb, nh, rstd, l, sl


def _conv_fwd(proj, dww, dwb, lnw, lnb, pww, pwb, nb, s):
    t = proj.shape[0]
    assert s % min(s, CONV_ROWS) == 0

    def body(a_ref, b_ref, dww_ref, dwb_ref, lnw_ref, lnb_ref, pww_ref, pwb_ref, y_ref, upad_ref):
        _, _, _, l, sl = _conv_core(a_ref[...], b_ref[...], dww_ref[...], dwb_ref[...], lnw_ref[...],
                                    lnb_ref[...], upad_ref, s)
        y_ref[...] = _dot((l * sl).astype(BF16), pww_ref[...]) + pwb_ref[...]

    vec = pl.BlockSpec((1, D_CONV), lambda b: (0, 0))
    return pl.pallas_call(
        body, name="conv_fwd", grid=(nb,),
        in_specs=[pl.BlockSpec((s, D_CONV), lambda b: (b, COL_CA)),
                  pl.BlockSpec((s, D_CONV), lambda b: (b, COL_CB)),
                  pl.BlockSpec((32, D_CONV), lambda b: (0, 0)), vec, vec, vec,
                  pl.BlockSpec((D_CONV, D_CONV), lambda b: (0, 0)), vec],
        out_specs=pl.BlockSpec((s, D_CONV), lambda b: (b, 0)),
        out_shape=jax.ShapeDtypeStruct((t, D_CONV), F32),
        scratch_shapes=[pltpu.VMEM((s + 32, D_CONV), F32)],
        compiler_params=_cparams(("parallel",)),
    )(proj, proj, dww, dwb, lnw, lnb, pww, pwb)


def _conv_bwd(proj, dy, dww, dwb, lnw, lnb, pww, nb, s):
    t = proj.shape[0]

    def body(a_ref, b_ref, dy_ref, dww_ref, dwb_ref, lnw_ref, lnb_ref, pww_ref,
             dab_ref, ddww_ref, ddwb_ref, dlnw_ref, dlnb_ref, dpww_ref, dpwb_ref, upad_ref, dcpad_ref):
        a = a_ref[...]
        dww = dww_ref[...]
        sb, nh, rstd, l, sl = _conv_core(a, b_ref[...], dww, dwb_ref[...], lnw_ref[...], lnb_ref[...],
                                         upad_ref, s)
        dyv = dy_ref[...]
        dyb = dyv.astype(BF16)
        ds = _dot_nt(dyb, pww_ref[...])
        dl = ds * (sl * (1.0 + l * (1.0 - sl)))
        dn = dl * lnw_ref[...]
        dc = rstd * (dn - jnp.mean(dn, axis=-1, keepdims=True)
                     - nh * jnp.mean(dn * nh, axis=-1, keepdims=True))

        @pl.when(pl.program_id(0) == 0)
        def _():
            for r in (ddww_ref, ddwb_ref, dlnw_ref, dlnb_ref, dpww_ref, dpwb_ref):
                r[...] = jnp.zeros_like(r)

        dpww_ref[...] += _dot_tn((l * sl).astype(BF16), dyb)
        dpwb_ref[...] += jnp.sum(dyv, axis=0, keepdims=True)
        dlnw_ref[...] += jnp.sum(dl * nh, axis=0, keepdims=True)
        dlnb_ref[...] += jnp.sum(dl, axis=0, keepdims=True)
        ddwb_ref[...] += jnp.sum(dc, axis=0, keepdims=True)

        dcpad_ref[0:16, :] = jnp.zeros((16, D_CONV), F32)
        dcpad_ref[16:16 + s, :] = dc
        dcpad_ref[16 + s:32 + s, :] = jnp.zeros((16, D_CONV), F32)
        rows = min(s, CONV_ROWS)
        r8 = lax.broadcasted_iota(jnp.int32, (32, D_CONV), 0)
        ddww = jnp.zeros((32, D_CONV), F32)
        pieces = []
        for r0 in range(0, s, rows):
            acc = None
            dcr = dcpad_ref[16 + r0:16 + r0 + rows, :]
            for j in range(CONV_W):
                term = dcpad_ref[r0 + 31 - j:r0 + 31 - j + rows, :] * dww[j:j + 1, :]
                acc = term if acc is None else acc + term
                wj = jnp.sum(dcr * upad_ref[r0 + 1 + j:r0 + 1 + j + rows, :], axis=0, keepdims=True)
                ddww = ddww + jnp.where(r8 == j, wj, 0.0)
            pieces.append(acc)
        du = jnp.concatenate(pieces, axis=0)
        ddww_ref[...] += ddww
        dab_ref[:, 0:D_CONV] = (du * sb).astype(BF16)
        dab_ref[:, D_CONV:2 * D_CONV] = (du * a * sb * (1.0 - sb)).astype(BF16)

    vec = pl.BlockSpec((1, D_CONV), lambda b: (0, 0))
    mat = pl.BlockSpec((D_CONV, D_CONV), lambda b: (0, 0))
    w32 = pl.BlockSpec((32, D_CONV), lambda b: (0, 0))
    vshape = jax.ShapeDtypeStruct((1, D_CONV), F32)
    return pl.pallas_call(
        body, name="conv_bwd", grid=(nb,),
        in_specs=[pl.BlockSpec((s, D_CONV), lambda b: (b, COL_CA)),
                  pl.BlockSpec((s, D_CONV), lambda b: (b, COL_CB)),
                  pl.BlockSpec((s, D_CONV), lambda b: (b, 0)),
                  w32, vec, vec, vec, mat],
        out_specs=[pl.BlockSpec((s, 2 * D_CONV), lambda b: (b, 0)), w32, vec, vec, vec, mat, vec],
        out_shape=[jax.ShapeDtypeStruct((t, 2 * D_CONV), BF16),
                   jax.ShapeDtypeStruct((32, D_CONV), F32), vshape, vshape, vshape,
                   jax.ShapeDtypeStruct((D_CONV, D_CONV), F32), vshape],
        scratch_shapes=[pltpu.VMEM((s + 32, D_CONV), F32), pltpu.VMEM((s + 32, D_CONV), F32)],
        compiler_params=_cparams(("arbitrary",)),
    )(proj, proj, dy, dww, dwb, lnw, lnb, pww)


def _mix_out(o_attn, y_hgrn, y_conv, x, aw, cw, w_out, tm):
    t = x.shape[0]

    def body(o_ref, h_ref, c_ref, x_ref, aw_ref, cw_ref, w_ref, mixed_ref, x1_ref):
        o = o_ref[...]
        a = o * lax.rsqrt(jnp.mean(o * o, axis=-1, keepdims=True) + EPS) * aw_ref[...]
        yc = c_ref[...]
        c = yc * lax.rsqrt(jnp.mean(yc * yc, axis=-1, keepdims=True) + EPS) * cw_ref[...]
        ab, hb, cb = a.astype(BF16), h_ref[...].astype(BF16), c.astype(BF16)
        mixed_ref[:, 0:512] = ab
        mixed_ref[:, 512:768] = hb
        mixed_ref[:, 768:1024] = cb
        x1_ref[...] = (x_ref[...] + _dot(ab, w_ref[0:512, :]) + _dot(hb, w_ref[512:768, :])
                       + _dot(cb, w_ref[768:1024, :]))

    def tok(w):
        return pl.BlockSpec((tm, w), lambda i: (i, 0))

    return pl.pallas_call(
        body, name="mix_out", grid=(t // tm,),
        in_specs=[tok(512), tok(256), tok(256), tok(D_MODEL),
                  pl.BlockSpec((1, 512), lambda i: (0, 0)), pl.BlockSpec((1, 256), lambda i: (0, 0)),
                  pl.BlockSpec((D_MODEL, D_MODEL), lambda i: (0, 0))],
        out_specs=[tok(D_MODEL), tok(D_MODEL)],
        out_shape=[jax.ShapeDtypeStruct((t, D_MODEL), BF16), jax.ShapeDtypeStruct((t, D_MODEL), F32)],
        compiler_params=_cparams(("parallel",)),
    )(o_attn, y_hgrn, y_conv, x, aw, cw, w_out)


def _mix_out_bwd(dx1, w_out, o_attn, y_conv, aw, cw, tm):
    t = dx1.shape[0]

    def body(dx_ref, w_ref, o_ref, c_ref, aw_ref, cw_ref, do_ref, dh_ref, dc_ref, daw_ref, dcw_ref):
        dm = _dot_nt(dx_ref[...].astype(BF16), w_ref[...])
        do, daw = _rms_bwd(dm[:, 0:512], o_ref[...], aw_ref[...])
        dc, dcw = _rms_bwd(dm[:, 768:1024], c_ref[...], cw_ref[...])
        do_ref[...] = do
        dh_ref[...] = dm[:, 512:768]
        dc_ref[...] = dc

        @pl.when(pl.program_id(0) == 0)
        def _():
            daw_ref[...] = jnp.zeros_like(daw_ref)
            dcw_ref[...] = jnp.zeros_like(dcw_ref)

        daw_ref[...] += jnp.sum(daw, axis=0, keepdims=True)
        dcw_ref[...] += jnp.sum(dcw, axis=0, keepdims=True)

    def tok(w):
        return pl.BlockSpec((tm, w), lambda i: (i, 0))

    v512 = pl.BlockSpec((1, 512), lambda i: (0, 0))
    v256 = pl.BlockSpec((1, 256), lambda i: (0, 0))
    return pl.pallas_call(
        body, name="mix_out_bwd", grid=(t // tm,),
        in_specs=[tok(D_MODEL), pl.BlockSpec((D_MODEL, D_MODEL), lambda i: (0, 0)), tok(512), tok(256),
                  v512, v256],
        out_specs=[tok(512), tok(256), tok(256), v512, v256],
        out_shape=[jax.ShapeDtypeStruct((t, 512), F32), jax.ShapeDtypeStruct((t, 256), F32),
                   jax.ShapeDtypeStruct((t, 256), F32), jax.ShapeDtypeStruct((1, 512), F32),
                   jax.ShapeDtypeStruct((1, 256), F32)],
        compiler_params=_cparams(("arbitrary",)),
    )(dx1, w_out, o_attn, y_conv, aw, cw)


FF_TILE = 1408


def _ffn_fwd(x1, fw, wg, wu, wd, tm):
    t = x1.shape[0]
    nf = D_FF // FF_TILE

    def body(x_ref, fw_ref, wg_ref, wu_ref, wd_ref, h_ref, g_ref, u_ref, a_ref, x2_ref, acc_ref):
        j = pl.program_id(1)

        @pl.when(j == 0)
        def _():
            xv = x_ref[...]
            r = lax.rsqrt(jnp.mean(xv * xv, axis=-1, keepdims=True) + EPS)
            h_ref[...] = (xv * r * fw_ref[...]).astype(BF16)
            acc_ref[...] = xv

        h = h_ref[...]
        g = _dot(h, wg_ref[...])
        u = _dot(h, wu_ref[...])
        a = (g * _sigmoid(g) * u).astype(BF16)
        g_ref[...] = g.astype(BF16)
        u_ref[...] = u.astype(BF16)
        a_ref[...] = a
        acc_ref[...] += _dot(a, wd_ref[...])

        @pl.when(j == nf - 1)
        def _():
            x2_ref[...] = acc_ref[...]

    tok = pl.BlockSpec((tm, D_MODEL), lambda i, j: (i, 0))
    ffb = pl.BlockSpec((tm, FF_TILE), lambda i, j: (i, j))
    ffs = jax.ShapeDtypeStruct((t, D_FF), BF16)
    return pl.pallas_call(
        body, name="ffn_fwd", grid=(t // tm, nf),
        in_specs=[tok, pl.BlockSpec((1, D_MODEL), lambda i, j: (0, 0)),
                  pl.BlockSpec((D_MODEL, FF_TILE), lambda i, j: (0, j)),
                  pl.BlockSpec((D_MODEL, FF_TILE), lambda i, j: (0, j)),
                  pl.BlockSpec((FF_TILE, D_MODEL), lambda i, j: (j, 0))],
        out_specs=[tok, ffb, ffb, ffb, tok],
        out_shape=[jax.ShapeDtypeStruct((t, D_MODEL), BF16), ffs, ffs, ffs,
                   jax.ShapeDtypeStruct((t, D_MODEL), F32)],
        scratch_shapes=[pltpu.VMEM((tm, D_MODEL), F32)],
        compiler_params=_cparams(("parallel", "arbitrary")),
    )(x1, fw, wg, wu, wd)


def _ffn_bwd(dx2, g, u, wg, wu, wd, x1, fw, tm):
    t = dx2.shape[0]
    nf = D_FF // FF_TILE

    def body(dx_ref, g_ref, u_ref, wg_ref, wu_ref, wd_ref, x_ref, fw_ref,
             dg_ref, du_ref, dx1_ref, dfw_ref, acc_ref):
        i = pl.program_id(0)
        j = pl.program_id(1)
        da = _dot_nt(dx_ref[...].astype(BF16), wd_ref[...])
        gv = g_ref[...].astype(F32)
        uv = u_ref[...].astype(F32)
        sg = _sigmoid(gv)
        dg = (da * uv * (sg * (1.0 + gv * (1.0 - sg)))).astype(BF16)
        du = (da * gv * sg).astype(BF16)
        dg_ref[...] = dg
        du_ref[...] = du
        dh = _dot_nt(dg, wg_ref[...]) + _dot_nt(du, wu_ref[...])

        @pl.when(j == 0)
        def _():
            acc_ref[...] = dh

        @pl.when(j > 0)
        def _():
            acc_ref[...] += dh

        @pl.when((i == 0) & (j == 0))
        def _():
            dfw_ref[...] = jnp.zeros_like(dfw_ref)

        @pl.when(j == nf - 1)
        def _():
            dx, dfw = _rms_bwd(acc_ref[...], x_ref[...], fw_ref[...])
            dx1_ref[...] = dx_ref[...] + dx
            dfw_ref[...] += jnp.sum(dfw, axis=0, keepdims=True)

    tok = pl.BlockSpec((tm, D_MODEL), lambda i, j: (i, 0))
    ffb = pl.BlockSpec((tm, FF_TILE), lambda i, j: (i, j))
    ffs = jax.ShapeDtypeStruct((t, D_FF), BF16)
    vec = pl.BlockSpec((1, D_MODEL), lambda i, j: (0, 0))
    return pl.pallas_call(
        body, name="ffn_bwd", grid=(t // tm, nf),
        in_specs=[tok, ffb, ffb,
                  pl.BlockSpec((D_MODEL, FF_TILE), lambda i, j: (0, j)),
                  pl.BlockSpec((D_MODEL, FF_TILE), lambda i, j: (0, j)),
                  pl.BlockSpec((FF_TILE, D_MODEL), lambda i, j: (j, 0)),
                  tok, vec],
        out_specs=[ffb, ffb, tok, vec],
        out_shape=[ffs, ffs, jax.ShapeDtypeStruct((t, D_MODEL), F32), jax.ShapeDtypeStruct((1, D_MODEL), F32)],
        scratch_shapes=[pltpu.VMEM((tm, D_MODEL), F32)],
        compiler_params=_cparams(("arbitrary", "arbitrary")),
    )(dx2, g, u, wg, wu, wd, x1, fw)


def _loss_grad(y, target, tm):
    t, d = y.shape

    def body(y_ref, t_ref, dy_ref, loss_ref):
        err = y_ref[...] - t_ref[...]
        dy_ref[...] = err * (1.0 / d)

        @pl.when(pl.program_id(0) == 0)
        def _():
            loss_ref[...] = jnp.zeros_like(loss_ref)

        part = jnp.sum(jnp.sum(err * err, axis=-1, keepdims=True), axis=0, keepdims=True)
        loss_ref[...] += part * (0.5 / d)

    tok = pl.BlockSpec((tm, d), lambda i: (i, 0))
    return pl.pallas_call(
        body, name="loss_grad", grid=(t // tm,),
        in_specs=[tok, tok],
        out_specs=[tok, pl.BlockSpec((1, 1), lambda i: (0, 0))],
        out_shape=[jax.ShapeDtypeStruct((t, d), F32), jax.ShapeDtypeStruct((1, 1), F32)],
        compiler_params=_cparams(("arbitrary",)),
    )(y, target)


def _tile(v, reps):
    return jnp.tile(v.reshape(1, -1), (1, reps))


class _LocalPlan:
    def __init__(self, wb):
        self.wb = wb

    def weights(self, l):
        return {n: self.wb[n][l] for n in BIG_AXIS}

    def fwd_ride(self, l):
        return None

    def fwd_done(self, l, outs):
        pass

    def bwd_ride(self, l, grads):
        return None

    def bwd_done(self, l, outs):
        pass


def _local_step(x, target, p, plan):
    nb, s, d = x.shape
    t = nb * s
    tm = min(512, s)
    tq = min(512, s)
    xf = x.reshape(t, d)
    cosq, sinq = _rope_tables(s)
    ones512 = _block_ones(512, HEAD_DIM)
    lbs = _lower_bounds(p["hgrn_lb_logits"].reshape(DEPTH, 2 * D_HGRN)).reshape(DEPTH, 2, D_HGRN)

    saved = []
    cur = xf
    wb = [None] * DEPTH
    for l in range(DEPTH):
        wb[l] = plan.weights(l)
        qw = _tile(p["q_norm_w"][l], N_HEADS)
        kw = _tile(p["k_norm_w"][l], N_KV)
        gw = _tile(p["hgrn_gnorm_w"][l], 2)
        dww = jnp.pad(p["conv_dw_w"][l], ((0, 1), (0, 0)))
        pww = p["conv_pw_w"][l].astype(BF16)
        h0, proj = _rms_proj(cur, _row(p["mix_norm_w"][l]), wb[l]["w_in"], tm)
        qr, kd, vd, kdt, vdt = _qkv_prep(proj, cosq, sinq, qw, kw, ones512, s, tm)
        o_attn, lse = _attn_fwd(qr, kd, vdt, nb, s, tq)
        y_hgrn, osum, *rode = _hgrn_fwd(proj, lbs[l], gw, nb, s, plan.fwd_ride(l))
        plan.fwd_done(l, rode)
        y_conv = _conv_fwd(proj, dww, _row(p["conv_dw_b"][l]), _row(p["conv_ln_w"][l]),
                           _row(p["conv_ln_b"][l]), pww, _row(p["conv_pw_b"][l]), nb, s)
        mixed, x1 = _mix_out(o_attn, y_hgrn, y_conv, cur, _row(p["attn_out_norm_w"][l]),
                             _row(p["conv_out_norm_w"][l]), wb[l]["w_out"], tm)
        hf, g, u, a, x2 = _ffn_fwd(x1, _row(p["ffn_norm_w"][l]), wb[l]["w_gate"], wb[l]["w_up"],
                                   wb[l]["w_down"], tm)
        saved.append(dict(x=cur, h0=h0, proj=proj, qr=qr, kd=kd, vd=vd, kdt=kdt, o_attn=o_attn, lse=lse,
                          osum=osum, y_conv=y_conv, mixed=mixed, x1=x1, hf=hf, g=g, u=u, a=a,
                          qw=qw, kw=kw, gw=gw, dww=dww, pww=pww))
        cur = x2

    dcur, loss = _loss_grad(cur, target.reshape(t, d), tm)

    grads = {k: [None] * DEPTH for k in WEIGHTS}
    dlb = [None] * DEPTH
    for l in reversed(range(DEPTH)):
        sv = saved[l]
        dg, du, dx1, dfw = _ffn_bwd(dcur, sv["g"], sv["u"], wb[l]["w_gate"], wb[l]["w_up"], wb[l]["w_down"],
                                    sv["x1"], _row(p["ffn_norm_w"][l]), tm)
        grads["ffn_norm_w"][l] = dfw[0]
        grads["w_gate"][l] = _mm_tn(sv["hf"], dg, FF_TILE, "dw_gate", tm)
        grads["w_up"][l] = _mm_tn(sv["hf"], du, FF_TILE, "dw_up", tm)
        grads["w_down"][l] = _mm_tn(sv["a"], dcur, 512, "dw_down", tm)
        do_attn, dy_hgrn, dy_conv, daw, dcw = _mix_out_bwd(
            dx1, wb[l]["w_out"], sv["o_attn"], sv["y_conv"], _row(p["attn_out_norm_w"][l]),
            _row(p["conv_out_norm_w"][l]), tm)
        grads["attn_out_norm_w"][l] = daw[0]
        grads["conv_out_norm_w"][l] = dcw[0]
        grads["w_out"][l] = _mm_tn(sv["mixed"], dx1, D_MODEL, "dw_out", tm)
        dq, dkd, dvd = _attn_bwd(sv["qr"], sv["kd"], sv["vd"], sv["kdt"], sv["o_attn"], sv["lse"], do_attn,
                                 nb, s, tq)
        dqkv, dqw, dkw = _qkv_bwd(sv["proj"], dq, dkd, dvd, cosq, sinq, sv["qw"], sv["kw"], ones512, s, tm)
        grads["q_norm_w"][l] = dqw.reshape(N_HEADS, HEAD_DIM).sum(0)
        grads["k_norm_w"][l] = dkw.reshape(N_KV, HEAD_DIM).sum(0)
        dhq, dzf, dzb, dhi, dhg, dgw, dlb_l, *rode = _hgrn_bwd(sv["proj"], lbs[l], sv["gw"], sv["osum"], dy_hgrn,
                                                               nb, s, plan.bwd_ride(l, grads))
        plan.bwd_done(l, rode)
        grads["hgrn_gnorm_w"][l] = dgw.reshape(nb * D_HGRN // HEAD_DIM, HEAD_DIM).sum(0)
        dlb[l] = dlb_l.sum(0)
        dab, ddww, ddwb, dlnw, dlnb, dpww, dpwb = _conv_bwd(
            sv["proj"], dy_conv, sv["dww"], _row(p["conv_dw_b"][l]), _row(p["conv_ln_w"][l]),
            _row(p["conv_ln_b"][l]), sv["pww"], nb, s)
        grads["conv_dw_w"][l] = ddww[:CONV_W]
        grads["conv_dw_b"][l] = ddwb[0]
        grads["conv_ln_w"][l] = dlnw[0]
        grads["conv_ln_b"][l] = dlnb[0]
        grads["conv_pw_w"][l] = dpww
        grads["conv_pw_b"][l] = dpwb[0]
        pieces = [dqkv, dhq, dzf, dzb, dhi, dhg, dab]
        grads["w_in"][l] = _dw_in(sv["h0"], pieces, tm)
        dcur, dnw = _proj_bwd(pieces, wb[l]["w_in"], sv["x"], _row(p["mix_norm_w"][l]), dx1, tm)
        grads["mix_norm_w"][l] = dnw[0]

    dlog = _lower_bounds_bwd(p["hgrn_lb_logits"].reshape(DEPTH, 2 * D_HGRN),
                             jnp.stack(dlb).reshape(DEPTH, 2 * D_HGRN))
    out = {k: (v if k in BIG_AXIS else jnp.stack(v)) for k, v in grads.items() if k != "hgrn_lb_logits"}
    out["hgrn_lb_logits"] = dlog.reshape(DEPTH, 2, D_HGRN)
    return loss, dcur.reshape(nb, s, d), out


BIG_AXIS = {"w_in": 2, "w_out": 1, "w_gate": 2, "w_up": 2, "w_down": 1}
SMALL_SHARD_AXIS = {"hgrn_lb_logits": 2, "conv_dw_w": 2, "conv_pw_w": 1}
WEIGHTS = ("mix_norm_w", "w_in", "q_norm_w", "k_norm_w", "hgrn_lb_logits", "hgrn_gnorm_w", "conv_dw_w",
           "conv_dw_b", "conv_ln_w", "conv_ln_b", "conv_pw_w", "conv_pw_b", "attn_out_norm_w",
           "conv_out_norm_w", "w_out", "ffn_norm_w", "w_gate", "w_up", "w_down")
SMALL = tuple(n for n in WEIGHTS if n not in BIG_AXIS)


def _my_index():
    return 4 * lax.axis_index("x") + 2 * lax.axis_index("y") + lax.axis_index("c")


class _RidePlan:
    def __init__(self, srcs, gather):
        self.srcs = list(srcs)
        self.gather = gather
        self.n = len(self.srcs)
        any_spec = pl.BlockSpec(memory_space=pl.ANY)
        self.in_specs = [any_spec] * self.n
        self.out_specs = [any_spec] * self.n
        self.out_shape = [jax.ShapeDtypeStruct(((N_DEV,) + s.shape) if gather else s.shape, s.dtype)
                          for s in self.srcs]
        npeer = N_DEV - 1
        self.scratch = [pltpu.SemaphoreType.DMA((self.n * npeer,)), pltpu.SemaphoreType.DMA((self.n * npeer,)),
                        pltpu.SemaphoreType.DMA((self.n,))] if self.n else []

    def copies(self, src_refs, out_refs, sems):
        if not self.n:
            return [], [], []
        send_sems, recv_sems, local_sems = sems
        npeer = N_DEV - 1
        x, y, c = lax.axis_index("x"), lax.axis_index("y"), lax.axis_index("c")
        me = 4 * x + 2 * y + c
        locals_, sends, recvs = [], [], []
        for a in range(self.n):
            src_ref, out_ref = src_refs[a], out_refs[a]

            def rows_for(j, src_ref=src_ref):
                return src_ref if self.gather else src_ref.at[j]

            locals_.append(pltpu.make_async_copy(rows_for(me), out_ref.at[me], local_sems.at[a]))
            for k in range(1, N_DEV):
                px = (1 - x) if (k & 4) else x
                py = (1 - y) if (k & 2) else y
                pc = (1 - c) if (k & 1) else c
                pidx = 4 * px + 2 * py + pc
                common = dict(send_sem=send_sems.at[a * npeer + k - 1], recv_sem=recv_sems.at[a * npeer + k - 1],
                              device_id=(px, py, pc), device_id_type=pl.DeviceIdType.MESH)
                sends.append(pltpu.make_async_remote_copy(src_ref=rows_for(pidx), dst_ref=out_ref.at[me], **common))
                recvs.append(pltpu.make_async_remote_copy(src_ref=rows_for(pidx), dst_ref=out_ref.at[pidx],
                                                          **common))
        return locals_, sends, recvs


def _ride_plan(ride):
    return _RidePlan(*ride) if ride else _RidePlan([], True)


def _ride_start(copies, when=None):
    locals_, sends, _ = copies

    def go():
        for cp in locals_ + sends:
            cp.start()

    if locals_:
        go() if when is None else pl.when(when)(go)


def _ride_wait(copies, when=None):
    locals_, sends, recvs = copies

    def go():
        for cp in recvs:
            cp.wait_recv()
        for cp in sends:
            cp.wait_send()
        for cp in locals_:
            cp.wait()

    if locals_:
        go() if when is None else pl.when(when)(go)


def _exchange(srcs, gather, name):
    rd = _RidePlan(srcs, gather)

    def body(*refs):
        copies = rd.copies(refs[:rd.n], refs[rd.n:2 * rd.n], refs[2 * rd.n:])
        _ride_start(copies)
        _ride_wait(copies)

    return pl.pallas_call(body, name=name, in_specs=rd.in_specs, out_specs=rd.out_specs,
                          out_shape=rd.out_shape, scratch_shapes=rd.scratch)(*srcs)


def _lane_group(n):
    g = 1
    while (g * n) % LANES:
        g += 1
    return g


def _cols_to_natural(gathered, name):
    _, k, n = gathered.shape
    grp = _lane_group(n)
    place = jnp.stack([jnp.asarray(np.eye(n, grp * n, k=i * n), BF16) for i in range(grp)])

    def body(g_ref, p_ref, o_ref):
        acc = None
        for i in range(grp):
            part = _dot(g_ref[i], p_ref[i])
            acc = part if acc is None else acc + part
        o_ref[...] = acc.astype(BF16)

    return pl.pallas_call(
        body, name=name, grid=(N_DEV // grp,),
        in_specs=[pl.BlockSpec((grp, k, n), lambda j: (j, 0, 0)),
                  pl.BlockSpec((grp, n, grp * n), lambda j: (0, 0, 0))],
        out_specs=pl.BlockSpec((k, grp * n), lambda j: (0, j)),
        out_shape=jax.ShapeDtypeStruct((k, N_DEV * n), BF16),
        compiler_params=_cparams(("parallel",)),
    )(gathered, place)


def _natural_to_cols(dw, name):
    k, n8 = dw.shape
    n = n8 // N_DEV
    grp = _lane_group(n)
    pick = jnp.stack([jnp.asarray(np.eye(grp * n, n, k=-i * n), BF16) for i in range(grp)])

    def body(d_ref, p_ref, o_ref):
        xb = d_ref[...].astype(BF16)
        for i in range(grp):
            o_ref[i] = _dot(xb, p_ref[i]).astype(BF16)

    return pl.pallas_call(
        body, name=name, grid=(N_DEV // grp,),
        in_specs=[pl.BlockSpec((k, grp * n), lambda j: (0, j)),
                  pl.BlockSpec((grp, grp * n, n), lambda j: (0, 0, 0))],
        out_specs=pl.BlockSpec((grp, k, n), lambda j: (j, 0, 0)),
        out_shape=jax.ShapeDtypeStruct((N_DEV, k, n), BF16),
        compiler_params=_cparams(("parallel",)),
    )(dw, pick)


def _adamw_math(w, g, m, v):
    m = ADAM_B1 * m + (1.0 - ADAM_B1) * g
    v = ADAM_B2 * v + (1.0 - ADAM_B2) * (g * g)
    m_hat = m / (1.0 - ADAM_B1 ** ADAM_STEP)
    v_hat = v / (1.0 - ADAM_B2 ** ADAM_STEP)
    delta = -ADAM_LR * (m_hat / (jnp.sqrt(v_hat) + ADAM_EPS) + ADAM_WD * w)
    return delta, m, v


def _sum_adamw(parts, w, m, v, name):
    _, k, n = w.shape
    tk = k
    for cand in (256, 176, 128):
        if k % cand == 0:
            tk = cand
            break

    def body(*refs):
        p_refs = refs[:DEPTH]
        w_ref, m_ref, v_ref, g_ref, d_ref, mo_ref, vo_ref = refs[DEPTH:]
        for l in range(DEPTH):
            @pl.when(pl.program_id(0) == l)
            def _(p_ref=p_refs[l]):
                g = p_ref[0].astype(F32)
                for i in range(1, N_DEV):
                    g = g + p_ref[i].astype(F32)
                g_ref[...] = g
                d_ref[...], mo_ref[...], vo_ref[...] = _adamw_math(w_ref[...], g, m_ref[...], v_ref[...])

    row = pl.BlockSpec((None, tk, n), lambda l, i: (l, i, 0))
    shp = jax.ShapeDtypeStruct(w.shape, F32)
    return pl.pallas_call(
        body, name=name, grid=(DEPTH, k // tk),
        in_specs=[pl.BlockSpec((N_DEV, tk, n), lambda l, i: (0, i, 0))] * DEPTH + [row, row, row],
        out_specs=[row, row, row, row],
        out_shape=[shp, shp, shp, shp],
        compiler_params=_cparams(("parallel", "parallel")),
    )(*parts, w, m, v)


def _sum8(parts):
    r = parts.shape[1]

    def body(p_ref, g_ref):
        g = p_ref[0]
        for i in range(1, N_DEV):
            g = g + p_ref[i]
        g_ref[...] = g

    return pl.pallas_call(body, name="sum_small_grads", out_shape=jax.ShapeDtypeStruct((r, LANES), F32))(parts)


def _adamw(w, g, m, v):
    def body(w_ref, g_ref, m_ref, v_ref, d_ref, mo_ref, vo_ref):
        d_ref[...], mo_ref[...], vo_ref[...] = _adamw_math(w_ref[...], g_ref[...], m_ref[...], v_ref[...])

    shp = jax.ShapeDtypeStruct(w.shape, F32)
    return pl.pallas_call(body, name="adamw_small", out_shape=[shp, shp, shp])(w, g, m, v)


def _pack(arrays, dtype, row_multiple):
    flat = jnp.concatenate([a.reshape(-1).astype(dtype) for a in arrays])
    n = flat.shape[0]
    unit = row_multiple * LANES
    total = -(-n // unit) * unit
    return jnp.pad(flat, (0, total - n)).reshape(total // LANES, LANES)


def _unpack(flat2d, shapes, lead=()):
    flat = flat2d.reshape(lead + (-1,))
    out, off = [], 0
    for shp in shapes:
        n = int(np.prod(shp))
        out.append(flat[..., off:off + n].reshape(lead + tuple(shp)))
        off += n
    return out


def _shard_to_rows(full, axis):
    shp = full.shape
    k = shp[axis] // N_DEV
    r = full.reshape(shp[:axis] + (N_DEV, k) + shp[axis + 1:])
    return jnp.moveaxis(r, axis, 0)


def _rows_to_full(rows, axis):
    r = jnp.moveaxis(rows, 0, axis)
    shp = r.shape
    return r.reshape(shp[:axis] + (shp[axis] * shp[axis + 1],) + shp[axis + 2:])


def kernel(x, mix_norm_w, w_in, q_norm_w, k_norm_w, hgrn_lb_logits, hgrn_gnorm_w, conv_dw_w, conv_dw_b, conv_ln_w, conv_ln_b, conv_pw_w, conv_pw_b, attn_out_norm_w, conv_out_norm_w, w_out, ffn_norm_w, w_gate, w_up, w_down, loss_target, m_mix_norm_w, m_w_in, m_q_norm_w, m_k_norm_w, m_hgrn_lb_logits, m_hgrn_gnorm_w, m_conv_dw_w, m_conv_dw_b, m_conv_ln_w, m_conv_ln_b, m_conv_pw_w, m_conv_pw_b, m_attn_out_norm_w, m_conv_out_norm_w, m_w_out, m_ffn_norm_w, m_w_gate, m_w_up, m_w_down, v_mix_norm_w, v_w_in, v_q_norm_w, v_k_norm_w, v_hgrn_lb_logits, v_hgrn_gnorm_w, v_conv_dw_w, v_conv_dw_b, v_conv_ln_w, v_conv_ln_b, v_conv_pw_w, v_conv_pw_b, v_attn_out_norm_w, v_conv_out_norm_w, v_w_out, v_ffn_norm_w, v_w_gate, v_w_up, v_w_down):
    w_loc = dict(zip(WEIGHTS, (mix_norm_w, w_in, q_norm_w, k_norm_w, hgrn_lb_logits, hgrn_gnorm_w, conv_dw_w,
                               conv_dw_b, conv_ln_w, conv_ln_b, conv_pw_w, conv_pw_b, attn_out_norm_w,
                               conv_out_norm_w, w_out, ffn_norm_w, w_gate, w_up, w_down)))
    m_loc = dict(zip(WEIGHTS, (m_mix_norm_w, m_w_in, m_q_norm_w, m_k_norm_w, m_hgrn_lb_logits, m_hgrn_gnorm_w,
                               m_conv_dw_w, m_conv_dw_b, m_conv_ln_w, m_conv_ln_b, m_conv_pw_w, m_conv_pw_b,
                               m_attn_out_norm_w, m_conv_out_norm_w, m_w_out, m_ffn_norm_w, m_w_gate, m_w_up,
                               m_w_down)))
    v_loc = dict(zip(WEIGHTS, (v_mix_norm_w, v_w_in, v_q_norm_w, v_k_norm_w, v_hgrn_lb_logits, v_hgrn_gnorm_w,
                               v_conv_dw_w, v_conv_dw_b, v_conv_ln_w, v_conv_ln_b, v_conv_pw_w, v_conv_pw_b,
                               v_attn_out_norm_w, v_conv_out_norm_w, v_w_out, v_ffn_norm_w, v_w_gate, v_w_up,
                               v_w_down)))
    me = _my_index()
    big = tuple(BIG_AXIS)
    sms = tuple(SMALL_SHARD_AXIS)

    sm_shapes = [w_loc[n].shape for n in sms]
    got_s = _exchange([_pack([w_loc[n] for n in sms], F32, 8)], True, "gather_small_params")[0]
    p_full = {n: w_loc[n] for n in SMALL if n not in SMALL_SHARD_AXIS}
    for n, a in zip(sms, _unpack(got_s, sm_shapes, (N_DEV,))):
        p_full[n] = _rows_to_full(a, SMALL_SHARD_AXIS[n])

    def natural(gathered):
        out = {}
        for n, a in zip(big, gathered):
            if BIG_AXIS[n] == 2:
                out[n] = _cols_to_natural(a, "relayout_" + n)
            else:
                out[n] = a.reshape(-1, a.shape[-1])
        return out

    def to_send(grads, l):
        out = []
        for n in big:
            gl = grads[n][l]
            if BIG_AXIS[n] == 2:
                out.append(_natural_to_cols(gl, "split_d" + n))
            else:
                out.append(gl.reshape(N_DEV, gl.shape[0] // N_DEV, gl.shape[1]).astype(BF16))
        return out

    class StepPlan:
        def __init__(self):
            self.w = [None] * DEPTH
            self.parts = [None] * DEPTH
            self.w[0] = natural(_exchange([w_loc[n][0].astype(BF16) for n in big], True, "gather_weights"))

        def weights(self, l):
            return self.w[l]

        def fwd_ride(self, l):
            if l + 1 < DEPTH:
                return [w_loc[n][l + 1].astype(BF16) for n in big], True
            return None

        def fwd_done(self, l, outs):
            if outs:
                self.w[l + 1] = natural(outs)

        def bwd_ride(self, l, grads):
            if l + 1 < DEPTH:
                return to_send(grads, l + 1), False
            return None

        def bwd_done(self, l, outs):
            if outs:
                self.parts[l + 1] = outs

    plan = StepPlan()
    loss_part, grad_x, g = _local_step(x, loss_target, p_full, plan)
    loss = lax.psum(loss_part[0, 0], MESH_AXES)

    plan.parts[0] = _exchange(to_send(g, 0), False, "exchange_weight_grads")
    big_out = {n: _sum_adamw([plan.parts[l][i] for l in range(DEPTH)], w_loc[n], m_loc[n], v_loc[n],
                             "sum_adamw_" + n) for i, n in enumerate(big)}

    small_shapes = [g[n].shape for n in SMALL]
    parts = _exchange([_pack([g[n] for n in SMALL], F32, 8)], True, "gather_small_grads")[0]
    g_small = dict(zip(SMALL, _unpack(_sum8(parts), small_shapes)))
    for n in sms:
        ax = SMALL_SHARD_AXIS[n]
        k = w_loc[n].shape[ax]
        g_small[n] = lax.dynamic_slice_in_dim(g_small[n], me * k, k, axis=ax)
    loc_shapes = [w_loc[n].shape for n in SMALL]
    packed = [_pack([d[n] for n in SMALL], F32, 8) for d in (w_loc, g_small, m_loc, v_loc)]
    res = _adamw(*packed)
    small_out = [g_small] + [dict(zip(SMALL, _unpack(r, loc_shapes))) for r in res]

    def pick(i, n):
        return big_out[n][i] if n in BIG_AXIS else small_out[i][n]

    return (loss, grad_x) + tuple(pick(i, n) for i in range(4) for n in WEIGHTS)
```

```python
import functools

import jax
import jax.numpy as jnp
import numpy as np
from jax import lax
from jax.experimental import pallas as pl
from jax.experimental.pallas import tpu as pltpu

F32 = jnp.float32
BF16 = jnp.bfloat16

D_MODEL = 1024
D_ATTN = 512
D_HGRN = 256
D_CONV = 256
HEAD_DIM = 64
N_HEADS = 8
N_KV = 2
GRID_W = 64
ROPE_THETA = 10000.0
CHUNK = 64
F_MIN = 1e-6
CONV_W = 31
CONV_PAD = 15
D_FF = 2816
D_PROJ = 2560
EPS = 1e-6
LN_EPS = 1e-5
DEPTH = 2
ADAM_LR = 0.001
ADAM_B1 = 0.9
ADAM_B2 = 0.999
ADAM_EPS = 1e-08
ADAM_WD = 0.01
ADAM_STEP = 10
N_DEV = 8
MESH_AXES = ("x", "y", "c")

COL_HQ, COL_ZFW, COL_ZBW, COL_HI, COL_HG = 6, 8, 10, 12, 14
COL_CA, COL_CB = 8, 9

LANES = 128
VMEM_LIMIT_MB = 56


def _cparams(dims=None):
    return pltpu.CompilerParams(dimension_semantics=dims, vmem_limit_bytes=VMEM_LIMIT_MB * 2 ** 20)


def _dot(a, b):
    return jnp.dot(a, b, preferred_element_type=F32)


def _dot_nt(a, b):
    return lax.dot_general(a, b, (((1,), (1,)), ((), ())), preferred_element_type=F32)


def _dot_tn(a, b):
    return lax.dot_general(a, b, (((0,), (0,)), ((), ())), preferred_element_type=F32)


def _split_bf16(x, parts):
    out = []
    r = x
    for _ in range(parts):
        p = r.astype(BF16)
        out.append(p)
        r = r - p.astype(F32)
    return out


def _dot_precise(x, m_bf16, parts=3):
    acc = None
    for p in _split_bf16(x, parts):
        t = _dot(p, m_bf16)
        acc = t if acc is None else acc + t
    return acc


def _block_ones(width, group):
    i = np.arange(width)
    return jnp.asarray((i[:, None] // group) == (i[None, :] // group), dtype=BF16)


def _sigmoid(x):
    return 1.0 / (1.0 + jnp.exp(-x))


def _rot(x):
    w = x.shape[1]
    lane = lax.broadcasted_iota(jnp.int32, x.shape, 1)
    first = (lane % 32) < 16
    return jnp.where(first, -pltpu.roll(x, w - 16, 1), pltpu.roll(x, 16, 1))


def _rope(x, cos, sin):
    return x * cos + _rot(x) * sin


def _rope_t(dy, cos, sin):
    return dy * cos - _rot(dy * sin)


def _row(v):
    return v.reshape(1, -1)


def _rms_proj(x, wn, w, tm):
    t, d = x.shape
    n = w.shape[1]

    def body(x_ref, wn_ref, w_ref, h_ref, y_ref):
        xv = x_ref[...]
        r = lax.rsqrt(jnp.mean(xv * xv, axis=-1, keepdims=True) + EPS)
        h = (xv * r * wn_ref[...]).astype(BF16)
        h_ref[...] = h
        y_ref[...] = _dot(h, w_ref[...])

    return pl.pallas_call(
        body, name="rms_proj", grid=(t // tm,),
        in_specs=[pl.BlockSpec((tm, d), lambda i: (i, 0)),
                  pl.BlockSpec((1, d), lambda i: (0, 0)),
                  pl.BlockSpec((d, n), lambda i: (0, 0))],
        out_specs=[pl.BlockSpec((tm, d), lambda i: (i, 0)),
                   pl.BlockSpec((tm, n), lambda i: (i, 0))],
        out_shape=[jax.ShapeDtypeStruct((t, d), BF16), jax.ShapeDtypeStruct((t, n), F32)],
        compiler_params=_cparams(("parallel",)),
    )(x, wn, w)


def _rms_bwd(dh, x, wn):
    r = lax.rsqrt(jnp.mean(x * x, axis=-1, keepdims=True) + EPS)
    g = dh * wn
    dx = r * (g - x * (r * r) * jnp.mean(g * x, axis=-1, keepdims=True))
    return dx, dh * x * r


def _proj_bwd(pieces, w, x, wn, dres, tm):
    t = x.shape[0]
    d = x.shape[1]
    n = w.shape[1]
    widths = [p.shape[1] for p in pieces]
    offs = [sum(widths[:i]) for i in range(len(widths))]
    assert sum(widths) == n
    npc = len(pieces)

    def body(*refs):
        p_refs = refs[:npc]
        w_ref, x_ref, wn_ref, dr_ref, dx_ref, dwn_ref = refs[npc:]
        dh = None
        for p_ref, o, wd in zip(p_refs, offs, widths):
            part = _dot_nt(p_ref[...], w_ref[:, o:o + wd])
            dh = part if dh is None else dh + part
        dx, dwn = _rms_bwd(dh, x_ref[...], wn_ref[...])
        dx_ref[...] = dr_ref[...] + dx

        @pl.when(pl.program_id(0) == 0)
        def _():
            dwn_ref[...] = jnp.zeros_like(dwn_ref)

        dwn_ref[...] += jnp.sum(dwn, axis=0, keepdims=True)

    return pl.pallas_call(
        body, name="proj_bwd", grid=(t // tm,),
        in_specs=[pl.BlockSpec((tm, wd), lambda i: (i, 0)) for wd in widths]
        + [pl.BlockSpec((d, n), lambda i: (0, 0)),
           pl.BlockSpec((tm, d), lambda i: (i, 0)),
           pl.BlockSpec((1, d), lambda i: (0, 0)),
           pl.BlockSpec((tm, d), lambda i: (i, 0))],
        out_specs=[pl.BlockSpec((tm, d), lambda i: (i, 0)),
                   pl.BlockSpec((1, d), lambda i: (0, 0))],
        out_shape=[jax.ShapeDtypeStruct((t, d), F32), jax.ShapeDtypeStruct((1, d), F32)],
        compiler_params=_cparams(("arbitrary",)),
    )(*pieces, w, x, wn, dres)


def _dw_in(h0, pieces, tm):
    t, k = h0.shape
    widths = [p.shape[1] for p in pieces]
    offs = [sum(widths[:i]) for i in range(len(widths))]
    n = sum(widths)
    npc = len(pieces)

    def body(*refs):
        h_ref = refs[0]
        p_refs = refs[1:1 + npc]
        o_ref = refs[1 + npc]

        @pl.when(pl.program_id(0) == 0)
        def _():
            o_ref[...] = jnp.zeros_like(o_ref)

        ht = h_ref[...].astype(F32).T.astype(BF16)
        for p_ref, o, wd in zip(p_refs, offs, widths):
            o_ref[:, o:o + wd] += _dot(ht, p_ref[...])

    return pl.pallas_call(
        body, name="dw_in", grid=(t // tm,),
        in_specs=[pl.BlockSpec((tm, k), lambda i: (i, 0))]
        + [pl.BlockSpec((tm, wd), lambda i: (i, 0)) for wd in widths],
        out_specs=pl.BlockSpec((k, n), lambda i: (0, 0)),
        out_shape=jax.ShapeDtypeStruct((k, n), F32),
        compiler_params=_cparams(("arbitrary",)),
    )(h0, *pieces)


def _mm_tn(a, b, tn, name, tm):
    t, k = a.shape
    n = b.shape[1]

    def body(a_ref, b_ref, o_ref):
        @pl.when(pl.program_id(1) == 0)
        def _():
            o_ref[...] = jnp.zeros_like(o_ref)

        o_ref[...] += _dot_tn(a_ref[...].astype(BF16), b_ref[...].astype(BF16))

    return pl.pallas_call(
        body, name=name, grid=(n // tn, t // tm),
        in_specs=[pl.BlockSpec((tm, k), lambda j, i: (i, 0)),
                  pl.BlockSpec((tm, tn), lambda j, i: (i, j))],
        out_specs=pl.BlockSpec((k, tn), lambda j, i: (0, j)),
        out_shape=jax.ShapeDtypeStruct((k, n), F32),
        compiler_params=_cparams(("parallel", "arbitrary")),
    )(a, b)


def _rope_tables(s):
    rows = s // GRID_W
    row_id = jnp.repeat(jnp.arange(rows, dtype=F32), GRID_W)
    col_id = jnp.tile(jnp.arange(GRID_W, dtype=F32), rows)
    half = HEAD_DIM // 2
    inv_freq = ROPE_THETA ** (-jnp.arange(0, half, 2, dtype=F32) / half)
    ang_r = row_id[:, None] * inv_freq[None, :]
    ang_c = col_id[:, None] * inv_freq[None, :]
    ang = jnp.concatenate([ang_r, ang_r, ang_c, ang_c], axis=-1)
    cos, sin = jnp.cos(ang), jnp.sin(ang)
    return jnp.tile(cos, (1, N_HEADS)), jnp.tile(sin, (1, N_HEADS))


def _head_rms(x, w, ones):
    r = lax.rsqrt(_dot_precise(x * x, ones, 2) * (1.0 / HEAD_DIM) + EPS)
    return x * r * w, r


def _dup_half(x, kv):
    lane = lax.broadcasted_iota(jnp.int32, x.shape, 1)
    sel = (lane < 64) if kv == 0 else (lane >= 64)
    return jnp.where(sel, x, pltpu.roll(x, 64, 1))


def _qkv_prep(proj, cosq, sinq, qw, kw, ones, s, tm):
    t = proj.shape[0]
    ns = s // tm

    def body(p_ref, cos_ref, sin_ref, qw_ref, kw_ref, ones_ref, q_out, kd_out, vd_out, kdt_out, vdt_out):
        cos = cos_ref[...]
        sin = sin_ref[...]
        ones_m = ones_ref[...]
        qn, _ = _head_rms(p_ref[:, 0:512], qw_ref[...], ones_m)
        q_out[...] = (_rope(qn, cos, sin) * (HEAD_DIM ** -0.5)).astype(BF16)
        kn, _ = _head_rms(p_ref[:, 512:640], kw_ref[...], ones_m[0:128, 0:128])
        kr = _rope(kn, cos[:, 0:128], sin[:, 0:128])
        v = p_ref[:, 640:768]
        for kv in range(N_KV):
            kd = _dup_half(kr, kv)
            vd = _dup_half(v, kv)
            kd_out[kv] = kd.astype(BF16)
            vd_out[kv] = vd.astype(BF16)
            kdt_out[kv] = kd.T.astype(BF16)
            vdt_out[kv] = vd.T.astype(BF16)

    return pl.pallas_call(
        body, name="qkv_prep", grid=(t // tm,),
        in_specs=[pl.BlockSpec((tm, 768), lambda i: (i, 0)),
                  pl.BlockSpec((tm, 512), lambda i: (i % ns, 0)),
                  pl.BlockSpec((tm, 512), lambda i: (i % ns, 0)),
                  pl.BlockSpec((1, 512), lambda i: (0, 0)),
                  pl.BlockSpec((1, 128), lambda i: (0, 0)),
                  pl.BlockSpec((512, 512), lambda i: (0, 0))],
        out_specs=[pl.BlockSpec((tm, 512), lambda i: (i, 0)),
                   pl.BlockSpec((N_KV, tm, 128), lambda i: (0, i, 0)),
                   pl.BlockSpec((N_KV, tm, 128), lambda i: (0, i, 0)),
                   pl.BlockSpec((N_KV, 128, tm), lambda i: (0, 0, i)),
                   pl.BlockSpec((N_KV, 128, tm), lambda i: (0, 0, i))],
        out_shape=[jax.ShapeDtypeStruct((t, 512), BF16),
                   jax.ShapeDtypeStruct((N_KV, t, 128), BF16),
                   jax.ShapeDtypeStruct((N_KV, t, 128), BF16),
                   jax.ShapeDtypeStruct((N_KV, 128, t), BF16),
                   jax.ShapeDtypeStruct((N_KV, 128, t), BF16)],
        compiler_params=_cparams(("parallel",)),
    )(proj, cosq, sinq, qw, kw, ones)


def _qkv_bwd(proj, dq, dkd, dvd, cosq, sinq, qw, kw, ones, s, tm):
    t = proj.shape[0]
    ns = s // tm

    def body(p_ref, dq_ref, dkd_ref, dvd_ref, cos_ref, sin_ref, qw_ref, kw_ref, ones_ref,
             out_ref, dqw_ref, dkw_ref):
        cos = cos_ref[...]
        sin = sin_ref[...]
        ones_m = ones_ref[...]
        ones_k = ones_m[0:128, 0:128]

        def norm_bwd(x, w, dn, om):
            r = lax.rsqrt(_dot_precise(x * x, om, 2) * (1.0 / HEAD_DIM) + EPS)
            g = dn * w
            dx = r * (g - x * (r * r) * (_dot_precise(g * x, om, 2) * (1.0 / HEAD_DIM)))
            return dx, jnp.sum(dn * x * r, axis=0, keepdims=True)

        q = p_ref[:, 0:512]
        dqn = _rope_t(dq_ref[...], cos, sin) * (HEAD_DIM ** -0.5)
        dq_raw, dqw = norm_bwd(q, qw_ref[...], dqn, ones_m)
        out_ref[:, 0:512] = dq_raw.astype(BF16)

        lane = lax.broadcasted_iota(jnp.int32, (tm, 128), 1)

        def fold(ref):
            a0 = ref[0]
            a1 = ref[1]
            f0 = a0 + pltpu.roll(a0, 64, 1)
            f1 = a1 + pltpu.roll(a1, 64, 1)
            return jnp.where(lane < 64, f0, f1)

        k = p_ref[:, 512:640]
        dkn = _rope_t(fold(dkd_ref), cos[:, 0:128], sin[:, 0:128])
        dk_raw, dkw = norm_bwd(k, kw_ref[...], dkn, ones_k)
        out_ref[:, 512:640] = dk_raw.astype(BF16)
        out_ref[:, 640:768] = fold(dvd_ref).astype(BF16)

        @pl.when(pl.program_id(0) == 0)
        def _():
            dqw_ref[...] = jnp.zeros_like(dqw_ref)
            dkw_ref[...] = jnp.zeros_like(dkw_ref)

        dqw_ref[...] += dqw
        dkw_ref[...] += dkw

    return pl.pallas_call(
        body, name="qkv_bwd", grid=(t // tm,),
        in_specs=[pl.BlockSpec((tm, 768), lambda i: (i, 0)),
                  pl.BlockSpec((tm, 512), lambda i: (i, 0)),
                  pl.BlockSpec((N_KV, tm, 128), lambda i: (0, i, 0)),
                  pl.BlockSpec((N_KV, tm, 128), lambda i: (0, i, 0)),
                  pl.BlockSpec((tm, 512), lambda i: (i % ns, 0)),
                  pl.BlockSpec((tm, 512), lambda i: (i % ns, 0)),
                  pl.BlockSpec((1, 512), lambda i: (0, 0)),
                  pl.BlockSpec((1, 128), lambda i: (0, 0)),
                  pl.BlockSpec((512, 512), lambda i: (0, 0))],
        out_specs=[pl.BlockSpec((tm, 768), lambda i: (i, 0)),
                   pl.BlockSpec((1, 512), lambda i: (0, 0)),
                   pl.BlockSpec((1, 128), lambda i: (0, 0))],
        out_shape=[jax.ShapeDtypeStruct((t, 768), BF16),
                   jax.ShapeDtypeStruct((1, 512), F32),
                   jax.ShapeDtypeStruct((1, 128), F32)],
        compiler_params=_cparams(("arbitrary",)),
    )(proj, dq, dkd, dvd, cosq, sinq, qw, kw, ones)


def _grid_step_id(grid):
    idx = pl.program_id(0)
    for ax in range(1, len(grid)):
        idx = idx * grid[ax] + pl.program_id(ax)
    return idx


def _attn_fwd(q, kd, vdt, nb, s, tq, ride=None):
    t = q.shape[0]
    nq = s // tq
    rd = _ride_plan(ride)
    grid = (nb, N_HEADS // 2, nq)
    nsteps = nb * (N_HEADS // 2) * nq

    def body(*refs):
        q_ref, k_ref, vt_ref = refs[:3]
        o_ref, lse_ref = refs[3 + rd.n:5 + rd.n]
        copies = rd.copies(refs[3:3 + rd.n], refs[5 + rd.n:5 + 2 * rd.n], refs[5 + 2 * rd.n:])
        step_id = _grid_step_id(grid)
        _ride_start(copies, step_id == 0)
        qv = q_ref[...].astype(F32)
        lane = lax.broadcasted_iota(jnp.int32, qv.shape, 1)
        k = k_ref[0]
        vt = vt_ref[0]
        outs = []
        for half in range(2):
            qh = jnp.where((lane < 64) if half == 0 else (lane >= 64), qv, 0.0).astype(BF16)
            st = _dot_nt(k, qh)
            m = jnp.max(st, axis=0, keepdims=True)
            p = jnp.exp(st - m)
            l = jnp.sum(p, axis=0, keepdims=True)
            ot = _dot(vt, p.astype(BF16)) / l
            lse_ref[0, half] = m + jnp.log(l)
            outs.append(ot)
        row = lax.broadcasted_iota(jnp.int32, outs[0].shape, 0)
        o_ref[...] = jnp.where(row < 64, outs[0], outs[1]).T
        _ride_wait(copies, step_id == nsteps - 1)

    return pl.pallas_call(
        body, name="attn_fwd", grid=grid,
        in_specs=[pl.BlockSpec((tq, 128), lambda b, p, i: (b * nq + i, p)),
                  pl.BlockSpec((1, s, 128), lambda b, p, i: (p // 2, b, 0)),
                  pl.BlockSpec((1, 128, s), lambda b, p, i: (p // 2, 0, b))] + rd.in_specs,
        out_specs=[pl.BlockSpec((tq, 128), lambda b, p, i: (b * nq + i, p)),
                   pl.BlockSpec((1, 2, 1, tq), lambda b, p, i: (b, p, 0, i))] + rd.out_specs,
        out_shape=[jax.ShapeDtypeStruct((t, D_ATTN), F32),
                   jax.ShapeDtypeStruct((nb, N_HEADS, 1, s), F32)] + rd.out_shape,
        scratch_shapes=rd.scratch,
        compiler_params=_cparams(("arbitrary", "arbitrary", "arbitrary")),
    )(q, kd, vdt, *rd.srcs)


def _attn_bwd(q, kd, vd, kdt, o, lse, do, nb, s, tq, ride=None):
    t = q.shape[0]
    nq = s // tq
    ones8 = jnp.ones((8, 128), BF16)
    rd = _ride_plan(ride)
    grid = (nb, N_KV, 2, nq)
    nsteps = nb * N_KV * 2 * nq

    def body(*refs):
        q_ref, k_ref, v_ref, kt_ref, o_ref, lse_ref, do_ref, ones_ref = refs[:8]
        dq_ref, dk_ref, dv_ref = refs[8 + rd.n:11 + rd.n]
        copies = rd.copies(refs[8:8 + rd.n], refs[11 + rd.n:11 + 2 * rd.n], refs[11 + 2 * rd.n:])
        step_id = _grid_step_id(grid)
        _ride_start(copies, step_id == 0)

        @pl.when((pl.program_id(2) == 0) & (pl.program_id(3) == 0))
        def _():
            dk_ref[...] = jnp.zeros_like(dk_ref)
            dv_ref[...] = jnp.zeros_like(dv_ref)

        qv = q_ref[...].astype(F32)
        dov = do_ref[...]
        ov = o_ref[...]
        lane = lax.broadcasted_iota(jnp.int32, qv.shape, 1)
        k = k_ref[0]
        v = v_ref[0]
        kt = kt_ref[0]
        dqs = []
        dk_acc = None
        dv_acc = None
        for half in range(2):
            sel = (lane < 64) if half == 0 else (lane >= 64)
            qh = jnp.where(sel, qv, 0.0).astype(BF16)
            doh = jnp.where(sel, dov, 0.0)
            dob = doh.astype(BF16)
            delta = None
            for part in _split_bf16(doh * ov, 3):
                d8 = _dot_nt(ones_ref[...], part)
                delta = d8 if delta is None else delta + d8
            delta = delta[0:1, :]
            st = _dot_nt(k, qh)
            pt = jnp.exp(st - lse_ref[0, half])
            dpt = _dot_nt(v, dob)
            dst = (pt * (dpt - delta)).astype(BF16)
            dkh = _dot(dst, qh)
            dvh = _dot(pt.astype(BF16), dob)
            dk_acc = dkh if dk_acc is None else dk_acc + dkh
            dv_acc = dvh if dv_acc is None else dv_acc + dvh
            dqs.append(_dot(kt, dst))
        dk_ref[0] += dk_acc
        dv_ref[0] += dv_acc
        row = lax.broadcasted_iota(jnp.int32, dqs[0].shape, 0)
        dq_ref[...] = jnp.where(row < 64, dqs[0], dqs[1]).T
        _ride_wait(copies, step_id == nsteps - 1)

    qmap = lambda b, g, p, i: (b * nq + i, g * 2 + p)
    kvmap = lambda b, g, p, i: (g, b, 0)
    return pl.pallas_call(
        body, name="attn_bwd", grid=grid,
        in_specs=[pl.BlockSpec((tq, 128), qmap),
                  pl.BlockSpec((1, s, 128), kvmap),
                  pl.BlockSpec((1, s, 128), kvmap),
                  pl.BlockSpec((1, 128, s), lambda b, g, p, i: (g, 0, b)),
                  pl.BlockSpec((tq, 128), qmap),
                  pl.BlockSpec((1, 2, 1, tq), lambda b, g, p, i: (b, g * 2 + p, 0, i)),
                  pl.BlockSpec((tq, 128), qmap),
                  pl.BlockSpec((8, 128), lambda b, g, p, i: (0, 0))] + rd.in_specs,
        out_specs=[pl.BlockSpec((tq, 128), qmap),
                   pl.BlockSpec((1, s, 128), kvmap),
                   pl.BlockSpec((1, s, 128), kvmap)] + rd.out_specs,
        out_shape=[jax.ShapeDtypeStruct((t, D_ATTN), F32),
                   jax.ShapeDtypeStruct((N_KV, t, 128), F32),
                   jax.ShapeDtypeStruct((N_KV, t, 128), F32)] + rd.out_shape,
        scratch_shapes=rd.scratch,
        compiler_params=_cparams(("arbitrary", "arbitrary", "arbitrary", "arbitrary")),
    )(q, kd, vd, kdt, o, lse, do, ones8, *rd.srcs)


def _tri_mats():
    i = np.arange(CHUNK)
    lower = jnp.asarray(i[:, None] >= i[None, :], dtype=BF16)
    upper = jnp.asarray(i[:, None] <= i[None, :], dtype=BF16)
    return jnp.stack([lower, upper])


def _running_sum(tri, x):
    acc = None
    for part in _split_bf16(x, 3):
        t = _dot(tri, part)
        acc = t if acc is None else acc + t
    return acc


def _gates(z, lb):
    sig = _sigmoid(z)
    f = lb + (1.0 - lb) * sig
    logf = jnp.log(jnp.maximum(f, F_MIN))
    k = (1.0 - lb) * (1.0 - sig)
    return sig, f, logf, k


def _row_group(jg, anti):
    if anti:
        return 0, 8 * jg + 8
    return 8 * jg, CHUNK


def _chunk_fwd(q, k, v, b, st, bones, bmask, anti):
    b_last = b[0:1] if anti else b[CHUNK - 1:CHUNK]
    qb = q * jnp.exp(b)
    o_inter = _dot_nt(qb.astype(BF16), st.astype(BF16))
    kt = k * jnp.exp(b_last - b)
    st_new = st * jnp.exp(b_last) + _dot_tn(v.astype(BF16), kt.astype(BF16)) * bmask
    tt = lax.broadcasted_iota(jnp.int32, (CHUNK, LANES), 0)
    blocks = []
    for jg in range(CHUNK // 8):
        r0, r1 = _row_group(jg, anti)
        nr = r1 - r0
        qr = q[r0:r1]
        br = b[r0:r1]
        tr = tt[r0:r1]
        ws = []
        for i in range(8):
            sc = 8 * jg + i
            mask = (tr <= sc) if anti else (tr >= sc)
            e = jnp.where(mask, jnp.exp(jnp.minimum(br - b[sc:sc + 1], 0.0)), 0.0)
            ws.append(qr * e * k[sc:sc + 1])
        pb = _dot(jnp.concatenate(ws, axis=0).astype(BF16), bones)
        acc = None
        for i in range(8):
            sc = 8 * jg + i
            term = pb[i * nr:(i + 1) * nr] * v[sc:sc + 1]
            acc = term if acc is None else acc + term
        blocks.append((r0, r1, acc))
    o = o_inter
    pieces = []
    for g in range(CHUNK // 8):
        tot = o[8 * g:8 * g + 8]
        for (r0, r1, acc) in blocks:
            if r0 <= 8 * g and 8 * g + 8 <= r1:
                tot = tot + acc[8 * g - r0:8 * g - r0 + 8]
        pieces.append(tot)
    return jnp.concatenate(pieces, axis=0), st_new


def _hgrn_fwd(proj, lb, gw, nb, s, ride=None):
    t = proj.shape[0]
    nc = s // CHUNK
    tri = _tri_mats()
    bones = _block_ones(LANES, HEAD_DIM)
    rd = _ride_plan(ride)

    def body(*refs):
        q_ref, zf_ref, zb_ref, v_ref, g_ref, lb_ref, gw_ref, tri_ref, bones_ref = refs[:9]
        y_ref, os_ref = refs[9 + rd.n:11 + rd.n]
        st_ref = refs[11 + 2 * rd.n]
        copies = rd.copies(refs[9:9 + rd.n], refs[11 + rd.n:11 + 2 * rd.n], refs[12 + 2 * rd.n:])
        step_id = pl.program_id(0) * 2 + pl.program_id(1)
        _ride_start(copies, step_id == 0)
        bones_m = bones_ref[...]
        bmask = bones_m.astype(F32)
        for anti in (False, True):
            z_ref = zb_ref if anti else zf_ref
            lbv = lb_ref[1:2] if anti else lb_ref[0:1]
            trim = tri_ref[1] if anti else tri_ref[0]
            st_ref[...] = jnp.zeros_like(st_ref)

            def step(n, carry, anti=anti, z_ref=z_ref, lbv=lbv, trim=trim):
                cn = (nc - 1 - n) if anti else n
                rows = pl.ds(pl.multiple_of(cn * CHUNK, CHUNK), CHUNK)
                q = q_ref[rows, :]
                v = v_ref[rows, :]
                _, _, logf, k = _gates(z_ref[rows, :], lbv)
                b = _running_sum(trim, logf)
                o, st_new = _chunk_fwd(q, k, v, b, st_ref[...], bones_m, bmask, anti)
                st_ref[...] = st_new
                if anti:
                    osum = os_ref[rows, :] + o
                    os_ref[rows, :] = osum
                    r = lax.rsqrt(_dot_precise(osum * osum, bones_m, 2) * (1.0 / HEAD_DIM) + EPS)
                    hg = g_ref[rows, :]
                    y_ref[rows, :] = osum * r * gw_ref[...] * (hg * _sigmoid(hg))
                else:
                    os_ref[rows, :] = o
                return carry

            lax.fori_loop(0, nc, step, 0)
        _ride_wait(copies, step_id == nb * 2 - 1)

    def col(c):
        return pl.BlockSpec((s, LANES), lambda b, p, c=c: (b, c + p))

    return pl.pallas_call(
        body, name="hgrn_fwd", grid=(nb, 2),
        in_specs=[col(COL_HQ), col(COL_ZFW), col(COL_ZBW), col(COL_HI), col(COL_HG),
                  pl.BlockSpec((2, LANES), lambda b, p: (0, p)),
                  pl.BlockSpec((1, LANES), lambda b, p: (0, 0)),
                  pl.BlockSpec((2, CHUNK, CHUNK), lambda b, p: (0, 0, 0)),
                  pl.BlockSpec((LANES, LANES), lambda b, p: (0, 0))] + rd.in_specs,
        out_specs=[pl.BlockSpec((s, LANES), lambda b, p: (b, p)),
                   pl.BlockSpec((s, LANES), lambda b, p: (b, p))] + rd.out_specs,
        out_shape=[jax.ShapeDtypeStruct((t, D_HGRN), F32), jax.ShapeDtypeStruct((t, D_HGRN), F32)] + rd.out_shape,
        scratch_shapes=[pltpu.VMEM((LANES, LANES), F32)] + rd.scratch,
        compiler_params=_cparams(("arbitrary", "arbitrary")),
    )(proj, proj, proj, proj, proj, lb, gw, tri, bones, *rd.srcs)


def _chunk_bwd(q, k, v, b, do, st_in, rt, bones, bmask, anti):
    b_last = b[0:1] if anti else b[CHUNK - 1:CHUNK]
    eb = jnp.exp(b)
    ebl = jnp.exp(b_last - b)
    dob = do.astype(BF16)
    rtb = rt.astype(BF16)
    dq_inter = eb * _dot(dob, st_in.astype(BF16))
    dk_inter = ebl * _dot(v.astype(BF16), rtb)
    dv_inter = _dot_nt((k * ebl).astype(BF16), rtb)
    rt_new = rt * jnp.exp(b_last) + _dot_tn(dob, (q * eb).astype(BF16)) * bmask
    tt = lax.broadcasted_iota(jnp.int32, (CHUNK, LANES), 0)
    r8 = lax.broadcasted_iota(jnp.int32, (8, LANES), 0)
    blocks = []
    dk_pieces = []
    dv_pieces = []
    for jg in range(CHUNK // 8):
        r0, r1 = _row_group(jg, anti)
        nr = r1 - r0
        qr = q[r0:r1]
        br = b[r0:r1]
        tr = tt[r0:r1]
        dor = do[r0:r1]
        qes, kes, prods = [], [], []
        for i in range(8):
            sc = 8 * jg + i
            mask = (tr <= sc) if anti else (tr >= sc)
            e = jnp.where(mask, jnp.exp(jnp.minimum(br - b[sc:sc + 1], 0.0)), 0.0)
            qe = qr * e
            qes.append(qe)
            kes.append(e * k[sc:sc + 1])
            prods.append(qe * k[sc:sc + 1])
        for i in range(8):
            prods.append(dor * v[8 * jg + i:8 * jg + i + 1])
        sums = _dot(jnp.concatenate(prods, axis=0).astype(BF16), bones)
        acc = None
        dk_blk = jnp.zeros((8, LANES), F32)
        dv_blk = jnp.zeros((8, LANES), F32)
        for i in range(8):
            pb = sums[i * nr:(i + 1) * nr]
            dpb = sums[(8 + i) * nr:(9 + i) * nr]
            term = dpb * kes[i]
            acc = term if acc is None else acc + term
            dk_s = jnp.sum(dpb * qes[i], axis=0, keepdims=True)
            dv_s = jnp.sum(pb * dor, axis=0, keepdims=True)
            dk_blk = jnp.where(r8 == i, dk_s, dk_blk)
            dv_blk = jnp.where(r8 == i, dv_s, dv_blk)
        blocks.append((r0, r1, acc))
        dk_pieces.append(dk_blk)
        dv_pieces.append(dv_blk)
    dq_pieces = []
    for g in range(CHUNK // 8):
        tot = dq_inter[8 * g:8 * g + 8]
        for (r0, r1, acc) in blocks:
            if r0 <= 8 * g and 8 * g + 8 <= r1:
                tot = tot + acc[8 * g - r0:8 * g - r0 + 8]
        dq_pieces.append(tot)
    dq = jnp.concatenate(dq_pieces, axis=0)
    dk = dk_inter + jnp.concatenate(dk_pieces, axis=0)
    dv = dv_inter + jnp.concatenate(dv_pieces, axis=0)
    db_last = (jnp.sum(k * dk_inter, axis=0, keepdims=True)
               + jnp.exp(b_last) * jnp.sum(rt * st_in, axis=0, keepdims=True))
    return dq, dk, dv, rt_new, db_last


def _hgrn_bwd(proj, lb, gw, osum, dy, nb, s, ride=None):
    t = proj.shape[0]
    nc = s // CHUNK
    tri = _tri_mats()
    bones = _block_ones(LANES, HEAD_DIM)
    rd = _ride_plan(ride)

    def body(*refs):
        q_ref, zf_ref, zb_ref, v_ref, g_ref, lb_ref, gw_ref, os_ref, dy_ref, tri_ref, bones_ref = refs[:11]
        dq_ref, dzf_ref, dzb_ref, dv_ref, dg_ref, dgw_ref, dlb_ref = refs[11 + rd.n:18 + rd.n]
        do_sc, dq_sc, dv_sc, st_sc, st_cur, rt_cur = refs[18 + 2 * rd.n:24 + 2 * rd.n]
        copies = rd.copies(refs[11:11 + rd.n], refs[18 + rd.n:18 + 2 * rd.n], refs[24 + 2 * rd.n:])
        step_id = pl.program_id(0) * 2 + pl.program_id(1)
        _ride_start(copies, step_id == 0)
        bones_m = bones_ref[...]
        bmask = bones_m.astype(F32)
        gwv = gw_ref[...]

        def head(n, acc):
            rows = pl.ds(pl.multiple_of(n * CHUNK, CHUNK), CHUNK)
            o = os_ref[rows, :]
            hg = g_ref[rows, :]
            dyv = dy_ref[rows, :]
            sg = _sigmoid(hg)
            r = lax.rsqrt(_dot_precise(o * o, bones_m, 2) * (1.0 / HEAD_DIM) + EPS)
            nrm = o * r * gwv
            dn = dyv * (hg * sg)
            dg_ref[rows, :] = (dyv * nrm * (sg * (1.0 + hg * (1.0 - sg)))).astype(BF16)
            g = dn * gwv
            mean_go = _dot_precise(g * o, bones_m, 2) * (1.0 / HEAD_DIM)
            do_sc[rows, :] = r * (g - o * (r * r) * mean_go)
            return acc + jnp.sum(dn * o * r, axis=0, keepdims=True)

        dgw_ref[0] = lax.fori_loop(0, nc, head, jnp.zeros((1, LANES), F32))
        dq_sc[...] = jnp.zeros_like(dq_sc)
        dv_sc[...] = jnp.zeros_like(dv_sc)

        for anti in (False, True):
            z_ref = zb_ref if anti else zf_ref
            dz_ref = dzb_ref if anti else dzf_ref
            lbv = lb_ref[1:2] if anti else lb_ref[0:1]
            trim = tri_ref[1] if anti else tri_ref[0]
            trim_r = tri_ref[0] if anti else tri_ref[1]

            def load(cn, z_ref=z_ref, lbv=lbv, trim=trim):
                rows = pl.ds(pl.multiple_of(cn * CHUNK, CHUNK), CHUNK)
                q = q_ref[rows, :]
                v = v_ref[rows, :]
                sig, f, logf, k = _gates(z_ref[rows, :], lbv)
                b = _running_sum(trim, logf)
                return rows, q, v, sig, f, k, b

            st_cur[...] = jnp.zeros_like(st_cur)

            def sweep(n, carry, anti=anti, load=load):
                cn = (nc - 1 - n) if anti else n
                _, q, v, _, _, k, b = load(cn)
                st_sc[cn] = st_cur[...]
                b_last = b[0:1] if anti else b[CHUNK - 1:CHUNK]
                kt = k * jnp.exp(b_last - b)
                st_cur[...] = st_cur[...] * jnp.exp(b_last) + _dot_tn(v.astype(BF16), kt.astype(BF16)) * bmask
                return carry

            lax.fori_loop(0, nc, sweep, 0)

            rt_cur[...] = jnp.zeros_like(rt_cur)

            def back(n, dlb, anti=anti, load=load, lbv=lbv, trim_r=trim_r, dz_ref=dz_ref):
                cn = n if anti else (nc - 1 - n)
                rows, q, v, sig, f, k, b = load(cn)
                do = do_sc[rows, :]
                dq, dk, dv, rt_new, db_last = _chunk_bwd(q, k, v, b, do, st_sc[cn], rt_cur[...], bones_m,
                                                         bmask, anti)
                rt_cur[...] = rt_new
                dq_sc[rows, :] += dq
                dv_sc[rows, :] += dv
                dlogf = _running_sum(trim_r, q * dq - k * dk) + db_last
                dfl = jnp.where(f > F_MIN, dlogf / f, 0.0)
                dz_ref[rows, :] = ((dfl - dk) * (1.0 - lbv) * sig * (1.0 - sig)).astype(BF16)
                return dlb + jnp.sum((dfl - dk) * (1.0 - sig), axis=0, keepdims=True)

            dlb = lax.fori_loop(0, nc, back, jnp.zeros((1, LANES), F32))
            side = 1 if anti else 0
            dlb_ref[0, side:side + 1, :] = dlb

        dq_ref[...] = dq_sc[...].astype(BF16)
        dv_ref[...] = dv_sc[...].astype(BF16)
        _ride_wait(copies, step_id == nb * 2 - 1)

    def col(c):
        return pl.BlockSpec((s, LANES), lambda b, p, c=c: (b, c + p))

    sl = pl.BlockSpec((s, LANES), lambda b, p: (b, p))
    out_t = jax.ShapeDtypeStruct((t, D_HGRN), BF16)
    return pl.pallas_call(
        body, name="hgrn_bwd", grid=(nb, 2),
        in_specs=[col(COL_HQ), col(COL_ZFW), col(COL_ZBW), col(COL_HI), col(COL_HG),
                  pl.BlockSpec((2, LANES), lambda b, p: (0, p)),
                  pl.BlockSpec((1, LANES), lambda b, p: (0, 0)),
                  sl, sl,
                  pl.BlockSpec((2, CHUNK, CHUNK), lambda b, p: (0, 0, 0)),
                  pl.BlockSpec((LANES, LANES), lambda b, p: (0, 0))] + rd.in_specs,
        out_specs=[sl, sl, sl, sl, sl,
                   pl.BlockSpec((1, 1, LANES), lambda b, p: (b, 0, p)),
                   pl.BlockSpec((1, 2, LANES), lambda b, p: (b, 0, p))] + rd.out_specs,
        out_shape=[out_t, out_t, out_t, out_t, out_t,
                   jax.ShapeDtypeStruct((nb, 1, D_HGRN), F32),
                   jax.ShapeDtypeStruct((nb, 2, D_HGRN), F32)] + rd.out_shape,
        scratch_shapes=[pltpu.VMEM((s, LANES), F32), pltpu.VMEM((s, LANES), F32), pltpu.VMEM((s, LANES), F32),
                        pltpu.VMEM((nc, LANES, LANES), F32), pltpu.VMEM((LANES, LANES), F32),
                        pltpu.VMEM((LANES, LANES), F32)] + rd.scratch,
        compiler_params=_cparams(("arbitrary", "arbitrary")),
    )(proj, proj, proj, proj, proj, lb, gw, osum, dy, tri, bones, *rd.srcs)


def _lower_bounds(logits):
    def body(lg_ref, lb_ref):
        rows = [lg_ref[l:l + 1, :] for l in range(DEPTH)]
        m = functools.reduce(jnp.maximum, rows)
        ex = [jnp.exp(r - m) for r in rows]
        den = functools.reduce(jnp.add, ex)
        run = jnp.zeros_like(m)
        for l in range(DEPTH):
            if l > 0:
                run = run + ex[l] / den
            lb_ref[l:l + 1, :] = run

    return pl.pallas_call(body, name="lower_bounds", out_shape=jax.ShapeDtypeStruct(logits.shape, F32))(logits)


def _lower_bounds_bwd(logits, dlb):
    def body(lg_ref, dlb_ref, dlg_ref):
        rows = [lg_ref[l:l + 1, :] for l in range(DEPTH)]
        m = functools.reduce(jnp.maximum, rows)
        ex = [jnp.exp(r - m) for r in rows]
        den = functools.reduce(jnp.add, ex)
        sm = [e / den for e in ex]
        dsm = [jnp.zeros_like(m) for _ in range(DEPTH)]
        for i in range(1, DEPTH):
            for l in range(i, DEPTH):
                dsm[i] = dsm[i] + dlb_ref[l:l + 1, :]
        dot = functools.reduce(jnp.add, [sm[i] * dsm[i] for i in range(DEPTH)])
        for i in range(DEPTH):
            dlg_ref[i:i + 1, :] = sm[i] * (dsm[i] - dot)

    return pl.pallas_call(body, name="lower_bounds_bwd", out_shape=jax.ShapeDtypeStruct(logits.shape, F32))(logits, dlb)


CONV_ROWS = 128


def _conv_core(a, bg, dww, dwb, lnw, lnb, upad_ref, s):
    sb = _sigmoid(bg)
    u = a * sb
    upad_ref[0:16, :] = jnp.zeros((16, D_CONV), F32)
    upad_ref[16:16 + s, :] = u
    upad_ref[16 + s:32 + s, :] = jnp.zeros((16, D_CONV), F32)
    rows = min(s, CONV_ROWS)
    pieces = []
    for r0 in range(0, s, rows):
        acc = None
        for j in range(CONV_W):
            term = upad_ref[r0 + 1 + j:r0 + 1 + j + rows, :] * dww[j:j + 1, :]
            acc = term if acc is None else acc + term
        pieces.append(acc)
    c = jnp.concatenate(pieces, axis=0) + dwb
    mu = jnp.mean(c, axis=-1, keepdims=True)
    xc = c - mu
    rstd = lax.rsqrt(jnp.mean(xc * xc, axis=-1, keepdims=True) + LN_EPS)
    nh = xc * rstd
    l = nh * lnw + lnb
    sl = _sigmoid(l)
    return sb, nh, rstd, l, sl


def _conv_fwd(proj, dww, dwb, lnw, lnb, pww, pwb, nb, s):
    t = proj.shape[0]
    assert s % min(s, CONV_ROWS) == 0

    def body(a_ref, b_ref, dww_ref, dwb_ref, lnw_ref, lnb_ref, pww_ref, pwb_ref, y_ref, upad_ref):
        _, _, _, l, sl = _conv_core(a_ref[...], b_ref[...], dww_ref[...], dwb_ref[...], lnw_ref[...],
                                    lnb_ref[...], upad_ref, s)
        y_ref[...] = _dot((l * sl).astype(BF16), pww_ref[...]) + pwb_ref[...]

    vec = pl.BlockSpec((1, D_CONV), lambda b: (0, 0))
    return pl.pallas_call(
        body, name="conv_fwd", grid=(nb,),
        in_specs=[pl.BlockSpec((s, D_CONV), lambda b: (b, COL_CA)),
                  pl.BlockSpec((s, D_CONV), lambda b: (b, COL_CB)),
                  pl.BlockSpec((32, D_CONV), lambda b: (0, 0)), vec, vec, vec,
                  pl.BlockSpec((D_CONV, D_CONV), lambda b: (0, 0)), vec],
        out_specs=pl.BlockSpec((s, D_CONV), lambda b: (b, 0)),
        out_shape=jax.ShapeDtypeStruct((t, D_CONV), F32),
        scratch_shapes=[pltpu.VMEM((s + 32, D_CONV), F32)],
        compiler_params=_cparams(("parallel",)),
    )(proj, proj, dww, dwb, lnw, lnb, pww, pwb)


def _conv_bwd(proj, dy, dww, dwb, lnw, lnb, pww, nb, s):
    t = proj.shape[0]

    def body(a_ref, b_ref, dy_ref, dww_ref, dwb_ref, lnw_ref, lnb_ref, pww_ref,
             dab_ref, ddww_ref, ddwb_ref, dlnw_ref, dlnb_ref, dpww_ref, dpwb_ref, upad_ref, dcpad_ref):
        a = a_ref[...]
        dww = dww_ref[...]
        sb, nh, rstd, l, sl = _conv_core(a, b_ref[...], dww, dwb_ref[...], lnw_ref[...], lnb_ref[...],
                                         upad_ref, s)
        dyv = dy_ref[...]
        dyb = dyv.astype(BF16)
        ds = _dot_nt(dyb, pww_ref[...])
        dl = ds * (sl * (1.0 + l * (1.0 - sl)))
        dn = dl * lnw_ref[...]
        dc = rstd * (dn - jnp.mean(dn, axis=-1, keepdims=True)
                     - nh * jnp.mean(dn * nh, axis=-1, keepdims=True))

        @pl.when(pl.program_id(0) == 0)
        def _():
            for r in (ddww_ref, ddwb_ref, dlnw_ref, dlnb_ref, dpww_ref, dpwb_ref):
                r[...] = jnp.zeros_like(r)

        dpww_ref[...] += _dot_tn((l * sl).astype(BF16), dyb)
        dpwb_ref[...] += jnp.sum(dyv, axis=0, keepdims=True)
        dlnw_ref[...] += jnp.sum(dl * nh, axis=0, keepdims=True)
        dlnb_ref[...] += jnp.sum(dl, axis=0, keepdims=True)
        ddwb_ref[...] += jnp.sum(dc, axis=0, keepdims=True)

        dcpad_ref[0:16, :] = jnp.zeros((16, D_CONV), F32)
        dcpad_ref[16:16 + s, :] = dc
        dcpad_ref[16 + s:32 + s, :] = jnp.zeros((16, D_CONV), F32)
        rows = min(s, CONV_ROWS)
        r8 = lax.broadcasted_iota(jnp.int32, (32, D_CONV), 0)
        ddww = jnp.zeros((32, D_CONV), F32)
        pieces = []
        for r0 in range(0, s, rows):
            acc = None
            dcr = dcpad_ref[16 + r0:16 + r0 + rows, :]
            for j in range(CONV_W):
                term = dcpad_ref[r0 + 31 - j:r0 + 31 - j + rows, :] * dww[j:j + 1, :]
                acc = term if acc is None else acc + term
                wj = jnp.sum(dcr * upad_ref[r0 + 1 + j:r0 + 1 + j + rows, :], axis=0, keepdims=True)
                ddww = ddww + jnp.where(r8 == j, wj, 0.0)
            pieces.append(acc)
        du = jnp.concatenate(pieces, axis=0)
        ddww_ref[...] += ddww
        dab_ref[:, 0:D_CONV] = (du * sb).astype(BF16)
        dab_ref[:, D_CONV:2 * D_CONV] = (du * a * sb * (1.0 - sb)).astype(BF16)

    vec = pl.BlockSpec((1, D_CONV), lambda b: (0, 0))
    mat = pl.BlockSpec((D_CONV, D_CONV), lambda b: (0, 0))
    w32 = pl.BlockSpec((32, D_CONV), lambda b: (0, 0))
    vshape = jax.ShapeDtypeStruct((1, D_CONV), F32)
    return pl.pallas_call(
        body, name="conv_bwd", grid=(nb,),
        in_specs=[pl.BlockSpec((s, D_CONV), lambda b: (b, COL_CA)),
                  pl.BlockSpec((s, D_CONV), lambda b: (b, COL_CB)),
                  pl.BlockSpec((s, D_CONV), lambda b: (b, 0)),
                  w32, vec, vec, vec, mat],
        out_specs=[pl.BlockSpec((s, 2 * D_CONV), lambda b: (b, 0)), w32, vec, vec, vec, mat, vec],
        out_shape=[jax.ShapeDtypeStruct((t, 2 * D_CONV), BF16),
                   jax.ShapeDtypeStruct((32, D_CONV), F32), vshape, vshape, vshape,
                   jax.ShapeDtypeStruct((D_CONV, D_CONV), F32), vshape],
        scratch_shapes=[pltpu.VMEM((s + 32, D_CONV), F32), pltpu.VMEM((s + 32, D_CONV), F32)],
        compiler_params=_cparams(("arbitrary",)),
    )(proj, proj, dy, dww, dwb, lnw, lnb, pww)


def _mix_out(o_attn, y_hgrn, y_conv, x, aw, cw, w_out, tm):
    t = x.shape[0]

    def body(o_ref, h_ref, c_ref, x_ref, aw_ref, cw_ref, w_ref, mixed_ref, x1_ref):
        o = o_ref[...]
        a = o * lax.rsqrt(jnp.mean(o * o, axis=-1, keepdims=True) + EPS) * aw_ref[...]
        yc = c_ref[...]
        c = yc * lax.rsqrt(jnp.mean(yc * yc, axis=-1, keepdims=True) + EPS) * cw_ref[...]
        ab, hb, cb = a.astype(BF16), h_ref[...].astype(BF16), c.astype(BF16)
        mixed_ref[:, 0:512] = ab
        mixed_ref[:, 512:768] = hb
        mixed_ref[:, 768:1024] = cb
        x1_ref[...] = (x_ref[...] + _dot(ab, w_ref[0:512, :]) + _dot(hb, w_ref[512:768, :])
                       + _dot(cb, w_ref[768:1024, :]))

    def tok(w):
        return pl.BlockSpec((tm, w), lambda i: (i, 0))

    return pl.pallas_call(
        body, name="mix_out", grid=(t // tm,),
        in_specs=[tok(512), tok(256), tok(256), tok(D_MODEL),
                  pl.BlockSpec((1, 512), lambda i: (0, 0)), pl.BlockSpec((1, 256), lambda i: (0, 0)),
                  pl.BlockSpec((D_MODEL, D_MODEL), lambda i: (0, 0))],
        out_specs=[tok(D_MODEL), tok(D_MODEL)],
        out_shape=[jax.ShapeDtypeStruct((t, D_MODEL), BF16), jax.ShapeDtypeStruct((t, D_MODEL), F32)],
        compiler_params=_cparams(("parallel",)),
    )(o_attn, y_hgrn, y_conv, x, aw, cw, w_out)


def _mix_out_bwd(dx1, w_out, o_attn, y_conv, aw, cw, tm):
    t = dx1.shape[0]

    def body(dx_ref, w_ref, o_ref, c_ref, aw_ref, cw_ref, do_ref, dh_ref, dc_ref, daw_ref, dcw_ref):
        dm = _dot_nt(dx_ref[...].astype(BF16), w_ref[...])
        do, daw = _rms_bwd(dm[:, 0:512], o_ref[...], aw_ref[...])
        dc, dcw = _rms_bwd(dm[:, 768:1024], c_ref[...], cw_ref[...])
        do_ref[...] = do
        dh_ref[...] = dm[:, 512:768]
        dc_ref[...] = dc

        @pl.when(pl.program_id(0) == 0)
        def _():
            daw_ref[...] = jnp.zeros_like(daw_ref)
            dcw_ref[...] = jnp.zeros_like(dcw_ref)

        daw_ref[...] += jnp.sum(daw, axis=0, keepdims=True)
        dcw_ref[...] += jnp.sum(dcw, axis=0, keepdims=True)

    def tok(w):
        return pl.BlockSpec((tm, w), lambda i: (i, 0))

    v512 = pl.BlockSpec((1, 512), lambda i: (0, 0))
    v256 = pl.BlockSpec((1, 256), lambda i: (0, 0))
    return pl.pallas_call(
        body, name="mix_out_bwd", grid=(t // tm,),
        in_specs=[tok(D_MODEL), pl.BlockSpec((D_MODEL, D_MODEL), lambda i: (0, 0)), tok(512), tok(256),
                  v512, v256],
        out_specs=[tok(512), tok(256), tok(256), v512, v256],
        out_shape=[jax.ShapeDtypeStruct((t, 512), F32), jax.ShapeDtypeStruct((t, 256), F32),
                   jax.ShapeDtypeStruct((t, 256), F32), jax.ShapeDtypeStruct((1, 512), F32),
                   jax.ShapeDtypeStruct((1, 256), F32)],
        compiler_params=_cparams(("arbitrary",)),
    )(dx1, w_out, o_attn, y_conv, aw, cw)


FF_TILE = 1408


def _ffn_fwd(x1, fw, wg, wu, wd, tm, ride=None):
    t = x1.shape[0]
    nf = D_FF // FF_TILE
    rd = _ride_plan(ride)
    grid = (t // tm, nf)

    def body(*refs):
        x_ref, fw_ref, wg_ref, wu_ref, wd_ref = refs[:5]
        h_ref, g_ref, u_ref, a_ref, x2_ref = refs[5 + rd.n:10 + rd.n]
        acc_ref = refs[10 + 2 * rd.n]
        copies = rd.copies(refs[5:5 + rd.n], refs[10 + rd.n:10 + 2 * rd.n], refs[11 + 2 * rd.n:])
        step_id = _grid_step_id(grid)
        _ride_start(copies, step_id == 0)
        j = pl.program_id(1)

        @pl.when(j == 0)
        def _():
            xv = x_ref[...]
            r = lax.rsqrt(jnp.mean(xv * xv, axis=-1, keepdims=True) + EPS)
            h_ref[...] = (xv * r * fw_ref[...]).astype(BF16)
            acc_ref[...] = xv

        h = h_ref[...]
        g = _dot(h, wg_ref[...])
        u = _dot(h, wu_ref[...])
        a = (g * _sigmoid(g) * u).astype(BF16)
        g_ref[...] = g.astype(BF16)
        u_ref[...] = u.astype(BF16)
        a_ref[...] = a
        acc_ref[...] += _dot(a, wd_ref[...])

        @pl.when(j == nf - 1)
        def _():
            x2_ref[...] = acc_ref[...]

        _ride_wait(copies, step_id == (t // tm) * nf - 1)

    tok = pl.BlockSpec((tm, D_MODEL), lambda i, j: (i, 0))
    ffb = pl.BlockSpec((tm, FF_TILE), lambda i, j: (i, j))
    ffs = jax.ShapeDtypeStruct((t, D_FF), BF16)
    return pl.pallas_call(
        body, name="ffn_fwd", grid=grid,
        in_specs=[tok, pl.BlockSpec((1, D_MODEL), lambda i, j: (0, 0)),
                  pl.BlockSpec((D_MODEL, FF_TILE), lambda i, j: (0, j)),
                  pl.BlockSpec((D_MODEL, FF_TILE), lambda i, j: (0, j)),
                  pl.BlockSpec((FF_TILE, D_MODEL), lambda i, j: (j, 0))] + rd.in_specs,
        out_specs=[tok, ffb, ffb, ffb, tok] + rd.out_specs,
        out_shape=[jax.ShapeDtypeStruct((t, D_MODEL), BF16), ffs, ffs, ffs,
                   jax.ShapeDtypeStruct((t, D_MODEL), F32)] + rd.out_shape,
        scratch_shapes=[pltpu.VMEM((tm, D_MODEL), F32)] + rd.scratch,
        compiler_params=_cparams(("arbitrary", "arbitrary")),
    )(x1, fw, wg, wu, wd, *rd.srcs)


def _ffn_bwd(dx2, g, u, wg, wu, wd, x1, fw, tm):
    t = dx2.shape[0]
    nf = D_FF // FF_TILE

    def body(dx_ref, g_ref, u_ref, wg_ref, wu_ref, wd_ref, x_ref, fw_ref,
             dg_ref, du_ref, dx1_ref, dfw_ref, acc_ref):
        i = pl.program_id(0)
        j = pl.program_id(1)
        da = _dot_nt(dx_ref[...].astype(BF16), wd_ref[...])
        gv = g_ref[...].astype(F32)
        uv = u_ref[...].astype(F32)
        sg = _sigmoid(gv)
        dg = (da * uv * (sg * (1.0 + gv * (1.0 - sg)))).astype(BF16)
        du = (da * gv * sg).astype(BF16)
        dg_ref[...] = dg
        du_ref[...] = du
        dh = _dot_nt(dg, wg_ref[...]) + _dot_nt(du, wu_ref[...])

        @pl.when(j == 0)
        def _():
            acc_ref[...] = dh

        @pl.when(j > 0)
        def _():
            acc_ref[...] += dh

        @pl.when((i == 0) & (j == 0))
        def _():
            dfw_ref[...] = jnp.zeros_like(dfw_ref)

        @pl.when(j == nf - 1)
        def _():
            dx, dfw = _rms_bwd(acc_ref[...], x_ref[...], fw_ref[...])
            dx1_ref[...] = dx_ref[...] + dx
            dfw_ref[...] += jnp.sum(dfw, axis=0, keepdims=True)

    tok = pl.BlockSpec((tm, D_MODEL), lambda i, j: (i, 0))
    ffb = pl.BlockSpec((tm, FF_TILE), lambda i, j: (i, j))
    ffs = jax.ShapeDtypeStruct((t, D_FF), BF16)
    vec = pl.BlockSpec((1, D_MODEL), lambda i, j: (0, 0))
    return pl.pallas_call(
        body, name="ffn_bwd", grid=(t // tm, nf),
        in_specs=[tok, ffb, ffb,
                  pl.BlockSpec((D_MODEL, FF_TILE), lambda i, j: (0, j)),
                  pl.BlockSpec((D_MODEL, FF_TILE), lambda i, j: (0, j)),
                  pl.BlockSpec((FF_TILE, D_MODEL), lambda i, j: (j, 0)),
                  tok, vec],
        out_specs=[ffb, ffb, tok, vec],
        out_shape=[ffs, ffs, jax.ShapeDtypeStruct((t, D_MODEL), F32), jax.ShapeDtypeStruct((1, D_MODEL), F32)],
        scratch_shapes=[pltpu.VMEM((tm, D_MODEL), F32)],
        compiler_params=_cparams(("arbitrary", "arbitrary")),
    )(dx2, g, u, wg, wu, wd, x1, fw)


def _loss_grad(y, target, tm):
    t, d = y.shape

    def body(y_ref, t_ref, dy_ref, loss_ref):
        err = y_ref[...] - t_ref[...]
        dy_ref[...] = err * (1.0 / d)

        @pl.when(pl.program_id(0) == 0)
        def _():
            loss_ref[...] = jnp.zeros_like(loss_ref)

        part = jnp.sum(jnp.sum(err * err, axis=-1, keepdims=True), axis=0, keepdims=True)
        loss_ref[...] += part * (0.5 / d)

    tok = pl.BlockSpec((tm, d), lambda i: (i, 0))
    return pl.pallas_call(
        body, name="loss_grad", grid=(t // tm,),
        in_specs=[tok, tok],
        out_specs=[tok, pl.BlockSpec((1, 1), lambda i: (0, 0))],
        out_shape=[jax.ShapeDtypeStruct((t, d), F32), jax.ShapeDtypeStruct((1, 1), F32)],
        compiler_params=_cparams(("arbitrary",)),
    )(y, target)


def _tile(v, reps):
    return jnp.tile(v.reshape(1, -1), (1, reps))


class _LocalPlan:
    def __init__(self, wb):
        self.w = [{n: wb[n][l] for n in BIG_AXIS} for l in range(DEPTH)]

    def ride(self, kernel_name, l, grads=None):
        return None

    def done(self, kernel_name, l, outs):
        pass


def _local_step(x, target, p, plan):
    nb, s, d = x.shape
    t = nb * s
    tm = min(512, s)
    tq = min(512, s)
    xf = x.reshape(t, d)
    cosq, sinq = _rope_tables(s)
    ones512 = _block_ones(512, HEAD_DIM)
    lbs = _lower_bounds(p["hgrn_lb_logits"].reshape(DEPTH, 2 * D_HGRN)).reshape(DEPTH, 2, D_HGRN)

    saved = []
    cur = xf
    wb = plan.w
    for l in range(DEPTH):
        qw = _tile(p["q_norm_w"][l], N_HEADS)
        kw = _tile(p["k_norm_w"][l], N_KV)
        gw = _tile(p["hgrn_gnorm_w"][l], 2)
        dww = jnp.pad(p["conv_dw_w"][l], ((0, 1), (0, 0)))
        pww = p["conv_pw_w"][l].astype(BF16)
        h0, proj = _rms_proj(cur, _row(p["mix_norm_w"][l]), wb[l]["w_in"], tm)
        qr, kd, vd, kdt, vdt = _qkv_prep(proj, cosq, sinq, qw, kw, ones512, s, tm)
        o_attn, lse, *rode = _attn_fwd(qr, kd, vdt, nb, s, tq, plan.ride("attn_fwd", l))
        plan.done("attn_fwd", l, rode)
        y_hgrn, osum, *rode = _hgrn_fwd(proj, lbs[l], gw, nb, s, plan.ride("hgrn_fwd", l))
        plan.done("hgrn_fwd", l, rode)
        y_conv = _conv_fwd(proj, dww, _row(p["conv_dw_b"][l]), _row(p["conv_ln_w"][l]),
                           _row(p["conv_ln_b"][l]), pww, _row(p["conv_pw_b"][l]), nb, s)
        mixed, x1 = _mix_out(o_attn, y_hgrn, y_conv, cur, _row(p["attn_out_norm_w"][l]),
                             _row(p["conv_out_norm_w"][l]), wb[l]["w_out"], tm)
        hf, g, u, a, x2, *rode = _ffn_fwd(x1, _row(p["ffn_norm_w"][l]), wb[l]["w_gate"], wb[l]["w_up"],
                                          wb[l]["w_down"], tm, plan.ride("ffn_fwd", l))
        plan.done("ffn_fwd", l, rode)
        saved.append(dict(x=cur, h0=h0, proj=proj, qr=qr, kd=kd, vd=vd, kdt=kdt, o_attn=o_attn, lse=lse,
                          osum=osum, y_conv=y_conv, mixed=mixed, x1=x1, hf=hf, g=g, u=u, a=a,
                          qw=qw, kw=kw, gw=gw, dww=dww, pww=pww))
        cur = x2

    dcur, loss = _loss_grad(cur, target.reshape(t, d), tm)

    grads = {k: [None] * DEPTH for k in WEIGHTS}
    dlb = [None] * DEPTH
    for l in reversed(range(DEPTH)):
        sv = saved[l]
        dg, du, dx1, dfw = _ffn_bwd(dcur, sv["g"], sv["u"], wb[l]["w_gate"], wb[l]["w_up"], wb[l]["w_down"],
                                    sv["x1"], _row(p["ffn_norm_w"][l]), tm)
        grads["ffn_norm_w"][l] = dfw[0]
        grads["w_gate"][l] = _mm_tn(sv["hf"], dg, FF_TILE, "dw_gate", tm)
        grads["w_up"][l] = _mm_tn(sv["hf"], du, FF_TILE, "dw_up", tm)
        grads["w_down"][l] = _mm_tn(sv["a"], dcur, 512, "dw_down", tm)
        do_attn, dy_hgrn, dy_conv, daw, dcw = _mix_out_bwd(
            dx1, wb[l]["w_out"], sv["o_attn"], sv["y_conv"], _row(p["attn_out_norm_w"][l]),
            _row(p["conv_out_norm_w"][l]), tm)
        grads["attn_out_norm_w"][l] = daw[0]
        grads["conv_out_norm_w"][l] = dcw[0]
        grads["w_out"][l] = _mm_tn(sv["mixed"], dx1, D_MODEL, "dw_out", tm)
        dq, dkd, dvd, *rode = _attn_bwd(sv["qr"], sv["kd"], sv["vd"], sv["kdt"], sv["o_attn"], sv["lse"], do_attn,
                                        nb, s, tq, plan.ride("attn_bwd", l, grads))
        plan.done("attn_bwd", l, rode)
        dqkv, dqw, dkw = _qkv_bwd(sv["proj"], dq, dkd, dvd, cosq, sinq, sv["qw"], sv["kw"], ones512, s, tm)
        grads["q_norm_w"][l] = dqw.reshape(N_HEADS, HEAD_DIM).sum(0)
        grads["k_norm_w"][l] = dkw.reshape(N_KV, HEAD_DIM).sum(0)
        dhq, dzf, dzb, dhi, dhg, dgw, dlb_l, *rode = _hgrn_bwd(sv["proj"], lbs[l], sv["gw"], sv["osum"], dy_hgrn,
                                                               nb, s, plan.ride("hgrn_bwd", l, grads))
        plan.done("hgrn_bwd", l, rode)
        grads["hgrn_gnorm_w"][l] = dgw.reshape(nb * D_HGRN // HEAD_DIM, HEAD_DIM).sum(0)
        dlb[l] = dlb_l.sum(0)
        dab, ddww, ddwb, dlnw, dlnb, dpww, dpwb = _conv_bwd(
            sv["proj"], dy_conv, sv["dww"], _row(p["conv_dw_b"][l]), _row(p["conv_ln_w"][l]),
            _row(p["conv_ln_b"][l]), sv["pww"], nb, s)
        grads["conv_dw_w"][l] = ddww[:CONV_W]
        grads["conv_dw_b"][l] = ddwb[0]
        grads["conv_ln_w"][l] = dlnw[0]
        grads["conv_ln_b"][l] = dlnb[0]
        grads["conv_pw_w"][l] = dpww
        grads["conv_pw_b"][l] = dpwb[0]
        pieces = [dqkv, dhq, dzf, dzb, dhi, dhg, dab]
        grads["w_in"][l] = _dw_in(sv["h0"], pieces, tm)
        dcur, dnw = _proj_bwd(pieces, wb[l]["w_in"], sv["x"], _row(p["mix_norm_w"][l]), dx1, tm)
        grads["mix_norm_w"][l] = dnw[0]

    dlog = _lower_bounds_bwd(p["hgrn_lb_logits"].reshape(DEPTH, 2 * D_HGRN),
                             jnp.stack(dlb).reshape(DEPTH, 2 * D_HGRN))
    out = {k: (v if k in BIG_AXIS else jnp.stack(v)) for k, v in grads.items() if k != "hgrn_lb_logits"}
    out["hgrn_lb_logits"] = dlog.reshape(DEPTH, 2, D_HGRN)
    return loss, dcur.reshape(nb, s, d), out


BIG_AXIS = {"w_in": 2, "w_out": 1, "w_gate": 2, "w_up": 2, "w_down": 1}
SMALL_SHARD_AXIS = {"hgrn_lb_logits": 2, "conv_dw_w": 2, "conv_pw_w": 1}
WEIGHTS = ("mix_norm_w", "w_in", "q_norm_w", "k_norm_w", "hgrn_lb_logits", "hgrn_gnorm_w", "conv_dw_w",
           "conv_dw_b", "conv_ln_w", "conv_ln_b", "conv_pw_w", "conv_pw_b", "attn_out_norm_w",
           "conv_out_norm_w", "w_out", "ffn_norm_w", "w_gate", "w_up", "w_down")
SMALL = tuple(n for n in WEIGHTS if n not in BIG_AXIS)


def _my_index():
    return 4 * lax.axis_index("x") + 2 * lax.axis_index("y") + lax.axis_index("c")


class _RidePlan:
    def __init__(self, srcs, gather):
        self.srcs = list(srcs)
        self.n = len(self.srcs)
        self.gather = list(gather) if isinstance(gather, (list, tuple)) else [gather] * self.n
        any_spec = pl.BlockSpec(memory_space=pl.ANY)
        self.in_specs = [any_spec] * self.n
        self.out_specs = [any_spec] * self.n
        self.out_shape = [jax.ShapeDtypeStruct(((N_DEV,) + s.shape) if g else s.shape, s.dtype)
                          for s, g in zip(self.srcs, self.gather)]
        npeer = N_DEV - 1
        self.scratch = [pltpu.SemaphoreType.DMA((self.n * npeer,)), pltpu.SemaphoreType.DMA((self.n * npeer,)),
                        pltpu.SemaphoreType.DMA((self.n,))] if self.n else []

    def copies(self, src_refs, out_refs, sems):
        if not self.n:
            return [], [], []
        send_sems, recv_sems, local_sems = sems
        npeer = N_DEV - 1
        x, y, c = lax.axis_index("x"), lax.axis_index("y"), lax.axis_index("c")
        me = 4 * x + 2 * y + c
        locals_, sends, recvs = [], [], []
        for a in range(self.n):
            src_ref, out_ref = src_refs[a], out_refs[a]

            def rows_for(j, src_ref=src_ref, gather=self.gather[a]):
                return src_ref if gather else src_ref.at[j]

            locals_.append(pltpu.make_async_copy(rows_for(me), out_ref.at[me], local_sems.at[a]))
            for k in range(1, N_DEV):
                px = (1 - x) if (k & 4) else x
                py = (1 - y) if (k & 2) else y
                pc = (1 - c) if (k & 1) else c
                pidx = 4 * px + 2 * py + pc
                common = dict(send_sem=send_sems.at[a * npeer + k - 1], recv_sem=recv_sems.at[a * npeer + k - 1],
                              device_id=(px, py, pc), device_id_type=pl.DeviceIdType.MESH)
                sends.append(pltpu.make_async_remote_copy(src_ref=rows_for(pidx), dst_ref=out_ref.at[me], **common))
                recvs.append(pltpu.make_async_remote_copy(src_ref=rows_for(pidx), dst_ref=out_ref.at[pidx],
                                                          **common))
        return locals_, sends, recvs


def _ride_plan(ride):
    return _RidePlan(*ride) if ride else _RidePlan([], True)


def _ride_start(copies, when=None):
    locals_, sends, _ = copies

    def go():
        for cp in locals_ + sends:
            cp.start()

    if locals_:
        go() if when is None else pl.when(when)(go)


def _ride_wait(copies, when=None):
    locals_, sends, recvs = copies

    def go():
        for cp in recvs:
            cp.wait_recv()
        for cp in sends:
            cp.wait_send()
        for cp in locals_:
            cp.wait()

    if locals_:
        go() if when is None else pl.when(when)(go)


def _exchange(srcs, gather, name):
    rd = _RidePlan(srcs, gather)

    def body(*refs):
        copies = rd.copies(refs[:rd.n], refs[rd.n:2 * rd.n], refs[2 * rd.n:])
        _ride_start(copies)
        _ride_wait(copies)

    return pl.pallas_call(body, name=name, in_specs=rd.in_specs, out_specs=rd.out_specs,
                          out_shape=rd.out_shape, scratch_shapes=rd.scratch)(*srcs)


def _lane_group(n):
    g = 1
    while (g * n) % LANES:
        g += 1
    return g


def _cols_to_natural(gathered, name):
    _, k, n = gathered.shape
    grp = _lane_group(n)
    place = jnp.stack([jnp.asarray(np.eye(n, grp * n, k=i * n), BF16) for i in range(grp)])

    def body(g_ref, p_ref, o_ref):
        acc = None
        for i in range(grp):
            part = _dot(g_ref[i], p_ref[i])
            acc = part if acc is None else acc + part
        o_ref[...] = acc.astype(BF16)

    return pl.pallas_call(
        body, name=name, grid=(N_DEV // grp,),
        in_specs=[pl.BlockSpec((grp, k, n), lambda j: (j, 0, 0)),
                  pl.BlockSpec((grp, n, grp * n), lambda j: (0, 0, 0))],
        out_specs=pl.BlockSpec((k, grp * n), lambda j: (0, j)),
        out_shape=jax.ShapeDtypeStruct((k, N_DEV * n), BF16),
        compiler_params=_cparams(("parallel",)),
    )(gathered, place)


def _natural_to_cols(dw, name):
    k, n8 = dw.shape
    n = n8 // N_DEV
    grp = _lane_group(n)
    pick = jnp.stack([jnp.asarray(np.eye(grp * n, n, k=-i * n), BF16) for i in range(grp)])

    def body(d_ref, p_ref, o_ref):
        xb = d_ref[...].astype(BF16)
        for i in range(grp):
            o_ref[i] = _dot(xb, p_ref[i]).astype(BF16)

    return pl.pallas_call(
        body, name=name, grid=(N_DEV // grp,),
        in_specs=[pl.BlockSpec((k, grp * n), lambda j: (0, j)),
                  pl.BlockSpec((grp, grp * n, n), lambda j: (0, 0, 0))],
        out_specs=pl.BlockSpec((grp, k, n), lambda j: (j, 0, 0)),
        out_shape=jax.ShapeDtypeStruct((N_DEV, k, n), BF16),
        compiler_params=_cparams(("parallel",)),
    )(dw, pick)


def _adamw_math(w, g, m, v):
    m = ADAM_B1 * m + (1.0 - ADAM_B1) * g
    v = ADAM_B2 * v + (1.0 - ADAM_B2) * (g * g)
    m_hat = m / (1.0 - ADAM_B1 ** ADAM_STEP)
    v_hat = v / (1.0 - ADAM_B2 ** ADAM_STEP)
    delta = -ADAM_LR * (m_hat / (jnp.sqrt(v_hat) + ADAM_EPS) + ADAM_WD * w)
    return delta, m, v


def _sum_adamw(parts, w, m, v, name):
    _, k, n = w.shape
    tk = k
    for cand in (256, 176, 128):
        if k % cand == 0:
            tk = cand
            break

    def body(*refs):
        p_refs = refs[:DEPTH]
        w_ref, m_ref, v_ref, g_ref, d_ref, mo_ref, vo_ref = refs[DEPTH:]
        for l in range(DEPTH):
            @pl.when(pl.program_id(0) == l)
            def _(p_ref=p_refs[l]):
                g = p_ref[0].astype(F32)
                for i in range(1, N_DEV):
                    g = g + p_ref[i].astype(F32)
                g_ref[...] = g
                d_ref[...], mo_ref[...], vo_ref[...] = _adamw_math(w_ref[...], g, m_ref[...], v_ref[...])

    row = pl.BlockSpec((None, tk, n), lambda l, i: (l, i, 0))
    shp = jax.ShapeDtypeStruct(w.shape, F32)
    return pl.pallas_call(
        body, name=name, grid=(DEPTH, k // tk),
        in_specs=[pl.BlockSpec((N_DEV, tk, n), lambda l, i: (0, i, 0))] * DEPTH + [row, row, row],
        out_specs=[row, row, row, row],
        out_shape=[shp, shp, shp, shp],
        compiler_params=_cparams(("parallel", "parallel")),
    )(*parts, w, m, v)


def _sum8(parts):
    r = parts.shape[1]

    def body(p_ref, g_ref):
        g = p_ref[0]
        for i in range(1, N_DEV):
            g = g + p_ref[i]
        g_ref[...] = g

    return pl.pallas_call(body, name="sum_small_grads", out_shape=jax.ShapeDtypeStruct((r, LANES), F32))(parts)


def _adamw(w, g, m, v):
    def body(w_ref, g_ref, m_ref, v_ref, d_ref, mo_ref, vo_ref):
        d_ref[...], mo_ref[...], vo_ref[...] = _adamw_math(w_ref[...], g_ref[...], m_ref[...], v_ref[...])

    shp = jax.ShapeDtypeStruct(w.shape, F32)
    return pl.pallas_call(body, name="adamw_small", out_shape=[shp, shp, shp])(w, g, m, v)


def _pack(arrays, dtype, row_multiple):
    flat = jnp.concatenate([a.reshape(-1).astype(dtype) for a in arrays])
    n = flat.shape[0]
    unit = row_multiple * LANES
    total = -(-n // unit) * unit
    return jnp.pad(flat, (0, total - n)).reshape(total // LANES, LANES)


def _unpack(flat2d, shapes, lead=()):
    flat = flat2d.reshape(lead + (-1,))
    out, off = [], 0
    for shp in shapes:
        n = int(np.prod(shp))
        out.append(flat[..., off:off + n].reshape(lead + tuple(shp)))
        off += n
    return out


def _shard_to_rows(full, axis):
    shp = full.shape
    k = shp[axis] // N_DEV
    r = full.reshape(shp[:axis] + (N_DEV, k) + shp[axis + 1:])
    return jnp.moveaxis(r, axis, 0)


def _rows_to_full(rows, axis):
    r = jnp.moveaxis(rows, 0, axis)
    shp = r.shape
    return r.reshape(shp[:axis] + (shp[axis] * shp[axis + 1],) + shp[axis + 2:])


def kernel(x, mix_norm_w, w_in, q_norm_w, k_norm_w, hgrn_lb_logits, hgrn_gnorm_w, conv_dw_w, conv_dw_b, conv_ln_w, conv_ln_b, conv_pw_w, conv_pw_b, attn_out_norm_w, conv_out_norm_w, w_out, ffn_norm_w, w_gate, w_up, w_down, loss_target, m_mix_norm_w, m_w_in, m_q_norm_w, m_k_norm_w, m_hgrn_lb_logits, m_hgrn_gnorm_w, m_conv_dw_w, m_conv_dw_b, m_conv_ln_w, m_conv_ln_b, m_conv_pw_w, m_conv_pw_b, m_attn_out_norm_w, m_conv_out_norm_w, m_w_out, m_ffn_norm_w, m_w_gate, m_w_up, m_w_down, v_mix_norm_w, v_w_in, v_q_norm_w, v_k_norm_w, v_hgrn_lb_logits, v_hgrn_gnorm_w, v_conv_dw_w, v_conv_dw_b, v_conv_ln_w, v_conv_ln_b, v_conv_pw_w, v_conv_pw_b, v_attn_out_norm_w, v_conv_out_norm_w, v_w_out, v_ffn_norm_w, v_w_gate, v_w_up, v_w_down):
    w_loc = dict(zip(WEIGHTS, (mix_norm_w, w_in, q_norm_w, k_norm_w, hgrn_lb_logits, hgrn_gnorm_w, conv_dw_w,
                               conv_dw_b, conv_ln_w, conv_ln_b, conv_pw_w, conv_pw_b, attn_out_norm_w,
                               conv_out_norm_w, w_out, ffn_norm_w, w_gate, w_up, w_down)))
    m_loc = dict(zip(WEIGHTS, (m_mix_norm_w, m_w_in, m_q_norm_w, m_k_norm_w, m_hgrn_lb_logits, m_hgrn_gnorm_w,
                               m_conv_dw_w, m_conv_dw_b, m_conv_ln_w, m_conv_ln_b, m_conv_pw_w, m_conv_pw_b,
                               m_attn_out_norm_w, m_conv_out_norm_w, m_w_out, m_ffn_norm_w, m_w_gate, m_w_up,
                               m_w_down)))
    v_loc = dict(zip(WEIGHTS, (v_mix_norm_w, v_w_in, v_q_norm_w, v_k_norm_w, v_hgrn_lb_logits, v_hgrn_gnorm_w,
                               v_conv_dw_w, v_conv_dw_b, v_conv_ln_w, v_conv_ln_b, v_conv_pw_w, v_conv_pw_b,
                               v_attn_out_norm_w, v_conv_out_norm_w, v_w_out, v_ffn_norm_w, v_w_gate, v_w_up,
                               v_w_down)))
    me = _my_index()
    big = tuple(BIG_AXIS)
    sms = tuple(SMALL_SHARD_AXIS)

    sm_shapes = [w_loc[n].shape for n in sms]
    got_s = _exchange([_pack([w_loc[n] for n in sms], F32, 8)], True, "gather_small_params")[0]
    p_full = {n: w_loc[n] for n in SMALL if n not in SMALL_SHARD_AXIS}
    for n, a in zip(sms, _unpack(got_s, sm_shapes, (N_DEV,))):
        p_full[n] = _rows_to_full(a, SMALL_SHARD_AXIS[n])

    def natural(n, gathered):
        if BIG_AXIS[n] == 2:
            return _cols_to_natural(gathered, "relayout_" + n)
        return gathered.reshape(-1, gathered.shape[-1])

    def to_send(n, gl):
        if BIG_AXIS[n] == 2:
            return _natural_to_cols(gl, "split_d" + n)
        return gl.reshape(N_DEV, gl.shape[0] // N_DEV, gl.shape[1]).astype(BF16)

    class StepPlan:
        def __init__(self):
            self.w = [dict() for _ in range(DEPTH)]
            self.parts = [dict() for _ in range(DEPTH)]
            self.pending = {}
            got = _exchange([w_loc["w_in"][0].astype(BF16)], True, "gather_w_in")
            self.w[0]["w_in"] = natural("w_in", got[0])

        def ride(self, kernel_name, l, grads=None):
            want = []
            if kernel_name == "attn_fwd":
                want = [("w_out", l), ("w_gate", l)]
            elif kernel_name == "hgrn_fwd":
                want = [("w_up", l), ("w_down", l)]
            elif kernel_name == "ffn_fwd" and l + 1 < DEPTH:
                want = [("w_in", l + 1)]
            elif kernel_name == "attn_bwd" and l + 1 < DEPTH:
                want = [("w_gate", l + 1), ("w_up", l + 1)]
            elif kernel_name == "hgrn_bwd" and l + 1 < DEPTH:
                want = [("w_in", l + 1), ("w_out", l + 1), ("w_down", l + 1)]
                if l == 0:
                    want += [(n, 0) for n in ("w_gate", "w_up", "w_down", "w_out")]
            if not want:
                return None
            self.pending[(kernel_name, l)] = want
            if grads is None:
                return [w_loc[n][wl].astype(BF16) for n, wl in want], True
            return [to_send(n, grads[n][wl]) for n, wl in want], False

        def done(self, kernel_name, l, outs):
            want = self.pending.pop((kernel_name, l), [])
            for (n, wl), out in zip(want, outs):
                if kernel_name.endswith("_fwd"):
                    self.w[wl][n] = natural(n, out)
                else:
                    self.parts[wl][n] = out

    plan = StepPlan()
    loss_part, grad_x, g = _local_step(x, loss_target, p_full, plan)
    loss = lax.psum(loss_part[0, 0], MESH_AXES)

    plan.parts[0]["w_in"] = _exchange([to_send("w_in", g["w_in"][0])], False, "exchange_dw_in")[0]
    big_out = {n: _sum_adamw([plan.parts[l][n] for l in range(DEPTH)], w_loc[n], m_loc[n], v_loc[n],
                             "sum_adamw_" + n) for n in big}

    small_shapes = [g[n].shape for n in SMALL]
    parts = _exchange([_pack([g[n] for n in SMALL], F32, 8)], True, "gather_small_grads")[0]
    g_small = dict(zip(SMALL, _unpack(_sum8(parts), small_shapes)))
    for n in sms:
        ax = SMALL_SHARD_AXIS[n]
        k = w_loc[n].shape[ax]
        g_small[n] = lax.dynamic_slice_in_dim(g_small[n], me * k, k, axis=ax)
    loc_shapes = [w_loc[n].shape for n in SMALL]
    packed = [_pack([d[n] for n in SMALL], F32, 8) for d in (w_loc, g_small, m_loc, v_loc)]
    res = _adamw(*packed)
    small_out = [g_small] + [dict(zip(SMALL, _unpack(r, loc_shapes))) for r in res]

    def pick(i, n):
        return big_out[n][i] if n in BIG_AXIS else small_out[i][n]

    return (loss, grad_x) + tuple(pick(i, n) for i in range(4) for n in WEIGHTS)
```

```python
import functools

import jax
import jax.numpy as jnp
import numpy as np
from jax import lax
from jax.experimental import pallas as pl
from jax.experimental.pallas import tpu as pltpu

F32 = jnp.float32
BF16 = jnp.bfloat16

D_MODEL = 1024
D_ATTN = 512
D_HGRN = 256
D_CONV = 256
HEAD_DIM = 64
N_HEADS = 8
N_KV = 2
GRID_W = 64
ROPE_THETA = 10000.0
CHUNK = 64
F_MIN = 1e-6
CONV_W = 31
CONV_PAD = 15
D_FF = 2816
D_PROJ = 2560
EPS = 1e-6
LN_EPS = 1e-5
DEPTH = 2
ADAM_LR = 0.001
ADAM_B1 = 0.9
ADAM_B2 = 0.999
ADAM_EPS = 1e-08
ADAM_WD = 0.01
ADAM_STEP = 10
N_DEV = 8
MESH_AXES = ("x", "y", "c")

COL_HQ, COL_ZFW, COL_ZBW, COL_HI, COL_HG = 6, 8, 10, 12, 14
COL_CA, COL_CB = 8, 9

LANES = 128
VMEM_LIMIT_MB = 56


def _cparams(dims=None):
    return pltpu.CompilerParams(dimension_semantics=dims, vmem_limit_bytes=VMEM_LIMIT_MB * 2 ** 20)


def _dot(a, b):
    return jnp.dot(a, b, preferred_element_type=F32)


def _dot_nt(a, b):
    return lax.dot_general(a, b, (((1,), (1,)), ((), ())), preferred_element_type=F32)


def _dot_tn(a, b):
    return lax.dot_general(a, b, (((0,), (0,)), ((), ())), preferred_element_type=F32)


def _split_bf16(x, parts):
    out = []
    r = x
    for _ in range(parts):
        p = r.astype(BF16)
        out.append(p)
        r = r - p.astype(F32)
    return out


def _dot_precise(x, m_bf16, parts=3):
    acc = None
    for p in _split_bf16(x, parts):
        t = _dot(p, m_bf16)
        acc = t if acc is None else acc + t
    return acc


def _block_ones(width, group):
    i = np.arange(width)
    return jnp.asarray((i[:, None] // group) == (i[None, :] // group), dtype=BF16)


def _sigmoid(x):
    return 1.0 / (1.0 + jnp.exp(-x))


def _rot(x):
    w = x.shape[1]
    lane = lax.broadcasted_iota(jnp.int32, x.shape, 1)
    first = (lane % 32) < 16
    return jnp.where(first, -pltpu.roll(x, w - 16, 1), pltpu.roll(x, 16, 1))


def _rope(x, cos, sin):
    return x * cos + _rot(x) * sin


def _rope_t(dy, cos, sin):
    return dy * cos - _rot(dy * sin)


def _row(v):
    return v.reshape(1, -1)


def _rms_proj(x, wn, w, tm):
    t, d = x.shape
    n = w.shape[1]

    def body(x_ref, wn_ref, w_ref, h_ref, y_ref):
        xv = x_ref[...]
        r = lax.rsqrt(jnp.mean(xv * xv, axis=-1, keepdims=True) + EPS)
        h = (xv * r * wn_ref[...]).astype(BF16)
        h_ref[...] = h
        y_ref[...] = _dot(h, w_ref[...])

    return pl.pallas_call(
        body, name="rms_proj", grid=(t // tm,),
        in_specs=[pl.BlockSpec((tm, d), lambda i: (i, 0)),
                  pl.BlockSpec((1, d), lambda i: (0, 0)),
                  pl.BlockSpec((d, n), lambda i: (0, 0))],
        out_specs=[pl.BlockSpec((tm, d), lambda i: (i, 0)),
                   pl.BlockSpec((tm, n), lambda i: (i, 0))],
        out_shape=[jax.ShapeDtypeStruct((t, d), BF16), jax.ShapeDtypeStruct((t, n), F32)],
        compiler_params=_cparams(("parallel",)),
    )(x, wn, w)


def _rms_bwd(dh, x, wn):
    r = lax.rsqrt(jnp.mean(x * x, axis=-1, keepdims=True) + EPS)
    g = dh * wn
    dx = r * (g - x * (r * r) * jnp.mean(g * x, axis=-1, keepdims=True))
    return dx, dh * x * r


def _proj_bwd(pieces, w, x, wn, dres, tm):
    t = x.shape[0]
    d = x.shape[1]
    n = w.shape[1]
    widths = [p.shape[1] for p in pieces]
    offs = [sum(widths[:i]) for i in range(len(widths))]
    assert sum(widths) == n
    npc = len(pieces)

    def body(*refs):
        p_refs = refs[:npc]
        w_ref, x_ref, wn_ref, dr_ref, dx_ref, dwn_ref = refs[npc:]
        dh = None
        for p_ref, o, wd in zip(p_refs, offs, widths):
            part = _dot_nt(p_ref[...], w_ref[:, o:o + wd])
            dh = part if dh is None else dh + part
        dx, dwn = _rms_bwd(dh, x_ref[...], wn_ref[...])
        dx_ref[...] = dr_ref[...] + dx

        @pl.when(pl.program_id(0) == 0)
        def _():
            dwn_ref[...] = jnp.zeros_like(dwn_ref)

        dwn_ref[...] += jnp.sum(dwn, axis=0, keepdims=True)

    return pl.pallas_call(
        body, name="proj_bwd", grid=(t // tm,),
        in_specs=[pl.BlockSpec((tm, wd), lambda i: (i, 0)) for wd in widths]
        + [pl.BlockSpec((d, n), lambda i: (0, 0)),
           pl.BlockSpec((tm, d), lambda i: (i, 0)),
           pl.BlockSpec((1, d), lambda i: (0, 0)),
           pl.BlockSpec((tm, d), lambda i: (i, 0))],
        out_specs=[pl.BlockSpec((tm, d), lambda i: (i, 0)),
                   pl.BlockSpec((1, d), lambda i: (0, 0))],
        out_shape=[jax.ShapeDtypeStruct((t, d), F32), jax.ShapeDtypeStruct((1, d), F32)],
        compiler_params=_cparams(("arbitrary",)),
    )(*pieces, w, x, wn, dres)


def _dw_in(h0, pieces, tm):
    t, k = h0.shape
    widths = [p.shape[1] for p in pieces]
    offs = [sum(widths[:i]) for i in range(len(widths))]
    n = sum(widths)
    npc = len(pieces)

    def body(*refs):
        h_ref = refs[0]
        p_refs = refs[1:1 + npc]
        o_ref = refs[1 + npc]

        @pl.when(pl.program_id(0) == 0)
        def _():
            o_ref[...] = jnp.zeros_like(o_ref)

        ht = h_ref[...].astype(F32).T.astype(BF16)
        for p_ref, o, wd in zip(p_refs, offs, widths):
            o_ref[:, o:o + wd] += _dot(ht, p_ref[...])

    return pl.pallas_call(
        body, name="dw_in", grid=(t // tm,),
        in_specs=[pl.BlockSpec((tm, k), lambda i: (i, 0))]
        + [pl.BlockSpec((tm, wd), lambda i: (i, 0)) for wd in widths],
        out_specs=pl.BlockSpec((k, n), lambda i: (0, 0)),
        out_shape=jax.ShapeDtypeStruct((k, n), F32),
        compiler_params=_cparams(("arbitrary",)),
    )(h0, *pieces)


def _mm_tn(a, b, tn, name, tm):
    t, k = a.shape
    n = b.shape[1]

    def body(a_ref, b_ref, o_ref):
        @pl.when(pl.program_id(1) == 0)
        def _():
            o_ref[...] = jnp.zeros_like(o_ref)

        o_ref[...] += _dot_tn(a_ref[...].astype(BF16), b_ref[...].astype(BF16))

    return pl.pallas_call(
        body, name=name, grid=(n // tn, t // tm),
        in_specs=[pl.BlockSpec((tm, k), lambda j, i: (i, 0)),
                  pl.BlockSpec((tm, tn), lambda j, i: (i, j))],
        out_specs=pl.BlockSpec((k, tn), lambda j, i: (0, j)),
        out_shape=jax.ShapeDtypeStruct((k, n), F32),
        compiler_params=_cparams(("parallel", "arbitrary")),
    )(a, b)


def _dw_ff(h, d, rows, name, tm):
    t = d.shape[0] if rows else h.shape[0]
    blocked, flat = (h, d) if rows else (d, h)
    fb = blocked.shape[2]
    dm = flat.shape[1]
    out_blk = (N_DEV, fb, dm) if rows else (N_DEV, dm, fb)

    def body(h_ref, d_ref, o_ref, acc_ref):
        i = pl.program_id(0)

        @pl.when(i == 0)
        def _():
            acc_ref[...] = jnp.zeros_like(acc_ref)

        if rows:
            db = d_ref[...].astype(BF16)
            for j in range(N_DEV):
                acc_ref[j] += _dot_tn(h_ref[j], db)
        else:
            ht = h_ref[...].astype(F32).T.astype(BF16)
            for j in range(N_DEV):
                acc_ref[j] += _dot(ht, d_ref[j])

        @pl.when(i == t // tm - 1)
        def _():
            o_ref[...] = acc_ref[...].astype(BF16)

    blk_spec = pl.BlockSpec((N_DEV, tm, fb), lambda i: (0, i, 0))
    flat_spec = pl.BlockSpec((tm, dm), lambda i: (i, 0))
    return pl.pallas_call(
        body, name=name, grid=(t // tm,),
        in_specs=[blk_spec, flat_spec] if rows else [flat_spec, blk_spec],
        out_specs=pl.BlockSpec(out_blk, lambda i: (0, 0, 0)),
        out_shape=jax.ShapeDtypeStruct(out_blk, BF16),
        scratch_shapes=[pltpu.VMEM(out_blk, F32)],
        compiler_params=_cparams(("arbitrary",)),
    )(h, d)


def _rope_tables(s):
    rows = s // GRID_W
    row_id = jnp.repeat(jnp.arange(rows, dtype=F32), GRID_W)
    col_id = jnp.tile(jnp.arange(GRID_W, dtype=F32), rows)
    half = HEAD_DIM // 2
    inv_freq = ROPE_THETA ** (-jnp.arange(0, half, 2, dtype=F32) / half)
    ang_r = row_id[:, None] * inv_freq[None, :]
    ang_c = col_id[:, None] * inv_freq[None, :]
    ang = jnp.concatenate([ang_r, ang_r, ang_c, ang_c], axis=-1)
    cos, sin = jnp.cos(ang), jnp.sin(ang)
    return jnp.tile(cos, (1, N_HEADS)), jnp.tile(sin, (1, N_HEADS))


def _head_rms(x, w, ones):
    r = lax.rsqrt(_dot_precise(x * x, ones, 2) * (1.0 / HEAD_DIM) + EPS)
    return x * r * w, r


def _dup_half(x, kv):
    lane = lax.broadcasted_iota(jnp.int32, x.shape, 1)
    sel = (lane < 64) if kv == 0 else (lane >= 64)
    return jnp.where(sel, x, pltpu.roll(x, 64, 1))


def _qkv_prep(proj, cosq, sinq, qw, kw, ones, s, tm):
    t = proj.shape[0]
    ns = s // tm

    def body(p_ref, cos_ref, sin_ref, qw_ref, kw_ref, ones_ref, q_out, kd_out, vd_out, kdt_out, vdt_out):
        cos = cos_ref[...]
        sin = sin_ref[...]
        ones_m = ones_ref[...]
        qn, _ = _head_rms(p_ref[:, 0:512], qw_ref[...], ones_m)
        q_out[...] = (_rope(qn, cos, sin) * (HEAD_DIM ** -0.5)).astype(BF16)
        kn, _ = _head_rms(p_ref[:, 512:640], kw_ref[...], ones_m[0:128, 0:128])
        kr = _rope(kn, cos[:, 0:128], sin[:, 0:128])
        v = p_ref[:, 640:768]
        for kv in range(N_KV):
            kd = _dup_half(kr, kv)
            vd = _dup_half(v, kv)
            kd_out[kv] = kd.astype(BF16)
            vd_out[kv] = vd.astype(BF16)
            kdt_out[kv] = kd.T.astype(BF16)
            vdt_out[kv] = vd.T.astype(BF16)

    return pl.pallas_call(
        body, name="qkv_prep", grid=(t // tm,),
        in_specs=[pl.BlockSpec((tm, 768), lambda i: (i, 0)),
                  pl.BlockSpec((tm, 512), lambda i: (i % ns, 0)),
                  pl.BlockSpec((tm, 512), lambda i: (i % ns, 0)),
                  pl.BlockSpec((1, 512), lambda i: (0, 0)),
                  pl.BlockSpec((1, 128), lambda i: (0, 0)),
                  pl.BlockSpec((512, 512), lambda i: (0, 0))],
        out_specs=[pl.BlockSpec((tm, 512), lambda i: (i, 0)),
                   pl.BlockSpec((N_KV, tm, 128), lambda i: (0, i, 0)),
                   pl.BlockSpec((N_KV, tm, 128), lambda i: (0, i, 0)),
                   pl.BlockSpec((N_KV, 128, tm), lambda i: (0, 0, i)),
                   pl.BlockSpec((N_KV, 128, tm), lambda i: (0, 0, i))],
        out_shape=[jax.ShapeDtypeStruct((t, 512), BF16),
                   jax.ShapeDtypeStruct((N_KV, t, 128), BF16),
                   jax.ShapeDtypeStruct((N_KV, t, 128), BF16),
                   jax.ShapeDtypeStruct((N_KV, 128, t), BF16),
                   jax.ShapeDtypeStruct((N_KV, 128, t), BF16)],
        compiler_params=_cparams(("parallel",)),
    )(proj, cosq, sinq, qw, kw, ones)


def _qkv_bwd(proj, dq, dkd, dvd, cosq, sinq, qw, kw, ones, s, tm):
    t = proj.shape[0]
    ns = s // tm

    def body(p_ref, dq_ref, dkd_ref, dvd_ref, cos_ref, sin_ref, qw_ref, kw_ref, ones_ref,
             out_ref, dqw_ref, dkw_ref):
        cos = cos_ref[...]
        sin = sin_ref[...]
        ones_m = ones_ref[...]
        ones_k = ones_m[0:128, 0:128]

        def norm_bwd(x, w, dn, om):
            r = lax.rsqrt(_dot_precise(x * x, om, 2) * (1.0 / HEAD_DIM) + EPS)
            g = dn * w
            dx = r * (g - x * (r * r) * (_dot_precise(g * x, om, 2) * (1.0 / HEAD_DIM)))
            return dx, jnp.sum(dn * x * r, axis=0, keepdims=True)

        q = p_ref[:, 0:512]
        dqn = _rope_t(dq_ref[...], cos, sin) * (HEAD_DIM ** -0.5)
        dq_raw, dqw = norm_bwd(q, qw_ref[...], dqn, ones_m)
        out_ref[:, 0:512] = dq_raw.astype(BF16)

        lane = lax.broadcasted_iota(jnp.int32, (tm, 128), 1)

        def fold(ref):
            a0 = ref[0]
            a1 = ref[1]
            f0 = a0 + pltpu.roll(a0, 64, 1)
            f1 = a1 + pltpu.roll(a1, 64, 1)
            return jnp.where(lane < 64, f0, f1)

        k = p_ref[:, 512:640]
        dkn = _rope_t(fold(dkd_ref), cos[:, 0:128], sin[:, 0:128])
        dk_raw, dkw = norm_bwd(k, kw_ref[...], dkn, ones_k)
        out_ref[:, 512:640] = dk_raw.astype(BF16)
        out_ref[:, 640:768] = fold(dvd_ref).astype(BF16)

        @pl.when(pl.program_id(0) == 0)
        def _():
            dqw_ref[...] = jnp.zeros_like(dqw_ref)
            dkw_ref[...] = jnp.zeros_like(dkw_ref)

        dqw_ref[...] += dqw
        dkw_ref[...] += dkw

    return pl.pallas_call(
        body, name="qkv_bwd", grid=(t // tm,),
        in_specs=[pl.BlockSpec((tm, 768), lambda i: (i, 0)),
                  pl.BlockSpec((tm, 512), lambda i: (i, 0)),
                  pl.BlockSpec((N_KV, tm, 128), lambda i: (0, i, 0)),
                  pl.BlockSpec((N_KV, tm, 128), lambda i: (0, i, 0)),
                  pl.BlockSpec((tm, 512), lambda i: (i % ns, 0)),
                  pl.BlockSpec((tm, 512), lambda i: (i % ns, 0)),
                  pl.BlockSpec((1, 512), lambda i: (0, 0)),
                  pl.BlockSpec((1, 128), lambda i: (0, 0)),
                  pl.BlockSpec((512, 512), lambda i: (0, 0))],
        out_specs=[pl.BlockSpec((tm, 768), lambda i: (i, 0)),
                   pl.BlockSpec((1, 512), lambda i: (0, 0)),
                   pl.BlockSpec((1, 128), lambda i: (0, 0))],
        out_shape=[jax.ShapeDtypeStruct((t, 768), BF16),
                   jax.ShapeDtypeStruct((1, 512), F32),
                   jax.ShapeDtypeStruct((1, 128), F32)],
        compiler_params=_cparams(("arbitrary",)),
    )(proj, dq, dkd, dvd, cosq, sinq, qw, kw, ones)


def _grid_step_id(grid):
    idx = pl.program_id(0)
    for ax in range(1, len(grid)):
        idx = idx * grid[ax] + pl.program_id(ax)
    return idx


def _attn_fwd(q, kd, vdt, nb, s, tq, ride=None):
    t = q.shape[0]
    nq = s // tq
    rd = _ride_plan(ride)
    grid = (nb, N_HEADS // 2, nq)
    nsteps = nb * (N_HEADS // 2) * nq

    def body(*refs):
        q_ref, k_ref, vt_ref = refs[:3]
        o_ref, lse_ref = refs[3 + rd.n:5 + rd.n]
        copies = rd.copies(refs[3:3 + rd.n], refs[5 + rd.n:5 + 2 * rd.n], refs[5 + 2 * rd.n:])
        step_id = _grid_step_id(grid)
        _ride_start(copies, step_id == 0)
        qv = q_ref[...].astype(F32)
        lane = lax.broadcasted_iota(jnp.int32, qv.shape, 1)
        k = k_ref[0]
        vt = vt_ref[0]
        outs = []
        for half in range(2):
            qh = jnp.where((lane < 64) if half == 0 else (lane >= 64), qv, 0.0).astype(BF16)
            st = _dot_nt(k, qh)
            m = jnp.max(st, axis=0, keepdims=True)
            p = jnp.exp(st - m)
            l = jnp.sum(p, axis=0, keepdims=True)
            ot = _dot(vt, p.astype(BF16)) / l
            lse_ref[0, half] = m + jnp.log(l)
            outs.append(ot)
        row = lax.broadcasted_iota(jnp.int32, outs[0].shape, 0)
        o_ref[...] = jnp.where(row < 64, outs[0], outs[1]).T
        _ride_wait(copies, step_id == nsteps - 1)

    return pl.pallas_call(
        body, name="attn_fwd", grid=grid,
        in_specs=[pl.BlockSpec((tq, 128), lambda b, p, i: (b * nq + i, p)),
                  pl.BlockSpec((1, s, 128), lambda b, p, i: (p // 2, b, 0)),
                  pl.BlockSpec((1, 128, s), lambda b, p, i: (p // 2, 0, b))] + rd.in_specs,
        out_specs=[pl.BlockSpec((tq, 128), lambda b, p, i: (b * nq + i, p)),
                   pl.BlockSpec((1, 2, 1, tq), lambda b, p, i: (b, p, 0, i))] + rd.out_specs,
        out_shape=[jax.ShapeDtypeStruct((t, D_ATTN), F32),
                   jax.ShapeDtypeStruct((nb, N_HEADS, 1, s), F32)] + rd.out_shape,
        scratch_shapes=rd.scratch,
        compiler_params=_cparams(("arbitrary", "arbitrary", "arbitrary")),
    )(q, kd, vdt, *rd.srcs)


def _attn_bwd(q, kd, vd, kdt, o, lse, do, nb, s, tq, ride=None):
    t = q.shape[0]
    nq = s // tq
    ones8 = jnp.ones((8, 128), BF16)
    rd = _ride_plan(ride)
    grid = (nb, N_KV, 2, nq)
    nsteps = nb * N_KV * 2 * nq

    def body(*refs):
        q_ref, k_ref, v_ref, kt_ref, o_ref, lse_ref, do_ref, ones_ref = refs[:8]
        dq_ref, dk_ref, dv_ref = refs[8 + rd.n:11 + rd.n]
        copies = rd.copies(refs[8:8 + rd.n], refs[11 + rd.n:11 + 2 * rd.n], refs[11 + 2 * rd.n:])
        step_id = _grid_step_id(grid)
        _ride_start(copies, step_id == 0)

        @pl.when((pl.program_id(2) == 0) & (pl.program_id(3) == 0))
        def _():
            dk_ref[...] = jnp.zeros_like(dk_ref)
            dv_ref[...] = jnp.zeros_like(dv_ref)

        qv = q_ref[...].astype(F32)
        dov = do_ref[...]
        ov = o_ref[...]
        lane = lax.broadcasted_iota(jnp.int32, qv.shape, 1)
        k = k_ref[0]
        v = v_ref[0]
        kt = kt_ref[0]
        dqs = []
        dk_acc = None
        dv_acc = None
        for half in range(2):
            sel = (lane < 64) if half == 0 else (lane >= 64)
            qh = jnp.where(sel, qv, 0.0).astype(BF16)
            doh = jnp.where(sel, dov, 0.0)
            dob = doh.astype(BF16)
            delta = None
            for part in _split_bf16(doh * ov, 3):
                d8 = _dot_nt(ones_ref[...], part)
                delta = d8 if delta is None else delta + d8
            delta = delta[0:1, :]
            st = _dot_nt(k, qh)
            pt = jnp.exp(st - lse_ref[0, half])
            dpt = _dot_nt(v, dob)
            dst = (pt * (dpt - delta)).astype(BF16)
            dkh = _dot(dst, qh)
            dvh = _dot(pt.astype(BF16), dob)
            dk_acc = dkh if dk_acc is None else dk_acc + dkh
            dv_acc = dvh if dv_acc is None else dv_acc + dvh
            dqs.append(_dot(kt, dst))
        dk_ref[0] += dk_acc
        dv_ref[0] += dv_acc
        row = lax.broadcasted_iota(jnp.int32, dqs[0].shape, 0)
        dq_ref[...] = jnp.where(row < 64, dqs[0], dqs[1]).T
        _ride_wait(copies, step_id == nsteps - 1)

    qmap = lambda b, g, p, i: (b * nq + i, g * 2 + p)
    kvmap = lambda b, g, p, i: (g, b, 0)
    return pl.pallas_call(
        body, name="attn_bwd", grid=grid,
        in_specs=[pl.BlockSpec((tq, 128), qmap),
                  pl.BlockSpec((1, s, 128), kvmap),
                  pl.BlockSpec((1, s, 128), kvmap),
                  pl.BlockSpec((1, 128, s), lambda b, g, p, i: (g, 0, b)),
                  pl.BlockSpec((tq, 128), qmap),
                  pl.BlockSpec((1, 2, 1, tq), lambda b, g, p, i: (b, g * 2 + p, 0, i)),
                  pl.BlockSpec((tq, 128), qmap),
                  pl.BlockSpec((8, 128), lambda b, g, p, i: (0, 0))] + rd.in_specs,
        out_specs=[pl.BlockSpec((tq, 128), qmap),
                   pl.BlockSpec((1, s, 128), kvmap),
                   pl.BlockSpec((1, s, 128), kvmap)] + rd.out_specs,
        out_shape=[jax.ShapeDtypeStruct((t, D_ATTN), F32),
                   jax.ShapeDtypeStruct((N_KV, t, 128), F32),
                   jax.ShapeDtypeStruct((N_KV, t, 128), F32)] + rd.out_shape,
        scratch_shapes=rd.scratch,
        compiler_params=_cparams(("arbitrary", "arbitrary", "arbitrary", "arbitrary")),
    )(q, kd, vd, kdt, o, lse, do, ones8, *rd.srcs)


def _tri_mats():
    i = np.arange(CHUNK)
    lower = jnp.asarray(i[:, None] >= i[None, :], dtype=BF16)
    upper = jnp.asarray(i[:, None] <= i[None, :], dtype=BF16)
    return jnp.stack([lower, upper])


def _running_sum(tri, x):
    acc = None
    for part in _split_bf16(x, 3):
        t = _dot(tri, part)
        acc = t if acc is None else acc + t
    return acc


def _gates(z, lb):
    sig = _sigmoid(z)
    f = lb + (1.0 - lb) * sig
    logf = jnp.log(jnp.maximum(f, F_MIN))
    k = (1.0 - lb) * (1.0 - sig)
    return sig, f, logf, k


def _row_group(jg, anti):
    if anti:
        return 0, 8 * jg + 8
    return 8 * jg, CHUNK


def _chunk_fwd(q, k, v, b, st, bones, bmask, anti):
    b_last = b[0:1] if anti else b[CHUNK - 1:CHUNK]
    qb = q * jnp.exp(b)
    o_inter = _dot_nt(qb.astype(BF16), st.astype(BF16))
    kt = k * jnp.exp(b_last - b)
    st_new = st * jnp.exp(b_last) + _dot_tn(v.astype(BF16), kt.astype(BF16)) * bmask
    tt = lax.broadcasted_iota(jnp.int32, (CHUNK, LANES), 0)
    blocks = []
    for jg in range(CHUNK // 8):
        r0, r1 = _row_group(jg, anti)
        nr = r1 - r0
        qr = q[r0:r1]
        br = b[r0:r1]
        tr = tt[r0:r1]
        ws = []
        for i in range(8):
            sc = 8 * jg + i
            mask = (tr <= sc) if anti else (tr >= sc)
            e = jnp.where(mask, jnp.exp(jnp.minimum(br - b[sc:sc + 1], 0.0)), 0.0)
            ws.append(qr * e * k[sc:sc + 1])
        pb = _dot(jnp.concatenate(ws, axis=0).astype(BF16), bones)
        acc = None
        for i in range(8):
            sc = 8 * jg + i
            term = pb[i * nr:(i + 1) * nr] * v[sc:sc + 1]
            acc = term if acc is None else acc + term
        blocks.append((r0, r1, acc))
    o = o_inter
    pieces = []
    for g in range(CHUNK // 8):
        tot = o[8 * g:8 * g + 8]
        for (r0, r1, acc) in blocks:
            if r0 <= 8 * g and 8 * g + 8 <= r1:
                tot = tot + acc[8 * g - r0:8 * g - r0 + 8]
        pieces.append(tot)
    return jnp.concatenate(pieces, axis=0), st_new


def _hgrn_fwd(proj, lb, gw, nb, s, ride=None):
    t = proj.shape[0]
    nc = s // CHUNK
    tri = _tri_mats()
    bones = _block_ones(LANES, HEAD_DIM)
    rd = _ride_plan(ride)

    def body(*refs):
        q_ref, zf_ref, zb_ref, v_ref, g_ref, lb_ref, gw_ref, tri_ref, bones_ref = refs[:9]
        y_ref, os_ref, sts_ref = refs[9 + rd.n:12 + rd.n]
        st_ref = refs[12 + 2 * rd.n]
        copies = rd.copies(refs[9:9 + rd.n], refs[12 + rd.n:12 + 2 * rd.n], refs[13 + 2 * rd.n:])
        step_id = pl.program_id(0) * 2 + pl.program_id(1)
        _ride_start(copies, step_id == 0)
        bones_m = bones_ref[...]
        bmask = bones_m.astype(F32)
        for anti in (False, True):
            z_ref = zb_ref if anti else zf_ref
            lbv = lb_ref[1:2] if anti else lb_ref[0:1]
            trim = tri_ref[1] if anti else tri_ref[0]
            st_ref[...] = jnp.zeros_like(st_ref)

            def step(n, carry, anti=anti, z_ref=z_ref, lbv=lbv, trim=trim):
                cn = (nc - 1 - n) if anti else n
                rows = pl.ds(pl.multiple_of(cn * CHUNK, CHUNK), CHUNK)
                q = q_ref[rows, :]
                v = v_ref[rows, :]
                _, _, logf, k = _gates(z_ref[rows, :], lbv)
                b = _running_sum(trim, logf)
                st_in = st_ref[...]
                sts_ref[0, 0, 1 if anti else 0, cn] = st_in
                o, st_new = _chunk_fwd(q, k, v, b, st_in, bones_m, bmask, anti)
                st_ref[...] = st_new
                if anti:
                    osum = os_ref[rows, :] + o
                    os_ref[rows, :] = osum
                    r = lax.rsqrt(_dot_precise(osum * osum, bones_m, 2) * (1.0 / HEAD_DIM) + EPS)
                    hg = g_ref[rows, :]
                    y_ref[rows, :] = osum * r * gw_ref[...] * (hg * _sigmoid(hg))
                else:
                    os_ref[rows, :] = o
                return carry

            lax.fori_loop(0, nc, step, 0)
        _ride_wait(copies, step_id == nb * 2 - 1)

    def col(c):
        return pl.BlockSpec((s, LANES), lambda b, p, c=c: (b, c + p))

    return pl.pallas_call(
        body, name="hgrn_fwd", grid=(nb, 2),
        in_specs=[col(COL_HQ), col(COL_ZFW), col(COL_ZBW), col(COL_HI), col(COL_HG),
                  pl.BlockSpec((2, LANES), lambda b, p: (0, p)),
                  pl.BlockSpec((1, LANES), lambda b, p: (0, 0)),
                  pl.BlockSpec((2, CHUNK, CHUNK), lambda b, p: (0, 0, 0)),
                  pl.BlockSpec((LANES, LANES), lambda b, p: (0, 0))] + rd.in_specs,
        out_specs=[pl.BlockSpec((s, LANES), lambda b, p: (b, p)),
                   pl.BlockSpec((s, LANES), lambda b, p: (b, p)),
                   pl.BlockSpec((1, 1, 2, nc, LANES, LANES), lambda b, p: (b, p, 0, 0, 0, 0))] + rd.out_specs,
        out_shape=[jax.ShapeDtypeStruct((t, D_HGRN), F32), jax.ShapeDtypeStruct((t, D_HGRN), F32),
                   jax.ShapeDtypeStruct((nb, 2, 2, nc, LANES, LANES), F32)] + rd.out_shape,
        scratch_shapes=[pltpu.VMEM((LANES, LANES), F32)] + rd.scratch,
        compiler_params=_cparams(("arbitrary", "arbitrary")),
    )(proj, proj, proj, proj, proj, lb, gw, tri, bones, *rd.srcs)


def _chunk_bwd(q, k, v, b, do, st_in, rt, bones, bmask, anti):
    b_last = b[0:1] if anti else b[CHUNK - 1:CHUNK]
    eb = jnp.exp(b)
    ebl = jnp.exp(b_last - b)
    dob = do.astype(BF16)
    rtb = rt.astype(BF16)
    dq_inter = eb * _dot(dob, st_in.astype(BF16))
    dk_inter = ebl * _dot(v.astype(BF16), rtb)
    dv_inter = _dot_nt((k * ebl).astype(BF16), rtb)
    rt_new = rt * jnp.exp(b_last) + _dot_tn(dob, (q * eb).astype(BF16)) * bmask
    tt = lax.broadcasted_iota(jnp.int32, (CHUNK, LANES), 0)
    r8 = lax.broadcasted_iota(jnp.int32, (8, LANES), 0)
    blocks = []
    dk_pieces = []
    dv_pieces = []
    for jg in range(CHUNK // 8):
        r0, r1 = _row_group(jg, anti)
        nr = r1 - r0
        qr = q[r0:r1]
        br = b[r0:r1]
        tr = tt[r0:r1]
        dor = do[r0:r1]
        qes, kes, prods = [], [], []
        for i in range(8):
            sc = 8 * jg + i
            mask = (tr <= sc) if anti else (tr >= sc)
            e = jnp.where(mask, jnp.exp(jnp.minimum(br - b[sc:sc + 1], 0.0)), 0.0)
            qe = qr * e
            qes.append(qe)
            kes.append(e * k[sc:sc + 1])
            prods.append(qe * k[sc:sc + 1])
        for i in range(8):
            prods.append(dor * v[8 * jg + i:8 * jg + i + 1])
        sums = _dot(jnp.concatenate(prods, axis=0).astype(BF16), bones)
        acc = None
        dk_blk = jnp.zeros((8, LANES), F32)
        dv_blk = jnp.zeros((8, LANES), F32)
        for i in range(8):
            pb = sums[i * nr:(i + 1) * nr]
            dpb = sums[(8 + i) * nr:(9 + i) * nr]
            term = dpb * kes[i]
            acc = term if acc is None else acc + term
            dk_s = jnp.sum(dpb * qes[i], axis=0, keepdims=True)
            dv_s = jnp.sum(pb * dor, axis=0, keepdims=True)
            dk_blk = jnp.where(r8 == i, dk_s, dk_blk)
            dv_blk = jnp.where(r8 == i, dv_s, dv_blk)
        blocks.append((r0, r1, acc))
        dk_pieces.append(dk_blk)
        dv_pieces.append(dv_blk)
    dq_pieces = []
    for g in range(CHUNK // 8):
        tot = dq_inter[8 * g:8 * g + 8]
        for (r0, r1, acc) in blocks:
            if r0 <= 8 * g and 8 * g + 8 <= r1:
                tot = tot + acc[8 * g - r0:8 * g - r0 + 8]
        dq_pieces.append(tot)
    dq = jnp.concatenate(dq_pieces, axis=0)
    dk = dk_inter + jnp.concatenate(dk_pieces, axis=0)
    dv = dv_inter + jnp.concatenate(dv_pieces, axis=0)
    db_last = (jnp.sum(k * dk_inter, axis=0, keepdims=True)
               + jnp.exp(b_last) * jnp.sum(rt * st_in, axis=0, keepdims=True))
    return dq, dk, dv, rt_new, db_last


def _hgrn_bwd(proj, lb, gw, osum, states, dy, nb, s, ride=None):
    t = proj.shape[0]
    nc = s // CHUNK
    tri = _tri_mats()
    bones = _block_ones(LANES, HEAD_DIM)
    rd = _ride_plan(ride)

    def body(*refs):
        (q_ref, zf_ref, zb_ref, v_ref, g_ref, lb_ref, gw_ref, os_ref, sts_ref, dy_ref, tri_ref,
         bones_ref) = refs[:12]
        dq_ref, dzf_ref, dzb_ref, dv_ref, dg_ref, dgw_ref, dlb_ref = refs[12 + rd.n:19 + rd.n]
        do_sc, dq_sc, dv_sc, rt_cur = refs[19 + 2 * rd.n:23 + 2 * rd.n]
        copies = rd.copies(refs[12:12 + rd.n], refs[19 + rd.n:19 + 2 * rd.n], refs[23 + 2 * rd.n:])
        step_id = pl.program_id(0) * 2 + pl.program_id(1)
        _ride_start(copies, step_id == 0)
        bones_m = bones_ref[...]
        bmask = bones_m.astype(F32)
        gwv = gw_ref[...]

        def head(n, acc):
            rows = pl.ds(pl.multiple_of(n * CHUNK, CHUNK), CHUNK)
            o = os_ref[rows, :]
            hg = g_ref[rows, :]
            dyv = dy_ref[rows, :]
            sg = _sigmoid(hg)
            r = lax.rsqrt(_dot_precise(o * o, bones_m, 2) * (1.0 / HEAD_DIM) + EPS)
            nrm = o * r * gwv
            dn = dyv * (hg * sg)
            dg_ref[rows, :] = (dyv * nrm * (sg * (1.0 + hg * (1.0 - sg)))).astype(BF16)
            g = dn * gwv
            mean_go = _dot_precise(g * o, bones_m, 2) * (1.0 / HEAD_DIM)
            do_sc[rows, :] = r * (g - o * (r * r) * mean_go)
            return acc + jnp.sum(dn * o * r, axis=0, keepdims=True)

        dgw_ref[0] = lax.fori_loop(0, nc, head, jnp.zeros((1, LANES), F32))
        dq_sc[...] = jnp.zeros_like(dq_sc)
        dv_sc[...] = jnp.zeros_like(dv_sc)

        for anti in (False, True):
            z_ref = zb_ref if anti else zf_ref
            dz_ref = dzb_ref if anti else dzf_ref
            lbv = lb_ref[1:2] if anti else lb_ref[0:1]
            trim = tri_ref[1] if anti else tri_ref[0]
            trim_r = tri_ref[0] if anti else tri_ref[1]

            def load(cn, z_ref=z_ref, lbv=lbv, trim=trim):
                rows = pl.ds(pl.multiple_of(cn * CHUNK, CHUNK), CHUNK)
                q = q_ref[rows, :]
                v = v_ref[rows, :]
                sig, f, logf, k = _gates(z_ref[rows, :], lbv)
                b = _running_sum(trim, logf)
                return rows, q, v, sig, f, k, b

            rt_cur[...] = jnp.zeros_like(rt_cur)

            def back(n, dlb, anti=anti, load=load, lbv=lbv, trim_r=trim_r, dz_ref=dz_ref):
                cn = n if anti else (nc - 1 - n)
                rows, q, v, sig, f, k, b = load(cn)
                do = do_sc[rows, :]
                dq, dk, dv, rt_new, db_last = _chunk_bwd(q, k, v, b, do, sts_ref[0, 0, 1 if anti else 0, cn],
                                                         rt_cur[...], bones_m, bmask, anti)
                rt_cur[...] = rt_new
                dq_sc[rows, :] += dq
                dv_sc[rows, :] += dv
                dlogf = _running_sum(trim_r, q * dq - k * dk) + db_last
                dfl = jnp.where(f > F_MIN, dlogf / f, 0.0)
                dz_ref[rows, :] = ((dfl - dk) * (1.0 - lbv) * sig * (1.0 - sig)).astype(BF16)
                return dlb + jnp.sum((dfl - dk) * (1.0 - sig), axis=0, keepdims=True)

            dlb = lax.fori_loop(0, nc, back, jnp.zeros((1, LANES), F32))
            side = 1 if anti else 0
            dlb_ref[0, side:side + 1, :] = dlb

        dq_ref[...] = dq_sc[...].astype(BF16)
        dv_ref[...] = dv_sc[...].astype(BF16)
        _ride_wait(copies, step_id == nb * 2 - 1)

    def col(c):
        return pl.BlockSpec((s, LANES), lambda b, p, c=c: (b, c + p))

    sl = pl.BlockSpec((s, LANES), lambda b, p: (b, p))
    out_t = jax.ShapeDtypeStruct((t, D_HGRN), BF16)
    return pl.pallas_call(
        body, name="hgrn_bwd", grid=(nb, 2),
        in_specs=[col(COL_HQ), col(COL_ZFW), col(COL_ZBW), col(COL_HI), col(COL_HG),
                  pl.BlockSpec((2, LANES), lambda b, p: (0, p)),
                  pl.BlockSpec((1, LANES), lambda b, p: (0, 0)),
                  sl,
                  pl.BlockSpec((1, 1, 2, nc, LANES, LANES), lambda b, p: (b, p, 0, 0, 0, 0)),
                  sl,
                  pl.BlockSpec((2, CHUNK, CHUNK), lambda b, p: (0, 0, 0)),
                  pl.BlockSpec((LANES, LANES), lambda b, p: (0, 0))] + rd.in_specs,
        out_specs=[sl, sl, sl, sl, sl,
                   pl.BlockSpec((1, 1, LANES), lambda b, p: (b, 0, p)),
                   pl.BlockSpec((1, 2, LANES), lambda b, p: (b, 0, p))] + rd.out_specs,
        out_shape=[out_t, out_t, out_t, out_t, out_t,
                   jax.ShapeDtypeStruct((nb, 1, D_HGRN), F32),
                   jax.ShapeDtypeStruct((nb, 2, D_HGRN), F32)] + rd.out_shape,
        scratch_shapes=[pltpu.VMEM((s, LANES), F32), pltpu.VMEM((s, LANES), F32), pltpu.VMEM((s, LANES), F32),
                        pltpu.VMEM((LANES, LANES), F32)] + rd.scratch,
        compiler_params=_cparams(("arbitrary", "arbitrary")),
    )(proj, proj, proj, proj, proj, lb, gw, osum, states, dy, tri, bones, *rd.srcs)


def _lower_bounds(logits):
    def body(lg_ref, lb_ref):
        rows = [lg_ref[l:l + 1, :] for l in range(DEPTH)]
        m = functools.reduce(jnp.maximum, rows)
        ex = [jnp.exp(r - m) for r in rows]
        den = functools.reduce(jnp.add, ex)
        run = jnp.zeros_like(m)
        for l in range(DEPTH):
            if l > 0:
                run = run + ex[l] / den
            lb_ref[l:l + 1, :] = run

    return pl.pallas_call(body, name="lower_bounds", out_shape=jax.ShapeDtypeStruct(logits.shape, F32))(logits)


def _lower_bounds_bwd(logits, dlb):
    def body(lg_ref, dlb_ref, dlg_ref):
        rows = [lg_ref[l:l + 1, :] for l in range(DEPTH)]
        m = functools.reduce(jnp.maximum, rows)
        ex = [jnp.exp(r - m) for r in rows]
        den = functools.reduce(jnp.add, ex)
        sm = [e / den for e in ex]
        dsm = [jnp.zeros_like(m) for _ in range(DEPTH)]
        for i in range(1, DEPTH):
            for l in range(i, DEPTH):
                dsm[i] = dsm[i] + dlb_ref[l:l + 1, :]
        dot = functools.reduce(jnp.add, [sm[i] * dsm[i] for i in range(DEPTH)])
        for i in range(DEPTH):
            dlg_ref[i:i + 1, :] = sm[i] * (dsm[i] - dot)

    return pl.pallas_call(body, name="lower_bounds_bwd", out_shape=jax.ShapeDtypeStruct(logits.shape, F32))(logits, dlb)


CONV_ROWS = 128


def _conv_core(a, bg, dww, dwb, lnw, lnb, upad_ref, s):
    sb = _sigmoid(bg)
    u = a * sb
    upad_ref[0:16, :] = jnp.zeros((16, D_CONV), F32)
    upad_ref[16:16 + s, :] = u
    upad_ref[16 + s:32 + s, :] = jnp.zeros((16, D_CONV), F32)
    rows = min(s, CONV_ROWS)
    pieces = []
    for r0 in range(0, s, rows):
        acc = None
        for j in range(CONV_W):
            term = upad_ref[r0 + 1 + j:r0 + 1 + j + rows, :] * dww[j:j + 1, :]
            acc = term if acc is None else acc + term
        pieces.append(acc)
    c = jnp.concatenate(pieces, axis=0) + dwb
    mu = jnp.mean(c, axis=-1, keepdims=True)
    xc = c - mu
    rstd = lax.rsqrt(jnp.mean(xc * xc, axis=-1, keepdims=True) + LN_EPS)
    nh = xc * rstd
    l = nh * lnw + lnb
    sl = _sigmoid(l)
    return sb, nh, rstd, l, sl


def _conv_fwd(proj, dww, dwb, lnw, lnb, pww, pwb, nb, s):
    t = proj.shape[0]
    assert s % min(s, CONV_ROWS) == 0

    def body(a_ref, b_ref, dww_ref, dwb_ref, lnw_ref, lnb_ref, pww_ref, pwb_ref, y_ref, upad_ref):
        _, _, _, l, sl = _conv_core(a_ref[...], b_ref[...], dww_ref[...], dwb_ref[...], lnw_ref[...],
                                    lnb_ref[...], upad_ref, s)
        y_ref[...] = _dot((l * sl).astype(BF16), pww_ref[...]) + pwb_ref[...]

    vec = pl.BlockSpec((1, D_CONV), lambda b: (0, 0))
    return pl.pallas_call(
        body, name="conv_fwd", grid=(nb,),
        in_specs=[pl.BlockSpec((s, D_CONV), lambda b: (b, COL_CA)),
                  pl.BlockSpec((s, D_CONV), lambda b: (b, COL_CB)),
                  pl.BlockSpec((32, D_CONV), lambda b: (0, 0)), vec, vec, vec,
                  pl.BlockSpec((D_CONV, D_CONV), lambda b: (0, 0)), vec],
        out_specs=pl.BlockSpec((s, D_CONV), lambda b: (b, 0)),
        out_shape=jax.ShapeDtypeStruct((t, D_CONV), F32),
        scratch_shapes=[pltpu.VMEM((s + 32, D_CONV), F32)],
        compiler_params=_cparams(("parallel",)),
    )(proj, proj, dww, dwb, lnw, lnb, pww, pwb)


def _conv_bwd(proj, dy, dww, dwb, lnw, lnb, pww, nb, s):
    t = proj.shape[0]

    def body(a_ref, b_ref, dy_ref, dww_ref, dwb_ref, lnw_ref, lnb_ref, pww_ref,
             dab_ref, ddww_ref, ddwb_ref, dlnw_ref, dlnb_ref, dpww_ref, dpwb_ref, upad_ref, dcpad_ref):
        a = a_ref[...]
        dww = dww_ref[...]
        sb, nh, rstd, l, sl = _conv_core(a, b_ref[...], dww, dwb_ref[...], lnw_ref[...], lnb_ref[...],
                                         upad_ref, s)
        dyv = dy_ref[...]
        dyb = dyv.astype(BF16)
        ds = _dot_nt(dyb, pww_ref[...])
        dl = ds * (sl * (1.0 + l * (1.0 - sl)))
        dn = dl * lnw_ref[...]
        dc = rstd * (dn - jnp.mean(dn, axis=-1, keepdims=True)
                     - nh * jnp.mean(dn * nh, axis=-1, keepdims=True))

        @pl.when(pl.program_id(0) == 0)
        def _():
            for r in (ddww_ref, ddwb_ref, dlnw_ref, dlnb_ref, dpww_ref, dpwb_ref):
                r[...] = jnp.zeros_like(r)

        dpww_ref[...] += _dot_tn((l * sl).astype(BF16), dyb)
        dpwb_ref[...] += jnp.sum(dyv, axis=0, keepdims=True)
        dlnw_ref[...] += jnp.sum(dl * nh, axis=0, keepdims=True)
        dlnb_ref[...] += jnp.sum(dl, axis=0, keepdims=True)
        ddwb_ref[...] += jnp.sum(dc, axis=0, keepdims=True)

        dcpad_ref[0:16, :] = jnp.zeros((16, D_CONV), F32)
        dcpad_ref[16:16 + s, :] = dc
        dcpad_ref[16 + s:32 + s, :] = jnp.zeros((16, D_CONV), F32)
        rows = min(s, CONV_ROWS)
        r8 = lax.broadcasted_iota(jnp.int32, (32, D_CONV), 0)
        ddww = jnp.zeros((32, D_CONV), F32)
        pieces = []
        for r0 in range(0, s, rows):
            acc = None
            dcr = dcpad_ref[16 + r0:16 + r0 + rows, :]
            for j in range(CONV_W):
                term = dcpad_ref[r0 + 31 - j:r0 + 31 - j + rows, :] * dww[j:j + 1, :]
                acc = term if acc is None else acc + term
                wj = jnp.sum(dcr * upad_ref[r0 + 1 + j:r0 + 1 + j + rows, :], axis=0, keepdims=True)
                ddww = ddww + jnp.where(r8 == j, wj, 0.0)
            pieces.append(acc)
        du = jnp.concatenate(pieces, axis=0)
        ddww_ref[...] += ddww
        dab_ref[:, 0:D_CONV] = (du * sb).astype(BF16)
        dab_ref[:, D_CONV:2 * D_CONV] = (du * a * sb * (1.0 - sb)).astype(BF16)

    vec = pl.BlockSpec((1, D_CONV), lambda b: (0, 0))
    mat = pl.BlockSpec((D_CONV, D_CONV), lambda b: (0, 0))
    w32 = pl.BlockSpec((32, D_CONV), lambda b: (0, 0))
    vshape = jax.ShapeDtypeStruct((1, D_CONV), F32)
    return pl.pallas_call(
        body, name="conv_bwd", grid=(nb,),
        in_specs=[pl.BlockSpec((s, D_CONV), lambda b: (b, COL_CA)),
                  pl.BlockSpec((s, D_CONV), lambda b: (b, COL_CB)),
                  pl.BlockSpec((s, D_CONV), lambda b: (b, 0)),
                  w32, vec, vec, vec, mat],
        out_specs=[pl.BlockSpec((s, 2 * D_CONV), lambda b: (b, 0)), w32, vec, vec, vec, mat, vec],
        out_shape=[jax.ShapeDtypeStruct((t, 2 * D_CONV), BF16),
                   jax.ShapeDtypeStruct((32, D_CONV), F32), vshape, vshape, vshape,
                   jax.ShapeDtypeStruct((D_CONV, D_CONV), F32), vshape],
        scratch_shapes=[pltpu.VMEM((s + 32, D_CONV), F32), pltpu.VMEM((s + 32, D_CONV), F32)],
        compiler_params=_cparams(("arbitrary",)),
    )(proj, proj, dy, dww, dwb, lnw, lnb, pww)


def _mix_out(o_attn, y_hgrn, y_conv, x, aw, cw, w_out, tm):
    t = x.shape[0]

    def body(o_ref, h_ref, c_ref, x_ref, aw_ref, cw_ref, w_ref, mixed_ref, x1_ref):
        o = o_ref[...]
        a = o * lax.rsqrt(jnp.mean(o * o, axis=-1, keepdims=True) + EPS) * aw_ref[...]
        yc = c_ref[...]
        c = yc * lax.rsqrt(jnp.mean(yc * yc, axis=-1, keepdims=True) + EPS) * cw_ref[...]
        ab, hb, cb = a.astype(BF16), h_ref[...].astype(BF16), c.astype(BF16)
        mixed_ref[:, 0:512] = ab
        mixed_ref[:, 512:768] = hb
        mixed_ref[:, 768:1024] = cb
        x1_ref[...] = (x_ref[...] + _dot(ab, w_ref[0:512, :]) + _dot(hb, w_ref[512:768, :])
                       + _dot(cb, w_ref[768:1024, :]))

    def tok(w):
        return pl.BlockSpec((tm, w), lambda i: (i, 0))

    return pl.pallas_call(
        body, name="mix_out", grid=(t // tm,),
        in_specs=[tok(512), tok(256), tok(256), tok(D_MODEL),
                  pl.BlockSpec((1, 512), lambda i: (0, 0)), pl.BlockSpec((1, 256), lambda i: (0, 0)),
                  pl.BlockSpec((D_MODEL, D_MODEL), lambda i: (0, 0))],
        out_specs=[tok(D_MODEL), tok(D_MODEL)],
        out_shape=[jax.ShapeDtypeStruct((t, D_MODEL), BF16), jax.ShapeDtypeStruct((t, D_MODEL), F32)],
        compiler_params=_cparams(("parallel",)),
    )(o_attn, y_hgrn, y_conv, x, aw, cw, w_out)


def _mix_out_bwd(dx1, w_out, o_attn, y_conv, aw, cw, tm):
    t = dx1.shape[0]

    def body(dx_ref, w_ref, o_ref, c_ref, aw_ref, cw_ref, do_ref, dh_ref, dc_ref, daw_ref, dcw_ref):
        dm = _dot_nt(dx_ref[...].astype(BF16), w_ref[...])
        do, daw = _rms_bwd(dm[:, 0:512], o_ref[...], aw_ref[...])
        dc, dcw = _rms_bwd(dm[:, 768:1024], c_ref[...], cw_ref[...])
        do_ref[...] = do
        dh_ref[...] = dm[:, 512:768]
        dc_ref[...] = dc

        @pl.when(pl.program_id(0) == 0)
        def _():
            daw_ref[...] = jnp.zeros_like(daw_ref)
            dcw_ref[...] = jnp.zeros_like(dcw_ref)

        daw_ref[...] += jnp.sum(daw, axis=0, keepdims=True)
        dcw_ref[...] += jnp.sum(dcw, axis=0, keepdims=True)

    def tok(w):
        return pl.BlockSpec((tm, w), lambda i: (i, 0))

    v512 = pl.BlockSpec((1, 512), lambda i: (0, 0))
    v256 = pl.BlockSpec((1, 256), lambda i: (0, 0))
    return pl.pallas_call(
        body, name="mix_out_bwd", grid=(t // tm,),
        in_specs=[tok(D_MODEL), pl.BlockSpec((D_MODEL, D_MODEL), lambda i: (0, 0)), tok(512), tok(256),
                  v512, v256],
        out_specs=[tok(512), tok(256), tok(256), v512, v256],
        out_shape=[jax.ShapeDtypeStruct((t, 512), F32), jax.ShapeDtypeStruct((t, 256), F32),
                   jax.ShapeDtypeStruct((t, 256), F32), jax.ShapeDtypeStruct((1, 512), F32),
                   jax.ShapeDtypeStruct((1, 256), F32)],
        compiler_params=_cparams(("arbitrary",)),
    )(dx1, w_out, o_attn, y_conv, aw, cw)


FF_BLOCKS = 4


def _ffn_fwd(x1, fw, wg, wu, wd, tm, ride=None):
    t = x1.shape[0]
    fb = wg.shape[2]
    nf = N_DEV // FF_BLOCKS
    rd = _ride_plan(ride)
    grid = (t // tm, nf)

    def body(*refs):
        x_ref, fw_ref, wg_ref, wu_ref, wd_ref = refs[:5]
        h_ref, g_ref, u_ref, a_ref, x2_ref = refs[5 + rd.n:10 + rd.n]
        acc_ref = refs[10 + 2 * rd.n]
        copies = rd.copies(refs[5:5 + rd.n], refs[10 + rd.n:10 + 2 * rd.n], refs[11 + 2 * rd.n:])
        step_id = _grid_step_id(grid)
        _ride_start(copies, step_id == 0)
        j = pl.program_id(1)

        @pl.when(j == 0)
        def _():
            xv = x_ref[...]
            r = lax.rsqrt(jnp.mean(xv * xv, axis=-1, keepdims=True) + EPS)
            h_ref[...] = (xv * r * fw_ref[...]).astype(BF16)
            acc_ref[...] = xv

        h = h_ref[...]
        out = None
        for c in range(FF_BLOCKS):
            g = _dot(h, wg_ref[c])
            u = _dot(h, wu_ref[c])
            a = (g * _sigmoid(g) * u).astype(BF16)
            g_ref[c] = g.astype(BF16)
            u_ref[c] = u.astype(BF16)
            a_ref[c] = a
            part = _dot(a, wd_ref[c])
            out = part if out is None else out + part
        acc_ref[...] += out

        @pl.when(j == nf - 1)
        def _():
            x2_ref[...] = acc_ref[...]

        _ride_wait(copies, step_id == (t // tm) * nf - 1)

    tok = pl.BlockSpec((tm, D_MODEL), lambda i, j: (i, 0))
    ffb = pl.BlockSpec((FF_BLOCKS, tm, fb), lambda i, j: (j, i, 0))
    ffs = jax.ShapeDtypeStruct((N_DEV, t, fb), BF16)
    return pl.pallas_call(
        body, name="ffn_fwd", grid=grid,
        in_specs=[tok, pl.BlockSpec((1, D_MODEL), lambda i, j: (0, 0)),
                  pl.BlockSpec((FF_BLOCKS, D_MODEL, fb), lambda i, j: (j, 0, 0)),
                  pl.BlockSpec((FF_BLOCKS, D_MODEL, fb), lambda i, j: (j, 0, 0)),
                  pl.BlockSpec((FF_BLOCKS, fb, D_MODEL), lambda i, j: (j, 0, 0))] + rd.in_specs,
        out_specs=[tok, ffb, ffb, ffb, tok] + rd.out_specs,
        out_shape=[jax.ShapeDtypeStruct((t, D_MODEL), BF16), ffs, ffs, ffs,
                   jax.ShapeDtypeStruct((t, D_MODEL), F32)] + rd.out_shape,
        scratch_shapes=[pltpu.VMEM((tm, D_MODEL), F32)] + rd.scratch,
        compiler_params=_cparams(("arbitrary", "arbitrary")),
    )(x1, fw, wg, wu, wd, *rd.srcs)


def _ffn_bwd(dx2, g, u, wg, wu, wd, x1, fw, tm):
    t = dx2.shape[0]
    fb = wg.shape[2]
    nf = N_DEV // FF_BLOCKS

    def body(dx_ref, g_ref, u_ref, wg_ref, wu_ref, wd_ref, x_ref, fw_ref,
             dg_ref, du_ref, dx1_ref, dfw_ref, acc_ref):
        i = pl.program_id(0)
        j = pl.program_id(1)
        dxb = dx_ref[...].astype(BF16)
        dh = None
        for c in range(FF_BLOCKS):
            da = _dot_nt(dxb, wd_ref[c])
            gv = g_ref[c].astype(F32)
            uv = u_ref[c].astype(F32)
            sg = _sigmoid(gv)
            dg = (da * uv * (sg * (1.0 + gv * (1.0 - sg)))).astype(BF16)
            du = (da * gv * sg).astype(BF16)
            dg_ref[c] = dg
            du_ref[c] = du
            part = _dot_nt(dg, wg_ref[c]) + _dot_nt(du, wu_ref[c])
            dh = part if dh is None else dh + part

        @pl.when(j == 0)
        def _():
            acc_ref[...] = dh

        @pl.when(j > 0)
        def _():
            acc_ref[...] += dh

        @pl.when((i == 0) & (j == 0))
        def _():
            dfw_ref[...] = jnp.zeros_like(dfw_ref)

        @pl.when(j == nf - 1)
        def _():
            dx, dfw = _rms_bwd(acc_ref[...], x_ref[...], fw_ref[...])
            dx1_ref[...] = dx_ref[...] + dx
            dfw_ref[...] += jnp.sum(dfw, axis=0, keepdims=True)

    tok = pl.BlockSpec((tm, D_MODEL), lambda i, j: (i, 0))
    ffb = pl.BlockSpec((FF_BLOCKS, tm, fb), lambda i, j: (j, i, 0))
    ffs = jax.ShapeDtypeStruct((N_DEV, t, fb), BF16)
    vec = pl.BlockSpec((1, D_MODEL), lambda i, j: (0, 0))
    return pl.pallas_call(
        body, name="ffn_bwd", grid=(t // tm, nf),
        in_specs=[tok, ffb, ffb,
                  pl.BlockSpec((FF_BLOCKS, D_MODEL, fb), lambda i, j: (j, 0, 0)),
                  pl.BlockSpec((FF_BLOCKS, D_MODEL, fb), lambda i, j: (j, 0, 0)),
                  pl.BlockSpec((FF_BLOCKS, fb, D_MODEL), lambda i, j: (j, 0, 0)),
                  tok, vec],
        out_specs=[ffb, ffb, tok, vec],
        out_shape=[ffs, ffs, jax.ShapeDtypeStruct((t, D_MODEL), F32), jax.ShapeDtypeStruct((1, D_MODEL), F32)],
        scratch_shapes=[pltpu.VMEM((tm, D_MODEL), F32)],
        compiler_params=_cparams(("arbitrary", "arbitrary")),
    )(dx2, g, u, wg, wu, wd, x1, fw)


def _loss_grad(y, target, tm):
    t, d = y.shape

    def body(y_ref, t_ref, dy_ref, loss_ref):
        err = y_ref[...] - t_ref[...]
        dy_ref[...] = err * (1.0 / d)

        @pl.when(pl.program_id(0) == 0)
        def _():
            loss_ref[...] = jnp.zeros_like(loss_ref)

        part = jnp.sum(jnp.sum(err * err, axis=-1, keepdims=True), axis=0, keepdims=True)
        loss_ref[...] += part * (0.5 / d)

    tok = pl.BlockSpec((tm, d), lambda i: (i, 0))
    return pl.pallas_call(
        body, name="loss_grad", grid=(t // tm,),
        in_specs=[tok, tok],
        out_specs=[tok, pl.BlockSpec((1, 1), lambda i: (0, 0))],
        out_shape=[jax.ShapeDtypeStruct((t, d), F32), jax.ShapeDtypeStruct((1, 1), F32)],
        compiler_params=_cparams(("arbitrary",)),
    )(y, target)


def _tile(v, reps):
    return jnp.tile(v.reshape(1, -1), (1, reps))


class _LocalPlan:
    def __init__(self, wb):
        self.w = [{n: wb[n][l] for n in BIG_AXIS} for l in range(DEPTH)]

    def ride(self, kernel_name, l, grads=None):
        return None

    def done(self, kernel_name, l, outs):
        pass


def _local_step(x, target, p, plan):
    nb, s, d = x.shape
    t = nb * s
    tm = min(512, s)
    tq = min(512, s)
    xf = x.reshape(t, d)
    cosq, sinq = _rope_tables(s)
    ones512 = _block_ones(512, HEAD_DIM)
    lbs = _lower_bounds(p["hgrn_lb_logits"].reshape(DEPTH, 2 * D_HGRN)).reshape(DEPTH, 2, D_HGRN)

    saved = []
    cur = xf
    wb = plan.w
    for l in range(DEPTH):
        qw = _tile(p["q_norm_w"][l], N_HEADS)
        kw = _tile(p["k_norm_w"][l], N_KV)
        gw = _tile(p["hgrn_gnorm_w"][l], 2)
        dww = jnp.pad(p["conv_dw_w"][l], ((0, 1), (0, 0)))
        pww = p["conv_pw_w"][l].astype(BF16)
        h0, proj = _rms_proj(cur, _row(p["mix_norm_w"][l]), wb[l]["w_in"], tm)
        qr, kd, vd, kdt, vdt = _qkv_prep(proj, cosq, sinq, qw, kw, ones512, s, tm)
        o_attn, lse, *rode = _attn_fwd(qr, kd, vdt, nb, s, tq, plan.ride("attn_fwd", l))
        plan.done("attn_fwd", l, rode)
        y_hgrn, osum, states, *rode = _hgrn_fwd(proj, lbs[l], gw, nb, s, plan.ride("hgrn_fwd", l))
        plan.done("hgrn_fwd", l, rode)
        y_conv = _conv_fwd(proj, dww, _row(p["conv_dw_b"][l]), _row(p["conv_ln_w"][l]),
                           _row(p["conv_ln_b"][l]), pww, _row(p["conv_pw_b"][l]), nb, s)
        mixed, x1 = _mix_out(o_attn, y_hgrn, y_conv, cur, _row(p["attn_out_norm_w"][l]),
                             _row(p["conv_out_norm_w"][l]), wb[l]["w_out"], tm)
        hf, g, u, a, x2, *rode = _ffn_fwd(x1, _row(p["ffn_norm_w"][l]), wb[l]["w_gate"], wb[l]["w_up"],
                                          wb[l]["w_down"], tm, plan.ride("ffn_fwd", l))
        plan.done("ffn_fwd", l, rode)
        saved.append(dict(x=cur, h0=h0, proj=proj, qr=qr, kd=kd, vd=vd, kdt=kdt, o_attn=o_attn, lse=lse,
                          osum=osum, states=states, y_conv=y_conv, mixed=mixed, x1=x1, hf=hf, g=g, u=u, a=a,
                          qw=qw, kw=kw, gw=gw, dww=dww, pww=pww))
        cur = x2

    dcur, loss = _loss_grad(cur, target.reshape(t, d), tm)

    grads = {k: [None] * DEPTH for k in WEIGHTS}
    dlb = [None] * DEPTH
    for l in reversed(range(DEPTH)):
        sv = saved[l]
        dg, du, dx1, dfw = _ffn_bwd(dcur, sv["g"], sv["u"], wb[l]["w_gate"], wb[l]["w_up"], wb[l]["w_down"],
                                    sv["x1"], _row(p["ffn_norm_w"][l]), tm)
        grads["ffn_norm_w"][l] = dfw[0]
        grads["w_gate"][l] = _dw_ff(sv["hf"], dg, False, "dw_gate", tm)
        grads["w_up"][l] = _dw_ff(sv["hf"], du, False, "dw_up", tm)
        grads["w_down"][l] = _dw_ff(sv["a"], dcur, True, "dw_down", tm)
        do_attn, dy_hgrn, dy_conv, daw, dcw = _mix_out_bwd(
            dx1, wb[l]["w_out"], sv["o_attn"], sv["y_conv"], _row(p["attn_out_norm_w"][l]),
            _row(p["conv_out_norm_w"][l]), tm)
        grads["attn_out_norm_w"][l] = daw[0]
        grads["conv_out_norm_w"][l] = dcw[0]
        grads["w_out"][l] = _mm_tn(sv["mixed"], dx1, D_MODEL, "dw_out", tm)
        dq, dkd, dvd, *rode = _attn_bwd(sv["qr"], sv["kd"], sv["vd"], sv["kdt"], sv["o_attn"], sv["lse"], do_attn,
                                        nb, s, tq, plan.ride("attn_bwd", l, grads))
        plan.done("attn_bwd", l, rode)
        dqkv, dqw, dkw = _qkv_bwd(sv["proj"], dq, dkd, dvd, cosq, sinq, sv["qw"], sv["kw"], ones512, s, tm)
        grads["q_norm_w"][l] = dqw.reshape(N_HEADS, HEAD_DIM).sum(0)
        grads["k_norm_w"][l] = dkw.reshape(N_KV, HEAD_DIM).sum(0)
        dhq, dzf, dzb, dhi, dhg, dgw, dlb_l, *rode = _hgrn_bwd(sv["proj"], lbs[l], sv["gw"], sv["osum"],
                                                               sv["states"], dy_hgrn, nb, s,
                                                               plan.ride("hgrn_bwd", l, grads))
        plan.done("hgrn_bwd", l, rode)
        grads["hgrn_gnorm_w"][l] = dgw.reshape(nb * D_HGRN // HEAD_DIM, HEAD_DIM).sum(0)
        dlb[l] = dlb_l.sum(0)
        dab, ddww, ddwb, dlnw, dlnb, dpww, dpwb = _conv_bwd(
            sv["proj"], dy_conv, sv["dww"], _row(p["conv_dw_b"][l]), _row(p["conv_ln_w"][l]),
            _row(p["conv_ln_b"][l]), sv["pww"], nb, s)
        grads["conv_dw_w"][l] = ddww[:CONV_W]
        grads["conv_dw_b"][l] = ddwb[0]
        grads["conv_ln_w"][l] = dlnw[0]
        grads["conv_ln_b"][l] = dlnb[0]
        grads["conv_pw_w"][l] = dpww
        grads["conv_pw_b"][l] = dpwb[0]
        pieces = [dqkv, dhq, dzf, dzb, dhi, dhg, dab]
        grads["w_in"][l] = _dw_in(sv["h0"], pieces, tm)
        dcur, dnw = _proj_bwd(pieces, wb[l]["w_in"], sv["x"], _row(p["mix_norm_w"][l]), dx1, tm)
        grads["mix_norm_w"][l] = dnw[0]

    dlog = _lower_bounds_bwd(p["hgrn_lb_logits"].reshape(DEPTH, 2 * D_HGRN),
                             jnp.stack(dlb).reshape(DEPTH, 2 * D_HGRN))
    out = {k: (v if k in BIG_AXIS else jnp.stack(v)) for k, v in grads.items() if k != "hgrn_lb_logits"}
    out["hgrn_lb_logits"] = dlog.reshape(DEPTH, 2, D_HGRN)
    return loss, dcur.reshape(nb, s, d), out


BIG_AXIS = {"w_in": 2, "w_out": 1, "w_gate": 2, "w_up": 2, "w_down": 1}
SMALL_SHARD_AXIS = {"hgrn_lb_logits": 2, "conv_dw_w": 2, "conv_pw_w": 1}
WEIGHTS = ("mix_norm_w", "w_in", "q_norm_w", "k_norm_w", "hgrn_lb_logits", "hgrn_gnorm_w", "conv_dw_w",
           "conv_dw_b", "conv_ln_w", "conv_ln_b", "conv_pw_w", "conv_pw_b", "attn_out_norm_w",
           "conv_out_norm_w", "w_out", "ffn_norm_w", "w_gate", "w_up", "w_down")
SMALL = tuple(n for n in WEIGHTS if n not in BIG_AXIS)


def _my_index():
    return 4 * lax.axis_index("x") + 2 * lax.axis_index("y") + lax.axis_index("c")


class _RidePlan:
    def __init__(self, srcs, gather):
        self.srcs = list(srcs)
        self.n = len(self.srcs)
        self.gather = list(gather) if isinstance(gather, (list, tuple)) else [gather] * self.n
        any_spec = pl.BlockSpec(memory_space=pl.ANY)
        self.in_specs = [any_spec] * self.n
        self.out_specs = [any_spec] * self.n
        self.out_shape = [jax.ShapeDtypeStruct(((N_DEV,) + s.shape) if g else s.shape, s.dtype)
                          for s, g in zip(self.srcs, self.gather)]
        npeer = N_DEV - 1
        self.scratch = [pltpu.SemaphoreType.DMA((self.n * npeer,)), pltpu.SemaphoreType.DMA((self.n * npeer,)),
                        pltpu.SemaphoreType.DMA((self.n,))] if self.n else []

    def copies(self, src_refs, out_refs, sems):
        if not self.n:
            return [], [], []
        send_sems, recv_sems, local_sems = sems
        npeer = N_DEV - 1
        x, y, c = lax.axis_index("x"), lax.axis_index("y"), lax.axis_index("c")
        me = 4 * x + 2 * y + c
        locals_, sends, recvs = [], [], []
        for a in range(self.n):
            src_ref, out_ref = src_refs[a], out_refs[a]

            def rows_for(j, src_ref=src_ref, gather=self.gather[a]):
                return src_ref if gather else src_ref.at[j]

            locals_.append(pltpu.make_async_copy(rows_for(me), out_ref.at[me], local_sems.at[a]))
            for k in range(1, N_DEV):
                px = (1 - x) if (k & 4) else x
                py = (1 - y) if (k & 2) else y
                pc = (1 - c) if (k & 1) else c
                pidx = 4 * px + 2 * py + pc
                common = dict(send_sem=send_sems.at[a * npeer + k - 1], recv_sem=recv_sems.at[a * npeer + k - 1],
                              device_id=(px, py, pc), device_id_type=pl.DeviceIdType.MESH)
                sends.append(pltpu.make_async_remote_copy(src_ref=rows_for(pidx), dst_ref=out_ref.at[me], **common))
                recvs.append(pltpu.make_async_remote_copy(src_ref=rows_for(pidx), dst_ref=out_ref.at[pidx],
                                                          **common))
        return locals_, sends, recvs


def _ride_plan(ride):
    return _RidePlan(*ride) if ride else _RidePlan([], True)


def _ride_start(copies, when=None):
    locals_, sends, _ = copies

    def go():
        for cp in locals_ + sends:
            cp.start()

    if locals_:
        go() if when is None else pl.when(when)(go)


def _ride_wait(copies, when=None):
    locals_, sends, recvs = copies

    def go():
        for cp in recvs:
            cp.wait_recv()
        for cp in sends:
            cp.wait_send()
        for cp in locals_:
            cp.wait()

    if locals_:
        go() if when is None else pl.when(when)(go)


def _exchange(srcs, gather, name):
    rd = _RidePlan(srcs, gather)

    def body(*refs):
        copies = rd.copies(refs[:rd.n], refs[rd.n:2 * rd.n], refs[2 * rd.n:])
        _ride_start(copies)
        _ride_wait(copies)

    return pl.pallas_call(body, name=name, in_specs=rd.in_specs, out_specs=rd.out_specs,
                          out_shape=rd.out_shape, scratch_shapes=rd.scratch)(*srcs)


def _lane_group(n):
    g = 1
    while (g * n) % LANES:
        g += 1
    return g


def _cols_to_natural(gathered, name):
    _, k, n = gathered.shape
    grp = _lane_group(n)
    place = jnp.stack([jnp.asarray(np.eye(n, grp * n, k=i * n), BF16) for i in range(grp)])

    def body(g_ref, p_ref, o_ref):
        acc = None
        for i in range(grp):
            part = _dot(g_ref[i], p_ref[i])
            acc = part if acc is None else acc + part
        o_ref[...] = acc.astype(BF16)

    return pl.pallas_call(
        body, name=name, grid=(N_DEV // grp,),
        in_specs=[pl.BlockSpec((grp, k, n), lambda j: (j, 0, 0)),
                  pl.BlockSpec((grp, n, grp * n), lambda j: (0, 0, 0))],
        out_specs=pl.BlockSpec((k, grp * n), lambda j: (0, j)),
        out_shape=jax.ShapeDtypeStruct((k, N_DEV * n), BF16),
        compiler_params=_cparams(("parallel",)),
    )(gathered, place)


def _natural_to_cols(dw, name):
    k, n8 = dw.shape
    n = n8 // N_DEV
    grp = _lane_group(n)
    pick = jnp.stack([jnp.asarray(np.eye(grp * n, n, k=-i * n), BF16) for i in range(grp)])

    def body(d_ref, p_ref, o_ref):
        xb = d_ref[...].astype(BF16)
        for i in range(grp):
            o_ref[i] = _dot(xb, p_ref[i]).astype(BF16)

    return pl.pallas_call(
        body, name=name, grid=(N_DEV // grp,),
        in_specs=[pl.BlockSpec((k, grp * n), lambda j: (0, j)),
                  pl.BlockSpec((grp, grp * n, n), lambda j: (0, 0, 0))],
        out_specs=pl.BlockSpec((grp, k, n), lambda j: (j, 0, 0)),
        out_shape=jax.ShapeDtypeStruct((N_DEV, k, n), BF16),
        compiler_params=_cparams(("parallel",)),
    )(dw, pick)


def _adamw_math(w, g, m, v):
    m = ADAM_B1 * m + (1.0 - ADAM_B1) * g
    v = ADAM_B2 * v + (1.0 - ADAM_B2) * (g * g)
    m_hat = m / (1.0 - ADAM_B1 ** ADAM_STEP)
    v_hat = v / (1.0 - ADAM_B2 ** ADAM_STEP)
    delta = -ADAM_LR * (m_hat / (jnp.sqrt(v_hat) + ADAM_EPS) + ADAM_WD * w)
    return delta, m, v


def _sum_adamw(parts, w, m, v, name):
    _, k, n = w.shape
    tk = k
    for cand in (256, 176, 128):
        if k % cand == 0:
            tk = cand
            break

    def body(*refs):
        p_refs = refs[:DEPTH]
        w_ref, m_ref, v_ref, g_ref, d_ref, mo_ref, vo_ref = refs[DEPTH:]
        for l in range(DEPTH):
            @pl.when(pl.program_id(0) == l)
            def _(p_ref=p_refs[l]):
                g = p_ref[0].astype(F32)
                for i in range(1, N_DEV):
                    g = g + p_ref[i].astype(F32)
                g_ref[...] = g
                d_ref[...], mo_ref[...], vo_ref[...] = _adamw_math(w_ref[...], g, m_ref[...], v_ref[...])

    row = pl.BlockSpec((None, tk, n), lambda l, i: (l, i, 0))
    shp = jax.ShapeDtypeStruct(w.shape, F32)
    return pl.pallas_call(
        body, name=name, grid=(DEPTH, k // tk),
        in_specs=[pl.BlockSpec((N_DEV, tk, n), lambda l, i: (0, i, 0))] * DEPTH + [row, row, row],
        out_specs=[row, row, row, row],
        out_shape=[shp, shp, shp, shp],
        compiler_params=_cparams(("parallel", "parallel")),
    )(*parts, w, m, v)


def _sum8(parts, name):
    r = parts.shape[1]

    def body(p_ref, g_ref):
        g = p_ref[0]
        for i in range(1, N_DEV):
            g = g + p_ref[i]
        g_ref[...] = g

    return pl.pallas_call(body, name=name, out_shape=jax.ShapeDtypeStruct((r, LANES), F32))(parts)


def _adamw(w, g, m, v):
    def body(w_ref, g_ref, m_ref, v_ref, d_ref, mo_ref, vo_ref):
        d_ref[...], mo_ref[...], vo_ref[...] = _adamw_math(w_ref[...], g_ref[...], m_ref[...], v_ref[...])

    shp = jax.ShapeDtypeStruct(w.shape, F32)
    return pl.pallas_call(body, name="adamw_small", out_shape=[shp, shp, shp])(w, g, m, v)


def _pack(arrays, dtype, row_multiple):
    flat = jnp.concatenate([a.reshape(-1).astype(dtype) for a in arrays])
    n = flat.shape[0]
    unit = row_multiple * LANES
    total = -(-n // unit) * unit
    return jnp.pad(flat, (0, total - n)).reshape(total // LANES, LANES)


def _unpack(flat2d, shapes, lead=()):
    flat = flat2d.reshape(lead + (-1,))
    out, off = [], 0
    for shp in shapes:
        n = int(np.prod(shp))
        out.append(flat[..., off:off + n].reshape(lead + tuple(shp)))
        off += n
    return out


def _shard_to_rows(full, axis):
    shp = full.shape
    k = shp[axis] // N_DEV
    r = full.reshape(shp[:axis] + (N_DEV, k) + shp[axis + 1:])
    return jnp.moveaxis(r, axis, 0)


def _rows_to_full(rows, axis):
    r = jnp.moveaxis(rows, 0, axis)
    shp = r.shape
    return r.reshape(shp[:axis] + (shp[axis] * shp[axis + 1],) + shp[axis + 2:])


def kernel(x, mix_norm_w, w_in, q_norm_w, k_norm_w, hgrn_lb_logits, hgrn_gnorm_w, conv_dw_w, conv_dw_b, conv_ln_w, conv_ln_b, conv_pw_w, conv_pw_b, attn_out_norm_w, conv_out_norm_w, w_out, ffn_norm_w, w_gate, w_up, w_down, loss_target, m_mix_norm_w, m_w_in, m_q_norm_w, m_k_norm_w, m_hgrn_lb_logits, m_hgrn_gnorm_w, m_conv_dw_w, m_conv_dw_b, m_conv_ln_w, m_conv_ln_b, m_conv_pw_w, m_conv_pw_b, m_attn_out_norm_w, m_conv_out_norm_w, m_w_out, m_ffn_norm_w, m_w_gate, m_w_up, m_w_down, v_mix_norm_w, v_w_in, v_q_norm_w, v_k_norm_w, v_hgrn_lb_logits, v_hgrn_gnorm_w, v_conv_dw_w, v_conv_dw_b, v_conv_ln_w, v_conv_ln_b, v_conv_pw_w, v_conv_pw_b, v_attn_out_norm_w, v_conv_out_norm_w, v_w_out, v_ffn_norm_w, v_w_gate, v_w_up, v_w_down):
    w_loc = dict(zip(WEIGHTS, (mix_norm_w, w_in, q_norm_w, k_norm_w, hgrn_lb_logits, hgrn_gnorm_w, conv_dw_w,
                               conv_dw_b, conv_ln_w, conv_ln_b, conv_pw_w, conv_pw_b, attn_out_norm_w,
                               conv_out_norm_w, w_out, ffn_norm_w, w_gate, w_up, w_down)))
    m_loc = dict(zip(WEIGHTS, (m_mix_norm_w, m_w_in, m_q_norm_w, m_k_norm_w, m_hgrn_lb_logits, m_hgrn_gnorm_w,
                               m_conv_dw_w, m_conv_dw_b, m_conv_ln_w, m_conv_ln_b, m_conv_pw_w, m_conv_pw_b,
                               m_attn_out_norm_w, m_conv_out_norm_w, m_w_out, m_ffn_norm_w, m_w_gate, m_w_up,
                               m_w_down)))
    v_loc = dict(zip(WEIGHTS, (v_mix_norm_w, v_w_in, v_q_norm_w, v_k_norm_w, v_hgrn_lb_logits, v_hgrn_gnorm_w,
                               v_conv_dw_w, v_conv_dw_b, v_conv_ln_w, v_conv_ln_b, v_conv_pw_w, v_conv_pw_b,
                               v_attn_out_norm_w, v_conv_out_norm_w, v_w_out, v_ffn_norm_w, v_w_gate, v_w_up,
                               v_w_down)))
    me = _my_index()
    big = tuple(BIG_AXIS)
    sms = tuple(SMALL_SHARD_AXIS)

    sm_shapes = [w_loc[n].shape for n in sms]
    got_s = _exchange([_pack([w_loc[n] for n in sms], F32, 8)], True, "gather_small_params")[0]
    p_full = {n: w_loc[n] for n in SMALL if n not in SMALL_SHARD_AXIS}
    for n, a in zip(sms, _unpack(got_s, sm_shapes, (N_DEV,))):
        p_full[n] = _rows_to_full(a, SMALL_SHARD_AXIS[n])

    def natural(n, gathered):
        if n == "w_in":
            return _cols_to_natural(gathered, "relayout_" + n)
        if n == "w_out":
            return gathered.reshape(-1, gathered.shape[-1])
        return gathered

    def to_send(n, gl):
        if n == "w_in":
            return _natural_to_cols(gl, "split_d" + n)
        if n == "w_out":
            return gl.reshape(N_DEV, gl.shape[0] // N_DEV, gl.shape[1]).astype(BF16)
        return gl

    class StepPlan:
        def __init__(self):
            self.w = [dict() for _ in range(DEPTH)]
            self.parts = [dict() for _ in range(DEPTH)]
            self.pending = {}
            got = _exchange([w_loc["w_in"][0].astype(BF16)], True, "gather_w_in")
            self.w[0]["w_in"] = natural("w_in", got[0])

        def ride(self, kernel_name, l, grads=None):
            want = []
            if kernel_name == "attn_fwd":
                want = [("w_out", l), ("w_gate", l)]
            elif kernel_name == "hgrn_fwd":
                want = [("w_up", l), ("w_down", l)]
            elif kernel_name == "ffn_fwd" and l + 1 < DEPTH:
                want = [("w_in", l + 1)]
            elif kernel_name == "attn_bwd" and l + 1 < DEPTH:
                want = [("w_gate", l + 1), ("w_up", l + 1)]
            elif kernel_name == "hgrn_bwd" and l + 1 < DEPTH:
                want = [("w_in", l + 1), ("w_out", l + 1), ("w_down", l + 1)]
                if l == 0:
                    want += [(n, 0) for n in ("w_gate", "w_up", "w_down", "w_out")]
            if not want:
                return None
            self.pending[(kernel_name, l)] = want
            if grads is None:
                return [w_loc[n][wl].astype(BF16) for n, wl in want], True
            return [to_send(n, grads[n][wl]) for n, wl in want], False

        def done(self, kernel_name, l, outs):
            want = self.pending.pop((kernel_name, l), [])
            for (n, wl), out in zip(want, outs):
                if kernel_name.endswith("_fwd"):
                    self.w[wl][n] = natural(n, out)
                else:
                    self.parts[wl][n] = out

    plan = StepPlan()
    loss_part, grad_x, g = _local_step(x, loss_target, p_full, plan)
    loss = lax.psum(loss_part[0, 0], MESH_AXES)

    pw = g["conv_pw_w"]
    k_pw = w_loc["conv_pw_w"].shape[1]
    pw_send = jnp.moveaxis(pw.reshape(DEPTH, N_DEV, k_pw, pw.shape[-1]), 1, 0).reshape(N_DEV, -1, LANES)
    gathered_small = [n for n in SMALL if n != "conv_pw_w"]
    small_shapes = [g[n].shape for n in gathered_small]
    din_parts, small_parts, pw_parts = _exchange(
        [to_send("w_in", g["w_in"][0]), _pack([g[n] for n in gathered_small], F32, 8), pw_send],
        [False, True, False], "exchange_last_grads")
    plan.parts[0]["w_in"] = din_parts
    big_out = {n: _sum_adamw([plan.parts[l][n] for l in range(DEPTH)], w_loc[n], m_loc[n], v_loc[n],
                             "sum_adamw_" + n) for n in big}

    g_small = dict(zip(gathered_small, _unpack(_sum8(small_parts, "sum_small_grads"), small_shapes)))
    g_small["conv_pw_w"] = _sum8(pw_parts, "sum_conv_pw_grads").reshape(w_loc["conv_pw_w"].shape)
    for n in sms:
        if n == "conv_pw_w":
            continue
        ax = SMALL_SHARD_AXIS[n]
        k = w_loc[n].shape[ax]
        g_small[n] = lax.dynamic_slice_in_dim(g_small[n], me * k, k, axis=ax)
    loc_shapes = [w_loc[n].shape for n in SMALL]
    packed = [_pack([d[n] for n in SMALL], F32, 8) for d in (w_loc, g_small, m_loc, v_loc)]
    res = _adamw(*packed)
    small_out = [g_small] + [dict(zip(SMALL, _unpack(r, loc_shapes))) for r in res]

    def pick(i, n):
        return big_out[n][i] if n in BIG_AXIS else small_out[i][n]

    return (loss, grad_x) + tuple(pick(i, n) for i in range(4) for n in WEIGHTS)
```

```python
import functools

import jax
import jax.numpy as jnp
import numpy as np
from jax import lax
from jax.experimental import pallas as pl
from jax.experimental.pallas import tpu as pltpu

F32 = jnp.float32
BF16 = jnp.bfloat16

D_MODEL = 1024
D_ATTN = 512
D_HGRN = 256
D_CONV = 256
HEAD_DIM = 64
N_HEADS = 8
N_KV = 2
GRID_W = 64
ROPE_THETA = 10000.0
CHUNK = 64
F_MIN = 1e-6
CONV_W = 31
CONV_PAD = 15
D_FF = 2816
D_PROJ = 2560
EPS = 1e-6
LN_EPS = 1e-5
DEPTH = 2
ADAM_LR = 0.001
ADAM_B1 = 0.9
ADAM_B2 = 0.999
ADAM_EPS = 1e-08
ADAM_WD = 0.01
ADAM_STEP = 10
N_DEV = 8
MESH_AXES = ("x", "y", "c")

COL_HQ, COL_ZFW, COL_ZBW, COL_HI, COL_HG = 6, 8, 10, 12, 14
COL_CA, COL_CB = 8, 9

LANES = 128
VMEM_LIMIT_MB = 56


def _cparams(dims=None):
    return pltpu.CompilerParams(dimension_semantics=dims, vmem_limit_bytes=VMEM_LIMIT_MB * 2 ** 20)


def _dot(a, b):
    return jnp.dot(a, b, preferred_element_type=F32)


def _dot_nt(a, b):
    return lax.dot_general(a, b, (((1,), (1,)), ((), ())), preferred_element_type=F32)


def _dot_tn(a, b):
    return lax.dot_general(a, b, (((0,), (0,)), ((), ())), preferred_element_type=F32)


def _split_bf16(x, parts):
    out = []
    r = x
    for _ in range(parts):
        p = r.astype(BF16)
        out.append(p)
        r = r - p.astype(F32)
    return out


def _dot_precise(x, m_bf16, parts=3):
    acc = None
    for p in _split_bf16(x, parts):
        t = _dot(p, m_bf16)
        acc = t if acc is None else acc + t
    return acc


def _block_ones(width, group):
    i = np.arange(width)
    return jnp.asarray((i[:, None] // group) == (i[None, :] // group), dtype=BF16)


def _sigmoid(x):
    return 1.0 / (1.0 + jnp.exp(-x))


def _rot(x):
    w = x.shape[1]
    lane = lax.broadcasted_iota(jnp.int32, x.shape, 1)
    first = (lane % 32) < 16
    return jnp.where(first, -pltpu.roll(x, w - 16, 1), pltpu.roll(x, 16, 1))


def _rope(x, cos, sin):
    return x * cos + _rot(x) * sin


def _rope_t(dy, cos, sin):
    return dy * cos - _rot(dy * sin)


def _row(v):
    return v.reshape(1, -1)


def _rms_proj(x, wn, w, tm):
    t, d = x.shape
    n = w.shape[1]

    def body(x_ref, wn_ref, w_ref, h_ref, y_ref):
        xv = x_ref[...]
        r = lax.rsqrt(jnp.mean(xv * xv, axis=-1, keepdims=True) + EPS)
        h = (xv * r * wn_ref[...]).astype(BF16)
        h_ref[...] = h
        y_ref[...] = _dot(h, w_ref[...])

    return pl.pallas_call(
        body, name="rms_proj", grid=(t // tm,),
        in_specs=[pl.BlockSpec((tm, d), lambda i: (i, 0)),
                  pl.BlockSpec((1, d), lambda i: (0, 0)),
                  pl.BlockSpec((d, n), lambda i: (0, 0))],
        out_specs=[pl.BlockSpec((tm, d), lambda i: (i, 0)),
                   pl.BlockSpec((tm, n), lambda i: (i, 0))],
        out_shape=[jax.ShapeDtypeStruct((t, d), BF16), jax.ShapeDtypeStruct((t, n), F32)],
        compiler_params=_cparams(("parallel",)),
    )(x, wn, w)


def _rms_bwd(dh, x, wn):
    r = lax.rsqrt(jnp.mean(x * x, axis=-1, keepdims=True) + EPS)
    g = dh * wn
    dx = r * (g - x * (r * r) * jnp.mean(g * x, axis=-1, keepdims=True))
    return dx, dh * x * r


def _proj_bwd(pieces, w, x, wn, dres, tm):
    t = x.shape[0]
    d = x.shape[1]
    n = w.shape[1]
    widths = [p.shape[1] for p in pieces]
    offs = [sum(widths[:i]) for i in range(len(widths))]
    assert sum(widths) == n
    npc = len(pieces)

    def body(*refs):
        p_refs = refs[:npc]
        w_ref, x_ref, wn_ref, dr_ref, dx_ref, dwn_ref = refs[npc:]
        dh = None
        for p_ref, o, wd in zip(p_refs, offs, widths):
            part = _dot_nt(p_ref[...], w_ref[:, o:o + wd])
            dh = part if dh is None else dh + part
        dx, dwn = _rms_bwd(dh, x_ref[...], wn_ref[...])
        dx_ref[...] = dr_ref[...] + dx

        @pl.when(pl.program_id(0) == 0)
        def _():
            dwn_ref[...] = jnp.zeros_like(dwn_ref)

        dwn_ref[...] += jnp.sum(dwn, axis=0, keepdims=True)

    return pl.pallas_call(
        body, name="proj_bwd", grid=(t // tm,),
        in_specs=[pl.BlockSpec((tm, wd), lambda i: (i, 0)) for wd in widths]
        + [pl.BlockSpec((d, n), lambda i: (0, 0)),
           pl.BlockSpec((tm, d), lambda i: (i, 0)),
           pl.BlockSpec((1, d), lambda i: (0, 0)),
           pl.BlockSpec((tm, d), lambda i: (i, 0))],
        out_specs=[pl.BlockSpec((tm, d), lambda i: (i, 0)),
                   pl.BlockSpec((1, d), lambda i: (0, 0))],
        out_shape=[jax.ShapeDtypeStruct((t, d), F32), jax.ShapeDtypeStruct((1, d), F32)],
        compiler_params=_cparams(("arbitrary",)),
    )(*pieces, w, x, wn, dres)


def _dw_in(h0, pieces, tm):
    t, k = h0.shape
    widths = [p.shape[1] for p in pieces]
    offs = [sum(widths[:i]) for i in range(len(widths))]
    n = sum(widths)
    npc = len(pieces)

    def body(*refs):
        h_ref = refs[0]
        p_refs = refs[1:1 + npc]
        o_ref = refs[1 + npc]

        @pl.when(pl.program_id(0) == 0)
        def _():
            o_ref[...] = jnp.zeros_like(o_ref)

        ht = h_ref[...].astype(F32).T.astype(BF16)
        for p_ref, o, wd in zip(p_refs, offs, widths):
            o_ref[:, o:o + wd] += _dot(ht, p_ref[...])

    return pl.pallas_call(
        body, name="dw_in", grid=(t // tm,),
        in_specs=[pl.BlockSpec((tm, k), lambda i: (i, 0))]
        + [pl.BlockSpec((tm, wd), lambda i: (i, 0)) for wd in widths],
        out_specs=pl.BlockSpec((k, n), lambda i: (0, 0)),
        out_shape=jax.ShapeDtypeStruct((k, n), F32),
        compiler_params=_cparams(("arbitrary",)),
    )(h0, *pieces)


def _mm_tn(a, b, tn, name, tm):
    t, k = a.shape
    n = b.shape[1]

    def body(a_ref, b_ref, o_ref):
        @pl.when(pl.program_id(1) == 0)
        def _():
            o_ref[...] = jnp.zeros_like(o_ref)

        o_ref[...] += _dot_tn(a_ref[...].astype(BF16), b_ref[...].astype(BF16))

    return pl.pallas_call(
        body, name=name, grid=(n // tn, t // tm),
        in_specs=[pl.BlockSpec((tm, k), lambda j, i: (i, 0)),
                  pl.BlockSpec((tm, tn), lambda j, i: (i, j))],
        out_specs=pl.BlockSpec((k, tn), lambda j, i: (0, j)),
        out_shape=jax.ShapeDtypeStruct((k, n), F32),
        compiler_params=_cparams(("parallel", "arbitrary")),
    )(a, b)


def _dw_ff(h, d, rows, name, tm):
    t = d.shape[0] if rows else h.shape[0]
    blocked, flat = (h, d) if rows else (d, h)
    fb = blocked.shape[2]
    dm = flat.shape[1]
    out_blk = (N_DEV, fb, dm) if rows else (N_DEV, dm, fb)

    def body(h_ref, d_ref, o_ref, acc_ref):
        i = pl.program_id(0)

        @pl.when(i == 0)
        def _():
            acc_ref[...] = jnp.zeros_like(acc_ref)

        if rows:
            db = d_ref[...].astype(BF16)
            for j in range(N_DEV):
                acc_ref[j] += _dot_tn(h_ref[j], db)
        else:
            ht = h_ref[...].astype(F32).T.astype(BF16)
            for j in range(N_DEV):
                acc_ref[j] += _dot(ht, d_ref[j])

        @pl.when(i == t // tm - 1)
        def _():
            o_ref[...] = acc_ref[...].astype(BF16)

    blk_spec = pl.BlockSpec((N_DEV, tm, fb), lambda i: (0, i, 0))
    flat_spec = pl.BlockSpec((tm, dm), lambda i: (i, 0))
    return pl.pallas_call(
        body, name=name, grid=(t // tm,),
        in_specs=[blk_spec, flat_spec] if rows else [flat_spec, blk_spec],
        out_specs=pl.BlockSpec(out_blk, lambda i: (0, 0, 0)),
        out_shape=jax.ShapeDtypeStruct(out_blk, BF16),
        scratch_shapes=[pltpu.VMEM(out_blk, F32)],
        compiler_params=_cparams(("arbitrary",)),
    )(h, d)


def _rope_tables(s):
    rows = s // GRID_W
    row_id = jnp.repeat(jnp.arange(rows, dtype=F32), GRID_W)
    col_id = jnp.tile(jnp.arange(GRID_W, dtype=F32), rows)
    half = HEAD_DIM // 2
    inv_freq = ROPE_THETA ** (-jnp.arange(0, half, 2, dtype=F32) / half)
    ang_r = row_id[:, None] * inv_freq[None, :]
    ang_c = col_id[:, None] * inv_freq[None, :]
    ang = jnp.concatenate([ang_r, ang_r, ang_c, ang_c], axis=-1)
    cos, sin = jnp.cos(ang), jnp.sin(ang)
    return jnp.tile(cos, (1, N_HEADS)), jnp.tile(sin, (1, N_HEADS))


def _head_rms(x, w, ones):
    r = lax.rsqrt(_dot_precise(x * x, ones, 2) * (1.0 / HEAD_DIM) + EPS)
    return x * r * w, r


def _dup_half(x, kv):
    lane = lax.broadcasted_iota(jnp.int32, x.shape, 1)
    sel = (lane < 64) if kv == 0 else (lane >= 64)
    return jnp.where(sel, x, pltpu.roll(x, 64, 1))


def _qkv_prep(proj, cosq, sinq, qw, kw, ones, s, tm):
    t = proj.shape[0]
    ns = s // tm

    def body(p_ref, cos_ref, sin_ref, qw_ref, kw_ref, ones_ref, q_out, kd_out, vd_out, kdt_out, vdt_out):
        cos = cos_ref[...]
        sin = sin_ref[...]
        ones_m = ones_ref[...]
        qn, _ = _head_rms(p_ref[:, 0:512], qw_ref[...], ones_m)
        q_out[...] = (_rope(qn, cos, sin) * (HEAD_DIM ** -0.5)).astype(BF16)
        kn, _ = _head_rms(p_ref[:, 512:640], kw_ref[...], ones_m[0:128, 0:128])
        kr = _rope(kn, cos[:, 0:128], sin[:, 0:128])
        v = p_ref[:, 640:768]
        for kv in range(N_KV):
            kd = _dup_half(kr, kv)
            vd = _dup_half(v, kv)
            kd_out[kv] = kd.astype(BF16)
            vd_out[kv] = vd.astype(BF16)
            kdt_out[kv] = kd.T.astype(BF16)
            vdt_out[kv] = vd.T.astype(BF16)

    return pl.pallas_call(
        body, name="qkv_prep", grid=(t // tm,),
        in_specs=[pl.BlockSpec((tm, 768), lambda i: (i, 0)),
                  pl.BlockSpec((tm, 512), lambda i: (i % ns, 0)),
                  pl.BlockSpec((tm, 512), lambda i: (i % ns, 0)),
                  pl.BlockSpec((1, 512), lambda i: (0, 0)),
                  pl.BlockSpec((1, 128), lambda i: (0, 0)),
                  pl.BlockSpec((512, 512), lambda i: (0, 0))],
        out_specs=[pl.BlockSpec((tm, 512), lambda i: (i, 0)),
                   pl.BlockSpec((N_KV, tm, 128), lambda i: (0, i, 0)),
                   pl.BlockSpec((N_KV, tm, 128), lambda i: (0, i, 0)),
                   pl.BlockSpec((N_KV, 128, tm), lambda i: (0, 0, i)),
                   pl.BlockSpec((N_KV, 128, tm), lambda i: (0, 0, i))],
        out_shape=[jax.ShapeDtypeStruct((t, 512), BF16),
                   jax.ShapeDtypeStruct((N_KV, t, 128), BF16),
                   jax.ShapeDtypeStruct((N_KV, t, 128), BF16),
                   jax.ShapeDtypeStruct((N_KV, 128, t), BF16),
                   jax.ShapeDtypeStruct((N_KV, 128, t), BF16)],
        compiler_params=_cparams(("parallel",)),
    )(proj, cosq, sinq, qw, kw, ones)


def _qkv_bwd(proj, dq, dkd, dvd, cosq, sinq, qw, kw, ones, s, tm):
    t = proj.shape[0]
    ns = s // tm

    def body(p_ref, dq_ref, dkd_ref, dvd_ref, cos_ref, sin_ref, qw_ref, kw_ref, ones_ref,
             out_ref, dqw_ref, dkw_ref):
        cos = cos_ref[...]
        sin = sin_ref[...]
        ones_m = ones_ref[...]
        ones_k = ones_m[0:128, 0:128]

        def norm_bwd(x, w, dn, om):
            r = lax.rsqrt(_dot_precise(x * x, om, 2) * (1.0 / HEAD_DIM) + EPS)
            g = dn * w
            dx = r * (g - x * (r * r) * (_dot_precise(g * x, om, 2) * (1.0 / HEAD_DIM)))
            return dx, jnp.sum(dn * x * r, axis=0, keepdims=True)

        q = p_ref[:, 0:512]
        dqn = _rope_t(dq_ref[...], cos, sin) * (HEAD_DIM ** -0.5)
        dq_raw, dqw = norm_bwd(q, qw_ref[...], dqn, ones_m)
        out_ref[:, 0:512] = dq_raw.astype(BF16)

        lane = lax.broadcasted_iota(jnp.int32, (tm, 128), 1)

        def fold(ref):
            a0 = ref[0]
            a1 = ref[1]
            f0 = a0 + pltpu.roll(a0, 64, 1)
            f1 = a1 + pltpu.roll(a1, 64, 1)
            return jnp.where(lane < 64, f0, f1)

        k = p_ref[:, 512:640]
        dkn = _rope_t(fold(dkd_ref), cos[:, 0:128], sin[:, 0:128])
        dk_raw, dkw = norm_bwd(k, kw_ref[...], dkn, ones_k)
        out_ref[:, 512:640] = dk_raw.astype(BF16)
        out_ref[:, 640:768] = fold(dvd_ref).astype(BF16)

        @pl.when(pl.program_id(0) == 0)
        def _():
            dqw_ref[...] = jnp.zeros_like(dqw_ref)
            dkw_ref[...] = jnp.zeros_like(dkw_ref)

        dqw_ref[...] += dqw
        dkw_ref[...] += dkw

    return pl.pallas_call(
        body, name="qkv_bwd", grid=(t // tm,),
        in_specs=[pl.BlockSpec((tm, 768), lambda i: (i, 0)),
                  pl.BlockSpec((tm, 512), lambda i: (i, 0)),
                  pl.BlockSpec((N_KV, tm, 128), lambda i: (0, i, 0)),
                  pl.BlockSpec((N_KV, tm, 128), lambda i: (0, i, 0)),
                  pl.BlockSpec((tm, 512), lambda i: (i % ns, 0)),
                  pl.BlockSpec((tm, 512), lambda i: (i % ns, 0)),
                  pl.BlockSpec((1, 512), lambda i: (0, 0)),
                  pl.BlockSpec((1, 128), lambda i: (0, 0)),
                  pl.BlockSpec((512, 512), lambda i: (0, 0))],
        out_specs=[pl.BlockSpec((tm, 768), lambda i: (i, 0)),
                   pl.BlockSpec((1, 512), lambda i: (0, 0)),
                   pl.BlockSpec((1, 128), lambda i: (0, 0))],
        out_shape=[jax.ShapeDtypeStruct((t, 768), BF16),
                   jax.ShapeDtypeStruct((1, 512), F32),
                   jax.ShapeDtypeStruct((1, 128), F32)],
        compiler_params=_cparams(("arbitrary",)),
    )(proj, dq, dkd, dvd, cosq, sinq, qw, kw, ones)


def _grid_step_id(grid):
    idx = pl.program_id(0)
    for ax in range(1, len(grid)):
        idx = idx * grid[ax] + pl.program_id(ax)
    return idx


def _attn_fwd(q, kd, vdt, nb, s, tq, ride=None):
    t = q.shape[0]
    nq = s // tq
    rd = _ride_plan(ride)
    grid = (nb, N_HEADS // 2, nq)
    nsteps = nb * (N_HEADS // 2) * nq

    def body(*refs):
        q_ref, k_ref, vt_ref = refs[:3]
        o_ref, lse_ref = refs[3 + rd.n:5 + rd.n]
        copies = rd.copies(refs[3:3 + rd.n], refs[5 + rd.n:5 + 2 * rd.n], refs[5 + 2 * rd.n:])
        step_id = _grid_step_id(grid)
        _ride_start(copies, step_id == 0)
        qv = q_ref[...].astype(F32)
        lane = lax.broadcasted_iota(jnp.int32, qv.shape, 1)
        k = k_ref[0]
        vt = vt_ref[0]
        outs = []
        for half in range(2):
            qh = jnp.where((lane < 64) if half == 0 else (lane >= 64), qv, 0.0).astype(BF16)
            st = _dot_nt(k, qh)
            m = jnp.max(st, axis=0, keepdims=True)
            p = jnp.exp(st - m)
            l = jnp.sum(p, axis=0, keepdims=True)
            ot = _dot(vt, p.astype(BF16)) / l
            lse_ref[0, half] = m + jnp.log(l)
            outs.append(ot)
        row = lax.broadcasted_iota(jnp.int32, outs[0].shape, 0)
        o_ref[...] = jnp.where(row < 64, outs[0], outs[1]).T
        _ride_wait(copies, step_id == nsteps - 1)

    return pl.pallas_call(
        body, name="attn_fwd", grid=grid,
        in_specs=[pl.BlockSpec((tq, 128), lambda b, p, i: (b * nq + i, p)),
                  pl.BlockSpec((1, s, 128), lambda b, p, i: (p // 2, b, 0)),
                  pl.BlockSpec((1, 128, s), lambda b, p, i: (p // 2, 0, b))] + rd.in_specs,
        out_specs=[pl.BlockSpec((tq, 128), lambda b, p, i: (b * nq + i, p)),
                   pl.BlockSpec((1, 2, 1, tq), lambda b, p, i: (b, p, 0, i))] + rd.out_specs,
        out_shape=[jax.ShapeDtypeStruct((t, D_ATTN), F32),
                   jax.ShapeDtypeStruct((nb, N_HEADS, 1, s), F32)] + rd.out_shape,
        scratch_shapes=rd.scratch,
        compiler_params=_cparams(("arbitrary", "arbitrary", "arbitrary")),
    )(q, kd, vdt, *rd.srcs)


def _attn_bwd(q, kd, vd, kdt, o, lse, do, nb, s, tq, ride=None):
    t = q.shape[0]
    nq = s // tq
    ones8 = jnp.ones((8, 128), BF16)
    rd = _ride_plan(ride)
    grid = (nb, N_KV, 2, nq)
    nsteps = nb * N_KV * 2 * nq

    def body(*refs):
        q_ref, k_ref, v_ref, kt_ref, o_ref, lse_ref, do_ref, ones_ref = refs[:8]
        dq_ref, dk_ref, dv_ref = refs[8 + rd.n:11 + rd.n]
        copies = rd.copies(refs[8:8 + rd.n], refs[11 + rd.n:11 + 2 * rd.n], refs[11 + 2 * rd.n:])
        step_id = _grid_step_id(grid)
        _ride_start(copies, step_id == 0)

        @pl.when((pl.program_id(2) == 0) & (pl.program_id(3) == 0))
        def _():
            dk_ref[...] = jnp.zeros_like(dk_ref)
            dv_ref[...] = jnp.zeros_like(dv_ref)

        qv = q_ref[...].astype(F32)
        dov = do_ref[...]
        ov = o_ref[...]
        lane = lax.broadcasted_iota(jnp.int32, qv.shape, 1)
        k = k_ref[0]
        v = v_ref[0]
        kt = kt_ref[0]
        dqs = []
        dk_acc = None
        dv_acc = None
        for half in range(2):
            sel = (lane < 64) if half == 0 else (lane >= 64)
            qh = jnp.where(sel, qv, 0.0).astype(BF16)
            doh = jnp.where(sel, dov, 0.0)
            dob = doh.astype(BF16)
            delta = None
            for part in _split_bf16(doh * ov, 3):
                d8 = _dot_nt(ones_ref[...], part)
                delta = d8 if delta is None else delta + d8
            delta = delta[0:1, :]
            st = _dot_nt(k, qh)
            pt = jnp.exp(st - lse_ref[0, half])
            dpt = _dot_nt(v, dob)
            dst = (pt * (dpt - delta)).astype(BF16)
            dkh = _dot(dst, qh)
            dvh = _dot(pt.astype(BF16), dob)
            dk_acc = dkh if dk_acc is None else dk_acc + dkh
            dv_acc = dvh if dv_acc is None else dv_acc + dvh
            dqs.append(_dot(kt, dst))
        dk_ref[0] += dk_acc
        dv_ref[0] += dv_acc
        row = lax.broadcasted_iota(jnp.int32, dqs[0].shape, 0)
        dq_ref[...] = jnp.where(row < 64, dqs[0], dqs[1]).T
        _ride_wait(copies, step_id == nsteps - 1)

    qmap = lambda b, g, p, i: (b * nq + i, g * 2 + p)
    kvmap = lambda b, g, p, i: (g, b, 0)
    return pl.pallas_call(
        body, name="attn_bwd", grid=grid,
        in_specs=[pl.BlockSpec((tq, 128), qmap),
                  pl.BlockSpec((1, s, 128), kvmap),
                  pl.BlockSpec((1, s, 128), kvmap),
                  pl.BlockSpec((1, 128, s), lambda b, g, p, i: (g, 0, b)),
                  pl.BlockSpec((tq, 128), qmap),
                  pl.BlockSpec((1, 2, 1, tq), lambda b, g, p, i: (b, g * 2 + p, 0, i)),
                  pl.BlockSpec((tq, 128), qmap),
                  pl.BlockSpec((8, 128), lambda b, g, p, i: (0, 0))] + rd.in_specs,
        out_specs=[pl.BlockSpec((tq, 128), qmap),
                   pl.BlockSpec((1, s, 128), kvmap),
                   pl.BlockSpec((1, s, 128), kvmap)] + rd.out_specs,
        out_shape=[jax.ShapeDtypeStruct((t, D_ATTN), F32),
                   jax.ShapeDtypeStruct((N_KV, t, 128), F32),
                   jax.ShapeDtypeStruct((N_KV, t, 128), F32)] + rd.out_shape,
        scratch_shapes=rd.scratch,
        compiler_params=_cparams(("arbitrary", "arbitrary", "arbitrary", "arbitrary")),
    )(q, kd, vd, kdt, o, lse, do, ones8, *rd.srcs)


SUB = 16
N_SUB = CHUNK // SUB


def _tri_mats():
    i = np.arange(CHUNK)
    same = (i[:, None] // SUB) == (i[None, :] // SUB)
    lower = jnp.asarray(same & (i[:, None] >= i[None, :]), dtype=BF16)
    upper = jnp.asarray(same & (i[:, None] <= i[None, :]), dtype=BF16)
    return jnp.stack([lower, upper])


def _running_sum(tri, x):
    acc = None
    for part in _split_bf16(x, 3):
        t = _dot(tri, part)
        acc = t if acc is None else acc + t
    return acc


def _gates(z, lb):
    sig = _sigmoid(z)
    f = lb + (1.0 - lb) * sig
    logf = jnp.log(jnp.maximum(f, F_MIN))
    k = (1.0 - lb) * (1.0 - sig)
    return sig, f, logf, k


def _row_group(jg, anti):
    if anti:
        return 0, 8 * jg + 8
    return 8 * jg, SUB


def _sub_order(anti):
    return range(N_SUB - 1, -1, -1) if anti else range(N_SUB)


def _block_columns(b, anti):
    tt = lax.broadcasted_iota(jnp.int32, (SUB, LANES), 0)
    cols = []
    for jg in range(SUB // 8):
        r0, r1 = _row_group(jg, anti)
        br = b[r0:r1]
        tr = tt[r0:r1]
        for i in range(8):
            sc = 8 * jg + i
            mask = (tr <= sc) if anti else (tr >= sc)
            cols.append((r0, r1, sc, jnp.where(mask, jnp.exp(jnp.minimum(br - b[sc:sc + 1], 0.0)), 0.0)))
    return cols


def _scatter_rows(base, accs):
    pieces = []
    for g in range(SUB // 8):
        tot = base[8 * g:8 * g + 8]
        for (r0, r1), acc in accs.items():
            if r0 <= 8 * g and 8 * g + 8 <= r1:
                tot = tot + acc[8 * g - r0:8 * g - r0 + 8]
        pieces.append(tot)
    return jnp.concatenate(pieces, axis=0)


def _chunk_fwd(q, k, v, b, st, bones, bmask, anti):
    rs = [slice(SUB * i, SUB * i + SUB) for i in range(N_SUB)]
    decay, update, prods, spans, qbs = [], [], [], [], []
    for i in range(N_SUB):
        qi, ki, vi, bi = q[rs[i]], k[rs[i]], v[rs[i]], b[rs[i]]
        b_last = bi[0:1] if anti else bi[SUB - 1:SUB]
        decay.append(jnp.exp(b_last))
        update.append(_dot_tn(vi.astype(BF16), (ki * jnp.exp(b_last - bi)).astype(BF16)) * bmask)
        qbs.append((qi * jnp.exp(bi)).astype(BF16))
        for r0, r1, sc, e in _block_columns(bi, anti):
            prods.append(qi[r0:r1] * e * ki[sc:sc + 1])
            spans.append((i, r0, r1, sc))
    pb = _dot(jnp.concatenate(prods, axis=0).astype(BF16), bones)
    entered = [None] * N_SUB
    for i in _sub_order(anti):
        entered[i] = st
        st = st * decay[i] + update[i]
    accs = [dict() for _ in range(N_SUB)]
    off = 0
    for i, r0, r1, sc in spans:
        term = pb[off:off + r1 - r0] * v[rs[i]][sc:sc + 1]
        off += r1 - r0
        accs[i][(r0, r1)] = term if (r0, r1) not in accs[i] else accs[i][(r0, r1)] + term
    outs = [_scatter_rows(_dot_nt(qbs[i], entered[i].astype(BF16)), accs[i]) for i in range(N_SUB)]
    return jnp.concatenate(outs, axis=0), st, entered


def _hgrn_fwd(proj, lb, gw, nb, s, ride=None):
    t = proj.shape[0]
    nc = s // CHUNK
    tri = _tri_mats()
    bones = _block_ones(LANES, HEAD_DIM)
    rd = _ride_plan(ride)

    def body(*refs):
        q_ref, zf_ref, zb_ref, v_ref, g_ref, lb_ref, gw_ref, tri_ref, bones_ref = refs[:9]
        y_ref, os_ref, sts_ref = refs[9 + rd.n:12 + rd.n]
        st_ref = refs[12 + 2 * rd.n]
        copies = rd.copies(refs[9:9 + rd.n], refs[12 + rd.n:12 + 2 * rd.n], refs[13 + 2 * rd.n:])
        step_id = pl.program_id(0) * 2 + pl.program_id(1)
        _ride_start(copies, step_id == 0)
        bones_m = bones_ref[...]
        bmask = bones_m.astype(F32)
        for anti in (False, True):
            z_ref = zb_ref if anti else zf_ref
            lbv = lb_ref[1:2] if anti else lb_ref[0:1]
            trim = tri_ref[1] if anti else tri_ref[0]
            st_ref[...] = jnp.zeros_like(st_ref)

            def step(n, carry, anti=anti, z_ref=z_ref, lbv=lbv, trim=trim):
                cn = (nc - 1 - n) if anti else n
                rows = pl.ds(pl.multiple_of(cn * CHUNK, CHUNK), CHUNK)
                q = q_ref[rows, :]
                v = v_ref[rows, :]
                _, _, logf, k = _gates(z_ref[rows, :], lbv)
                b = _running_sum(trim, logf)
                o, st_new, entered = _chunk_fwd(q, k, v, b, st_ref[...], bones_m, bmask, anti)
                for i in range(N_SUB):
                    sts_ref[0, 0, 1 if anti else 0, cn * N_SUB + i] = entered[i].astype(BF16)
                st_ref[...] = st_new
                if anti:
                    osum = os_ref[rows, :] + o
                    os_ref[rows, :] = osum
                    r = lax.rsqrt(_dot_precise(osum * osum, bones_m, 2) * (1.0 / HEAD_DIM) + EPS)
                    hg = g_ref[rows, :]
                    y_ref[rows, :] = osum * r * gw_ref[...] * (hg * _sigmoid(hg))
                else:
                    os_ref[rows, :] = o
                return carry

            lax.fori_loop(0, nc, step, 0)
        _ride_wait(copies, step_id == nb * 2 - 1)

    def col(c):
        return pl.BlockSpec((s, LANES), lambda b, p, c=c: (b, c + p))

    return pl.pallas_call(
        body, name="hgrn_fwd", grid=(nb, 2),
        in_specs=[col(COL_HQ), col(COL_ZFW), col(COL_ZBW), col(COL_HI), col(COL_HG),
                  pl.BlockSpec((2, LANES), lambda b, p: (0, p)),
                  pl.BlockSpec((1, LANES), lambda b, p: (0, 0)),
                  pl.BlockSpec((2, CHUNK, CHUNK), lambda b, p: (0, 0, 0)),
                  pl.BlockSpec((LANES, LANES), lambda b, p: (0, 0))] + rd.in_specs,
        out_specs=[pl.BlockSpec((s, LANES), lambda b, p: (b, p)),
                   pl.BlockSpec((s, LANES), lambda b, p: (b, p)),
                   pl.BlockSpec((1, 1, 2, nc * N_SUB, LANES, LANES), lambda b, p: (b, p, 0, 0, 0, 0))]
        + rd.out_specs,
        out_shape=[jax.ShapeDtypeStruct((t, D_HGRN), F32), jax.ShapeDtypeStruct((t, D_HGRN), F32),
                   jax.ShapeDtypeStruct((nb, 2, 2, nc * N_SUB, LANES, LANES), BF16)] + rd.out_shape,
        scratch_shapes=[pltpu.VMEM((LANES, LANES), F32)] + rd.scratch,
        compiler_params=_cparams(("arbitrary", "arbitrary")),
    )(proj, proj, proj, proj, proj, lb, gw, tri, bones, *rd.srcs)


def _chunk_bwd(q, k, v, b, do, states, rt, bones, bmask, anti):
    rs = [slice(SUB * i, SUB * i + SUB) for i in range(N_SUB)]
    r8 = lax.broadcasted_iota(jnp.int32, (8, LANES), 0)
    decay, update, dq_inter, ebls, prods_p, prods_d, spans, qes, kes = [], [], [], [], [], [], [], [], []
    for i in range(N_SUB):
        qi, ki, vi, bi, doi = q[rs[i]], k[rs[i]], v[rs[i]], b[rs[i]], do[rs[i]]
        b_last = bi[0:1] if anti else bi[SUB - 1:SUB]
        eb = jnp.exp(bi)
        dob = doi.astype(BF16)
        decay.append(jnp.exp(b_last))
        ebls.append(jnp.exp(b_last - bi))
        update.append(_dot_tn(dob, (qi * eb).astype(BF16)) * bmask)
        dq_inter.append(eb * _dot(dob, states[i]))
        for r0, r1, sc, e in _block_columns(bi, anti):
            qe = qi[r0:r1] * e
            qes.append(qe)
            kes.append(e * ki[sc:sc + 1])
            prods_p.append(qe * ki[sc:sc + 1])
            prods_d.append(doi[r0:r1] * vi[sc:sc + 1])
            spans.append((i, r0, r1, sc))
    sums = _dot(jnp.concatenate(prods_p + prods_d, axis=0).astype(BF16), bones)
    half = sum(r1 - r0 for _, r0, r1, _ in spans)
    entered = [None] * N_SUB
    for i in reversed(list(_sub_order(anti))):
        entered[i] = rt
        rt = rt * decay[i] + update[i]
    accs = [dict() for _ in range(N_SUB)]
    dk_blks = [[jnp.zeros((8, LANES), F32) for _ in range(SUB // 8)] for _ in range(N_SUB)]
    dv_blks = [[jnp.zeros((8, LANES), F32) for _ in range(SUB // 8)] for _ in range(N_SUB)]
    off = 0
    for n, (i, r0, r1, sc) in enumerate(spans):
        nr = r1 - r0
        pb = sums[off:off + nr]
        dpb = sums[half + off:half + off + nr]
        off += nr
        term = dpb * kes[n]
        accs[i][(r0, r1)] = term if (r0, r1) not in accs[i] else accs[i][(r0, r1)] + term
        dk_s = jnp.sum(dpb * qes[n], axis=0, keepdims=True)
        dv_s = jnp.sum(pb * do[rs[i]][r0:r1], axis=0, keepdims=True)
        dk_blks[i][sc // 8] = jnp.where(r8 == sc % 8, dk_s, dk_blks[i][sc // 8])
        dv_blks[i][sc // 8] = jnp.where(r8 == sc % 8, dv_s, dv_blks[i][sc // 8])
    dqs, dks, dvs, dbs = [], [], [], []
    for i in range(N_SUB):
        ki, vi = k[rs[i]], v[rs[i]]
        rtb = entered[i].astype(BF16)
        dk_inter = ebls[i] * _dot(vi.astype(BF16), rtb)
        dv_inter = _dot_nt((ki * ebls[i]).astype(BF16), rtb)
        dqs.append(_scatter_rows(dq_inter[i], accs[i]))
        dks.append(dk_inter + jnp.concatenate(dk_blks[i], axis=0))
        dvs.append(dv_inter + jnp.concatenate(dv_blks[i], axis=0))
        db_last = (jnp.sum(ki * dk_inter, axis=0, keepdims=True)
                   + decay[i] * jnp.sum(entered[i] * states[i].astype(F32), axis=0, keepdims=True))
        dbs.append(jnp.broadcast_to(db_last, (SUB, LANES)))
    cat = lambda xs: jnp.concatenate(xs, axis=0)
    return cat(dqs), cat(dks), cat(dvs), rt, cat(dbs)


def _hgrn_bwd(proj, lb, gw, osum, states, dy, nb, s, ride=None):
    t = proj.shape[0]
    nc = s // CHUNK
    tri = _tri_mats()
    bones = _block_ones(LANES, HEAD_DIM)
    rd = _ride_plan(ride)

    def body(*refs):
        (q_ref, zf_ref, zb_ref, v_ref, g_ref, lb_ref, gw_ref, os_ref, sts_ref, dy_ref, tri_ref,
         bones_ref) = refs[:12]
        dq_ref, dzf_ref, dzb_ref, dv_ref, dg_ref, dgw_ref, dlb_ref = refs[12 + rd.n:19 + rd.n]
        do_sc, dq_sc, dv_sc, rt_cur = refs[19 + 2 * rd.n:23 + 2 * rd.n]
        copies = rd.copies(refs[12:12 + rd.n], refs[19 + rd.n:19 + 2 * rd.n], refs[23 + 2 * rd.n:])
        step_id = pl.program_id(0) * 2 + pl.program_id(1)
        _ride_start(copies, step_id == 0)
        bones_m = bones_ref[...]
        bmask = bones_m.astype(F32)
        gwv = gw_ref[...]

        def head(n, acc):
            rows = pl.ds(pl.multiple_of(n * CHUNK, CHUNK), CHUNK)
            o = os_ref[rows, :]
            hg = g_ref[rows, :]
            dyv = dy_ref[rows, :]
            sg = _sigmoid(hg)
            r = lax.rsqrt(_dot_precise(o * o, bones_m, 2) * (1.0 / HEAD_DIM) + EPS)
            nrm = o * r * gwv
            dn = dyv * (hg * sg)
            dg_ref[rows, :] = (dyv * nrm * (sg * (1.0 + hg * (1.0 - sg)))).astype(BF16)
            g = dn * gwv
            mean_go = _dot_precise(g * o, bones_m, 2) * (1.0 / HEAD_DIM)
            do_sc[rows, :] = r * (g - o * (r * r) * mean_go)
            return acc + jnp.sum(dn * o * r, axis=0, keepdims=True)

        dgw_ref[0] = lax.fori_loop(0, nc, head, jnp.zeros((1, LANES), F32))
        dq_sc[...] = jnp.zeros_like(dq_sc)
        dv_sc[...] = jnp.zeros_like(dv_sc)

        for anti in (False, True):
            z_ref = zb_ref if anti else zf_ref
            dz_ref = dzb_ref if anti else dzf_ref
            lbv = lb_ref[1:2] if anti else lb_ref[0:1]
            trim = tri_ref[1] if anti else tri_ref[0]
            trim_r = tri_ref[0] if anti else tri_ref[1]

            def load(cn, z_ref=z_ref, lbv=lbv, trim=trim):
                rows = pl.ds(pl.multiple_of(cn * CHUNK, CHUNK), CHUNK)
                q = q_ref[rows, :]
                v = v_ref[rows, :]
                sig, f, logf, k = _gates(z_ref[rows, :], lbv)
                b = _running_sum(trim, logf)
                return rows, q, v, sig, f, k, b

            rt_cur[...] = jnp.zeros_like(rt_cur)

            def back(n, dlb, anti=anti, load=load, lbv=lbv, trim_r=trim_r, dz_ref=dz_ref):
                cn = n if anti else (nc - 1 - n)
                rows, q, v, sig, f, k, b = load(cn)
                do = do_sc[rows, :]
                entered = [sts_ref[0, 0, 1 if anti else 0, cn * N_SUB + i] for i in range(N_SUB)]
                dq, dk, dv, rt_new, db_last = _chunk_bwd(q, k, v, b, do, entered, rt_cur[...], bones_m,
                                                         bmask, anti)
                rt_cur[...] = rt_new
                dq_sc[rows, :] += dq
                dv_sc[rows, :] += dv
                dlogf = _running_sum(trim_r, q * dq - k * dk) + db_last
                dfl = jnp.where(f > F_MIN, dlogf / f, 0.0)
                dz_ref[rows, :] = ((dfl - dk) * (1.0 - lbv) * sig * (1.0 - sig)).astype(BF16)
                return dlb + jnp.sum((dfl - dk) * (1.0 - sig), axis=0, keepdims=True)

            dlb = lax.fori_loop(0, nc, back, jnp.zeros((1, LANES), F32))
            side = 1 if anti else 0
            dlb_ref[0, side:side + 1, :] = dlb

        dq_ref[...] = dq_sc[...].astype(BF16)
        dv_ref[...] = dv_sc[...].astype(BF16)
        _ride_wait(copies, step_id == nb * 2 - 1)

    def col(c):
        return pl.BlockSpec((s, LANES), lambda b, p, c=c: (b, c + p))

    sl = pl.BlockSpec((s, LANES), lambda b, p: (b, p))
    out_t = jax.ShapeDtypeStruct((t, D_HGRN), BF16)
    return pl.pallas_call(
        body, name="hgrn_bwd", grid=(nb, 2),
        in_specs=[col(COL_HQ), col(COL_ZFW), col(COL_ZBW), col(COL_HI), col(COL_HG),
                  pl.BlockSpec((2, LANES), lambda b, p: (0, p)),
                  pl.BlockSpec((1, LANES), lambda b, p: (0, 0)),
                  sl,
                  pl.BlockSpec((1, 1, 2, nc * N_SUB, LANES, LANES), lambda b, p: (b, p, 0, 0, 0, 0)),
                  sl,
                  pl.BlockSpec((2, CHUNK, CHUNK), lambda b, p: (0, 0, 0)),
                  pl.BlockSpec((LANES, LANES), lambda b, p: (0, 0))] + rd.in_specs,
        out_specs=[sl, sl, sl, sl, sl,
                   pl.BlockSpec((1, 1, LANES), lambda b, p: (b, 0, p)),
                   pl.BlockSpec((1, 2, LANES), lambda b, p: (b, 0, p))] + rd.out_specs,
        out_shape=[out_t, out_t, out_t, out_t, out_t,
                   jax.ShapeDtypeStruct((nb, 1, D_HGRN), F32),
                   jax.ShapeDtypeStruct((nb, 2, D_HGRN), F32)] + rd.out_shape,
        scratch_shapes=[pltpu.VMEM((s, LANES), F32), pltpu.VMEM((s, LANES), F32), pltpu.VMEM((s, LANES), F32),
                        pltpu.VMEM((LANES, LANES), F32)] + rd.scratch,
        compiler_params=_cparams(("arbitrary", "arbitrary")),
    )(proj, proj, proj, proj, proj, lb, gw, osum, states, dy, tri, bones, *rd.srcs)


def _lower_bounds(logits):
    def body(lg_ref, lb_ref):
        rows = [lg_ref[l:l + 1, :] for l in range(DEPTH)]
        m = functools.reduce(jnp.maximum, rows)
        ex = [jnp.exp(r - m) for r in rows]
        den = functools.reduce(jnp.add, ex)
        run = jnp.zeros_like(m)
        for l in range(DEPTH):
            if l > 0:
                run = run + ex[l] / den
            lb_ref[l:l + 1, :] = run

    return pl.pallas_call(body, name="lower_bounds", out_shape=jax.ShapeDtypeStruct(logits.shape, F32))(logits)


def _lower_bounds_bwd(logits, dlb):
    def body(lg_ref, dlb_ref, dlg_ref):
        rows = [lg_ref[l:l + 1, :] for l in range(DEPTH)]
        m = functools.reduce(jnp.maximum, rows)
        ex = [jnp.exp(r - m) for r in rows]
        den = functools.reduce(jnp.add, ex)
        sm = [e / den for e in ex]
        dsm = [jnp.zeros_like(m) for _ in range(DEPTH)]
        for i in range(1, DEPTH):
            for l in range(i, DEPTH):
                dsm[i] = dsm[i] + dlb_ref[l:l + 1, :]
        dot = functools.reduce(jnp.add, [sm[i] * dsm[i] for i in range(DEPTH)])
        for i in range(DEPTH):
            dlg_ref[i:i + 1, :] = sm[i] * (dsm[i] - dot)

    return pl.pallas_call(body, name="lower_bounds_bwd", out_shape=jax.ShapeDtypeStruct(logits.shape, F32))(logits, dlb)


CONV_ROWS = 128


def _conv_core(a, bg, dww, dwb, lnw, lnb, upad_ref, s):
    sb = _sigmoid(bg)
    u = a * sb
    upad_ref[0:16, :] = jnp.zeros((16, D_CONV), F32)
    upad_ref[16:16 + s, :] = u
    upad_ref[16 + s:32 + s, :] = jnp.zeros((16, D_CONV), F32)
    rows = min(s, CONV_ROWS)
    pieces = []
    for r0 in range(0, s, rows):
        acc = None
        for j in range(CONV_W):
            term = upad_ref[r0 + 1 + j:r0 + 1 + j + rows, :] * dww[j:j + 1, :]
            acc = term if acc is None else acc + term
        pieces.append(acc)
    c = jnp.concatenate(pieces, axis=0) + dwb
    mu = jnp.mean(c, axis=-1, keepdims=True)
    xc = c - mu
    rstd = lax.rsqrt(jnp.mean(xc * xc, axis=-1, keepdims=True) + LN_EPS)
    nh = xc * rstd
    l = nh * lnw + lnb
    sl = _sigmoid(l)
    return sb, nh, rstd, l, sl


def _conv_fwd(proj, dww, dwb, lnw, lnb, pww, pwb, nb, s):
    t = proj.shape[0]
    assert s % min(s, CONV_ROWS) == 0

    def body(a_ref, b_ref, dww_ref, dwb_ref, lnw_ref, lnb_ref, pww_ref, pwb_ref, y_ref, upad_ref):
        _, _, _, l, sl = _conv_core(a_ref[...], b_ref[...], dww_ref[...], dwb_ref[...], lnw_ref[...],
                                    lnb_ref[...], upad_ref, s)
        y_ref[...] = _dot((l * sl).astype(BF16), pww_ref[...]) + pwb_ref[...]

    vec = pl.BlockSpec((1, D_CONV), lambda b: (0, 0))
    return pl.pallas_call(
        body, name="conv_fwd", grid=(nb,),
        in_specs=[pl.BlockSpec((s, D_CONV), lambda b: (b, COL_CA)),
                  pl.BlockSpec((s, D_CONV), lambda b: (b, COL_CB)),
                  pl.BlockSpec((32, D_CONV), lambda b: (0, 0)), vec, vec, vec,
                  pl.BlockSpec((D_CONV, D_CONV), lambda b: (0, 0)), vec],
        out_specs=pl.BlockSpec((s, D_CONV), lambda b: (b, 0)),
        out_shape=jax.ShapeDtypeStruct((t, D_CONV), F32),
        scratch_shapes=[pltpu.VMEM((s + 32, D_CONV), F32)],
        compiler_params=_cparams(("parallel",)),
    )(proj, proj, dww, dwb, lnw, lnb, pww, pwb)


def _conv_bwd(proj, dy, dww, dwb, lnw, lnb, pww, nb, s):
    t = proj.shape[0]

    def body(a_ref, b_ref, dy_ref, dww_ref, dwb_ref, lnw_ref, lnb_ref, pww_ref,
             dab_ref, ddww_ref, ddwb_ref, dlnw_ref, dlnb_ref, dpww_ref, dpwb_ref, upad_ref, dcpad_ref):
        a = a_ref[...]
        dww = dww_ref[...]
        sb, nh, rstd, l, sl = _conv_core(a, b_ref[...], dww, dwb_ref[...], lnw_ref[...], lnb_ref[...],
                                         upad_ref, s)
        dyv = dy_ref[...]
        dyb = dyv.astype(BF16)
        ds = _dot_nt(dyb, pww_ref[...])
        dl = ds * (sl * (1.0 + l * (1.0 - sl)))
        dn = dl * lnw_ref[...]
        dc = rstd * (dn - jnp.mean(dn, axis=-1, keepdims=True)
                     - nh * jnp.mean(dn * nh, axis=-1, keepdims=True))

        @pl.when(pl.program_id(0) == 0)
        def _():
            for r in (ddww_ref, ddwb_ref, dlnw_ref, dlnb_ref, dpww_ref, dpwb_ref):
                r[...] = jnp.zeros_like(r)

        dpww_ref[...] += _dot_tn((l * sl).astype(BF16), dyb)
        dpwb_ref[...] += jnp.sum(dyv, axis=0, keepdims=True)
        dlnw_ref[...] += jnp.sum(dl * nh, axis=0, keepdims=True)
        dlnb_ref[...] += jnp.sum(dl, axis=0, keepdims=True)
        ddwb_ref[...] += jnp.sum(dc, axis=0, keepdims=True)

        dcpad_ref[0:16, :] = jnp.zeros((16, D_CONV), F32)
        dcpad_ref[16:16 + s, :] = dc
        dcpad_ref[16 + s:32 + s, :] = jnp.zeros((16, D_CONV), F32)
        rows = min(s, CONV_ROWS)
        r8 = lax.broadcasted_iota(jnp.int32, (32, D_CONV), 0)
        ddww = jnp.zeros((32, D_CONV), F32)
        pieces = []
        for r0 in range(0, s, rows):
            acc = None
            dcr = dcpad_ref[16 + r0:16 + r0 + rows, :]
            for j in range(CONV_W):
                term = dcpad_ref[r0 + 31 - j:r0 + 31 - j + rows, :] * dww[j:j + 1, :]
                acc = term if acc is None else acc + term
                wj = jnp.sum(dcr * upad_ref[r0 + 1 + j:r0 + 1 + j + rows, :], axis=0, keepdims=True)
                ddww = ddww + jnp.where(r8 == j, wj, 0.0)
            pieces.append(acc)
        du = jnp.concatenate(pieces, axis=0)
        ddww_ref[...] += ddww
        dab_ref[:, 0:D_CONV] = (du * sb).astype(BF16)
        dab_ref[:, D_CONV:2 * D_CONV] = (du * a * sb * (1.0 - sb)).astype(BF16)

    vec = pl.BlockSpec((1, D_CONV), lambda b: (0, 0))
    mat = pl.BlockSpec((D_CONV, D_CONV), lambda b: (0, 0))
    w32 = pl.BlockSpec((32, D_CONV), lambda b: (0, 0))
    vshape = jax.ShapeDtypeStruct((1, D_CONV), F32)
    return pl.pallas_call(
        body, name="conv_bwd", grid=(nb,),
        in_specs=[pl.BlockSpec((s, D_CONV), lambda b: (b, COL_CA)),
                  pl.BlockSpec((s, D_CONV), lambda b: (b, COL_CB)),
                  pl.BlockSpec((s, D_CONV), lambda b: (b, 0)),
                  w32, vec, vec, vec, mat],
        out_specs=[pl.BlockSpec((s, 2 * D_CONV), lambda b: (b, 0)), w32, vec, vec, vec, mat, vec],
        out_shape=[jax.ShapeDtypeStruct((t, 2 * D_CONV), BF16),
                   jax.ShapeDtypeStruct((32, D_CONV), F32), vshape, vshape, vshape,
                   jax.ShapeDtypeStruct((D_CONV, D_CONV), F32), vshape],
        scratch_shapes=[pltpu.VMEM((s + 32, D_CONV), F32), pltpu.VMEM((s + 32, D_CONV), F32)],
        compiler_params=_cparams(("arbitrary",)),
    )(proj, proj, dy, dww, dwb, lnw, lnb, pww)


def _mix_out(o_attn, y_hgrn, y_conv, x, aw, cw, w_out, tm):
    t = x.shape[0]

    def body(o_ref, h_ref, c_ref, x_ref, aw_ref, cw_ref, w_ref, mixed_ref, x1_ref):
        o = o_ref[...]
        a = o * lax.rsqrt(jnp.mean(o * o, axis=-1, keepdims=True) + EPS) * aw_ref[...]
        yc = c_ref[...]
        c = yc * lax.rsqrt(jnp.mean(yc * yc, axis=-1, keepdims=True) + EPS) * cw_ref[...]
        ab, hb, cb = a.astype(BF16), h_ref[...].astype(BF16), c.astype(BF16)
        mixed_ref[:, 0:512] = ab
        mixed_ref[:, 512:768] = hb
        mixed_ref[:, 768:1024] = cb
        x1_ref[...] = (x_ref[...] + _dot(ab, w_ref[0:512, :]) + _dot(hb, w_ref[512:768, :])
                       + _dot(cb, w_ref[768:1024, :]))

    def tok(w):
        return pl.BlockSpec((tm, w), lambda i: (i, 0))

    return pl.pallas_call(
        body, name="mix_out", grid=(t // tm,),
        in_specs=[tok(512), tok(256), tok(256), tok(D_MODEL),
                  pl.BlockSpec((1, 512), lambda i: (0, 0)), pl.BlockSpec((1, 256), lambda i: (0, 0)),
                  pl.BlockSpec((D_MODEL, D_MODEL), lambda i: (0, 0))],
        out_specs=[tok(D_MODEL), tok(D_MODEL)],
        out_shape=[jax.ShapeDtypeStruct((t, D_MODEL), BF16), jax.ShapeDtypeStruct((t, D_MODEL), F32)],
        compiler_params=_cparams(("parallel",)),
    )(o_attn, y_hgrn, y_conv, x, aw, cw, w_out)


def _mix_out_bwd(dx1, w_out, o_attn, y_conv, aw, cw, tm):
    t = dx1.shape[0]

    def body(dx_ref, w_ref, o_ref, c_ref, aw_ref, cw_ref, do_ref, dh_ref, dc_ref, daw_ref, dcw_ref):
        dm = _dot_nt(dx_ref[...].astype(BF16), w_ref[...])
        do, daw = _rms_bwd(dm[:, 0:512], o_ref[...], aw_ref[...])
        dc, dcw = _rms_bwd(dm[:, 768:1024], c_ref[...], cw_ref[...])
        do_ref[...] = do
        dh_ref[...] = dm[:, 512:768]
        dc_ref[...] = dc

        @pl.when(pl.program_id(0) == 0)
        def _():
            daw_ref[...] = jnp.zeros_like(daw_ref)
            dcw_ref[...] = jnp.zeros_like(dcw_ref)

        daw_ref[...] += jnp.sum(daw, axis=0, keepdims=True)
        dcw_ref[...] += jnp.sum(dcw, axis=0, keepdims=True)

    def tok(w):
        return pl.BlockSpec((tm, w), lambda i: (i, 0))

    v512 = pl.BlockSpec((1, 512), lambda i: (0, 0))
    v256 = pl.BlockSpec((1, 256), lambda i: (0, 0))
    return pl.pallas_call(
        body, name="mix_out_bwd", grid=(t // tm,),
        in_specs=[tok(D_MODEL), pl.BlockSpec((D_MODEL, D_MODEL), lambda i: (0, 0)), tok(512), tok(256),
                  v512, v256],
        out_specs=[tok(512), tok(256), tok(256), v512, v256],
        out_shape=[jax.ShapeDtypeStruct((t, 512), F32), jax.ShapeDtypeStruct((t, 256), F32),
                   jax.ShapeDtypeStruct((t, 256), F32), jax.ShapeDtypeStruct((1, 512), F32),
                   jax.ShapeDtypeStruct((1, 256), F32)],
        compiler_params=_cparams(("arbitrary",)),
    )(dx1, w_out, o_attn, y_conv, aw, cw)


FF_BLOCKS = 4


def _ffn_fwd(x1, fw, wg, wu, wd, tm, ride=None):
    t = x1.shape[0]
    fb = wg.shape[2]
    nf = N_DEV // FF_BLOCKS
    rd = _ride_plan(ride)
    grid = (t // tm, nf)

    def body(*refs):
        x_ref, fw_ref, wg_ref, wu_ref, wd_ref = refs[:5]
        h_ref, g_ref, u_ref, a_ref, x2_ref = refs[5 + rd.n:10 + rd.n]
        acc_ref = refs[10 + 2 * rd.n]
        copies = rd.copies(refs[5:5 + rd.n], refs[10 + rd.n:10 + 2 * rd.n], refs[11 + 2 * rd.n:])
        step_id = _grid_step_id(grid)
        _ride_start(copies, step_id == 0)
        j = pl.program_id(1)

        @pl.when(j == 0)
        def _():
            xv = x_ref[...]
            r = lax.rsqrt(jnp.mean(xv * xv, axis=-1, keepdims=True) + EPS)
            h_ref[...] = (xv * r * fw_ref[...]).astype(BF16)
            acc_ref[...] = xv

        h = h_ref[...]
        out = None
        for c in range(FF_BLOCKS):
            g = _dot(h, wg_ref[c])
            u = _dot(h, wu_ref[c])
            a = (g * _sigmoid(g) * u).astype(BF16)
            g_ref[c] = g.astype(BF16)
            u_ref[c] = u.astype(BF16)
            a_ref[c] = a
            part = _dot(a, wd_ref[c])
            out = part if out is None else out + part
        acc_ref[...] += out

        @pl.when(j == nf - 1)
        def _():
            x2_ref[...] = acc_ref[...]

        _ride_wait(copies, step_id == (t // tm) * nf - 1)

    tok = pl.BlockSpec((tm, D_MODEL), lambda i, j: (i, 0))
    ffb = pl.BlockSpec((FF_BLOCKS, tm, fb), lambda i, j: (j, i, 0))
    ffs = jax.ShapeDtypeStruct((N_DEV, t, fb), BF16)
    return pl.pallas_call(
        body, name="ffn_fwd", grid=grid,
        in_specs=[tok, pl.BlockSpec((1, D_MODEL), lambda i, j: (0, 0)),
                  pl.BlockSpec((FF_BLOCKS, D_MODEL, fb), lambda i, j: (j, 0, 0)),
                  pl.BlockSpec((FF_BLOCKS, D_MODEL, fb), lambda i, j: (j, 0, 0)),
                  pl.BlockSpec((FF_BLOCKS, fb, D_MODEL), lambda i, j: (j, 0, 0))] + rd.in_specs,
        out_specs=[tok, ffb, ffb, ffb, tok] + rd.out_specs,
        out_shape=[jax.ShapeDtypeStruct((t, D_MODEL), BF16), ffs, ffs, ffs,
                   jax.ShapeDtypeStruct((t, D_MODEL), F32)] + rd.out_shape,
        scratch_shapes=[pltpu.VMEM((tm, D_MODEL), F32)] + rd.scratch,
        compiler_params=_cparams(("arbitrary", "arbitrary")),
    )(x1, fw, wg, wu, wd, *rd.srcs)


def _ffn_bwd(dx2, g, u, wg, wu, wd, x1, fw, tm, ride=None):
    t = dx2.shape[0]
    fb = wg.shape[2]
    nf = N_DEV // FF_BLOCKS
    rd = _ride_plan(ride)
    grid = (t // tm, nf)

    def body(*refs):
        dx_ref, g_ref, u_ref, wg_ref, wu_ref, wd_ref, x_ref, fw_ref = refs[:8]
        dg_ref, du_ref, dx1_ref, dfw_ref = refs[8 + rd.n:12 + rd.n]
        acc_ref = refs[12 + 2 * rd.n]
        copies = rd.copies(refs[8:8 + rd.n], refs[12 + rd.n:12 + 2 * rd.n], refs[13 + 2 * rd.n:])
        step_id = _grid_step_id(grid)
        _ride_start(copies, step_id == 0)
        i = pl.program_id(0)
        j = pl.program_id(1)
        dxb = dx_ref[...].astype(BF16)
        dh = None
        for c in range(FF_BLOCKS):
            da = _dot_nt(dxb, wd_ref[c])
            gv = g_ref[c].astype(F32)
            uv = u_ref[c].astype(F32)
            sg = _sigmoid(gv)
            dg = (da * uv * (sg * (1.0 + gv * (1.0 - sg)))).astype(BF16)
            du = (da * gv * sg).astype(BF16)
            dg_ref[c] = dg
            du_ref[c] = du
            part = _dot_nt(dg, wg_ref[c]) + _dot_nt(du, wu_ref[c])
            dh = part if dh is None else dh + part

        @pl.when(j == 0)
        def _():
            acc_ref[...] = dh

        @pl.when(j > 0)
        def _():
            acc_ref[...] += dh

        @pl.when((i == 0) & (j == 0))
        def _():
            dfw_ref[...] = jnp.zeros_like(dfw_ref)

        @pl.when(j == nf - 1)
        def _():
            dx, dfw = _rms_bwd(acc_ref[...], x_ref[...], fw_ref[...])
            dx1_ref[...] = dx_ref[...] + dx
            dfw_ref[...] += jnp.sum(dfw, axis=0, keepdims=True)

        _ride_wait(copies, step_id == (t // tm) * nf - 1)

    tok = pl.BlockSpec((tm, D_MODEL), lambda i, j: (i, 0))
    ffb = pl.BlockSpec((FF_BLOCKS, tm, fb), lambda i, j: (j, i, 0))
    ffs = jax.ShapeDtypeStruct((N_DEV, t, fb), BF16)
    vec = pl.BlockSpec((1, D_MODEL), lambda i, j: (0, 0))
    return pl.pallas_call(
        body, name="ffn_bwd", grid=grid,
        in_specs=[tok, ffb, ffb,
                  pl.BlockSpec((FF_BLOCKS, D_MODEL, fb), lambda i, j: (j, 0, 0)),
                  pl.BlockSpec((FF_BLOCKS, D_MODEL, fb), lambda i, j: (j, 0, 0)),
                  pl.BlockSpec((FF_BLOCKS, fb, D_MODEL), lambda i, j: (j, 0, 0)),
                  tok, vec] + rd.in_specs,
        out_specs=[ffb, ffb, tok, vec] + rd.out_specs,
        out_shape=[ffs, ffs, jax.ShapeDtypeStruct((t, D_MODEL), F32),
                   jax.ShapeDtypeStruct((1, D_MODEL), F32)] + rd.out_shape,
        scratch_shapes=[pltpu.VMEM((tm, D_MODEL), F32)] + rd.scratch,
        compiler_params=_cparams(("arbitrary", "arbitrary")),
    )(dx2, g, u, wg, wu, wd, x1, fw, *rd.srcs)


def _loss_grad(y, target, tm):
    t, d = y.shape

    def body(y_ref, t_ref, dy_ref, loss_ref):
        err = y_ref[...] - t_ref[...]
        dy_ref[...] = err * (1.0 / d)

        @pl.when(pl.program_id(0) == 0)
        def _():
            loss_ref[...] = jnp.zeros_like(loss_ref)

        part = jnp.sum(jnp.sum(err * err, axis=-1, keepdims=True), axis=0, keepdims=True)
        loss_ref[...] += part * (0.5 / d)

    tok = pl.BlockSpec((tm, d), lambda i: (i, 0))
    return pl.pallas_call(
        body, name="loss_grad", grid=(t // tm,),
        in_specs=[tok, tok],
        out_specs=[tok, pl.BlockSpec((1, 1), lambda i: (0, 0))],
        out_shape=[jax.ShapeDtypeStruct((t, d), F32), jax.ShapeDtypeStruct((1, 1), F32)],
        compiler_params=_cparams(("arbitrary",)),
    )(y, target)


def _tile(v, reps):
    return jnp.tile(v.reshape(1, -1), (1, reps))


class _LocalPlan:
    def __init__(self, wb):
        self.w = [{n: wb[n][l] for n in BIG_AXIS} for l in range(DEPTH)]

    def ride(self, kernel_name, l, grads=None):
        return None

    def done(self, kernel_name, l, outs):
        pass


def _local_step(x, target, p, plan):
    nb, s, d = x.shape
    t = nb * s
    tm = min(512, s)
    tq = min(512, s)
    xf = x.reshape(t, d)
    cosq, sinq = _rope_tables(s)
    ones512 = _block_ones(512, HEAD_DIM)
    lbs = _lower_bounds(p["hgrn_lb_logits"].reshape(DEPTH, 2 * D_HGRN)).reshape(DEPTH, 2, D_HGRN)

    saved = []
    cur = xf
    wb = plan.w
    for l in range(DEPTH):
        qw = _tile(p["q_norm_w"][l], N_HEADS)
        kw = _tile(p["k_norm_w"][l], N_KV)
        gw = _tile(p["hgrn_gnorm_w"][l], 2)
        dww = jnp.pad(p["conv_dw_w"][l], ((0, 1), (0, 0)))
        pww = p["conv_pw_w"][l].astype(BF16)
        h0, proj = _rms_proj(cur, _row(p["mix_norm_w"][l]), wb[l]["w_in"], tm)
        qr, kd, vd, kdt, vdt = _qkv_prep(proj, cosq, sinq, qw, kw, ones512, s, tm)
        o_attn, lse, *rode = _attn_fwd(qr, kd, vdt, nb, s, tq, plan.ride("attn_fwd", l))
        plan.done("attn_fwd", l, rode)
        y_hgrn, osum, states, *rode = _hgrn_fwd(proj, lbs[l], gw, nb, s, plan.ride("hgrn_fwd", l))
        plan.done("hgrn_fwd", l, rode)
        y_conv = _conv_fwd(proj, dww, _row(p["conv_dw_b"][l]), _row(p["conv_ln_w"][l]),
                           _row(p["conv_ln_b"][l]), pww, _row(p["conv_pw_b"][l]), nb, s)
        mixed, x1 = _mix_out(o_attn, y_hgrn, y_conv, cur, _row(p["attn_out_norm_w"][l]),
                             _row(p["conv_out_norm_w"][l]), wb[l]["w_out"], tm)
        hf, g, u, a, x2, *rode = _ffn_fwd(x1, _row(p["ffn_norm_w"][l]), wb[l]["w_gate"], wb[l]["w_up"],
                                          wb[l]["w_down"], tm, plan.ride("ffn_fwd", l))
        plan.done("ffn_fwd", l, rode)
        saved.append(dict(x=cur, h0=h0, proj=proj, qr=qr, kd=kd, vd=vd, kdt=kdt, o_attn=o_attn, lse=lse,
                          osum=osum, states=states, y_conv=y_conv, mixed=mixed, x1=x1, hf=hf, g=g, u=u, a=a,
                          qw=qw, kw=kw, gw=gw, dww=dww, pww=pww))
        cur = x2

    dcur, loss = _loss_grad(cur, target.reshape(t, d), tm)

    grads = {k: [None] * DEPTH for k in WEIGHTS}
    dlb = [None] * DEPTH
    for l in reversed(range(DEPTH)):
        sv = saved[l]
        dg, du, dx1, dfw, *rode = _ffn_bwd(dcur, sv["g"], sv["u"], wb[l]["w_gate"], wb[l]["w_up"],
                                           wb[l]["w_down"], sv["x1"], _row(p["ffn_norm_w"][l]), tm,
                                           plan.ride("ffn_bwd", l, grads))
        plan.done("ffn_bwd", l, rode)
        grads["ffn_norm_w"][l] = dfw[0]
        grads["w_gate"][l] = _dw_ff(sv["hf"], dg, False, "dw_gate", tm)
        grads["w_up"][l] = _dw_ff(sv["hf"], du, False, "dw_up", tm)
        grads["w_down"][l] = _dw_ff(sv["a"], dcur, True, "dw_down", tm)
        do_attn, dy_hgrn, dy_conv, daw, dcw = _mix_out_bwd(
            dx1, wb[l]["w_out"], sv["o_attn"], sv["y_conv"], _row(p["attn_out_norm_w"][l]),
            _row(p["conv_out_norm_w"][l]), tm)
        grads["attn_out_norm_w"][l] = daw[0]
        grads["conv_out_norm_w"][l] = dcw[0]
        grads["w_out"][l] = _mm_tn(sv["mixed"], dx1, D_MODEL, "dw_out", tm)
        dq, dkd, dvd, *rode = _attn_bwd(sv["qr"], sv["kd"], sv["vd"], sv["kdt"], sv["o_attn"], sv["lse"], do_attn,
                                        nb, s, tq, plan.ride("attn_bwd", l, grads))
        plan.done("attn_bwd", l, rode)
        dqkv, dqw, dkw = _qkv_bwd(sv["proj"], dq, dkd, dvd, cosq, sinq, sv["qw"], sv["kw"], ones512, s, tm)
        grads["q_norm_w"][l] = dqw.reshape(N_HEADS, HEAD_DIM).sum(0)
        grads["k_norm_w"][l] = dkw.reshape(N_KV, HEAD_DIM).sum(0)
        dhq, dzf, dzb, dhi, dhg, dgw, dlb_l, *rode = _hgrn_bwd(sv["proj"], lbs[l], sv["gw"], sv["osum"],
                                                               sv["states"], dy_hgrn, nb, s,
                                                               plan.ride("hgrn_bwd", l, grads))
        plan.done("hgrn_bwd", l, rode)
        grads["hgrn_gnorm_w"][l] = dgw.reshape(nb * D_HGRN // HEAD_DIM, HEAD_DIM).sum(0)
        dlb[l] = dlb_l.sum(0)
        dab, ddww, ddwb, dlnw, dlnb, dpww, dpwb = _conv_bwd(
            sv["proj"], dy_conv, sv["dww"], _row(p["conv_dw_b"][l]), _row(p["conv_ln_w"][l]),
            _row(p["conv_ln_b"][l]), sv["pww"], nb, s)
        grads["conv_dw_w"][l] = ddww[:CONV_W]
        grads["conv_dw_b"][l] = ddwb[0]
        grads["conv_ln_w"][l] = dlnw[0]
        grads["conv_ln_b"][l] = dlnb[0]
        grads["conv_pw_w"][l] = dpww
        grads["conv_pw_b"][l] = dpwb[0]
        pieces = [dqkv, dhq, dzf, dzb, dhi, dhg, dab]
        grads["w_in"][l] = _dw_in(sv["h0"], pieces, tm)
        dcur, dnw = _proj_bwd(pieces, wb[l]["w_in"], sv["x"], _row(p["mix_norm_w"][l]), dx1, tm)
        grads["mix_norm_w"][l] = dnw[0]

    dlog = _lower_bounds_bwd(p["hgrn_lb_logits"].reshape(DEPTH, 2 * D_HGRN),
                             jnp.stack(dlb).reshape(DEPTH, 2 * D_HGRN))
    out = {k: (v if k in BIG_AXIS else jnp.stack(v)) for k, v in grads.items() if k != "hgrn_lb_logits"}
    out["hgrn_lb_logits"] = dlog.reshape(DEPTH, 2, D_HGRN)
    return loss, dcur.reshape(nb, s, d), out


BIG_AXIS = {"w_in": 2, "w_out": 1, "w_gate": 2, "w_up": 2, "w_down": 1}
SMALL_SHARD_AXIS = {"hgrn_lb_logits": 2, "conv_dw_w": 2, "conv_pw_w": 1}
WEIGHTS = ("mix_norm_w", "w_in", "q_norm_w", "k_norm_w", "hgrn_lb_logits", "hgrn_gnorm_w", "conv_dw_w",
           "conv_dw_b", "conv_ln_w", "conv_ln_b", "conv_pw_w", "conv_pw_b", "attn_out_norm_w",
           "conv_out_norm_w", "w_out", "ffn_norm_w", "w_gate", "w_up", "w_down")
SMALL = tuple(n for n in WEIGHTS if n not in BIG_AXIS)


def _my_index():
    return 4 * lax.axis_index("x") + 2 * lax.axis_index("y") + lax.axis_index("c")


class _RidePlan:
    def __init__(self, srcs, gather):
        self.srcs = list(srcs)
        self.n = len(self.srcs)
        self.gather = list(gather) if isinstance(gather, (list, tuple)) else [gather] * self.n
        any_spec = pl.BlockSpec(memory_space=pl.ANY)
        self.in_specs = [any_spec] * self.n
        self.out_specs = [any_spec] * self.n
        self.out_shape = [jax.ShapeDtypeStruct(((N_DEV,) + s.shape) if g else s.shape, s.dtype)
                          for s, g in zip(self.srcs, self.gather)]
        npeer = N_DEV - 1
        self.scratch = [pltpu.SemaphoreType.DMA((self.n * npeer,)), pltpu.SemaphoreType.DMA((self.n * npeer,)),
                        pltpu.SemaphoreType.DMA((self.n,))] if self.n else []

    def copies(self, src_refs, out_refs, sems):
        if not self.n:
            return [], [], []
        send_sems, recv_sems, local_sems = sems
        npeer = N_DEV - 1
        x, y, c = lax.axis_index("x"), lax.axis_index("y"), lax.axis_index("c")
        me = 4 * x + 2 * y + c
        locals_, sends, recvs = [], [], []
        for a in range(self.n):
            src_ref, out_ref = src_refs[a], out_refs[a]

            def rows_for(j, src_ref=src_ref, gather=self.gather[a]):
                return src_ref if gather else src_ref.at[j]

            locals_.append(pltpu.make_async_copy(rows_for(me), out_ref.at[me], local_sems.at[a]))
            for k in range(1, N_DEV):
                px = (1 - x) if (k & 4) else x
                py = (1 - y) if (k & 2) else y
                pc = (1 - c) if (k & 1) else c
                pidx = 4 * px + 2 * py + pc
                common = dict(send_sem=send_sems.at[a * npeer + k - 1], recv_sem=recv_sems.at[a * npeer + k - 1],
                              device_id=(px, py, pc), device_id_type=pl.DeviceIdType.MESH)
                sends.append(pltpu.make_async_remote_copy(src_ref=rows_for(pidx), dst_ref=out_ref.at[me], **common))
                recvs.append(pltpu.make_async_remote_copy(src_ref=rows_for(pidx), dst_ref=out_ref.at[pidx],
                                                          **common))
        return locals_, sends, recvs


def _ride_plan(ride):
    return _RidePlan(*ride) if ride else _RidePlan([], True)


def _ride_start(copies, when=None):
    locals_, sends, _ = copies

    def go():
        for cp in locals_ + sends:
            cp.start()

    if locals_:
        go() if when is None else pl.when(when)(go)


def _ride_wait(copies, when=None):
    locals_, sends, recvs = copies

    def go():
        for cp in recvs:
            cp.wait_recv()
        for cp in sends:
            cp.wait_send()
        for cp in locals_:
            cp.wait()

    if locals_:
        go() if when is None else pl.when(when)(go)


def _exchange(srcs, gather, name):
    rd = _RidePlan(srcs, gather)

    def body(*refs):
        copies = rd.copies(refs[:rd.n], refs[rd.n:2 * rd.n], refs[2 * rd.n:])
        _ride_start(copies)
        _ride_wait(copies)

    return pl.pallas_call(body, name=name, in_specs=rd.in_specs, out_specs=rd.out_specs,
                          out_shape=rd.out_shape, scratch_shapes=rd.scratch)(*srcs)


def _lane_group(n):
    g = 1
    while (g * n) % LANES:
        g += 1
    return g


def _cols_to_natural(gathered, name):
    _, k, n = gathered.shape
    grp = _lane_group(n)
    place = jnp.stack([jnp.asarray(np.eye(n, grp * n, k=i * n), BF16) for i in range(grp)])

    def body(g_ref, p_ref, o_ref):
        acc = None
        for i in range(grp):
            part = _dot(g_ref[i], p_ref[i])
            acc = part if acc is None else acc + part
        o_ref[...] = acc.astype(BF16)

    return pl.pallas_call(
        body, name=name, grid=(N_DEV // grp,),
        in_specs=[pl.BlockSpec((grp, k, n), lambda j: (j, 0, 0)),
                  pl.BlockSpec((grp, n, grp * n), lambda j: (0, 0, 0))],
        out_specs=pl.BlockSpec((k, grp * n), lambda j: (0, j)),
        out_shape=jax.ShapeDtypeStruct((k, N_DEV * n), BF16),
        compiler_params=_cparams(("parallel",)),
    )(gathered, place)


def _natural_to_cols(dw, name):
    k, n8 = dw.shape
    n = n8 // N_DEV
    grp = _lane_group(n)
    pick = jnp.stack([jnp.asarray(np.eye(grp * n, n, k=-i * n), BF16) for i in range(grp)])

    def body(d_ref, p_ref, o_ref):
        xb = d_ref[...].astype(BF16)
        for i in range(grp):
            o_ref[i] = _dot(xb, p_ref[i]).astype(BF16)

    return pl.pallas_call(
        body, name=name, grid=(N_DEV // grp,),
        in_specs=[pl.BlockSpec((k, grp * n), lambda j: (0, j)),
                  pl.BlockSpec((grp, grp * n, n), lambda j: (0, 0, 0))],
        out_specs=pl.BlockSpec((grp, k, n), lambda j: (j, 0, 0)),
        out_shape=jax.ShapeDtypeStruct((N_DEV, k, n), BF16),
        compiler_params=_cparams(("parallel",)),
    )(dw, pick)


def _adamw_math(w, g, m, v):
    m = ADAM_B1 * m + (1.0 - ADAM_B1) * g
    v = ADAM_B2 * v + (1.0 - ADAM_B2) * (g * g)
    m_hat = m / (1.0 - ADAM_B1 ** ADAM_STEP)
    v_hat = v / (1.0 - ADAM_B2 ** ADAM_STEP)
    delta = -ADAM_LR * (m_hat / (jnp.sqrt(v_hat) + ADAM_EPS) + ADAM_WD * w)
    return delta, m, v


def _sum_adamw(parts, w, m, v, name):
    _, k, n = w.shape
    tk = k
    for cand in (256, 176, 128):
        if k % cand == 0:
            tk = cand
            break

    def body(*refs):
        p_refs = refs[:DEPTH]
        w_ref, m_ref, v_ref, g_ref, d_ref, mo_ref, vo_ref = refs[DEPTH:]
        for l in range(DEPTH):
            @pl.when(pl.program_id(0) == l)
            def _(p_ref=p_refs[l]):
                g = p_ref[0].astype(F32)
                for i in range(1, N_DEV):
                    g = g + p_ref[i].astype(F32)
                g_ref[...] = g
                d_ref[...], mo_ref[...], vo_ref[...] = _adamw_math(w_ref[...], g, m_ref[...], v_ref[...])

    row = pl.BlockSpec((None, tk, n), lambda l, i: (l, i, 0))
    shp = jax.ShapeDtypeStruct(w.shape, F32)
    return pl.pallas_call(
        body, name=name, grid=(DEPTH, k // tk),
        in_specs=[pl.BlockSpec((N_DEV, tk, n), lambda l, i: (0, i, 0))] * DEPTH + [row, row, row],
        out_specs=[row, row, row, row],
        out_shape=[shp, shp, shp, shp],
        compiler_params=_cparams(("parallel", "parallel")),
    )(*parts, w, m, v)


def _sum8(parts, name):
    r = parts.shape[1]

    def body(p_ref, g_ref):
        g = p_ref[0]
        for i in range(1, N_DEV):
            g = g + p_ref[i]
        g_ref[...] = g

    return pl.pallas_call(body, name=name, out_shape=jax.ShapeDtypeStruct((r, LANES), F32))(parts)


def _adamw(w, g, m, v):
    def body(w_ref, g_ref, m_ref, v_ref, d_ref, mo_ref, vo_ref):
        d_ref[...], mo_ref[...], vo_ref[...] = _adamw_math(w_ref[...], g_ref[...], m_ref[...], v_ref[...])

    shp = jax.ShapeDtypeStruct(w.shape, F32)
    return pl.pallas_call(body, name="adamw_small", out_shape=[shp, shp, shp])(w, g, m, v)


def _pack(arrays, dtype, row_multiple):
    flat = jnp.concatenate([a.reshape(-1).astype(dtype) for a in arrays])
    n = flat.shape[0]
    unit = row_multiple * LANES
    total = -(-n // unit) * unit
    return jnp.pad(flat, (0, total - n)).reshape(total // LANES, LANES)


def _unpack(flat2d, shapes, lead=()):
    flat = flat2d.reshape(lead + (-1,))
    out, off = [], 0
    for shp in shapes:
        n = int(np.prod(shp))
        out.append(flat[..., off:off + n].reshape(lead + tuple(shp)))
        off += n
    return out


def _shard_to_rows(full, axis):
    shp = full.shape
    k = shp[axis] // N_DEV
    r = full.reshape(shp[:axis] + (N_DEV, k) + shp[axis + 1:])
    return jnp.moveaxis(r, axis, 0)


def _rows_to_full(rows, axis):
    r = jnp.moveaxis(rows, 0, axis)
    shp = r.shape
    return r.reshape(shp[:axis] + (shp[axis] * shp[axis + 1],) + shp[axis + 2:])


def kernel(x, mix_norm_w, w_in, q_norm_w, k_norm_w, hgrn_lb_logits, hgrn_gnorm_w, conv_dw_w, conv_dw_b, conv_ln_w, conv_ln_b, conv_pw_w, conv_pw_b, attn_out_norm_w, conv_out_norm_w, w_out, ffn_norm_w, w_gate, w_up, w_down, loss_target, m_mix_norm_w, m_w_in, m_q_norm_w, m_k_norm_w, m_hgrn_lb_logits, m_hgrn_gnorm_w, m_conv_dw_w, m_conv_dw_b, m_conv_ln_w, m_conv_ln_b, m_conv_pw_w, m_conv_pw_b, m_attn_out_norm_w, m_conv_out_norm_w, m_w_out, m_ffn_norm_w, m_w_gate, m_w_up, m_w_down, v_mix_norm_w, v_w_in, v_q_norm_w, v_k_norm_w, v_hgrn_lb_logits, v_hgrn_gnorm_w, v_conv_dw_w, v_conv_dw_b, v_conv_ln_w, v_conv_ln_b, v_conv_pw_w, v_conv_pw_b, v_attn_out_norm_w, v_conv_out_norm_w, v_w_out, v_ffn_norm_w, v_w_gate, v_w_up, v_w_down):
    w_loc = dict(zip(WEIGHTS, (mix_norm_w, w_in, q_norm_w, k_norm_w, hgrn_lb_logits, hgrn_gnorm_w, conv_dw_w,
                               conv_dw_b, conv_ln_w, conv_ln_b, conv_pw_w, conv_pw_b, attn_out_norm_w,
                               conv_out_norm_w, w_out, ffn_norm_w, w_gate, w_up, w_down)))
    m_loc = dict(zip(WEIGHTS, (m_mix_norm_w, m_w_in, m_q_norm_w, m_k_norm_w, m_hgrn_lb_logits, m_hgrn_gnorm_w,
                               m_conv_dw_w, m_conv_dw_b, m_conv_ln_w, m_conv_ln_b, m_conv_pw_w, m_conv_pw_b,
                               m_attn_out_norm_w, m_conv_out_norm_w, m_w_out, m_ffn_norm_w, m_w_gate, m_w_up,
                               m_w_down)))
    v_loc = dict(zip(WEIGHTS, (v_mix_norm_w, v_w_in, v_q_norm_w, v_k_norm_w, v_hgrn_lb_logits, v_hgrn_gnorm_w,
                               v_conv_dw_w, v_conv_dw_b, v_conv_ln_w, v_conv_ln_b, v_conv_pw_w, v_conv_pw_b,
                               v_attn_out_norm_w, v_conv_out_norm_w, v_w_out, v_ffn_norm_w, v_w_gate, v_w_up,
                               v_w_down)))
    me = _my_index()
    big = tuple(BIG_AXIS)
    sms = tuple(SMALL_SHARD_AXIS)

    sm_shapes = [w_loc[n].shape for n in sms]
    got_s = _exchange([_pack([w_loc[n] for n in sms], F32, 8)], True, "gather_small_params")[0]
    p_full = {n: w_loc[n] for n in SMALL if n not in SMALL_SHARD_AXIS}
    for n, a in zip(sms, _unpack(got_s, sm_shapes, (N_DEV,))):
        p_full[n] = _rows_to_full(a, SMALL_SHARD_AXIS[n])

    def natural(n, gathered):
        if n == "w_in":
            return _cols_to_natural(gathered, "relayout_" + n)
        if n == "w_out":
            return gathered.reshape(-1, gathered.shape[-1])
        return gathered

    def to_send(n, gl):
        if n == "w_in":
            return _natural_to_cols(gl, "split_d" + n)
        if n == "w_out":
            return gl.reshape(N_DEV, gl.shape[0] // N_DEV, gl.shape[1]).astype(BF16)
        return gl

    class StepPlan:
        def __init__(self):
            self.w = [dict() for _ in range(DEPTH)]
            self.parts = [dict() for _ in range(DEPTH)]
            self.pending = {}
            got = _exchange([w_loc["w_in"][0].astype(BF16)], True, "gather_w_in")
            self.w[0]["w_in"] = natural("w_in", got[0])

        def ride(self, kernel_name, l, grads=None):
            want = []
            if kernel_name == "attn_fwd":
                want = [("w_out", l), ("w_gate", l)]
            elif kernel_name == "hgrn_fwd":
                want = [("w_up", l), ("w_down", l)]
            elif kernel_name == "ffn_fwd" and l + 1 < DEPTH:
                want = [("w_in", l + 1)]
            elif kernel_name == "ffn_bwd" and l + 1 < DEPTH:
                want = [("w_gate", l + 1), ("w_up", l + 1)]
            elif kernel_name == "attn_bwd" and l + 1 < DEPTH:
                want = [("w_in", l + 1), ("w_out", l + 1), ("w_down", l + 1)]
                if l == 0:
                    want += [("w_out", 0)]
            elif kernel_name == "hgrn_bwd" and l == 0:
                want = [(n, 0) for n in ("w_gate", "w_up", "w_down")]
            if not want:
                return None
            self.pending[(kernel_name, l)] = want
            if grads is None:
                return [w_loc[n][wl].astype(BF16) for n, wl in want], True
            return [to_send(n, grads[n][wl]) for n, wl in want], False

        def done(self, kernel_name, l, outs):
            want = self.pending.pop((kernel_name, l), [])
            for (n, wl), out in zip(want, outs):
                if kernel_name.endswith("_fwd"):
                    self.w[wl][n] = natural(n, out)
                else:
                    self.parts[wl][n] = out

    plan = StepPlan()
    loss_part, grad_x, g = _local_step(x, loss_target, p_full, plan)
    loss = lax.psum(loss_part[0, 0], MESH_AXES)

    pw = g["conv_pw_w"]
    k_pw = w_loc["conv_pw_w"].shape[1]
    pw_send = jnp.moveaxis(pw.reshape(DEPTH, N_DEV, k_pw, pw.shape[-1]), 1, 0).reshape(N_DEV, -1, LANES)
    gathered_small = [n for n in SMALL if n != "conv_pw_w"]
    small_shapes = [g[n].shape for n in gathered_small]
    din_parts, small_parts, pw_parts = _exchange(
        [to_send("w_in", g["w_in"][0]), _pack([g[n] for n in gathered_small], F32, 8), pw_send],
        [False, True, False], "exchange_last_grads")
    plan.parts[0]["w_in"] = din_parts
    big_out = {n: _sum_adamw([plan.parts[l][n] for l in range(DEPTH)], w_loc[n], m_loc[n], v_loc[n],
                             "sum_adamw_" + n) for n in big}

    g_small = dict(zip(gathered_small, _unpack(_sum8(small_parts, "sum_small_grads"), small_shapes)))
    g_small["conv_pw_w"] = _sum8(pw_parts, "sum_conv_pw_grads").reshape(w_loc["conv_pw_w"].shape)
    for n in sms:
        if n == "conv_pw_w":
            continue
        ax = SMALL_SHARD_AXIS[n]
        k = w_loc[n].shape[ax]
        g_small[n] = lax.dynamic_slice_in_dim(g_small[n], me * k, k, axis=ax)
    loc_shapes = [w_loc[n].shape for n in SMALL]
    packed = [_pack([d[n] for n in SMALL], F32, 8) for d in (w_loc, g_small, m_loc, v_loc)]
    res = _adamw(*packed)
    small_out = [g_small] + [dict(zip(SMALL, _unpack(r, loc_shapes))) for r in res]

    def pick(i, n):
        return big_out[n][i] if n in BIG_AXIS else small_out[i][n]

    return (loss, grad_x) + tuple(pick(i, n) for i in range(4) for n in WEIGHTS)
```

```python
import functools

import jax
import jax.numpy as jnp
import numpy as np
from jax import lax
from jax.experimental import pallas as pl
from jax.experimental.pallas import tpu as pltpu

F32 = jnp.float32
BF16 = jnp.bfloat16

D_MODEL = 1024
D_ATTN = 512
D_HGRN = 256
D_CONV = 256
HEAD_DIM = 64
N_HEADS = 8
N_KV = 2
GRID_W = 64
ROPE_THETA = 10000.0
CHUNK = 64
F_MIN = 1e-6
CONV_W = 31
CONV_PAD = 15
D_FF = 2816
D_PROJ = 2560
EPS = 1e-6
LN_EPS = 1e-5
DEPTH = 2
ADAM_LR = 0.001
ADAM_B1 = 0.9
ADAM_B2 = 0.999
ADAM_EPS = 1e-08
ADAM_WD = 0.01
ADAM_STEP = 10
N_DEV = 8
MESH_AXES = ("x", "y", "c")

COL_HQ, COL_ZFW, COL_ZBW, COL_HI, COL_HG = 6, 8, 10, 12, 14
COL_CA, COL_CB = 8, 9

LANES = 128
VMEM_LIMIT_MB = 56


def _cparams(dims=None):
    return pltpu.CompilerParams(dimension_semantics=dims, vmem_limit_bytes=VMEM_LIMIT_MB * 2 ** 20)


def _dot(a, b):
    return jnp.dot(a, b, preferred_element_type=F32)


def _dot_nt(a, b):
    return lax.dot_general(a, b, (((1,), (1,)), ((), ())), preferred_element_type=F32)


def _dot_tn(a, b):
    return lax.dot_general(a, b, (((0,), (0,)), ((), ())), preferred_element_type=F32)


def _split_bf16(x, parts):
    out = []
    r = x
    for _ in range(parts):
        p = r.astype(BF16)
        out.append(p)
        r = r - p.astype(F32)
    return out


def _dot_precise(x, m_bf16, parts=3):
    acc = None
    for p in _split_bf16(x, parts):
        t = _dot(p, m_bf16)
        acc = t if acc is None else acc + t
    return acc


def _block_ones(width, group):
    i = np.arange(width)
    return jnp.asarray((i[:, None] // group) == (i[None, :] // group), dtype=BF16)


def _sigmoid(x):
    return 1.0 / (1.0 + jnp.exp(-x))


def _rot(x):
    w = x.shape[1]
    lane = lax.broadcasted_iota(jnp.int32, x.shape, 1)
    first = (lane % 32) < 16
    return jnp.where(first, -pltpu.roll(x, w - 16, 1), pltpu.roll(x, 16, 1))


def _rope(x, cos, sin):
    return x * cos + _rot(x) * sin


def _rope_t(dy, cos, sin):
    return dy * cos - _rot(dy * sin)


def _row(v):
    return v.reshape(1, -1)


def _rms_proj(x, wn, w, tm):
    t, d = x.shape
    n = w.shape[1]

    def body(x_ref, wn_ref, w_ref, h_ref, y_ref):
        xv = x_ref[...]
        r = lax.rsqrt(jnp.mean(xv * xv, axis=-1, keepdims=True) + EPS)
        h = (xv * r * wn_ref[...]).astype(BF16)
        h_ref[...] = h
        y_ref[...] = _dot(h, w_ref[...])

    return pl.pallas_call(
        body, name="rms_proj", grid=(t // tm,),
        in_specs=[pl.BlockSpec((tm, d), lambda i: (i, 0)),
                  pl.BlockSpec((1, d), lambda i: (0, 0)),
                  pl.BlockSpec((d, n), lambda i: (0, 0))],
        out_specs=[pl.BlockSpec((tm, d), lambda i: (i, 0)),
                   pl.BlockSpec((tm, n), lambda i: (i, 0))],
        out_shape=[jax.ShapeDtypeStruct((t, d), BF16), jax.ShapeDtypeStruct((t, n), F32)],
        compiler_params=_cparams(("parallel",)),
    )(x, wn, w)


def _rms_bwd(dh, x, wn):
    r = lax.rsqrt(jnp.mean(x * x, axis=-1, keepdims=True) + EPS)
    g = dh * wn
    dx = r * (g - x * (r * r) * jnp.mean(g * x, axis=-1, keepdims=True))
    return dx, dh * x * r


def _proj_bwd(pieces, w, x, wn, dres, tm):
    t = x.shape[0]
    d = x.shape[1]
    n = w.shape[1]
    widths = [p.shape[1] for p in pieces]
    offs = [sum(widths[:i]) for i in range(len(widths))]
    assert sum(widths) == n
    npc = len(pieces)

    def body(*refs):
        p_refs = refs[:npc]
        w_ref, x_ref, wn_ref, dr_ref, dx_ref, dwn_ref = refs[npc:]
        dh = None
        for p_ref, o, wd in zip(p_refs, offs, widths):
            part = _dot_nt(p_ref[...], w_ref[:, o:o + wd])
            dh = part if dh is None else dh + part
        dx, dwn = _rms_bwd(dh, x_ref[...], wn_ref[...])
        dx_ref[...] = dr_ref[...] + dx

        @pl.when(pl.program_id(0) == 0)
        def _():
            dwn_ref[...] = jnp.zeros_like(dwn_ref)

        dwn_ref[...] += jnp.sum(dwn, axis=0, keepdims=True)

    return pl.pallas_call(
        body, name="proj_bwd", grid=(t // tm,),
        in_specs=[pl.BlockSpec((tm, wd), lambda i: (i, 0)) for wd in widths]
        + [pl.BlockSpec((d, n), lambda i: (0, 0)),
           pl.BlockSpec((tm, d), lambda i: (i, 0)),
           pl.BlockSpec((1, d), lambda i: (0, 0)),
           pl.BlockSpec((tm, d), lambda i: (i, 0))],
        out_specs=[pl.BlockSpec((tm, d), lambda i: (i, 0)),
                   pl.BlockSpec((1, d), lambda i: (0, 0))],
        out_shape=[jax.ShapeDtypeStruct((t, d), F32), jax.ShapeDtypeStruct((1, d), F32)],
        compiler_params=_cparams(("arbitrary",)),
    )(*pieces, w, x, wn, dres)


def _dw_in(h0, pieces, tm):
    t, k = h0.shape
    widths = [p.shape[1] for p in pieces]
    offs = [sum(widths[:i]) for i in range(len(widths))]
    n = sum(widths)
    npc = len(pieces)

    def body(*refs):
        h_ref = refs[0]
        p_refs = refs[1:1 + npc]
        o_ref = refs[1 + npc]

        @pl.when(pl.program_id(0) == 0)
        def _():
            o_ref[...] = jnp.zeros_like(o_ref)

        ht = h_ref[...].astype(F32).T.astype(BF16)
        for p_ref, o, wd in zip(p_refs, offs, widths):
            o_ref[:, o:o + wd] += _dot(ht, p_ref[...])

    return pl.pallas_call(
        body, name="dw_in", grid=(t // tm,),
        in_specs=[pl.BlockSpec((tm, k), lambda i: (i, 0))]
        + [pl.BlockSpec((tm, wd), lambda i: (i, 0)) for wd in widths],
        out_specs=pl.BlockSpec((k, n), lambda i: (0, 0)),
        out_shape=jax.ShapeDtypeStruct((k, n), F32),
        compiler_params=_cparams(("arbitrary",)),
    )(h0, *pieces)


def _mm_tn(a, b, tn, name, tm):
    t, k = a.shape
    n = b.shape[1]

    def body(a_ref, b_ref, o_ref):
        @pl.when(pl.program_id(1) == 0)
        def _():
            o_ref[...] = jnp.zeros_like(o_ref)

        o_ref[...] += _dot_tn(a_ref[...].astype(BF16), b_ref[...].astype(BF16))

    return pl.pallas_call(
        body, name=name, grid=(n // tn, t // tm),
        in_specs=[pl.BlockSpec((tm, k), lambda j, i: (i, 0)),
                  pl.BlockSpec((tm, tn), lambda j, i: (i, j))],
        out_specs=pl.BlockSpec((k, tn), lambda j, i: (0, j)),
        out_shape=jax.ShapeDtypeStruct((k, n), F32),
        compiler_params=_cparams(("parallel", "arbitrary")),
    )(a, b)


def _dw_ff(h, d, rows, name, tm):
    t = d.shape[0] if rows else h.shape[0]
    blocked, flat = (h, d) if rows else (d, h)
    fb = blocked.shape[2]
    dm = flat.shape[1]
    out_blk = (N_DEV, fb, dm) if rows else (N_DEV, dm, fb)

    def body(h_ref, d_ref, o_ref, acc_ref):
        i = pl.program_id(0)

        @pl.when(i == 0)
        def _():
            acc_ref[...] = jnp.zeros_like(acc_ref)

        if rows:
            db = d_ref[...].astype(BF16)
            for j in range(N_DEV):
                acc_ref[j] += _dot_tn(h_ref[j], db)
        else:
            ht = h_ref[...].astype(F32).T.astype(BF16)
            for j in range(N_DEV):
                acc_ref[j] += _dot(ht, d_ref[j])

        @pl.when(i == t // tm - 1)
        def _():
            o_ref[...] = acc_ref[...].astype(BF16)

    blk_spec = pl.BlockSpec((N_DEV, tm, fb), lambda i: (0, i, 0))
    flat_spec = pl.BlockSpec((tm, dm), lambda i: (i, 0))
    return pl.pallas_call(
        body, name=name, grid=(t // tm,),
        in_specs=[blk_spec, flat_spec] if rows else [flat_spec, blk_spec],
        out_specs=pl.BlockSpec(out_blk, lambda i: (0, 0, 0)),
        out_shape=jax.ShapeDtypeStruct(out_blk, BF16),
        scratch_shapes=[pltpu.VMEM(out_blk, F32)],
        compiler_params=_cparams(("arbitrary",)),
    )(h, d)


def _rope_tables(s):
    rows = s // GRID_W
    row_id = jnp.repeat(jnp.arange(rows, dtype=F32), GRID_W)
    col_id = jnp.tile(jnp.arange(GRID_W, dtype=F32), rows)
    half = HEAD_DIM // 2
    inv_freq = ROPE_THETA ** (-jnp.arange(0, half, 2, dtype=F32) / half)
    ang_r = row_id[:, None] * inv_freq[None, :]
    ang_c = col_id[:, None] * inv_freq[None, :]
    ang = jnp.concatenate([ang_r, ang_r, ang_c, ang_c], axis=-1)
    cos, sin = jnp.cos(ang), jnp.sin(ang)
    return jnp.tile(cos, (1, N_HEADS)), jnp.tile(sin, (1, N_HEADS))


def _head_rms(x, w, ones):
    r = lax.rsqrt(_dot_precise(x * x, ones, 2) * (1.0 / HEAD_DIM) + EPS)
    return x * r * w, r


def _dup_half(x, kv):
    lane = lax.broadcasted_iota(jnp.int32, x.shape, 1)
    sel = (lane < 64) if kv == 0 else (lane >= 64)
    return jnp.where(sel, x, pltpu.roll(x, 64, 1))


def _qkv_prep(proj, cosq, sinq, qw, kw, ones, s, tm):
    t = proj.shape[0]
    ns = s // tm

    def body(p_ref, cos_ref, sin_ref, qw_ref, kw_ref, ones_ref, q_out, kd_out, vd_out, kdt_out, vdt_out):
        cos = cos_ref[...]
        sin = sin_ref[...]
        ones_m = ones_ref[...]
        qn, _ = _head_rms(p_ref[:, 0:512], qw_ref[...], ones_m)
        q_out[...] = (_rope(qn, cos, sin) * (HEAD_DIM ** -0.5)).astype(BF16)
        kn, _ = _head_rms(p_ref[:, 512:640], kw_ref[...], ones_m[0:128, 0:128])
        kr = _rope(kn, cos[:, 0:128], sin[:, 0:128])
        v = p_ref[:, 640:768]
        for kv in range(N_KV):
            kd = _dup_half(kr, kv)
            vd = _dup_half(v, kv)
            kd_out[kv] = kd.astype(BF16)
            vd_out[kv] = vd.astype(BF16)
            kdt_out[kv] = kd.T.astype(BF16)
            vdt_out[kv] = vd.T.astype(BF16)

    return pl.pallas_call(
        body, name="qkv_prep", grid=(t // tm,),
        in_specs=[pl.BlockSpec((tm, 768), lambda i: (i, 0)),
                  pl.BlockSpec((tm, 512), lambda i: (i % ns, 0)),
                  pl.BlockSpec((tm, 512), lambda i: (i % ns, 0)),
                  pl.BlockSpec((1, 512), lambda i: (0, 0)),
                  pl.BlockSpec((1, 128), lambda i: (0, 0)),
                  pl.BlockSpec((512, 512), lambda i: (0, 0))],
        out_specs=[pl.BlockSpec((tm, 512), lambda i: (i, 0)),
                   pl.BlockSpec((N_KV, tm, 128), lambda i: (0, i, 0)),
                   pl.BlockSpec((N_KV, tm, 128), lambda i: (0, i, 0)),
                   pl.BlockSpec((N_KV, 128, tm), lambda i: (0, 0, i)),
                   pl.BlockSpec((N_KV, 128, tm), lambda i: (0, 0, i))],
        out_shape=[jax.ShapeDtypeStruct((t, 512), BF16),
                   jax.ShapeDtypeStruct((N_KV, t, 128), BF16),
                   jax.ShapeDtypeStruct((N_KV, t, 128), BF16),
                   jax.ShapeDtypeStruct((N_KV, 128, t), BF16),
                   jax.ShapeDtypeStruct((N_KV, 128, t), BF16)],
        compiler_params=_cparams(("parallel",)),
    )(proj, cosq, sinq, qw, kw, ones)


def _qkv_bwd(proj, dq, dkd, dvd, cosq, sinq, qw, kw, ones, s, tm):
    t = proj.shape[0]
    ns = s // tm

    def body(p_ref, dq_ref, dkd_ref, dvd_ref, cos_ref, sin_ref, qw_ref, kw_ref, ones_ref,
             out_ref, dqw_ref, dkw_ref):
        cos = cos_ref[...]
        sin = sin_ref[...]
        ones_m = ones_ref[...]
        ones_k = ones_m[0:128, 0:128]

        def norm_bwd(x, w, dn, om):
            r = lax.rsqrt(_dot_precise(x * x, om, 2) * (1.0 / HEAD_DIM) + EPS)
            g = dn * w
            dx = r * (g - x * (r * r) * (_dot_precise(g * x, om, 2) * (1.0 / HEAD_DIM)))
            return dx, jnp.sum(dn * x * r, axis=0, keepdims=True)

        q = p_ref[:, 0:512]
        dqn = _rope_t(dq_ref[...], cos, sin) * (HEAD_DIM ** -0.5)
        dq_raw, dqw = norm_bwd(q, qw_ref[...], dqn, ones_m)
        out_ref[:, 0:512] = dq_raw.astype(BF16)

        lane = lax.broadcasted_iota(jnp.int32, (tm, 128), 1)

        def fold(ref):
            a0 = ref[0]
            a1 = ref[1]
            f0 = a0 + pltpu.roll(a0, 64, 1)
            f1 = a1 + pltpu.roll(a1, 64, 1)
            return jnp.where(lane < 64, f0, f1)

        k = p_ref[:, 512:640]
        dkn = _rope_t(fold(dkd_ref), cos[:, 0:128], sin[:, 0:128])
        dk_raw, dkw = norm_bwd(k, kw_ref[...], dkn, ones_k)
        out_ref[:, 512:640] = dk_raw.astype(BF16)
        out_ref[:, 640:768] = fold(dvd_ref).astype(BF16)

        @pl.when(pl.program_id(0) == 0)
        def _():
            dqw_ref[...] = jnp.zeros_like(dqw_ref)
            dkw_ref[...] = jnp.zeros_like(dkw_ref)

        dqw_ref[...] += dqw
        dkw_ref[...] += dkw

    return pl.pallas_call(
        body, name="qkv_bwd", grid=(t // tm,),
        in_specs=[pl.BlockSpec((tm, 768), lambda i: (i, 0)),
                  pl.BlockSpec((tm, 512), lambda i: (i, 0)),
                  pl.BlockSpec((N_KV, tm, 128), lambda i: (0, i, 0)),
                  pl.BlockSpec((N_KV, tm, 128), lambda i: (0, i, 0)),
                  pl.BlockSpec((tm, 512), lambda i: (i % ns, 0)),
                  pl.BlockSpec((tm, 512), lambda i: (i % ns, 0)),
                  pl.BlockSpec((1, 512), lambda i: (0, 0)),
                  pl.BlockSpec((1, 128), lambda i: (0, 0)),
                  pl.BlockSpec((512, 512), lambda i: (0, 0))],
        out_specs=[pl.BlockSpec((tm, 768), lambda i: (i, 0)),
                   pl.BlockSpec((1, 512), lambda i: (0, 0)),
                   pl.BlockSpec((1, 128), lambda i: (0, 0))],
        out_shape=[jax.ShapeDtypeStruct((t, 768), BF16),
                   jax.ShapeDtypeStruct((1, 512), F32),
                   jax.ShapeDtypeStruct((1, 128), F32)],
        compiler_params=_cparams(("arbitrary",)),
    )(proj, dq, dkd, dvd, cosq, sinq, qw, kw, ones)


def _grid_step_id(grid):
    idx = pl.program_id(0)
    for ax in range(1, len(grid)):
        idx = idx * grid[ax] + pl.program_id(ax)
    return idx


def _attn_fwd(q, kd, vdt, nb, s, tq, ride=None):
    t = q.shape[0]
    nq = s // tq
    rd = _ride_plan(ride)
    grid = (nb, N_HEADS // 2, nq)
    nsteps = nb * (N_HEADS // 2) * nq

    def body(*refs):
        q_ref, k_ref, vt_ref = refs[:3]
        o_ref, lse_ref = refs[3 + rd.n:5 + rd.n]
        copies = rd.copies(refs[3:3 + rd.n], refs[5 + rd.n:5 + 2 * rd.n], refs[5 + 2 * rd.n:])
        step_id = _grid_step_id(grid)
        _ride_start(copies, step_id == 0)
        qv = q_ref[...].astype(F32)
        lane = lax.broadcasted_iota(jnp.int32, qv.shape, 1)
        k = k_ref[0]
        vt = vt_ref[0]
        outs = []
        for half in range(2):
            qh = jnp.where((lane < 64) if half == 0 else (lane >= 64), qv, 0.0).astype(BF16)
            st = _dot_nt(k, qh)
            m = jnp.max(st, axis=0, keepdims=True)
            p = jnp.exp(st - m)
            l = jnp.sum(p, axis=0, keepdims=True)
            ot = _dot(vt, p.astype(BF16)) / l
            lse_ref[0, half] = m + jnp.log(l)
            outs.append(ot)
        row = lax.broadcasted_iota(jnp.int32, outs[0].shape, 0)
        o_ref[...] = jnp.where(row < 64, outs[0], outs[1]).T
        _ride_wait(copies, step_id == nsteps - 1)

    return pl.pallas_call(
        body, name="attn_fwd", grid=grid,
        in_specs=[pl.BlockSpec((tq, 128), lambda b, p, i: (b * nq + i, p)),
                  pl.BlockSpec((1, s, 128), lambda b, p, i: (p // 2, b, 0)),
                  pl.BlockSpec((1, 128, s), lambda b, p, i: (p // 2, 0, b))] + rd.in_specs,
        out_specs=[pl.BlockSpec((tq, 128), lambda b, p, i: (b * nq + i, p)),
                   pl.BlockSpec((1, 2, 1, tq), lambda b, p, i: (b, p, 0, i))] + rd.out_specs,
        out_shape=[jax.ShapeDtypeStruct((t, D_ATTN), F32),
                   jax.ShapeDtypeStruct((nb, N_HEADS, 1, s), F32)] + rd.out_shape,
        scratch_shapes=rd.scratch,
        compiler_params=_cparams(("arbitrary", "arbitrary", "arbitrary")),
    )(q, kd, vdt, *rd.srcs)


def _attn_bwd(q, kd, vd, kdt, o, lse, do, nb, s, tq, ride=None):
    t = q.shape[0]
    nq = s // tq
    ones8 = jnp.ones((8, 128), BF16)
    rd = _ride_plan(ride)
    grid = (nb, N_KV, 2, nq)
    nsteps = nb * N_KV * 2 * nq

    def body(*refs):
        q_ref, k_ref, v_ref, kt_ref, o_ref, lse_ref, do_ref, ones_ref = refs[:8]
        dq_ref, dk_ref, dv_ref = refs[8 + rd.n:11 + rd.n]
        copies = rd.copies(refs[8:8 + rd.n], refs[11 + rd.n:11 + 2 * rd.n], refs[11 + 2 * rd.n:])
        step_id = _grid_step_id(grid)
        _ride_start(copies, step_id == 0)

        @pl.when((pl.program_id(2) == 0) & (pl.program_id(3) == 0))
        def _():
            dk_ref[...] = jnp.zeros_like(dk_ref)
            dv_ref[...] = jnp.zeros_like(dv_ref)

        qv = q_ref[...].astype(F32)
        dov = do_ref[...]
        ov = o_ref[...]
        lane = lax.broadcasted_iota(jnp.int32, qv.shape, 1)
        k = k_ref[0]
        v = v_ref[0]
        kt = kt_ref[0]
        dqs = []
        dk_acc = None
        dv_acc = None
        for half in range(2):
            sel = (lane < 64) if half == 0 else (lane >= 64)
            qh = jnp.where(sel, qv, 0.0).astype(BF16)
            doh = jnp.where(sel, dov, 0.0)
            dob = doh.astype(BF16)
            delta = None
            for part in _split_bf16(doh * ov, 3):
                d8 = _dot_nt(ones_ref[...], part)
                delta = d8 if delta is None else delta + d8
            delta = delta[0:1, :]
            st = _dot_nt(k, qh)
            pt = jnp.exp(st - lse_ref[0, half])
            dpt = _dot_nt(v, dob)
            dst = (pt * (dpt - delta)).astype(BF16)
            dkh = _dot(dst, qh)
            dvh = _dot(pt.astype(BF16), dob)
            dk_acc = dkh if dk_acc is None else dk_acc + dkh
            dv_acc = dvh if dv_acc is None else dv_acc + dvh
            dqs.append(_dot(kt, dst))
        dk_ref[0] += dk_acc
        dv_ref[0] += dv_acc
        row = lax.broadcasted_iota(jnp.int32, dqs[0].shape, 0)
        dq_ref[...] = jnp.where(row < 64, dqs[0], dqs[1]).T
        _ride_wait(copies, step_id == nsteps - 1)

    qmap = lambda b, g, p, i: (b * nq + i, g * 2 + p)
    kvmap = lambda b, g, p, i: (g, b, 0)
    return pl.pallas_call(
        body, name="attn_bwd", grid=grid,
        in_specs=[pl.BlockSpec((tq, 128), qmap),
                  pl.BlockSpec((1, s, 128), kvmap),
                  pl.BlockSpec((1, s, 128), kvmap),
                  pl.BlockSpec((1, 128, s), lambda b, g, p, i: (g, 0, b)),
                  pl.BlockSpec((tq, 128), qmap),
                  pl.BlockSpec((1, 2, 1, tq), lambda b, g, p, i: (b, g * 2 + p, 0, i)),
                  pl.BlockSpec((tq, 128), qmap),
                  pl.BlockSpec((8, 128), lambda b, g, p, i: (0, 0))] + rd.in_specs,
        out_specs=[pl.BlockSpec((tq, 128), qmap),
                   pl.BlockSpec((1, s, 128), kvmap),
                   pl.BlockSpec((1, s, 128), kvmap)] + rd.out_specs,
        out_shape=[jax.ShapeDtypeStruct((t, D_ATTN), F32),
                   jax.ShapeDtypeStruct((N_KV, t, 128), F32),
                   jax.ShapeDtypeStruct((N_KV, t, 128), F32)] + rd.out_shape,
        scratch_shapes=rd.scratch,
        compiler_params=_cparams(("arbitrary", "arbitrary", "arbitrary", "arbitrary")),
    )(q, kd, vd, kdt, o, lse, do, ones8, *rd.srcs)


SUB = 16
N_SUB = CHUNK // SUB


def _tri_mats():
    i = np.arange(CHUNK)
    same = (i[:, None] // SUB) == (i[None, :] // SUB)
    lower = jnp.asarray(same & (i[:, None] >= i[None, :]), dtype=BF16)
    upper = jnp.asarray(same & (i[:, None] <= i[None, :]), dtype=BF16)
    return jnp.stack([lower, upper])


def _running_sum(tri, x):
    acc = None
    for part in _split_bf16(x, 3):
        t = _dot(tri, part)
        acc = t if acc is None else acc + t
    return acc


def _gates(z, lb):
    sig = _sigmoid(z)
    f = lb + (1.0 - lb) * sig
    logf = jnp.log(jnp.maximum(f, F_MIN))
    k = (1.0 - lb) * (1.0 - sig)
    return sig, f, logf, k


def _row_group(jg, anti):
    if anti:
        return 0, 8 * jg + 8
    return 8 * jg, SUB


def _sub_order(anti):
    return range(N_SUB - 1, -1, -1) if anti else range(N_SUB)


def _block_columns(b, anti):
    tt = lax.broadcasted_iota(jnp.int32, (SUB, LANES), 0)
    cols = []
    for jg in range(SUB // 8):
        r0, r1 = _row_group(jg, anti)
        br = b[r0:r1]
        tr = tt[r0:r1]
        for i in range(8):
            sc = 8 * jg + i
            mask = (tr <= sc) if anti else (tr >= sc)
            cols.append((r0, r1, sc, jnp.where(mask, jnp.exp(jnp.minimum(br - b[sc:sc + 1], 0.0)), 0.0)))
    return cols


def _lockstep(gens):
    results = [None] * len(gens)
    live = list(range(len(gens)))
    while live:
        for i in list(live):
            try:
                next(gens[i])
            except StopIteration as stop:
                results[i] = stop.value
                live.remove(i)
    return results


def _scatter_rows(base, accs):
    pieces = []
    for g in range(SUB // 8):
        tot = base[8 * g:8 * g + 8]
        for (r0, r1), acc in accs.items():
            if r0 <= 8 * g and 8 * g + 8 <= r1:
                tot = tot + acc[8 * g - r0:8 * g - r0 + 8]
        pieces.append(tot)
    return jnp.concatenate(pieces, axis=0)


def _chunk_fwd(q, k, v, b, st, bones, bmask, anti):
    rs = [slice(SUB * i, SUB * i + SUB) for i in range(N_SUB)]
    decay, update, prods, spans, qbs = [], [], [], [], []
    for i in range(N_SUB):
        qi, ki, vi, bi = q[rs[i]], k[rs[i]], v[rs[i]], b[rs[i]]
        b_last = bi[0:1] if anti else bi[SUB - 1:SUB]
        decay.append(jnp.exp(b_last))
        update.append(_dot_tn(vi.astype(BF16), (ki * jnp.exp(b_last - bi)).astype(BF16)) * bmask)
        qbs.append((qi * jnp.exp(bi)).astype(BF16))
        for r0, r1, sc, e in _block_columns(bi, anti):
            prods.append(qi[r0:r1] * e * ki[sc:sc + 1])
            spans.append((i, r0, r1, sc))
    pb = _dot(jnp.concatenate(prods, axis=0).astype(BF16), bones)
    yield
    entered = [None] * N_SUB
    for i in _sub_order(anti):
        entered[i] = st
        st = st * decay[i] + update[i]
    yield
    accs = [dict() for _ in range(N_SUB)]
    off = 0
    for i, r0, r1, sc in spans:
        term = pb[off:off + r1 - r0] * v[rs[i]][sc:sc + 1]
        off += r1 - r0
        accs[i][(r0, r1)] = term if (r0, r1) not in accs[i] else accs[i][(r0, r1)] + term
    outs = [_scatter_rows(_dot_nt(qbs[i], entered[i].astype(BF16)), accs[i]) for i in range(N_SUB)]
    return jnp.concatenate(outs, axis=0), st, entered


def _hgrn_fwd(proj, lb, gw, nb, s, ride=None):
    t = proj.shape[0]
    nc = s // CHUNK
    tri = _tri_mats()
    bones = _block_ones(LANES, HEAD_DIM)
    rd = _ride_plan(ride)

    def body(*refs):
        q_ref, zf_ref, zb_ref, v_ref, g_ref, lb_ref, gw_ref, tri_ref, bones_ref = refs[:9]
        y_ref, os_ref, sts_ref = refs[9 + rd.n:12 + rd.n]
        st_ref = refs[12 + 2 * rd.n]
        copies = rd.copies(refs[9:9 + rd.n], refs[12 + rd.n:12 + 2 * rd.n], refs[13 + 2 * rd.n:])
        step_id = pl.program_id(0) * 2 + pl.program_id(1)
        _ride_start(copies, step_id == 0)
        bones_m = bones_ref[...]
        bmask = bones_m.astype(F32)
        st_ref[...] = jnp.zeros_like(st_ref)

        def one_direction(n, anti):
            side = 1 if anti else 0
            z_ref = zb_ref if anti else zf_ref
            cn = (nc - 1 - n) if anti else n
            rows = pl.ds(pl.multiple_of(cn * CHUNK, CHUNK), CHUNK)
            q = q_ref[rows, :]
            v = v_ref[rows, :]
            _, _, logf, k = _gates(z_ref[rows, :], lb_ref[side:side + 1])
            b = _running_sum(tri_ref[side], logf)
            yield
            o, st_new, entered = yield from _chunk_fwd(q, k, v, b, st_ref[side], bones_m, bmask, anti)
            for i in range(N_SUB):
                sts_ref[0, 0, side, cn * N_SUB + i] = entered[i].astype(BF16)
            st_ref[side] = st_new
            (y_ref if anti else os_ref)[rows, :] = o

        def step(n, carry):
            _lockstep([one_direction(n, False), one_direction(n, True)])
            return carry

        lax.fori_loop(0, nc, step, 0)

        def join(n, carry):
            rows = pl.ds(pl.multiple_of(n * CHUNK, CHUNK), CHUNK)
            osum = os_ref[rows, :] + y_ref[rows, :]
            os_ref[rows, :] = osum
            r = lax.rsqrt(_dot_precise(osum * osum, bones_m, 2) * (1.0 / HEAD_DIM) + EPS)
            hg = g_ref[rows, :]
            y_ref[rows, :] = osum * r * gw_ref[...] * (hg * _sigmoid(hg))
            return carry

        lax.fori_loop(0, nc, join, 0)
        _ride_wait(copies, step_id == nb * 2 - 1)

    def col(c):
        return pl.BlockSpec((s, LANES), lambda b, p, c=c: (b, c + p))

    return pl.pallas_call(
        body, name="hgrn_fwd", grid=(nb, 2),
        in_specs=[col(COL_HQ), col(COL_ZFW), col(COL_ZBW), col(COL_HI), col(COL_HG),
                  pl.BlockSpec((2, LANES), lambda b, p: (0, p)),
                  pl.BlockSpec((1, LANES), lambda b, p: (0, 0)),
                  pl.BlockSpec((2, CHUNK, CHUNK), lambda b, p: (0, 0, 0)),
                  pl.BlockSpec((LANES, LANES), lambda b, p: (0, 0))] + rd.in_specs,
        out_specs=[pl.BlockSpec((s, LANES), lambda b, p: (b, p)),
                   pl.BlockSpec((s, LANES), lambda b, p: (b, p)),
                   pl.BlockSpec((1, 1, 2, nc * N_SUB, LANES, LANES), lambda b, p: (b, p, 0, 0, 0, 0))]
        + rd.out_specs,
        out_shape=[jax.ShapeDtypeStruct((t, D_HGRN), F32), jax.ShapeDtypeStruct((t, D_HGRN), F32),
                   jax.ShapeDtypeStruct((nb, 2, 2, nc * N_SUB, LANES, LANES), BF16)] + rd.out_shape,
        scratch_shapes=[pltpu.VMEM((2, LANES, LANES), F32)] + rd.scratch,
        compiler_params=_cparams(("arbitrary", "arbitrary")),
    )(proj, proj, proj, proj, proj, lb, gw, tri, bones, *rd.srcs)


def _chunk_bwd(q, k, v, b, do, states, rt, bones, bmask, anti):
    rs = [slice(SUB * i, SUB * i + SUB) for i in range(N_SUB)]
    r8 = lax.broadcasted_iota(jnp.int32, (8, LANES), 0)
    decay, update, dq_inter, ebls, prods_p, prods_d, spans, qes, kes = [], [], [], [], [], [], [], [], []
    for i in range(N_SUB):
        qi, ki, vi, bi, doi = q[rs[i]], k[rs[i]], v[rs[i]], b[rs[i]], do[rs[i]]
        b_last = bi[0:1] if anti else bi[SUB - 1:SUB]
        eb = jnp.exp(bi)
        dob = doi.astype(BF16)
        decay.append(jnp.exp(b_last))
        ebls.append(jnp.exp(b_last - bi))
        update.append(_dot_tn(dob, (qi * eb).astype(BF16)) * bmask)
        dq_inter.append(eb * _dot(dob, states[i]))
        for r0, r1, sc, e in _block_columns(bi, anti):
            qe = qi[r0:r1] * e
            qes.append(qe)
            kes.append(e * ki[sc:sc + 1])
            prods_p.append(qe * ki[sc:sc + 1])
            prods_d.append(doi[r0:r1] * vi[sc:sc + 1])
            spans.append((i, r0, r1, sc))
    sums = _dot(jnp.concatenate(prods_p + prods_d, axis=0).astype(BF16), bones)
    half = sum(r1 - r0 for _, r0, r1, _ in spans)
    yield
    entered = [None] * N_SUB
    for i in reversed(list(_sub_order(anti))):
        entered[i] = rt
        rt = rt * decay[i] + update[i]
    yield
    accs = [dict() for _ in range(N_SUB)]
    dk_blks = [[jnp.zeros((8, LANES), F32) for _ in range(SUB // 8)] for _ in range(N_SUB)]
    dv_blks = [[jnp.zeros((8, LANES), F32) for _ in range(SUB // 8)] for _ in range(N_SUB)]
    off = 0
    for n, (i, r0, r1, sc) in enumerate(spans):
        nr = r1 - r0
        pb = sums[off:off + nr]
        dpb = sums[half + off:half + off + nr]
        off += nr
        term = dpb * kes[n]
        accs[i][(r0, r1)] = term if (r0, r1) not in accs[i] else accs[i][(r0, r1)] + term
        dk_s = jnp.sum(dpb * qes[n], axis=0, keepdims=True)
        dv_s = jnp.sum(pb * do[rs[i]][r0:r1], axis=0, keepdims=True)
        dk_blks[i][sc // 8] = jnp.where(r8 == sc % 8, dk_s, dk_blks[i][sc // 8])
        dv_blks[i][sc // 8] = jnp.where(r8 == sc % 8, dv_s, dv_blks[i][sc // 8])
    dqs, dks, dvs, dbs = [], [], [], []
    for i in range(N_SUB):
        ki, vi = k[rs[i]], v[rs[i]]
        rtb = entered[i].astype(BF16)
        dk_inter = ebls[i] * _dot(vi.astype(BF16), rtb)
        dv_inter = _dot_nt((ki * ebls[i]).astype(BF16), rtb)
        dqs.append(_scatter_rows(dq_inter[i], accs[i]))
        dks.append(dk_inter + jnp.concatenate(dk_blks[i], axis=0))
        dvs.append(dv_inter + jnp.concatenate(dv_blks[i], axis=0))
        db_last = (jnp.sum(ki * dk_inter, axis=0, keepdims=True)
                   + decay[i] * jnp.sum(entered[i] * states[i].astype(F32), axis=0, keepdims=True))
        dbs.append(jnp.broadcast_to(db_last, (SUB, LANES)))
    cat = lambda xs: jnp.concatenate(xs, axis=0)
    return cat(dqs), cat(dks), cat(dvs), rt, cat(dbs)


def _hgrn_bwd(proj, lb, gw, osum, states, dy, nb, s, ride=None):
    t = proj.shape[0]
    nc = s // CHUNK
    assert nc % 2 == 0
    tri = _tri_mats()
    bones = _block_ones(LANES, HEAD_DIM)
    rd = _ride_plan(ride)

    def body(*refs):
        (q_ref, zf_ref, zb_ref, v_ref, g_ref, lb_ref, gw_ref, os_ref, sts_ref, dy_ref, tri_ref,
         bones_ref) = refs[:12]
        dq_ref, dzf_ref, dzb_ref, dv_ref, dg_ref, dgw_ref, dlb_ref = refs[12 + rd.n:19 + rd.n]
        do_sc, dq_sc, dv_sc, rt_cur = refs[19 + 2 * rd.n:23 + 2 * rd.n]
        copies = rd.copies(refs[12:12 + rd.n], refs[19 + rd.n:19 + 2 * rd.n], refs[23 + 2 * rd.n:])
        step_id = pl.program_id(0) * 2 + pl.program_id(1)
        _ride_start(copies, step_id == 0)
        bones_m = bones_ref[...]
        bmask = bones_m.astype(F32)
        gwv = gw_ref[...]

        def head(n, acc):
            rows = pl.ds(pl.multiple_of(n * CHUNK, CHUNK), CHUNK)
            o = os_ref[rows, :]
            hg = g_ref[rows, :]
            dyv = dy_ref[rows, :]
            sg = _sigmoid(hg)
            r = lax.rsqrt(_dot_precise(o * o, bones_m, 2) * (1.0 / HEAD_DIM) + EPS)
            nrm = o * r * gwv
            dn = dyv * (hg * sg)
            dg_ref[rows, :] = (dyv * nrm * (sg * (1.0 + hg * (1.0 - sg)))).astype(BF16)
            g = dn * gwv
            mean_go = _dot_precise(g * o, bones_m, 2) * (1.0 / HEAD_DIM)
            do_sc[rows, :] = r * (g - o * (r * r) * mean_go)
            return acc + jnp.sum(dn * o * r, axis=0, keepdims=True)

        dgw_ref[0] = lax.fori_loop(0, nc, head, jnp.zeros((1, LANES), F32))
        dq_sc[...] = jnp.zeros_like(dq_sc)
        dv_sc[...] = jnp.zeros_like(dv_sc)

        rt_cur[...] = jnp.zeros_like(rt_cur)

        def one_direction(n, anti):
            side = 1 if anti else 0
            z_ref = zb_ref if anti else zf_ref
            dz_ref = dzb_ref if anti else dzf_ref
            lbv = lb_ref[side:side + 1]
            cn = n if anti else (nc - 1 - n)
            rows = pl.ds(pl.multiple_of(cn * CHUNK, CHUNK), CHUNK)
            q = q_ref[rows, :]
            v = v_ref[rows, :]
            sig, f, logf, k = _gates(z_ref[rows, :], lbv)
            b = _running_sum(tri_ref[side], logf)
            yield
            do = do_sc[rows, :]
            entered = [sts_ref[0, 0, side, cn * N_SUB + i] for i in range(N_SUB)]
            dq, dk, dv, rt_new, db_last = yield from _chunk_bwd(q, k, v, b, do, entered, rt_cur[side], bones_m,
                                                                bmask, anti)
            rt_cur[side] = rt_new
            dq_sc[rows, :] += dq
            dv_sc[rows, :] += dv
            dlogf = _running_sum(tri_ref[1 - side], q * dq - k * dk) + db_last
            dfl = jnp.where(f > F_MIN, dlogf / f, 0.0)
            dz_ref[rows, :] = ((dfl - dk) * (1.0 - lbv) * sig * (1.0 - sig)).astype(BF16)
            return jnp.sum((dfl - dk) * (1.0 - sig), axis=0, keepdims=True)

        def back(n, dlb):
            d0, d1 = _lockstep([one_direction(n, False), one_direction(n, True)])
            return dlb[0] + d0, dlb[1] + d1

        zero = jnp.zeros((1, LANES), F32)
        dlb0, dlb1 = lax.fori_loop(0, nc, back, (zero, zero))
        dlb_ref[0, 0:1, :] = dlb0
        dlb_ref[0, 1:2, :] = dlb1

        dq_ref[...] = dq_sc[...].astype(BF16)
        dv_ref[...] = dv_sc[...].astype(BF16)
        _ride_wait(copies, step_id == nb * 2 - 1)

    def col(c):
        return pl.BlockSpec((s, LANES), lambda b, p, c=c: (b, c + p))

    sl = pl.BlockSpec((s, LANES), lambda b, p: (b, p))
    out_t = jax.ShapeDtypeStruct((t, D_HGRN), BF16)
    return pl.pallas_call(
        body, name="hgrn_bwd", grid=(nb, 2),
        in_specs=[col(COL_HQ), col(COL_ZFW), col(COL_ZBW), col(COL_HI), col(COL_HG),
                  pl.BlockSpec((2, LANES), lambda b, p: (0, p)),
                  pl.BlockSpec((1, LANES), lambda b, p: (0, 0)),
                  sl,
                  pl.BlockSpec((1, 1, 2, nc * N_SUB, LANES, LANES), lambda b, p: (b, p, 0, 0, 0, 0)),
                  sl,
                  pl.BlockSpec((2, CHUNK, CHUNK), lambda b, p: (0, 0, 0)),
                  pl.BlockSpec((LANES, LANES), lambda b, p: (0, 0))] + rd.in_specs,
        out_specs=[sl, sl, sl, sl, sl,
                   pl.BlockSpec((1, 1, LANES), lambda b, p: (b, 0, p)),
                   pl.BlockSpec((1, 2, LANES), lambda b, p: (b, 0, p))] + rd.out_specs,
        out_shape=[out_t, out_t, out_t, out_t, out_t,
                   jax.ShapeDtypeStruct((nb, 1, D_HGRN), F32),
                   jax.ShapeDtypeStruct((nb, 2, D_HGRN), F32)] + rd.out_shape,
        scratch_shapes=[pltpu.VMEM((s, LANES), F32), pltpu.VMEM((s, LANES), F32), pltpu.VMEM((s, LANES), F32),
                        pltpu.VMEM((2, LANES, LANES), F32)] + rd.scratch,
        compiler_params=_cparams(("arbitrary", "arbitrary")),
    )(proj, proj, proj, proj, proj, lb, gw, osum, states, dy, tri, bones, *rd.srcs)


def _lower_bounds(logits):
    def body(lg_ref, lb_ref):
        rows = [lg_ref[l:l + 1, :] for l in range(DEPTH)]
        m = functools.reduce(jnp.maximum, rows)
        ex = [jnp.exp(r - m) for r in rows]
        den = functools.reduce(jnp.add, ex)
        run = jnp.zeros_like(m)
        for l in range(DEPTH):
            if l > 0:
                run = run + ex[l] / den
            lb_ref[l:l + 1, :] = run

    return pl.pallas_call(body, name="lower_bounds", out_shape=jax.ShapeDtypeStruct(logits.shape, F32))(logits)


def _lower_bounds_bwd(logits, dlb):
    def body(lg_ref, dlb_ref, dlg_ref):
        rows = [lg_ref[l:l + 1, :] for l in range(DEPTH)]
        m = functools.reduce(jnp.maximum, rows)
        ex = [jnp.exp(r - m) for r in rows]
        den = functools.reduce(jnp.add, ex)
        sm = [e / den for e in ex]
        dsm = [jnp.zeros_like(m) for _ in range(DEPTH)]
        for i in range(1, DEPTH):
            for l in range(i, DEPTH):
                dsm[i] = dsm[i] + dlb_ref[l:l + 1, :]
        dot = functools.reduce(jnp.add, [sm[i] * dsm[i] for i in range(DEPTH)])
        for i in range(DEPTH):
            dlg_ref[i:i + 1, :] = sm[i] * (dsm[i] - dot)

    return pl.pallas_call(body, name="lower_bounds_bwd", out_shape=jax.ShapeDtypeStruct(logits.shape, F32))(logits, dlb)


CONV_ROWS = 128


def _conv_core(a, bg, dww, dwb, lnw, lnb, upad_ref, s):
    sb = _sigmoid(bg)
    u = a * sb
    upad_ref[0:16, :] = jnp.zeros((16, D_CONV), F32)
    upad_ref[16:16 + s, :] = u
    upad_ref[16 + s:32 + s, :] = jnp.zeros((16, D_CONV), F32)
    rows = min(s, CONV_ROWS)
    pieces = []
    for r0 in range(0, s, rows):
        acc = None
        for j in range(CONV_W):
            term = upad_ref[r0 + 1 + j:r0 + 1 + j + rows, :] * dww[j:j + 1, :]
            acc = term if acc is None else acc + term
        pieces.append(acc)
    c = jnp.concatenate(pieces, axis=0) + dwb
    mu = jnp.mean(c, axis=-1, keepdims=True)
    xc = c - mu
    rstd = lax.rsqrt(jnp.mean(xc * xc, axis=-1, keepdims=True) + LN_EPS)
    nh = xc * rstd
    l = nh * lnw + lnb
    sl = _sigmoid(l)
    return sb, nh, rstd, l, sl


def _conv_fwd(proj, dww, dwb, lnw, lnb, pww, pwb, nb, s):
    t = proj.shape[0]
    assert s % min(s, CONV_ROWS) == 0

    def body(a_ref, b_ref, dww_ref, dwb_ref, lnw_ref, lnb_ref, pww_ref, pwb_ref, y_ref, upad_ref):
        _, _, _, l, sl = _conv_core(a_ref[...], b_ref[...], dww_ref[...], dwb_ref[...], lnw_ref[...],
                                    lnb_ref[...], upad_ref, s)
        y_ref[...] = _dot((l * sl).astype(BF16), pww_ref[...]) + pwb_ref[...]

    vec = pl.BlockSpec((1, D_CONV), lambda b: (0, 0))
    return pl.pallas_call(
        body, name="conv_fwd", grid=(nb,),
        in_specs=[pl.BlockSpec((s, D_CONV), lambda b: (b, COL_CA)),
                  pl.BlockSpec((s, D_CONV), lambda b: (b, COL_CB)),
                  pl.BlockSpec((32, D_CONV), lambda b: (0, 0)), vec, vec, vec,
                  pl.BlockSpec((D_CONV, D_CONV), lambda b: (0, 0)), vec],
        out_specs=pl.BlockSpec((s, D_CONV), lambda b: (b, 0)),
        out_shape=jax.ShapeDtypeStruct((t, D_CONV), F32),
        scratch_shapes=[pltpu.VMEM((s + 32, D_CONV), F32)],
        compiler_params=_cparams(("parallel",)),
    )(proj, proj, dww, dwb, lnw, lnb, pww, pwb)


def _conv_bwd(proj, dy, dww, dwb, lnw, lnb, pww, nb, s):
    t = proj.shape[0]

    def body(a_ref, b_ref, dy_ref, dww_ref, dwb_ref, lnw_ref, lnb_ref, pww_ref,
             dab_ref, ddww_ref, ddwb_ref, dlnw_ref, dlnb_ref, dpww_ref, dpwb_ref, upad_ref, dcpad_ref):
        a = a_ref[...]
        dww = dww_ref[...]
        sb, nh, rstd, l, sl = _conv_core(a, b_ref[...], dww, dwb_ref[...], lnw_ref[...], lnb_ref[...],
                                         upad_ref, s)
        dyv = dy_ref[...]
        dyb = dyv.astype(BF16)
        ds = _dot_nt(dyb, pww_ref[...])
        dl = ds * (sl * (1.0 + l * (1.0 - sl)))
        dn = dl * lnw_ref[...]
        dc = rstd * (dn - jnp.mean(dn, axis=-1, keepdims=True)
                     - nh * jnp.mean(dn * nh, axis=-1, keepdims=True))

        @pl.when(pl.program_id(0) == 0)
        def _():
            for r in (ddww_ref, ddwb_ref, dlnw_ref, dlnb_ref, dpww_ref, dpwb_ref):
                r[...] = jnp.zeros_like(r)

        dpww_ref[...] += _dot_tn((l * sl).astype(BF16), dyb)
        dpwb_ref[...] += jnp.sum(dyv, axis=0, keepdims=True)
        dlnw_ref[...] += jnp.sum(dl * nh, axis=0, keepdims=True)
        dlnb_ref[...] += jnp.sum(dl, axis=0, keepdims=True)
        ddwb_ref[...] += jnp.sum(dc, axis=0, keepdims=True)

        dcpad_ref[0:16, :] = jnp.zeros((16, D_CONV), F32)
        dcpad_ref[16:16 + s, :] = dc
        dcpad_ref[16 + s:32 + s, :] = jnp.zeros((16, D_CONV), F32)
        rows = min(s, CONV_ROWS)
        r8 = lax.broadcasted_iota(jnp.int32, (32, D_CONV), 0)
        ddww = jnp.zeros((32, D_CONV), F32)
        pieces = []
        for r0 in range(0, s, rows):
            acc = None
            dcr = dcpad_ref[16 + r0:16 + r0 + rows, :]
            for j in range(CONV_W):
                term = dcpad_ref[r0 + 31 - j:r0 + 31 - j + rows, :] * dww[j:j + 1, :]
                acc = term if acc is None else acc + term
                wj = jnp.sum(dcr * upad_ref[r0 + 1 + j:r0 + 1 + j + rows, :], axis=0, keepdims=True)
                ddww = ddww + jnp.where(r8 == j, wj, 0.0)
            pieces.append(acc)
        du = jnp.concatenate(pieces, axis=0)
        ddww_ref[...] += ddww
        dab_ref[:, 0:D_CONV] = (du * sb).astype(BF16)
        dab_ref[:, D_CONV:2 * D_CONV] = (du * a * sb * (1.0 - sb)).astype(BF16)

    vec = pl.BlockSpec((1, D_CONV), lambda b: (0, 0))
    mat = pl.BlockSpec((D_CONV, D_CONV), lambda b: (0, 0))
    w32 = pl.BlockSpec((32, D_CONV), lambda b: (0, 0))
    vshape = jax.ShapeDtypeStruct((1, D_CONV), F32)
    return pl.pallas_call(
        body, name="conv_bwd", grid=(nb,),
        in_specs=[pl.BlockSpec((s, D_CONV), lambda b: (b, COL_CA)),
                  pl.BlockSpec((s, D_CONV), lambda b: (b, COL_CB)),
                  pl.BlockSpec((s, D_CONV), lambda b: (b, 0)),
                  w32, vec, vec, vec, mat],
        out_specs=[pl.BlockSpec((s, 2 * D_CONV), lambda b: (b, 0)), w32, vec, vec, vec, mat, vec],
        out_shape=[jax.ShapeDtypeStruct((t, 2 * D_CONV), BF16),
                   jax.ShapeDtypeStruct((32, D_CONV), F32), vshape, vshape, vshape,
                   jax.ShapeDtypeStruct((D_CONV, D_CONV), F32), vshape],
        scratch_shapes=[pltpu.VMEM((s + 32, D_CONV), F32), pltpu.VMEM((s + 32, D_CONV), F32)],
        compiler_params=_cparams(("arbitrary",)),
    )(proj, proj, dy, dww, dwb, lnw, lnb, pww)


def _mix_out(o_attn, y_hgrn, y_conv, x, aw, cw, w_out, tm):
    t = x.shape[0]

    def body(o_ref, h_ref, c_ref, x_ref, aw_ref, cw_ref, w_ref, mixed_ref, x1_ref):
        o = o_ref[...]
        a = o * lax.rsqrt(jnp.mean(o * o, axis=-1, keepdims=True) + EPS) * aw_ref[...]
        yc = c_ref[...]
        c = yc * lax.rsqrt(jnp.mean(yc * yc, axis=-1, keepdims=True) + EPS) * cw_ref[...]
        ab, hb, cb = a.astype(BF16), h_ref[...].astype(BF16), c.astype(BF16)
        mixed_ref[:, 0:512] = ab
        mixed_ref[:, 512:768] = hb
        mixed_ref[:, 768:1024] = cb
        x1_ref[...] = (x_ref[...] + _dot(ab, w_ref[0:512, :]) + _dot(hb, w_ref[512:768, :])
                       + _dot(cb, w_ref[768:1024, :]))

    def tok(w):
        return pl.BlockSpec((tm, w), lambda i: (i, 0))

    return pl.pallas_call(
        body, name="mix_out", grid=(t // tm,),
        in_specs=[tok(512), tok(256), tok(256), tok(D_MODEL),
                  pl.BlockSpec((1, 512), lambda i: (0, 0)), pl.BlockSpec((1, 256), lambda i: (0, 0)),
                  pl.BlockSpec((D_MODEL, D_MODEL), lambda i: (0, 0))],
        out_specs=[tok(D_MODEL), tok(D_MODEL)],
        out_shape=[jax.ShapeDtypeStruct((t, D_MODEL), BF16), jax.ShapeDtypeStruct((t, D_MODEL), F32)],
        compiler_params=_cparams(("parallel",)),
    )(o_attn, y_hgrn, y_conv, x, aw, cw, w_out)


def _mix_out_bwd(dx1, w_out, o_attn, y_conv, aw, cw, tm):
    t = dx1.shape[0]

    def body(dx_ref, w_ref, o_ref, c_ref, aw_ref, cw_ref, do_ref, dh_ref, dc_ref, daw_ref, dcw_ref):
        dm = _dot_nt(dx_ref[...].astype(BF16), w_ref[...])
        do, daw = _rms_bwd(dm[:, 0:512], o_ref[...], aw_ref[...])
        dc, dcw = _rms_bwd(dm[:, 768:1024], c_ref[...], cw_ref[...])
        do_ref[...] = do
        dh_ref[...] = dm[:, 512:768]
        dc_ref[...] = dc

        @pl.when(pl.program_id(0) == 0)
        def _():
            daw_ref[...] = jnp.zeros_like(daw_ref)
            dcw_ref[...] = jnp.zeros_like(dcw_ref)

        daw_ref[...] += jnp.sum(daw, axis=0, keepdims=True)
        dcw_ref[...] += jnp.sum(dcw, axis=0, keepdims=True)

    def tok(w):
        return pl.BlockSpec((tm, w), lambda i: (i, 0))

    v512 = pl.BlockSpec((1, 512), lambda i: (0, 0))
    v256 = pl.BlockSpec((1, 256), lambda i: (0, 0))
    return pl.pallas_call(
        body, name="mix_out_bwd", grid=(t // tm,),
        in_specs=[tok(D_MODEL), pl.BlockSpec((D_MODEL, D_MODEL), lambda i: (0, 0)), tok(512), tok(256),
                  v512, v256],
        out_specs=[tok(512), tok(256), tok(256), v512, v256],
        out_shape=[jax.ShapeDtypeStruct((t, 512), F32), jax.ShapeDtypeStruct((t, 256), F32),
                   jax.ShapeDtypeStruct((t, 256), F32), jax.ShapeDtypeStruct((1, 512), F32),
                   jax.ShapeDtypeStruct((1, 256), F32)],
        compiler_params=_cparams(("arbitrary",)),
    )(dx1, w_out, o_attn, y_conv, aw, cw)


FF_BLOCKS = 4


def _ffn_fwd(x1, fw, wg, wu, wd, tm, ride=None):
    t = x1.shape[0]
    fb = wg.shape[2]
    nf = N_DEV // FF_BLOCKS
    rd = _ride_plan(ride)
    grid = (t // tm, nf)

    def body(*refs):
        x_ref, fw_ref, wg_ref, wu_ref, wd_ref = refs[:5]
        h_ref, g_ref, u_ref, a_ref, x2_ref = refs[5 + rd.n:10 + rd.n]
        acc_ref = refs[10 + 2 * rd.n]
        copies = rd.copies(refs[5:5 + rd.n], refs[10 + rd.n:10 + 2 * rd.n], refs[11 + 2 * rd.n:])
        step_id = _grid_step_id(grid)
        _ride_start(copies, step_id == 0)
        j = pl.program_id(1)

        @pl.when(j == 0)
        def _():
            xv = x_ref[...]
            r = lax.rsqrt(jnp.mean(xv * xv, axis=-1, keepdims=True) + EPS)
            h_ref[...] = (xv * r * fw_ref[...]).astype(BF16)
            acc_ref[...] = xv

        h = h_ref[...]
        out = None
        for c in range(FF_BLOCKS):
            g = _dot(h, wg_ref[c])
            u = _dot(h, wu_ref[c])
            a = (g * _sigmoid(g) * u).astype(BF16)
            g_ref[c] = g.astype(BF16)
            u_ref[c] = u.astype(BF16)
            a_ref[c] = a
            part = _dot(a, wd_ref[c])
            out = part if out is None else out + part
        acc_ref[...] += out

        @pl.when(j == nf - 1)
        def _():
            x2_ref[...] = acc_ref[...]

        _ride_wait(copies, step_id == (t // tm) * nf - 1)

    tok = pl.BlockSpec((tm, D_MODEL), lambda i, j: (i, 0))
    ffb = pl.BlockSpec((FF_BLOCKS, tm, fb), lambda i, j: (j, i, 0))
    ffs = jax.ShapeDtypeStruct((N_DEV, t, fb), BF16)
    return pl.pallas_call(
        body, name="ffn_fwd", grid=grid,
        in_specs=[tok, pl.BlockSpec((1, D_MODEL), lambda i, j: (0, 0)),
                  pl.BlockSpec((FF_BLOCKS, D_MODEL, fb), lambda i, j: (j, 0, 0)),
                  pl.BlockSpec((FF_BLOCKS, D_MODEL, fb), lambda i, j: (j, 0, 0)),
                  pl.BlockSpec((FF_BLOCKS, fb, D_MODEL), lambda i, j: (j, 0, 0))] + rd.in_specs,
        out_specs=[tok, ffb, ffb, ffb, tok] + rd.out_specs,
        out_shape=[jax.ShapeDtypeStruct((t, D_MODEL), BF16), ffs, ffs, ffs,
                   jax.ShapeDtypeStruct((t, D_MODEL), F32)] + rd.out_shape,
        scratch_shapes=[pltpu.VMEM((tm, D_MODEL), F32)] + rd.scratch,
        compiler_params=_cparams(("arbitrary", "arbitrary")),
    )(x1, fw, wg, wu, wd, *rd.srcs)


def _ffn_bwd(dx2, g, u, wg, wu, wd, x1, fw, tm, ride=None):
    t = dx2.shape[0]
    fb = wg.shape[2]
    nf = N_DEV // FF_BLOCKS
    rd = _ride_plan(ride)
    grid = (t // tm, nf)

    def body(*refs):
        dx_ref, g_ref, u_ref, wg_ref, wu_ref, wd_ref, x_ref, fw_ref = refs[:8]
        dg_ref, du_ref, dx1_ref, dfw_ref = refs[8 + rd.n:12 + rd.n]
        acc_ref = refs[12 + 2 * rd.n]
        copies = rd.copies(refs[8:8 + rd.n], refs[12 + rd.n:12 + 2 * rd.n], refs[13 + 2 * rd.n:])
        step_id = _grid_step_id(grid)
        _ride_start(copies, step_id == 0)
        i = pl.program_id(0)
        j = pl.program_id(1)
        dxb = dx_ref[...].astype(BF16)
        dh = None
        for c in range(FF_BLOCKS):
            da = _dot_nt(dxb, wd_ref[c])
            gv = g_ref[c].astype(F32)
            uv = u_ref[c].astype(F32)
            sg = _sigmoid(gv)
            dg = (da * uv * (sg * (1.0 + gv * (1.0 - sg)))).astype(BF16)
            du = (da * gv * sg).astype(BF16)
            dg_ref[c] = dg
            du_ref[c] = du
            part = _dot_nt(dg, wg_ref[c]) + _dot_nt(du, wu_ref[c])
            dh = part if dh is None else dh + part

        @pl.when(j == 0)
        def _():
            acc_ref[...] = dh

        @pl.when(j > 0)
        def _():
            acc_ref[...] += dh

        @pl.when((i == 0) & (j == 0))
        def _():
            dfw_ref[...] = jnp.zeros_like(dfw_ref)

        @pl.when(j == nf - 1)
        def _():
            dx, dfw = _rms_bwd(acc_ref[...], x_ref[...], fw_ref[...])
            dx1_ref[...] = dx_ref[...] + dx
            dfw_ref[...] += jnp.sum(dfw, axis=0, keepdims=True)

        _ride_wait(copies, step_id == (t // tm) * nf - 1)

    tok = pl.BlockSpec((tm, D_MODEL), lambda i, j: (i, 0))
    ffb = pl.BlockSpec((FF_BLOCKS, tm, fb), lambda i, j: (j, i, 0))
    ffs = jax.ShapeDtypeStruct((N_DEV, t, fb), BF16)
    vec = pl.BlockSpec((1, D_MODEL), lambda i, j: (0, 0))
    return pl.pallas_call(
        body, name="ffn_bwd", grid=grid,
        in_specs=[tok, ffb, ffb,
                  pl.BlockSpec((FF_BLOCKS, D_MODEL, fb), lambda i, j: (j, 0, 0)),
                  pl.BlockSpec((FF_BLOCKS, D_MODEL, fb), lambda i, j: (j, 0, 0)),
                  pl.BlockSpec((FF_BLOCKS, fb, D_MODEL), lambda i, j: (j, 0, 0)),
                  tok, vec] + rd.in_specs,
        out_specs=[ffb, ffb, tok, vec] + rd.out_specs,
        out_shape=[ffs, ffs, jax.ShapeDtypeStruct((t, D_MODEL), F32),
                   jax.ShapeDtypeStruct((1, D_MODEL), F32)] + rd.out_shape,
        scratch_shapes=[pltpu.VMEM((tm, D_MODEL), F32)] + rd.scratch,
        compiler_params=_cparams(("arbitrary", "arbitrary")),
    )(dx2, g, u, wg, wu, wd, x1, fw, *rd.srcs)


def _loss_grad(y, target, tm):
    t, d = y.shape

    def body(y_ref, t_ref, dy_ref, loss_ref):
        err = y_ref[...] - t_ref[...]
        dy_ref[...] = err * (1.0 / d)

        @pl.when(pl.program_id(0) == 0)
        def _():
            loss_ref[...] = jnp.zeros_like(loss_ref)

        part = jnp.sum(jnp.sum(err * err, axis=-1, keepdims=True), axis=0, keepdims=True)
        loss_ref[...] += part * (0.5 / d)

    tok = pl.BlockSpec((tm, d), lambda i: (i, 0))
    return pl.pallas_call(
        body, name="loss_grad", grid=(t // tm,),
        in_specs=[tok, tok],
        out_specs=[tok, pl.BlockSpec((1, 1), lambda i: (0, 0))],
        out_shape=[jax.ShapeDtypeStruct((t, d), F32), jax.ShapeDtypeStruct((1, 1), F32)],
        compiler_params=_cparams(("arbitrary",)),
    )(y, target)


def _tile(v, reps):
    return jnp.tile(v.reshape(1, -1), (1, reps))


class _LocalPlan:
    def __init__(self, wb):
        self.w = [{n: wb[n][l] for n in BIG_AXIS} for l in range(DEPTH)]

    def ride(self, kernel_name, l, grads=None):
        return None

    def done(self, kernel_name, l, outs):
        pass


def _local_step(x, target, p, plan):
    nb, s, d = x.shape
    t = nb * s
    tm = min(512, s)
    tq = min(512, s)
    xf = x.reshape(t, d)
    cosq, sinq = _rope_tables(s)
    ones512 = _block_ones(512, HEAD_DIM)
    lbs = _lower_bounds(p["hgrn_lb_logits"].reshape(DEPTH, 2 * D_HGRN)).reshape(DEPTH, 2, D_HGRN)

    saved = []
    cur = xf
    wb = plan.w
    for l in range(DEPTH):
        qw = _tile(p["q_norm_w"][l], N_HEADS)
        kw = _tile(p["k_norm_w"][l], N_KV)
        gw = _tile(p["hgrn_gnorm_w"][l], 2)
        dww = jnp.pad(p["conv_dw_w"][l], ((0, 1), (0, 0)))
        pww = p["conv_pw_w"][l].astype(BF16)
        h0, proj = _rms_proj(cur, _row(p["mix_norm_w"][l]), wb[l]["w_in"], tm)
        qr, kd, vd, kdt, vdt = _qkv_prep(proj, cosq, sinq, qw, kw, ones512, s, tm)
        o_attn, lse, *rode = _attn_fwd(qr, kd, vdt, nb, s, tq, plan.ride("attn_fwd", l))
        plan.done("attn_fwd", l, rode)
        y_hgrn, osum, states, *rode = _hgrn_fwd(proj, lbs[l], gw, nb, s, plan.ride("hgrn_fwd", l))
        plan.done("hgrn_fwd", l, rode)
        y_conv = _conv_fwd(proj, dww, _row(p["conv_dw_b"][l]), _row(p["conv_ln_w"][l]),
                           _row(p["conv_ln_b"][l]), pww, _row(p["conv_pw_b"][l]), nb, s)
        mixed, x1 = _mix_out(o_attn, y_hgrn, y_conv, cur, _row(p["attn_out_norm_w"][l]),
                             _row(p["conv_out_norm_w"][l]), wb[l]["w_out"], tm)
        hf, g, u, a, x2, *rode = _ffn_fwd(x1, _row(p["ffn_norm_w"][l]), wb[l]["w_gate"], wb[l]["w_up"],
                                          wb[l]["w_down"], tm, plan.ride("ffn_fwd", l))
        plan.done("ffn_fwd", l, rode)
        saved.append(dict(x=cur, h0=h0, proj=proj, qr=qr, kd=kd, vd=vd, kdt=kdt, o_attn=o_attn, lse=lse,
                          osum=osum, states=states, y_conv=y_conv, mixed=mixed, x1=x1, hf=hf, g=g, u=u, a=a,
                          qw=qw, kw=kw, gw=gw, dww=dww, pww=pww))
        cur = x2

    dcur, loss = _loss_grad(cur, target.reshape(t, d), tm)

    grads = {k: [None] * DEPTH for k in WEIGHTS}
    dlb = [None] * DEPTH
    for l in reversed(range(DEPTH)):
        sv = saved[l]
        dg, du, dx1, dfw, *rode = _ffn_bwd(dcur, sv["g"], sv["u"], wb[l]["w_gate"], wb[l]["w_up"],
                                           wb[l]["w_down"], sv["x1"], _row(p["ffn_norm_w"][l]), tm,
                                           plan.ride("ffn_bwd", l, grads))
        plan.done("ffn_bwd", l, rode)
        grads["ffn_norm_w"][l] = dfw[0]
        grads["w_gate"][l] = _dw_ff(sv["hf"], dg, False, "dw_gate", tm)
        grads["w_up"][l] = _dw_ff(sv["hf"], du, False, "dw_up", tm)
        grads["w_down"][l] = _dw_ff(sv["a"], dcur, True, "dw_down", tm)
        do_attn, dy_hgrn, dy_conv, daw, dcw = _mix_out_bwd(
            dx1, wb[l]["w_out"], sv["o_attn"], sv["y_conv"], _row(p["attn_out_norm_w"][l]),
            _row(p["conv_out_norm_w"][l]), tm)
        grads["attn_out_norm_w"][l] = daw[0]
        grads["conv_out_norm_w"][l] = dcw[0]
        grads["w_out"][l] = _mm_tn(sv["mixed"], dx1, D_MODEL, "dw_out", tm)
        dq, dkd, dvd, *rode = _attn_bwd(sv["qr"], sv["kd"], sv["vd"], sv["kdt"], sv["o_attn"], sv["lse"], do_attn,
                                        nb, s, tq, plan.ride("attn_bwd", l, grads))
        plan.done("attn_bwd", l, rode)
        dqkv, dqw, dkw = _qkv_bwd(sv["proj"], dq, dkd, dvd, cosq, sinq, sv["qw"], sv["kw"], ones512, s, tm)
        grads["q_norm_w"][l] = dqw.reshape(N_HEADS, HEAD_DIM).sum(0)
        grads["k_norm_w"][l] = dkw.reshape(N_KV, HEAD_DIM).sum(0)
        dhq, dzf, dzb, dhi, dhg, dgw, dlb_l, *rode = _hgrn_bwd(sv["proj"], lbs[l], sv["gw"], sv["osum"],
                                                               sv["states"], dy_hgrn, nb, s,
                                                               plan.ride("hgrn_bwd", l, grads))
        plan.done("hgrn_bwd", l, rode)
        grads["hgrn_gnorm_w"][l] = dgw.reshape(nb * D_HGRN // HEAD_DIM, HEAD_DIM).sum(0)
        dlb[l] = dlb_l.sum(0)
        dab, ddww, ddwb, dlnw, dlnb, dpww, dpwb = _conv_bwd(
            sv["proj"], dy_conv, sv["dww"], _row(p["conv_dw_b"][l]), _row(p["conv_ln_w"][l]),
            _row(p["conv_ln_b"][l]), sv["pww"], nb, s)
        grads["conv_dw_w"][l] = ddww[:CONV_W]
        grads["conv_dw_b"][l] = ddwb[0]
        grads["conv_ln_w"][l] = dlnw[0]
        grads["conv_ln_b"][l] = dlnb[0]
        grads["conv_pw_w"][l] = dpww
        grads["conv_pw_b"][l] = dpwb[0]
        pieces = [dqkv, dhq, dzf, dzb, dhi, dhg, dab]
        grads["w_in"][l] = _dw_in(sv["h0"], pieces, tm)
        dcur, dnw = _proj_bwd(pieces, wb[l]["w_in"], sv["x"], _row(p["mix_norm_w"][l]), dx1, tm)
        grads["mix_norm_w"][l] = dnw[0]

    dlog = _lower_bounds_bwd(p["hgrn_lb_logits"].reshape(DEPTH, 2 * D_HGRN),
                             jnp.stack(dlb).reshape(DEPTH, 2 * D_HGRN))
    out = {k: (v if k in BIG_AXIS else jnp.stack(v)) for k, v in grads.items() if k != "hgrn_lb_logits"}
    out["hgrn_lb_logits"] = dlog.reshape(DEPTH, 2, D_HGRN)
    return loss, dcur.reshape(nb, s, d), out


BIG_AXIS = {"w_in": 2, "w_out": 1, "w_gate": 2, "w_up": 2, "w_down": 1}
SMALL_SHARD_AXIS = {"hgrn_lb_logits": 2, "conv_dw_w": 2, "conv_pw_w": 1}
WEIGHTS = ("mix_norm_w", "w_in", "q_norm_w", "k_norm_w", "hgrn_lb_logits", "hgrn_gnorm_w", "conv_dw_w",
           "conv_dw_b", "conv_ln_w", "conv_ln_b", "conv_pw_w", "conv_pw_b", "attn_out_norm_w",
           "conv_out_norm_w", "w_out", "ffn_norm_w", "w_gate", "w_up", "w_down")
SMALL = tuple(n for n in WEIGHTS if n not in BIG_AXIS)


def _my_index():
    return 4 * lax.axis_index("x") + 2 * lax.axis_index("y") + lax.axis_index("c")


class _RidePlan:
    def __init__(self, srcs, gather):
        self.srcs = list(srcs)
        self.n = len(self.srcs)
        self.gather = list(gather) if isinstance(gather, (list, tuple)) else [gather] * self.n
        any_spec = pl.BlockSpec(memory_space=pl.ANY)
        self.in_specs = [any_spec] * self.n
        self.out_specs = [any_spec] * self.n
        self.out_shape = [jax.ShapeDtypeStruct(((N_DEV,) + s.shape) if g else s.shape, s.dtype)
                          for s, g in zip(self.srcs, self.gather)]
        npeer = N_DEV - 1
        self.scratch = [pltpu.SemaphoreType.DMA((self.n * npeer,)), pltpu.SemaphoreType.DMA((self.n * npeer,)),
                        pltpu.SemaphoreType.DMA((self.n,))] if self.n else []

    def copies(self, src_refs, out_refs, sems):
        if not self.n:
            return [], [], []
        send_sems, recv_sems, local_sems = sems
        npeer = N_DEV - 1
        x, y, c = lax.axis_index("x"), lax.axis_index("y"), lax.axis_index("c")
        me = 4 * x + 2 * y + c
        locals_, sends, recvs = [], [], []
        for a in range(self.n):
            src_ref, out_ref = src_refs[a], out_refs[a]

            def rows_for(j, src_ref=src_ref, gather=self.gather[a]):
                return src_ref if gather else src_ref.at[j]

            locals_.append(pltpu.make_async_copy(rows_for(me), out_ref.at[me], local_sems.at[a]))
            for k in range(1, N_DEV):
                px = (1 - x) if (k & 4) else x
                py = (1 - y) if (k & 2) else y
                pc = (1 - c) if (k & 1) else c
                pidx = 4 * px + 2 * py + pc
                common = dict(send_sem=send_sems.at[a * npeer + k - 1], recv_sem=recv_sems.at[a * npeer + k - 1],
                              device_id=(px, py, pc), device_id_type=pl.DeviceIdType.MESH)
                sends.append(pltpu.make_async_remote_copy(src_ref=rows_for(pidx), dst_ref=out_ref.at[me], **common))
                recvs.append(pltpu.make_async_remote_copy(src_ref=rows_for(pidx), dst_ref=out_ref.at[pidx],
                                                          **common))
        return locals_, sends, recvs


def _ride_plan(ride):
    return _RidePlan(*ride) if ride else _RidePlan([], True)


def _ride_start(copies, when=None):
    locals_, sends, _ = copies

    def go():
        for cp in locals_ + sends:
            cp.start()

    if locals_:
        go() if when is None else pl.when(when)(go)


def _ride_wait(copies, when=None):
    locals_, sends, recvs = copies

    def go():
        for cp in recvs:
            cp.wait_recv()
        for cp in sends:
            cp.wait_send()
        for cp in locals_:
            cp.wait()

    if locals_:
        go() if when is None else pl.when(when)(go)


def _exchange(srcs, gather, name):
    rd = _RidePlan(srcs, gather)

    def body(*refs):
        copies = rd.copies(refs[:rd.n], refs[rd.n:2 * rd.n], refs[2 * rd.n:])
        _ride_start(copies)
        _ride_wait(copies)

    return pl.pallas_call(body, name=name, in_specs=rd.in_specs, out_specs=rd.out_specs,
                          out_shape=rd.out_shape, scratch_shapes=rd.scratch)(*srcs)


def _lane_group(n):
    g = 1
    while (g * n) % LANES:
        g += 1
    return g


def _cols_to_natural(gathered, name):
    _, k, n = gathered.shape
    grp = _lane_group(n)
    place = jnp.stack([jnp.asarray(np.eye(n, grp * n, k=i * n), BF16) for i in range(grp)])

    def body(g_ref, p_ref, o_ref):
        acc = None
        for i in range(grp):
            part = _dot(g_ref[i], p_ref[i])
            acc = part if acc is None else acc + part
        o_ref[...] = acc.astype(BF16)

    return pl.pallas_call(
        body, name=name, grid=(N_DEV // grp,),
        in_specs=[pl.BlockSpec((grp, k, n), lambda j: (j, 0, 0)),
                  pl.BlockSpec((grp, n, grp * n), lambda j: (0, 0, 0))],
        out_specs=pl.BlockSpec((k, grp * n), lambda j: (0, j)),
        out_shape=jax.ShapeDtypeStruct((k, N_DEV * n), BF16),
        compiler_params=_cparams(("parallel",)),
    )(gathered, place)


def _natural_to_cols(dw, name):
    k, n8 = dw.shape
    n = n8 // N_DEV
    grp = _lane_group(n)
    pick = jnp.stack([jnp.asarray(np.eye(grp * n, n, k=-i * n), BF16) for i in range(grp)])

    def body(d_ref, p_ref, o_ref):
        xb = d_ref[...].astype(BF16)
        for i in range(grp):
            o_ref[i] = _dot(xb, p_ref[i]).astype(BF16)

    return pl.pallas_call(
        body, name=name, grid=(N_DEV // grp,),
        in_specs=[pl.BlockSpec((k, grp * n), lambda j: (0, j)),
                  pl.BlockSpec((grp, grp * n, n), lambda j: (0, 0, 0))],
        out_specs=pl.BlockSpec((grp, k, n), lambda j: (j, 0, 0)),
        out_shape=jax.ShapeDtypeStruct((N_DEV, k, n), BF16),
        compiler_params=_cparams(("parallel",)),
    )(dw, pick)


def _adamw_math(w, g, m, v):
    m = ADAM_B1 * m + (1.0 - ADAM_B1) * g
    v = ADAM_B2 * v + (1.0 - ADAM_B2) * (g * g)
    m_hat = m / (1.0 - ADAM_B1 ** ADAM_STEP)
    v_hat = v / (1.0 - ADAM_B2 ** ADAM_STEP)
    delta = -ADAM_LR * (m_hat / (jnp.sqrt(v_hat) + ADAM_EPS) + ADAM_WD * w)
    return delta, m, v


def _sum_adamw(parts, w, m, v, name):
    _, k, n = w.shape
    tk = k
    for cand in (256, 176, 128):
        if k % cand == 0:
            tk = cand
            break

    def body(*refs):
        p_refs = refs[:DEPTH]
        w_ref, m_ref, v_ref, g_ref, d_ref, mo_ref, vo_ref = refs[DEPTH:]
        for l in range(DEPTH):
            @pl.when(pl.program_id(0) == l)
            def _(p_ref=p_refs[l]):
                g = p_ref[0].astype(F32)
                for i in range(1, N_DEV):
                    g = g + p_ref[i].astype(F32)
                g_ref[...] = g
                d_ref[...], mo_ref[...], vo_ref[...] = _adamw_math(w_ref[...], g, m_ref[...], v_ref[...])

    row = pl.BlockSpec((None, tk, n), lambda l, i: (l, i, 0))
    shp = jax.ShapeDtypeStruct(w.shape, F32)
    return pl.pallas_call(
        body, name=name, grid=(DEPTH, k // tk),
        in_specs=[pl.BlockSpec((N_DEV, tk, n), lambda l, i: (0, i, 0))] * DEPTH + [row, row, row],
        out_specs=[row, row, row, row],
        out_shape=[shp, shp, shp, shp],
        compiler_params=_cparams(("parallel", "parallel")),
    )(*parts, w, m, v)


def _sum8(parts, name):
    r = parts.shape[1]

    def body(p_ref, g_ref):
        g = p_ref[0]
        for i in range(1, N_DEV):
            g = g + p_ref[i]
        g_ref[...] = g

    return pl.pallas_call(body, name=name, out_shape=jax.ShapeDtypeStruct((r, LANES), F32))(parts)


def _adamw(w, g, m, v):
    def body(w_ref, g_ref, m_ref, v_ref, d_ref, mo_ref, vo_ref):
        d_ref[...], mo_ref[...], vo_ref[...] = _adamw_math(w_ref[...], g_ref[...], m_ref[...], v_ref[...])

    shp = jax.ShapeDtypeStruct(w.shape, F32)
    return pl.pallas_call(body, name="adamw_small", out_shape=[shp, shp, shp])(w, g, m, v)


def _pack(arrays, dtype, row_multiple):
    flat = jnp.concatenate([a.reshape(-1).astype(dtype) for a in arrays])
    n = flat.shape[0]
    unit = row_multiple * LANES
    total = -(-n // unit) * unit
    return jnp.pad(flat, (0, total - n)).reshape(total // LANES, LANES)


def _unpack(flat2d, shapes, lead=()):
    flat = flat2d.reshape(lead + (-1,))
    out, off = [], 0
    for shp in shapes:
        n = int(np.prod(shp))
        out.append(flat[..., off:off + n].reshape(lead + tuple(shp)))
        off += n
    return out


def _shard_to_rows(full, axis):
    shp = full.shape
    k = shp[axis] // N_DEV
    r = full.reshape(shp[:axis] + (N_DEV, k) + shp[axis + 1:])
    return jnp.moveaxis(r, axis, 0)


def _rows_to_full(rows, axis):
    r = jnp.moveaxis(rows, 0, axis)
    shp = r.shape
    return r.reshape(shp[:axis] + (shp[axis] * shp[axis + 1],) + shp[axis + 2:])


def kernel(x, mix_norm_w, w_in, q_norm_w, k_norm_w, hgrn_lb_logits, hgrn_gnorm_w, conv_dw_w, conv_dw_b, conv_ln_w, conv_ln_b, conv_pw_w, conv_pw_b, attn_out_norm_w, conv_out_norm_w, w_out, ffn_norm_w, w_gate, w_up, w_down, loss_target, m_mix_norm_w, m_w_in, m_q_norm_w, m_k_norm_w, m_hgrn_lb_logits, m_hgrn_gnorm_w, m_conv_dw_w, m_conv_dw_b, m_conv_ln_w, m_conv_ln_b, m_conv_pw_w, m_conv_pw_b, m_attn_out_norm_w, m_conv_out_norm_w, m_w_out, m_ffn_norm_w, m_w_gate, m_w_up, m_w_down, v_mix_norm_w, v_w_in, v_q_norm_w, v_k_norm_w, v_hgrn_lb_logits, v_hgrn_gnorm_w, v_conv_dw_w, v_conv_dw_b, v_conv_ln_w, v_conv_ln_b, v_conv_pw_w, v_conv_pw_b, v_attn_out_norm_w, v_conv_out_norm_w, v_w_out, v_ffn_norm_w, v_w_gate, v_w_up, v_w_down):
    w_loc = dict(zip(WEIGHTS, (mix_norm_w, w_in, q_norm_w, k_norm_w, hgrn_lb_logits, hgrn_gnorm_w, conv_dw_w,
                               conv_dw_b, conv_ln_w, conv_ln_b, conv_pw_w, conv_pw_b, attn_out_norm_w,
                               conv_out_norm_w, w_out, ffn_norm_w, w_gate, w_up, w_down)))
    m_loc = dict(zip(WEIGHTS, (m_mix_norm_w, m_w_in, m_q_norm_w, m_k_norm_w, m_hgrn_lb_logits, m_hgrn_gnorm_w,
                               m_conv_dw_w, m_conv_dw_b, m_conv_ln_w, m_conv_ln_b, m_conv_pw_w, m_conv_pw_b,
                               m_attn_out_norm_w, m_conv_out_norm_w, m_w_out, m_ffn_norm_w, m_w_gate, m_w_up,
                               m_w_down)))
    v_loc = dict(zip(WEIGHTS, (v_mix_norm_w, v_w_in, v_q_norm_w, v_k_norm_w, v_hgrn_lb_logits, v_hgrn_gnorm_w,
                               v_conv_dw_w, v_conv_dw_b, v_conv_ln_w, v_conv_ln_b, v_conv_pw_w, v_conv_pw_b,
                               v_attn_out_norm_w, v_conv_out_norm_w, v_w_out, v_ffn_norm_w, v_w_gate, v_w_up,
                               v_w_down)))
    me = _my_index()
    big = tuple(BIG_AXIS)
    sms = tuple(SMALL_SHARD_AXIS)

    sm_shapes = [w_loc[n].shape for n in sms]
    got_s = _exchange([_pack([w_loc[n] for n in sms], F32, 8)], True, "gather_small_params")[0]
    p_full = {n: w_loc[n] for n in SMALL if n not in SMALL_SHARD_AXIS}
    for n, a in zip(sms, _unpack(got_s, sm_shapes, (N_DEV,))):
        p_full[n] = _rows_to_full(a, SMALL_SHARD_AXIS[n])

    def natural(n, gathered):
        if n == "w_in":
            return _cols_to_natural(gathered, "relayout_" + n)
        if n == "w_out":
            return gathered.reshape(-1, gathered.shape[-1])
        return gathered

    def to_send(n, gl):
        if n == "w_in":
            return _natural_to_cols(gl, "split_d" + n)
        if n == "w_out":
            return gl.reshape(N_DEV, gl.shape[0] // N_DEV, gl.shape[1]).astype(BF16)
        return gl

    class StepPlan:
        def __init__(self):
            self.w = [dict() for _ in range(DEPTH)]
            self.parts = [dict() for _ in range(DEPTH)]
            self.pending = {}
            got = _exchange([w_loc["w_in"][0].astype(BF16)], True, "gather_w_in")
            self.w[0]["w_in"] = natural("w_in", got[0])

        def ride(self, kernel_name, l, grads=None):
            want = []
            if kernel_name == "attn_fwd":
                want = [("w_out", l), ("w_gate", l)]
            elif kernel_name == "hgrn_fwd":
                want = [("w_up", l), ("w_down", l)]
            elif kernel_name == "ffn_fwd" and l + 1 < DEPTH:
                want = [("w_in", l + 1)]
            elif kernel_name == "ffn_bwd" and l + 1 < DEPTH:
                want = [("w_gate", l + 1), ("w_up", l + 1)]
            elif kernel_name == "attn_bwd" and l + 1 < DEPTH:
                want = [("w_in", l + 1), ("w_out", l + 1), ("w_down", l + 1)]
                if l == 0:
                    want += [("w_out", 0)]
            elif kernel_name == "hgrn_bwd" and l == 0:
                want = [(n, 0) for n in ("w_gate", "w_up", "w_down")]
            if not want:
                return None
            self.pending[(kernel_name, l)] = want
            if grads is None:
                return [w_loc[n][wl].astype(BF16) for n, wl in want], True
            return [to_send(n, grads[n][wl]) for n, wl in want], False

        def done(self, kernel_name, l, outs):
            want = self.pending.pop((kernel_name, l), [])
            for (n, wl), out in zip(want, outs):
                if kernel_name.endswith("_fwd"):
                    self.w[wl][n] = natural(n, out)
                else:
                    self.parts[wl][n] = out

    plan = StepPlan()
    loss_part, grad_x, g = _local_step(x, loss_target, p_full, plan)
    loss = lax.psum(loss_part[0, 0], MESH_AXES)

    pw = g["conv_pw_w"]
    k_pw = w_loc["conv_pw_w"].shape[1]
    pw_send = jnp.moveaxis(pw.reshape(DEPTH, N_DEV, k_pw, pw.shape[-1]), 1, 0).reshape(N_DEV, -1, LANES)
    gathered_small = [n for n in SMALL if n != "conv_pw_w"]
    small_shapes = [g[n].shape for n in gathered_small]
    din_parts, small_parts, pw_parts = _exchange(
        [to_send("w_in", g["w_in"][0]), _pack([g[n] for n in gathered_small], F32, 8), pw_send],
        [False, True, False], "exchange_last_grads")
    plan.parts[0]["w_in"] = din_parts
    big_out = {n: _sum_adamw([plan.parts[l][n] for l in range(DEPTH)], w_loc[n], m_loc[n], v_loc[n],
                             "sum_adamw_" + n) for n in big}

    g_small = dict(zip(gathered_small, _unpack(_sum8(small_parts, "sum_small_grads"), small_shapes)))
    g_small["conv_pw_w"] = _sum8(pw_parts, "sum_conv_pw_grads").reshape(w_loc["conv_pw_w"].shape)
    for n in sms:
        if n == "conv_pw_w":
            continue
        ax = SMALL_SHARD_AXIS[n]
        k = w_loc[n].shape[ax]
        g_small[n] = lax.dynamic_slice_in_dim(g_small[n], me * k, k, axis=ax)
    loc_shapes = [w_loc[n].shape for n in SMALL]
    packed = [_pack([d[n] for n in SMALL], F32, 8) for d in (w_loc, g_small, m_loc, v_loc)]
    res = _adamw(*packed)
    small_out = [g_small] + [dict(zip(SMALL, _unpack(r, loc_shapes))) for r in res]

    def pick(i, n):
        return big_out[n][i] if n in BIG_AXIS else small_out[i][n]

    return (loss, grad_x) + tuple(pick(i, n) for i in range(4) for n in WEIGHTS)
```

```python
import functools

import jax
import jax.numpy as jnp
import numpy as np
from jax import lax
from jax.experimental import pallas as pl
from jax.experimental.pallas import tpu as pltpu

F32 = jnp.float32
BF16 = jnp.bfloat16

D_MODEL = 1024
D_ATTN = 512
D_HGRN = 256
D_CONV = 256
HEAD_DIM = 64
N_HEADS = 8
N_KV = 2
GRID_W = 64
ROPE_THETA = 10000.0
CHUNK = 64
F_MIN = 1e-6
CONV_W = 31
CONV_PAD = 15
D_FF = 2816
D_PROJ = 2560
EPS = 1e-6
LN_EPS = 1e-5
DEPTH = 2
ADAM_LR = 0.001
ADAM_B1 = 0.9
ADAM_B2 = 0.999
ADAM_EPS = 1e-08
ADAM_WD = 0.01
ADAM_STEP = 10
N_DEV = 8
MESH_AXES = ("x", "y", "c")

COL_HQ, COL_ZFW, COL_ZBW, COL_HI, COL_HG = 6, 8, 10, 12, 14
COL_CA, COL_CB = 8, 9

LANES = 128
VMEM_LIMIT_MB = 56


def _cparams(dims=None):
    return pltpu.CompilerParams(dimension_semantics=dims, vmem_limit_bytes=VMEM_LIMIT_MB * 2 ** 20)


def _dot(a, b):
    return jnp.dot(a, b, preferred_element_type=F32)


def _dot_nt(a, b):
    return lax.dot_general(a, b, (((1,), (1,)), ((), ())), preferred_element_type=F32)


def _dot_tn(a, b):
    return lax.dot_general(a, b, (((0,), (0,)), ((), ())), preferred_element_type=F32)


def _split_bf16(x, parts):
    out = []
    r = x
    for _ in range(parts):
        p = r.astype(BF16)
        out.append(p)
        r = r - p.astype(F32)
    return out


def _dot_precise(x, m_bf16, parts=3):
    acc = None
    for p in _split_bf16(x, parts):
        t = _dot(p, m_bf16)
        acc = t if acc is None else acc + t
    return acc


def _block_ones(width, group):
    i = np.arange(width)
    return jnp.asarray((i[:, None] // group) == (i[None, :] // group), dtype=BF16)


def _sigmoid(x):
    return 1.0 / (1.0 + jnp.exp(-x))


def _rot(x):
    w = x.shape[1]
    lane = lax.broadcasted_iota(jnp.int32, x.shape, 1)
    first = (lane % 32) < 16
    return jnp.where(first, -pltpu.roll(x, w - 16, 1), pltpu.roll(x, 16, 1))


def _rope(x, cos, sin):
    return x * cos + _rot(x) * sin


def _rope_t(dy, cos, sin):
    return dy * cos - _rot(dy * sin)


def _row(v):
    return v.reshape(1, -1)


def _rms_proj(x, wn, w, tm):
    t, d = x.shape
    n = w.shape[1]

    def body(x_ref, wn_ref, w_ref, h_ref, y_ref):
        xv = x_ref[...]
        r = lax.rsqrt(jnp.mean(xv * xv, axis=-1, keepdims=True) + EPS)
        h = (xv * r * wn_ref[...]).astype(BF16)
        h_ref[...] = h
        y_ref[...] = _dot(h, w_ref[...])

    return pl.pallas_call(
        body, name="rms_proj", grid=(t // tm,),
        in_specs=[pl.BlockSpec((tm, d), lambda i: (i, 0)),
                  pl.BlockSpec((1, d), lambda i: (0, 0)),
                  pl.BlockSpec((d, n), lambda i: (0, 0))],
        out_specs=[pl.BlockSpec((tm, d), lambda i: (i, 0)),
                   pl.BlockSpec((tm, n), lambda i: (i, 0))],
        out_shape=[jax.ShapeDtypeStruct((t, d), BF16), jax.ShapeDtypeStruct((t, n), F32)],
        compiler_params=_cparams(("parallel",)),
    )(x, wn, w)


def _rms_bwd(dh, x, wn):
    r = lax.rsqrt(jnp.mean(x * x, axis=-1, keepdims=True) + EPS)
    g = dh * wn
    dx = r * (g - x * (r * r) * jnp.mean(g * x, axis=-1, keepdims=True))
    return dx, dh * x * r


def _proj_bwd(pieces, w, x, wn, dres, tm):
    t = x.shape[0]
    d = x.shape[1]
    n = w.shape[1]
    widths = [p.shape[1] for p in pieces]
    offs = [sum(widths[:i]) for i in range(len(widths))]
    assert sum(widths) == n
    npc = len(pieces)

    def body(*refs):
        p_refs = refs[:npc]
        w_ref, x_ref, wn_ref, dr_ref, dx_ref, dwn_ref = refs[npc:]
        dh = None
        for p_ref, o, wd in zip(p_refs, offs, widths):
            part = _dot_nt(p_ref[...], w_ref[:, o:o + wd])
            dh = part if dh is None else dh + part
        dx, dwn = _rms_bwd(dh, x_ref[...], wn_ref[...])
        dx_ref[...] = dr_ref[...] + dx

        @pl.when(pl.program_id(0) == 0)
        def _():
            dwn_ref[...] = jnp.zeros_like(dwn_ref)

        dwn_ref[...] += jnp.sum(dwn, axis=0, keepdims=True)

    return pl.pallas_call(
        body, name="proj_bwd", grid=(t // tm,),
        in_specs=[pl.BlockSpec((tm, wd), lambda i: (i, 0)) for wd in widths]
        + [pl.BlockSpec((d, n), lambda i: (0, 0)),
           pl.BlockSpec((tm, d), lambda i: (i, 0)),
           pl.BlockSpec((1, d), lambda i: (0, 0)),
           pl.BlockSpec((tm, d), lambda i: (i, 0))],
        out_specs=[pl.BlockSpec((tm, d), lambda i: (i, 0)),
                   pl.BlockSpec((1, d), lambda i: (0, 0))],
        out_shape=[jax.ShapeDtypeStruct((t, d), F32), jax.ShapeDtypeStruct((1, d), F32)],
        compiler_params=_cparams(("arbitrary",)),
    )(*pieces, w, x, wn, dres)


def _dw_in(h0, pieces, tm):
    t, k = h0.shape
    widths = [p.shape[1] for p in pieces]
    offs = [sum(widths[:i]) for i in range(len(widths))]
    n = sum(widths)
    npc = len(pieces)

    def body(*refs):
        h_ref = refs[0]
        p_refs = refs[1:1 + npc]
        o_ref = refs[1 + npc]

        @pl.when(pl.program_id(0) == 0)
        def _():
            o_ref[...] = jnp.zeros_like(o_ref)

        ht = h_ref[...].astype(F32).T.astype(BF16)
        for p_ref, o, wd in zip(p_refs, offs, widths):
            o_ref[:, o:o + wd] += _dot(ht, p_ref[...])

    return pl.pallas_call(
        body, name="dw_in", grid=(t // tm,),
        in_specs=[pl.BlockSpec((tm, k), lambda i: (i, 0))]
        + [pl.BlockSpec((tm, wd), lambda i: (i, 0)) for wd in widths],
        out_specs=pl.BlockSpec((k, n), lambda i: (0, 0)),
        out_shape=jax.ShapeDtypeStruct((k, n), F32),
        compiler_params=_cparams(("arbitrary",)),
    )(h0, *pieces)


def _mm_tn(a, b, tn, name, tm):
    t, k = a.shape
    n = b.shape[1]

    def body(a_ref, b_ref, o_ref):
        @pl.when(pl.program_id(1) == 0)
        def _():
            o_ref[...] = jnp.zeros_like(o_ref)

        o_ref[...] += _dot_tn(a_ref[...].astype(BF16), b_ref[...].astype(BF16))

    return pl.pallas_call(
        body, name=name, grid=(n // tn, t // tm),
        in_specs=[pl.BlockSpec((tm, k), lambda j, i: (i, 0)),
                  pl.BlockSpec((tm, tn), lambda j, i: (i, j))],
        out_specs=pl.BlockSpec((k, tn), lambda j, i: (0, j)),
        out_shape=jax.ShapeDtypeStruct((k, n), F32),
        compiler_params=_cparams(("parallel", "arbitrary")),
    )(a, b)


def _dw_ff(h, d, rows, name, tm):
    t = d.shape[0] if rows else h.shape[0]
    blocked, flat = (h, d) if rows else (d, h)
    fb = blocked.shape[2]
    dm = flat.shape[1]
    out_blk = (N_DEV, fb, dm) if rows else (N_DEV, dm, fb)

    def body(h_ref, d_ref, o_ref, acc_ref):
        i = pl.program_id(0)

        @pl.when(i == 0)
        def _():
            acc_ref[...] = jnp.zeros_like(acc_ref)

        if rows:
            db = d_ref[...].astype(BF16)
            for j in range(N_DEV):
                acc_ref[j] += _dot_tn(h_ref[j], db)
        else:
            ht = h_ref[...].astype(F32).T.astype(BF16)
            for j in range(N_DEV):
                acc_ref[j] += _dot(ht, d_ref[j])

        @pl.when(i == t // tm - 1)
        def _():
            o_ref[...] = acc_ref[...].astype(BF16)

    blk_spec = pl.BlockSpec((N_DEV, tm, fb), lambda i: (0, i, 0))
    flat_spec = pl.BlockSpec((tm, dm), lambda i: (i, 0))
    return pl.pallas_call(
        body, name=name, grid=(t // tm,),
        in_specs=[blk_spec, flat_spec] if rows else [flat_spec, blk_spec],
        out_specs=pl.BlockSpec(out_blk, lambda i: (0, 0, 0)),
        out_shape=jax.ShapeDtypeStruct(out_blk, BF16),
        scratch_shapes=[pltpu.VMEM(out_blk, F32)],
        compiler_params=_cparams(("arbitrary",)),
    )(h, d)


def _rope_tables(s):
    rows = s // GRID_W
    row_id = jnp.repeat(jnp.arange(rows, dtype=F32), GRID_W)
    col_id = jnp.tile(jnp.arange(GRID_W, dtype=F32), rows)
    half = HEAD_DIM // 2
    inv_freq = ROPE_THETA ** (-jnp.arange(0, half, 2, dtype=F32) / half)
    ang_r = row_id[:, None] * inv_freq[None, :]
    ang_c = col_id[:, None] * inv_freq[None, :]
    ang = jnp.concatenate([ang_r, ang_r, ang_c, ang_c], axis=-1)
    cos, sin = jnp.cos(ang), jnp.sin(ang)
    return jnp.tile(cos, (1, N_HEADS)), jnp.tile(sin, (1, N_HEADS))


def _head_rms(x, w, ones):
    r = lax.rsqrt(_dot_precise(x * x, ones, 2) * (1.0 / HEAD_DIM) + EPS)
    return x * r * w, r


def _dup_half(x, kv):
    lane = lax.broadcasted_iota(jnp.int32, x.shape, 1)
    sel = (lane < 64) if kv == 0 else (lane >= 64)
    return jnp.where(sel, x, pltpu.roll(x, 64, 1))


def _qkv_prep(proj, cosq, sinq, qw, kw, ones, s, tm):
    t = proj.shape[0]
    ns = s // tm

    def body(p_ref, cos_ref, sin_ref, qw_ref, kw_ref, ones_ref, q_out, kd_out, vd_out, kdt_out, vdt_out):
        cos = cos_ref[...]
        sin = sin_ref[...]
        ones_m = ones_ref[...]
        qn, _ = _head_rms(p_ref[:, 0:512], qw_ref[...], ones_m)
        q_out[...] = (_rope(qn, cos, sin) * (HEAD_DIM ** -0.5)).astype(BF16)
        kn, _ = _head_rms(p_ref[:, 512:640], kw_ref[...], ones_m[0:128, 0:128])
        kr = _rope(kn, cos[:, 0:128], sin[:, 0:128])
        v = p_ref[:, 640:768]
        for kv in range(N_KV):
            kd = _dup_half(kr, kv)
            vd = _dup_half(v, kv)
            kd_out[kv] = kd.astype(BF16)
            vd_out[kv] = vd.astype(BF16)
            kdt_out[kv] = kd.T.astype(BF16)
            vdt_out[kv] = vd.T.astype(BF16)

    return pl.pallas_call(
        body, name="qkv_prep", grid=(t // tm,),
        in_specs=[pl.BlockSpec((tm, 768), lambda i: (i, 0)),
                  pl.BlockSpec((tm, 512), lambda i: (i % ns, 0)),
                  pl.BlockSpec((tm, 512), lambda i: (i % ns, 0)),
                  pl.BlockSpec((1, 512), lambda i: (0, 0)),
                  pl.BlockSpec((1, 128), lambda i: (0, 0)),
                  pl.BlockSpec((512, 512), lambda i: (0, 0))],
        out_specs=[pl.BlockSpec((tm, 512), lambda i: (i, 0)),
                   pl.BlockSpec((N_KV, tm, 128), lambda i: (0, i, 0)),
                   pl.BlockSpec((N_KV, tm, 128), lambda i: (0, i, 0)),
                   pl.BlockSpec((N_KV, 128, tm), lambda i: (0, 0, i)),
                   pl.BlockSpec((N_KV, 128, tm), lambda i: (0, 0, i))],
        out_shape=[jax.ShapeDtypeStruct((t, 512), BF16),
                   jax.ShapeDtypeStruct((N_KV, t, 128), BF16),
                   jax.ShapeDtypeStruct((N_KV, t, 128), BF16),
                   jax.ShapeDtypeStruct((N_KV, 128, t), BF16),
                   jax.ShapeDtypeStruct((N_KV, 128, t), BF16)],
        compiler_params=_cparams(("parallel",)),
    )(proj, cosq, sinq, qw, kw, ones)


def _qkv_bwd(proj, dq, dkd, dvd, cosq, sinq, qw, kw, ones, s, tm):
    t = proj.shape[0]
    ns = s // tm

    def body(p_ref, dq_ref, dkd_ref, dvd_ref, cos_ref, sin_ref, qw_ref, kw_ref, ones_ref,
             out_ref, dqw_ref, dkw_ref):
        cos = cos_ref[...]
        sin = sin_ref[...]
        ones_m = ones_ref[...]
        ones_k = ones_m[0:128, 0:128]

        def norm_bwd(x, w, dn, om):
            r = lax.rsqrt(_dot_precise(x * x, om, 2) * (1.0 / HEAD_DIM) + EPS)
            g = dn * w
            dx = r * (g - x * (r * r) * (_dot_precise(g * x, om, 2) * (1.0 / HEAD_DIM)))
            return dx, jnp.sum(dn * x * r, axis=0, keepdims=True)

        q = p_ref[:, 0:512]
        dqn = _rope_t(dq_ref[...], cos, sin) * (HEAD_DIM ** -0.5)
        dq_raw, dqw = norm_bwd(q, qw_ref[...], dqn, ones_m)
        out_ref[:, 0:512] = dq_raw.astype(BF16)

        lane = lax.broadcasted_iota(jnp.int32, (tm, 128), 1)

        def fold(ref):
            a0 = ref[0]
            a1 = ref[1]
            f0 = a0 + pltpu.roll(a0, 64, 1)
            f1 = a1 + pltpu.roll(a1, 64, 1)
            return jnp.where(lane < 64, f0, f1)

        k = p_ref[:, 512:640]
        dkn = _rope_t(fold(dkd_ref), cos[:, 0:128], sin[:, 0:128])
        dk_raw, dkw = norm_bwd(k, kw_ref[...], dkn, ones_k)
        out_ref[:, 512:640] = dk_raw.astype(BF16)
        out_ref[:, 640:768] = fold(dvd_ref).astype(BF16)

        @pl.when(pl.program_id(0) == 0)
        def _():
            dqw_ref[...] = jnp.zeros_like(dqw_ref)
            dkw_ref[...] = jnp.zeros_like(dkw_ref)

        dqw_ref[...] += dqw
        dkw_ref[...] += dkw

    return pl.pallas_call(
        body, name="qkv_bwd", grid=(t // tm,),
        in_specs=[pl.BlockSpec((tm, 768), lambda i: (i, 0)),
                  pl.BlockSpec((tm, 512), lambda i: (i, 0)),
                  pl.BlockSpec((N_KV, tm, 128), lambda i: (0, i, 0)),
                  pl.BlockSpec((N_KV, tm, 128), lambda i: (0, i, 0)),
                  pl.BlockSpec((tm, 512), lambda i: (i % ns, 0)),
                  pl.BlockSpec((tm, 512), lambda i: (i % ns, 0)),
                  pl.BlockSpec((1, 512), lambda i: (0, 0)),
                  pl.BlockSpec((1, 128), lambda i: (0, 0)),
                  pl.BlockSpec((512, 512), lambda i: (0, 0))],
        out_specs=[pl.BlockSpec((tm, 768), lambda i: (i, 0)),
                   pl.BlockSpec((1, 512), lambda i: (0, 0)),
                   pl.BlockSpec((1, 128), lambda i: (0, 0))],
        out_shape=[jax.ShapeDtypeStruct((t, 768), BF16),
                   jax.ShapeDtypeStruct((1, 512), F32),
                   jax.ShapeDtypeStruct((1, 128), F32)],
        compiler_params=_cparams(("arbitrary",)),
    )(proj, dq, dkd, dvd, cosq, sinq, qw, kw, ones)


def _grid_step_id(grid):
    idx = pl.program_id(0)
    for ax in range(1, len(grid)):
        idx = idx * grid[ax] + pl.program_id(ax)
    return idx


def _attn_fwd(q, kd, vdt, nb, s, tq, ride=None):
    t = q.shape[0]
    nq = s // tq
    rd = _ride_plan(ride)
    grid = (nb, N_HEADS // 2, nq)
    nsteps = nb * (N_HEADS // 2) * nq

    def body(*refs):
        q_ref, k_ref, vt_ref = refs[:3]
        o_ref, lse_ref = refs[3 + rd.n:5 + rd.n]
        copies = rd.copies(refs[3:3 + rd.n], refs[5 + rd.n:5 + 2 * rd.n], refs[5 + 2 * rd.n:])
        step_id = _grid_step_id(grid)
        _ride_start(copies, step_id == 0)
        qv = q_ref[...].astype(F32)
        lane = lax.broadcasted_iota(jnp.int32, qv.shape, 1)
        k = k_ref[0]
        vt = vt_ref[0]
        outs = []
        scores = [_dot_nt(k, jnp.where((lane < 64) if half == 0 else (lane >= 64), qv, 0.0).astype(BF16))
                  for half in range(2)]
        for half in range(2):
            st = scores[half]
            m = jnp.max(st, axis=0, keepdims=True)
            p = jnp.exp(st - m)
            l = jnp.sum(p, axis=0, keepdims=True)
            ot = _dot(vt, p.astype(BF16)) / l
            lse_ref[0, half] = m + jnp.log(l)
            outs.append(ot)
        row = lax.broadcasted_iota(jnp.int32, outs[0].shape, 0)
        o_ref[...] = jnp.where(row < 64, outs[0], outs[1]).T
        _ride_wait(copies, step_id == nsteps - 1)

    return pl.pallas_call(
        body, name="attn_fwd", grid=grid,
        in_specs=[pl.BlockSpec((tq, 128), lambda b, p, i: (b * nq + i, p)),
                  pl.BlockSpec((1, s, 128), lambda b, p, i: (p // 2, b, 0)),
                  pl.BlockSpec((1, 128, s), lambda b, p, i: (p // 2, 0, b))] + rd.in_specs,
        out_specs=[pl.BlockSpec((tq, 128), lambda b, p, i: (b * nq + i, p)),
                   pl.BlockSpec((1, 2, 1, tq), lambda b, p, i: (b, p, 0, i))] + rd.out_specs,
        out_shape=[jax.ShapeDtypeStruct((t, D_ATTN), F32),
                   jax.ShapeDtypeStruct((nb, N_HEADS, 1, s), F32)] + rd.out_shape,
        scratch_shapes=rd.scratch,
        compiler_params=_cparams(("arbitrary", "arbitrary", "arbitrary")),
    )(q, kd, vdt, *rd.srcs)


def _attn_bwd(q, kd, vd, kdt, o, lse, do, nb, s, tq, ride=None):
    t = q.shape[0]
    nq = s // tq
    ones8 = jnp.ones((8, 128), BF16)
    rd = _ride_plan(ride)
    grid = (nb, N_KV, 2, nq)
    nsteps = nb * N_KV * 2 * nq

    def body(*refs):
        q_ref, k_ref, v_ref, kt_ref, o_ref, lse_ref, do_ref, ones_ref = refs[:8]
        dq_ref, dk_ref, dv_ref = refs[8 + rd.n:11 + rd.n]
        copies = rd.copies(refs[8:8 + rd.n], refs[11 + rd.n:11 + 2 * rd.n], refs[11 + 2 * rd.n:])
        step_id = _grid_step_id(grid)
        _ride_start(copies, step_id == 0)

        @pl.when((pl.program_id(2) == 0) & (pl.program_id(3) == 0))
        def _():
            dk_ref[...] = jnp.zeros_like(dk_ref)
            dv_ref[...] = jnp.zeros_like(dv_ref)

        qv = q_ref[...].astype(F32)
        dov = do_ref[...]
        ov = o_ref[...]
        lane = lax.broadcasted_iota(jnp.int32, qv.shape, 1)
        k = k_ref[0]
        v = v_ref[0]
        kt = kt_ref[0]
        dqs = []
        dk_acc = None
        dv_acc = None
        for half in range(2):
            sel = (lane < 64) if half == 0 else (lane >= 64)
            qh = jnp.where(sel, qv, 0.0).astype(BF16)
            doh = jnp.where(sel, dov, 0.0)
            dob = doh.astype(BF16)
            delta = None
            for part in _split_bf16(doh * ov, 3):
                d8 = _dot_nt(ones_ref[...], part)
                delta = d8 if delta is None else delta + d8
            delta = delta[0:1, :]
            st = _dot_nt(k, qh)
            pt = jnp.exp(st - lse_ref[0, half])
            dpt = _dot_nt(v, dob)
            dst = (pt * (dpt - delta)).astype(BF16)
            dkh = _dot(dst, qh)
            dvh = _dot(pt.astype(BF16), dob)
            dk_acc = dkh if dk_acc is None else dk_acc + dkh
            dv_acc = dvh if dv_acc is None else dv_acc + dvh
            dqs.append(_dot(kt, dst))
        dk_ref[0] += dk_acc
        dv_ref[0] += dv_acc
        row = lax.broadcasted_iota(jnp.int32, dqs[0].shape, 0)
        dq_ref[...] = jnp.where(row < 64, dqs[0], dqs[1]).T
        _ride_wait(copies, step_id == nsteps - 1)

    qmap = lambda b, g, p, i: (b * nq + i, g * 2 + p)
    kvmap = lambda b, g, p, i: (g, b, 0)
    return pl.pallas_call(
        body, name="attn_bwd", grid=grid,
        in_specs=[pl.BlockSpec((tq, 128), qmap),
                  pl.BlockSpec((1, s, 128), kvmap),
                  pl.BlockSpec((1, s, 128), kvmap),
                  pl.BlockSpec((1, 128, s), lambda b, g, p, i: (g, 0, b)),
                  pl.BlockSpec((tq, 128), qmap),
                  pl.BlockSpec((1, 2, 1, tq), lambda b, g, p, i: (b, g * 2 + p, 0, i)),
                  pl.BlockSpec((tq, 128), qmap),
                  pl.BlockSpec((8, 128), lambda b, g, p, i: (0, 0))] + rd.in_specs,
        out_specs=[pl.BlockSpec((tq, 128), qmap),
                   pl.BlockSpec((1, s, 128), kvmap),
                   pl.BlockSpec((1, s, 128), kvmap)] + rd.out_specs,
        out_shape=[jax.ShapeDtypeStruct((t, D_ATTN), F32),
                   jax.ShapeDtypeStruct((N_KV, t, 128), F32),
                   jax.ShapeDtypeStruct((N_KV, t, 128), F32)] + rd.out_shape,
        scratch_shapes=rd.scratch,
        compiler_params=_cparams(("arbitrary", "arbitrary", "arbitrary", "arbitrary")),
    )(q, kd, vd, kdt, o, lse, do, ones8, *rd.srcs)


SUB = 16
N_SUB = CHUNK // SUB


def _tri_mats():
    i = np.arange(CHUNK)
    same = (i[:, None] // SUB) == (i[None, :] // SUB)
    lower = jnp.asarray(same & (i[:, None] >= i[None, :]), dtype=BF16)
    upper = jnp.asarray(same & (i[:, None] <= i[None, :]), dtype=BF16)
    return jnp.stack([lower, upper])


def _running_sum(tri, x):
    acc = None
    for part in _split_bf16(x, 3):
        t = _dot(tri, part)
        acc = t if acc is None else acc + t
    return acc


def _gates(z, lb):
    sig = _sigmoid(z)
    f = lb + (1.0 - lb) * sig
    logf = jnp.log(jnp.maximum(f, F_MIN))
    k = (1.0 - lb) * (1.0 - sig)
    return sig, f, logf, k


def _row_group(jg, anti):
    if anti:
        return 0, 8 * jg + 8
    return 8 * jg, SUB


def _sub_order(anti):
    return range(N_SUB - 1, -1, -1) if anti else range(N_SUB)


def _block_columns(b, anti):
    tt = lax.broadcasted_iota(jnp.int32, (SUB, LANES), 0)
    cols = []
    for jg in range(SUB // 8):
        r0, r1 = _row_group(jg, anti)
        br = b[r0:r1]
        tr = tt[r0:r1]
        for i in range(8):
            sc = 8 * jg + i
            mask = (tr <= sc) if anti else (tr >= sc)
            cols.append((r0, r1, sc, jnp.where(mask, jnp.exp(jnp.minimum(br - b[sc:sc + 1], 0.0)), 0.0)))
    return cols


def _lockstep(gens):
    results = [None] * len(gens)
    live = list(range(len(gens)))
    while live:
        for i in list(live):
            try:
                next(gens[i])
            except StopIteration as stop:
                results[i] = stop.value
                live.remove(i)
    return results


def _scatter_rows(base, accs):
    pieces = []
    for g in range(SUB // 8):
        tot = base[8 * g:8 * g + 8]
        for (r0, r1), acc in accs.items():
            if r0 <= 8 * g and 8 * g + 8 <= r1:
                tot = tot + acc[8 * g - r0:8 * g - r0 + 8]
        pieces.append(tot)
    return jnp.concatenate(pieces, axis=0)


def _chunk_fwd(q, k, v, b, st, bones, bmask, anti):
    rs = [slice(SUB * i, SUB * i + SUB) for i in range(N_SUB)]
    decay, update, prods, spans, qbs = [], [], [], [], []
    for i in range(N_SUB):
        qi, ki, vi, bi = q[rs[i]], k[rs[i]], v[rs[i]], b[rs[i]]
        b_last = bi[0:1] if anti else bi[SUB - 1:SUB]
        decay.append(jnp.exp(b_last))
        update.append(_dot_tn(vi.astype(BF16), (ki * jnp.exp(b_last - bi)).astype(BF16)) * bmask)
        qbs.append((qi * jnp.exp(bi)).astype(BF16))
        for r0, r1, sc, e in _block_columns(bi, anti):
            prods.append(qi[r0:r1] * e * ki[sc:sc + 1])
            spans.append((i, r0, r1, sc))
    pb = _dot(jnp.concatenate(prods, axis=0).astype(BF16), bones)
    yield
    entered = [None] * N_SUB
    for i in _sub_order(anti):
        entered[i] = st
        st = st * decay[i] + update[i]
    yield
    accs = [dict() for _ in range(N_SUB)]
    off = 0
    for i, r0, r1, sc in spans:
        term = pb[off:off + r1 - r0] * v[rs[i]][sc:sc + 1]
        off += r1 - r0
        accs[i][(r0, r1)] = term if (r0, r1) not in accs[i] else accs[i][(r0, r1)] + term
    outs = [_scatter_rows(_dot_nt(qbs[i], entered[i].astype(BF16)), accs[i]) for i in range(N_SUB)]
    return jnp.concatenate(outs, axis=0), st, entered


def _hgrn_fwd(proj, lb, gw, nb, s, ride=None):
    t = proj.shape[0]
    nc = s // CHUNK
    tri = _tri_mats()
    bones = _block_ones(LANES, HEAD_DIM)
    rd = _ride_plan(ride)

    def body(*refs):
        q_ref, zf_ref, zb_ref, v_ref, g_ref, lb_ref, gw_ref, tri_ref, bones_ref = refs[:9]
        y_ref, os_ref, sts_ref = refs[9 + rd.n:12 + rd.n]
        st_ref = refs[12 + 2 * rd.n]
        copies = rd.copies(refs[9:9 + rd.n], refs[12 + rd.n:12 + 2 * rd.n], refs[13 + 2 * rd.n:])
        step_id = pl.program_id(0) * 2 + pl.program_id(1)
        _ride_start(copies, step_id == 0)
        bones_m = bones_ref[...]
        bmask = bones_m.astype(F32)
        st_ref[...] = jnp.zeros_like(st_ref)

        def one_direction(n, anti):
            side = 1 if anti else 0
            z_ref = zb_ref if anti else zf_ref
            cn = (nc - 1 - n) if anti else n
            rows = pl.ds(pl.multiple_of(cn * CHUNK, CHUNK), CHUNK)
            q = q_ref[rows, :]
            v = v_ref[rows, :]
            _, _, logf, k = _gates(z_ref[rows, :], lb_ref[side:side + 1])
            b = _running_sum(tri_ref[side], logf)
            yield
            o, st_new, entered = yield from _chunk_fwd(q, k, v, b, st_ref[side], bones_m, bmask, anti)
            for i in range(N_SUB):
                sts_ref[0, 0, side, cn * N_SUB + i] = entered[i].astype(BF16)
            st_ref[side] = st_new
            (y_ref if anti else os_ref)[rows, :] = o

        def step(n, carry):
            _lockstep([one_direction(n, False), one_direction(n, True)])
            return carry

        lax.fori_loop(0, nc, step, 0)

        def join(n, carry):
            rows = pl.ds(pl.multiple_of(n * CHUNK, CHUNK), CHUNK)
            osum = os_ref[rows, :] + y_ref[rows, :]
            os_ref[rows, :] = osum
            r = lax.rsqrt(_dot_precise(osum * osum, bones_m, 2) * (1.0 / HEAD_DIM) + EPS)
            hg = g_ref[rows, :]
            y_ref[rows, :] = osum * r * gw_ref[...] * (hg * _sigmoid(hg))
            return carry

        lax.fori_loop(0, nc, join, 0)
        _ride_wait(copies, step_id == nb * 2 - 1)

    def col(c):
        return pl.BlockSpec((s, LANES), lambda b, p, c=c: (b, c + p))

    return pl.pallas_call(
        body, name="hgrn_fwd", grid=(nb, 2),
        in_specs=[col(COL_HQ), col(COL_ZFW), col(COL_ZBW), col(COL_HI), col(COL_HG),
                  pl.BlockSpec((2, LANES), lambda b, p: (0, p)),
                  pl.BlockSpec((1, LANES), lambda b, p: (0, 0)),
                  pl.BlockSpec((2, CHUNK, CHUNK), lambda b, p: (0, 0, 0)),
                  pl.BlockSpec((LANES, LANES), lambda b, p: (0, 0))] + rd.in_specs,
        out_specs=[pl.BlockSpec((s, LANES), lambda b, p: (b, p)),
                   pl.BlockSpec((s, LANES), lambda b, p: (b, p)),
                   pl.BlockSpec((1, 1, 2, nc * N_SUB, LANES, LANES), lambda b, p: (b, p, 0, 0, 0, 0))]
        + rd.out_specs,
        out_shape=[jax.ShapeDtypeStruct((t, D_HGRN), F32), jax.ShapeDtypeStruct((t, D_HGRN), F32),
                   jax.ShapeDtypeStruct((nb, 2, 2, nc * N_SUB, LANES, LANES), BF16)] + rd.out_shape,
        scratch_shapes=[pltpu.VMEM((2, LANES, LANES), F32)] + rd.scratch,
        compiler_params=_cparams(("arbitrary", "arbitrary")),
    )(proj, proj, proj, proj, proj, lb, gw, tri, bones, *rd.srcs)


def _chunk_bwd(q, k, v, b, do, states, rt, bones, bmask, anti):
    rs = [slice(SUB * i, SUB * i + SUB) for i in range(N_SUB)]
    r8 = lax.broadcasted_iota(jnp.int32, (8, LANES), 0)
    decay, update, dq_inter, ebls, prods_p, prods_d, spans, qes, kes = [], [], [], [], [], [], [], [], []
    for i in range(N_SUB):
        qi, ki, vi, bi, doi = q[rs[i]], k[rs[i]], v[rs[i]], b[rs[i]], do[rs[i]]
        b_last = bi[0:1] if anti else bi[SUB - 1:SUB]
        eb = jnp.exp(bi)
        dob = doi.astype(BF16)
        decay.append(jnp.exp(b_last))
        ebls.append(jnp.exp(b_last - bi))
        update.append(_dot_tn(dob, (qi * eb).astype(BF16)) * bmask)
        dq_inter.append(eb * _dot(dob, states[i]))
        for r0, r1, sc, e in _block_columns(bi, anti):
            qe = qi[r0:r1] * e
            qes.append(qe)
            kes.append(e * ki[sc:sc + 1])
            prods_p.append(qe * ki[sc:sc + 1])
            prods_d.append(doi[r0:r1] * vi[sc:sc + 1])
            spans.append((i, r0, r1, sc))
    sums = _dot(jnp.concatenate(prods_p + prods_d, axis=0).astype(BF16), bones)
    half = sum(r1 - r0 for _, r0, r1, _ in spans)
    yield
    entered = [None] * N_SUB
    for i in reversed(list(_sub_order(anti))):
        entered[i] = rt
        rt = rt * decay[i] + update[i]
    yield
    accs = [dict() for _ in range(N_SUB)]
    dk_blks = [[jnp.zeros((8, LANES), F32) for _ in range(SUB // 8)] for _ in range(N_SUB)]
    dv_blks = [[jnp.zeros((8, LANES), F32) for _ in range(SUB // 8)] for _ in range(N_SUB)]
    off = 0
    for n, (i, r0, r1, sc) in enumerate(spans):
        nr = r1 - r0
        pb = sums[off:off + nr]
        dpb = sums[half + off:half + off + nr]
        off += nr
        term = dpb * kes[n]
        accs[i][(r0, r1)] = term if (r0, r1) not in accs[i] else accs[i][(r0, r1)] + term
        dk_s = jnp.sum(dpb * qes[n], axis=0, keepdims=True)
        dv_s = jnp.sum(pb * do[rs[i]][r0:r1], axis=0, keepdims=True)
        dk_blks[i][sc // 8] = jnp.where(r8 == sc % 8, dk_s, dk_blks[i][sc // 8])
        dv_blks[i][sc // 8] = jnp.where(r8 == sc % 8, dv_s, dv_blks[i][sc // 8])
    dqs, dks, dvs, dbs = [], [], [], []
    for i in range(N_SUB):
        ki, vi = k[rs[i]], v[rs[i]]
        rtb = entered[i].astype(BF16)
        dk_inter = ebls[i] * _dot(vi.astype(BF16), rtb)
        dv_inter = _dot_nt((ki * ebls[i]).astype(BF16), rtb)
        dqs.append(_scatter_rows(dq_inter[i], accs[i]))
        dks.append(dk_inter + jnp.concatenate(dk_blks[i], axis=0))
        dvs.append(dv_inter + jnp.concatenate(dv_blks[i], axis=0))
        db_last = (jnp.sum(ki * dk_inter, axis=0, keepdims=True)
                   + decay[i] * jnp.sum(entered[i] * states[i].astype(F32), axis=0, keepdims=True))
        dbs.append(jnp.broadcast_to(db_last, (SUB, LANES)))
    cat = lambda xs: jnp.concatenate(xs, axis=0)
    return cat(dqs), cat(dks), cat(dvs), rt, cat(dbs)


def _hgrn_bwd(proj, lb, gw, osum, states, dy, nb, s, ride=None):
    t = proj.shape[0]
    nc = s // CHUNK
    assert nc % 2 == 0
    tri = _tri_mats()
    bones = _block_ones(LANES, HEAD_DIM)
    rd = _ride_plan(ride)

    def body(*refs):
        (q_ref, zf_ref, zb_ref, v_ref, g_ref, lb_ref, gw_ref, os_ref, sts_ref, dy_ref, tri_ref,
         bones_ref) = refs[:12]
        dq_ref, dzf_ref, dzb_ref, dv_ref, dg_ref, dgw_ref, dlb_ref = refs[12 + rd.n:19 + rd.n]
        do_sc, dq_sc, dv_sc, rt_cur = refs[19 + 2 * rd.n:23 + 2 * rd.n]
        copies = rd.copies(refs[12:12 + rd.n], refs[19 + rd.n:19 + 2 * rd.n], refs[23 + 2 * rd.n:])
        step_id = pl.program_id(0) * 2 + pl.program_id(1)
        _ride_start(copies, step_id == 0)
        bones_m = bones_ref[...]
        bmask = bones_m.astype(F32)
        gwv = gw_ref[...]

        def head(n, acc):
            rows = pl.ds(pl.multiple_of(n * CHUNK, CHUNK), CHUNK)
            o = os_ref[rows, :]
            hg = g_ref[rows, :]
            dyv = dy_ref[rows, :]
            sg = _sigmoid(hg)
            r = lax.rsqrt(_dot_precise(o * o, bones_m, 2) * (1.0 / HEAD_DIM) + EPS)
            nrm = o * r * gwv
            dn = dyv * (hg * sg)
            dg_ref[rows, :] = (dyv * nrm * (sg * (1.0 + hg * (1.0 - sg)))).astype(BF16)
            g = dn * gwv
            mean_go = _dot_precise(g * o, bones_m, 2) * (1.0 / HEAD_DIM)
            do_sc[rows, :] = r * (g - o * (r * r) * mean_go)
            return acc + jnp.sum(dn * o * r, axis=0, keepdims=True)

        dgw_ref[0] = lax.fori_loop(0, nc, head, jnp.zeros((1, LANES), F32))
        dq_sc[...] = jnp.zeros_like(dq_sc)
        dv_sc[...] = jnp.zeros_like(dv_sc)

        rt_cur[...] = jnp.zeros_like(rt_cur)

        def one_direction(n, anti):
            side = 1 if anti else 0
            z_ref = zb_ref if anti else zf_ref
            dz_ref = dzb_ref if anti else dzf_ref
            lbv = lb_ref[side:side + 1]
            cn = n if anti else (nc - 1 - n)
            rows = pl.ds(pl.multiple_of(cn * CHUNK, CHUNK), CHUNK)
            q = q_ref[rows, :]
            v = v_ref[rows, :]
            sig, f, logf, k = _gates(z_ref[rows, :], lbv)
            b = _running_sum(tri_ref[side], logf)
            yield
            do = do_sc[rows, :]
            entered = [sts_ref[0, 0, side, cn * N_SUB + i] for i in range(N_SUB)]
            dq, dk, dv, rt_new, db_last = yield from _chunk_bwd(q, k, v, b, do, entered, rt_cur[side], bones_m,
                                                                bmask, anti)
            rt_cur[side] = rt_new
            dq_sc[rows, :] += dq
            dv_sc[rows, :] += dv
            dlogf = _running_sum(tri_ref[1 - side], q * dq - k * dk) + db_last
            dfl = jnp.where(f > F_MIN, dlogf / f, 0.0)
            dz_ref[rows, :] = ((dfl - dk) * (1.0 - lbv) * sig * (1.0 - sig)).astype(BF16)
            return jnp.sum((dfl - dk) * (1.0 - sig), axis=0, keepdims=True)

        def back(n, dlb):
            d0, d1 = _lockstep([one_direction(n, False), one_direction(n, True)])
            return dlb[0] + d0, dlb[1] + d1

        zero = jnp.zeros((1, LANES), F32)
        dlb0, dlb1 = lax.fori_loop(0, nc, back, (zero, zero))
        dlb_ref[0, 0:1, :] = dlb0
        dlb_ref[0, 1:2, :] = dlb1

        dq_ref[...] = dq_sc[...].astype(BF16)
        dv_ref[...] = dv_sc[...].astype(BF16)
        _ride_wait(copies, step_id == nb * 2 - 1)

    def col(c):
        return pl.BlockSpec((s, LANES), lambda b, p, c=c: (b, c + p))

    sl = pl.BlockSpec((s, LANES), lambda b, p: (b, p))
    out_t = jax.ShapeDtypeStruct((t, D_HGRN), BF16)
    return pl.pallas_call(
        body, name="hgrn_bwd", grid=(nb, 2),
        in_specs=[col(COL_HQ), col(COL_ZFW), col(COL_ZBW), col(COL_HI), col(COL_HG),
                  pl.BlockSpec((2, LANES), lambda b, p: (0, p)),
                  pl.BlockSpec((1, LANES), lambda b, p: (0, 0)),
                  sl,
                  pl.BlockSpec((1, 1, 2, nc * N_SUB, LANES, LANES), lambda b, p: (b, p, 0, 0, 0, 0)),
                  sl,
                  pl.BlockSpec((2, CHUNK, CHUNK), lambda b, p: (0, 0, 0)),
                  pl.BlockSpec((LANES, LANES), lambda b, p: (0, 0))] + rd.in_specs,
        out_specs=[sl, sl, sl, sl, sl,
                   pl.BlockSpec((1, 1, LANES), lambda b, p: (b, 0, p)),
                   pl.BlockSpec((1, 2, LANES), lambda b, p: (b, 0, p))] + rd.out_specs,
        out_shape=[out_t, out_t, out_t, out_t, out_t,
                   jax.ShapeDtypeStruct((nb, 1, D_HGRN), F32),
                   jax.ShapeDtypeStruct((nb, 2, D_HGRN), F32)] + rd.out_shape,
        scratch_shapes=[pltpu.VMEM((s, LANES), F32), pltpu.VMEM((s, LANES), F32), pltpu.VMEM((s, LANES), F32),
                        pltpu.VMEM((2, LANES, LANES), F32)] + rd.scratch,
        compiler_params=_cparams(("arbitrary", "arbitrary")),
    )(proj, proj, proj, proj, proj, lb, gw, osum, states, dy, tri, bones, *rd.srcs)


def _lower_bounds(logits):
    def body(lg_ref, lb_ref):
        rows = [lg_ref[l:l + 1, :] for l in range(DEPTH)]
        m = functools.reduce(jnp.maximum, rows)
        ex = [jnp.exp(r - m) for r in rows]
        den = functools.reduce(jnp.add, ex)
        run = jnp.zeros_like(m)
        for l in range(DEPTH):
            if l > 0:
                run = run + ex[l] / den
            lb_ref[l:l + 1, :] = run

    return pl.pallas_call(body, name="lower_bounds", out_shape=jax.ShapeDtypeStruct(logits.shape, F32))(logits)


def _lower_bounds_bwd(logits, dlb):
    def body(lg_ref, dlb_ref, dlg_ref):
        rows = [lg_ref[l:l + 1, :] for l in range(DEPTH)]
        m = functools.reduce(jnp.maximum, rows)
        ex = [jnp.exp(r - m) for r in rows]
        den = functools.reduce(jnp.add, ex)
        sm = [e / den for e in ex]
        dsm = [jnp.zeros_like(m) for _ in range(DEPTH)]
        for i in range(1, DEPTH):
            for l in range(i, DEPTH):
                dsm[i] = dsm[i] + dlb_ref[l:l + 1, :]
        dot = functools.reduce(jnp.add, [sm[i] * dsm[i] for i in range(DEPTH)])
        for i in range(DEPTH):
            dlg_ref[i:i + 1, :] = sm[i] * (dsm[i] - dot)

    return pl.pallas_call(body, name="lower_bounds_bwd", out_shape=jax.ShapeDtypeStruct(logits.shape, F32))(logits, dlb)


CONV_ROWS = 128


def _conv_core(a, bg, dww, dwb, lnw, lnb, upad_ref, s):
    sb = _sigmoid(bg)
    u = a * sb
    upad_ref[0:16, :] = jnp.zeros((16, D_CONV), F32)
    upad_ref[16:16 + s, :] = u
    upad_ref[16 + s:32 + s, :] = jnp.zeros((16, D_CONV), F32)
    rows = min(s, CONV_ROWS)
    pieces = []
    for r0 in range(0, s, rows):
        acc = None
        for j in range(CONV_W):
            term = upad_ref[r0 + 1 + j:r0 + 1 + j + rows, :] * dww[j:j + 1, :]
            acc = term if acc is None else acc + term
        pieces.append(acc)
    c = jnp.concatenate(pieces, axis=0) + dwb
    mu = jnp.mean(c, axis=-1, keepdims=True)
    xc = c - mu
    rstd = lax.rsqrt(jnp.mean(xc * xc, axis=-1, keepdims=True) + LN_EPS)
    nh = xc * rstd
    l = nh * lnw + lnb
    sl = _sigmoid(l)
    return sb, nh, rstd, l, sl


def _conv_fwd(proj, dww, dwb, lnw, lnb, pww, pwb, nb, s):
    t = proj.shape[0]
    assert s % min(s, CONV_ROWS) == 0

    def body(a_ref, b_ref, dww_ref, dwb_ref, lnw_ref, lnb_ref, pww_ref, pwb_ref, y_ref, upad_ref):
        _, _, _, l, sl = _conv_core(a_ref[...], b_ref[...], dww_ref[...], dwb_ref[...], lnw_ref[...],
                                    lnb_ref[...], upad_ref, s)
        y_ref[...] = _dot((l * sl).astype(BF16), pww_ref[...]) + pwb_ref[...]

    vec = pl.BlockSpec((1, D_CONV), lambda b: (0, 0))
    return pl.pallas_call(
        body, name="conv_fwd", grid=(nb,),
        in_specs=[pl.BlockSpec((s, D_CONV), lambda b: (b, COL_CA)),
                  pl.BlockSpec((s, D_CONV), lambda b: (b, COL_CB)),
                  pl.BlockSpec((32, D_CONV), lambda b: (0, 0)), vec, vec, vec,
                  pl.BlockSpec((D_CONV, D_CONV), lambda b: (0, 0)), vec],
        out_specs=pl.BlockSpec((s, D_CONV), lambda b: (b, 0)),
        out_shape=jax.ShapeDtypeStruct((t, D_CONV), F32),
        scratch_shapes=[pltpu.VMEM((s + 32, D_CONV), F32)],
        compiler_params=_cparams(("parallel",)),
    )(proj, proj, dww, dwb, lnw, lnb, pww, pwb)


def _conv_bwd(proj, dy, dww, dwb, lnw, lnb, pww, nb, s):
    t = proj.shape[0]

    def body(a_ref, b_ref, dy_ref, dww_ref, dwb_ref, lnw_ref, lnb_ref, pww_ref,
             dab_ref, ddww_ref, ddwb_ref, dlnw_ref, dlnb_ref, dpww_ref, dpwb_ref, upad_ref, dcpad_ref):
        a = a_ref[...]
        dww = dww_ref[...]
        sb, nh, rstd, l, sl = _conv_core(a, b_ref[...], dww, dwb_ref[...], lnw_ref[...], lnb_ref[...],
                                         upad_ref, s)
        dyv = dy_ref[...]
        dyb = dyv.astype(BF16)
        ds = _dot_nt(dyb, pww_ref[...])
        dl = ds * (sl * (1.0 + l * (1.0 - sl)))
        dn = dl * lnw_ref[...]
        dc = rstd * (dn - jnp.mean(dn, axis=-1, keepdims=True)
                     - nh * jnp.mean(dn * nh, axis=-1, keepdims=True))

        @pl.when(pl.program_id(0) == 0)
        def _():
            for r in (ddww_ref, ddwb_ref, dlnw_ref, dlnb_ref, dpww_ref, dpwb_ref):
                r[...] = jnp.zeros_like(r)

        dpww_ref[...] += _dot_tn((l * sl).astype(BF16), dyb)
        dpwb_ref[...] += jnp.sum(dyv, axis=0, keepdims=True)
        dlnw_ref[...] += jnp.sum(dl * nh, axis=0, keepdims=True)
        dlnb_ref[...] += jnp.sum(dl, axis=0, keepdims=True)
        ddwb_ref[...] += jnp.sum(dc, axis=0, keepdims=True)

        dcpad_ref[0:16, :] = jnp.zeros((16, D_CONV), F32)
        dcpad_ref[16:16 + s, :] = dc
        dcpad_ref[16 + s:32 + s, :] = jnp.zeros((16, D_CONV), F32)
        rows = min(s, CONV_ROWS)
        r8 = lax.broadcasted_iota(jnp.int32, (32, D_CONV), 0)
        ddww = jnp.zeros((32, D_CONV), F32)
        pieces = []
        for r0 in range(0, s, rows):
            acc = None
            dcr = dcpad_ref[16 + r0:16 + r0 + rows, :]
            for j in range(CONV_W):
                term = dcpad_ref[r0 + 31 - j:r0 + 31 - j + rows, :] * dww[j:j + 1, :]
                acc = term if acc is None else acc + term
                wj = jnp.sum(dcr * upad_ref[r0 + 1 + j:r0 + 1 + j + rows, :], axis=0, keepdims=True)
                ddww = ddww + jnp.where(r8 == j, wj, 0.0)
            pieces.append(acc)
        du = jnp.concatenate(pieces, axis=0)
        ddww_ref[...] += ddww
        dab_ref[:, 0:D_CONV] = (du * sb).astype(BF16)
        dab_ref[:, D_CONV:2 * D_CONV] = (du * a * sb * (1.0 - sb)).astype(BF16)

    vec = pl.BlockSpec((1, D_CONV), lambda b: (0, 0))
    mat = pl.BlockSpec((D_CONV, D_CONV), lambda b: (0, 0))
    w32 = pl.BlockSpec((32, D_CONV), lambda b: (0, 0))
    vshape = jax.ShapeDtypeStruct((1, D_CONV), F32)
    return pl.pallas_call(
        body, name="conv_bwd", grid=(nb,),
        in_specs=[pl.BlockSpec((s, D_CONV), lambda b: (b, COL_CA)),
                  pl.BlockSpec((s, D_CONV), lambda b: (b, COL_CB)),
                  pl.BlockSpec((s, D_CONV), lambda b: (b, 0)),
                  w32, vec, vec, vec, mat],
        out_specs=[pl.BlockSpec((s, 2 * D_CONV), lambda b: (b, 0)), w32, vec, vec, vec, mat, vec],
        out_shape=[jax.ShapeDtypeStruct((t, 2 * D_CONV), BF16),
                   jax.ShapeDtypeStruct((32, D_CONV), F32), vshape, vshape, vshape,
                   jax.ShapeDtypeStruct((D_CONV, D_CONV), F32), vshape],
        scratch_shapes=[pltpu.VMEM((s + 32, D_CONV), F32), pltpu.VMEM((s + 32, D_CONV), F32)],
        compiler_params=_cparams(("arbitrary",)),
    )(proj, proj, dy, dww, dwb, lnw, lnb, pww)


def _mix_out(o_attn, y_hgrn, y_conv, x, aw, cw, w_out, tm):
    t = x.shape[0]

    def body(o_ref, h_ref, c_ref, x_ref, aw_ref, cw_ref, w_ref, mixed_ref, x1_ref):
        o = o_ref[...]
        a = o * lax.rsqrt(jnp.mean(o * o, axis=-1, keepdims=True) + EPS) * aw_ref[...]
        yc = c_ref[...]
        c = yc * lax.rsqrt(jnp.mean(yc * yc, axis=-1, keepdims=True) + EPS) * cw_ref[...]
        ab, hb, cb = a.astype(BF16), h_ref[...].astype(BF16), c.astype(BF16)
        mixed_ref[:, 0:512] = ab
        mixed_ref[:, 512:768] = hb
        mixed_ref[:, 768:1024] = cb
        x1_ref[...] = (x_ref[...] + _dot(ab, w_ref[0:512, :]) + _dot(hb, w_ref[512:768, :])
                       + _dot(cb, w_ref[768:1024, :]))

    def tok(w):
        return pl.BlockSpec((tm, w), lambda i: (i, 0))

    return pl.pallas_call(
        body, name="mix_out", grid=(t // tm,),
        in_specs=[tok(512), tok(256), tok(256), tok(D_MODEL),
                  pl.BlockSpec((1, 512), lambda i: (0, 0)), pl.BlockSpec((1, 256), lambda i: (0, 0)),
                  pl.BlockSpec((D_MODEL, D_MODEL), lambda i: (0, 0))],
        out_specs=[tok(D_MODEL), tok(D_MODEL)],
        out_shape=[jax.ShapeDtypeStruct((t, D_MODEL), BF16), jax.ShapeDtypeStruct((t, D_MODEL), F32)],
        compiler_params=_cparams(("parallel",)),
    )(o_attn, y_hgrn, y_conv, x, aw, cw, w_out)


def _mix_out_bwd(dx1, w_out, o_attn, y_conv, aw, cw, tm):
    t = dx1.shape[0]

    def body(dx_ref, w_ref, o_ref, c_ref, aw_ref, cw_ref, do_ref, dh_ref, dc_ref, daw_ref, dcw_ref):
        dm = _dot_nt(dx_ref[...].astype(BF16), w_ref[...])
        do, daw = _rms_bwd(dm[:, 0:512], o_ref[...], aw_ref[...])
        dc, dcw = _rms_bwd(dm[:, 768:1024], c_ref[...], cw_ref[...])
        do_ref[...] = do
        dh_ref[...] = dm[:, 512:768]
        dc_ref[...] = dc

        @pl.when(pl.program_id(0) == 0)
        def _():
            daw_ref[...] = jnp.zeros_like(daw_ref)
            dcw_ref[...] = jnp.zeros_like(dcw_ref)

        daw_ref[...] += jnp.sum(daw, axis=0, keepdims=True)
        dcw_ref[...] += jnp.sum(dcw, axis=0, keepdims=True)

    def tok(w):
        return pl.BlockSpec((tm, w), lambda i: (i, 0))

    v512 = pl.BlockSpec((1, 512), lambda i: (0, 0))
    v256 = pl.BlockSpec((1, 256), lambda i: (0, 0))
    return pl.pallas_call(
        body, name="mix_out_bwd", grid=(t // tm,),
        in_specs=[tok(D_MODEL), pl.BlockSpec((D_MODEL, D_MODEL), lambda i: (0, 0)), tok(512), tok(256),
                  v512, v256],
        out_specs=[tok(512), tok(256), tok(256), v512, v256],
        out_shape=[jax.ShapeDtypeStruct((t, 512), F32), jax.ShapeDtypeStruct((t, 256), F32),
                   jax.ShapeDtypeStruct((t, 256), F32), jax.ShapeDtypeStruct((1, 512), F32),
                   jax.ShapeDtypeStruct((1, 256), F32)],
        compiler_params=_cparams(("arbitrary",)),
    )(dx1, w_out, o_attn, y_conv, aw, cw)


FF_BLOCKS = 4


def _ffn_fwd(x1, fw, wg, wu, wd, tm, ride=None):
    t = x1.shape[0]
    fb = wg.shape[2]
    nf = N_DEV // FF_BLOCKS
    rd = _ride_plan(ride)
    grid = (t // tm, nf)

    def body(*refs):
        x_ref, fw_ref, wg_ref, wu_ref, wd_ref = refs[:5]
        h_ref, g_ref, u_ref, a_ref, x2_ref = refs[5 + rd.n:10 + rd.n]
        acc_ref = refs[10 + 2 * rd.n]
        copies = rd.copies(refs[5:5 + rd.n], refs[10 + rd.n:10 + 2 * rd.n], refs[11 + 2 * rd.n:])
        step_id = _grid_step_id(grid)
        _ride_start(copies, step_id == 0)
        j = pl.program_id(1)

        @pl.when(j == 0)
        def _():
            xv = x_ref[...]
            r = lax.rsqrt(jnp.mean(xv * xv, axis=-1, keepdims=True) + EPS)
            h_ref[...] = (xv * r * fw_ref[...]).astype(BF16)
            acc_ref[...] = xv

        h = h_ref[...]
        out = None
        for c in range(FF_BLOCKS):
            g = _dot(h, wg_ref[c])
            u = _dot(h, wu_ref[c])
            a = (g * _sigmoid(g) * u).astype(BF16)
            g_ref[c] = g.astype(BF16)
            u_ref[c] = u.astype(BF16)
            a_ref[c] = a
            part = _dot(a, wd_ref[c])
            out = part if out is None else out + part
        acc_ref[...] += out

        @pl.when(j == nf - 1)
        def _():
            x2_ref[...] = acc_ref[...]

        _ride_wait(copies, step_id == (t // tm) * nf - 1)

    tok = pl.BlockSpec((tm, D_MODEL), lambda i, j: (i, 0))
    ffb = pl.BlockSpec((FF_BLOCKS, tm, fb), lambda i, j: (j, i, 0))
    ffs = jax.ShapeDtypeStruct((N_DEV, t, fb), BF16)
    return pl.pallas_call(
        body, name="ffn_fwd", grid=grid,
        in_specs=[tok, pl.BlockSpec((1, D_MODEL), lambda i, j: (0, 0)),
                  pl.BlockSpec((FF_BLOCKS, D_MODEL, fb), lambda i, j: (j, 0, 0)),
                  pl.BlockSpec((FF_BLOCKS, D_MODEL, fb), lambda i, j: (j, 0, 0)),
                  pl.BlockSpec((FF_BLOCKS, fb, D_MODEL), lambda i, j: (j, 0, 0))] + rd.in_specs,
        out_specs=[tok, ffb, ffb, ffb, tok] + rd.out_specs,
        out_shape=[jax.ShapeDtypeStruct((t, D_MODEL), BF16), ffs, ffs, ffs,
                   jax.ShapeDtypeStruct((t, D_MODEL), F32)] + rd.out_shape,
        scratch_shapes=[pltpu.VMEM((tm, D_MODEL), F32)] + rd.scratch,
        compiler_params=_cparams(("arbitrary", "arbitrary")),
    )(x1, fw, wg, wu, wd, *rd.srcs)


def _ffn_bwd(dx2, g, u, wg, wu, wd, x1, fw, tm, ride=None):
    t = dx2.shape[0]
    fb = wg.shape[2]
    nf = N_DEV // FF_BLOCKS
    rd = _ride_plan(ride)
    grid = (t // tm, nf)

    def body(*refs):
        dx_ref, g_ref, u_ref, wg_ref, wu_ref, wd_ref, x_ref, fw_ref = refs[:8]
        dg_ref, du_ref, dx1_ref, dfw_ref = refs[8 + rd.n:12 + rd.n]
        acc_ref = refs[12 + 2 * rd.n]
        copies = rd.copies(refs[8:8 + rd.n], refs[12 + rd.n:12 + 2 * rd.n], refs[13 + 2 * rd.n:])
        step_id = _grid_step_id(grid)
        _ride_start(copies, step_id == 0)
        i = pl.program_id(0)
        j = pl.program_id(1)
        dxb = dx_ref[...].astype(BF16)
        dh = None
        for c in range(FF_BLOCKS):
            da = _dot_nt(dxb, wd_ref[c])
            gv = g_ref[c].astype(F32)
            uv = u_ref[c].astype(F32)
            sg = _sigmoid(gv)
            dg = (da * uv * (sg * (1.0 + gv * (1.0 - sg)))).astype(BF16)
            du = (da * gv * sg).astype(BF16)
            dg_ref[c] = dg
            du_ref[c] = du
            part = _dot_nt(dg, wg_ref[c]) + _dot_nt(du, wu_ref[c])
            dh = part if dh is None else dh + part

        @pl.when(j == 0)
        def _():
            acc_ref[...] = dh

        @pl.when(j > 0)
        def _():
            acc_ref[...] += dh

        @pl.when((i == 0) & (j == 0))
        def _():
            dfw_ref[...] = jnp.zeros_like(dfw_ref)

        @pl.when(j == nf - 1)
        def _():
            dx, dfw = _rms_bwd(acc_ref[...], x_ref[...], fw_ref[...])
            dx1_ref[...] = dx_ref[...] + dx
            dfw_ref[...] += jnp.sum(dfw, axis=0, keepdims=True)

        _ride_wait(copies, step_id == (t // tm) * nf - 1)

    tok = pl.BlockSpec((tm, D_MODEL), lambda i, j: (i, 0))
    ffb = pl.BlockSpec((FF_BLOCKS, tm, fb), lambda i, j: (j, i, 0))
    ffs = jax.ShapeDtypeStruct((N_DEV, t, fb), BF16)
    vec = pl.BlockSpec((1, D_MODEL), lambda i, j: (0, 0))
    return pl.pallas_call(
        body, name="ffn_bwd", grid=grid,
        in_specs=[tok, ffb, ffb,
                  pl.BlockSpec((FF_BLOCKS, D_MODEL, fb), lambda i, j: (j, 0, 0)),
                  pl.BlockSpec((FF_BLOCKS, D_MODEL, fb), lambda i, j: (j, 0, 0)),
                  pl.BlockSpec((FF_BLOCKS, fb, D_MODEL), lambda i, j: (j, 0, 0)),
                  tok, vec] + rd.in_specs,
        out_specs=[ffb, ffb, tok, vec] + rd.out_specs,
        out_shape=[ffs, ffs, jax.ShapeDtypeStruct((t, D_MODEL), F32),
                   jax.ShapeDtypeStruct((1, D_MODEL), F32)] + rd.out_shape,
        scratch_shapes=[pltpu.VMEM((tm, D_MODEL), F32)] + rd.scratch,
        compiler_params=_cparams(("arbitrary", "arbitrary")),
    )(dx2, g, u, wg, wu, wd, x1, fw, *rd.srcs)


def _loss_grad(y, target, tm):
    t, d = y.shape

    def body(y_ref, t_ref, dy_ref, loss_ref):
        err = y_ref[...] - t_ref[...]
        dy_ref[...] = err * (1.0 / d)

        @pl.when(pl.program_id(0) == 0)
        def _():
            loss_ref[...] = jnp.zeros_like(loss_ref)

        part = jnp.sum(jnp.sum(err * err, axis=-1, keepdims=True), axis=0, keepdims=True)
        loss_ref[...] += part * (0.5 / d)

    tok = pl.BlockSpec((tm, d), lambda i: (i, 0))
    return pl.pallas_call(
        body, name="loss_grad", grid=(t // tm,),
        in_specs=[tok, tok],
        out_specs=[tok, pl.BlockSpec((1, 1), lambda i: (0, 0))],
        out_shape=[jax.ShapeDtypeStruct((t, d), F32), jax.ShapeDtypeStruct((1, 1), F32)],
        compiler_params=_cparams(("arbitrary",)),
    )(y, target)


def _tile(v, reps):
    return jnp.tile(v.reshape(1, -1), (1, reps))


class _LocalPlan:
    def __init__(self, wb):
        self.w = [{n: wb[n][l] for n in BIG_AXIS} for l in range(DEPTH)]

    def ride(self, kernel_name, l, grads=None):
        return None

    def done(self, kernel_name, l, outs):
        pass


def _local_step(x, target, p, plan):
    nb, s, d = x.shape
    t = nb * s
    tm = min(512, s)
    tq = min(1024, s)
    xf = x.reshape(t, d)
    cosq, sinq = _rope_tables(s)
    ones512 = _block_ones(512, HEAD_DIM)
    lbs = _lower_bounds(p["hgrn_lb_logits"].reshape(DEPTH, 2 * D_HGRN)).reshape(DEPTH, 2, D_HGRN)

    saved = []
    cur = xf
    wb = plan.w
    for l in range(DEPTH):
        qw = _tile(p["q_norm_w"][l], N_HEADS)
        kw = _tile(p["k_norm_w"][l], N_KV)
        gw = _tile(p["hgrn_gnorm_w"][l], 2)
        dww = jnp.pad(p["conv_dw_w"][l], ((0, 1), (0, 0)))
        pww = p["conv_pw_w"][l].astype(BF16)
        h0, proj = _rms_proj(cur, _row(p["mix_norm_w"][l]), wb[l]["w_in"], tm)
        qr, kd, vd, kdt, vdt = _qkv_prep(proj, cosq, sinq, qw, kw, ones512, s, tm)
        o_attn, lse, *rode = _attn_fwd(qr, kd, vdt, nb, s, tq, plan.ride("attn_fwd", l))
        plan.done("attn_fwd", l, rode)
        y_hgrn, osum, states, *rode = _hgrn_fwd(proj, lbs[l], gw, nb, s, plan.ride("hgrn_fwd", l))
        plan.done("hgrn_fwd", l, rode)
        y_conv = _conv_fwd(proj, dww, _row(p["conv_dw_b"][l]), _row(p["conv_ln_w"][l]),
                           _row(p["conv_ln_b"][l]), pww, _row(p["conv_pw_b"][l]), nb, s)
        mixed, x1 = _mix_out(o_attn, y_hgrn, y_conv, cur, _row(p["attn_out_norm_w"][l]),
                             _row(p["conv_out_norm_w"][l]), wb[l]["w_out"], tm)
        hf, g, u, a, x2, *rode = _ffn_fwd(x1, _row(p["ffn_norm_w"][l]), wb[l]["w_gate"], wb[l]["w_up"],
                                          wb[l]["w_down"], tm, plan.ride("ffn_fwd", l))
        plan.done("ffn_fwd", l, rode)
        saved.append(dict(x=cur, h0=h0, proj=proj, qr=qr, kd=kd, vd=vd, kdt=kdt, o_attn=o_attn, lse=lse,
                          osum=osum, states=states, y_conv=y_conv, mixed=mixed, x1=x1, hf=hf, g=g, u=u, a=a,
                          qw=qw, kw=kw, gw=gw, dww=dww, pww=pww))
        cur = x2

    dcur, loss = _loss_grad(cur, target.reshape(t, d), tm)

    grads = {k: [None] * DEPTH for k in WEIGHTS}
    dlb = [None] * DEPTH
    for l in reversed(range(DEPTH)):
        sv = saved[l]
        dg, du, dx1, dfw, *rode = _ffn_bwd(dcur, sv["g"], sv["u"], wb[l]["w_gate"], wb[l]["w_up"],
                                           wb[l]["w_down"], sv["x1"], _row(p["ffn_norm_w"][l]), tm,
                                           plan.ride("ffn_bwd", l, grads))
        plan.done("ffn_bwd", l, rode)
        grads["ffn_norm_w"][l] = dfw[0]
        grads["w_gate"][l] = _dw_ff(sv["hf"], dg, False, "dw_gate", tm)
        grads["w_up"][l] = _dw_ff(sv["hf"], du, False, "dw_up", tm)
        grads["w_down"][l] = _dw_ff(sv["a"], dcur, True, "dw_down", tm)
        do_attn, dy_hgrn, dy_conv, daw, dcw = _mix_out_bwd(
            dx1, wb[l]["w_out"], sv["o_attn"], sv["y_conv"], _row(p["attn_out_norm_w"][l]),
            _row(p["conv_out_norm_w"][l]), tm)
        grads["attn_out_norm_w"][l] = daw[0]
        grads["conv_out_norm_w"][l] = dcw[0]
        grads["w_out"][l] = _mm_tn(sv["mixed"], dx1, D_MODEL, "dw_out", tm)
        dq, dkd, dvd, *rode = _attn_bwd(sv["qr"], sv["kd"], sv["vd"], sv["kdt"], sv["o_attn"], sv["lse"], do_attn,
                                        nb, s, tq, plan.ride("attn_bwd", l, grads))
        plan.done("attn_bwd", l, rode)
        dqkv, dqw, dkw = _qkv_bwd(sv["proj"], dq, dkd, dvd, cosq, sinq, sv["qw"], sv["kw"], ones512, s, tm)
        grads["q_norm_w"][l] = dqw.reshape(N_HEADS, HEAD_DIM).sum(0)
        grads["k_norm_w"][l] = dkw.reshape(N_KV, HEAD_DIM).sum(0)
        dhq, dzf, dzb, dhi, dhg, dgw, dlb_l, *rode = _hgrn_bwd(sv["proj"], lbs[l], sv["gw"], sv["osum"],
                                                               sv["states"], dy_hgrn, nb, s,
                                                               plan.ride("hgrn_bwd", l, grads))
        plan.done("hgrn_bwd", l, rode)
        grads["hgrn_gnorm_w"][l] = dgw.reshape(nb * D_HGRN // HEAD_DIM, HEAD_DIM).sum(0)
        dlb[l] = dlb_l.sum(0)
        dab, ddww, ddwb, dlnw, dlnb, dpww, dpwb = _conv_bwd(
            sv["proj"], dy_conv, sv["dww"], _row(p["conv_dw_b"][l]), _row(p["conv_ln_w"][l]),
            _row(p["conv_ln_b"][l]), sv["pww"], nb, s)
        grads["conv_dw_w"][l] = ddww[:CONV_W]
        grads["conv_dw_b"][l] = ddwb[0]
        grads["conv_ln_w"][l] = dlnw[0]
        grads["conv_ln_b"][l] = dlnb[0]
        grads["conv_pw_w"][l] = dpww
        grads["conv_pw_b"][l] = dpwb[0]
        pieces = [dqkv, dhq, dzf, dzb, dhi, dhg, dab]
        grads["w_in"][l] = _dw_in(sv["h0"], pieces, tm)
        dcur, dnw = _proj_bwd(pieces, wb[l]["w_in"], sv["x"], _row(p["mix_norm_w"][l]), dx1, tm)
        grads["mix_norm_w"][l] = dnw[0]

    dlog = _lower_bounds_bwd(p["hgrn_lb_logits"].reshape(DEPTH, 2 * D_HGRN),
                             jnp.stack(dlb).reshape(DEPTH, 2 * D_HGRN))
    out = {k: (v if k in BIG_AXIS else jnp.stack(v)) for k, v in grads.items() if k != "hgrn_lb_logits"}
    out["hgrn_lb_logits"] = dlog.reshape(DEPTH, 2, D_HGRN)
    return loss, dcur.reshape(nb, s, d), out


BIG_AXIS = {"w_in": 2, "w_out": 1, "w_gate": 2, "w_up": 2, "w_down": 1}
SMALL_SHARD_AXIS = {"hgrn_lb_logits": 2, "conv_dw_w": 2, "conv_pw_w": 1}
WEIGHTS = ("mix_norm_w", "w_in", "q_norm_w", "k_norm_w", "hgrn_lb_logits", "hgrn_gnorm_w", "conv_dw_w",
           "conv_dw_b", "conv_ln_w", "conv_ln_b", "conv_pw_w", "conv_pw_b", "attn_out_norm_w",
           "conv_out_norm_w", "w_out", "ffn_norm_w", "w_gate", "w_up", "w_down")
SMALL = tuple(n for n in WEIGHTS if n not in BIG_AXIS)


def _my_index():
    return 4 * lax.axis_index("x") + 2 * lax.axis_index("y") + lax.axis_index("c")


class _RidePlan:
    def __init__(self, srcs, gather):
        self.srcs = list(srcs)
        self.n = len(self.srcs)
        self.gather = list(gather) if isinstance(gather, (list, tuple)) else [gather] * self.n
        any_spec = pl.BlockSpec(memory_space=pl.ANY)
        self.in_specs = [any_spec] * self.n
        self.out_specs = [any_spec] * self.n
        self.out_shape = [jax.ShapeDtypeStruct(((N_DEV,) + s.shape) if g else s.shape, s.dtype)
                          for s, g in zip(self.srcs, self.gather)]
        npeer = N_DEV - 1
        self.scratch = [pltpu.SemaphoreType.DMA((self.n * npeer,)), pltpu.SemaphoreType.DMA((self.n * npeer,)),
                        pltpu.SemaphoreType.DMA((self.n,))] if self.n else []

    def copies(self, src_refs, out_refs, sems):
        if not self.n:
            return [], [], []
        send_sems, recv_sems, local_sems = sems
        npeer = N_DEV - 1
        x, y, c = lax.axis_index("x"), lax.axis_index("y"), lax.axis_index("c")
        me = 4 * x + 2 * y + c
        locals_, sends, recvs = [], [], []
        for a in range(self.n):
            src_ref, out_ref = src_refs[a], out_refs[a]

            def rows_for(j, src_ref=src_ref, gather=self.gather[a]):
                return src_ref if gather else src_ref.at[j]

            locals_.append(pltpu.make_async_copy(rows_for(me), out_ref.at[me], local_sems.at[a]))
            for k in range(1, N_DEV):
                px = (1 - x) if (k & 4) else x
                py = (1 - y) if (k & 2) else y
                pc = (1 - c) if (k & 1) else c
                pidx = 4 * px + 2 * py + pc
                common = dict(send_sem=send_sems.at[a * npeer + k - 1], recv_sem=recv_sems.at[a * npeer + k - 1],
                              device_id=(px, py, pc), device_id_type=pl.DeviceIdType.MESH)
                sends.append(pltpu.make_async_remote_copy(src_ref=rows_for(pidx), dst_ref=out_ref.at[me], **common))
                recvs.append(pltpu.make_async_remote_copy(src_ref=rows_for(pidx), dst_ref=out_ref.at[pidx],
                                                          **common))
        return locals_, sends, recvs


def _ride_plan(ride):
    return _RidePlan(*ride) if ride else _RidePlan([], True)


def _ride_start(copies, when=None):
    locals_, sends, _ = copies

    def go():
        for cp in locals_ + sends:
            cp.start()

    if locals_:
        go() if when is None else pl.when(when)(go)


def _ride_wait(copies, when=None):
    locals_, sends, recvs = copies

    def go():
        for cp in recvs:
            cp.wait_recv()
        for cp in sends:
            cp.wait_send()
        for cp in locals_:
            cp.wait()

    if locals_:
        go() if when is None else pl.when(when)(go)


def _exchange(srcs, gather, name):
    rd = _RidePlan(srcs, gather)

    def body(*refs):
        copies = rd.copies(refs[:rd.n], refs[rd.n:2 * rd.n], refs[2 * rd.n:])
        _ride_start(copies)
        _ride_wait(copies)

    return pl.pallas_call(body, name=name, in_specs=rd.in_specs, out_specs=rd.out_specs,
                          out_shape=rd.out_shape, scratch_shapes=rd.scratch)(*srcs)


def _lane_group(n):
    g = 1
    while (g * n) % LANES:
        g += 1
    return g


def _cols_to_natural(gathered, name):
    _, k, n = gathered.shape
    grp = _lane_group(n)
    place = jnp.stack([jnp.asarray(np.eye(n, grp * n, k=i * n), BF16) for i in range(grp)])

    def body(g_ref, p_ref, o_ref):
        acc = None
        for i in range(grp):
            part = _dot(g_ref[i], p_ref[i])
            acc = part if acc is None else acc + part
        o_ref[...] = acc.astype(BF16)

    return pl.pallas_call(
        body, name=name, grid=(N_DEV // grp,),
        in_specs=[pl.BlockSpec((grp, k, n), lambda j: (j, 0, 0)),
                  pl.BlockSpec((grp, n, grp * n), lambda j: (0, 0, 0))],
        out_specs=pl.BlockSpec((k, grp * n), lambda j: (0, j)),
        out_shape=jax.ShapeDtypeStruct((k, N_DEV * n), BF16),
        compiler_params=_cparams(("parallel",)),
    )(gathered, place)


def _natural_to_cols(dw, name):
    k, n8 = dw.shape
    n = n8 // N_DEV
    grp = _lane_group(n)
    pick = jnp.stack([jnp.asarray(np.eye(grp * n, n, k=-i * n), BF16) for i in range(grp)])

    def body(d_ref, p_ref, o_ref):
        xb = d_ref[...].astype(BF16)
        for i in range(grp):
            o_ref[i] = _dot(xb, p_ref[i]).astype(BF16)

    return pl.pallas_call(
        body, name=name, grid=(N_DEV // grp,),
        in_specs=[pl.BlockSpec((k, grp * n), lambda j: (0, j)),
                  pl.BlockSpec((grp, grp * n, n), lambda j: (0, 0, 0))],
        out_specs=pl.BlockSpec((grp, k, n), lambda j: (j, 0, 0)),
        out_shape=jax.ShapeDtypeStruct((N_DEV, k, n), BF16),
        compiler_params=_cparams(("parallel",)),
    )(dw, pick)


def _adamw_math(w, g, m, v):
    m = ADAM_B1 * m + (1.0 - ADAM_B1) * g
    v = ADAM_B2 * v + (1.0 - ADAM_B2) * (g * g)
    m_hat = m / (1.0 - ADAM_B1 ** ADAM_STEP)
    v_hat = v / (1.0 - ADAM_B2 ** ADAM_STEP)
    delta = -ADAM_LR * (m_hat / (jnp.sqrt(v_hat) + ADAM_EPS) + ADAM_WD * w)
    return delta, m, v


def _sum_adamw(parts, w, m, v, name):
    _, k, n = w.shape
    tk = k
    for cand in (256, 176, 128):
        if k % cand == 0:
            tk = cand
            break

    def body(*refs):
        p_refs = refs[:DEPTH]
        w_ref, m_ref, v_ref, g_ref, d_ref, mo_ref, vo_ref = refs[DEPTH:]
        for l in range(DEPTH):
            @pl.when(pl.program_id(0) == l)
            def _(p_ref=p_refs[l]):
                g = p_ref[0].astype(F32)
                for i in range(1, N_DEV):
                    g = g + p_ref[i].astype(F32)
                g_ref[...] = g
                d_ref[...], mo_ref[...], vo_ref[...] = _adamw_math(w_ref[...], g, m_ref[...], v_ref[...])

    row = pl.BlockSpec((None, tk, n), lambda l, i: (l, i, 0))
    shp = jax.ShapeDtypeStruct(w.shape, F32)
    return pl.pallas_call(
        body, name=name, grid=(DEPTH, k // tk),
        in_specs=[pl.BlockSpec((N_DEV, tk, n), lambda l, i: (0, i, 0))] * DEPTH + [row, row, row],
        out_specs=[row, row, row, row],
        out_shape=[shp, shp, shp, shp],
        compiler_params=_cparams(("parallel", "parallel")),
    )(*parts, w, m, v)


def _sum8(parts, name):
    r = parts.shape[1]

    def body(p_ref, g_ref):
        g = p_ref[0]
        for i in range(1, N_DEV):
            g = g + p_ref[i]
        g_ref[...] = g

    return pl.pallas_call(body, name=name, out_shape=jax.ShapeDtypeStruct((r, LANES), F32))(parts)


def _adamw(w, g, m, v):
    def body(w_ref, g_ref, m_ref, v_ref, d_ref, mo_ref, vo_ref):
        d_ref[...], mo_ref[...], vo_ref[...] = _adamw_math(w_ref[...], g_ref[...], m_ref[...], v_ref[...])

    shp = jax.ShapeDtypeStruct(w.shape, F32)
    return pl.pallas_call(body, name="adamw_small", out_shape=[shp, shp, shp])(w, g, m, v)


def _pack(arrays, dtype, row_multiple):
    flat = jnp.concatenate([a.reshape(-1).astype(dtype) for a in arrays])
    n = flat.shape[0]
    unit = row_multiple * LANES
    total = -(-n // unit) * unit
    return jnp.pad(flat, (0, total - n)).reshape(total // LANES, LANES)


def _unpack(flat2d, shapes, lead=()):
    flat = flat2d.reshape(lead + (-1,))
    out, off = [], 0
    for shp in shapes:
        n = int(np.prod(shp))
        out.append(flat[..., off:off + n].reshape(lead + tuple(shp)))
        off += n
    return out


def _shard_to_rows(full, axis):
    shp = full.shape
    k = shp[axis] // N_DEV
    r = full.reshape(shp[:axis] + (N_DEV, k) + shp[axis + 1:])
    return jnp.moveaxis(r, axis, 0)


def _rows_to_full(rows, axis):
    r = jnp.moveaxis(rows, 0, axis)
    shp = r.shape
    return r.reshape(shp[:axis] + (shp[axis] * shp[axis + 1],) + shp[axis + 2:])


def kernel(x, mix_norm_w, w_in, q_norm_w, k_norm_w, hgrn_lb_logits, hgrn_gnorm_w, conv_dw_w, conv_dw_b, conv_ln_w, conv_ln_b, conv_pw_w, conv_pw_b, attn_out_norm_w, conv_out_norm_w, w_out, ffn_norm_w, w_gate, w_up, w_down, loss_target, m_mix_norm_w, m_w_in, m_q_norm_w, m_k_norm_w, m_hgrn_lb_logits, m_hgrn_gnorm_w, m_conv_dw_w, m_conv_dw_b, m_conv_ln_w, m_conv_ln_b, m_conv_pw_w, m_conv_pw_b, m_attn_out_norm_w, m_conv_out_norm_w, m_w_out, m_ffn_norm_w, m_w_gate, m_w_up, m_w_down, v_mix_norm_w, v_w_in, v_q_norm_w, v_k_norm_w, v_hgrn_lb_logits, v_hgrn_gnorm_w, v_conv_dw_w, v_conv_dw_b, v_conv_ln_w, v_conv_ln_b, v_conv_pw_w, v_conv_pw_b, v_attn_out_norm_w, v_conv_out_norm_w, v_w_out, v_ffn_norm_w, v_w_gate, v_w_up, v_w_down):
    w_loc = dict(zip(WEIGHTS, (mix_norm_w, w_in, q_norm_w, k_norm_w, hgrn_lb_logits, hgrn_gnorm_w, conv_dw_w,
                               conv_dw_b, conv_ln_w, conv_ln_b, conv_pw_w, conv_pw_b, attn_out_norm_w,
                               conv_out_norm_w, w_out, ffn_norm_w, w_gate, w_up, w_down)))
    m_loc = dict(zip(WEIGHTS, (m_mix_norm_w, m_w_in, m_q_norm_w, m_k_norm_w, m_hgrn_lb_logits, m_hgrn_gnorm_w,
                               m_conv_dw_w, m_conv_dw_b, m_conv_ln_w, m_conv_ln_b, m_conv_pw_w, m_conv_pw_b,
                               m_attn_out_norm_w, m_conv_out_norm_w, m_w_out, m_ffn_norm_w, m_w_gate, m_w_up,
                               m_w_down)))
    v_loc = dict(zip(WEIGHTS, (v_mix_norm_w, v_w_in, v_q_norm_w, v_k_norm_w, v_hgrn_lb_logits, v_hgrn_gnorm_w,
                               v_conv_dw_w, v_conv_dw_b, v_conv_ln_w, v_conv_ln_b, v_conv_pw_w, v_conv_pw_b,
                               v_attn_out_norm_w, v_conv_out_norm_w, v_w_out, v_ffn_norm_w, v_w_gate, v_w_up,
                               v_w_down)))
    me = _my_index()
    big = tuple(BIG_AXIS)
    sms = tuple(SMALL_SHARD_AXIS)

    sm_shapes = [w_loc[n].shape for n in sms]
    got_s = _exchange([_pack([w_loc[n] for n in sms], F32, 8)], True, "gather_small_params")[0]
    p_full = {n: w_loc[n] for n in SMALL if n not in SMALL_SHARD_AXIS}
    for n, a in zip(sms, _unpack(got_s, sm_shapes, (N_DEV,))):
        p_full[n] = _rows_to_full(a, SMALL_SHARD_AXIS[n])

    def natural(n, gathered):
        if n == "w_in":
            return _cols_to_natural(gathered, "relayout_" + n)
        if n == "w_out":
            return gathered.reshape(-1, gathered.shape[-1])
        return gathered

    def to_send(n, gl):
        if n == "w_in":
            return _natural_to_cols(gl, "split_d" + n)
        if n == "w_out":
            return gl.reshape(N_DEV, gl.shape[0] // N_DEV, gl.shape[1]).astype(BF16)
        return gl

    class StepPlan:
        def __init__(self):
            self.w = [dict() for _ in range(DEPTH)]
            self.parts = [dict() for _ in range(DEPTH)]
            self.pending = {}
            got = _exchange([w_loc["w_in"][0].astype(BF16)], True, "gather_w_in")
            self.w[0]["w_in"] = natural("w_in", got[0])

        def ride(self, kernel_name, l, grads=None):
            want = []
            if kernel_name == "attn_fwd":
                want = [("w_out", l), ("w_gate", l)]
            elif kernel_name == "hgrn_fwd":
                want = [("w_up", l), ("w_down", l)]
            elif kernel_name == "ffn_fwd" and l + 1 < DEPTH:
                want = [("w_in", l + 1)]
            elif kernel_name == "ffn_bwd" and l + 1 < DEPTH:
                want = [("w_gate", l + 1), ("w_up", l + 1)]
            elif kernel_name == "attn_bwd" and l + 1 < DEPTH:
                want = [("w_in", l + 1), ("w_out", l + 1), ("w_down", l + 1)]
                if l == 0:
                    want += [("w_out", 0)]
            elif kernel_name == "hgrn_bwd" and l == 0:
                want = [(n, 0) for n in ("w_gate", "w_up", "w_down")]
            if not want:
                return None
            self.pending[(kernel_name, l)] = want
            if grads is None:
                return [w_loc[n][wl].astype(BF16) for n, wl in want], True
            return [to_send(n, grads[n][wl]) for n, wl in want], False

        def done(self, kernel_name, l, outs):
            want = self.pending.pop((kernel_name, l), [])
            for (n, wl), out in zip(want, outs):
                if kernel_name.endswith("_fwd"):
                    self.w[wl][n] = natural(n, out)
                else:
                    self.parts[wl][n] = out

    plan = StepPlan()
    loss_part, grad_x, g = _local_step(x, loss_target, p_full, plan)
    loss = lax.psum(loss_part[0, 0], MESH_AXES)

    pw = g["conv_pw_w"]
    k_pw = w_loc["conv_pw_w"].shape[1]
    pw_send = jnp.moveaxis(pw.reshape(DEPTH, N_DEV, k_pw, pw.shape[-1]), 1, 0).reshape(N_DEV, -1, LANES)
    gathered_small = [n for n in SMALL if n != "conv_pw_w"]
    small_shapes = [g[n].shape for n in gathered_small]
    din_parts, small_parts, pw_parts = _exchange(
        [to_send("w_in", g["w_in"][0]), _pack([g[n] for n in gathered_small], F32, 8), pw_send],
        [False, True, False], "exchange_last_grads")
    plan.parts[0]["w_in"] = din_parts
    big_out = {n: _sum_adamw([plan.parts[l][n] for l in range(DEPTH)], w_loc[n], m_loc[n], v_loc[n],
                             "sum_adamw_" + n) for n in big}

    g_small = dict(zip(gathered_small, _unpack(_sum8(small_parts, "sum_small_grads"), small_shapes)))
    g_small["conv_pw_w"] = _sum8(pw_parts, "sum_conv_pw_grads").reshape(w_loc["conv_pw_w"].shape)
    for n in sms:
        if n == "conv_pw_w":
            continue
        ax = SMALL_SHARD_AXIS[n]
        k = w_loc[n].shape[ax]
        g_small[n] = lax.dynamic_slice_in_dim(g_small[n], me * k, k, axis=ax)
    loc_shapes = [w_loc[n].shape for n in SMALL]
    packed = [_pack([d[n] for n in SMALL], F32, 8) for d in (w_loc, g_small, m_loc, v_loc)]
    res = _adamw(*packed)
    small_out = [g_small] + [dict(zip(SMALL, _unpack(r, loc_shapes))) for r in res]

    def pick(i, n):
        return big_out[n][i] if n in BIG_AXIS else small_out[i][n]

    return (loss, grad_x) + tuple(pick(i, n) for i in range(4) for n in WEIGHTS)
```

```python
import functools

import jax
import jax.numpy as jnp
import numpy as np
from jax import lax
from jax.experimental import pallas as pl
from jax.experimental.pallas import tpu as pltpu

F32 = jnp.float32
BF16 = jnp.bfloat16

D_MODEL = 1024
D_ATTN = 512
D_HGRN = 256
D_CONV = 256
HEAD_DIM = 64
N_HEADS = 8
N_KV = 2
GRID_W = 64
ROPE_THETA = 10000.0
CHUNK = 64
F_MIN = 1e-6
CONV_W = 31
CONV_PAD = 15
D_FF = 2816
D_PROJ = 2560
EPS = 1e-6
LN_EPS = 1e-5
DEPTH = 2
ADAM_LR = 0.001
ADAM_B1 = 0.9
ADAM_B2 = 0.999
ADAM_EPS = 1e-08
ADAM_WD = 0.01
ADAM_STEP = 10
N_DEV = 8
MESH_AXES = ("x", "y", "c")

COL_HQ, COL_ZFW, COL_ZBW, COL_HI, COL_HG = 6, 8, 10, 12, 14
COL_CA, COL_CB = 8, 9

LANES = 128
VMEM_LIMIT_MB = 56


def _cparams(dims=None):
    return pltpu.CompilerParams(dimension_semantics=dims, vmem_limit_bytes=VMEM_LIMIT_MB * 2 ** 20)


def _dot(a, b):
    return jnp.dot(a, b, preferred_element_type=F32)


def _dot_nt(a, b):
    return lax.dot_general(a, b, (((1,), (1,)), ((), ())), preferred_element_type=F32)


def _dot_tn(a, b):
    return lax.dot_general(a, b, (((0,), (0,)), ((), ())), preferred_element_type=F32)


def _split_bf16(x, parts):
    out = []
    r = x
    for _ in range(parts):
        p = r.astype(BF16)
        out.append(p)
        r = r - p.astype(F32)
    return out


def _dot_precise(x, m_bf16, parts=3):
    acc = None
    for p in _split_bf16(x, parts):
        t = _dot(p, m_bf16)
        acc = t if acc is None else acc + t
    return acc


def _block_ones(width, group):
    i = np.arange(width)
    return jnp.asarray((i[:, None] // group) == (i[None, :] // group), dtype=BF16)


def _sigmoid(x):
    return 1.0 / (1.0 + jnp.exp(-x))


def _rot(x):
    w = x.shape[1]
    lane = lax.broadcasted_iota(jnp.int32, x.shape, 1)
    first = (lane % 32) < 16
    return jnp.where(first, -pltpu.roll(x, w - 16, 1), pltpu.roll(x, 16, 1))


def _rope(x, cos, sin):
    return x * cos + _rot(x) * sin


def _rope_t(dy, cos, sin):
    return dy * cos - _rot(dy * sin)


def _row(v):
    return v.reshape(1, -1)


def _rms_proj(x, wn, wt, tm):
    t, d = x.shape
    n = wt.shape[0]

    def body(x_ref, wn_ref, w_ref, h_ref, y_ref):
        xv = x_ref[...]
        r = lax.rsqrt(jnp.mean(xv * xv, axis=-1, keepdims=True) + EPS)
        h = (xv * r * wn_ref[...]).astype(BF16)
        h_ref[...] = h
        y_ref[...] = _dot_nt(h, w_ref[...])

    return pl.pallas_call(
        body, name="rms_proj", grid=(t // tm,),
        in_specs=[pl.BlockSpec((tm, d), lambda i: (i, 0)),
                  pl.BlockSpec((1, d), lambda i: (0, 0)),
                  pl.BlockSpec((n, d), lambda i: (0, 0))],
        out_specs=[pl.BlockSpec((tm, d), lambda i: (i, 0)),
                   pl.BlockSpec((tm, n), lambda i: (i, 0))],
        out_shape=[jax.ShapeDtypeStruct((t, d), BF16), jax.ShapeDtypeStruct((t, n), F32)],
        compiler_params=_cparams(("parallel",)),
    )(x, wn, wt)


def _rms_bwd(dh, x, wn):
    r = lax.rsqrt(jnp.mean(x * x, axis=-1, keepdims=True) + EPS)
    g = dh * wn
    dx = r * (g - x * (r * r) * jnp.mean(g * x, axis=-1, keepdims=True))
    return dx, dh * x * r


def _proj_bwd(pieces, wt, x, wn, dres, tm):
    t = x.shape[0]
    d = x.shape[1]
    n = wt.shape[0]
    widths = [p.shape[1] for p in pieces]
    offs = [sum(widths[:i]) for i in range(len(widths))]
    assert sum(widths) == n
    npc = len(pieces)

    def body(*refs):
        p_refs = refs[:npc]
        w_ref, x_ref, wn_ref, dr_ref, dx_ref, dwn_ref = refs[npc:]
        dh = None
        for p_ref, o, wd in zip(p_refs, offs, widths):
            part = _dot(p_ref[...], w_ref[o:o + wd, :])
            dh = part if dh is None else dh + part
        dx, dwn = _rms_bwd(dh, x_ref[...], wn_ref[...])
        dx_ref[...] = dr_ref[...] + dx

        @pl.when(pl.program_id(0) == 0)
        def _():
            dwn_ref[...] = jnp.zeros_like(dwn_ref)

        dwn_ref[...] += jnp.sum(dwn, axis=0, keepdims=True)

    return pl.pallas_call(
        body, name="proj_bwd", grid=(t // tm,),
        in_specs=[pl.BlockSpec((tm, wd), lambda i: (i, 0)) for wd in widths]
        + [pl.BlockSpec((n, d), lambda i: (0, 0)),
           pl.BlockSpec((tm, d), lambda i: (i, 0)),
           pl.BlockSpec((1, d), lambda i: (0, 0)),
           pl.BlockSpec((tm, d), lambda i: (i, 0))],
        out_specs=[pl.BlockSpec((tm, d), lambda i: (i, 0)),
                   pl.BlockSpec((1, d), lambda i: (0, 0))],
        out_shape=[jax.ShapeDtypeStruct((t, d), F32), jax.ShapeDtypeStruct((1, d), F32)],
        compiler_params=_cparams(("arbitrary",)),
    )(*pieces, wt, x, wn, dres)


def _dw_in(h0, pieces, tm):
    t, k = h0.shape
    widths = [p.shape[1] for p in pieces]
    offs = [sum(widths[:i]) for i in range(len(widths))]
    n = sum(widths)
    npc = len(pieces)

    def body(*refs):
        h_ref = refs[0]
        p_refs = refs[1:1 + npc]
        o_ref, acc_ref = refs[1 + npc:]
        i = pl.program_id(0)

        @pl.when(i == 0)
        def _():
            acc_ref[...] = jnp.zeros_like(acc_ref)

        h = h_ref[...]
        for p_ref, o, wd in zip(p_refs, offs, widths):
            acc_ref[o:o + wd, :] += _dot_tn(p_ref[...], h)

        @pl.when(i == t // tm - 1)
        def _():
            o_ref[...] = acc_ref[...].astype(BF16)

    return pl.pallas_call(
        body, name="dw_in", grid=(t // tm,),
        in_specs=[pl.BlockSpec((tm, k), lambda i: (i, 0))]
        + [pl.BlockSpec((tm, wd), lambda i: (i, 0)) for wd in widths],
        out_specs=pl.BlockSpec((n, k), lambda i: (0, 0)),
        out_shape=jax.ShapeDtypeStruct((n, k), BF16),
        scratch_shapes=[pltpu.VMEM((n, k), F32)],
        compiler_params=_cparams(("arbitrary",)),
    )(h0, *pieces)


def _mm_tn(a, b, tn, name, tm):
    t, k = a.shape
    n = b.shape[1]

    def body(a_ref, b_ref, o_ref):
        @pl.when(pl.program_id(1) == 0)
        def _():
            o_ref[...] = jnp.zeros_like(o_ref)

        o_ref[...] += _dot_tn(a_ref[...].astype(BF16), b_ref[...].astype(BF16))

    return pl.pallas_call(
        body, name=name, grid=(n // tn, t // tm),
        in_specs=[pl.BlockSpec((tm, k), lambda j, i: (i, 0)),
                  pl.BlockSpec((tm, tn), lambda j, i: (i, j))],
        out_specs=pl.BlockSpec((k, tn), lambda j, i: (0, j)),
        out_shape=jax.ShapeDtypeStruct((k, n), F32),
        compiler_params=_cparams(("parallel", "arbitrary")),
    )(a, b)


def _dw_ff(blocked, flat, name, tm):
    t, dm = flat.shape
    fb = blocked.shape[2]
    out_blk = (N_DEV, fb, dm)

    def body(b_ref, f_ref, o_ref, acc_ref):
        i = pl.program_id(0)

        @pl.when(i == 0)
        def _():
            acc_ref[...] = jnp.zeros_like(acc_ref)

        fv = f_ref[...].astype(BF16)
        for j in range(N_DEV):
            acc_ref[j] += _dot_tn(b_ref[j], fv)

        @pl.when(i == t // tm - 1)
        def _():
            o_ref[...] = acc_ref[...].astype(BF16)

    return pl.pallas_call(
        body, name=name, grid=(t // tm,),
        in_specs=[pl.BlockSpec((N_DEV, tm, fb), lambda i: (0, i, 0)),
                  pl.BlockSpec((tm, dm), lambda i: (i, 0))],
        out_specs=pl.BlockSpec(out_blk, lambda i: (0, 0, 0)),
        out_shape=jax.ShapeDtypeStruct(out_blk, BF16),
        scratch_shapes=[pltpu.VMEM(out_blk, F32)],
        compiler_params=_cparams(("arbitrary",)),
    )(blocked, flat)


def _rope_tables(s):
    rows = s // GRID_W
    row_id = jnp.repeat(jnp.arange(rows, dtype=F32), GRID_W)
    col_id = jnp.tile(jnp.arange(GRID_W, dtype=F32), rows)
    half = HEAD_DIM // 2
    inv_freq = ROPE_THETA ** (-jnp.arange(0, half, 2, dtype=F32) / half)
    ang_r = row_id[:, None] * inv_freq[None, :]
    ang_c = col_id[:, None] * inv_freq[None, :]
    ang = jnp.concatenate([ang_r, ang_r, ang_c, ang_c], axis=-1)
    cos, sin = jnp.cos(ang), jnp.sin(ang)
    return jnp.tile(cos, (1, N_HEADS)), jnp.tile(sin, (1, N_HEADS))


def _head_rms(x, w, ones):
    r = lax.rsqrt(_dot_precise(x * x, ones, 2) * (1.0 / HEAD_DIM) + EPS)
    return x * r * w, r


def _dup_half(x, kv):
    lane = lax.broadcasted_iota(jnp.int32, x.shape, 1)
    sel = (lane < 64) if kv == 0 else (lane >= 64)
    return jnp.where(sel, x, pltpu.roll(x, 64, 1))


def _qkv_prep(proj, cosq, sinq, qw, kw, ones, s, tm):
    t = proj.shape[0]
    ns = s // tm

    def body(p_ref, cos_ref, sin_ref, qw_ref, kw_ref, ones_ref, q_out, kd_out, vd_out, kdt_out, vdt_out):
        cos = cos_ref[...]
        sin = sin_ref[...]
        ones_m = ones_ref[...]
        qn, _ = _head_rms(p_ref[:, 0:512], qw_ref[...], ones_m)
        q_out[...] = (_rope(qn, cos, sin) * (HEAD_DIM ** -0.5)).astype(BF16)
        kn, _ = _head_rms(p_ref[:, 512:640], kw_ref[...], ones_m[0:128, 0:128])
        kr = _rope(kn, cos[:, 0:128], sin[:, 0:128])
        v = p_ref[:, 640:768]
        for kv in range(N_KV):
            kd = _dup_half(kr, kv)
            vd = _dup_half(v, kv)
            kd_out[kv] = kd.astype(BF16)
            vd_out[kv] = vd.astype(BF16)
            kdt_out[kv] = kd.T.astype(BF16)
            vdt_out[kv] = vd.T.astype(BF16)

    return pl.pallas_call(
        body, name="qkv_prep", grid=(t // tm,),
        in_specs=[pl.BlockSpec((tm, 768), lambda i: (i, 0)),
                  pl.BlockSpec((tm, 512), lambda i: (i % ns, 0)),
                  pl.BlockSpec((tm, 512), lambda i: (i % ns, 0)),
                  pl.BlockSpec((1, 512), lambda i: (0, 0)),
                  pl.BlockSpec((1, 128), lambda i: (0, 0)),
                  pl.BlockSpec((512, 512), lambda i: (0, 0))],
        out_specs=[pl.BlockSpec((tm, 512), lambda i: (i, 0)),
                   pl.BlockSpec((N_KV, tm, 128), lambda i: (0, i, 0)),
                   pl.BlockSpec((N_KV, tm, 128), lambda i: (0, i, 0)),
                   pl.BlockSpec((N_KV, 128, tm), lambda i: (0, 0, i)),
                   pl.BlockSpec((N_KV, 128, tm), lambda i: (0, 0, i))],
        out_shape=[jax.ShapeDtypeStruct((t, 512), BF16),
                   jax.ShapeDtypeStruct((N_KV, t, 128), BF16),
                   jax.ShapeDtypeStruct((N_KV, t, 128), BF16),
                   jax.ShapeDtypeStruct((N_KV, 128, t), BF16),
                   jax.ShapeDtypeStruct((N_KV, 128, t), BF16)],
        compiler_params=_cparams(("parallel",)),
    )(proj, cosq, sinq, qw, kw, ones)


def _qkv_bwd(proj, dq, dkd, dvd, cosq, sinq, qw, kw, ones, s, tm):
    t = proj.shape[0]
    ns = s // tm

    def body(p_ref, dq_ref, dkd_ref, dvd_ref, cos_ref, sin_ref, qw_ref, kw_ref, ones_ref,
             out_ref, dqw_ref, dkw_ref):
        cos = cos_ref[...]
        sin = sin_ref[...]
        ones_m = ones_ref[...]
        ones_k = ones_m[0:128, 0:128]

        def norm_bwd(x, w, dn, om):
            r = lax.rsqrt(_dot_precise(x * x, om, 2) * (1.0 / HEAD_DIM) + EPS)
            g = dn * w
            dx = r * (g - x * (r * r) * (_dot_precise(g * x, om, 2) * (1.0 / HEAD_DIM)))
            return dx, jnp.sum(dn * x * r, axis=0, keepdims=True)

        q = p_ref[:, 0:512]
        dqn = _rope_t(dq_ref[...], cos, sin) * (HEAD_DIM ** -0.5)
        dq_raw, dqw = norm_bwd(q, qw_ref[...], dqn, ones_m)
        out_ref[:, 0:512] = dq_raw.astype(BF16)

        lane = lax.broadcasted_iota(jnp.int32, (tm, 128), 1)

        def fold(ref):
            a0 = ref[0]
            a1 = ref[1]
            f0 = a0 + pltpu.roll(a0, 64, 1)
            f1 = a1 + pltpu.roll(a1, 64, 1)
            return jnp.where(lane < 64, f0, f1)

        k = p_ref[:, 512:640]
        dkn = _rope_t(fold(dkd_ref), cos[:, 0:128], sin[:, 0:128])
        dk_raw, dkw = norm_bwd(k, kw_ref[...], dkn, ones_k)
        out_ref[:, 512:640] = dk_raw.astype(BF16)
        out_ref[:, 640:768] = fold(dvd_ref).astype(BF16)

        @pl.when(pl.program_id(0) == 0)
        def _():
            dqw_ref[...] = jnp.zeros_like(dqw_ref)
            dkw_ref[...] = jnp.zeros_like(dkw_ref)

        dqw_ref[...] += dqw
        dkw_ref[...] += dkw

    return pl.pallas_call(
        body, name="qkv_bwd", grid=(t // tm,),
        in_specs=[pl.BlockSpec((tm, 768), lambda i: (i, 0)),
                  pl.BlockSpec((tm, 512), lambda i: (i, 0)),
                  pl.BlockSpec((N_KV, tm, 128), lambda i: (0, i, 0)),
                  pl.BlockSpec((N_KV, tm, 128), lambda i: (0, i, 0)),
                  pl.BlockSpec((tm, 512), lambda i: (i % ns, 0)),
                  pl.BlockSpec((tm, 512), lambda i: (i % ns, 0)),
                  pl.BlockSpec((1, 512), lambda i: (0, 0)),
                  pl.BlockSpec((1, 128), lambda i: (0, 0)),
                  pl.BlockSpec((512, 512), lambda i: (0, 0))],
        out_specs=[pl.BlockSpec((tm, 768), lambda i: (i, 0)),
                   pl.BlockSpec((1, 512), lambda i: (0, 0)),
                   pl.BlockSpec((1, 128), lambda i: (0, 0))],
        out_shape=[jax.ShapeDtypeStruct((t, 768), BF16),
                   jax.ShapeDtypeStruct((1, 512), F32),
                   jax.ShapeDtypeStruct((1, 128), F32)],
        compiler_params=_cparams(("arbitrary",)),
    )(proj, dq, dkd, dvd, cosq, sinq, qw, kw, ones)


def _grid_step_id(grid):
    idx = pl.program_id(0)
    for ax in range(1, len(grid)):
        idx = idx * grid[ax] + pl.program_id(ax)
    return idx


def _attn_fwd(q, kd, vdt, nb, s, tq, ride=None):
    t = q.shape[0]
    nq = s // tq
    rd = _ride_plan(ride)
    grid = (nb, N_HEADS // 2, nq)
    nsteps = nb * (N_HEADS // 2) * nq

    def body(*refs):
        q_ref, k_ref, vt_ref = refs[:3]
        o_ref, lse_ref = refs[3 + rd.n:5 + rd.n]
        copies = rd.copies(refs[3:3 + rd.n], refs[5 + rd.n:5 + 2 * rd.n], refs[5 + 2 * rd.n:])
        step_id = _grid_step_id(grid)
        _ride_start(copies, step_id == 0)
        qv = q_ref[...].astype(F32)
        lane = lax.broadcasted_iota(jnp.int32, qv.shape, 1)
        k = k_ref[0]
        vt = vt_ref[0]
        outs = []
        scores = [_dot_nt(k, jnp.where((lane < 64) if half == 0 else (lane >= 64), qv, 0.0).astype(BF16))
                  for half in range(2)]
        for half in range(2):
            st = scores[half]
            m = jnp.max(st, axis=0, keepdims=True)
            p = jnp.exp(st - m)
            l = jnp.sum(p, axis=0, keepdims=True)
            ot = _dot(vt, p.astype(BF16)) / l
            lse_ref[0, half] = m + jnp.log(l)
            outs.append(ot)
        row = lax.broadcasted_iota(jnp.int32, outs[0].shape, 0)
        o_ref[...] = jnp.where(row < 64, outs[0], outs[1]).T
        _ride_wait(copies, step_id == nsteps - 1)

    return pl.pallas_call(
        body, name="attn_fwd", grid=grid,
        in_specs=[pl.BlockSpec((tq, 128), lambda b, p, i: (b * nq + i, p)),
                  pl.BlockSpec((1, s, 128), lambda b, p, i: (p // 2, b, 0)),
                  pl.BlockSpec((1, 128, s), lambda b, p, i: (p // 2, 0, b))] + rd.in_specs,
        out_specs=[pl.BlockSpec((tq, 128), lambda b, p, i: (b * nq + i, p)),
                   pl.BlockSpec((1, 2, 1, tq), lambda b, p, i: (b, p, 0, i))] + rd.out_specs,
        out_shape=[jax.ShapeDtypeStruct((t, D_ATTN), F32),
                   jax.ShapeDtypeStruct((nb, N_HEADS, 1, s), F32)] + rd.out_shape,
        scratch_shapes=rd.scratch,
        compiler_params=_cparams(("arbitrary", "arbitrary", "arbitrary")),
    )(q, kd, vdt, *rd.srcs)


def _attn_bwd(q, kd, vd, kdt, o, lse, do, nb, s, tq, ride=None):
    t = q.shape[0]
    nq = s // tq
    ones8 = jnp.ones((8, 128), BF16)
    rd = _ride_plan(ride)
    grid = (nb, N_KV, 2, nq)
    nsteps = nb * N_KV * 2 * nq

    def body(*refs):
        q_ref, k_ref, v_ref, kt_ref, o_ref, lse_ref, do_ref, ones_ref = refs[:8]
        dq_ref, dk_ref, dv_ref = refs[8 + rd.n:11 + rd.n]
        copies = rd.copies(refs[8:8 + rd.n], refs[11 + rd.n:11 + 2 * rd.n], refs[11 + 2 * rd.n:])
        step_id = _grid_step_id(grid)
        _ride_start(copies, step_id == 0)

        @pl.when((pl.program_id(2) == 0) & (pl.program_id(3) == 0))
        def _():
            dk_ref[...] = jnp.zeros_like(dk_ref)
            dv_ref[...] = jnp.zeros_like(dv_ref)

        qv = q_ref[...].astype(F32)
        dov = do_ref[...]
        ov = o_ref[...]
        lane = lax.broadcasted_iota(jnp.int32, qv.shape, 1)
        k = k_ref[0]
        v = v_ref[0]
        kt = kt_ref[0]
        dqs = []
        dk_acc = None
        dv_acc = None
        for half in range(2):
            sel = (lane < 64) if half == 0 else (lane >= 64)
            qh = jnp.where(sel, qv, 0.0).astype(BF16)
            doh = jnp.where(sel, dov, 0.0)
            dob = doh.astype(BF16)
            delta = None
            for part in _split_bf16(doh * ov, 3):
                d8 = _dot_nt(ones_ref[...], part)
                delta = d8 if delta is None else delta + d8
            delta = delta[0:1, :]
            st = _dot_nt(k, qh)
            pt = jnp.exp(st - lse_ref[0, half])
            dpt = _dot_nt(v, dob)
            dst = (pt * (dpt - delta)).astype(BF16)
            dkh = _dot(dst, qh)
            dvh = _dot(pt.astype(BF16), dob)
            dk_acc = dkh if dk_acc is None else dk_acc + dkh
            dv_acc = dvh if dv_acc is None else dv_acc + dvh
            dqs.append(_dot(kt, dst))
        dk_ref[0] += dk_acc
        dv_ref[0] += dv_acc
        row = lax.broadcasted_iota(jnp.int32, dqs[0].shape, 0)
        dq_ref[...] = jnp.where(row < 64, dqs[0], dqs[1]).T
        _ride_wait(copies, step_id == nsteps - 1)

    qmap = lambda b, g, p, i: (b * nq + i, g * 2 + p)
    kvmap = lambda b, g, p, i: (g, b, 0)
    return pl.pallas_call(
        body, name="attn_bwd", grid=grid,
        in_specs=[pl.BlockSpec((tq, 128), qmap),
                  pl.BlockSpec((1, s, 128), kvmap),
                  pl.BlockSpec((1, s, 128), kvmap),
                  pl.BlockSpec((1, 128, s), lambda b, g, p, i: (g, 0, b)),
                  pl.BlockSpec((tq, 128), qmap),
                  pl.BlockSpec((1, 2, 1, tq), lambda b, g, p, i: (b, g * 2 + p, 0, i)),
                  pl.BlockSpec((tq, 128), qmap),
                  pl.BlockSpec((8, 128), lambda b, g, p, i: (0, 0))] + rd.in_specs,
        out_specs=[pl.BlockSpec((tq, 128), qmap),
                   pl.BlockSpec((1, s, 128), kvmap),
                   pl.BlockSpec((1, s, 128), kvmap)] + rd.out_specs,
        out_shape=[jax.ShapeDtypeStruct((t, D_ATTN), F32),
                   jax.ShapeDtypeStruct((N_KV, t, 128), F32),
                   jax.ShapeDtypeStruct((N_KV, t, 128), F32)] + rd.out_shape,
        scratch_shapes=rd.scratch,
        compiler_params=_cparams(("arbitrary", "arbitrary", "arbitrary", "arbitrary")),
    )(q, kd, vd, kdt, o, lse, do, ones8, *rd.srcs)


SUB = 16
N_SUB = CHUNK // SUB


def _tri_mats():
    i = np.arange(CHUNK)
    same = (i[:, None] // SUB) == (i[None, :] // SUB)
    lower = jnp.asarray(same & (i[:, None] >= i[None, :]), dtype=BF16)
    upper = jnp.asarray(same & (i[:, None] <= i[None, :]), dtype=BF16)
    return jnp.stack([lower, upper])


def _running_sum(tri, x):
    acc = None
    for part in _split_bf16(x, 3):
        t = _dot(tri, part)
        acc = t if acc is None else acc + t
    return acc


def _gates(z, lb):
    sig = _sigmoid(z)
    f = lb + (1.0 - lb) * sig
    logf = jnp.log(jnp.maximum(f, F_MIN))
    k = (1.0 - lb) * (1.0 - sig)
    return sig, f, logf, k


def _row_group(jg, anti):
    if anti:
        return 0, 8 * jg + 8
    return 8 * jg, SUB


def _sub_order(anti):
    return range(N_SUB - 1, -1, -1) if anti else range(N_SUB)


def _block_columns(b, anti):
    tt = lax.broadcasted_iota(jnp.int32, (SUB, LANES), 0)
    cols = []
    for jg in range(SUB // 8):
        r0, r1 = _row_group(jg, anti)
        br = b[r0:r1]
        tr = tt[r0:r1]
        for i in range(8):
            sc = 8 * jg + i
            mask = (tr <= sc) if anti else (tr >= sc)
            cols.append((r0, r1, sc, jnp.where(mask, jnp.exp(jnp.minimum(br - b[sc:sc + 1], 0.0)), 0.0)))
    return cols


def _lockstep(gens):
    results = [None] * len(gens)
    live = list(range(len(gens)))
    while live:
        for i in list(live):
            try:
                next(gens[i])
            except StopIteration as stop:
                results[i] = stop.value
                live.remove(i)
    return results


def _scatter_rows(base, accs):
    pieces = []
    for g in range(SUB // 8):
        tot = base[8 * g:8 * g + 8]
        for (r0, r1), acc in accs.items():
            if r0 <= 8 * g and 8 * g + 8 <= r1:
                tot = tot + acc[8 * g - r0:8 * g - r0 + 8]
        pieces.append(tot)
    return jnp.concatenate(pieces, axis=0)


def _chunk_fwd(q, k, v, b, st, bones, bmask, anti):
    rs = [slice(SUB * i, SUB * i + SUB) for i in range(N_SUB)]
    decay, update, prods, spans, qbs = [], [], [], [], []
    for i in range(N_SUB):
        qi, ki, vi, bi = q[rs[i]], k[rs[i]], v[rs[i]], b[rs[i]]
        b_last = bi[0:1] if anti else bi[SUB - 1:SUB]
        decay.append(jnp.exp(b_last))
        update.append(_dot_tn(vi.astype(BF16), (ki * jnp.exp(b_last - bi)).astype(BF16)) * bmask)
        qbs.append((qi * jnp.exp(bi)).astype(BF16))
        for r0, r1, sc, e in _block_columns(bi, anti):
            prods.append(qi[r0:r1] * e * ki[sc:sc + 1])
            spans.append((i, r0, r1, sc))
    pb = _dot(jnp.concatenate(prods, axis=0).astype(BF16), bones)
    yield
    entered = [None] * N_SUB
    for i in _sub_order(anti):
        entered[i] = st
        st = st * decay[i] + update[i]
    yield
    accs = [dict() for _ in range(N_SUB)]
    off = 0
    for i, r0, r1, sc in spans:
        term = pb[off:off + r1 - r0] * v[rs[i]][sc:sc + 1]
        off += r1 - r0
        accs[i][(r0, r1)] = term if (r0, r1) not in accs[i] else accs[i][(r0, r1)] + term
    outs = [_scatter_rows(_dot_nt(qbs[i], entered[i].astype(BF16)), accs[i]) for i in range(N_SUB)]
    return jnp.concatenate(outs, axis=0), st, entered


def _hgrn_fwd(proj, lb, gw, nb, s, ride=None):
    t = proj.shape[0]
    nc = s // CHUNK
    tri = _tri_mats()
    bones = _block_ones(LANES, HEAD_DIM)
    rd = _ride_plan(ride)

    def body(*refs):
        q_ref, zf_ref, zb_ref, v_ref, g_ref, lb_ref, gw_ref, tri_ref, bones_ref = refs[:9]
        y_ref, os_ref, sts_ref = refs[9 + rd.n:12 + rd.n]
        st_ref = refs[12 + 2 * rd.n]
        copies = rd.copies(refs[9:9 + rd.n], refs[12 + rd.n:12 + 2 * rd.n], refs[13 + 2 * rd.n:])
        step_id = pl.program_id(0) * 2 + pl.program_id(1)
        _ride_start(copies, step_id == 0)
        bones_m = bones_ref[...]
        bmask = bones_m.astype(F32)
        st_ref[...] = jnp.zeros_like(st_ref)

        def one_direction(n, anti):
            side = 1 if anti else 0
            z_ref = zb_ref if anti else zf_ref
            cn = (nc - 1 - n) if anti else n
            rows = pl.ds(pl.multiple_of(cn * CHUNK, CHUNK), CHUNK)
            q = q_ref[rows, :]
            v = v_ref[rows, :]
            _, _, logf, k = _gates(z_ref[rows, :], lb_ref[side:side + 1])
            b = _running_sum(tri_ref[side], logf)
            yield
            o, st_new, entered = yield from _chunk_fwd(q, k, v, b, st_ref[side], bones_m, bmask, anti)
            for i in range(N_SUB):
                sts_ref[0, 0, side, cn * N_SUB + i] = entered[i].astype(BF16)
            st_ref[side] = st_new
            (y_ref if anti else os_ref)[rows, :] = o

        def step(n, carry):
            _lockstep([one_direction(n, False), one_direction(n, True)])
            return carry

        lax.fori_loop(0, nc, step, 0)

        def join(n, carry):
            rows = pl.ds(pl.multiple_of(n * CHUNK, CHUNK), CHUNK)
            osum = os_ref[rows, :] + y_ref[rows, :]
            os_ref[rows, :] = osum
            r = lax.rsqrt(_dot_precise(osum * osum, bones_m, 2) * (1.0 / HEAD_DIM) + EPS)
            hg = g_ref[rows, :]
            y_ref[rows, :] = osum * r * gw_ref[...] * (hg * _sigmoid(hg))
            return carry

        lax.fori_loop(0, nc, join, 0)
        _ride_wait(copies, step_id == nb * 2 - 1)

    def col(c):
        return pl.BlockSpec((s, LANES), lambda b, p, c=c: (b, c + p))

    return pl.pallas_call(
        body, name="hgrn_fwd", grid=(nb, 2),
        in_specs=[col(COL_HQ), col(COL_ZFW), col(COL_ZBW), col(COL_HI), col(COL_HG),
                  pl.BlockSpec((2, LANES), lambda b, p: (0, p)),
                  pl.BlockSpec((1, LANES), lambda b, p: (0, 0)),
                  pl.BlockSpec((2, CHUNK, CHUNK), lambda b, p: (0, 0, 0)),
                  pl.BlockSpec((LANES, LANES), lambda b, p: (0, 0))] + rd.in_specs,
        out_specs=[pl.BlockSpec((s, LANES), lambda b, p: (b, p)),
                   pl.BlockSpec((s, LANES), lambda b, p: (b, p)),
                   pl.BlockSpec((1, 1, 2, nc * N_SUB, LANES, LANES), lambda b, p: (b, p, 0, 0, 0, 0))]
        + rd.out_specs,
        out_shape=[jax.ShapeDtypeStruct((t, D_HGRN), F32), jax.ShapeDtypeStruct((t, D_HGRN), F32),
                   jax.ShapeDtypeStruct((nb, 2, 2, nc * N_SUB, LANES, LANES), BF16)] + rd.out_shape,
        scratch_shapes=[pltpu.VMEM((2, LANES, LANES), F32)] + rd.scratch,
        compiler_params=_cparams(("arbitrary", "arbitrary")),
    )(proj, proj, proj, proj, proj, lb, gw, tri, bones, *rd.srcs)


def _chunk_bwd(q, k, v, b, do, states, rt, bones, bmask, anti):
    rs = [slice(SUB * i, SUB * i + SUB) for i in range(N_SUB)]
    r8 = lax.broadcasted_iota(jnp.int32, (8, LANES), 0)
    decay, update, dq_inter, ebls, prods_p, prods_d, spans, qes, kes = [], [], [], [], [], [], [], [], []
    for i in range(N_SUB):
        qi, ki, vi, bi, doi = q[rs[i]], k[rs[i]], v[rs[i]], b[rs[i]], do[rs[i]]
        b_last = bi[0:1] if anti else bi[SUB - 1:SUB]
        eb = jnp.exp(bi)
        dob = doi.astype(BF16)
        decay.append(jnp.exp(b_last))
        ebls.append(jnp.exp(b_last - bi))
        update.append(_dot_tn(dob, (qi * eb).astype(BF16)) * bmask)
        dq_inter.append(eb * _dot(dob, states[i]))
        for r0, r1, sc, e in _block_columns(bi, anti):
            qe = qi[r0:r1] * e
            qes.append(qe)
            kes.append(e * ki[sc:sc + 1])
            prods_p.append(qe * ki[sc:sc + 1])
            prods_d.append(doi[r0:r1] * vi[sc:sc + 1])
            spans.append((i, r0, r1, sc))
    sums = _dot(jnp.concatenate(prods_p + prods_d, axis=0).astype(BF16), bones)
    half = sum(r1 - r0 for _, r0, r1, _ in spans)
    yield
    entered = [None] * N_SUB
    for i in reversed(list(_sub_order(anti))):
        entered[i] = rt
        rt = rt * decay[i] + update[i]
    yield
    accs = [dict() for _ in range(N_SUB)]
    dk_blks = [[jnp.zeros((8, LANES), F32) for _ in range(SUB // 8)] for _ in range(N_SUB)]
    dv_blks = [[jnp.zeros((8, LANES), F32) for _ in range(SUB // 8)] for _ in range(N_SUB)]
    off = 0
    for n, (i, r0, r1, sc) in enumerate(spans):
        nr = r1 - r0
        pb = sums[off:off + nr]
        dpb = sums[half + off:half + off + nr]
        off += nr
        term = dpb * kes[n]
        accs[i][(r0, r1)] = term if (r0, r1) not in accs[i] else accs[i][(r0, r1)] + term
        dk_s = jnp.sum(dpb * qes[n], axis=0, keepdims=True)
        dv_s = jnp.sum(pb * do[rs[i]][r0:r1], axis=0, keepdims=True)
        dk_blks[i][sc // 8] = jnp.where(r8 == sc % 8, dk_s, dk_blks[i][sc // 8])
        dv_blks[i][sc // 8] = jnp.where(r8 == sc % 8, dv_s, dv_blks[i][sc // 8])
    dqs, dks, dvs, dbs = [], [], [], []
    for i in range(N_SUB):
        ki, vi = k[rs[i]], v[rs[i]]
        rtb = entered[i].astype(BF16)
        dk_inter = ebls[i] * _dot(vi.astype(BF16), rtb)
        dv_inter = _dot_nt((ki * ebls[i]).astype(BF16), rtb)
        dqs.append(_scatter_rows(dq_inter[i], accs[i]))
        dks.append(dk_inter + jnp.concatenate(dk_blks[i], axis=0))
        dvs.append(dv_inter + jnp.concatenate(dv_blks[i], axis=0))
        db_last = (jnp.sum(ki * dk_inter, axis=0, keepdims=True)
                   + decay[i] * jnp.sum(entered[i] * states[i].astype(F32), axis=0, keepdims=True))
        dbs.append(jnp.broadcast_to(db_last, (SUB, LANES)))
    cat = lambda xs: jnp.concatenate(xs, axis=0)
    return cat(dqs), cat(dks), cat(dvs), rt, cat(dbs)


def _hgrn_bwd(proj, lb, gw, osum, states, dy, nb, s, ride=None):
    t = proj.shape[0]
    nc = s // CHUNK
    assert nc % 2 == 0
    tri = _tri_mats()
    bones = _block_ones(LANES, HEAD_DIM)
    rd = _ride_plan(ride)

    def body(*refs):
        (q_ref, zf_ref, zb_ref, v_ref, g_ref, lb_ref, gw_ref, os_ref, sts_ref, dy_ref, tri_ref,
         bones_ref) = refs[:12]
        dq_ref, dzf_ref, dzb_ref, dv_ref, dg_ref, dgw_ref, dlb_ref = refs[12 + rd.n:19 + rd.n]
        do_sc, dq_sc, dv_sc, rt_cur = refs[19 + 2 * rd.n:23 + 2 * rd.n]
        copies = rd.copies(refs[12:12 + rd.n], refs[19 + rd.n:19 + 2 * rd.n], refs[23 + 2 * rd.n:])
        step_id = pl.program_id(0) * 2 + pl.program_id(1)
        _ride_start(copies, step_id == 0)
        bones_m = bones_ref[...]
        bmask = bones_m.astype(F32)
        gwv = gw_ref[...]

        def head(n, acc):
            rows = pl.ds(pl.multiple_of(n * CHUNK, CHUNK), CHUNK)
            o = os_ref[rows, :]
            hg = g_ref[rows, :]
            dyv = dy_ref[rows, :]
            sg = _sigmoid(hg)
            r = lax.rsqrt(_dot_precise(o * o, bones_m, 2) * (1.0 / HEAD_DIM) + EPS)
            nrm = o * r * gwv
            dn = dyv * (hg * sg)
            dg_ref[rows, :] = (dyv * nrm * (sg * (1.0 + hg * (1.0 - sg)))).astype(BF16)
            g = dn * gwv
            mean_go = _dot_precise(g * o, bones_m, 2) * (1.0 / HEAD_DIM)
            do_sc[rows, :] = r * (g - o * (r * r) * mean_go)
            return acc + jnp.sum(dn * o * r, axis=0, keepdims=True)

        dgw_ref[0] = lax.fori_loop(0, nc, head, jnp.zeros((1, LANES), F32))
        dq_sc[...] = jnp.zeros_like(dq_sc)
        dv_sc[...] = jnp.zeros_like(dv_sc)

        rt_cur[...] = jnp.zeros_like(rt_cur)

        def one_direction(n, anti):
            side = 1 if anti else 0
            z_ref = zb_ref if anti else zf_ref
            dz_ref = dzb_ref if anti else dzf_ref
            lbv = lb_ref[side:side + 1]
            cn = n if anti else (nc - 1 - n)
            rows = pl.ds(pl.multiple_of(cn * CHUNK, CHUNK), CHUNK)
            q = q_ref[rows, :]
            v = v_ref[rows, :]
            sig, f, logf, k = _gates(z_ref[rows, :], lbv)
            b = _running_sum(tri_ref[side], logf)
            yield
            do = do_sc[rows, :]
            entered = [sts_ref[0, 0, side, cn * N_SUB + i] for i in range(N_SUB)]
            dq, dk, dv, rt_new, db_last = yield from _chunk_bwd(q, k, v, b, do, entered, rt_cur[side], bones_m,
                                                                bmask, anti)
            rt_cur[side] = rt_new
            dq_sc[rows, :] += dq
            dv_sc[rows, :] += dv
            dlogf = _running_sum(tri_ref[1 - side], q * dq - k * dk) + db_last
            dfl = jnp.where(f > F_MIN, dlogf / f, 0.0)
            dz_ref[rows, :] = ((dfl - dk) * (1.0 - lbv) * sig * (1.0 - sig)).astype(BF16)
            return jnp.sum((dfl - dk) * (1.0 - sig), axis=0, keepdims=True)

        def back(n, dlb):
            d0, d1 = _lockstep([one_direction(n, False), one_direction(n, True)])
            return dlb[0] + d0, dlb[1] + d1

        zero = jnp.zeros((1, LANES), F32)
        dlb0, dlb1 = lax.fori_loop(0, nc, back, (zero, zero))
        dlb_ref[0, 0:1, :] = dlb0
        dlb_ref[0, 1:2, :] = dlb1

        dq_ref[...] = dq_sc[...].astype(BF16)
        dv_ref[...] = dv_sc[...].astype(BF16)
        _ride_wait(copies, step_id == nb * 2 - 1)

    def col(c):
        return pl.BlockSpec((s, LANES), lambda b, p, c=c: (b, c + p))

    sl = pl.BlockSpec((s, LANES), lambda b, p: (b, p))
    out_t = jax.ShapeDtypeStruct((t, D_HGRN), BF16)
    return pl.pallas_call(
        body, name="hgrn_bwd", grid=(nb, 2),
        in_specs=[col(COL_HQ), col(COL_ZFW), col(COL_ZBW), col(COL_HI), col(COL_HG),
                  pl.BlockSpec((2, LANES), lambda b, p: (0, p)),
                  pl.BlockSpec((1, LANES), lambda b, p: (0, 0)),
                  sl,
                  pl.BlockSpec((1, 1, 2, nc * N_SUB, LANES, LANES), lambda b, p: (b, p, 0, 0, 0, 0)),
                  sl,
                  pl.BlockSpec((2, CHUNK, CHUNK), lambda b, p: (0, 0, 0)),
                  pl.BlockSpec((LANES, LANES), lambda b, p: (0, 0))] + rd.in_specs,
        out_specs=[sl, sl, sl, sl, sl,
                   pl.BlockSpec((1, 1, LANES), lambda b, p: (b, 0, p)),
                   pl.BlockSpec((1, 2, LANES), lambda b, p: (b, 0, p))] + rd.out_specs,
        out_shape=[out_t, out_t, out_t, out_t, out_t,
                   jax.ShapeDtypeStruct((nb, 1, D_HGRN), F32),
                   jax.ShapeDtypeStruct((nb, 2, D_HGRN), F32)] + rd.out_shape,
        scratch_shapes=[pltpu.VMEM((s, LANES), F32), pltpu.VMEM((s, LANES), F32), pltpu.VMEM((s, LANES), F32),
                        pltpu.VMEM((2, LANES, LANES), F32)] + rd.scratch,
        compiler_params=_cparams(("arbitrary", "arbitrary")),
    )(proj, proj, proj, proj, proj, lb, gw, osum, states, dy, tri, bones, *rd.srcs)


def _lower_bounds(logits):
    def body(lg_ref, lb_ref):
        rows = [lg_ref[l:l + 1, :] for l in range(DEPTH)]
        m = functools.reduce(jnp.maximum, rows)
        ex = [jnp.exp(r - m) for r in rows]
        den = functools.reduce(jnp.add, ex)
        run = jnp.zeros_like(m)
        for l in range(DEPTH):
            if l > 0:
                run = run + ex[l] / den
            lb_ref[l:l + 1, :] = run

    return pl.pallas_call(body, name="lower_bounds", out_shape=jax.ShapeDtypeStruct(logits.shape, F32))(logits)


def _lower_bounds_bwd(logits, dlb):
    def body(lg_ref, dlb_ref, dlg_ref):
        rows = [lg_ref[l:l + 1, :] for l in range(DEPTH)]
        m = functools.reduce(jnp.maximum, rows)
        ex = [jnp.exp(r - m) for r in rows]
        den = functools.reduce(jnp.add, ex)
        sm = [e / den for e in ex]
        dsm = [jnp.zeros_like(m) for _ in range(DEPTH)]
        for i in range(1, DEPTH):
            for l in range(i, DEPTH):
                dsm[i] = dsm[i] + dlb_ref[l:l + 1, :]
        dot = functools.reduce(jnp.add, [sm[i] * dsm[i] for i in range(DEPTH)])
        for i in range(DEPTH):
            dlg_ref[i:i + 1, :] = sm[i] * (dsm[i] - dot)

    return pl.pallas_call(body, name="lower_bounds_bwd", out_shape=jax.ShapeDtypeStruct(logits.shape, F32))(logits, dlb)


CONV_ROWS = 128


def _conv_core(a, bg, dww, dwb, lnw, lnb, upad_ref, s):
    sb = _sigmoid(bg)
    u = a * sb
    upad_ref[0:16, :] = jnp.zeros((16, D_CONV), F32)
    upad_ref[16:16 + s, :] = u
    upad_ref[16 + s:32 + s, :] = jnp.zeros((16, D_CONV), F32)
    rows = min(s, CONV_ROWS)
    pieces = []
    for r0 in range(0, s, rows):
        acc = None
        for j in range(CONV_W):
            term = upad_ref[r0 + 1 + j:r0 + 1 + j + rows, :] * dww[j:j + 1, :]
            acc = term if acc is None else acc + term
        pieces.append(acc)
    c = jnp.concatenate(pieces, axis=0) + dwb
    mu = jnp.mean(c, axis=-1, keepdims=True)
    xc = c - mu
    rstd = lax.rsqrt(jnp.mean(xc * xc, axis=-1, keepdims=True) + LN_EPS)
    nh = xc * rstd
    l = nh * lnw + lnb
    sl = _sigmoid(l)
    return sb, nh, rstd, l, sl


def _conv_fwd(proj, dww, dwb, lnw, lnb, pww, pwb, nb, s):
    t = proj.shape[0]
    assert s % min(s, CONV_ROWS) == 0

    def body(a_ref, b_ref, dww_ref, dwb_ref, lnw_ref, lnb_ref, pww_ref, pwb_ref, y_ref, upad_ref):
        _, _, _, l, sl = _conv_core(a_ref[...], b_ref[...], dww_ref[...], dwb_ref[...], lnw_ref[...],
                                    lnb_ref[...], upad_ref, s)
        y_ref[...] = _dot((l * sl).astype(BF16), pww_ref[...]) + pwb_ref[...]

    vec = pl.BlockSpec((1, D_CONV), lambda b: (0, 0))
    return pl.pallas_call(
        body, name="conv_fwd", grid=(nb,),
        in_specs=[pl.BlockSpec((s, D_CONV), lambda b: (b, COL_CA)),
                  pl.BlockSpec((s, D_CONV), lambda b: (b, COL_CB)),
                  pl.BlockSpec((32, D_CONV), lambda b: (0, 0)), vec, vec, vec,
                  pl.BlockSpec((D_CONV, D_CONV), lambda b: (0, 0)), vec],
        out_specs=pl.BlockSpec((s, D_CONV), lambda b: (b, 0)),
        out_shape=jax.ShapeDtypeStruct((t, D_CONV), F32),
        scratch_shapes=[pltpu.VMEM((s + 32, D_CONV), F32)],
        compiler_params=_cparams(("parallel",)),
    )(proj, proj, dww, dwb, lnw, lnb, pww, pwb)


def _conv_bwd(proj, dy, dww, dwb, lnw, lnb, pww, nb, s):
    t = proj.shape[0]

    def body(a_ref, b_ref, dy_ref, dww_ref, dwb_ref, lnw_ref, lnb_ref, pww_ref,
             dab_ref, ddww_ref, ddwb_ref, dlnw_ref, dlnb_ref, dpww_ref, dpwb_ref, upad_ref, dcpad_ref):
        a = a_ref[...]
        dww = dww_ref[...]
        sb, nh, rstd, l, sl = _conv_core(a, b_ref[...], dww, dwb_ref[...], lnw_ref[...], lnb_ref[...],
                                         upad_ref, s)
        dyv = dy_ref[...]
        dyb = dyv.astype(BF16)
        ds = _dot_nt(dyb, pww_ref[...])
        dl = ds * (sl * (1.0 + l * (1.0 - sl)))
        dn = dl * lnw_ref[...]
        dc = rstd * (dn - jnp.mean(dn, axis=-1, keepdims=True)
                     - nh * jnp.mean(dn * nh, axis=-1, keepdims=True))

        @pl.when(pl.program_id(0) == 0)
        def _():
            for r in (ddww_ref, ddwb_ref, dlnw_ref, dlnb_ref, dpww_ref, dpwb_ref):
                r[...] = jnp.zeros_like(r)

        dpww_ref[...] += _dot_tn((l * sl).astype(BF16), dyb)
        dpwb_ref[...] += jnp.sum(dyv, axis=0, keepdims=True)
        dlnw_ref[...] += jnp.sum(dl * nh, axis=0, keepdims=True)
        dlnb_ref[...] += jnp.sum(dl, axis=0, keepdims=True)
        ddwb_ref[...] += jnp.sum(dc, axis=0, keepdims=True)

        dcpad_ref[0:16, :] = jnp.zeros((16, D_CONV), F32)
        dcpad_ref[16:16 + s, :] = dc
        dcpad_ref[16 + s:32 + s, :] = jnp.zeros((16, D_CONV), F32)
        rows = min(s, CONV_ROWS)
        r8 = lax.broadcasted_iota(jnp.int32, (32, D_CONV), 0)
        ddww = jnp.zeros((32, D_CONV), F32)
        pieces = []
        for r0 in range(0, s, rows):
            acc = None
            dcr = dcpad_ref[16 + r0:16 + r0 + rows, :]
            for j in range(CONV_W):
                term = dcpad_ref[r0 + 31 - j:r0 + 31 - j + rows, :] * dww[j:j + 1, :]
                acc = term if acc is None else acc + term
                wj = jnp.sum(dcr * upad_ref[r0 + 1 + j:r0 + 1 + j + rows, :], axis=0, keepdims=True)
                ddww = ddww + jnp.where(r8 == j, wj, 0.0)
            pieces.append(acc)
        du = jnp.concatenate(pieces, axis=0)
        ddww_ref[...] += ddww
        dab_ref[:, 0:D_CONV] = (du * sb).astype(BF16)
        dab_ref[:, D_CONV:2 * D_CONV] = (du * a * sb * (1.0 - sb)).astype(BF16)

    vec = pl.BlockSpec((1, D_CONV), lambda b: (0, 0))
    mat = pl.BlockSpec((D_CONV, D_CONV), lambda b: (0, 0))
    w32 = pl.BlockSpec((32, D_CONV), lambda b: (0, 0))
    vshape = jax.ShapeDtypeStruct((1, D_CONV), F32)
    return pl.pallas_call(
        body, name="conv_bwd", grid=(nb,),
        in_specs=[pl.BlockSpec((s, D_CONV), lambda b: (b, COL_CA)),
                  pl.BlockSpec((s, D_CONV), lambda b: (b, COL_CB)),
                  pl.BlockSpec((s, D_CONV), lambda b: (b, 0)),
                  w32, vec, vec, vec, mat],
        out_specs=[pl.BlockSpec((s, 2 * D_CONV), lambda b: (b, 0)), w32, vec, vec, vec, mat, vec],
        out_shape=[jax.ShapeDtypeStruct((t, 2 * D_CONV), BF16),
                   jax.ShapeDtypeStruct((32, D_CONV), F32), vshape, vshape, vshape,
                   jax.ShapeDtypeStruct((D_CONV, D_CONV), F32), vshape],
        scratch_shapes=[pltpu.VMEM((s + 32, D_CONV), F32), pltpu.VMEM((s + 32, D_CONV), F32)],
        compiler_params=_cparams(("arbitrary",)),
    )(proj, proj, dy, dww, dwb, lnw, lnb, pww)


def _mix_out(o_attn, y_hgrn, y_conv, x, aw, cw, w_out, tm):
    t = x.shape[0]

    def body(o_ref, h_ref, c_ref, x_ref, aw_ref, cw_ref, w_ref, mixed_ref, x1_ref):
        o = o_ref[...]
        a = o * lax.rsqrt(jnp.mean(o * o, axis=-1, keepdims=True) + EPS) * aw_ref[...]
        yc = c_ref[...]
        c = yc * lax.rsqrt(jnp.mean(yc * yc, axis=-1, keepdims=True) + EPS) * cw_ref[...]
        ab, hb, cb = a.astype(BF16), h_ref[...].astype(BF16), c.astype(BF16)
        mixed_ref[:, 0:512] = ab
        mixed_ref[:, 512:768] = hb
        mixed_ref[:, 768:1024] = cb
        x1_ref[...] = (x_ref[...] + _dot(ab, w_ref[0:512, :]) + _dot(hb, w_ref[512:768, :])
                       + _dot(cb, w_ref[768:1024, :]))

    def tok(w):
        return pl.BlockSpec((tm, w), lambda i: (i, 0))

    return pl.pallas_call(
        body, name="mix_out", grid=(t // tm,),
        in_specs=[tok(512), tok(256), tok(256), tok(D_MODEL),
                  pl.BlockSpec((1, 512), lambda i: (0, 0)), pl.BlockSpec((1, 256), lambda i: (0, 0)),
                  pl.BlockSpec((D_MODEL, D_MODEL), lambda i: (0, 0))],
        out_specs=[tok(D_MODEL), tok(D_MODEL)],
        out_shape=[jax.ShapeDtypeStruct((t, D_MODEL), BF16), jax.ShapeDtypeStruct((t, D_MODEL), F32)],
        compiler_params=_cparams(("parallel",)),
    )(o_attn, y_hgrn, y_conv, x, aw, cw, w_out)


def _mix_out_bwd(dx1, w_out, o_attn, y_conv, aw, cw, tm):
    t = dx1.shape[0]

    def body(dx_ref, w_ref, o_ref, c_ref, aw_ref, cw_ref, do_ref, dh_ref, dc_ref, daw_ref, dcw_ref):
        dm = _dot_nt(dx_ref[...].astype(BF16), w_ref[...])
        do, daw = _rms_bwd(dm[:, 0:512], o_ref[...], aw_ref[...])
        dc, dcw = _rms_bwd(dm[:, 768:1024], c_ref[...], cw_ref[...])
        do_ref[...] = do
        dh_ref[...] = dm[:, 512:768]
        dc_ref[...] = dc

        @pl.when(pl.program_id(0) == 0)
        def _():
            daw_ref[...] = jnp.zeros_like(daw_ref)
            dcw_ref[...] = jnp.zeros_like(dcw_ref)

        daw_ref[...] += jnp.sum(daw, axis=0, keepdims=True)
        dcw_ref[...] += jnp.sum(dcw, axis=0, keepdims=True)

    def tok(w):
        return pl.BlockSpec((tm, w), lambda i: (i, 0))

    v512 = pl.BlockSpec((1, 512), lambda i: (0, 0))
    v256 = pl.BlockSpec((1, 256), lambda i: (0, 0))
    return pl.pallas_call(
        body, name="mix_out_bwd", grid=(t // tm,),
        in_specs=[tok(D_MODEL), pl.BlockSpec((D_MODEL, D_MODEL), lambda i: (0, 0)), tok(512), tok(256),
                  v512, v256],
        out_specs=[tok(512), tok(256), tok(256), v512, v256],
        out_shape=[jax.ShapeDtypeStruct((t, 512), F32), jax.ShapeDtypeStruct((t, 256), F32),
                   jax.ShapeDtypeStruct((t, 256), F32), jax.ShapeDtypeStruct((1, 512), F32),
                   jax.ShapeDtypeStruct((1, 256), F32)],
        compiler_params=_cparams(("arbitrary",)),
    )(dx1, w_out, o_attn, y_conv, aw, cw)


FF_BLOCKS = 4


def _ffn_fwd(x1, fw, wg, wu, wd, tm, ride=None):
    t = x1.shape[0]
    fb = wg.shape[1]
    nf = N_DEV // FF_BLOCKS
    rd = _ride_plan(ride)
    grid = (t // tm, nf)

    def body(*refs):
        x_ref, fw_ref, wg_ref, wu_ref, wd_ref = refs[:5]
        h_ref, g_ref, u_ref, a_ref, x2_ref = refs[5 + rd.n:10 + rd.n]
        acc_ref = refs[10 + 2 * rd.n]
        copies = rd.copies(refs[5:5 + rd.n], refs[10 + rd.n:10 + 2 * rd.n], refs[11 + 2 * rd.n:])
        step_id = _grid_step_id(grid)
        _ride_start(copies, step_id == 0)
        j = pl.program_id(1)

        @pl.when(j == 0)
        def _():
            xv = x_ref[...]
            r = lax.rsqrt(jnp.mean(xv * xv, axis=-1, keepdims=True) + EPS)
            h_ref[...] = (xv * r * fw_ref[...]).astype(BF16)
            acc_ref[...] = xv

        h = h_ref[...]
        out = None
        for c in range(FF_BLOCKS):
            g = _dot_nt(h, wg_ref[c])
            u = _dot_nt(h, wu_ref[c])
            a = (g * _sigmoid(g) * u).astype(BF16)
            g_ref[c] = g.astype(BF16)
            u_ref[c] = u.astype(BF16)
            a_ref[c] = a
            part = _dot(a, wd_ref[c])
            out = part if out is None else out + part
        acc_ref[...] += out

        @pl.when(j == nf - 1)
        def _():
            x2_ref[...] = acc_ref[...]

        _ride_wait(copies, step_id == (t // tm) * nf - 1)

    tok = pl.BlockSpec((tm, D_MODEL), lambda i, j: (i, 0))
    ffb = pl.BlockSpec((FF_BLOCKS, tm, fb), lambda i, j: (j, i, 0))
    ffs = jax.ShapeDtypeStruct((N_DEV, t, fb), BF16)
    return pl.pallas_call(
        body, name="ffn_fwd", grid=grid,
        in_specs=[tok, pl.BlockSpec((1, D_MODEL), lambda i, j: (0, 0)),
                  pl.BlockSpec((FF_BLOCKS, fb, D_MODEL), lambda i, j: (j, 0, 0)),
                  pl.BlockSpec((FF_BLOCKS, fb, D_MODEL), lambda i, j: (j, 0, 0)),
                  pl.BlockSpec((FF_BLOCKS, fb, D_MODEL), lambda i, j: (j, 0, 0))] + rd.in_specs,
        out_specs=[tok, ffb, ffb, ffb, tok] + rd.out_specs,
        out_shape=[jax.ShapeDtypeStruct((t, D_MODEL), BF16), ffs, ffs, ffs,
                   jax.ShapeDtypeStruct((t, D_MODEL), F32)] + rd.out_shape,
        scratch_shapes=[pltpu.VMEM((tm, D_MODEL), F32)] + rd.scratch,
        compiler_params=_cparams(("arbitrary", "arbitrary")),
    )(x1, fw, wg, wu, wd, *rd.srcs)


def _ffn_bwd(dx2, g, u, wg, wu, wd, x1, fw, tm, ride=None):
    t = dx2.shape[0]
    fb = wg.shape[1]
    nf = N_DEV // FF_BLOCKS
    rd = _ride_plan(ride)
    grid = (t // tm, nf)

    def body(*refs):
        dx_ref, g_ref, u_ref, wg_ref, wu_ref, wd_ref, x_ref, fw_ref = refs[:8]
        dg_ref, du_ref, dx1_ref, dfw_ref = refs[8 + rd.n:12 + rd.n]
        acc_ref = refs[12 + 2 * rd.n]
        copies = rd.copies(refs[8:8 + rd.n], refs[12 + rd.n:12 + 2 * rd.n], refs[13 + 2 * rd.n:])
        step_id = _grid_step_id(grid)
        _ride_start(copies, step_id == 0)
        i = pl.program_id(0)
        j = pl.program_id(1)
        dxb = dx_ref[...].astype(BF16)
        dh = None
        for c in range(FF_BLOCKS):
            da = _dot_nt(dxb, wd_ref[c])
            gv = g_ref[c].astype(F32)
            uv = u_ref[c].astype(F32)
            sg = _sigmoid(gv)
            dg = (da * uv * (sg * (1.0 + gv * (1.0 - sg)))).astype(BF16)
            du = (da * gv * sg).astype(BF16)
            dg_ref[c] = dg
            du_ref[c] = du
            part = _dot(dg, wg_ref[c]) + _dot(du, wu_ref[c])
            dh = part if dh is None else dh + part

        @pl.when(j == 0)
        def _():
            acc_ref[...] = dh

        @pl.when(j > 0)
        def _():
            acc_ref[...] += dh

        @pl.when((i == 0) & (j == 0))
        def _():
            dfw_ref[...] = jnp.zeros_like(dfw_ref)

        @pl.when(j == nf - 1)
        def _():
            dx, dfw = _rms_bwd(acc_ref[...], x_ref[...], fw_ref[...])
            dx1_ref[...] = dx_ref[...] + dx
            dfw_ref[...] += jnp.sum(dfw, axis=0, keepdims=True)

        _ride_wait(copies, step_id == (t // tm) * nf - 1)

    tok = pl.BlockSpec((tm, D_MODEL), lambda i, j: (i, 0))
    ffb = pl.BlockSpec((FF_BLOCKS, tm, fb), lambda i, j: (j, i, 0))
    ffs = jax.ShapeDtypeStruct((N_DEV, t, fb), BF16)
    vec = pl.BlockSpec((1, D_MODEL), lambda i, j: (0, 0))
    return pl.pallas_call(
        body, name="ffn_bwd", grid=grid,
        in_specs=[tok, ffb, ffb,
                  pl.BlockSpec((FF_BLOCKS, fb, D_MODEL), lambda i, j: (j, 0, 0)),
                  pl.BlockSpec((FF_BLOCKS, fb, D_MODEL), lambda i, j: (j, 0, 0)),
                  pl.BlockSpec((FF_BLOCKS, fb, D_MODEL), lambda i, j: (j, 0, 0)),
                  tok, vec] + rd.in_specs,
        out_specs=[ffb, ffb, tok, vec] + rd.out_specs,
        out_shape=[ffs, ffs, jax.ShapeDtypeStruct((t, D_MODEL), F32),
                   jax.ShapeDtypeStruct((1, D_MODEL), F32)] + rd.out_shape,
        scratch_shapes=[pltpu.VMEM((tm, D_MODEL), F32)] + rd.scratch,
        compiler_params=_cparams(("arbitrary", "arbitrary")),
    )(dx2, g, u, wg, wu, wd, x1, fw, *rd.srcs)


def _loss_grad(y, target, tm):
    t, d = y.shape

    def body(y_ref, t_ref, dy_ref, loss_ref):
        err = y_ref[...] - t_ref[...]
        dy_ref[...] = err * (1.0 / d)

        @pl.when(pl.program_id(0) == 0)
        def _():
            loss_ref[...] = jnp.zeros_like(loss_ref)

        part = jnp.sum(jnp.sum(err * err, axis=-1, keepdims=True), axis=0, keepdims=True)
        loss_ref[...] += part * (0.5 / d)

    tok = pl.BlockSpec((tm, d), lambda i: (i, 0))
    return pl.pallas_call(
        body, name="loss_grad", grid=(t // tm,),
        in_specs=[tok, tok],
        out_specs=[tok, pl.BlockSpec((1, 1), lambda i: (0, 0))],
        out_shape=[jax.ShapeDtypeStruct((t, d), F32), jax.ShapeDtypeStruct((1, 1), F32)],
        compiler_params=_cparams(("arbitrary",)),
    )(y, target)


def _tile(v, reps):
    return jnp.tile(v.reshape(1, -1), (1, reps))


class _LocalPlan:
    def __init__(self, wb):
        self.w = [{n: wb[n][l] for n in BIG_AXIS} for l in range(DEPTH)]

    def ride(self, kernel_name, l, grads=None):
        return None

    def done(self, kernel_name, l, outs):
        pass


def _local_step(x, target, p, plan):
    nb, s, d = x.shape
    t = nb * s
    tm = min(512, s)
    tq = min(1024, s)
    xf = x.reshape(t, d)
    cosq, sinq = _rope_tables(s)
    ones512 = _block_ones(512, HEAD_DIM)
    lbs = _lower_bounds(p["hgrn_lb_logits"].reshape(DEPTH, 2 * D_HGRN)).reshape(DEPTH, 2, D_HGRN)

    saved = []
    cur = xf
    wb = plan.w
    for l in range(DEPTH):
        qw = _tile(p["q_norm_w"][l], N_HEADS)
        kw = _tile(p["k_norm_w"][l], N_KV)
        gw = _tile(p["hgrn_gnorm_w"][l], 2)
        dww = jnp.pad(p["conv_dw_w"][l], ((0, 1), (0, 0)))
        pww = p["conv_pw_w"][l].astype(BF16)
        h0, proj = _rms_proj(cur, _row(p["mix_norm_w"][l]), wb[l]["w_in"], tm)
        qr, kd, vd, kdt, vdt = _qkv_prep(proj, cosq, sinq, qw, kw, ones512, s, tm)
        o_attn, lse, *rode = _attn_fwd(qr, kd, vdt, nb, s, tq, plan.ride("attn_fwd", l))
        plan.done("attn_fwd", l, rode)
        y_hgrn, osum, states, *rode = _hgrn_fwd(proj, lbs[l], gw, nb, s, plan.ride("hgrn_fwd", l))
        plan.done("hgrn_fwd", l, rode)
        y_conv = _conv_fwd(proj, dww, _row(p["conv_dw_b"][l]), _row(p["conv_ln_w"][l]),
                           _row(p["conv_ln_b"][l]), pww, _row(p["conv_pw_b"][l]), nb, s)
        mixed, x1 = _mix_out(o_attn, y_hgrn, y_conv, cur, _row(p["attn_out_norm_w"][l]),
                             _row(p["conv_out_norm_w"][l]), wb[l]["w_out"], tm)
        hf, g, u, a, x2, *rode = _ffn_fwd(x1, _row(p["ffn_norm_w"][l]), wb[l]["w_gate"], wb[l]["w_up"],
                                          wb[l]["w_down"], tm, plan.ride("ffn_fwd", l))
        plan.done("ffn_fwd", l, rode)
        saved.append(dict(x=cur, h0=h0, proj=proj, qr=qr, kd=kd, vd=vd, kdt=kdt, o_attn=o_attn, lse=lse,
                          osum=osum, states=states, y_conv=y_conv, mixed=mixed, x1=x1, hf=hf, g=g, u=u, a=a,
                          qw=qw, kw=kw, gw=gw, dww=dww, pww=pww))
        cur = x2

    dcur, loss = _loss_grad(cur, target.reshape(t, d), tm)

    grads = {k: [None] * DEPTH for k in WEIGHTS}
    dlb = [None] * DEPTH
    for l in reversed(range(DEPTH)):
        sv = saved[l]
        dg, du, dx1, dfw, *rode = _ffn_bwd(dcur, sv["g"], sv["u"], wb[l]["w_gate"], wb[l]["w_up"],
                                           wb[l]["w_down"], sv["x1"], _row(p["ffn_norm_w"][l]), tm,
                                           plan.ride("ffn_bwd", l, grads))
        plan.done("ffn_bwd", l, rode)
        grads["ffn_norm_w"][l] = dfw[0]
        grads["w_gate"][l] = _dw_ff(dg, sv["hf"], "dw_gate", tm)
        grads["w_up"][l] = _dw_ff(du, sv["hf"], "dw_up", tm)
        grads["w_down"][l] = _dw_ff(sv["a"], dcur, "dw_down", tm)
        do_attn, dy_hgrn, dy_conv, daw, dcw = _mix_out_bwd(
            dx1, wb[l]["w_out"], sv["o_attn"], sv["y_conv"], _row(p["attn_out_norm_w"][l]),
            _row(p["conv_out_norm_w"][l]), tm)
        grads["attn_out_norm_w"][l] = daw[0]
        grads["conv_out_norm_w"][l] = dcw[0]
        grads["w_out"][l] = _mm_tn(sv["mixed"], dx1, D_MODEL, "dw_out", tm)
        dq, dkd, dvd, *rode = _attn_bwd(sv["qr"], sv["kd"], sv["vd"], sv["kdt"], sv["o_attn"], sv["lse"], do_attn,
                                        nb, s, tq, plan.ride("attn_bwd", l, grads))
        plan.done("attn_bwd", l, rode)
        dqkv, dqw, dkw = _qkv_bwd(sv["proj"], dq, dkd, dvd, cosq, sinq, sv["qw"], sv["kw"], ones512, s, tm)
        grads["q_norm_w"][l] = dqw.reshape(N_HEADS, HEAD_DIM).sum(0)
        grads["k_norm_w"][l] = dkw.reshape(N_KV, HEAD_DIM).sum(0)
        dhq, dzf, dzb, dhi, dhg, dgw, dlb_l, *rode = _hgrn_bwd(sv["proj"], lbs[l], sv["gw"], sv["osum"],
                                                               sv["states"], dy_hgrn, nb, s,
                                                               plan.ride("hgrn_bwd", l, grads))
        plan.done("hgrn_bwd", l, rode)
        grads["hgrn_gnorm_w"][l] = dgw.reshape(nb * D_HGRN // HEAD_DIM, HEAD_DIM).sum(0)
        dlb[l] = dlb_l.sum(0)
        dab, ddww, ddwb, dlnw, dlnb, dpww, dpwb = _conv_bwd(
            sv["proj"], dy_conv, sv["dww"], _row(p["conv_dw_b"][l]), _row(p["conv_ln_w"][l]),
            _row(p["conv_ln_b"][l]), sv["pww"], nb, s)
        grads["conv_dw_w"][l] = ddww[:CONV_W]
        grads["conv_dw_b"][l] = ddwb[0]
        grads["conv_ln_w"][l] = dlnw[0]
        grads["conv_ln_b"][l] = dlnb[0]
        grads["conv_pw_w"][l] = dpww
        grads["conv_pw_b"][l] = dpwb[0]
        pieces = [dqkv, dhq, dzf, dzb, dhi, dhg, dab]
        grads["w_in"][l] = _dw_in(sv["h0"], pieces, tm)
        dcur, dnw = _proj_bwd(pieces, wb[l]["w_in"], sv["x"], _row(p["mix_norm_w"][l]), dx1, tm)
        grads["mix_norm_w"][l] = dnw[0]

    dlog = _lower_bounds_bwd(p["hgrn_lb_logits"].reshape(DEPTH, 2 * D_HGRN),
                             jnp.stack(dlb).reshape(DEPTH, 2 * D_HGRN))
    out = {k: (v if k in BIG_AXIS else jnp.stack(v)) for k, v in grads.items() if k != "hgrn_lb_logits"}
    out["hgrn_lb_logits"] = dlog.reshape(DEPTH, 2, D_HGRN)
    return loss, dcur.reshape(nb, s, d), out


BIG_AXIS = {"w_in": 2, "w_out": 1, "w_gate": 2, "w_up": 2, "w_down": 1}
SMALL_SHARD_AXIS = {"hgrn_lb_logits": 2, "conv_dw_w": 2, "conv_pw_w": 1}
WEIGHTS = ("mix_norm_w", "w_in", "q_norm_w", "k_norm_w", "hgrn_lb_logits", "hgrn_gnorm_w", "conv_dw_w",
           "conv_dw_b", "conv_ln_w", "conv_ln_b", "conv_pw_w", "conv_pw_b", "attn_out_norm_w",
           "conv_out_norm_w", "w_out", "ffn_norm_w", "w_gate", "w_up", "w_down")
SMALL = tuple(n for n in WEIGHTS if n not in BIG_AXIS)


def _my_index():
    return 4 * lax.axis_index("x") + 2 * lax.axis_index("y") + lax.axis_index("c")


class _RidePlan:
    def __init__(self, srcs, gather):
        self.srcs = list(srcs)
        self.n = len(self.srcs)
        self.gather = list(gather) if isinstance(gather, (list, tuple)) else [gather] * self.n
        any_spec = pl.BlockSpec(memory_space=pl.ANY)
        self.in_specs = [any_spec] * self.n
        self.out_specs = [any_spec] * self.n
        self.out_shape = [jax.ShapeDtypeStruct(((N_DEV,) + s.shape) if g else s.shape, s.dtype)
                          for s, g in zip(self.srcs, self.gather)]
        npeer = N_DEV - 1
        self.scratch = [pltpu.SemaphoreType.DMA((self.n * npeer,)), pltpu.SemaphoreType.DMA((self.n * npeer,)),
                        pltpu.SemaphoreType.DMA((self.n,))] if self.n else []

    def copies(self, src_refs, out_refs, sems):
        if not self.n:
            return [], [], []
        send_sems, recv_sems, local_sems = sems
        npeer = N_DEV - 1
        x, y, c = lax.axis_index("x"), lax.axis_index("y"), lax.axis_index("c")
        me = 4 * x + 2 * y + c
        locals_, sends, recvs = [], [], []
        for a in range(self.n):
            src_ref, out_ref = src_refs[a], out_refs[a]

            def rows_for(j, src_ref=src_ref, gather=self.gather[a]):
                return src_ref if gather else src_ref.at[j]

            locals_.append(pltpu.make_async_copy(rows_for(me), out_ref.at[me], local_sems.at[a]))
            for k in range(1, N_DEV):
                px = (1 - x) if (k & 4) else x
                py = (1 - y) if (k & 2) else y
                pc = (1 - c) if (k & 1) else c
                pidx = 4 * px + 2 * py + pc
                common = dict(send_sem=send_sems.at[a * npeer + k - 1], recv_sem=recv_sems.at[a * npeer + k - 1],
                              device_id=(px, py, pc), device_id_type=pl.DeviceIdType.MESH)
                sends.append(pltpu.make_async_remote_copy(src_ref=rows_for(pidx), dst_ref=out_ref.at[me], **common))
                recvs.append(pltpu.make_async_remote_copy(src_ref=rows_for(pidx), dst_ref=out_ref.at[pidx],
                                                          **common))
        return locals_, sends, recvs


def _ride_plan(ride):
    return _RidePlan(*ride) if ride else _RidePlan([], True)


def _ride_start(copies, when=None):
    locals_, sends, _ = copies

    def go():
        for cp in locals_ + sends:
            cp.start()

    if locals_:
        go() if when is None else pl.when(when)(go)


def _ride_wait(copies, when=None):
    locals_, sends, recvs = copies

    def go():
        for cp in recvs:
            cp.wait_recv()
        for cp in sends:
            cp.wait_send()
        for cp in locals_:
            cp.wait()

    if locals_:
        go() if when is None else pl.when(when)(go)


def _exchange(srcs, gather, name):
    rd = _RidePlan(srcs, gather)

    def body(*refs):
        copies = rd.copies(refs[:rd.n], refs[rd.n:2 * rd.n], refs[2 * rd.n:])
        _ride_start(copies)
        _ride_wait(copies)

    return pl.pallas_call(body, name=name, in_specs=rd.in_specs, out_specs=rd.out_specs,
                          out_shape=rd.out_shape, scratch_shapes=rd.scratch)(*srcs)


def _adamw_math(w, g, m, v):
    m = ADAM_B1 * m + (1.0 - ADAM_B1) * g
    v = ADAM_B2 * v + (1.0 - ADAM_B2) * (g * g)
    m_hat = m / (1.0 - ADAM_B1 ** ADAM_STEP)
    v_hat = v / (1.0 - ADAM_B2 ** ADAM_STEP)
    delta = -ADAM_LR * (m_hat / (jnp.sqrt(v_hat) + ADAM_EPS) + ADAM_WD * w)
    return delta, m, v


def _sum_adamw(parts, w, m, v, name):
    _, k, n = w.shape
    tk = k
    for cand in (256, 176, 160, 128):
        if k % cand == 0:
            tk = cand
            break

    def body(*refs):
        p_refs = refs[:DEPTH]
        w_ref, m_ref, v_ref, g_ref, d_ref, mo_ref, vo_ref = refs[DEPTH:]
        for l in range(DEPTH):
            @pl.when(pl.program_id(0) == l)
            def _(p_ref=p_refs[l]):
                g = p_ref[0].astype(F32)
                for i in range(1, N_DEV):
                    g = g + p_ref[i].astype(F32)
                g_ref[...] = g
                d_ref[...], mo_ref[...], vo_ref[...] = _adamw_math(w_ref[...], g, m_ref[...], v_ref[...])

    row = pl.BlockSpec((None, tk, n), lambda l, i: (l, i, 0))
    shp = jax.ShapeDtypeStruct(w.shape, F32)
    return pl.pallas_call(
        body, name=name, grid=(DEPTH, k // tk),
        in_specs=[pl.BlockSpec((N_DEV, tk, n), lambda l, i: (0, i, 0))] * DEPTH + [row, row, row],
        out_specs=[row, row, row, row],
        out_shape=[shp, shp, shp, shp],
        compiler_params=_cparams(("parallel", "parallel")),
    )(*parts, w, m, v)


def _sum8(parts, name):
    r = parts.shape[1]

    def body(p_ref, g_ref):
        g = p_ref[0]
        for i in range(1, N_DEV):
            g = g + p_ref[i]
        g_ref[...] = g

    return pl.pallas_call(body, name=name, out_shape=jax.ShapeDtypeStruct((r, LANES), F32))(parts)


def _adamw(w, g, m, v):
    def body(w_ref, g_ref, m_ref, v_ref, d_ref, mo_ref, vo_ref):
        d_ref[...], mo_ref[...], vo_ref[...] = _adamw_math(w_ref[...], g_ref[...], m_ref[...], v_ref[...])

    shp = jax.ShapeDtypeStruct(w.shape, F32)
    return pl.pallas_call(body, name="adamw_small", out_shape=[shp, shp, shp])(w, g, m, v)


def _pack(arrays, dtype, row_multiple):
    flat = jnp.concatenate([a.reshape(-1).astype(dtype) for a in arrays])
    n = flat.shape[0]
    unit = row_multiple * LANES
    total = -(-n // unit) * unit
    return jnp.pad(flat, (0, total - n)).reshape(total // LANES, LANES)


def _unpack(flat2d, shapes, lead=()):
    flat = flat2d.reshape(lead + (-1,))
    out, off = [], 0
    for shp in shapes:
        n = int(np.prod(shp))
        out.append(flat[..., off:off + n].reshape(lead + tuple(shp)))
        off += n
    return out


def _shard_to_rows(full, axis):
    shp = full.shape
    k = shp[axis] // N_DEV
    r = full.reshape(shp[:axis] + (N_DEV, k) + shp[axis + 1:])
    return jnp.moveaxis(r, axis, 0)


def _rows_to_full(rows, axis):
    r = jnp.moveaxis(rows, 0, axis)
    shp = r.shape
    return r.reshape(shp[:axis] + (shp[axis] * shp[axis + 1],) + shp[axis + 2:])


def kernel(x, mix_norm_w, w_in, q_norm_w, k_norm_w, hgrn_lb_logits, hgrn_gnorm_w, conv_dw_w, conv_dw_b, conv_ln_w, conv_ln_b, conv_pw_w, conv_pw_b, attn_out_norm_w, conv_out_norm_w, w_out, ffn_norm_w, w_gate, w_up, w_down, loss_target, m_mix_norm_w, m_w_in, m_q_norm_w, m_k_norm_w, m_hgrn_lb_logits, m_hgrn_gnorm_w, m_conv_dw_w, m_conv_dw_b, m_conv_ln_w, m_conv_ln_b, m_conv_pw_w, m_conv_pw_b, m_attn_out_norm_w, m_conv_out_norm_w, m_w_out, m_ffn_norm_w, m_w_gate, m_w_up, m_w_down, v_mix_norm_w, v_w_in, v_q_norm_w, v_k_norm_w, v_hgrn_lb_logits, v_hgrn_gnorm_w, v_conv_dw_w, v_conv_dw_b, v_conv_ln_w, v_conv_ln_b, v_conv_pw_w, v_conv_pw_b, v_attn_out_norm_w, v_conv_out_norm_w, v_w_out, v_ffn_norm_w, v_w_gate, v_w_up, v_w_down):
    w_loc = dict(zip(WEIGHTS, (mix_norm_w, w_in, q_norm_w, k_norm_w, hgrn_lb_logits, hgrn_gnorm_w, conv_dw_w,
                               conv_dw_b, conv_ln_w, conv_ln_b, conv_pw_w, conv_pw_b, attn_out_norm_w,
                               conv_out_norm_w, w_out, ffn_norm_w, w_gate, w_up, w_down)))
    m_loc = dict(zip(WEIGHTS, (m_mix_norm_w, m_w_in, m_q_norm_w, m_k_norm_w, m_hgrn_lb_logits, m_hgrn_gnorm_w,
                               m_conv_dw_w, m_conv_dw_b, m_conv_ln_w, m_conv_ln_b, m_conv_pw_w, m_conv_pw_b,
                               m_attn_out_norm_w, m_conv_out_norm_w, m_w_out, m_ffn_norm_w, m_w_gate, m_w_up,
                               m_w_down)))
    v_loc = dict(zip(WEIGHTS, (v_mix_norm_w, v_w_in, v_q_norm_w, v_k_norm_w, v_hgrn_lb_logits, v_hgrn_gnorm_w,
                               v_conv_dw_w, v_conv_dw_b, v_conv_ln_w, v_conv_ln_b, v_conv_pw_w, v_conv_pw_b,
                               v_attn_out_norm_w, v_conv_out_norm_w, v_w_out, v_ffn_norm_w, v_w_gate, v_w_up,
                               v_w_down)))
    me = _my_index()
    big = tuple(BIG_AXIS)
    sms = tuple(SMALL_SHARD_AXIS)

    sm_shapes = [w_loc[n].shape for n in sms]
    got_s = _exchange([_pack([w_loc[n] for n in sms], F32, 8)], True, "gather_small_params")[0]
    p_full = {n: w_loc[n] for n in SMALL if n not in SMALL_SHARD_AXIS}
    for n, a in zip(sms, _unpack(got_s, sm_shapes, (N_DEV,))):
        p_full[n] = _rows_to_full(a, SMALL_SHARD_AXIS[n])

    col_sharded = tuple(n for n in big if BIG_AXIS[n] == 2)

    def shard_t(n, a):
        return jnp.swapaxes(a, 1, 2) if n in col_sharded else a

    def natural(n, gathered):
        if n in ("w_in", "w_out"):
            return gathered.reshape(-1, gathered.shape[-1])
        return gathered

    def to_send(n, gl):
        if n in ("w_in", "w_out"):
            return gl.reshape(N_DEV, gl.shape[0] // N_DEV, gl.shape[1]).astype(BF16)
        return gl

    class StepPlan:
        def __init__(self):
            self.w = [dict() for _ in range(DEPTH)]
            self.parts = [dict() for _ in range(DEPTH)]
            self.pending = {}
            got = _exchange([w_send["w_in"][0]], True, "gather_w_in")
            self.w[0]["w_in"] = natural("w_in", got[0])

        def ride(self, kernel_name, l, grads=None):
            want = []
            if kernel_name == "attn_fwd":
                want = [("w_out", l), ("w_gate", l)]
            elif kernel_name == "hgrn_fwd":
                want = [("w_up", l), ("w_down", l)]
            elif kernel_name == "ffn_fwd" and l + 1 < DEPTH:
                want = [("w_in", l + 1)]
            elif kernel_name == "ffn_bwd" and l + 1 < DEPTH:
                want = [("w_gate", l + 1), ("w_up", l + 1)]
            elif kernel_name == "attn_bwd" and l + 1 < DEPTH:
                want = [("w_in", l + 1), ("w_out", l + 1), ("w_down", l + 1)]
                if l == 0:
                    want += [("w_out", 0)]
            elif kernel_name == "hgrn_bwd" and l == 0:
                want = [(n, 0) for n in ("w_gate", "w_up", "w_down")]
            if not want:
                return None
            self.pending[(kernel_name, l)] = want
            if grads is None:
                return [w_send[n][wl] for n, wl in want], True
            return [to_send(n, grads[n][wl]) for n, wl in want], False

        def done(self, kernel_name, l, outs):
            want = self.pending.pop((kernel_name, l), [])
            for (n, wl), out in zip(want, outs):
                if kernel_name.endswith("_fwd"):
                    self.w[wl][n] = natural(n, out)
                else:
                    self.parts[wl][n] = out

    w_send = {n: shard_t(n, w_loc[n]).astype(BF16) for n in big}
    plan = StepPlan()
    loss_part, grad_x, g = _local_step(x, loss_target, p_full, plan)
    loss = lax.psum(loss_part[0, 0], MESH_AXES)

    pw = g["conv_pw_w"]
    k_pw = w_loc["conv_pw_w"].shape[1]
    pw_send = jnp.moveaxis(pw.reshape(DEPTH, N_DEV, k_pw, pw.shape[-1]), 1, 0).reshape(N_DEV, -1, LANES)
    gathered_small = [n for n in SMALL if n != "conv_pw_w"]
    small_shapes = [g[n].shape for n in gathered_small]
    din_parts, small_parts, pw_parts = _exchange(
        [to_send("w_in", g["w_in"][0]), _pack([g[n] for n in gathered_small], F32, 8), pw_send],
        [False, True, False], "exchange_last_grads")
    plan.parts[0]["w_in"] = din_parts
    big_out = {}
    for n in big:
        res = _sum_adamw([plan.parts[l][n] for l in range(DEPTH)], shard_t(n, w_loc[n]), shard_t(n, m_loc[n]),
                         shard_t(n, v_loc[n]), "sum_adamw_" + n)
        big_out[n] = [shard_t(n, r) for r in res]

    g_small = dict(zip(gathered_small, _unpack(_sum8(small_parts, "sum_small_grads"), small_shapes)))
    g_small["conv_pw_w"] = _sum8(pw_parts, "sum_conv_pw_grads").reshape(w_loc["conv_pw_w"].shape)
    for n in sms:
        if n == "conv_pw_w":
            continue
        ax = SMALL_SHARD_AXIS[n]
        k = w_loc[n].shape[ax]
        g_small[n] = lax.dynamic_slice_in_dim(g_small[n], me * k, k, axis=ax)
    loc_shapes = [w_loc[n].shape for n in SMALL]
    packed = [_pack([d[n] for n in SMALL], F32, 8) for d in (w_loc, g_small, m_loc, v_loc)]
    res = _adamw(*packed)
    small_out = [g_small] + [dict(zip(SMALL, _unpack(r, loc_shapes))) for r in res]

    def pick(i, n):
        return big_out[n][i] if n in BIG_AXIS else small_out[i][n]

    return (loss, grad_x) + tuple(pick(i, n) for i in range(4) for n in WEIGHTS)
```

```python
import functools

import jax
import jax.numpy as jnp
import numpy as np
from jax import lax
from jax.experimental import pallas as pl
from jax.experimental.pallas import tpu as pltpu

F32 = jnp.float32
BF16 = jnp.bfloat16

D_MODEL = 1024
D_ATTN = 512
D_HGRN = 256
D_CONV = 256
HEAD_DIM = 64
N_HEADS = 8
N_KV = 2
GRID_W = 64
ROPE_THETA = 10000.0
CHUNK = 64
F_MIN = 1e-6
CONV_W = 31
CONV_PAD = 15
D_FF = 2816
D_PROJ = 2560
EPS = 1e-6
LN_EPS = 1e-5
DEPTH = 2
ADAM_LR = 0.001
ADAM_B1 = 0.9
ADAM_B2 = 0.999
ADAM_EPS = 1e-08
ADAM_WD = 0.01
ADAM_STEP = 10
N_DEV = 8
MESH_AXES = ("x", "y", "c")

COL_HQ, COL_ZFW, COL_ZBW, COL_HI, COL_HG = 6, 8, 10, 12, 14
COL_CA, COL_CB = 8, 9

LANES = 128
VMEM_LIMIT_MB = 56


def _cparams(dims=None):
    return pltpu.CompilerParams(dimension_semantics=dims, vmem_limit_bytes=VMEM_LIMIT_MB * 2 ** 20)


def _dot(a, b):
    return jnp.dot(a, b, preferred_element_type=F32)


def _dot_nt(a, b):
    return lax.dot_general(a, b, (((1,), (1,)), ((), ())), preferred_element_type=F32)


def _dot_tn(a, b):
    return lax.dot_general(a, b, (((0,), (0,)), ((), ())), preferred_element_type=F32)


def _split_bf16(x, parts):
    out = []
    r = x
    for _ in range(parts):
        p = r.astype(BF16)
        out.append(p)
        r = r - p.astype(F32)
    return out


def _dot_precise(x, m_bf16, parts=3):
    acc = None
    for p in _split_bf16(x, parts):
        t = _dot(p, m_bf16)
        acc = t if acc is None else acc + t
    return acc


def _block_ones(width, group):
    i = np.arange(width)
    return jnp.asarray((i[:, None] // group) == (i[None, :] // group), dtype=BF16)


def _sigmoid(x):
    return 1.0 / (1.0 + jnp.exp(-x))


def _rot(x):
    w = x.shape[1]
    lane = lax.broadcasted_iota(jnp.int32, x.shape, 1)
    first = (lane % 32) < 16
    return jnp.where(first, -pltpu.roll(x, w - 16, 1), pltpu.roll(x, 16, 1))


def _rope(x, cos, sin):
    return x * cos + _rot(x) * sin


def _rope_t(dy, cos, sin):
    return dy * cos - _rot(dy * sin)


def _row(v):
    return v.reshape(1, -1)


def _rms_proj(x, wn, wt, tm):
    t, d = x.shape
    n = wt.shape[0]

    def body(x_ref, wn_ref, w_ref, h_ref, y_ref):
        xv = x_ref[...]
        r = lax.rsqrt(jnp.mean(xv * xv, axis=-1, keepdims=True) + EPS)
        h = (xv * r * wn_ref[...]).astype(BF16)
        h_ref[...] = h
        y_ref[...] = _dot_nt(h, w_ref[...])

    return pl.pallas_call(
        body, name="rms_proj", grid=(t // tm,),
        in_specs=[pl.BlockSpec((tm, d), lambda i: (i, 0)),
                  pl.BlockSpec((1, d), lambda i: (0, 0)),
                  pl.BlockSpec((n, d), lambda i: (0, 0))],
        out_specs=[pl.BlockSpec((tm, d), lambda i: (i, 0)),
                   pl.BlockSpec((tm, n), lambda i: (i, 0))],
        out_shape=[jax.ShapeDtypeStruct((t, d), BF16), jax.ShapeDtypeStruct((t, n), F32)],
        compiler_params=_cparams(("parallel",)),
    )(x, wn, wt)


def _rms_bwd(dh, x, wn):
    r = lax.rsqrt(jnp.mean(x * x, axis=-1, keepdims=True) + EPS)
    g = dh * wn
    dx = r * (g - x * (r * r) * jnp.mean(g * x, axis=-1, keepdims=True))
    return dx, dh * x * r


def _proj_bwd(pieces, wt, x, wn, dres, tm):
    t = x.shape[0]
    d = x.shape[1]
    n = wt.shape[0]
    widths = [p.shape[1] for p in pieces]
    offs = [sum(widths[:i]) for i in range(len(widths))]
    assert sum(widths) == n
    npc = len(pieces)

    def body(*refs):
        p_refs = refs[:npc]
        w_ref, x_ref, wn_ref, dr_ref, dx_ref, dwn_ref = refs[npc:]
        dh = None
        for p_ref, o, wd in zip(p_refs, offs, widths):
            part = _dot(p_ref[...], w_ref[o:o + wd, :])
            dh = part if dh is None else dh + part
        dx, dwn = _rms_bwd(dh, x_ref[...], wn_ref[...])
        dx_ref[...] = dr_ref[...] + dx

        @pl.when(pl.program_id(0) == 0)
        def _():
            dwn_ref[...] = jnp.zeros_like(dwn_ref)

        dwn_ref[...] += jnp.sum(dwn, axis=0, keepdims=True)

    return pl.pallas_call(
        body, name="proj_bwd", grid=(t // tm,),
        in_specs=[pl.BlockSpec((tm, wd), lambda i: (i, 0)) for wd in widths]
        + [pl.BlockSpec((n, d), lambda i: (0, 0)),
           pl.BlockSpec((tm, d), lambda i: (i, 0)),
           pl.BlockSpec((1, d), lambda i: (0, 0)),
           pl.BlockSpec((tm, d), lambda i: (i, 0))],
        out_specs=[pl.BlockSpec((tm, d), lambda i: (i, 0)),
                   pl.BlockSpec((1, d), lambda i: (0, 0))],
        out_shape=[jax.ShapeDtypeStruct((t, d), F32), jax.ShapeDtypeStruct((1, d), F32)],
        compiler_params=_cparams(("arbitrary",)),
    )(*pieces, wt, x, wn, dres)


def _dw_in(h0, pieces, tm):
    t, k = h0.shape
    widths = [p.shape[1] for p in pieces]
    offs = [sum(widths[:i]) for i in range(len(widths))]
    n = sum(widths)
    npc = len(pieces)

    def body(*refs):
        h_ref = refs[0]
        p_refs = refs[1:1 + npc]
        o_ref, acc_ref = refs[1 + npc:]
        i = pl.program_id(0)

        @pl.when(i == 0)
        def _():
            acc_ref[...] = jnp.zeros_like(acc_ref)

        h = h_ref[...]
        for p_ref, o, wd in zip(p_refs, offs, widths):
            acc_ref[o:o + wd, :] += _dot_tn(p_ref[...], h)

        @pl.when(i == t // tm - 1)
        def _():
            o_ref[...] = acc_ref[...].astype(BF16)

    return pl.pallas_call(
        body, name="dw_in", grid=(t // tm,),
        in_specs=[pl.BlockSpec((tm, k), lambda i: (i, 0))]
        + [pl.BlockSpec((tm, wd), lambda i: (i, 0)) for wd in widths],
        out_specs=pl.BlockSpec((n, k), lambda i: (0, 0)),
        out_shape=jax.ShapeDtypeStruct((n, k), BF16),
        scratch_shapes=[pltpu.VMEM((n, k), F32)],
        compiler_params=_cparams(("arbitrary",)),
    )(h0, *pieces)


def _mm_tn(a, b, tn, name, tm):
    t, k = a.shape
    n = b.shape[1]

    def body(a_ref, b_ref, o_ref):
        @pl.when(pl.program_id(1) == 0)
        def _():
            o_ref[...] = jnp.zeros_like(o_ref)

        o_ref[...] += _dot_tn(a_ref[...].astype(BF16), b_ref[...].astype(BF16))

    return pl.pallas_call(
        body, name=name, grid=(n // tn, t // tm),
        in_specs=[pl.BlockSpec((tm, k), lambda j, i: (i, 0)),
                  pl.BlockSpec((tm, tn), lambda j, i: (i, j))],
        out_specs=pl.BlockSpec((k, tn), lambda j, i: (0, j)),
        out_shape=jax.ShapeDtypeStruct((k, n), F32),
        compiler_params=_cparams(("parallel", "arbitrary")),
    )(a, b)


def _dw_ff(blocked, flat, name, tm):
    t, dm = flat.shape
    fb = blocked.shape[2]
    out_blk = (N_DEV, fb, dm)

    def body(b_ref, f_ref, o_ref, acc_ref):
        i = pl.program_id(0)

        @pl.when(i == 0)
        def _():
            acc_ref[...] = jnp.zeros_like(acc_ref)

        fv = f_ref[...].astype(BF16)
        for j in range(N_DEV):
            acc_ref[j] += _dot_tn(b_ref[j], fv)

        @pl.when(i == t // tm - 1)
        def _():
            o_ref[...] = acc_ref[...].astype(BF16)

    return pl.pallas_call(
        body, name=name, grid=(t // tm,),
        in_specs=[pl.BlockSpec((N_DEV, tm, fb), lambda i: (0, i, 0)),
                  pl.BlockSpec((tm, dm), lambda i: (i, 0))],
        out_specs=pl.BlockSpec(out_blk, lambda i: (0, 0, 0)),
        out_shape=jax.ShapeDtypeStruct(out_blk, BF16),
        scratch_shapes=[pltpu.VMEM(out_blk, F32)],
        compiler_params=_cparams(("arbitrary",)),
    )(blocked, flat)


def _rope_tables(s):
    rows = s // GRID_W
    row_id = jnp.repeat(jnp.arange(rows, dtype=F32), GRID_W)
    col_id = jnp.tile(jnp.arange(GRID_W, dtype=F32), rows)
    half = HEAD_DIM // 2
    inv_freq = ROPE_THETA ** (-jnp.arange(0, half, 2, dtype=F32) / half)
    ang_r = row_id[:, None] * inv_freq[None, :]
    ang_c = col_id[:, None] * inv_freq[None, :]
    ang = jnp.concatenate([ang_r, ang_r, ang_c, ang_c], axis=-1)
    cos, sin = jnp.cos(ang), jnp.sin(ang)
    return jnp.tile(cos, (1, N_HEADS)), jnp.tile(sin, (1, N_HEADS))


def _head_rms(x, w, ones):
    r = lax.rsqrt(_dot_precise(x * x, ones, 2) * (1.0 / HEAD_DIM) + EPS)
    return x * r * w, r


def _dup_half(x, kv):
    lane = lax.broadcasted_iota(jnp.int32, x.shape, 1)
    sel = (lane < 64) if kv == 0 else (lane >= 64)
    return jnp.where(sel, x, pltpu.roll(x, 64, 1))


def _qkv_prep(proj, cosq, sinq, qw, kw, ones, s, tm):
    t = proj.shape[0]
    ns = s // tm

    def body(p_ref, cos_ref, sin_ref, qw_ref, kw_ref, ones_ref, q_out, kd_out, vd_out, kdt_out, vdt_out):
        cos = cos_ref[...]
        sin = sin_ref[...]
        ones_m = ones_ref[...]
        qn, _ = _head_rms(p_ref[:, 0:512], qw_ref[...], ones_m)
        q_out[...] = (_rope(qn, cos, sin) * (HEAD_DIM ** -0.5)).astype(BF16)
        kn, _ = _head_rms(p_ref[:, 512:640], kw_ref[...], ones_m[0:128, 0:128])
        kr = _rope(kn, cos[:, 0:128], sin[:, 0:128])
        v = p_ref[:, 640:768]
        for kv in range(N_KV):
            kd = _dup_half(kr, kv)
            vd = _dup_half(v, kv)
            kd_out[kv] = kd.astype(BF16)
            vd_out[kv] = vd.astype(BF16)
            kdt_out[kv] = kd.T.astype(BF16)
            vdt_out[kv] = vd.T.astype(BF16)

    return pl.pallas_call(
        body, name="qkv_prep", grid=(t // tm,),
        in_specs=[pl.BlockSpec((tm, 768), lambda i: (i, 0)),
                  pl.BlockSpec((tm, 512), lambda i: (i % ns, 0)),
                  pl.BlockSpec((tm, 512), lambda i: (i % ns, 0)),
                  pl.BlockSpec((1, 512), lambda i: (0, 0)),
                  pl.BlockSpec((1, 128), lambda i: (0, 0)),
                  pl.BlockSpec((512, 512), lambda i: (0, 0))],
        out_specs=[pl.BlockSpec((tm, 512), lambda i: (i, 0)),
                   pl.BlockSpec((N_KV, tm, 128), lambda i: (0, i, 0)),
                   pl.BlockSpec((N_KV, tm, 128), lambda i: (0, i, 0)),
                   pl.BlockSpec((N_KV, 128, tm), lambda i: (0, 0, i)),
                   pl.BlockSpec((N_KV, 128, tm), lambda i: (0, 0, i))],
        out_shape=[jax.ShapeDtypeStruct((t, 512), BF16),
                   jax.ShapeDtypeStruct((N_KV, t, 128), BF16),
                   jax.ShapeDtypeStruct((N_KV, t, 128), BF16),
                   jax.ShapeDtypeStruct((N_KV, 128, t), BF16),
                   jax.ShapeDtypeStruct((N_KV, 128, t), BF16)],
        compiler_params=_cparams(("parallel",)),
    )(proj, cosq, sinq, qw, kw, ones)


def _qkv_bwd(proj, dq, dkd, dvd, cosq, sinq, qw, kw, ones, s, tm):
    t = proj.shape[0]
    ns = s // tm

    def body(p_ref, dq_ref, dkd_ref, dvd_ref, cos_ref, sin_ref, qw_ref, kw_ref, ones_ref,
             out_ref, dqw_ref, dkw_ref):
        cos = cos_ref[...]
        sin = sin_ref[...]
        ones_m = ones_ref[...]
        ones_k = ones_m[0:128, 0:128]

        def norm_bwd(x, w, dn, om):
            r = lax.rsqrt(_dot_precise(x * x, om, 2) * (1.0 / HEAD_DIM) + EPS)
            g = dn * w
            dx = r * (g - x * (r * r) * (_dot_precise(g * x, om, 2) * (1.0 / HEAD_DIM)))
            return dx, jnp.sum(dn * x * r, axis=0, keepdims=True)

        q = p_ref[:, 0:512]
        dqn = _rope_t(dq_ref[...], cos, sin) * (HEAD_DIM ** -0.5)
        dq_raw, dqw = norm_bwd(q, qw_ref[...], dqn, ones_m)
        out_ref[:, 0:512] = dq_raw.astype(BF16)

        lane = lax.broadcasted_iota(jnp.int32, (tm, 128), 1)

        def fold(ref):
            a0 = ref[0]
            a1 = ref[1]
            f0 = a0 + pltpu.roll(a0, 64, 1)
            f1 = a1 + pltpu.roll(a1, 64, 1)
            return jnp.where(lane < 64, f0, f1)

        k = p_ref[:, 512:640]
        dkn = _rope_t(fold(dkd_ref), cos[:, 0:128], sin[:, 0:128])
        dk_raw, dkw = norm_bwd(k, kw_ref[...], dkn, ones_k)
        out_ref[:, 512:640] = dk_raw.astype(BF16)
        out_ref[:, 640:768] = fold(dvd_ref).astype(BF16)

        @pl.when(pl.program_id(0) == 0)
        def _():
            dqw_ref[...] = jnp.zeros_like(dqw_ref)
            dkw_ref[...] = jnp.zeros_like(dkw_ref)

        dqw_ref[...] += dqw
        dkw_ref[...] += dkw

    return pl.pallas_call(
        body, name="qkv_bwd", grid=(t // tm,),
        in_specs=[pl.BlockSpec((tm, 768), lambda i: (i, 0)),
                  pl.BlockSpec((tm, 512), lambda i: (i, 0)),
                  pl.BlockSpec((N_KV, tm, 128), lambda i: (0, i, 0)),
                  pl.BlockSpec((N_KV, tm, 128), lambda i: (0, i, 0)),
                  pl.BlockSpec((tm, 512), lambda i: (i % ns, 0)),
                  pl.BlockSpec((tm, 512), lambda i: (i % ns, 0)),
                  pl.BlockSpec((1, 512), lambda i: (0, 0)),
                  pl.BlockSpec((1, 128), lambda i: (0, 0)),
                  pl.BlockSpec((512, 512), lambda i: (0, 0))],
        out_specs=[pl.BlockSpec((tm, 768), lambda i: (i, 0)),
                   pl.BlockSpec((1, 512), lambda i: (0, 0)),
                   pl.BlockSpec((1, 128), lambda i: (0, 0))],
        out_shape=[jax.ShapeDtypeStruct((t, 768), BF16),
                   jax.ShapeDtypeStruct((1, 512), F32),
                   jax.ShapeDtypeStruct((1, 128), F32)],
        compiler_params=_cparams(("arbitrary",)),
    )(proj, dq, dkd, dvd, cosq, sinq, qw, kw, ones)


def _grid_step_id(grid):
    idx = pl.program_id(0)
    for ax in range(1, len(grid)):
        idx = idx * grid[ax] + pl.program_id(ax)
    return idx


def _attn_fwd(q, kd, vdt, nb, s, tq, ride=None):
    t = q.shape[0]
    nq = s // tq
    rd = _ride_plan(ride)
    grid = (nb, N_HEADS // 2, nq)
    nsteps = nb * (N_HEADS // 2) * nq

    def body(*refs):
        q_ref, k_ref, vt_ref = refs[:3]
        o_ref, lse_ref = refs[3 + rd.n:5 + rd.n]
        copies = rd.copies(refs[3:3 + rd.n], refs[5 + rd.n:5 + 2 * rd.n], refs[5 + 2 * rd.n:])
        step_id = _grid_step_id(grid)
        _ride_start(copies, step_id == 0)
        qv = q_ref[...].astype(F32)
        lane = lax.broadcasted_iota(jnp.int32, qv.shape, 1)
        k = k_ref[0]
        vt = vt_ref[0]
        outs = []
        scores = [_dot_nt(k, jnp.where((lane < 64) if half == 0 else (lane >= 64), qv, 0.0).astype(BF16))
                  for half in range(2)]
        for half in range(2):
            st = scores[half]
            m = jnp.max(st, axis=0, keepdims=True)
            p = jnp.exp(st - m)
            l = jnp.sum(p, axis=0, keepdims=True)
            ot = _dot(vt, p.astype(BF16)) / l
            lse_ref[0, half] = m + jnp.log(l)
            outs.append(ot)
        row = lax.broadcasted_iota(jnp.int32, outs[0].shape, 0)
        o_ref[...] = jnp.where(row < 64, outs[0], outs[1]).T
        _ride_wait(copies, step_id == nsteps - 1)

    return pl.pallas_call(
        body, name="attn_fwd", grid=grid,
        in_specs=[pl.BlockSpec((tq, 128), lambda b, p, i: (b * nq + i, p)),
                  pl.BlockSpec((1, s, 128), lambda b, p, i: (p // 2, b, 0)),
                  pl.BlockSpec((1, 128, s), lambda b, p, i: (p // 2, 0, b))] + rd.in_specs,
        out_specs=[pl.BlockSpec((tq, 128), lambda b, p, i: (b * nq + i, p)),
                   pl.BlockSpec((1, 2, 1, tq), lambda b, p, i: (b, p, 0, i))] + rd.out_specs,
        out_shape=[jax.ShapeDtypeStruct((t, D_ATTN), F32),
                   jax.ShapeDtypeStruct((nb, N_HEADS, 1, s), F32)] + rd.out_shape,
        scratch_shapes=rd.scratch,
        compiler_params=_cparams(("arbitrary", "arbitrary", "arbitrary")),
    )(q, kd, vdt, *rd.srcs)


def _attn_bwd(q, kd, vd, kdt, o, lse, do, nb, s, tq, ride=None):
    t = q.shape[0]
    nq = s // tq
    ones8 = jnp.ones((8, 128), BF16)
    rd = _ride_plan(ride)
    grid = (nb, N_KV, 2, nq)
    nsteps = nb * N_KV * 2 * nq

    def body(*refs):
        q_ref, k_ref, v_ref, kt_ref, o_ref, lse_ref, do_ref, ones_ref = refs[:8]
        dq_ref, dk_ref, dv_ref = refs[8 + rd.n:11 + rd.n]
        copies = rd.copies(refs[8:8 + rd.n], refs[11 + rd.n:11 + 2 * rd.n], refs[11 + 2 * rd.n:])
        step_id = _grid_step_id(grid)
        _ride_start(copies, step_id == 0)

        @pl.when((pl.program_id(2) == 0) & (pl.program_id(3) == 0))
        def _():
            dk_ref[...] = jnp.zeros_like(dk_ref)
            dv_ref[...] = jnp.zeros_like(dv_ref)

        qv = q_ref[...].astype(F32)
        dov = do_ref[...]
        ov = o_ref[...]
        lane = lax.broadcasted_iota(jnp.int32, qv.shape, 1)
        k = k_ref[0]
        v = v_ref[0]
        kt = kt_ref[0]
        dqs = []
        dk_acc = None
        dv_acc = None
        for half in range(2):
            sel = (lane < 64) if half == 0 else (lane >= 64)
            qh = jnp.where(sel, qv, 0.0).astype(BF16)
            doh = jnp.where(sel, dov, 0.0)
            dob = doh.astype(BF16)
            delta = None
            for part in _split_bf16(doh * ov, 3):
                d8 = _dot_nt(ones_ref[...], part)
                delta = d8 if delta is None else delta + d8
            delta = delta[0:1, :]
            st = _dot_nt(k, qh)
            pt = jnp.exp(st - lse_ref[0, half])
            dpt = _dot_nt(v, dob)
            dst = (pt * (dpt - delta)).astype(BF16)
            dkh = _dot(dst, qh)
            dvh = _dot(pt.astype(BF16), dob)
            dk_acc = dkh if dk_acc is None else dk_acc + dkh
            dv_acc = dvh if dv_acc is None else dv_acc + dvh
            dqs.append(_dot(kt, dst))
        dk_ref[0] += dk_acc
        dv_ref[0] += dv_acc
        row = lax.broadcasted_iota(jnp.int32, dqs[0].shape, 0)
        dq_ref[...] = jnp.where(row < 64, dqs[0], dqs[1]).T
        _ride_wait(copies, step_id == nsteps - 1)

    qmap = lambda b, g, p, i: (b * nq + i, g * 2 + p)
    kvmap = lambda b, g, p, i: (g, b, 0)
    return pl.pallas_call(
        body, name="attn_bwd", grid=grid,
        in_specs=[pl.BlockSpec((tq, 128), qmap),
                  pl.BlockSpec((1, s, 128), kvmap),
                  pl.BlockSpec((1, s, 128), kvmap),
                  pl.BlockSpec((1, 128, s), lambda b, g, p, i: (g, 0, b)),
                  pl.BlockSpec((tq, 128), qmap),
                  pl.BlockSpec((1, 2, 1, tq), lambda b, g, p, i: (b, g * 2 + p, 0, i)),
                  pl.BlockSpec((tq, 128), qmap),
                  pl.BlockSpec((8, 128), lambda b, g, p, i: (0, 0))] + rd.in_specs,
        out_specs=[pl.BlockSpec((tq, 128), qmap),
                   pl.BlockSpec((1, s, 128), kvmap),
                   pl.BlockSpec((1, s, 128), kvmap)] + rd.out_specs,
        out_shape=[jax.ShapeDtypeStruct((t, D_ATTN), F32),
                   jax.ShapeDtypeStruct((N_KV, t, 128), F32),
                   jax.ShapeDtypeStruct((N_KV, t, 128), F32)] + rd.out_shape,
        scratch_shapes=rd.scratch,
        compiler_params=_cparams(("arbitrary", "arbitrary", "arbitrary", "arbitrary")),
    )(q, kd, vd, kdt, o, lse, do, ones8, *rd.srcs)


SUB = 16
N_SUB = CHUNK // SUB


def _tri_mats():
    i = np.arange(CHUNK)
    same = (i[:, None] // SUB) == (i[None, :] // SUB)
    lower = jnp.asarray(same & (i[:, None] >= i[None, :]), dtype=BF16)
    upper = jnp.asarray(same & (i[:, None] <= i[None, :]), dtype=BF16)
    return jnp.stack([lower, upper])


def _running_sum(tri, x):
    acc = None
    for part in _split_bf16(x, 3):
        t = _dot(tri, part)
        acc = t if acc is None else acc + t
    return acc


def _gates(z, lb):
    sig = _sigmoid(z)
    f = lb + (1.0 - lb) * sig
    logf = jnp.log(jnp.maximum(f, F_MIN))
    k = (1.0 - lb) * (1.0 - sig)
    return sig, f, logf, k


def _row_group(jg, anti):
    if anti:
        return 0, 8 * jg + 8
    return 8 * jg, SUB


def _sub_order(anti):
    return range(N_SUB - 1, -1, -1) if anti else range(N_SUB)


def _block_columns(b, anti):
    tt = lax.broadcasted_iota(jnp.int32, (SUB, LANES), 0)
    cols = []
    for jg in range(SUB // 8):
        r0, r1 = _row_group(jg, anti)
        br = b[r0:r1]
        tr = tt[r0:r1]
        for i in range(8):
            sc = 8 * jg + i
            mask = (tr <= sc) if anti else (tr >= sc)
            cols.append((r0, r1, sc, jnp.where(mask, jnp.exp(jnp.minimum(br - b[sc:sc + 1], 0.0)), 0.0)))
    return cols


def _lockstep(gens):
    results = [None] * len(gens)
    live = list(range(len(gens)))
    while live:
        for i in list(live):
            try:
                next(gens[i])
            except StopIteration as stop:
                results[i] = stop.value
                live.remove(i)
    return results


def _scatter_rows(base, accs):
    pieces = []
    for g in range(SUB // 8):
        tot = base[8 * g:8 * g + 8]
        for (r0, r1), acc in accs.items():
            if r0 <= 8 * g and 8 * g + 8 <= r1:
                tot = tot + acc[8 * g - r0:8 * g - r0 + 8]
        pieces.append(tot)
    return jnp.concatenate(pieces, axis=0)


def _chunk_fwd(q, k, v, b, st, bones, bmask, anti):
    rs = [slice(SUB * i, SUB * i + SUB) for i in range(N_SUB)]
    decay, update, prods, spans, qbs = [], [], [], [], []
    for i in range(N_SUB):
        qi, ki, vi, bi = q[rs[i]], k[rs[i]], v[rs[i]], b[rs[i]]
        b_last = bi[0:1] if anti else bi[SUB - 1:SUB]
        decay.append(jnp.exp(b_last))
        update.append(_dot_tn(vi.astype(BF16), (ki * jnp.exp(b_last - bi)).astype(BF16)) * bmask)
        qbs.append((qi * jnp.exp(bi)).astype(BF16))
        for r0, r1, sc, e in _block_columns(bi, anti):
            prods.append(qi[r0:r1] * e * ki[sc:sc + 1])
            spans.append((i, r0, r1, sc))
    pb = _dot(jnp.concatenate(prods, axis=0).astype(BF16), bones)
    yield
    entered = [None] * N_SUB
    for i in _sub_order(anti):
        entered[i] = st
        st = st * decay[i] + update[i]
    yield
    accs = [dict() for _ in range(N_SUB)]
    off = 0
    for i, r0, r1, sc in spans:
        term = pb[off:off + r1 - r0] * v[rs[i]][sc:sc + 1]
        off += r1 - r0
        accs[i][(r0, r1)] = term if (r0, r1) not in accs[i] else accs[i][(r0, r1)] + term
    outs = [_scatter_rows(_dot_nt(qbs[i], entered[i].astype(BF16)), accs[i]) for i in range(N_SUB)]
    return jnp.concatenate(outs, axis=0), st, entered


def _hgrn_fwd(proj, lb, gw, nb, s, ride=None):
    t = proj.shape[0]
    nc = s // CHUNK
    tri = _tri_mats()
    bones = _block_ones(LANES, HEAD_DIM)
    rd = _ride_plan(ride)

    def body(*refs):
        q_ref, zf_ref, zb_ref, v_ref, g_ref, lb_ref, gw_ref, tri_ref, bones_ref = refs[:9]
        y_ref, os_ref, sts_ref = refs[9 + rd.n:12 + rd.n]
        st_ref = refs[12 + 2 * rd.n]
        copies = rd.copies(refs[9:9 + rd.n], refs[12 + rd.n:12 + 2 * rd.n], refs[13 + 2 * rd.n:])
        step_id = pl.program_id(0) * 2 + pl.program_id(1)
        _ride_start(copies, step_id == 0)
        bones_m = bones_ref[...]
        bmask = bones_m.astype(F32)
        st_ref[...] = jnp.zeros_like(st_ref)

        def one_direction(n, anti):
            side = 1 if anti else 0
            z_ref = zb_ref if anti else zf_ref
            cn = (nc - 1 - n) if anti else n
            rows = pl.ds(pl.multiple_of(cn * CHUNK, CHUNK), CHUNK)
            q = q_ref[rows, :]
            v = v_ref[rows, :]
            _, _, logf, k = _gates(z_ref[rows, :], lb_ref[side:side + 1])
            b = _running_sum(tri_ref[side], logf)
            yield
            o, st_new, entered = yield from _chunk_fwd(q, k, v, b, st_ref[side], bones_m, bmask, anti)
            for i in range(N_SUB):
                sts_ref[0, 0, side, cn * N_SUB + i] = entered[i].astype(BF16)
            st_ref[side] = st_new
            (y_ref if anti else os_ref)[rows, :] = o

        def step(n, carry):
            _lockstep([one_direction(n, False), one_direction(n, True)])
            return carry

        lax.fori_loop(0, nc, step, 0)

        def join(n, carry):
            rows = pl.ds(pl.multiple_of(n * CHUNK, CHUNK), CHUNK)
            osum = os_ref[rows, :] + y_ref[rows, :]
            os_ref[rows, :] = osum
            r = lax.rsqrt(_dot_precise(osum * osum, bones_m, 2) * (1.0 / HEAD_DIM) + EPS)
            hg = g_ref[rows, :]
            y_ref[rows, :] = osum * r * gw_ref[...] * (hg * _sigmoid(hg))
            return carry

        lax.fori_loop(0, nc, join, 0)
        _ride_wait(copies, step_id == nb * 2 - 1)

    def col(c):
        return pl.BlockSpec((s, LANES), lambda b, p, c=c: (b, c + p))

    return pl.pallas_call(
        body, name="hgrn_fwd", grid=(nb, 2),
        in_specs=[col(COL_HQ), col(COL_ZFW), col(COL_ZBW), col(COL_HI), col(COL_HG),
                  pl.BlockSpec((2, LANES), lambda b, p: (0, p)),
                  pl.BlockSpec((1, LANES), lambda b, p: (0, 0)),
                  pl.BlockSpec((2, CHUNK, CHUNK), lambda b, p: (0, 0, 0)),
                  pl.BlockSpec((LANES, LANES), lambda b, p: (0, 0))] + rd.in_specs,
        out_specs=[pl.BlockSpec((s, LANES), lambda b, p: (b, p)),
                   pl.BlockSpec((s, LANES), lambda b, p: (b, p)),
                   pl.BlockSpec((1, 1, 2, nc * N_SUB, LANES, LANES), lambda b, p: (b, p, 0, 0, 0, 0))]
        + rd.out_specs,
        out_shape=[jax.ShapeDtypeStruct((t, D_HGRN), F32), jax.ShapeDtypeStruct((t, D_HGRN), F32),
                   jax.ShapeDtypeStruct((nb, 2, 2, nc * N_SUB, LANES, LANES), BF16)] + rd.out_shape,
        scratch_shapes=[pltpu.VMEM((2, LANES, LANES), F32)] + rd.scratch,
        compiler_params=_cparams(("arbitrary", "arbitrary")),
    )(proj, proj, proj, proj, proj, lb, gw, tri, bones, *rd.srcs)


def _chunk_bwd(q, k, v, b, do, states, rt, bones, bmask, anti):
    rs = [slice(SUB * i, SUB * i + SUB) for i in range(N_SUB)]
    r8 = lax.broadcasted_iota(jnp.int32, (8, LANES), 0)
    decay, update, dq_inter, ebls, prods_p, prods_d, spans, qes, kes = [], [], [], [], [], [], [], [], []
    for i in range(N_SUB):
        qi, ki, vi, bi, doi = q[rs[i]], k[rs[i]], v[rs[i]], b[rs[i]], do[rs[i]]
        b_last = bi[0:1] if anti else bi[SUB - 1:SUB]
        eb = jnp.exp(bi)
        dob = doi.astype(BF16)
        decay.append(jnp.exp(b_last))
        ebls.append(jnp.exp(b_last - bi))
        update.append(_dot_tn(dob, (qi * eb).astype(BF16)) * bmask)
        dq_inter.append(eb * _dot(dob, states[i]))
        for r0, r1, sc, e in _block_columns(bi, anti):
            qe = qi[r0:r1] * e
            qes.append(qe)
            kes.append(e * ki[sc:sc + 1])
            prods_p.append(qe * ki[sc:sc + 1])
            prods_d.append(doi[r0:r1] * vi[sc:sc + 1])
            spans.append((i, r0, r1, sc))
    sums = _dot(jnp.concatenate(prods_p + prods_d, axis=0).astype(BF16), bones)
    half = sum(r1 - r0 for _, r0, r1, _ in spans)
    yield
    entered = [None] * N_SUB
    for i in reversed(list(_sub_order(anti))):
        entered[i] = rt
        rt = rt * decay[i] + update[i]
    yield
    accs = [dict() for _ in range(N_SUB)]
    dk_blks = [[jnp.zeros((8, LANES), F32) for _ in range(SUB // 8)] for _ in range(N_SUB)]
    dv_blks = [[jnp.zeros((8, LANES), F32) for _ in range(SUB // 8)] for _ in range(N_SUB)]
    off = 0
    for n, (i, r0, r1, sc) in enumerate(spans):
        nr = r1 - r0
        pb = sums[off:off + nr]
        dpb = sums[half + off:half + off + nr]
        off += nr
        term = dpb * kes[n]
        accs[i][(r0, r1)] = term if (r0, r1) not in accs[i] else accs[i][(r0, r1)] + term
        dk_s = jnp.sum(dpb * qes[n], axis=0, keepdims=True)
        dv_s = jnp.sum(pb * do[rs[i]][r0:r1], axis=0, keepdims=True)
        dk_blks[i][sc // 8] = jnp.where(r8 == sc % 8, dk_s, dk_blks[i][sc // 8])
        dv_blks[i][sc // 8] = jnp.where(r8 == sc % 8, dv_s, dv_blks[i][sc // 8])
    dqs, dks, dvs, dbs = [], [], [], []
    for i in range(N_SUB):
        ki, vi = k[rs[i]], v[rs[i]]
        rtb = entered[i].astype(BF16)
        dk_inter = ebls[i] * _dot(vi.astype(BF16), rtb)
        dv_inter = _dot_nt((ki * ebls[i]).astype(BF16), rtb)
        dqs.append(_scatter_rows(dq_inter[i], accs[i]))
        dks.append(dk_inter + jnp.concatenate(dk_blks[i], axis=0))
        dvs.append(dv_inter + jnp.concatenate(dv_blks[i], axis=0))
        db_last = (jnp.sum(ki * dk_inter, axis=0, keepdims=True)
                   + decay[i] * jnp.sum(entered[i] * states[i].astype(F32), axis=0, keepdims=True))
        dbs.append(jnp.broadcast_to(db_last, (SUB, LANES)))
    cat = lambda xs: jnp.concatenate(xs, axis=0)
    return cat(dqs), cat(dks), cat(dvs), rt, cat(dbs)


def _hgrn_bwd(proj, lb, gw, osum, states, dy, nb, s, ride=None):
    t = proj.shape[0]
    nc = s // CHUNK
    assert nc % 2 == 0
    tri = _tri_mats()
    bones = _block_ones(LANES, HEAD_DIM)
    rd = _ride_plan(ride)

    def body(*refs):
        (q_ref, zf_ref, zb_ref, v_ref, g_ref, lb_ref, gw_ref, os_ref, sts_ref, dy_ref, tri_ref,
         bones_ref) = refs[:12]
        dq_ref, dzf_ref, dzb_ref, dv_ref, dg_ref, dgw_ref, dlb_ref = refs[12 + rd.n:19 + rd.n]
        do_sc, dq_sc, dv_sc, rt_cur = refs[19 + 2 * rd.n:23 + 2 * rd.n]
        copies = rd.copies(refs[12:12 + rd.n], refs[19 + rd.n:19 + 2 * rd.n], refs[23 + 2 * rd.n:])
        step_id = pl.program_id(0) * 2 + pl.program_id(1)
        _ride_start(copies, step_id == 0)
        bones_m = bones_ref[...]
        bmask = bones_m.astype(F32)
        gwv = gw_ref[...]

        def head(n, acc):
            rows = pl.ds(pl.multiple_of(n * CHUNK, CHUNK), CHUNK)
            o = os_ref[rows, :]
            hg = g_ref[rows, :]
            dyv = dy_ref[rows, :]
            sg = _sigmoid(hg)
            r = lax.rsqrt(_dot_precise(o * o, bones_m, 2) * (1.0 / HEAD_DIM) + EPS)
            nrm = o * r * gwv
            dn = dyv * (hg * sg)
            dg_ref[rows, :] = (dyv * nrm * (sg * (1.0 + hg * (1.0 - sg)))).astype(BF16)
            g = dn * gwv
            mean_go = _dot_precise(g * o, bones_m, 2) * (1.0 / HEAD_DIM)
            do_sc[rows, :] = r * (g - o * (r * r) * mean_go)
            return acc + jnp.sum(dn * o * r, axis=0, keepdims=True)

        dgw_ref[0] = lax.fori_loop(0, nc, head, jnp.zeros((1, LANES), F32))
        dq_sc[...] = jnp.zeros_like(dq_sc)
        dv_sc[...] = jnp.zeros_like(dv_sc)

        rt_cur[...] = jnp.zeros_like(rt_cur)

        def one_direction(n, anti):
            side = 1 if anti else 0
            z_ref = zb_ref if anti else zf_ref
            dz_ref = dzb_ref if anti else dzf_ref
            lbv = lb_ref[side:side + 1]
            cn = n if anti else (nc - 1 - n)
            rows = pl.ds(pl.multiple_of(cn * CHUNK, CHUNK), CHUNK)
            q = q_ref[rows, :]
            v = v_ref[rows, :]
            sig, f, logf, k = _gates(z_ref[rows, :], lbv)
            b = _running_sum(tri_ref[side], logf)
            yield
            do = do_sc[rows, :]
            entered = [sts_ref[0, 0, side, cn * N_SUB + i] for i in range(N_SUB)]
            dq, dk, dv, rt_new, db_last = yield from _chunk_bwd(q, k, v, b, do, entered, rt_cur[side], bones_m,
                                                                bmask, anti)
            rt_cur[side] = rt_new
            dq_sc[rows, :] += dq
            dv_sc[rows, :] += dv
            dlogf = _running_sum(tri_ref[1 - side], q * dq - k * dk) + db_last
            dfl = jnp.where(f > F_MIN, dlogf / f, 0.0)
            dz_ref[rows, :] = ((dfl - dk) * (1.0 - lbv) * sig * (1.0 - sig)).astype(BF16)
            return jnp.sum((dfl - dk) * (1.0 - sig), axis=0, keepdims=True)

        def back(n, dlb):
            d0, d1 = _lockstep([one_direction(n, False), one_direction(n, True)])
            return dlb[0] + d0, dlb[1] + d1

        zero = jnp.zeros((1, LANES), F32)
        dlb0, dlb1 = lax.fori_loop(0, nc, back, (zero, zero))
        dlb_ref[0, 0:1, :] = dlb0
        dlb_ref[0, 1:2, :] = dlb1

        dq_ref[...] = dq_sc[...].astype(BF16)
        dv_ref[...] = dv_sc[...].astype(BF16)
        _ride_wait(copies, step_id == nb * 2 - 1)

    def col(c):
        return pl.BlockSpec((s, LANES), lambda b, p, c=c: (b, c + p))

    sl = pl.BlockSpec((s, LANES), lambda b, p: (b, p))
    out_t = jax.ShapeDtypeStruct((t, D_HGRN), BF16)
    return pl.pallas_call(
        body, name="hgrn_bwd", grid=(nb, 2),
        in_specs=[col(COL_HQ), col(COL_ZFW), col(COL_ZBW), col(COL_HI), col(COL_HG),
                  pl.BlockSpec((2, LANES), lambda b, p: (0, p)),
                  pl.BlockSpec((1, LANES), lambda b, p: (0, 0)),
                  sl,
                  pl.BlockSpec((1, 1, 2, nc * N_SUB, LANES, LANES), lambda b, p: (b, p, 0, 0, 0, 0)),
                  sl,
                  pl.BlockSpec((2, CHUNK, CHUNK), lambda b, p: (0, 0, 0)),
                  pl.BlockSpec((LANES, LANES), lambda b, p: (0, 0))] + rd.in_specs,
        out_specs=[sl, sl, sl, sl, sl,
                   pl.BlockSpec((1, 1, LANES), lambda b, p: (b, 0, p)),
                   pl.BlockSpec((1, 2, LANES), lambda b, p: (b, 0, p))] + rd.out_specs,
        out_shape=[out_t, out_t, out_t, out_t, out_t,
                   jax.ShapeDtypeStruct((nb, 1, D_HGRN), F32),
                   jax.ShapeDtypeStruct((nb, 2, D_HGRN), F32)] + rd.out_shape,
        scratch_shapes=[pltpu.VMEM((s, LANES), F32), pltpu.VMEM((s, LANES), F32), pltpu.VMEM((s, LANES), F32),
                        pltpu.VMEM((2, LANES, LANES), F32)] + rd.scratch,
        compiler_params=_cparams(("arbitrary", "arbitrary")),
    )(proj, proj, proj, proj, proj, lb, gw, osum, states, dy, tri, bones, *rd.srcs)


def _lower_bounds(logits):
    def body(lg_ref, lb_ref):
        rows = [lg_ref[l:l + 1, :] for l in range(DEPTH)]
        m = functools.reduce(jnp.maximum, rows)
        ex = [jnp.exp(r - m) for r in rows]
        den = functools.reduce(jnp.add, ex)
        run = jnp.zeros_like(m)
        for l in range(DEPTH):
            if l > 0:
                run = run + ex[l] / den
            lb_ref[l:l + 1, :] = run

    return pl.pallas_call(body, name="lower_bounds", out_shape=jax.ShapeDtypeStruct(logits.shape, F32))(logits)


def _lower_bounds_bwd(logits, dlb):
    def body(lg_ref, dlb_ref, dlg_ref):
        rows = [lg_ref[l:l + 1, :] for l in range(DEPTH)]
        m = functools.reduce(jnp.maximum, rows)
        ex = [jnp.exp(r - m) for r in rows]
        den = functools.reduce(jnp.add, ex)
        sm = [e / den for e in ex]
        dsm = [jnp.zeros_like(m) for _ in range(DEPTH)]
        for i in range(1, DEPTH):
            for l in range(i, DEPTH):
                dsm[i] = dsm[i] + dlb_ref[l:l + 1, :]
        dot = functools.reduce(jnp.add, [sm[i] * dsm[i] for i in range(DEPTH)])
        for i in range(DEPTH):
            dlg_ref[i:i + 1, :] = sm[i] * (dsm[i] - dot)

    return pl.pallas_call(body, name="lower_bounds_bwd", out_shape=jax.ShapeDtypeStruct(logits.shape, F32))(logits, dlb)


CONV_ROWS = 128


def _conv_core(a, bg, dww, dwb, lnw, lnb, upad_ref, s):
    sb = _sigmoid(bg)
    u = a * sb
    upad_ref[0:16, :] = jnp.zeros((16, D_CONV), F32)
    upad_ref[16:16 + s, :] = u
    upad_ref[16 + s:32 + s, :] = jnp.zeros((16, D_CONV), F32)
    rows = min(s, CONV_ROWS)
    pieces = []
    for r0 in range(0, s, rows):
        acc = None
        for j in range(CONV_W):
            term = upad_ref[r0 + 1 + j:r0 + 1 + j + rows, :] * dww[j:j + 1, :]
            acc = term if acc is None else acc + term
        pieces.append(acc)
    c = jnp.concatenate(pieces, axis=0) + dwb
    mu = jnp.mean(c, axis=-1, keepdims=True)
    xc = c - mu
    rstd = lax.rsqrt(jnp.mean(xc * xc, axis=-1, keepdims=True) + LN_EPS)
    nh = xc * rstd
    l = nh * lnw + lnb
    sl = _sigmoid(l)
    return sb, nh, rstd, l, sl


def _conv_fwd(proj, dww, dwb, lnw, lnb, pww, pwb, nb, s):
    t = proj.shape[0]
    assert s % min(s, CONV_ROWS) == 0

    def body(a_ref, b_ref, dww_ref, dwb_ref, lnw_ref, lnb_ref, pww_ref, pwb_ref, y_ref, upad_ref):
        _, _, _, l, sl = _conv_core(a_ref[...], b_ref[...], dww_ref[...], dwb_ref[...], lnw_ref[...],
                                    lnb_ref[...], upad_ref, s)
        y_ref[...] = _dot((l * sl).astype(BF16), pww_ref[...]) + pwb_ref[...]

    vec = pl.BlockSpec((1, D_CONV), lambda b: (0, 0))
    return pl.pallas_call(
        body, name="conv_fwd", grid=(nb,),
        in_specs=[pl.BlockSpec((s, D_CONV), lambda b: (b, COL_CA)),
                  pl.BlockSpec((s, D_CONV), lambda b: (b, COL_CB)),
                  pl.BlockSpec((32, D_CONV), lambda b: (0, 0)), vec, vec, vec,
                  pl.BlockSpec((D_CONV, D_CONV), lambda b: (0, 0)), vec],
        out_specs=pl.BlockSpec((s, D_CONV), lambda b: (b, 0)),
        out_shape=jax.ShapeDtypeStruct((t, D_CONV), F32),
        scratch_shapes=[pltpu.VMEM((s + 32, D_CONV), F32)],
        compiler_params=_cparams(("parallel",)),
    )(proj, proj, dww, dwb, lnw, lnb, pww, pwb)


def _conv_bwd(proj, dy, dww, dwb, lnw, lnb, pww, nb, s):
    t = proj.shape[0]

    def body(a_ref, b_ref, dy_ref, dww_ref, dwb_ref, lnw_ref, lnb_ref, pww_ref,
             dab_ref, ddww_ref, ddwb_ref, dlnw_ref, dlnb_ref, dpww_ref, dpwb_ref, upad_ref, dcpad_ref):
        a = a_ref[...]
        dww = dww_ref[...]
        sb, nh, rstd, l, sl = _conv_core(a, b_ref[...], dww, dwb_ref[...], lnw_ref[...], lnb_ref[...],
                                         upad_ref, s)
        dyv = dy_ref[...]
        dyb = dyv.astype(BF16)
        ds = _dot_nt(dyb, pww_ref[...])
        dl = ds * (sl * (1.0 + l * (1.0 - sl)))
        dn = dl * lnw_ref[...]
        dc = rstd * (dn - jnp.mean(dn, axis=-1, keepdims=True)
                     - nh * jnp.mean(dn * nh, axis=-1, keepdims=True))

        @pl.when(pl.program_id(0) == 0)
        def _():
            for r in (ddww_ref, ddwb_ref, dlnw_ref, dlnb_ref, dpww_ref, dpwb_ref):
                r[...] = jnp.zeros_like(r)

        dpww_ref[...] += _dot_tn((l * sl).astype(BF16), dyb)
        dpwb_ref[...] += jnp.sum(dyv, axis=0, keepdims=True)
        dlnw_ref[...] += jnp.sum(dl * nh, axis=0, keepdims=True)
        dlnb_ref[...] += jnp.sum(dl, axis=0, keepdims=True)
        ddwb_ref[...] += jnp.sum(dc, axis=0, keepdims=True)

        dcpad_ref[0:16, :] = jnp.zeros((16, D_CONV), F32)
        dcpad_ref[16:16 + s, :] = dc
        dcpad_ref[16 + s:32 + s, :] = jnp.zeros((16, D_CONV), F32)
        rows = min(s, CONV_ROWS)
        r8 = lax.broadcasted_iota(jnp.int32, (32, D_CONV), 0)
        ddww = jnp.zeros((32, D_CONV), F32)
        pieces = []
        for r0 in range(0, s, rows):
            acc = None
            dcr = dcpad_ref[16 + r0:16 + r0 + rows, :]
            for j in range(CONV_W):
                term = dcpad_ref[r0 + 31 - j:r0 + 31 - j + rows, :] * dww[j:j + 1, :]
                acc = term if acc is None else acc + term
                wj = jnp.sum(dcr * upad_ref[r0 + 1 + j:r0 + 1 + j + rows, :], axis=0, keepdims=True)
                ddww = ddww + jnp.where(r8 == j, wj, 0.0)
            pieces.append(acc)
        du = jnp.concatenate(pieces, axis=0)
        ddww_ref[...] += ddww
        dab_ref[:, 0:D_CONV] = (du * sb).astype(BF16)
        dab_ref[:, D_CONV:2 * D_CONV] = (du * a * sb * (1.0 - sb)).astype(BF16)

    vec = pl.BlockSpec((1, D_CONV), lambda b: (0, 0))
    mat = pl.BlockSpec((D_CONV, D_CONV), lambda b: (0, 0))
    w32 = pl.BlockSpec((32, D_CONV), lambda b: (0, 0))
    vshape = jax.ShapeDtypeStruct((1, D_CONV), F32)
    return pl.pallas_call(
        body, name="conv_bwd", grid=(nb,),
        in_specs=[pl.BlockSpec((s, D_CONV), lambda b: (b, COL_CA)),
                  pl.BlockSpec((s, D_CONV), lambda b: (b, COL_CB)),
                  pl.BlockSpec((s, D_CONV), lambda b: (b, 0)),
                  w32, vec, vec, vec, mat],
        out_specs=[pl.BlockSpec((s, 2 * D_CONV), lambda b: (b, 0)), w32, vec, vec, vec, mat, vec],
        out_shape=[jax.ShapeDtypeStruct((t, 2 * D_CONV), BF16),
                   jax.ShapeDtypeStruct((32, D_CONV), F32), vshape, vshape, vshape,
                   jax.ShapeDtypeStruct((D_CONV, D_CONV), F32), vshape],
        scratch_shapes=[pltpu.VMEM((s + 32, D_CONV), F32), pltpu.VMEM((s + 32, D_CONV), F32)],
        compiler_params=_cparams(("arbitrary",)),
    )(proj, proj, dy, dww, dwb, lnw, lnb, pww)


def _mix_out(o_attn, y_hgrn, y_conv, x, aw, cw, w_out, tm):
    t = x.shape[0]

    def body(o_ref, h_ref, c_ref, x_ref, aw_ref, cw_ref, w_ref, mixed_ref, x1_ref):
        o = o_ref[...]
        a = o * lax.rsqrt(jnp.mean(o * o, axis=-1, keepdims=True) + EPS) * aw_ref[...]
        yc = c_ref[...]
        c = yc * lax.rsqrt(jnp.mean(yc * yc, axis=-1, keepdims=True) + EPS) * cw_ref[...]
        ab, hb, cb = a.astype(BF16), h_ref[...].astype(BF16), c.astype(BF16)
        mixed_ref[:, 0:512] = ab
        mixed_ref[:, 512:768] = hb
        mixed_ref[:, 768:1024] = cb
        x1_ref[...] = (x_ref[...] + _dot(ab, w_ref[0:512, :]) + _dot(hb, w_ref[512:768, :])
                       + _dot(cb, w_ref[768:1024, :]))

    def tok(w):
        return pl.BlockSpec((tm, w), lambda i: (i, 0))

    return pl.pallas_call(
        body, name="mix_out", grid=(t // tm,),
        in_specs=[tok(512), tok(256), tok(256), tok(D_MODEL),
                  pl.BlockSpec((1, 512), lambda i: (0, 0)), pl.BlockSpec((1, 256), lambda i: (0, 0)),
                  pl.BlockSpec((D_MODEL, D_MODEL), lambda i: (0, 0))],
        out_specs=[tok(D_MODEL), tok(D_MODEL)],
        out_shape=[jax.ShapeDtypeStruct((t, D_MODEL), BF16), jax.ShapeDtypeStruct((t, D_MODEL), F32)],
        compiler_params=_cparams(("parallel",)),
    )(o_attn, y_hgrn, y_conv, x, aw, cw, w_out)


def _mix_out_bwd(dx1, w_out, o_attn, y_conv, aw, cw, tm):
    t = dx1.shape[0]

    def body(dx_ref, w_ref, o_ref, c_ref, aw_ref, cw_ref, do_ref, dh_ref, dc_ref, daw_ref, dcw_ref):
        dm = _dot_nt(dx_ref[...].astype(BF16), w_ref[...])
        do, daw = _rms_bwd(dm[:, 0:512], o_ref[...], aw_ref[...])
        dc, dcw = _rms_bwd(dm[:, 768:1024], c_ref[...], cw_ref[...])
        do_ref[...] = do
        dh_ref[...] = dm[:, 512:768]
        dc_ref[...] = dc

        @pl.when(pl.program_id(0) == 0)
        def _():
            daw_ref[...] = jnp.zeros_like(daw_ref)
            dcw_ref[...] = jnp.zeros_like(dcw_ref)

        daw_ref[...] += jnp.sum(daw, axis=0, keepdims=True)
        dcw_ref[...] += jnp.sum(dcw, axis=0, keepdims=True)

    def tok(w):
        return pl.BlockSpec((tm, w), lambda i: (i, 0))

    v512 = pl.BlockSpec((1, 512), lambda i: (0, 0))
    v256 = pl.BlockSpec((1, 256), lambda i: (0, 0))
    return pl.pallas_call(
        body, name="mix_out_bwd", grid=(t // tm,),
        in_specs=[tok(D_MODEL), pl.BlockSpec((D_MODEL, D_MODEL), lambda i: (0, 0)), tok(512), tok(256),
                  v512, v256],
        out_specs=[tok(512), tok(256), tok(256), v512, v256],
        out_shape=[jax.ShapeDtypeStruct((t, 512), F32), jax.ShapeDtypeStruct((t, 256), F32),
                   jax.ShapeDtypeStruct((t, 256), F32), jax.ShapeDtypeStruct((1, 512), F32),
                   jax.ShapeDtypeStruct((1, 256), F32)],
        compiler_params=_cparams(("arbitrary",)),
    )(dx1, w_out, o_attn, y_conv, aw, cw)


FF_BLOCKS = 2


def _ffn_fwd(x1, fw, wg, wu, wd, tm, ride=None):
    t = x1.shape[0]
    fb = wg.shape[1]
    nf = N_DEV // FF_BLOCKS
    rd = _ride_plan(ride)
    grid = (t // tm, nf)

    def body(*refs):
        x_ref, fw_ref, wg_ref, wu_ref, wd_ref = refs[:5]
        h_ref, g_ref, u_ref, a_ref, x2_ref = refs[5 + rd.n:10 + rd.n]
        acc_ref = refs[10 + 2 * rd.n]
        copies = rd.copies(refs[5:5 + rd.n], refs[10 + rd.n:10 + 2 * rd.n], refs[11 + 2 * rd.n:])
        step_id = _grid_step_id(grid)
        _ride_start(copies, step_id == 0)
        j = pl.program_id(1)

        @pl.when(j == 0)
        def _():
            xv = x_ref[...]
            r = lax.rsqrt(jnp.mean(xv * xv, axis=-1, keepdims=True) + EPS)
            h_ref[...] = (xv * r * fw_ref[...]).astype(BF16)
            acc_ref[...] = xv

        h = h_ref[...]
        out = None
        for c in range(FF_BLOCKS):
            g = _dot_nt(h, wg_ref[c])
            u = _dot_nt(h, wu_ref[c])
            a = (g * _sigmoid(g) * u).astype(BF16)
            g_ref[c] = g.astype(BF16)
            u_ref[c] = u.astype(BF16)
            a_ref[c] = a
            part = _dot(a, wd_ref[c])
            out = part if out is None else out + part
        acc_ref[...] += out

        @pl.when(j == nf - 1)
        def _():
            x2_ref[...] = acc_ref[...]

        _ride_wait(copies, step_id == (t // tm) * nf - 1)

    tok = pl.BlockSpec((tm, D_MODEL), lambda i, j: (i, 0))
    ffb = pl.BlockSpec((FF_BLOCKS, tm, fb), lambda i, j: (j, i, 0))
    ffs = jax.ShapeDtypeStruct((N_DEV, t, fb), BF16)
    return pl.pallas_call(
        body, name="ffn_fwd", grid=grid,
        in_specs=[tok, pl.BlockSpec((1, D_MODEL), lambda i, j: (0, 0)),
                  pl.BlockSpec((FF_BLOCKS, fb, D_MODEL), lambda i, j: (j, 0, 0)),
                  pl.BlockSpec((FF_BLOCKS, fb, D_MODEL), lambda i, j: (j, 0, 0)),
                  pl.BlockSpec((FF_BLOCKS, fb, D_MODEL), lambda i, j: (j, 0, 0))] + rd.in_specs,
        out_specs=[tok, ffb, ffb, ffb, tok] + rd.out_specs,
        out_shape=[jax.ShapeDtypeStruct((t, D_MODEL), BF16), ffs, ffs, ffs,
                   jax.ShapeDtypeStruct((t, D_MODEL), F32)] + rd.out_shape,
        scratch_shapes=[pltpu.VMEM((tm, D_MODEL), F32)] + rd.scratch,
        compiler_params=_cparams(("arbitrary", "arbitrary")),
    )(x1, fw, wg, wu, wd, *rd.srcs)


def _ffn_bwd(dx2, g, u, wg, wu, wd, x1, fw, tm, ride=None):
    t = dx2.shape[0]
    fb = wg.shape[1]
    nf = N_DEV // FF_BLOCKS
    rd = _ride_plan(ride)
    grid = (t // tm, nf)

    def body(*refs):
        dx_ref, g_ref, u_ref, wg_ref, wu_ref, wd_ref, x_ref, fw_ref = refs[:8]
        dg_ref, du_ref, dx1_ref, dfw_ref = refs[8 + rd.n:12 + rd.n]
        acc_ref = refs[12 + 2 * rd.n]
        copies = rd.copies(refs[8:8 + rd.n], refs[12 + rd.n:12 + 2 * rd.n], refs[13 + 2 * rd.n:])
        step_id = _grid_step_id(grid)
        _ride_start(copies, step_id == 0)
        i = pl.program_id(0)
        j = pl.program_id(1)
        dxb = dx_ref[...].astype(BF16)
        dh = None
        for c in range(FF_BLOCKS):
            da = _dot_nt(dxb, wd_ref[c])
            gv = g_ref[c].astype(F32)
            uv = u_ref[c].astype(F32)
            sg = _sigmoid(gv)
            dg = (da * uv * (sg * (1.0 + gv * (1.0 - sg)))).astype(BF16)
            du = (da * gv * sg).astype(BF16)
            dg_ref[c] = dg
            du_ref[c] = du
            part = _dot(dg, wg_ref[c]) + _dot(du, wu_ref[c])
            dh = part if dh is None else dh + part

        @pl.when(j == 0)
        def _():
            acc_ref[...] = dh

        @pl.when(j > 0)
        def _():
            acc_ref[...] += dh

        @pl.when((i == 0) & (j == 0))
        def _():
            dfw_ref[...] = jnp.zeros_like(dfw_ref)

        @pl.when(j == nf - 1)
        def _():
            dx, dfw = _rms_bwd(acc_ref[...], x_ref[...], fw_ref[...])
            dx1_ref[...] = dx_ref[...] + dx
            dfw_ref[...] += jnp.sum(dfw, axis=0, keepdims=True)

        _ride_wait(copies, step_id == (t // tm) * nf - 1)

    tok = pl.BlockSpec((tm, D_MODEL), lambda i, j: (i, 0))
    ffb = pl.BlockSpec((FF_BLOCKS, tm, fb), lambda i, j: (j, i, 0))
    ffs = jax.ShapeDtypeStruct((N_DEV, t, fb), BF16)
    vec = pl.BlockSpec((1, D_MODEL), lambda i, j: (0, 0))
    return pl.pallas_call(
        body, name="ffn_bwd", grid=grid,
        in_specs=[tok, ffb, ffb,
                  pl.BlockSpec((FF_BLOCKS, fb, D_MODEL), lambda i, j: (j, 0, 0)),
                  pl.BlockSpec((FF_BLOCKS, fb, D_MODEL), lambda i, j: (j, 0, 0)),
                  pl.BlockSpec((FF_BLOCKS, fb, D_MODEL), lambda i, j: (j, 0, 0)),
                  tok, vec] + rd.in_specs,
        out_specs=[ffb, ffb, tok, vec] + rd.out_specs,
        out_shape=[ffs, ffs, jax.ShapeDtypeStruct((t, D_MODEL), F32),
                   jax.ShapeDtypeStruct((1, D_MODEL), F32)] + rd.out_shape,
        scratch_shapes=[pltpu.VMEM((tm, D_MODEL), F32)] + rd.scratch,
        compiler_params=_cparams(("arbitrary", "arbitrary")),
    )(dx2, g, u, wg, wu, wd, x1, fw, *rd.srcs)


def _loss_grad(y, target, tm):
    t, d = y.shape

    def body(y_ref, t_ref, dy_ref, loss_ref):
        err = y_ref[...] - t_ref[...]
        dy_ref[...] = err * (1.0 / d)

        @pl.when(pl.program_id(0) == 0)
        def _():
            loss_ref[...] = jnp.zeros_like(loss_ref)

        part = jnp.sum(jnp.sum(err * err, axis=-1, keepdims=True), axis=0, keepdims=True)
        loss_ref[...] += part * (0.5 / d)

    tok = pl.BlockSpec((tm, d), lambda i: (i, 0))
    return pl.pallas_call(
        body, name="loss_grad", grid=(t // tm,),
        in_specs=[tok, tok],
        out_specs=[tok, pl.BlockSpec((1, 1), lambda i: (0, 0))],
        out_shape=[jax.ShapeDtypeStruct((t, d), F32), jax.ShapeDtypeStruct((1, 1), F32)],
        compiler_params=_cparams(("arbitrary",)),
    )(y, target)


def _tile(v, reps):
    return jnp.tile(v.reshape(1, -1), (1, reps))


class _LocalPlan:
    def __init__(self, wb):
        self.w = [{n: wb[n][l] for n in BIG_AXIS} for l in range(DEPTH)]

    def ride(self, kernel_name, l, grads=None):
        return None

    def done(self, kernel_name, l, outs):
        pass


def _local_step(x, target, p, plan):
    nb, s, d = x.shape
    t = nb * s
    tm = min(512, s)
    tm_ff = min(1024, t)
    tq = min(1024, s)
    xf = x.reshape(t, d)
    cosq, sinq = _rope_tables(s)
    ones512 = _block_ones(512, HEAD_DIM)
    lbs = _lower_bounds(p["hgrn_lb_logits"].reshape(DEPTH, 2 * D_HGRN)).reshape(DEPTH, 2, D_HGRN)

    saved = []
    cur = xf
    wb = plan.w
    for l in range(DEPTH):
        qw = _tile(p["q_norm_w"][l], N_HEADS)
        kw = _tile(p["k_norm_w"][l], N_KV)
        gw = _tile(p["hgrn_gnorm_w"][l], 2)
        dww = jnp.pad(p["conv_dw_w"][l], ((0, 1), (0, 0)))
        pww = p["conv_pw_w"][l].astype(BF16)
        h0, proj = _rms_proj(cur, _row(p["mix_norm_w"][l]), wb[l]["w_in"], tm)
        qr, kd, vd, kdt, vdt = _qkv_prep(proj, cosq, sinq, qw, kw, ones512, s, tm)
        o_attn, lse, *rode = _attn_fwd(qr, kd, vdt, nb, s, tq, plan.ride("attn_fwd", l))
        plan.done("attn_fwd", l, rode)
        y_hgrn, osum, states, *rode = _hgrn_fwd(proj, lbs[l], gw, nb, s, plan.ride("hgrn_fwd", l))
        plan.done("hgrn_fwd", l, rode)
        y_conv = _conv_fwd(proj, dww, _row(p["conv_dw_b"][l]), _row(p["conv_ln_w"][l]),
                           _row(p["conv_ln_b"][l]), pww, _row(p["conv_pw_b"][l]), nb, s)
        mixed, x1 = _mix_out(o_attn, y_hgrn, y_conv, cur, _row(p["attn_out_norm_w"][l]),
                             _row(p["conv_out_norm_w"][l]), wb[l]["w_out"], tm)
        hf, g, u, a, x2, *rode = _ffn_fwd(x1, _row(p["ffn_norm_w"][l]), wb[l]["w_gate"], wb[l]["w_up"],
                                          wb[l]["w_down"], tm_ff, plan.ride("ffn_fwd", l))
        plan.done("ffn_fwd", l, rode)
        saved.append(dict(x=cur, h0=h0, proj=proj, qr=qr, kd=kd, vd=vd, kdt=kdt, o_attn=o_attn, lse=lse,
                          osum=osum, states=states, y_conv=y_conv, mixed=mixed, x1=x1, hf=hf, g=g, u=u, a=a,
                          qw=qw, kw=kw, gw=gw, dww=dww, pww=pww))
        cur = x2

    dcur, loss = _loss_grad(cur, target.reshape(t, d), tm)

    grads = {k: [None] * DEPTH for k in WEIGHTS}
    dlb = [None] * DEPTH
    for l in reversed(range(DEPTH)):
        sv = saved[l]
        dg, du, dx1, dfw, *rode = _ffn_bwd(dcur, sv["g"], sv["u"], wb[l]["w_gate"], wb[l]["w_up"],
                                           wb[l]["w_down"], sv["x1"], _row(p["ffn_norm_w"][l]), tm,
                                           plan.ride("ffn_bwd", l, grads))
        plan.done("ffn_bwd", l, rode)
        grads["ffn_norm_w"][l] = dfw[0]
        grads["w_gate"][l] = _dw_ff(dg, sv["hf"], "dw_gate", tm)
        grads["w_up"][l] = _dw_ff(du, sv["hf"], "dw_up", tm)
        grads["w_down"][l] = _dw_ff(sv["a"], dcur, "dw_down", tm)
        do_attn, dy_hgrn, dy_conv, daw, dcw = _mix_out_bwd(
            dx1, wb[l]["w_out"], sv["o_attn"], sv["y_conv"], _row(p["attn_out_norm_w"][l]),
            _row(p["conv_out_norm_w"][l]), tm)
        grads["attn_out_norm_w"][l] = daw[0]
        grads["conv_out_norm_w"][l] = dcw[0]
        grads["w_out"][l] = _mm_tn(sv["mixed"], dx1, D_MODEL, "dw_out", tm)
        dq, dkd, dvd, *rode = _attn_bwd(sv["qr"], sv["kd"], sv["vd"], sv["kdt"], sv["o_attn"], sv["lse"], do_attn,
                                        nb, s, tq, plan.ride("attn_bwd", l, grads))
        plan.done("attn_bwd", l, rode)
        dqkv, dqw, dkw = _qkv_bwd(sv["proj"], dq, dkd, dvd, cosq, sinq, sv["qw"], sv["kw"], ones512, s, tm)
        grads["q_norm_w"][l] = dqw.reshape(N_HEADS, HEAD_DIM).sum(0)
        grads["k_norm_w"][l] = dkw.reshape(N_KV, HEAD_DIM).sum(0)
        dhq, dzf, dzb, dhi, dhg, dgw, dlb_l, *rode = _hgrn_bwd(sv["proj"], lbs[l], sv["gw"], sv["osum"],
                                                               sv["states"], dy_hgrn, nb, s,
                                                               plan.ride("hgrn_bwd", l, grads))
        plan.done("hgrn_bwd", l, rode)
        grads["hgrn_gnorm_w"][l] = dgw.reshape(nb * D_HGRN // HEAD_DIM, HEAD_DIM).sum(0)
        dlb[l] = dlb_l.sum(0)
        dab, ddww, ddwb, dlnw, dlnb, dpww, dpwb = _conv_bwd(
            sv["proj"], dy_conv, sv["dww"], _row(p["conv_dw_b"][l]), _row(p["conv_ln_w"][l]),
            _row(p["conv_ln_b"][l]), sv["pww"], nb, s)
        grads["conv_dw_w"][l] = ddww[:CONV_W]
        grads["conv_dw_b"][l] = ddwb[0]
        grads["conv_ln_w"][l] = dlnw[0]
        grads["conv_ln_b"][l] = dlnb[0]
        grads["conv_pw_w"][l] = dpww
        grads["conv_pw_b"][l] = dpwb[0]
        pieces = [dqkv, dhq, dzf, dzb, dhi, dhg, dab]
        grads["w_in"][l] = _dw_in(sv["h0"], pieces, tm)
        dcur, dnw = _proj_bwd(pieces, wb[l]["w_in"], sv["x"], _row(p["mix_norm_w"][l]), dx1, tm)
        grads["mix_norm_w"][l] = dnw[0]

    dlog = _lower_bounds_bwd(p["hgrn_lb_logits"].reshape(DEPTH, 2 * D_HGRN),
                             jnp.stack(dlb).reshape(DEPTH, 2 * D_HGRN))
    out = {k: (v if k in BIG_AXIS else jnp.stack(v)) for k, v in grads.items() if k != "hgrn_lb_logits"}
    out["hgrn_lb_logits"] = dlog.reshape(DEPTH, 2, D_HGRN)
    return loss, dcur.reshape(nb, s, d), out


BIG_AXIS = {"w_in": 2, "w_out": 1, "w_gate": 2, "w_up": 2, "w_down": 1}
SMALL_SHARD_AXIS = {"hgrn_lb_logits": 2, "conv_dw_w": 2, "conv_pw_w": 1}
WEIGHTS = ("mix_norm_w", "w_in", "q_norm_w", "k_norm_w", "hgrn_lb_logits", "hgrn_gnorm_w", "conv_dw_w",
           "conv_dw_b", "conv_ln_w", "conv_ln_b", "conv_pw_w", "conv_pw_b", "attn_out_norm_w",
           "conv_out_norm_w", "w_out", "ffn_norm_w", "w_gate", "w_up", "w_down")
SMALL = tuple(n for n in WEIGHTS if n not in BIG_AXIS)


def _my_index():
    return 4 * lax.axis_index("x") + 2 * lax.axis_index("y") + lax.axis_index("c")


class _RidePlan:
    def __init__(self, srcs, gather):
        self.srcs = list(srcs)
        self.n = len(self.srcs)
        self.gather = list(gather) if isinstance(gather, (list, tuple)) else [gather] * self.n
        any_spec = pl.BlockSpec(memory_space=pl.ANY)
        self.in_specs = [any_spec] * self.n
        self.out_specs = [any_spec] * self.n
        self.out_shape = [jax.ShapeDtypeStruct(((N_DEV,) + s.shape) if g else s.shape, s.dtype)
                          for s, g in zip(self.srcs, self.gather)]
        npeer = N_DEV - 1
        self.scratch = [pltpu.SemaphoreType.DMA((self.n * npeer,)), pltpu.SemaphoreType.DMA((self.n * npeer,)),
                        pltpu.SemaphoreType.DMA((self.n,))] if self.n else []

    def copies(self, src_refs, out_refs, sems):
        if not self.n:
            return [], [], []
        send_sems, recv_sems, local_sems = sems
        npeer = N_DEV - 1
        x, y, c = lax.axis_index("x"), lax.axis_index("y"), lax.axis_index("c")
        me = 4 * x + 2 * y + c
        locals_, sends, recvs = [], [], []
        for a in range(self.n):
            src_ref, out_ref = src_refs[a], out_refs[a]

            def rows_for(j, src_ref=src_ref, gather=self.gather[a]):
                return src_ref if gather else src_ref.at[j]

            locals_.append(pltpu.make_async_copy(rows_for(me), out_ref.at[me], local_sems.at[a]))
            for k in range(1, N_DEV):
                px = (1 - x) if (k & 4) else x
                py = (1 - y) if (k & 2) else y
                pc = (1 - c) if (k & 1) else c
                pidx = 4 * px + 2 * py + pc
                common = dict(send_sem=send_sems.at[a * npeer + k - 1], recv_sem=recv_sems.at[a * npeer + k - 1],
                              device_id=(px, py, pc), device_id_type=pl.DeviceIdType.MESH)
                sends.append(pltpu.make_async_remote_copy(src_ref=rows_for(pidx), dst_ref=out_ref.at[me], **common))
                recvs.append(pltpu.make_async_remote_copy(src_ref=rows_for(pidx), dst_ref=out_ref.at[pidx],
                                                          **common))
        return locals_, sends, recvs


def _ride_plan(ride):
    return _RidePlan(*ride) if ride else _RidePlan([], True)


def _ride_start(copies, when=None):
    locals_, sends, _ = copies

    def go():
        for cp in locals_ + sends:
            cp.start()

    if locals_:
        go() if when is None else pl.when(when)(go)


def _ride_wait(copies, when=None):
    locals_, sends, recvs = copies

    def go():
        for cp in recvs:
            cp.wait_recv()
        for cp in sends:
            cp.wait_send()
        for cp in locals_:
            cp.wait()

    if locals_:
        go() if when is None else pl.when(when)(go)


def _exchange(srcs, gather, name):
    rd = _RidePlan(srcs, gather)

    def body(*refs):
        copies = rd.copies(refs[:rd.n], refs[rd.n:2 * rd.n], refs[2 * rd.n:])
        _ride_start(copies)
        _ride_wait(copies)

    return pl.pallas_call(body, name=name, in_specs=rd.in_specs, out_specs=rd.out_specs,
                          out_shape=rd.out_shape, scratch_shapes=rd.scratch)(*srcs)


def _adamw_math(w, g, m, v):
    m = ADAM_B1 * m + (1.0 - ADAM_B1) * g
    v = ADAM_B2 * v + (1.0 - ADAM_B2) * (g * g)
    m_hat = m / (1.0 - ADAM_B1 ** ADAM_STEP)
    v_hat = v / (1.0 - ADAM_B2 ** ADAM_STEP)
    delta = -ADAM_LR * (m_hat / (jnp.sqrt(v_hat) + ADAM_EPS) + ADAM_WD * w)
    return delta, m, v


def _sum_adamw(parts, w, m, v, name):
    _, k, n = w.shape
    tk = k
    for cand in (256, 176, 160, 128):
        if k % cand == 0:
            tk = cand
            break

    def body(*refs):
        p_refs = refs[:DEPTH]
        w_ref, m_ref, v_ref, g_ref, d_ref, mo_ref, vo_ref = refs[DEPTH:]
        for l in range(DEPTH):
            @pl.when(pl.program_id(0) == l)
            def _(p_ref=p_refs[l]):
                g = p_ref[0].astype(F32)
                for i in range(1, N_DEV):
                    g = g + p_ref[i].astype(F32)
                g_ref[...] = g
                d_ref[...], mo_ref[...], vo_ref[...] = _adamw_math(w_ref[...], g, m_ref[...], v_ref[...])

    row = pl.BlockSpec((None, tk, n), lambda l, i: (l, i, 0))
    shp = jax.ShapeDtypeStruct(w.shape, F32)
    return pl.pallas_call(
        body, name=name, grid=(DEPTH, k // tk),
        in_specs=[pl.BlockSpec((N_DEV, tk, n), lambda l, i: (0, i, 0))] * DEPTH + [row, row, row],
        out_specs=[row, row, row, row],
        out_shape=[shp, shp, shp, shp],
        compiler_params=_cparams(("parallel", "parallel")),
    )(*parts, w, m, v)


def _sum8(parts, name):
    r = parts.shape[1]

    def body(p_ref, g_ref):
        g = p_ref[0]
        for i in range(1, N_DEV):
            g = g + p_ref[i]
        g_ref[...] = g

    return pl.pallas_call(body, name=name, out_shape=jax.ShapeDtypeStruct((r, LANES), F32))(parts)


def _adamw(w, g, m, v):
    def body(w_ref, g_ref, m_ref, v_ref, d_ref, mo_ref, vo_ref):
        d_ref[...], mo_ref[...], vo_ref[...] = _adamw_math(w_ref[...], g_ref[...], m_ref[...], v_ref[...])

    shp = jax.ShapeDtypeStruct(w.shape, F32)
    return pl.pallas_call(body, name="adamw_small", out_shape=[shp, shp, shp])(w, g, m, v)


def _pack(arrays, dtype, row_multiple):
    flat = jnp.concatenate([a.reshape(-1).astype(dtype) for a in arrays])
    n = flat.shape[0]
    unit = row_multiple * LANES
    total = -(-n // unit) * unit
    return jnp.pad(flat, (0, total - n)).reshape(total // LANES, LANES)


def _unpack(flat2d, shapes, lead=()):
    flat = flat2d.reshape(lead + (-1,))
    out, off = [], 0
    for shp in shapes:
        n = int(np.prod(shp))
        out.append(flat[..., off:off + n].reshape(lead + tuple(shp)))
        off += n
    return out


def _shard_to_rows(full, axis):
    shp = full.shape
    k = shp[axis] // N_DEV
    r = full.reshape(shp[:axis] + (N_DEV, k) + shp[axis + 1:])
    return jnp.moveaxis(r, axis, 0)


def _rows_to_full(rows, axis):
    r = jnp.moveaxis(rows, 0, axis)
    shp = r.shape
    return r.reshape(shp[:axis] + (shp[axis] * shp[axis + 1],) + shp[axis + 2:])


def kernel(x, mix_norm_w, w_in, q_norm_w, k_norm_w, hgrn_lb_logits, hgrn_gnorm_w, conv_dw_w, conv_dw_b, conv_ln_w, conv_ln_b, conv_pw_w, conv_pw_b, attn_out_norm_w, conv_out_norm_w, w_out, ffn_norm_w, w_gate, w_up, w_down, loss_target, m_mix_norm_w, m_w_in, m_q_norm_w, m_k_norm_w, m_hgrn_lb_logits, m_hgrn_gnorm_w, m_conv_dw_w, m_conv_dw_b, m_conv_ln_w, m_conv_ln_b, m_conv_pw_w, m_conv_pw_b, m_attn_out_norm_w, m_conv_out_norm_w, m_w_out, m_ffn_norm_w, m_w_gate, m_w_up, m_w_down, v_mix_norm_w, v_w_in, v_q_norm_w, v_k_norm_w, v_hgrn_lb_logits, v_hgrn_gnorm_w, v_conv_dw_w, v_conv_dw_b, v_conv_ln_w, v_conv_ln_b, v_conv_pw_w, v_conv_pw_b, v_attn_out_norm_w, v_conv_out_norm_w, v_w_out, v_ffn_norm_w, v_w_gate, v_w_up, v_w_down):
    w_loc = dict(zip(WEIGHTS, (mix_norm_w, w_in, q_norm_w, k_norm_w, hgrn_lb_logits, hgrn_gnorm_w, conv_dw_w,
                               conv_dw_b, conv_ln_w, conv_ln_b, conv_pw_w, conv_pw_b, attn_out_norm_w,
                               conv_out_norm_w, w_out, ffn_norm_w, w_gate, w_up, w_down)))
    m_loc = dict(zip(WEIGHTS, (m_mix_norm_w, m_w_in, m_q_norm_w, m_k_norm_w, m_hgrn_lb_logits, m_hgrn_gnorm_w,
                               m_conv_dw_w, m_conv_dw_b, m_conv_ln_w, m_conv_ln_b, m_conv_pw_w, m_conv_pw_b,
                               m_attn_out_norm_w, m_conv_out_norm_w, m_w_out, m_ffn_norm_w, m_w_gate, m_w_up,
                               m_w_down)))
    v_loc = dict(zip(WEIGHTS, (v_mix_norm_w, v_w_in, v_q_norm_w, v_k_norm_w, v_hgrn_lb_logits, v_hgrn_gnorm_w,
                               v_conv_dw_w, v_conv_dw_b, v_conv_ln_w, v_conv_ln_b, v_conv_pw_w, v_conv_pw_b,
                               v_attn_out_norm_w, v_conv_out_norm_w, v_w_out, v_ffn_norm_w, v_w_gate, v_w_up,
                               v_w_down)))
    me = _my_index()
    big = tuple(BIG_AXIS)
    sms = tuple(SMALL_SHARD_AXIS)

    sm_shapes = [w_loc[n].shape for n in sms]
    got_s = _exchange([_pack([w_loc[n] for n in sms], F32, 8)], True, "gather_small_params")[0]
    p_full = {n: w_loc[n] for n in SMALL if n not in SMALL_SHARD_AXIS}
    for n, a in zip(sms, _unpack(got_s, sm_shapes, (N_DEV,))):
        p_full[n] = _rows_to_full(a, SMALL_SHARD_AXIS[n])

    col_sharded = tuple(n for n in big if BIG_AXIS[n] == 2)

    def shard_t(n, a):
        return jnp.swapaxes(a, 1, 2) if n in col_sharded else a

    def natural(n, gathered):
        if n in ("w_in", "w_out"):
            return gathered.reshape(-1, gathered.shape[-1])
        return gathered

    def to_send(n, gl):
        if n in ("w_in", "w_out"):
            return gl.reshape(N_DEV, gl.shape[0] // N_DEV, gl.shape[1]).astype(BF16)
        return gl

    class StepPlan:
        def __init__(self):
            self.w = [dict() for _ in range(DEPTH)]
            self.parts = [dict() for _ in range(DEPTH)]
            self.pending = {}
            got = _exchange([w_send["w_in"][0]], True, "gather_w_in")
            self.w[0]["w_in"] = natural("w_in", got[0])

        def ride(self, kernel_name, l, grads=None):
            want = []
            if kernel_name == "attn_fwd":
                want = [("w_out", l), ("w_gate", l)]
            elif kernel_name == "hgrn_fwd":
                want = [("w_up", l), ("w_down", l)]
            elif kernel_name == "ffn_fwd" and l + 1 < DEPTH:
                want = [("w_in", l + 1)]
            elif kernel_name == "ffn_bwd" and l + 1 < DEPTH:
                want = [("w_gate", l + 1), ("w_up", l + 1)]
            elif kernel_name == "attn_bwd" and l + 1 < DEPTH:
                want = [("w_in", l + 1), ("w_out", l + 1), ("w_down", l + 1)]
                if l == 0:
                    want += [("w_out", 0)]
            elif kernel_name == "hgrn_bwd" and l == 0:
                want = [(n, 0) for n in ("w_gate", "w_up", "w_down")]
            if not want:
                return None
            self.pending[(kernel_name, l)] = want
            if grads is None:
                return [w_send[n][wl] for n, wl in want], True
            return [to_send(n, grads[n][wl]) for n, wl in want], False

        def done(self, kernel_name, l, outs):
            want = self.pending.pop((kernel_name, l), [])
            for (n, wl), out in zip(want, outs):
                if kernel_name.endswith("_fwd"):
                    self.w[wl][n] = natural(n, out)
                else:
                    self.parts[wl][n] = out

    w_send = {n: shard_t(n, w_loc[n]).astype(BF16) for n in big}
    plan = StepPlan()
    loss_part, grad_x, g = _local_step(x, loss_target, p_full, plan)
    loss = lax.psum(loss_part[0, 0], MESH_AXES)

    pw = g["conv_pw_w"]
    k_pw = w_loc["conv_pw_w"].shape[1]
    pw_send = jnp.moveaxis(pw.reshape(DEPTH, N_DEV, k_pw, pw.shape[-1]), 1, 0).reshape(N_DEV, -1, LANES)
    gathered_small = [n for n in SMALL if n != "conv_pw_w"]
    small_shapes = [g[n].shape for n in gathered_small]
    din_parts, small_parts, pw_parts = _exchange(
        [to_send("w_in", g["w_in"][0]), _pack([g[n] for n in gathered_small], F32, 8), pw_send],
        [False, True, False], "exchange_last_grads")
    plan.parts[0]["w_in"] = din_parts
    big_out = {}
    for n in big:
        res = _sum_adamw([plan.parts[l][n] for l in range(DEPTH)], shard_t(n, w_loc[n]), shard_t(n, m_loc[n]),
                         shard_t(n, v_loc[n]), "sum_adamw_" + n)
        big_out[n] = [shard_t(n, r) for r in res]

    g_small = dict(zip(gathered_small, _unpack(_sum8(small_parts, "sum_small_grads"), small_shapes)))
    g_small["conv_pw_w"] = _sum8(pw_parts, "sum_conv_pw_grads").reshape(w_loc["conv_pw_w"].shape)
    for n in sms:
        if n == "conv_pw_w":
            continue
        ax = SMALL_SHARD_AXIS[n]
        k = w_loc[n].shape[ax]
        g_small[n] = lax.dynamic_slice_in_dim(g_small[n], me * k, k, axis=ax)
    loc_shapes = [w_loc[n].shape for n in SMALL]
    packed = [_pack([d[n] for n in SMALL], F32, 8) for d in (w_loc, g_small, m_loc, v_loc)]
    res = _adamw(*packed)
    small_out = [g_small] + [dict(zip(SMALL, _unpack(r, loc_shapes))) for r in res]

    def pick(i, n):
        return big_out[n][i] if n in BIG_AXIS else small_out[i][n]

    return (loss, grad_x) + tuple(pick(i, n) for i in range(4) for n in WEIGHTS)
```

```python
import functools

import jax
import jax.numpy as jnp
import numpy as np
from jax import lax
from jax.experimental import pallas as pl
from jax.experimental.pallas import tpu as pltpu

F32 = jnp.float32
BF16 = jnp.bfloat16

D_MODEL = 1024
D_ATTN = 512
D_HGRN = 256
D_CONV = 256
HEAD_DIM = 64
N_HEADS = 8
N_KV = 2
GRID_W = 64
ROPE_THETA = 10000.0
CHUNK = 64
F_MIN = 1e-6
CONV_W = 31
CONV_PAD = 15
D_FF = 2816
D_PROJ = 2560
EPS = 1e-6
LN_EPS = 1e-5
DEPTH = 2
ADAM_LR = 0.001
ADAM_B1 = 0.9
ADAM_B2 = 0.999
ADAM_EPS = 1e-08
ADAM_WD = 0.01
ADAM_STEP = 10
N_DEV = 8
MESH_AXES = ("x", "y", "c")

COL_HQ, COL_ZFW, COL_ZBW, COL_HI, COL_HG = 6, 8, 10, 12, 14
COL_CA, COL_CB = 8, 9

LANES = 128
VMEM_LIMIT_MB = 56


def _cparams(dims=None):
    return pltpu.CompilerParams(dimension_semantics=dims, vmem_limit_bytes=VMEM_LIMIT_MB * 2 ** 20)


def _dot(a, b):
    return jnp.dot(a, b, preferred_element_type=F32)


def _dot_nt(a, b):
    return lax.dot_general(a, b, (((1,), (1,)), ((), ())), preferred_element_type=F32)


def _dot_tn(a, b):
    return lax.dot_general(a, b, (((0,), (0,)), ((), ())), preferred_element_type=F32)


def _split_bf16(x, parts):
    out = []
    r = x
    for _ in range(parts):
        p = r.astype(BF16)
        out.append(p)
        r = r - p.astype(F32)
    return out


def _dot_precise(x, m_bf16, parts=3):
    acc = None
    for p in _split_bf16(x, parts):
        t = _dot(p, m_bf16)
        acc = t if acc is None else acc + t
    return acc


def _block_ones(width, group):
    i = np.arange(width)
    return jnp.asarray((i[:, None] // group) == (i[None, :] // group), dtype=BF16)


def _sigmoid(x):
    return 1.0 / (1.0 + jnp.exp(-x))


def _rot(x):
    w = x.shape[1]
    lane = lax.broadcasted_iota(jnp.int32, x.shape, 1)
    first = (lane % 32) < 16
    return jnp.where(first, -pltpu.roll(x, w - 16, 1), pltpu.roll(x, 16, 1))


def _rope(x, cos, sin):
    return x * cos + _rot(x) * sin


def _rope_t(dy, cos, sin):
    return dy * cos - _rot(dy * sin)


def _row(v):
    return v.reshape(1, -1)


def _rms_proj(x, wn, wt, tm):
    t, d = x.shape
    n = wt.shape[0]

    def body(x_ref, wn_ref, w_ref, h_ref, y_ref):
        xv = x_ref[...]
        r = lax.rsqrt(jnp.mean(xv * xv, axis=-1, keepdims=True) + EPS)
        h = (xv * r * wn_ref[...]).astype(BF16)
        h_ref[...] = h
        y_ref[...] = _dot_nt(h, w_ref[...])

    return pl.pallas_call(
        body, name="rms_proj", grid=(t // tm,),
        in_specs=[pl.BlockSpec((tm, d), lambda i: (i, 0)),
                  pl.BlockSpec((1, d), lambda i: (0, 0)),
                  pl.BlockSpec((n, d), lambda i: (0, 0))],
        out_specs=[pl.BlockSpec((tm, d), lambda i: (i, 0)),
                   pl.BlockSpec((tm, n), lambda i: (i, 0))],
        out_shape=[jax.ShapeDtypeStruct((t, d), BF16), jax.ShapeDtypeStruct((t, n), F32)],
        compiler_params=_cparams(("parallel",)),
    )(x, wn, wt)


def _rms_bwd(dh, x, wn):
    r = lax.rsqrt(jnp.mean(x * x, axis=-1, keepdims=True) + EPS)
    g = dh * wn
    dx = r * (g - x * (r * r) * jnp.mean(g * x, axis=-1, keepdims=True))
    return dx, dh * x * r


def _proj_bwd(pieces, wt, x, wn, dres, tm, ride=None):
    t = x.shape[0]
    d = x.shape[1]
    n = wt.shape[0]
    widths = [p.shape[1] for p in pieces]
    offs = [sum(widths[:i]) for i in range(len(widths))]
    assert sum(widths) == n
    npc = len(pieces)
    rd = _ride_plan(ride)

    def body(*refs):
        p_refs = refs[:npc]
        w_ref, x_ref, wn_ref, dr_ref = refs[npc:npc + 4]
        dx_ref, dwn_ref = refs[npc + 4 + rd.n:npc + 6 + rd.n]
        copies = rd.copies(refs[npc + 4:npc + 4 + rd.n], refs[npc + 6 + rd.n:npc + 6 + 2 * rd.n],
                           refs[npc + 6 + 2 * rd.n:])
        _ride_start(copies, pl.program_id(0) == 0)
        dh = None
        for p_ref, o, wd in zip(p_refs, offs, widths):
            part = _dot(p_ref[...], w_ref[o:o + wd, :])
            dh = part if dh is None else dh + part
        dx, dwn = _rms_bwd(dh, x_ref[...], wn_ref[...])
        dx_ref[...] = dr_ref[...] + dx

        @pl.when(pl.program_id(0) == 0)
        def _():
            dwn_ref[...] = jnp.zeros_like(dwn_ref)

        dwn_ref[...] += jnp.sum(dwn, axis=0, keepdims=True)
        _ride_wait(copies, pl.program_id(0) == t // tm - 1)

    return pl.pallas_call(
        body, name="proj_bwd", grid=(t // tm,),
        in_specs=[pl.BlockSpec((tm, wd), lambda i: (i, 0)) for wd in widths]
        + [pl.BlockSpec((n, d), lambda i: (0, 0)),
           pl.BlockSpec((tm, d), lambda i: (i, 0)),
           pl.BlockSpec((1, d), lambda i: (0, 0)),
           pl.BlockSpec((tm, d), lambda i: (i, 0))] + rd.in_specs,
        out_specs=[pl.BlockSpec((tm, d), lambda i: (i, 0)),
                   pl.BlockSpec((1, d), lambda i: (0, 0))] + rd.out_specs,
        out_shape=[jax.ShapeDtypeStruct((t, d), F32), jax.ShapeDtypeStruct((1, d), F32)] + rd.out_shape,
        scratch_shapes=rd.scratch,
        compiler_params=_cparams(("arbitrary",)),
    )(*pieces, wt, x, wn, dres, *rd.srcs)


def _dw_in(h0, pieces, tm):
    t, k = h0.shape
    widths = [p.shape[1] for p in pieces]
    offs = [sum(widths[:i]) for i in range(len(widths))]
    n = sum(widths)
    npc = len(pieces)

    def body(*refs):
        h_ref = refs[0]
        p_refs = refs[1:1 + npc]
        o_ref, acc_ref = refs[1 + npc:]
        i = pl.program_id(0)

        @pl.when(i == 0)
        def _():
            acc_ref[...] = jnp.zeros_like(acc_ref)

        h = h_ref[...]
        for p_ref, o, wd in zip(p_refs, offs, widths):
            acc_ref[o:o + wd, :] += _dot_tn(p_ref[...], h)

        @pl.when(i == t // tm - 1)
        def _():
            o_ref[...] = acc_ref[...].astype(BF16)

    return pl.pallas_call(
        body, name="dw_in", grid=(t // tm,),
        in_specs=[pl.BlockSpec((tm, k), lambda i: (i, 0))]
        + [pl.BlockSpec((tm, wd), lambda i: (i, 0)) for wd in widths],
        out_specs=pl.BlockSpec((n, k), lambda i: (0, 0)),
        out_shape=jax.ShapeDtypeStruct((n, k), BF16),
        scratch_shapes=[pltpu.VMEM((n, k), F32)],
        compiler_params=_cparams(("arbitrary",)),
    )(h0, *pieces)


def _mm_tn(a, b, tn, name, tm):
    t, k = a.shape
    n = b.shape[1]

    def body(a_ref, b_ref, o_ref):
        @pl.when(pl.program_id(1) == 0)
        def _():
            o_ref[...] = jnp.zeros_like(o_ref)

        o_ref[...] += _dot_tn(a_ref[...].astype(BF16), b_ref[...].astype(BF16))

    return pl.pallas_call(
        body, name=name, grid=(n // tn, t // tm),
        in_specs=[pl.BlockSpec((tm, k), lambda j, i: (i, 0)),
                  pl.BlockSpec((tm, tn), lambda j, i: (i, j))],
        out_specs=pl.BlockSpec((k, tn), lambda j, i: (0, j)),
        out_shape=jax.ShapeDtypeStruct((k, n), F32),
        compiler_params=_cparams(("parallel", "arbitrary")),
    )(a, b)


def _dw_ff(blocked, flat, name, tm):
    t, dm = flat.shape
    fb = blocked.shape[2]
    out_blk = (N_DEV, fb, dm)

    def body(b_ref, f_ref, o_ref, acc_ref):
        i = pl.program_id(0)

        @pl.when(i == 0)
        def _():
            acc_ref[...] = jnp.zeros_like(acc_ref)

        fv = f_ref[...].astype(BF16)
        for j in range(N_DEV):
            acc_ref[j] += _dot_tn(b_ref[j], fv)

        @pl.when(i == t // tm - 1)
        def _():
            o_ref[...] = acc_ref[...].astype(BF16)

    return pl.pallas_call(
        body, name=name, grid=(t // tm,),
        in_specs=[pl.BlockSpec((N_DEV, tm, fb), lambda i: (0, i, 0)),
                  pl.BlockSpec((tm, dm), lambda i: (i, 0))],
        out_specs=pl.BlockSpec(out_blk, lambda i: (0, 0, 0)),
        out_shape=jax.ShapeDtypeStruct(out_blk, BF16),
        scratch_shapes=[pltpu.VMEM(out_blk, F32)],
        compiler_params=_cparams(("arbitrary",)),
    )(blocked, flat)


def _rope_tables(s):
    rows = s // GRID_W
    row_id = jnp.repeat(jnp.arange(rows, dtype=F32), GRID_W)
    col_id = jnp.tile(jnp.arange(GRID_W, dtype=F32), rows)
    half = HEAD_DIM // 2
    inv_freq = ROPE_THETA ** (-jnp.arange(0, half, 2, dtype=F32) / half)
    ang_r = row_id[:, None] * inv_freq[None, :]
    ang_c = col_id[:, None] * inv_freq[None, :]
    ang = jnp.concatenate([ang_r, ang_r, ang_c, ang_c], axis=-1)
    cos, sin = jnp.cos(ang), jnp.sin(ang)
    return jnp.tile(cos, (1, N_HEADS)), jnp.tile(sin, (1, N_HEADS))


def _head_rms(x, w, ones):
    r = lax.rsqrt(_dot_precise(x * x, ones, 2) * (1.0 / HEAD_DIM) + EPS)
    return x * r * w, r


def _dup_half(x, kv):
    lane = lax.broadcasted_iota(jnp.int32, x.shape, 1)
    sel = (lane < 64) if kv == 0 else (lane >= 64)
    return jnp.where(sel, x, pltpu.roll(x, 64, 1))


def _qkv_prep(proj, cosq, sinq, qw, kw, ones, s, tm):
    t = proj.shape[0]
    ns = s // tm

    def body(p_ref, cos_ref, sin_ref, qw_ref, kw_ref, ones_ref, q_out, kd_out, vd_out, kdt_out, vdt_out):
        cos = cos_ref[...]
        sin = sin_ref[...]
        ones_m = ones_ref[...]
        qn, _ = _head_rms(p_ref[:, 0:512], qw_ref[...], ones_m)
        q_out[...] = (_rope(qn, cos, sin) * (HEAD_DIM ** -0.5)).astype(BF16)
        kn, _ = _head_rms(p_ref[:, 512:640], kw_ref[...], ones_m[0:128, 0:128])
        kr = _rope(kn, cos[:, 0:128], sin[:, 0:128])
        v = p_ref[:, 640:768]
        for kv in range(N_KV):
            kd = _dup_half(kr, kv)
            vd = _dup_half(v, kv)
            kd_out[kv] = kd.astype(BF16)
            vd_out[kv] = vd.astype(BF16)
            kdt_out[kv] = kd.T.astype(BF16)
            vdt_out[kv] = vd.T.astype(BF16)

    return pl.pallas_call(
        body, name="qkv_prep", grid=(t // tm,),
        in_specs=[pl.BlockSpec((tm, 768), lambda i: (i, 0)),
                  pl.BlockSpec((tm, 512), lambda i: (i % ns, 0)),
                  pl.BlockSpec((tm, 512), lambda i: (i % ns, 0)),
                  pl.BlockSpec((1, 512), lambda i: (0, 0)),
                  pl.BlockSpec((1, 128), lambda i: (0, 0)),
                  pl.BlockSpec((512, 512), lambda i: (0, 0))],
        out_specs=[pl.BlockSpec((tm, 512), lambda i: (i, 0)),
                   pl.BlockSpec((N_KV, tm, 128), lambda i: (0, i, 0)),
                   pl.BlockSpec((N_KV, tm, 128), lambda i: (0, i, 0)),
                   pl.BlockSpec((N_KV, 128, tm), lambda i: (0, 0, i)),
                   pl.BlockSpec((N_KV, 128, tm), lambda i: (0, 0, i))],
        out_shape=[jax.ShapeDtypeStruct((t, 512), BF16),
                   jax.ShapeDtypeStruct((N_KV, t, 128), BF16),
                   jax.ShapeDtypeStruct((N_KV, t, 128), BF16),
                   jax.ShapeDtypeStruct((N_KV, 128, t), BF16),
                   jax.ShapeDtypeStruct((N_KV, 128, t), BF16)],
        compiler_params=_cparams(("parallel",)),
    )(proj, cosq, sinq, qw, kw, ones)


def _qkv_bwd(proj, dq, dkd, dvd, cosq, sinq, qw, kw, ones, s, tm):
    t = proj.shape[0]
    ns = s // tm

    def body(p_ref, dq_ref, dkd_ref, dvd_ref, cos_ref, sin_ref, qw_ref, kw_ref, ones_ref,
             out_ref, dqw_ref, dkw_ref):
        cos = cos_ref[...]
        sin = sin_ref[...]
        ones_m = ones_ref[...]
        ones_k = ones_m[0:128, 0:128]

        def norm_bwd(x, w, dn, om):
            r = lax.rsqrt(_dot_precise(x * x, om, 2) * (1.0 / HEAD_DIM) + EPS)
            g = dn * w
            dx = r * (g - x * (r * r) * (_dot_precise(g * x, om, 2) * (1.0 / HEAD_DIM)))
            return dx, jnp.sum(dn * x * r, axis=0, keepdims=True)

        q = p_ref[:, 0:512]
        dqn = _rope_t(dq_ref[...], cos, sin) * (HEAD_DIM ** -0.5)
        dq_raw, dqw = norm_bwd(q, qw_ref[...], dqn, ones_m)
        out_ref[:, 0:512] = dq_raw.astype(BF16)

        lane = lax.broadcasted_iota(jnp.int32, (tm, 128), 1)

        def fold(ref):
            a0 = ref[0]
            a1 = ref[1]
            f0 = a0 + pltpu.roll(a0, 64, 1)
            f1 = a1 + pltpu.roll(a1, 64, 1)
            return jnp.where(lane < 64, f0, f1)

        k = p_ref[:, 512:640]
        dkn = _rope_t(fold(dkd_ref), cos[:, 0:128], sin[:, 0:128])
        dk_raw, dkw = norm_bwd(k, kw_ref[...], dkn, ones_k)
        out_ref[:, 512:640] = dk_raw.astype(BF16)
        out_ref[:, 640:768] = fold(dvd_ref).astype(BF16)

        @pl.when(pl.program_id(0) == 0)
        def _():
            dqw_ref[...] = jnp.zeros_like(dqw_ref)
            dkw_ref[...] = jnp.zeros_like(dkw_ref)

        dqw_ref[...] += dqw
        dkw_ref[...] += dkw

    return pl.pallas_call(
        body, name="qkv_bwd", grid=(t // tm,),
        in_specs=[pl.BlockSpec((tm, 768), lambda i: (i, 0)),
                  pl.BlockSpec((tm, 512), lambda i: (i, 0)),
                  pl.BlockSpec((N_KV, tm, 128), lambda i: (0, i, 0)),
                  pl.BlockSpec((N_KV, tm, 128), lambda i: (0, i, 0)),
                  pl.BlockSpec((tm, 512), lambda i: (i % ns, 0)),
                  pl.BlockSpec((tm, 512), lambda i: (i % ns, 0)),
                  pl.BlockSpec((1, 512), lambda i: (0, 0)),
                  pl.BlockSpec((1, 128), lambda i: (0, 0)),
                  pl.BlockSpec((512, 512), lambda i: (0, 0))],
        out_specs=[pl.BlockSpec((tm, 768), lambda i: (i, 0)),
                   pl.BlockSpec((1, 512), lambda i: (0, 0)),
                   pl.BlockSpec((1, 128), lambda i: (0, 0))],
        out_shape=[jax.ShapeDtypeStruct((t, 768), BF16),
                   jax.ShapeDtypeStruct((1, 512), F32),
                   jax.ShapeDtypeStruct((1, 128), F32)],
        compiler_params=_cparams(("arbitrary",)),
    )(proj, dq, dkd, dvd, cosq, sinq, qw, kw, ones)


def _grid_step_id(grid):
    idx = pl.program_id(0)
    for ax in range(1, len(grid)):
        idx = idx * grid[ax] + pl.program_id(ax)
    return idx


def _attn_fwd(q, kd, vdt, nb, s, tq, ride=None):
    t = q.shape[0]
    nq = s // tq
    rd = _ride_plan(ride)
    grid = (nb, N_HEADS // 2, nq)
    nsteps = nb * (N_HEADS // 2) * nq

    def body(*refs):
        q_ref, k_ref, vt_ref = refs[:3]
        o_ref, lse_ref = refs[3 + rd.n:5 + rd.n]
        copies = rd.copies(refs[3:3 + rd.n], refs[5 + rd.n:5 + 2 * rd.n], refs[5 + 2 * rd.n:])
        step_id = _grid_step_id(grid)
        _ride_start(copies, step_id == 0)
        qv = q_ref[...].astype(F32)
        lane = lax.broadcasted_iota(jnp.int32, qv.shape, 1)
        k = k_ref[0]
        vt = vt_ref[0]
        outs = []
        scores = [_dot_nt(k, jnp.where((lane < 64) if half == 0 else (lane >= 64), qv, 0.0).astype(BF16))
                  for half in range(2)]
        for half in range(2):
            st = scores[half]
            m = jnp.max(st, axis=0, keepdims=True)
            p = jnp.exp(st - m)
            l = jnp.sum(p, axis=0, keepdims=True)
            ot = _dot(vt, p.astype(BF16)) / l
            lse_ref[0, half] = m + jnp.log(l)
            outs.append(ot)
        row = lax.broadcasted_iota(jnp.int32, outs[0].shape, 0)
        o_ref[...] = jnp.where(row < 64, outs[0], outs[1]).T
        _ride_wait(copies, step_id == nsteps - 1)

    return pl.pallas_call(
        body, name="attn_fwd", grid=grid,
        in_specs=[pl.BlockSpec((tq, 128), lambda b, p, i: (b * nq + i, p)),
                  pl.BlockSpec((1, s, 128), lambda b, p, i: (p // 2, b, 0)),
                  pl.BlockSpec((1, 128, s), lambda b, p, i: (p // 2, 0, b))] + rd.in_specs,
        out_specs=[pl.BlockSpec((tq, 128), lambda b, p, i: (b * nq + i, p)),
                   pl.BlockSpec((1, 2, 1, tq), lambda b, p, i: (b, p, 0, i))] + rd.out_specs,
        out_shape=[jax.ShapeDtypeStruct((t, D_ATTN), F32),
                   jax.ShapeDtypeStruct((nb, N_HEADS, 1, s), F32)] + rd.out_shape,
        scratch_shapes=rd.scratch,
        compiler_params=_cparams(("arbitrary", "arbitrary", "arbitrary")),
    )(q, kd, vdt, *rd.srcs)


def _attn_bwd(q, kd, vd, kdt, o, lse, do, nb, s, tq, ride=None):
    t = q.shape[0]
    nq = s // tq
    ones8 = jnp.ones((8, 128), BF16)
    rd = _ride_plan(ride)
    grid = (nb, N_KV, 2, nq)
    nsteps = nb * N_KV * 2 * nq

    def body(*refs):
        q_ref, k_ref, v_ref, kt_ref, o_ref, lse_ref, do_ref, ones_ref = refs[:8]
        dq_ref, dk_ref, dv_ref = refs[8 + rd.n:11 + rd.n]
        copies = rd.copies(refs[8:8 + rd.n], refs[11 + rd.n:11 + 2 * rd.n], refs[11 + 2 * rd.n:])
        step_id = _grid_step_id(grid)
        _ride_start(copies, step_id == 0)

        @pl.when((pl.program_id(2) == 0) & (pl.program_id(3) == 0))
        def _():
            dk_ref[...] = jnp.zeros_like(dk_ref)
            dv_ref[...] = jnp.zeros_like(dv_ref)

        qv = q_ref[...].astype(F32)
        dov = do_ref[...]
        ov = o_ref[...]
        lane = lax.broadcasted_iota(jnp.int32, qv.shape, 1)
        k = k_ref[0]
        v = v_ref[0]
        kt = kt_ref[0]
        dqs = []
        dk_acc = None
        dv_acc = None
        for half in range(2):
            sel = (lane < 64) if half == 0 else (lane >= 64)
            qh = jnp.where(sel, qv, 0.0).astype(BF16)
            doh = jnp.where(sel, dov, 0.0)
            dob = doh.astype(BF16)
            delta = None
            for part in _split_bf16(doh * ov, 3):
                d8 = _dot_nt(ones_ref[...], part)
                delta = d8 if delta is None else delta + d8
            delta = delta[0:1, :]
            st = _dot_nt(k, qh)
            pt = jnp.exp(st - lse_ref[0, half])
            dpt = _dot_nt(v, dob)
            dst = (pt * (dpt - delta)).astype(BF16)
            dkh = _dot(dst, qh)
            dvh = _dot(pt.astype(BF16), dob)
            dk_acc = dkh if dk_acc is None else dk_acc + dkh
            dv_acc = dvh if dv_acc is None else dv_acc + dvh
            dqs.append(_dot(kt, dst))
        dk_ref[0] += dk_acc
        dv_ref[0] += dv_acc
        row = lax.broadcasted_iota(jnp.int32, dqs[0].shape, 0)
        dq_ref[...] = jnp.where(row < 64, dqs[0], dqs[1]).T
        _ride_wait(copies, step_id == nsteps - 1)

    qmap = lambda b, g, p, i: (b * nq + i, g * 2 + p)
    kvmap = lambda b, g, p, i: (g, b, 0)
    return pl.pallas_call(
        body, name="attn_bwd", grid=grid,
        in_specs=[pl.BlockSpec((tq, 128), qmap),
                  pl.BlockSpec((1, s, 128), kvmap),
                  pl.BlockSpec((1, s, 128), kvmap),
                  pl.BlockSpec((1, 128, s), lambda b, g, p, i: (g, 0, b)),
                  pl.BlockSpec((tq, 128), qmap),
                  pl.BlockSpec((1, 2, 1, tq), lambda b, g, p, i: (b, g * 2 + p, 0, i)),
                  pl.BlockSpec((tq, 128), qmap),
                  pl.BlockSpec((8, 128), lambda b, g, p, i: (0, 0))] + rd.in_specs,
        out_specs=[pl.BlockSpec((tq, 128), qmap),
                   pl.BlockSpec((1, s, 128), kvmap),
                   pl.BlockSpec((1, s, 128), kvmap)] + rd.out_specs,
        out_shape=[jax.ShapeDtypeStruct((t, D_ATTN), F32),
                   jax.ShapeDtypeStruct((N_KV, t, 128), F32),
                   jax.ShapeDtypeStruct((N_KV, t, 128), F32)] + rd.out_shape,
        scratch_shapes=rd.scratch,
        compiler_params=_cparams(("arbitrary", "arbitrary", "arbitrary", "arbitrary")),
    )(q, kd, vd, kdt, o, lse, do, ones8, *rd.srcs)


SUB = 16
N_SUB = CHUNK // SUB


def _tri_mats():
    i = np.arange(CHUNK)
    same = (i[:, None] // SUB) == (i[None, :] // SUB)
    lower = jnp.asarray(same & (i[:, None] >= i[None, :]), dtype=BF16)
    upper = jnp.asarray(same & (i[:, None] <= i[None, :]), dtype=BF16)
    return jnp.stack([lower, upper])


def _running_sum(tri, x):
    acc = None
    for part in _split_bf16(x, 3):
        t = _dot(tri, part)
        acc = t if acc is None else acc + t
    return acc


def _gates(z, lb):
    sig = _sigmoid(z)
    f = lb + (1.0 - lb) * sig
    logf = jnp.log(jnp.maximum(f, F_MIN))
    k = (1.0 - lb) * (1.0 - sig)
    return sig, f, logf, k


def _row_group(jg, anti):
    if anti:
        return 0, 8 * jg + 8
    return 8 * jg, SUB


def _sub_order(anti):
    return range(N_SUB - 1, -1, -1) if anti else range(N_SUB)


def _block_columns(b, anti):
    tt = lax.broadcasted_iota(jnp.int32, (SUB, LANES), 0)
    cols = []
    for jg in range(SUB // 8):
        r0, r1 = _row_group(jg, anti)
        br = b[r0:r1]
        tr = tt[r0:r1]
        for i in range(8):
            sc = 8 * jg + i
            mask = (tr <= sc) if anti else (tr >= sc)
            cols.append((r0, r1, sc, jnp.where(mask, jnp.exp(jnp.minimum(br - b[sc:sc + 1], 0.0)), 0.0)))
    return cols


def _lockstep(gens):
    results = [None] * len(gens)
    live = list(range(len(gens)))
    while live:
        for i in list(live):
            try:
                next(gens[i])
            except StopIteration as stop:
                results[i] = stop.value
                live.remove(i)
    return results


def _scatter_rows(base, accs):
    pieces = []
    for g in range(SUB // 8):
        tot = base[8 * g:8 * g + 8]
        for (r0, r1), acc in accs.items():
            if r0 <= 8 * g and 8 * g + 8 <= r1:
                tot = tot + acc[8 * g - r0:8 * g - r0 + 8]
        pieces.append(tot)
    return jnp.concatenate(pieces, axis=0)


def _chunk_fwd(q, k, v, b, st, bones, bmask, anti):
    rs = [slice(SUB * i, SUB * i + SUB) for i in range(N_SUB)]
    decay, update, prods, spans, qbs = [], [], [], [], []
    for i in range(N_SUB):
        qi, ki, vi, bi = q[rs[i]], k[rs[i]], v[rs[i]], b[rs[i]]
        b_last = bi[0:1] if anti else bi[SUB - 1:SUB]
        decay.append(jnp.exp(b_last))
        update.append(_dot_tn(vi.astype(BF16), (ki * jnp.exp(b_last - bi)).astype(BF16)) * bmask)
        qbs.append((qi * jnp.exp(bi)).astype(BF16))
        for r0, r1, sc, e in _block_columns(bi, anti):
            prods.append(qi[r0:r1] * e * ki[sc:sc + 1])
            spans.append((i, r0, r1, sc))
    pb = _dot(jnp.concatenate(prods, axis=0).astype(BF16), bones)
    yield
    entered = [None] * N_SUB
    for i in _sub_order(anti):
        entered[i] = st
        st = st * decay[i] + update[i]
    yield
    accs = [dict() for _ in range(N_SUB)]
    off = 0
    for i, r0, r1, sc in spans:
        term = pb[off:off + r1 - r0] * v[rs[i]][sc:sc + 1]
        off += r1 - r0
        accs[i][(r0, r1)] = term if (r0, r1) not in accs[i] else accs[i][(r0, r1)] + term
    outs = [_scatter_rows(_dot_nt(qbs[i], entered[i].astype(BF16)), accs[i]) for i in range(N_SUB)]
    return jnp.concatenate(outs, axis=0), st, entered


def _hgrn_fwd(proj, lb, gw, nb, s, ride=None):
    t = proj.shape[0]
    nc = s // CHUNK
    tri = _tri_mats()
    bones = _block_ones(LANES, HEAD_DIM)
    rd = _ride_plan(ride)

    def body(*refs):
        q_ref, zf_ref, zb_ref, v_ref, g_ref, lb_ref, gw_ref, tri_ref, bones_ref = refs[:9]
        y_ref, os_ref, sts_ref = refs[9 + rd.n:12 + rd.n]
        st_ref = refs[12 + 2 * rd.n]
        copies = rd.copies(refs[9:9 + rd.n], refs[12 + rd.n:12 + 2 * rd.n], refs[13 + 2 * rd.n:])
        step_id = pl.program_id(0) * 2 + pl.program_id(1)
        _ride_start(copies, step_id == 0)
        bones_m = bones_ref[...]
        bmask = bones_m.astype(F32)
        st_ref[...] = jnp.zeros_like(st_ref)

        def one_direction(n, anti):
            side = 1 if anti else 0
            z_ref = zb_ref if anti else zf_ref
            cn = (nc - 1 - n) if anti else n
            rows = pl.ds(pl.multiple_of(cn * CHUNK, CHUNK), CHUNK)
            q = q_ref[rows, :]
            v = v_ref[rows, :]
            _, _, logf, k = _gates(z_ref[rows, :], lb_ref[side:side + 1])
            b = _running_sum(tri_ref[side], logf)
            yield
            o, st_new, entered = yield from _chunk_fwd(q, k, v, b, st_ref[side], bones_m, bmask, anti)
            for i in range(N_SUB):
                sts_ref[0, 0, side, cn * N_SUB + i] = entered[i].astype(BF16)
            st_ref[side] = st_new
            (y_ref if anti else os_ref)[rows, :] = o

        def step(n, carry):
            _lockstep([one_direction(n, False), one_direction(n, True)])
            return carry

        lax.fori_loop(0, nc, step, 0)

        def join(n, carry):
            rows = pl.ds(pl.multiple_of(n * CHUNK, CHUNK), CHUNK)
            osum = os_ref[rows, :] + y_ref[rows, :]
            os_ref[rows, :] = osum
            r = lax.rsqrt(_dot_precise(osum * osum, bones_m, 2) * (1.0 / HEAD_DIM) + EPS)
            hg = g_ref[rows, :]
            y_ref[rows, :] = osum * r * gw_ref[...] * (hg * _sigmoid(hg))
            return carry

        lax.fori_loop(0, nc, join, 0)
        _ride_wait(copies, step_id == nb * 2 - 1)

    def col(c):
        return pl.BlockSpec((s, LANES), lambda b, p, c=c: (b, c + p))

    return pl.pallas_call(
        body, name="hgrn_fwd", grid=(nb, 2),
        in_specs=[col(COL_HQ), col(COL_ZFW), col(COL_ZBW), col(COL_HI), col(COL_HG),
                  pl.BlockSpec((2, LANES), lambda b, p: (0, p)),
                  pl.BlockSpec((1, LANES), lambda b, p: (0, 0)),
                  pl.BlockSpec((2, CHUNK, CHUNK), lambda b, p: (0, 0, 0)),
                  pl.BlockSpec((LANES, LANES), lambda b, p: (0, 0))] + rd.in_specs,
        out_specs=[pl.BlockSpec((s, LANES), lambda b, p: (b, p)),
                   pl.BlockSpec((s, LANES), lambda b, p: (b, p)),
                   pl.BlockSpec((1, 1, 2, nc * N_SUB, LANES, LANES), lambda b, p: (b, p, 0, 0, 0, 0))]
        + rd.out_specs,
        out_shape=[jax.ShapeDtypeStruct((t, D_HGRN), F32), jax.ShapeDtypeStruct((t, D_HGRN), F32),
                   jax.ShapeDtypeStruct((nb, 2, 2, nc * N_SUB, LANES, LANES), BF16)] + rd.out_shape,
        scratch_shapes=[pltpu.VMEM((2, LANES, LANES), F32)] + rd.scratch,
        compiler_params=_cparams(("arbitrary", "arbitrary")),
    )(proj, proj, proj, proj, proj, lb, gw, tri, bones, *rd.srcs)


def _chunk_bwd(q, k, v, b, do, states, rt, bones, bmask, anti):
    rs = [slice(SUB * i, SUB * i + SUB) for i in range(N_SUB)]
    r8 = lax.broadcasted_iota(jnp.int32, (8, LANES), 0)
    decay, update, dq_inter, ebls, prods_p, prods_d, spans, qes, kes = [], [], [], [], [], [], [], [], []
    for i in range(N_SUB):
        qi, ki, vi, bi, doi = q[rs[i]], k[rs[i]], v[rs[i]], b[rs[i]], do[rs[i]]
        b_last = bi[0:1] if anti else bi[SUB - 1:SUB]
        eb = jnp.exp(bi)
        dob = doi.astype(BF16)
        decay.append(jnp.exp(b_last))
        ebls.append(jnp.exp(b_last - bi))
        update.append(_dot_tn(dob, (qi * eb).astype(BF16)) * bmask)
        dq_inter.append(eb * _dot(dob, states[i]))
        for r0, r1, sc, e in _block_columns(bi, anti):
            qe = qi[r0:r1] * e
            qes.append(qe)
            kes.append(e * ki[sc:sc + 1])
            prods_p.append(qe * ki[sc:sc + 1])
            prods_d.append(doi[r0:r1] * vi[sc:sc + 1])
            spans.append((i, r0, r1, sc))
    sums = _dot(jnp.concatenate(prods_p + prods_d, axis=0).astype(BF16), bones)
    half = sum(r1 - r0 for _, r0, r1, _ in spans)
    yield
    entered = [None] * N_SUB
    for i in reversed(list(_sub_order(anti))):
        entered[i] = rt
        rt = rt * decay[i] + update[i]
    yield
    accs = [dict() for _ in range(N_SUB)]
    dk_blks = [[jnp.zeros((8, LANES), F32) for _ in range(SUB // 8)] for _ in range(N_SUB)]
    dv_blks = [[jnp.zeros((8, LANES), F32) for _ in range(SUB // 8)] for _ in range(N_SUB)]
    off = 0
    for n, (i, r0, r1, sc) in enumerate(spans):
        nr = r1 - r0
        pb = sums[off:off + nr]
        dpb = sums[half + off:half + off + nr]
        off += nr
        term = dpb * kes[n]
        accs[i][(r0, r1)] = term if (r0, r1) not in accs[i] else accs[i][(r0, r1)] + term
        dk_s = jnp.sum(dpb * qes[n], axis=0, keepdims=True)
        dv_s = jnp.sum(pb * do[rs[i]][r0:r1], axis=0, keepdims=True)
        dk_blks[i][sc // 8] = jnp.where(r8 == sc % 8, dk_s, dk_blks[i][sc // 8])
        dv_blks[i][sc // 8] = jnp.where(r8 == sc % 8, dv_s, dv_blks[i][sc // 8])
    dqs, dks, dvs, dbs = [], [], [], []
    for i in range(N_SUB):
        ki, vi = k[rs[i]], v[rs[i]]
        rtb = entered[i].astype(BF16)
        dk_inter = ebls[i] * _dot(vi.astype(BF16), rtb)
        dv_inter = _dot_nt((ki * ebls[i]).astype(BF16), rtb)
        dqs.append(_scatter_rows(dq_inter[i], accs[i]))
        dks.append(dk_inter + jnp.concatenate(dk_blks[i], axis=0))
        dvs.append(dv_inter + jnp.concatenate(dv_blks[i], axis=0))
        db_last = (jnp.sum(ki * dk_inter, axis=0, keepdims=True)
                   + decay[i] * jnp.sum(entered[i] * states[i].astype(F32), axis=0, keepdims=True))
        dbs.append(jnp.broadcast_to(db_last, (SUB, LANES)))
    cat = lambda xs: jnp.concatenate(xs, axis=0)
    return cat(dqs), cat(dks), cat(dvs), rt, cat(dbs)


def _hgrn_bwd(proj, lb, gw, osum, states, dy, nb, s, ride=None):
    t = proj.shape[0]
    nc = s // CHUNK
    assert nc % 2 == 0
    tri = _tri_mats()
    bones = _block_ones(LANES, HEAD_DIM)
    rd = _ride_plan(ride)

    def body(*refs):
        (q_ref, zf_ref, zb_ref, v_ref, g_ref, lb_ref, gw_ref, os_ref, sts_ref, dy_ref, tri_ref,
         bones_ref) = refs[:12]
        dq_ref, dzf_ref, dzb_ref, dv_ref, dg_ref, dgw_ref, dlb_ref = refs[12 + rd.n:19 + rd.n]
        do_sc, dq_sc, dv_sc, rt_cur = refs[19 + 2 * rd.n:23 + 2 * rd.n]
        copies = rd.copies(refs[12:12 + rd.n], refs[19 + rd.n:19 + 2 * rd.n], refs[23 + 2 * rd.n:])
        step_id = pl.program_id(0) * 2 + pl.program_id(1)
        _ride_start(copies, step_id == 0)
        bones_m = bones_ref[...]
        bmask = bones_m.astype(F32)
        gwv = gw_ref[...]

        def head(n, acc):
            rows = pl.ds(pl.multiple_of(n * CHUNK, CHUNK), CHUNK)
            o = os_ref[rows, :]
            hg = g_ref[rows, :]
            dyv = dy_ref[rows, :]
            sg = _sigmoid(hg)
            r = lax.rsqrt(_dot_precise(o * o, bones_m, 2) * (1.0 / HEAD_DIM) + EPS)
            nrm = o * r * gwv
            dn = dyv * (hg * sg)
            dg_ref[rows, :] = (dyv * nrm * (sg * (1.0 + hg * (1.0 - sg)))).astype(BF16)
            g = dn * gwv
            mean_go = _dot_precise(g * o, bones_m, 2) * (1.0 / HEAD_DIM)
            do_sc[rows, :] = r * (g - o * (r * r) * mean_go)
            return acc + jnp.sum(dn * o * r, axis=0, keepdims=True)

        dgw_ref[0] = lax.fori_loop(0, nc, head, jnp.zeros((1, LANES), F32))
        dq_sc[...] = jnp.zeros_like(dq_sc)
        dv_sc[...] = jnp.zeros_like(dv_sc)

        rt_cur[...] = jnp.zeros_like(rt_cur)

        def one_direction(n, anti):
            side = 1 if anti else 0
            z_ref = zb_ref if anti else zf_ref
            dz_ref = dzb_ref if anti else dzf_ref
            lbv = lb_ref[side:side + 1]
            cn = n if anti else (nc - 1 - n)
            rows = pl.ds(pl.multiple_of(cn * CHUNK, CHUNK), CHUNK)
            q = q_ref[rows, :]
            v = v_ref[rows, :]
            sig, f, logf, k = _gates(z_ref[rows, :], lbv)
            b = _running_sum(tri_ref[side], logf)
            yield
            do = do_sc[rows, :]
            entered = [sts_ref[0, 0, side, cn * N_SUB + i] for i in range(N_SUB)]
            dq, dk, dv, rt_new, db_last = yield from _chunk_bwd(q, k, v, b, do, entered, rt_cur[side], bones_m,
                                                                bmask, anti)
            rt_cur[side] = rt_new
            dq_sc[rows, :] += dq
            dv_sc[rows, :] += dv
            dlogf = _running_sum(tri_ref[1 - side], q * dq - k * dk) + db_last
            dfl = jnp.where(f > F_MIN, dlogf / f, 0.0)
            dz_ref[rows, :] = ((dfl - dk) * (1.0 - lbv) * sig * (1.0 - sig)).astype(BF16)
            return jnp.sum((dfl - dk) * (1.0 - sig), axis=0, keepdims=True)

        def back(n, dlb):
            d0, d1 = _lockstep([one_direction(n, False), one_direction(n, True)])
            return dlb[0] + d0, dlb[1] + d1

        zero = jnp.zeros((1, LANES), F32)
        dlb0, dlb1 = lax.fori_loop(0, nc, back, (zero, zero))
        dlb_ref[0, 0:1, :] = dlb0
        dlb_ref[0, 1:2, :] = dlb1

        dq_ref[...] = dq_sc[...].astype(BF16)
        dv_ref[...] = dv_sc[...].astype(BF16)
        _ride_wait(copies, step_id == nb * 2 - 1)

    def col(c):
        return pl.BlockSpec((s, LANES), lambda b, p, c=c: (b, c + p))

    sl = pl.BlockSpec((s, LANES), lambda b, p: (b, p))
    out_t = jax.ShapeDtypeStruct((t, D_HGRN), BF16)
    return pl.pallas_call(
        body, name="hgrn_bwd", grid=(nb, 2),
        in_specs=[col(COL_HQ), col(COL_ZFW), col(COL_ZBW), col(COL_HI), col(COL_HG),
                  pl.BlockSpec((2, LANES), lambda b, p: (0, p)),
                  pl.BlockSpec((1, LANES), lambda b, p: (0, 0)),
                  sl,
                  pl.BlockSpec((1, 1, 2, nc * N_SUB, LANES, LANES), lambda b, p: (b, p, 0, 0, 0, 0)),
                  sl,
                  pl.BlockSpec((2, CHUNK, CHUNK), lambda b, p: (0, 0, 0)),
                  pl.BlockSpec((LANES, LANES), lambda b, p: (0, 0))] + rd.in_specs,
        out_specs=[sl, sl, sl, sl, sl,
                   pl.BlockSpec((1, 1, LANES), lambda b, p: (b, 0, p)),
                   pl.BlockSpec((1, 2, LANES), lambda b, p: (b, 0, p))] + rd.out_specs,
        out_shape=[out_t, out_t, out_t, out_t, out_t,
                   jax.ShapeDtypeStruct((nb, 1, D_HGRN), F32),
                   jax.ShapeDtypeStruct((nb, 2, D_HGRN), F32)] + rd.out_shape,
        scratch_shapes=[pltpu.VMEM((s, LANES), F32), pltpu.VMEM((s, LANES), F32), pltpu.VMEM((s, LANES), F32),
                        pltpu.VMEM((2, LANES, LANES), F32)] + rd.scratch,
        compiler_params=_cparams(("arbitrary", "arbitrary")),
    )(proj, proj, proj, proj, proj, lb, gw, osum, states, dy, tri, bones, *rd.srcs)


def _lower_bounds(logits):
    def body(lg_ref, lb_ref):
        rows = [lg_ref[l:l + 1, :] for l in range(DEPTH)]
        m = functools.reduce(jnp.maximum, rows)
        ex = [jnp.exp(r - m) for r in rows]
        den = functools.reduce(jnp.add, ex)
        run = jnp.zeros_like(m)
        for l in range(DEPTH):
            if l > 0:
                run = run + ex[l] / den
            lb_ref[l:l + 1, :] = run

    return pl.pallas_call(body, name="lower_bounds", out_shape=jax.ShapeDtypeStruct(logits.shape, F32))(logits)


def _lower_bounds_bwd(logits, dlb):
    def body(lg_ref, dlb_ref, dlg_ref):
        rows = [lg_ref[l:l + 1, :] for l in range(DEPTH)]
        m = functools.reduce(jnp.maximum, rows)
        ex = [jnp.exp(r - m) for r in rows]
        den = functools.reduce(jnp.add, ex)
        sm = [e / den for e in ex]
        dsm = [jnp.zeros_like(m) for _ in range(DEPTH)]
        for i in range(1, DEPTH):
            for l in range(i, DEPTH):
                dsm[i] = dsm[i] + dlb_ref[l:l + 1, :]
        dot = functools.reduce(jnp.add, [sm[i] * dsm[i] for i in range(DEPTH)])
        for i in range(DEPTH):
            dlg_ref[i:i + 1, :] = sm[i] * (dsm[i] - dot)

    return pl.pallas_call(body, name="lower_bounds_bwd", out_shape=jax.ShapeDtypeStruct(logits.shape, F32))(logits, dlb)


CONV_ROWS = 128


def _conv_core(a, bg, dww, dwb, lnw, lnb, upad_ref, s):
    sb = _sigmoid(bg)
    u = a * sb
    upad_ref[0:16, :] = jnp.zeros((16, D_CONV), F32)
    upad_ref[16:16 + s, :] = u
    upad_ref[16 + s:32 + s, :] = jnp.zeros((16, D_CONV), F32)
    rows = min(s, CONV_ROWS)
    pieces = []
    for r0 in range(0, s, rows):
        acc = None
        for j in range(CONV_W):
            term = upad_ref[r0 + 1 + j:r0 + 1 + j + rows, :] * dww[j:j + 1, :]
            acc = term if acc is None else acc + term
        pieces.append(acc)
    c = jnp.concatenate(pieces, axis=0) + dwb
    mu = jnp.mean(c, axis=-1, keepdims=True)
    xc = c - mu
    rstd = lax.rsqrt(jnp.mean(xc * xc, axis=-1, keepdims=True) + LN_EPS)
    nh = xc * rstd
    l = nh * lnw + lnb
    sl = _sigmoid(l)
    return sb, nh, rstd, l, sl


def _conv_fwd(proj, dww, dwb, lnw, lnb, pww, pwb, nb, s):
    t = proj.shape[0]
    assert s % min(s, CONV_ROWS) == 0

    def body(a_ref, b_ref, dww_ref, dwb_ref, lnw_ref, lnb_ref, pww_ref, pwb_ref, y_ref, upad_ref):
        _, _, _, l, sl = _conv_core(a_ref[...], b_ref[...], dww_ref[...], dwb_ref[...], lnw_ref[...],
                                    lnb_ref[...], upad_ref, s)
        y_ref[...] = _dot((l * sl).astype(BF16), pww_ref[...]) + pwb_ref[...]

    vec = pl.BlockSpec((1, D_CONV), lambda b: (0, 0))
    return pl.pallas_call(
        body, name="conv_fwd", grid=(nb,),
        in_specs=[pl.BlockSpec((s, D_CONV), lambda b: (b, COL_CA)),
                  pl.BlockSpec((s, D_CONV), lambda b: (b, COL_CB)),
                  pl.BlockSpec((32, D_CONV), lambda b: (0, 0)), vec, vec, vec,
                  pl.BlockSpec((D_CONV, D_CONV), lambda b: (0, 0)), vec],
        out_specs=pl.BlockSpec((s, D_CONV), lambda b: (b, 0)),
        out_shape=jax.ShapeDtypeStruct((t, D_CONV), F32),
        scratch_shapes=[pltpu.VMEM((s + 32, D_CONV), F32)],
        compiler_params=_cparams(("parallel",)),
    )(proj, proj, dww, dwb, lnw, lnb, pww, pwb)


def _conv_bwd(proj, dy, dww, dwb, lnw, lnb, pww, nb, s):
    t = proj.shape[0]

    def body(a_ref, b_ref, dy_ref, dww_ref, dwb_ref, lnw_ref, lnb_ref, pww_ref,
             dab_ref, ddww_ref, ddwb_ref, dlnw_ref, dlnb_ref, dpww_ref, dpwb_ref, upad_ref, dcpad_ref):
        a = a_ref[...]
        dww = dww_ref[...]
        sb, nh, rstd, l, sl = _conv_core(a, b_ref[...], dww, dwb_ref[...], lnw_ref[...], lnb_ref[...],
                                         upad_ref, s)
        dyv = dy_ref[...]
        dyb = dyv.astype(BF16)
        ds = _dot_nt(dyb, pww_ref[...])
        dl = ds * (sl * (1.0 + l * (1.0 - sl)))
        dn = dl * lnw_ref[...]
        dc = rstd * (dn - jnp.mean(dn, axis=-1, keepdims=True)
                     - nh * jnp.mean(dn * nh, axis=-1, keepdims=True))

        @pl.when(pl.program_id(0) == 0)
        def _():
            for r in (ddww_ref, ddwb_ref, dlnw_ref, dlnb_ref, dpww_ref, dpwb_ref):
                r[...] = jnp.zeros_like(r)

        dpww_ref[...] += _dot_tn((l * sl).astype(BF16), dyb)
        dpwb_ref[...] += jnp.sum(dyv, axis=0, keepdims=True)
        dlnw_ref[...] += jnp.sum(dl * nh, axis=0, keepdims=True)
        dlnb_ref[...] += jnp.sum(dl, axis=0, keepdims=True)
        ddwb_ref[...] += jnp.sum(dc, axis=0, keepdims=True)

        dcpad_ref[0:16, :] = jnp.zeros((16, D_CONV), F32)
        dcpad_ref[16:16 + s, :] = dc
        dcpad_ref[16 + s:32 + s, :] = jnp.zeros((16, D_CONV), F32)
        rows = min(s, CONV_ROWS)
        r8 = lax.broadcasted_iota(jnp.int32, (32, D_CONV), 0)
        ddww = jnp.zeros((32, D_CONV), F32)
        pieces = []
        for r0 in range(0, s, rows):
            acc = None
            dcr = dcpad_ref[16 + r0:16 + r0 + rows, :]
            for j in range(CONV_W):
                term = dcpad_ref[r0 + 31 - j:r0 + 31 - j + rows, :] * dww[j:j + 1, :]
                acc = term if acc is None else acc + term
                wj = jnp.sum(dcr * upad_ref[r0 + 1 + j:r0 + 1 + j + rows, :], axis=0, keepdims=True)
                ddww = ddww + jnp.where(r8 == j, wj, 0.0)
            pieces.append(acc)
        du = jnp.concatenate(pieces, axis=0)
        ddww_ref[...] += ddww
        dab_ref[:, 0:D_CONV] = (du * sb).astype(BF16)
        dab_ref[:, D_CONV:2 * D_CONV] = (du * a * sb * (1.0 - sb)).astype(BF16)

    vec = pl.BlockSpec((1, D_CONV), lambda b: (0, 0))
    mat = pl.BlockSpec((D_CONV, D_CONV), lambda b: (0, 0))
    w32 = pl.BlockSpec((32, D_CONV), lambda b: (0, 0))
    vshape = jax.ShapeDtypeStruct((1, D_CONV), F32)
    return pl.pallas_call(
        body, name="conv_bwd", grid=(nb,),
        in_specs=[pl.BlockSpec((s, D_CONV), lambda b: (b, COL_CA)),
                  pl.BlockSpec((s, D_CONV), lambda b: (b, COL_CB)),
                  pl.BlockSpec((s, D_CONV), lambda b: (b, 0)),
                  w32, vec, vec, vec, mat],
        out_specs=[pl.BlockSpec((s, 2 * D_CONV), lambda b: (b, 0)), w32, vec, vec, vec, mat, vec],
        out_shape=[jax.ShapeDtypeStruct((t, 2 * D_CONV), BF16),
                   jax.ShapeDtypeStruct((32, D_CONV), F32), vshape, vshape, vshape,
                   jax.ShapeDtypeStruct((D_CONV, D_CONV), F32), vshape],
        scratch_shapes=[pltpu.VMEM((s + 32, D_CONV), F32), pltpu.VMEM((s + 32, D_CONV), F32)],
        compiler_params=_cparams(("arbitrary",)),
    )(proj, proj, dy, dww, dwb, lnw, lnb, pww)


def _mix_out(o_attn, y_hgrn, y_conv, x, aw, cw, w_out, tm):
    t = x.shape[0]

    def body(o_ref, h_ref, c_ref, x_ref, aw_ref, cw_ref, w_ref, mixed_ref, x1_ref):
        o = o_ref[...]
        a = o * lax.rsqrt(jnp.mean(o * o, axis=-1, keepdims=True) + EPS) * aw_ref[...]
        yc = c_ref[...]
        c = yc * lax.rsqrt(jnp.mean(yc * yc, axis=-1, keepdims=True) + EPS) * cw_ref[...]
        ab, hb, cb = a.astype(BF16), h_ref[...].astype(BF16), c.astype(BF16)
        mixed_ref[:, 0:512] = ab
        mixed_ref[:, 512:768] = hb
        mixed_ref[:, 768:1024] = cb
        x1_ref[...] = (x_ref[...] + _dot(ab, w_ref[0:512, :]) + _dot(hb, w_ref[512:768, :])
                       + _dot(cb, w_ref[768:1024, :]))

    def tok(w):
        return pl.BlockSpec((tm, w), lambda i: (i, 0))

    return pl.pallas_call(
        body, name="mix_out", grid=(t // tm,),
        in_specs=[tok(512), tok(256), tok(256), tok(D_MODEL),
                  pl.BlockSpec((1, 512), lambda i: (0, 0)), pl.BlockSpec((1, 256), lambda i: (0, 0)),
                  pl.BlockSpec((D_MODEL, D_MODEL), lambda i: (0, 0))],
        out_specs=[tok(D_MODEL), tok(D_MODEL)],
        out_shape=[jax.ShapeDtypeStruct((t, D_MODEL), BF16), jax.ShapeDtypeStruct((t, D_MODEL), F32)],
        compiler_params=_cparams(("parallel",)),
    )(o_attn, y_hgrn, y_conv, x, aw, cw, w_out)


def _mix_out_bwd(dx1, w_out, o_attn, y_conv, aw, cw, tm):
    t = dx1.shape[0]

    def body(dx_ref, w_ref, o_ref, c_ref, aw_ref, cw_ref, do_ref, dh_ref, dc_ref, daw_ref, dcw_ref):
        dm = _dot_nt(dx_ref[...].astype(BF16), w_ref[...])
        do, daw = _rms_bwd(dm[:, 0:512], o_ref[...], aw_ref[...])
        dc, dcw = _rms_bwd(dm[:, 768:1024], c_ref[...], cw_ref[...])
        do_ref[...] = do
        dh_ref[...] = dm[:, 512:768]
        dc_ref[...] = dc

        @pl.when(pl.program_id(0) == 0)
        def _():
            daw_ref[...] = jnp.zeros_like(daw_ref)
            dcw_ref[...] = jnp.zeros_like(dcw_ref)

        daw_ref[...] += jnp.sum(daw, axis=0, keepdims=True)
        dcw_ref[...] += jnp.sum(dcw, axis=0, keepdims=True)

    def tok(w):
        return pl.BlockSpec((tm, w), lambda i: (i, 0))

    v512 = pl.BlockSpec((1, 512), lambda i: (0, 0))
    v256 = pl.BlockSpec((1, 256), lambda i: (0, 0))
    return pl.pallas_call(
        body, name="mix_out_bwd", grid=(t // tm,),
        in_specs=[tok(D_MODEL), pl.BlockSpec((D_MODEL, D_MODEL), lambda i: (0, 0)), tok(512), tok(256),
                  v512, v256],
        out_specs=[tok(512), tok(256), tok(256), v512, v256],
        out_shape=[jax.ShapeDtypeStruct((t, 512), F32), jax.ShapeDtypeStruct((t, 256), F32),
                   jax.ShapeDtypeStruct((t, 256), F32), jax.ShapeDtypeStruct((1, 512), F32),
                   jax.ShapeDtypeStruct((1, 256), F32)],
        compiler_params=_cparams(("arbitrary",)),
    )(dx1, w_out, o_attn, y_conv, aw, cw)


FF_BLOCKS = 4


def _ffn_fwd(x1, fw, wg, wu, wd, tm, ride=None):
    t = x1.shape[0]
    fb = wg.shape[1]
    nf = N_DEV // FF_BLOCKS
    rd = _ride_plan(ride)
    grid = (t // tm, nf)

    def body(*refs):
        x_ref, fw_ref, wg_ref, wu_ref, wd_ref = refs[:5]
        h_ref, g_ref, u_ref, a_ref, x2_ref = refs[5 + rd.n:10 + rd.n]
        acc_ref = refs[10 + 2 * rd.n]
        copies = rd.copies(refs[5:5 + rd.n], refs[10 + rd.n:10 + 2 * rd.n], refs[11 + 2 * rd.n:])
        step_id = _grid_step_id(grid)
        _ride_start(copies, step_id == 0)
        j = pl.program_id(1)

        @pl.when(j == 0)
        def _():
            xv = x_ref[...]
            r = lax.rsqrt(jnp.mean(xv * xv, axis=-1, keepdims=True) + EPS)
            h_ref[...] = (xv * r * fw_ref[...]).astype(BF16)
            acc_ref[...] = xv

        h = h_ref[...]
        out = None
        for c in range(FF_BLOCKS):
            g = _dot_nt(h, wg_ref[c])
            u = _dot_nt(h, wu_ref[c])
            a = (g * _sigmoid(g) * u).astype(BF16)
            g_ref[c] = g.astype(BF16)
            u_ref[c] = u.astype(BF16)
            a_ref[c] = a
            part = _dot(a, wd_ref[c])
            out = part if out is None else out + part
        acc_ref[...] += out

        @pl.when(j == nf - 1)
        def _():
            x2_ref[...] = acc_ref[...]

        _ride_wait(copies, step_id == (t // tm) * nf - 1)

    tok = pl.BlockSpec((tm, D_MODEL), lambda i, j: (i, 0))
    ffb = pl.BlockSpec((FF_BLOCKS, tm, fb), lambda i, j: (j, i, 0))
    ffs = jax.ShapeDtypeStruct((N_DEV, t, fb), BF16)
    return pl.pallas_call(
        body, name="ffn_fwd", grid=grid,
        in_specs=[tok, pl.BlockSpec((1, D_MODEL), lambda i, j: (0, 0)),
                  pl.BlockSpec((FF_BLOCKS, fb, D_MODEL), lambda i, j: (j, 0, 0)),
                  pl.BlockSpec((FF_BLOCKS, fb, D_MODEL), lambda i, j: (j, 0, 0)),
                  pl.BlockSpec((FF_BLOCKS, fb, D_MODEL), lambda i, j: (j, 0, 0))] + rd.in_specs,
        out_specs=[tok, ffb, ffb, ffb, tok] + rd.out_specs,
        out_shape=[jax.ShapeDtypeStruct((t, D_MODEL), BF16), ffs, ffs, ffs,
                   jax.ShapeDtypeStruct((t, D_MODEL), F32)] + rd.out_shape,
        scratch_shapes=[pltpu.VMEM((tm, D_MODEL), F32)] + rd.scratch,
        compiler_params=_cparams(("arbitrary", "arbitrary")),
    )(x1, fw, wg, wu, wd, *rd.srcs)


def _ffn_bwd(dx2, g, u, wg, wu, wd, x1, fw, tm, ride=None):
    t = dx2.shape[0]
    fb = wg.shape[1]
    nf = N_DEV // FF_BLOCKS
    rd = _ride_plan(ride)
    grid = (t // tm, nf)

    def body(*refs):
        dx_ref, g_ref, u_ref, wg_ref, wu_ref, wd_ref, x_ref, fw_ref = refs[:8]
        dg_ref, du_ref, dx1_ref, dfw_ref = refs[8 + rd.n:12 + rd.n]
        acc_ref = refs[12 + 2 * rd.n]
        copies = rd.copies(refs[8:8 + rd.n], refs[12 + rd.n:12 + 2 * rd.n], refs[13 + 2 * rd.n:])
        step_id = _grid_step_id(grid)
        _ride_start(copies, step_id == 0)
        i = pl.program_id(0)
        j = pl.program_id(1)
        dxb = dx_ref[...].astype(BF16)
        dh = None
        for c in range(FF_BLOCKS):
            da = _dot_nt(dxb, wd_ref[c])
            gv = g_ref[c].astype(F32)
            uv = u_ref[c].astype(F32)
            sg = _sigmoid(gv)
            dg = (da * uv * (sg * (1.0 + gv * (1.0 - sg)))).astype(BF16)
            du = (da * gv * sg).astype(BF16)
            dg_ref[c] = dg
            du_ref[c] = du
            part = _dot(dg, wg_ref[c]) + _dot(du, wu_ref[c])
            dh = part if dh is None else dh + part

        @pl.when(j == 0)
        def _():
            acc_ref[...] = dh

        @pl.when(j > 0)
        def _():
            acc_ref[...] += dh

        @pl.when((i == 0) & (j == 0))
        def _():
            dfw_ref[...] = jnp.zeros_like(dfw_ref)

        @pl.when(j == nf - 1)
        def _():
            dx, dfw = _rms_bwd(acc_ref[...], x_ref[...], fw_ref[...])
            dx1_ref[...] = dx_ref[...] + dx
            dfw_ref[...] += jnp.sum(dfw, axis=0, keepdims=True)

        _ride_wait(copies, step_id == (t // tm) * nf - 1)

    tok = pl.BlockSpec((tm, D_MODEL), lambda i, j: (i, 0))
    ffb = pl.BlockSpec((FF_BLOCKS, tm, fb), lambda i, j: (j, i, 0))
    ffs = jax.ShapeDtypeStruct((N_DEV, t, fb), BF16)
    vec = pl.BlockSpec((1, D_MODEL), lambda i, j: (0, 0))
    return pl.pallas_call(
        body, name="ffn_bwd", grid=grid,
        in_specs=[tok, ffb, ffb,
                  pl.BlockSpec((FF_BLOCKS, fb, D_MODEL), lambda i, j: (j, 0, 0)),
                  pl.BlockSpec((FF_BLOCKS, fb, D_MODEL), lambda i, j: (j, 0, 0)),
                  pl.BlockSpec((FF_BLOCKS, fb, D_MODEL), lambda i, j: (j, 0, 0)),
                  tok, vec] + rd.in_specs,
        out_specs=[ffb, ffb, tok, vec] + rd.out_specs,
        out_shape=[ffs, ffs, jax.ShapeDtypeStruct((t, D_MODEL), F32),
                   jax.ShapeDtypeStruct((1, D_MODEL), F32)] + rd.out_shape,
        scratch_shapes=[pltpu.VMEM((tm, D_MODEL), F32)] + rd.scratch,
        compiler_params=_cparams(("arbitrary", "arbitrary")),
    )(dx2, g, u, wg, wu, wd, x1, fw, *rd.srcs)


def _loss_grad(y, target, tm):
    t, d = y.shape

    def body(y_ref, t_ref, dy_ref, loss_ref):
        err = y_ref[...] - t_ref[...]
        dy_ref[...] = err * (1.0 / d)

        @pl.when(pl.program_id(0) == 0)
        def _():
            loss_ref[...] = jnp.zeros_like(loss_ref)

        part = jnp.sum(jnp.sum(err * err, axis=-1, keepdims=True), axis=0, keepdims=True)
        loss_ref[...] += part * (0.5 / d)

    tok = pl.BlockSpec((tm, d), lambda i: (i, 0))
    return pl.pallas_call(
        body, name="loss_grad", grid=(t // tm,),
        in_specs=[tok, tok],
        out_specs=[tok, pl.BlockSpec((1, 1), lambda i: (0, 0))],
        out_shape=[jax.ShapeDtypeStruct((t, d), F32), jax.ShapeDtypeStruct((1, 1), F32)],
        compiler_params=_cparams(("arbitrary",)),
    )(y, target)


def _tile(v, reps):
    return jnp.tile(v.reshape(1, -1), (1, reps))


class _LocalPlan:
    def __init__(self, wb):
        self.w = [{n: wb[n][l] for n in BIG_AXIS} for l in range(DEPTH)]

    def ride(self, kernel_name, l, grads=None):
        return None

    def done(self, kernel_name, l, outs):
        pass


def _local_step(x, target, p, plan):
    nb, s, d = x.shape
    t = nb * s
    tm = min(512, s)
    tq = min(1024, s)
    xf = x.reshape(t, d)
    cosq, sinq = _rope_tables(s)
    ones512 = _block_ones(512, HEAD_DIM)
    lbs = _lower_bounds(p["hgrn_lb_logits"].reshape(DEPTH, 2 * D_HGRN)).reshape(DEPTH, 2, D_HGRN)

    saved = []
    cur = xf
    wb = plan.w
    for l in range(DEPTH):
        qw = _tile(p["q_norm_w"][l], N_HEADS)
        kw = _tile(p["k_norm_w"][l], N_KV)
        gw = _tile(p["hgrn_gnorm_w"][l], 2)
        dww = jnp.pad(p["conv_dw_w"][l], ((0, 1), (0, 0)))
        pww = p["conv_pw_w"][l].astype(BF16)
        h0, proj = _rms_proj(cur, _row(p["mix_norm_w"][l]), wb[l]["w_in"], tm)
        qr, kd, vd, kdt, vdt = _qkv_prep(proj, cosq, sinq, qw, kw, ones512, s, tm)
        o_attn, lse, *rode = _attn_fwd(qr, kd, vdt, nb, s, tq, plan.ride("attn_fwd", l))
        plan.done("attn_fwd", l, rode)
        y_hgrn, osum, states, *rode = _hgrn_fwd(proj, lbs[l], gw, nb, s, plan.ride("hgrn_fwd", l))
        plan.done("hgrn_fwd", l, rode)
        y_conv = _conv_fwd(proj, dww, _row(p["conv_dw_b"][l]), _row(p["conv_ln_w"][l]),
                           _row(p["conv_ln_b"][l]), pww, _row(p["conv_pw_b"][l]), nb, s)
        mixed, x1 = _mix_out(o_attn, y_hgrn, y_conv, cur, _row(p["attn_out_norm_w"][l]),
                             _row(p["conv_out_norm_w"][l]), wb[l]["w_out"], tm)
        hf, g, u, a, x2, *rode = _ffn_fwd(x1, _row(p["ffn_norm_w"][l]), wb[l]["w_gate"], wb[l]["w_up"],
                                          wb[l]["w_down"], tm, plan.ride("ffn_fwd", l))
        plan.done("ffn_fwd", l, rode)
        saved.append(dict(x=cur, h0=h0, proj=proj, qr=qr, kd=kd, vd=vd, kdt=kdt, o_attn=o_attn, lse=lse,
                          osum=osum, states=states, y_conv=y_conv, mixed=mixed, x1=x1, hf=hf, g=g, u=u, a=a,
                          qw=qw, kw=kw, gw=gw, dww=dww, pww=pww))
        cur = x2

    dcur, loss = _loss_grad(cur, target.reshape(t, d), tm)

    grads = {k: [None] * DEPTH for k in WEIGHTS}
    dlb = [None] * DEPTH
    for l in reversed(range(DEPTH)):
        sv = saved[l]
        dg, du, dx1, dfw, *rode = _ffn_bwd(dcur, sv["g"], sv["u"], wb[l]["w_gate"], wb[l]["w_up"],
                                           wb[l]["w_down"], sv["x1"], _row(p["ffn_norm_w"][l]), tm,
                                           plan.ride("ffn_bwd", l, grads))
        plan.done("ffn_bwd", l, rode)
        grads["ffn_norm_w"][l] = dfw[0]
        grads["w_gate"][l] = _dw_ff(dg, sv["hf"], "dw_gate", tm)
        grads["w_up"][l] = _dw_ff(du, sv["hf"], "dw_up", tm)
        grads["w_down"][l] = _dw_ff(sv["a"], dcur, "dw_down", tm)
        do_attn, dy_hgrn, dy_conv, daw, dcw = _mix_out_bwd(
            dx1, wb[l]["w_out"], sv["o_attn"], sv["y_conv"], _row(p["attn_out_norm_w"][l]),
            _row(p["conv_out_norm_w"][l]), tm)
        grads["attn_out_norm_w"][l] = daw[0]
        grads["conv_out_norm_w"][l] = dcw[0]
        grads["w_out"][l] = _mm_tn(sv["mixed"], dx1, D_MODEL, "dw_out", tm)
        dq, dkd, dvd, *rode = _attn_bwd(sv["qr"], sv["kd"], sv["vd"], sv["kdt"], sv["o_attn"], sv["lse"], do_attn,
                                        nb, s, tq, plan.ride("attn_bwd", l, grads))
        plan.done("attn_bwd", l, rode)
        dqkv, dqw, dkw = _qkv_bwd(sv["proj"], dq, dkd, dvd, cosq, sinq, sv["qw"], sv["kw"], ones512, s, tm)
        grads["q_norm_w"][l] = dqw.reshape(N_HEADS, HEAD_DIM).sum(0)
        grads["k_norm_w"][l] = dkw.reshape(N_KV, HEAD_DIM).sum(0)
        dhq, dzf, dzb, dhi, dhg, dgw, dlb_l, *rode = _hgrn_bwd(sv["proj"], lbs[l], sv["gw"], sv["osum"],
                                                               sv["states"], dy_hgrn, nb, s,
                                                               plan.ride("hgrn_bwd", l, grads))
        plan.done("hgrn_bwd", l, rode)
        grads["hgrn_gnorm_w"][l] = dgw.reshape(nb * D_HGRN // HEAD_DIM, HEAD_DIM).sum(0)
        dlb[l] = dlb_l.sum(0)
        dab, ddww, ddwb, dlnw, dlnb, dpww, dpwb = _conv_bwd(
            sv["proj"], dy_conv, sv["dww"], _row(p["conv_dw_b"][l]), _row(p["conv_ln_w"][l]),
            _row(p["conv_ln_b"][l]), sv["pww"], nb, s)
        grads["conv_dw_w"][l] = ddww[:CONV_W]
        grads["conv_dw_b"][l] = ddwb[0]
        grads["conv_ln_w"][l] = dlnw[0]
        grads["conv_ln_b"][l] = dlnb[0]
        grads["conv_pw_w"][l] = dpww
        grads["conv_pw_b"][l] = dpwb[0]
        pieces = [dqkv, dhq, dzf, dzb, dhi, dhg, dab]
        grads["w_in"][l] = _dw_in(sv["h0"], pieces, tm)
        dcur, dnw, *rode = _proj_bwd(pieces, wb[l]["w_in"], sv["x"], _row(p["mix_norm_w"][l]), dx1, tm,
                                     plan.ride("proj_bwd", l, grads))
        plan.done("proj_bwd", l, rode)
        grads["mix_norm_w"][l] = dnw[0]

    dlog = _lower_bounds_bwd(p["hgrn_lb_logits"].reshape(DEPTH, 2 * D_HGRN),
                             jnp.stack(dlb).reshape(DEPTH, 2 * D_HGRN))
    out = {k: (v if k in BIG_AXIS else jnp.stack(v)) for k, v in grads.items() if k != "hgrn_lb_logits"}
    out["hgrn_lb_logits"] = dlog.reshape(DEPTH, 2, D_HGRN)
    return loss, dcur.reshape(nb, s, d), out


BIG_AXIS = {"w_in": 2, "w_out": 1, "w_gate": 2, "w_up": 2, "w_down": 1}
SMALL_SHARD_AXIS = {"hgrn_lb_logits": 2, "conv_dw_w": 2, "conv_pw_w": 1}
WEIGHTS = ("mix_norm_w", "w_in", "q_norm_w", "k_norm_w", "hgrn_lb_logits", "hgrn_gnorm_w", "conv_dw_w",
           "conv_dw_b", "conv_ln_w", "conv_ln_b", "conv_pw_w", "conv_pw_b", "attn_out_norm_w",
           "conv_out_norm_w", "w_out", "ffn_norm_w", "w_gate", "w_up", "w_down")
SMALL = tuple(n for n in WEIGHTS if n not in BIG_AXIS)


def _my_index():
    return 4 * lax.axis_index("x") + 2 * lax.axis_index("y") + lax.axis_index("c")


class _RidePlan:
    def __init__(self, srcs, gather):
        self.srcs = list(srcs)
        self.n = len(self.srcs)
        self.gather = list(gather) if isinstance(gather, (list, tuple)) else [gather] * self.n
        any_spec = pl.BlockSpec(memory_space=pl.ANY)
        self.in_specs = [any_spec] * self.n
        self.out_specs = [any_spec] * self.n
        self.out_shape = [jax.ShapeDtypeStruct(((N_DEV,) + s.shape) if g else s.shape, s.dtype)
                          for s, g in zip(self.srcs, self.gather)]
        npeer = N_DEV - 1
        self.scratch = [pltpu.SemaphoreType.DMA((self.n * npeer,)), pltpu.SemaphoreType.DMA((self.n * npeer,)),
                        pltpu.SemaphoreType.DMA((self.n,))] if self.n else []

    def copies(self, src_refs, out_refs, sems):
        if not self.n:
            return [], [], []
        send_sems, recv_sems, local_sems = sems
        npeer = N_DEV - 1
        x, y, c = lax.axis_index("x"), lax.axis_index("y"), lax.axis_index("c")
        me = 4 * x + 2 * y + c
        locals_, sends, recvs = [], [], []
        for a in range(self.n):
            src_ref, out_ref = src_refs[a], out_refs[a]

            def rows_for(j, src_ref=src_ref, gather=self.gather[a]):
                return src_ref if gather else src_ref.at[j]

            locals_.append(pltpu.make_async_copy(rows_for(me), out_ref.at[me], local_sems.at[a]))
            for k in range(1, N_DEV):
                px = (1 - x) if (k & 4) else x
                py = (1 - y) if (k & 2) else y
                pc = (1 - c) if (k & 1) else c
                pidx = 4 * px + 2 * py + pc
                common = dict(send_sem=send_sems.at[a * npeer + k - 1], recv_sem=recv_sems.at[a * npeer + k - 1],
                              device_id=(px, py, pc), device_id_type=pl.DeviceIdType.MESH)
                sends.append(pltpu.make_async_remote_copy(src_ref=rows_for(pidx), dst_ref=out_ref.at[me], **common))
                recvs.append(pltpu.make_async_remote_copy(src_ref=rows_for(pidx), dst_ref=out_ref.at[pidx],
                                                          **common))
        return locals_, sends, recvs


def _ride_plan(ride):
    return _RidePlan(*ride) if ride else _RidePlan([], True)


def _ride_start(copies, when=None):
    locals_, sends, _ = copies

    def go():
        for cp in locals_ + sends:
            cp.start()

    if locals_:
        go() if when is None else pl.when(when)(go)


def _ride_wait(copies, when=None):
    locals_, sends, recvs = copies

    def go():
        for cp in recvs:
            cp.wait_recv()
        for cp in sends:
            cp.wait_send()
        for cp in locals_:
            cp.wait()

    if locals_:
        go() if when is None else pl.when(when)(go)


def _exchange(srcs, gather, name):
    rd = _RidePlan(srcs, gather)

    def body(*refs):
        copies = rd.copies(refs[:rd.n], refs[rd.n:2 * rd.n], refs[2 * rd.n:])
        _ride_start(copies)
        _ride_wait(copies)

    return pl.pallas_call(body, name=name, in_specs=rd.in_specs, out_specs=rd.out_specs,
                          out_shape=rd.out_shape, scratch_shapes=rd.scratch)(*srcs)


def _adamw_math(w, g, m, v):
    m = ADAM_B1 * m + (1.0 - ADAM_B1) * g
    v = ADAM_B2 * v + (1.0 - ADAM_B2) * (g * g)
    m_hat = m / (1.0 - ADAM_B1 ** ADAM_STEP)
    v_hat = v / (1.0 - ADAM_B2 ** ADAM_STEP)
    delta = -ADAM_LR * (m_hat / (jnp.sqrt(v_hat) + ADAM_EPS) + ADAM_WD * w)
    return delta, m, v


def _sum_adamw(parts, w, m, v, name):
    _, k, n = w.shape
    tk = k
    for cand in (256, 176, 160, 128):
        if k % cand == 0:
            tk = cand
            break

    def body(*refs):
        p_refs = refs[:DEPTH]
        w_ref, m_ref, v_ref, g_ref, d_ref, mo_ref, vo_ref = refs[DEPTH:]
        for l in range(DEPTH):
            @pl.when(pl.program_id(0) == l)
            def _(p_ref=p_refs[l]):
                g = p_ref[0].astype(F32)
                for i in range(1, N_DEV):
                    g = g + p_ref[i].astype(F32)
                g_ref[...] = g
                d_ref[...], mo_ref[...], vo_ref[...] = _adamw_math(w_ref[...], g, m_ref[...], v_ref[...])

    row = pl.BlockSpec((None, tk, n), lambda l, i: (l, i, 0))
    shp = jax.ShapeDtypeStruct(w.shape, F32)
    return pl.pallas_call(
        body, name=name, grid=(DEPTH, k // tk),
        in_specs=[pl.BlockSpec((N_DEV, tk, n), lambda l, i: (0, i, 0))] * DEPTH + [row, row, row],
        out_specs=[row, row, row, row],
        out_shape=[shp, shp, shp, shp],
        compiler_params=_cparams(("parallel", "parallel")),
    )(*parts, w, m, v)


def _sum8(parts, name):
    r = parts.shape[1]

    def body(p_ref, g_ref):
        g = p_ref[0]
        for i in range(1, N_DEV):
            g = g + p_ref[i]
        g_ref[...] = g

    return pl.pallas_call(body, name=name, out_shape=jax.ShapeDtypeStruct((r, LANES), F32))(parts)


def _adamw(w, g, m, v):
    def body(w_ref, g_ref, m_ref, v_ref, d_ref, mo_ref, vo_ref):
        d_ref[...], mo_ref[...], vo_ref[...] = _adamw_math(w_ref[...], g_ref[...], m_ref[...], v_ref[...])

    shp = jax.ShapeDtypeStruct(w.shape, F32)
    return pl.pallas_call(body, name="adamw_small", out_shape=[shp, shp, shp])(w, g, m, v)


def _pack(arrays, dtype, row_multiple):
    flat = jnp.concatenate([a.reshape(-1).astype(dtype) for a in arrays])
    n = flat.shape[0]
    unit = row_multiple * LANES
    total = -(-n // unit) * unit
    return jnp.pad(flat, (0, total - n)).reshape(total // LANES, LANES)


def _unpack(flat2d, shapes, lead=()):
    flat = flat2d.reshape(lead + (-1,))
    out, off = [], 0
    for shp in shapes:
        n = int(np.prod(shp))
        out.append(flat[..., off:off + n].reshape(lead + tuple(shp)))
        off += n
    return out


def _shard_to_rows(full, axis):
    shp = full.shape
    k = shp[axis] // N_DEV
    r = full.reshape(shp[:axis] + (N_DEV, k) + shp[axis + 1:])
    return jnp.moveaxis(r, axis, 0)


def _rows_to_full(rows, axis):
    r = jnp.moveaxis(rows, 0, axis)
    shp = r.shape
    return r.reshape(shp[:axis] + (shp[axis] * shp[axis + 1],) + shp[axis + 2:])


def kernel(x, mix_norm_w, w_in, q_norm_w, k_norm_w, hgrn_lb_logits, hgrn_gnorm_w, conv_dw_w, conv_dw_b, conv_ln_w, conv_ln_b, conv_pw_w, conv_pw_b, attn_out_norm_w, conv_out_norm_w, w_out, ffn_norm_w, w_gate, w_up, w_down, loss_target, m_mix_norm_w, m_w_in, m_q_norm_w, m_k_norm_w, m_hgrn_lb_logits, m_hgrn_gnorm_w, m_conv_dw_w, m_conv_dw_b, m_conv_ln_w, m_conv_ln_b, m_conv_pw_w, m_conv_pw_b, m_attn_out_norm_w, m_conv_out_norm_w, m_w_out, m_ffn_norm_w, m_w_gate, m_w_up, m_w_down, v_mix_norm_w, v_w_in, v_q_norm_w, v_k_norm_w, v_hgrn_lb_logits, v_hgrn_gnorm_w, v_conv_dw_w, v_conv_dw_b, v_conv_ln_w, v_conv_ln_b, v_conv_pw_w, v_conv_pw_b, v_attn_out_norm_w, v_conv_out_norm_w, v_w_out, v_ffn_norm_w, v_w_gate, v_w_up, v_w_down):
    w_loc = dict(zip(WEIGHTS, (mix_norm_w, w_in, q_norm_w, k_norm_w, hgrn_lb_logits, hgrn_gnorm_w, conv_dw_w,
                               conv_dw_b, conv_ln_w, conv_ln_b, conv_pw_w, conv_pw_b, attn_out_norm_w,
                               conv_out_norm_w, w_out, ffn_norm_w, w_gate, w_up, w_down)))
    m_loc = dict(zip(WEIGHTS, (m_mix_norm_w, m_w_in, m_q_norm_w, m_k_norm_w, m_hgrn_lb_logits, m_hgrn_gnorm_w,
                               m_conv_dw_w, m_conv_dw_b, m_conv_ln_w, m_conv_ln_b, m_conv_pw_w, m_conv_pw_b,
                               m_attn_out_norm_w, m_conv_out_norm_w, m_w_out, m_ffn_norm_w, m_w_gate, m_w_up,
                               m_w_down)))
    v_loc = dict(zip(WEIGHTS, (v_mix_norm_w, v_w_in, v_q_norm_w, v_k_norm_w, v_hgrn_lb_logits, v_hgrn_gnorm_w,
                               v_conv_dw_w, v_conv_dw_b, v_conv_ln_w, v_conv_ln_b, v_conv_pw_w, v_conv_pw_b,
                               v_attn_out_norm_w, v_conv_out_norm_w, v_w_out, v_ffn_norm_w, v_w_gate, v_w_up,
                               v_w_down)))
    me = _my_index()
    big = tuple(BIG_AXIS)
    sms = tuple(SMALL_SHARD_AXIS)

    sm_shapes = [w_loc[n].shape for n in sms]
    got_s = _exchange([_pack([w_loc[n] for n in sms], F32, 8)], True, "gather_small_params")[0]
    p_full = {n: w_loc[n] for n in SMALL if n not in SMALL_SHARD_AXIS}
    for n, a in zip(sms, _unpack(got_s, sm_shapes, (N_DEV,))):
        p_full[n] = _rows_to_full(a, SMALL_SHARD_AXIS[n])

    col_sharded = tuple(n for n in big if BIG_AXIS[n] == 2)

    def shard_t(n, a):
        return jnp.swapaxes(a, 1, 2) if n in col_sharded else a

    def natural(n, gathered):
        if n in ("w_in", "w_out"):
            return gathered.reshape(-1, gathered.shape[-1])
        return gathered

    def to_send(n, gl):
        if n in ("w_in", "w_out"):
            return gl.reshape(N_DEV, gl.shape[0] // N_DEV, gl.shape[1]).astype(BF16)
        return gl

    class StepPlan:
        def __init__(self):
            self.w = [dict() for _ in range(DEPTH)]
            self.parts = [dict() for _ in range(DEPTH)]
            self.pending = {}
            got = _exchange([w_send["w_in"][0]], True, "gather_w_in")
            self.w[0]["w_in"] = natural("w_in", got[0])

        def ride(self, kernel_name, l, grads=None):
            want = []
            if kernel_name == "attn_fwd":
                want = [("w_out", l), ("w_gate", l)]
            elif kernel_name == "hgrn_fwd":
                want = [("w_up", l), ("w_down", l)]
            elif kernel_name == "ffn_fwd" and l + 1 < DEPTH:
                want = [("w_in", l + 1)]
            elif kernel_name == "ffn_bwd" and l + 1 < DEPTH:
                want = [("w_gate", l + 1), ("w_up", l + 1)]
            elif kernel_name == "attn_bwd" and l + 1 < DEPTH:
                want = [("w_in", l + 1), ("w_out", l + 1), ("w_down", l + 1)]
                if l == 0:
                    want += [("w_out", 0)]
            elif kernel_name == "hgrn_bwd" and l == 0:
                want = [(n, 0) for n in ("w_gate", "w_up", "w_down")]
            elif kernel_name == "proj_bwd" and l == 0:
                want = [("w_in", 0)]
            if not want:
                return None
            self.pending[(kernel_name, l)] = want
            if grads is None:
                return [w_send[n][wl] for n, wl in want], True
            return [to_send(n, grads[n][wl]) for n, wl in want], False

        def done(self, kernel_name, l, outs):
            want = self.pending.pop((kernel_name, l), [])
            for (n, wl), out in zip(want, outs):
                if kernel_name.endswith("_fwd"):
                    self.w[wl][n] = natural(n, out)
                else:
                    self.parts[wl][n] = out

    w_send = {n: shard_t(n, w_loc[n]).astype(BF16) for n in big}
    plan = StepPlan()
    loss_part, grad_x, g = _local_step(x, loss_target, p_full, plan)
    loss = lax.psum(loss_part[0, 0], MESH_AXES)

    pw = g["conv_pw_w"]
    k_pw = w_loc["conv_pw_w"].shape[1]
    pw_send = jnp.moveaxis(pw.reshape(DEPTH, N_DEV, k_pw, pw.shape[-1]), 1, 0).reshape(N_DEV, -1, LANES)
    gathered_small = [n for n in SMALL if n != "conv_pw_w"]
    small_shapes = [g[n].shape for n in gathered_small]
    small_parts, pw_parts = _exchange([_pack([g[n] for n in gathered_small], F32, 8), pw_send],
                                      [True, False], "exchange_small_grads")
    big_out = {}
    for n in big:
        res = _sum_adamw([plan.parts[l][n] for l in range(DEPTH)], shard_t(n, w_loc[n]), shard_t(n, m_loc[n]),
                         shard_t(n, v_loc[n]), "sum_adamw_" + n)
        big_out[n] = [shard_t(n, r) for r in res]

    g_small = dict(zip(gathered_small, _unpack(_sum8(small_parts, "sum_small_grads"), small_shapes)))
    g_small["conv_pw_w"] = _sum8(pw_parts, "sum_conv_pw_grads").reshape(w_loc["conv_pw_w"].shape)
    for n in sms:
        if n == "conv_pw_w":
            continue
        ax = SMALL_SHARD_AXIS[n]
        k = w_loc[n].shape[ax]
        g_small[n] = lax.dynamic_slice_in_dim(g_small[n], me * k, k, axis=ax)
    loc_shapes = [w_loc[n].shape for n in SMALL]
    packed = [_pack([d[n] for n in SMALL], F32, 8) for d in (w_loc, g_small, m_loc, v_loc)]
    res = _adamw(*packed)
    small_out = [g_small] + [dict(zip(SMALL, _unpack(r, loc_shapes))) for r in res]

    def pick(i, n):
        return big_out[n][i] if n in BIG_AXIS else small_out[i][n]

    return (loss, grad_x) + tuple(pick(i, n) for i in range(4) for n in WEIGHTS)
```

```python
import functools

import jax
import jax.numpy as jnp
import numpy as np
from jax import lax
from jax.experimental import pallas as pl
from jax.experimental.pallas import tpu as pltpu

F32 = jnp.float32
BF16 = jnp.bfloat16

D_MODEL = 1024
D_ATTN = 512
D_HGRN = 256
D_CONV = 256
HEAD_DIM = 64
N_HEADS = 8
N_KV = 2
GRID_W = 64
ROPE_THETA = 10000.0
CHUNK = 64
F_MIN = 1e-6
CONV_W = 31
CONV_PAD = 15
D_FF = 2816
D_PROJ = 2560
EPS = 1e-6
LN_EPS = 1e-5
DEPTH = 2
ADAM_LR = 0.001
ADAM_B1 = 0.9
ADAM_B2 = 0.999
ADAM_EPS = 1e-08
ADAM_WD = 0.01
ADAM_STEP = 10
N_DEV = 8
MESH_AXES = ("x", "y", "c")

COL_HQ, COL_ZFW, COL_ZBW, COL_HI, COL_HG = 6, 8, 10, 12, 14
COL_CA, COL_CB = 8, 9

LANES = 128
VMEM_LIMIT_MB = 56


def _cparams(dims=None):
    return pltpu.CompilerParams(dimension_semantics=dims, vmem_limit_bytes=VMEM_LIMIT_MB * 2 ** 20)


def _dot(a, b):
    return jnp.dot(a, b, preferred_element_type=F32)


def _dot_nt(a, b):
    return lax.dot_general(a, b, (((1,), (1,)), ((), ())), preferred_element_type=F32)


def _dot_tn(a, b):
    return lax.dot_general(a, b, (((0,), (0,)), ((), ())), preferred_element_type=F32)


def _split_bf16(x, parts):
    out = []
    r = x
    for _ in range(parts):
        p = r.astype(BF16)
        out.append(p)
        r = r - p.astype(F32)
    return out


def _dot_precise(x, m_bf16, parts=3):
    acc = None
    for p in _split_bf16(x, parts):
        t = _dot(p, m_bf16)
        acc = t if acc is None else acc + t
    return acc


def _block_ones(width, group):
    i = np.arange(width)
    return jnp.asarray((i[:, None] // group) == (i[None, :] // group), dtype=BF16)


def _sigmoid(x):
    return 1.0 / (1.0 + jnp.exp(-x))


def _rot(x):
    w = x.shape[1]
    lane = lax.broadcasted_iota(jnp.int32, x.shape, 1)
    first = (lane % 32) < 16
    return jnp.where(first, -pltpu.roll(x, w - 16, 1), pltpu.roll(x, 16, 1))


def _rope(x, cos, sin):
    return x * cos + _rot(x) * sin


def _rope_t(dy, cos, sin):
    return dy * cos - _rot(dy * sin)


def _row(v):
    return v.reshape(1, -1)


def _rms_proj(x, wn, wt, tm):
    t, d = x.shape
    n = wt.shape[0]

    def body(x_ref, wn_ref, w_ref, h_ref, y_ref):
        xv = x_ref[...]
        r = lax.rsqrt(jnp.mean(xv * xv, axis=-1, keepdims=True) + EPS)
        h = (xv * r * wn_ref[...]).astype(BF16)
        h_ref[...] = h
        y_ref[...] = _dot_nt(h, w_ref[...])

    return pl.pallas_call(
        body, name="rms_proj", grid=(t // tm,),
        in_specs=[pl.BlockSpec((tm, d), lambda i: (i, 0)),
                  pl.BlockSpec((1, d), lambda i: (0, 0)),
                  pl.BlockSpec((n, d), lambda i: (0, 0))],
        out_specs=[pl.BlockSpec((tm, d), lambda i: (i, 0)),
                   pl.BlockSpec((tm, n), lambda i: (i, 0))],
        out_shape=[jax.ShapeDtypeStruct((t, d), BF16), jax.ShapeDtypeStruct((t, n), F32)],
        compiler_params=_cparams(("parallel",)),
    )(x, wn, wt)


def _rms_bwd(dh, x, wn):
    r = lax.rsqrt(jnp.mean(x * x, axis=-1, keepdims=True) + EPS)
    g = dh * wn
    dx = r * (g - x * (r * r) * jnp.mean(g * x, axis=-1, keepdims=True))
    return dx, dh * x * r


def _proj_bwd(pieces, wt, x, wn, dres, tm, ride=None):
    t = x.shape[0]
    d = x.shape[1]
    n = wt.shape[0]
    widths = [p.shape[1] for p in pieces]
    offs = [sum(widths[:i]) for i in range(len(widths))]
    assert sum(widths) == n
    npc = len(pieces)
    rd = _ride_plan(ride)

    def body(*refs):
        p_refs = refs[:npc]
        w_ref, x_ref, wn_ref, dr_ref = refs[npc:npc + 4]
        dx_ref, dwn_ref = refs[npc + 4 + rd.n:npc + 6 + rd.n]
        copies = rd.copies(refs[npc + 4:npc + 4 + rd.n], refs[npc + 6 + rd.n:npc + 6 + 2 * rd.n],
                           refs[npc + 6 + 2 * rd.n:])
        _ride_start(copies, pl.program_id(0) == 0)
        dh = None
        for p_ref, o, wd in zip(p_refs, offs, widths):
            part = _dot(p_ref[...], w_ref[o:o + wd, :])
            dh = part if dh is None else dh + part
        dx, dwn = _rms_bwd(dh, x_ref[...], wn_ref[...])
        dx_ref[...] = dr_ref[...] + dx

        @pl.when(pl.program_id(0) == 0)
        def _():
            dwn_ref[...] = jnp.zeros_like(dwn_ref)

        dwn_ref[...] += jnp.sum(dwn, axis=0, keepdims=True)
        _ride_wait(copies, pl.program_id(0) == t // tm - 1)

    return pl.pallas_call(
        body, name="proj_bwd", grid=(t // tm,),
        in_specs=[pl.BlockSpec((tm, wd), lambda i: (i, 0)) for wd in widths]
        + [pl.BlockSpec((n, d), lambda i: (0, 0)),
           pl.BlockSpec((tm, d), lambda i: (i, 0)),
           pl.BlockSpec((1, d), lambda i: (0, 0)),
           pl.BlockSpec((tm, d), lambda i: (i, 0))] + rd.in_specs,
        out_specs=[pl.BlockSpec((tm, d), lambda i: (i, 0)),
                   pl.BlockSpec((1, d), lambda i: (0, 0))] + rd.out_specs,
        out_shape=[jax.ShapeDtypeStruct((t, d), F32), jax.ShapeDtypeStruct((1, d), F32)] + rd.out_shape,
        scratch_shapes=rd.scratch,
        compiler_params=_cparams(("arbitrary",)),
    )(*pieces, wt, x, wn, dres, *rd.srcs)


def _dw_in(h0, pieces, tm):
    t, k = h0.shape
    widths = [p.shape[1] for p in pieces]
    offs = [sum(widths[:i]) for i in range(len(widths))]
    n = sum(widths)
    npc = len(pieces)

    def body(*refs):
        h_ref = refs[0]
        p_refs = refs[1:1 + npc]
        o_ref, acc_ref = refs[1 + npc:]
        i = pl.program_id(0)

        @pl.when(i == 0)
        def _():
            acc_ref[...] = jnp.zeros_like(acc_ref)

        h = h_ref[...]
        for p_ref, o, wd in zip(p_refs, offs, widths):
            acc_ref[o:o + wd, :] += _dot_tn(p_ref[...], h)

        @pl.when(i == t // tm - 1)
        def _():
            o_ref[...] = acc_ref[...].astype(BF16)

    return pl.pallas_call(
        body, name="dw_in", grid=(t // tm,),
        in_specs=[pl.BlockSpec((tm, k), lambda i: (i, 0))]
        + [pl.BlockSpec((tm, wd), lambda i: (i, 0)) for wd in widths],
        out_specs=pl.BlockSpec((n, k), lambda i: (0, 0)),
        out_shape=jax.ShapeDtypeStruct((n, k), BF16),
        scratch_shapes=[pltpu.VMEM((n, k), F32)],
        compiler_params=_cparams(("arbitrary",)),
    )(h0, *pieces)


def _mm_tn(a, b, tn, name, tm):
    t, k = a.shape
    n = b.shape[1]

    def body(a_ref, b_ref, o_ref):
        @pl.when(pl.program_id(1) == 0)
        def _():
            o_ref[...] = jnp.zeros_like(o_ref)

        o_ref[...] += _dot_tn(a_ref[...].astype(BF16), b_ref[...].astype(BF16))

    return pl.pallas_call(
        body, name=name, grid=(n // tn, t // tm),
        in_specs=[pl.BlockSpec((tm, k), lambda j, i: (i, 0)),
                  pl.BlockSpec((tm, tn), lambda j, i: (i, j))],
        out_specs=pl.BlockSpec((k, tn), lambda j, i: (0, j)),
        out_shape=jax.ShapeDtypeStruct((k, n), F32),
        compiler_params=_cparams(("parallel", "arbitrary")),
    )(a, b)


def _dw_ff(blocked, flat, name, tm):
    t, dm = flat.shape
    fb = blocked.shape[2]
    out_blk = (N_DEV, fb, dm)

    def body(b_ref, f_ref, o_ref, acc_ref):
        i = pl.program_id(0)

        @pl.when(i == 0)
        def _():
            acc_ref[...] = jnp.zeros_like(acc_ref)

        fv = f_ref[...].astype(BF16)
        for j in range(N_DEV):
            acc_ref[j] += _dot_tn(b_ref[j], fv)

        @pl.when(i == t // tm - 1)
        def _():
            o_ref[...] = acc_ref[...].astype(BF16)

    return pl.pallas_call(
        body, name=name, grid=(t // tm,),
        in_specs=[pl.BlockSpec((N_DEV, tm, fb), lambda i: (0, i, 0)),
                  pl.BlockSpec((tm, dm), lambda i: (i, 0))],
        out_specs=pl.BlockSpec(out_blk, lambda i: (0, 0, 0)),
        out_shape=jax.ShapeDtypeStruct(out_blk, BF16),
        scratch_shapes=[pltpu.VMEM(out_blk, F32)],
        compiler_params=_cparams(("arbitrary",)),
    )(blocked, flat)


def _rope_tables(s):
    rows = s // GRID_W
    row_id = jnp.repeat(jnp.arange(rows, dtype=F32), GRID_W)
    col_id = jnp.tile(jnp.arange(GRID_W, dtype=F32), rows)
    half = HEAD_DIM // 2
    inv_freq = ROPE_THETA ** (-jnp.arange(0, half, 2, dtype=F32) / half)
    ang_r = row_id[:, None] * inv_freq[None, :]
    ang_c = col_id[:, None] * inv_freq[None, :]
    ang = jnp.concatenate([ang_r, ang_r, ang_c, ang_c], axis=-1)
    cos, sin = jnp.cos(ang), jnp.sin(ang)
    return jnp.tile(cos, (1, LANES // HEAD_DIM)), jnp.tile(sin, (1, LANES // HEAD_DIM))


def _head_rms(x, w, ones):
    r = lax.rsqrt(_dot_precise(x * x, ones, 2) * (1.0 / HEAD_DIM) + EPS)
    return x * r * w, r


def _dup_half(x, kv):
    lane = lax.broadcasted_iota(jnp.int32, x.shape, 1)
    sel = (lane < 64) if kv == 0 else (lane >= 64)
    return jnp.where(sel, x, pltpu.roll(x, 64, 1))


def _qkv_prep(proj, cosq, sinq, qw, kw, ones, s, tm):
    t = proj.shape[0]
    ns = s // tm

    def body(p_ref, cos_ref, sin_ref, qw_ref, kw_ref, ones_ref, q_out, kd_out, vd_out, kdt_out, vdt_out):
        cos = jnp.tile(cos_ref[...], (1, D_ATTN // LANES))
        sin = jnp.tile(sin_ref[...], (1, D_ATTN // LANES))
        ones_m = ones_ref[...]
        qn, _ = _head_rms(p_ref[:, 0:512], qw_ref[...], ones_m)
        q_out[...] = (_rope(qn, cos, sin) * (HEAD_DIM ** -0.5)).astype(BF16)
        kn, _ = _head_rms(p_ref[:, 512:640], kw_ref[...], ones_m[0:128, 0:128])
        kr = _rope(kn, cos[:, 0:128], sin[:, 0:128])
        v = p_ref[:, 640:768]
        for kv in range(N_KV):
            kd = _dup_half(kr, kv)
            vd = _dup_half(v, kv)
            kd_out[kv] = kd.astype(BF16)
            vd_out[kv] = vd.astype(BF16)
            kdt_out[kv] = kd.T.astype(BF16)
            vdt_out[kv] = vd.T.astype(BF16)

    return pl.pallas_call(
        body, name="qkv_prep", grid=(t // tm,),
        in_specs=[pl.BlockSpec((tm, 768), lambda i: (i, 0)),
                  pl.BlockSpec((tm, LANES), lambda i: (i % ns, 0)),
                  pl.BlockSpec((tm, LANES), lambda i: (i % ns, 0)),
                  pl.BlockSpec((1, 512), lambda i: (0, 0)),
                  pl.BlockSpec((1, 128), lambda i: (0, 0)),
                  pl.BlockSpec((512, 512), lambda i: (0, 0))],
        out_specs=[pl.BlockSpec((tm, 512), lambda i: (i, 0)),
                   pl.BlockSpec((N_KV, tm, 128), lambda i: (0, i, 0)),
                   pl.BlockSpec((N_KV, tm, 128), lambda i: (0, i, 0)),
                   pl.BlockSpec((N_KV, 128, tm), lambda i: (0, 0, i)),
                   pl.BlockSpec((N_KV, 128, tm), lambda i: (0, 0, i))],
        out_shape=[jax.ShapeDtypeStruct((t, 512), BF16),
                   jax.ShapeDtypeStruct((N_KV, t, 128), BF16),
                   jax.ShapeDtypeStruct((N_KV, t, 128), BF16),
                   jax.ShapeDtypeStruct((N_KV, 128, t), BF16),
                   jax.ShapeDtypeStruct((N_KV, 128, t), BF16)],
        compiler_params=_cparams(("parallel",)),
    )(proj, cosq, sinq, qw, kw, ones)


def _qkv_bwd(proj, dq, dkd, dvd, cosq, sinq, qw, kw, ones, s, tm):
    t = proj.shape[0]
    ns = s // tm

    def body(p_ref, dq_ref, dkd_ref, dvd_ref, cos_ref, sin_ref, qw_ref, kw_ref, ones_ref,
             out_ref, dqw_ref, dkw_ref):
        cos = jnp.tile(cos_ref[...], (1, D_ATTN // LANES))
        sin = jnp.tile(sin_ref[...], (1, D_ATTN // LANES))
        ones_m = ones_ref[...]
        ones_k = ones_m[0:128, 0:128]

        def norm_bwd(x, w, dn, om):
            r = lax.rsqrt(_dot_precise(x * x, om, 2) * (1.0 / HEAD_DIM) + EPS)
            g = dn * w
            dx = r * (g - x * (r * r) * (_dot_precise(g * x, om, 2) * (1.0 / HEAD_DIM)))
            return dx, jnp.sum(dn * x * r, axis=0, keepdims=True)

        q = p_ref[:, 0:512]
        dqn = _rope_t(dq_ref[...], cos, sin) * (HEAD_DIM ** -0.5)
        dq_raw, dqw = norm_bwd(q, qw_ref[...], dqn, ones_m)
        out_ref[:, 0:512] = dq_raw.astype(BF16)

        lane = lax.broadcasted_iota(jnp.int32, (tm, 128), 1)

        def fold(ref):
            a0 = ref[0]
            a1 = ref[1]
            f0 = a0 + pltpu.roll(a0, 64, 1)
            f1 = a1 + pltpu.roll(a1, 64, 1)
            return jnp.where(lane < 64, f0, f1)

        k = p_ref[:, 512:640]
        dkn = _rope_t(fold(dkd_ref), cos[:, 0:128], sin[:, 0:128])
        dk_raw, dkw = norm_bwd(k, kw_ref[...], dkn, ones_k)
        out_ref[:, 512:640] = dk_raw.astype(BF16)
        out_ref[:, 640:768] = fold(dvd_ref).astype(BF16)

        @pl.when(pl.program_id(0) == 0)
        def _():
            dqw_ref[...] = jnp.zeros_like(dqw_ref)
            dkw_ref[...] = jnp.zeros_like(dkw_ref)

        dqw_ref[...] += dqw
        dkw_ref[...] += dkw

    return pl.pallas_call(
        body, name="qkv_bwd", grid=(t // tm,),
        in_specs=[pl.BlockSpec((tm, 768), lambda i: (i, 0)),
                  pl.BlockSpec((tm, 512), lambda i: (i, 0)),
                  pl.BlockSpec((N_KV, tm, 128), lambda i: (0, i, 0)),
                  pl.BlockSpec((N_KV, tm, 128), lambda i: (0, i, 0)),
                  pl.BlockSpec((tm, LANES), lambda i: (i % ns, 0)),
                  pl.BlockSpec((tm, LANES), lambda i: (i % ns, 0)),
                  pl.BlockSpec((1, 512), lambda i: (0, 0)),
                  pl.BlockSpec((1, 128), lambda i: (0, 0)),
                  pl.BlockSpec((512, 512), lambda i: (0, 0))],
        out_specs=[pl.BlockSpec((tm, 768), lambda i: (i, 0)),
                   pl.BlockSpec((1, 512), lambda i: (0, 0)),
                   pl.BlockSpec((1, 128), lambda i: (0, 0))],
        out_shape=[jax.ShapeDtypeStruct((t, 768), BF16),
                   jax.ShapeDtypeStruct((1, 512), F32),
                   jax.ShapeDtypeStruct((1, 128), F32)],
        compiler_params=_cparams(("arbitrary",)),
    )(proj, dq, dkd, dvd, cosq, sinq, qw, kw, ones)


def _grid_step_id(grid):
    idx = pl.program_id(0)
    for ax in range(1, len(grid)):
        idx = idx * grid[ax] + pl.program_id(ax)
    return idx


def _attn_fwd(q, kd, vdt, nb, s, tq, ride=None):
    t = q.shape[0]
    nq = s // tq
    rd = _ride_plan(ride)
    grid = (nb, N_HEADS // 2, nq)
    nsteps = nb * (N_HEADS // 2) * nq

    def body(*refs):
        q_ref, k_ref, vt_ref = refs[:3]
        o_ref, lse_ref = refs[3 + rd.n:5 + rd.n]
        copies = rd.copies(refs[3:3 + rd.n], refs[5 + rd.n:5 + 2 * rd.n], refs[5 + 2 * rd.n:])
        step_id = _grid_step_id(grid)
        _ride_start(copies, step_id == 0)
        qv = q_ref[...].astype(F32)
        lane = lax.broadcasted_iota(jnp.int32, qv.shape, 1)
        k = k_ref[0]
        vt = vt_ref[0]
        outs = []
        scores = [_dot_nt(k, jnp.where((lane < 64) if half == 0 else (lane >= 64), qv, 0.0).astype(BF16))
                  for half in range(2)]
        for half in range(2):
            st = scores[half]
            m = jnp.max(st, axis=0, keepdims=True)
            p = jnp.exp(st - m)
            l = jnp.sum(p, axis=0, keepdims=True)
            ot = _dot(vt, p.astype(BF16)) / l
            lse_ref[0, half] = m + jnp.log(l)
            outs.append(ot)
        row = lax.broadcasted_iota(jnp.int32, outs[0].shape, 0)
        o_ref[...] = jnp.where(row < 64, outs[0], outs[1]).T
        _ride_wait(copies, step_id == nsteps - 1)

    return pl.pallas_call(
        body, name="attn_fwd", grid=grid,
        in_specs=[pl.BlockSpec((tq, 128), lambda b, p, i: (b * nq + i, p)),
                  pl.BlockSpec((1, s, 128), lambda b, p, i: (p // 2, b, 0)),
                  pl.BlockSpec((1, 128, s), lambda b, p, i: (p // 2, 0, b))] + rd.in_specs,
        out_specs=[pl.BlockSpec((tq, 128), lambda b, p, i: (b * nq + i, p)),
                   pl.BlockSpec((1, 2, 1, tq), lambda b, p, i: (b, p, 0, i))] + rd.out_specs,
        out_shape=[jax.ShapeDtypeStruct((t, D_ATTN), F32),
                   jax.ShapeDtypeStruct((nb, N_HEADS, 1, s), F32)] + rd.out_shape,
        scratch_shapes=rd.scratch,
        compiler_params=_cparams(("arbitrary", "arbitrary", "arbitrary")),
    )(q, kd, vdt, *rd.srcs)


def _attn_bwd(q, kd, vd, kdt, o, lse, do, nb, s, tq, ride=None):
    t = q.shape[0]
    nq = s // tq
    ones8 = jnp.ones((8, 128), BF16)
    rd = _ride_plan(ride)
    grid = (nb, N_KV, 2, nq)
    nsteps = nb * N_KV * 2 * nq

    def body(*refs):
        q_ref, k_ref, v_ref, kt_ref, o_ref, lse_ref, do_ref, ones_ref = refs[:8]
        dq_ref, dk_ref, dv_ref = refs[8 + rd.n:11 + rd.n]
        copies = rd.copies(refs[8:8 + rd.n], refs[11 + rd.n:11 + 2 * rd.n], refs[11 + 2 * rd.n:])
        step_id = _grid_step_id(grid)
        _ride_start(copies, step_id == 0)

        @pl.when((pl.program_id(2) == 0) & (pl.program_id(3) == 0))
        def _():
            dk_ref[...] = jnp.zeros_like(dk_ref)
            dv_ref[...] = jnp.zeros_like(dv_ref)

        qv = q_ref[...].astype(F32)
        dov = do_ref[...]
        ov = o_ref[...]
        lane = lax.broadcasted_iota(jnp.int32, qv.shape, 1)
        k = k_ref[0]
        v = v_ref[0]
        kt = kt_ref[0]
        dqs = []
        dk_acc = None
        dv_acc = None
        for half in range(2):
            sel = (lane < 64) if half == 0 else (lane >= 64)
            qh = jnp.where(sel, qv, 0.0).astype(BF16)
            doh = jnp.where(sel, dov, 0.0)
            dob = doh.astype(BF16)
            delta = None
            for part in _split_bf16(doh * ov, 3):
                d8 = _dot_nt(ones_ref[...], part)
                delta = d8 if delta is None else delta + d8
            delta = delta[0:1, :]
            st = _dot_nt(k, qh)
            pt = jnp.exp(st - lse_ref[0, half])
            dpt = _dot_nt(v, dob)
            dst = (pt * (dpt - delta)).astype(BF16)
            dkh = _dot(dst, qh)
            dvh = _dot(pt.astype(BF16), dob)
            dk_acc = dkh if dk_acc is None else dk_acc + dkh
            dv_acc = dvh if dv_acc is None else dv_acc + dvh
            dqs.append(_dot(kt, dst))
        dk_ref[0] += dk_acc
        dv_ref[0] += dv_acc
        row = lax.broadcasted_iota(jnp.int32, dqs[0].shape, 0)
        dq_ref[...] = jnp.where(row < 64, dqs[0], dqs[1]).T
        _ride_wait(copies, step_id == nsteps - 1)

    qmap = lambda b, g, p, i: (b * nq + i, g * 2 + p)
    kvmap = lambda b, g, p, i: (g, b, 0)
    return pl.pallas_call(
        body, name="attn_bwd", grid=grid,
        in_specs=[pl.BlockSpec((tq, 128), qmap),
                  pl.BlockSpec((1, s, 128), kvmap),
                  pl.BlockSpec((1, s, 128), kvmap),
                  pl.BlockSpec((1, 128, s), lambda b, g, p, i: (g, 0, b)),
                  pl.BlockSpec((tq, 128), qmap),
                  pl.BlockSpec((1, 2, 1, tq), lambda b, g, p, i: (b, g * 2 + p, 0, i)),
                  pl.BlockSpec((tq, 128), qmap),
                  pl.BlockSpec((8, 128), lambda b, g, p, i: (0, 0))] + rd.in_specs,
        out_specs=[pl.BlockSpec((tq, 128), qmap),
                   pl.BlockSpec((1, s, 128), kvmap),
                   pl.BlockSpec((1, s, 128), kvmap)] + rd.out_specs,
        out_shape=[jax.ShapeDtypeStruct((t, D_ATTN), F32),
                   jax.ShapeDtypeStruct((N_KV, t, 128), F32),
                   jax.ShapeDtypeStruct((N_KV, t, 128), F32)] + rd.out_shape,
        scratch_shapes=rd.scratch,
        compiler_params=_cparams(("arbitrary", "arbitrary", "arbitrary", "arbitrary")),
    )(q, kd, vd, kdt, o, lse, do, ones8, *rd.srcs)


SUB = 16
N_SUB = CHUNK // SUB


def _tri_mats():
    i = np.arange(CHUNK)
    same = (i[:, None] // SUB) == (i[None, :] // SUB)
    lower = jnp.asarray(same & (i[:, None] >= i[None, :]), dtype=BF16)
    upper = jnp.asarray(same & (i[:, None] <= i[None, :]), dtype=BF16)
    return jnp.stack([lower, upper])


def _running_sum(tri, x):
    acc = None
    for part in _split_bf16(x, 3):
        t = _dot(tri, part)
        acc = t if acc is None else acc + t
    return acc


def _gates(z, lb):
    sig = _sigmoid(z)
    f = lb + (1.0 - lb) * sig
    logf = jnp.log(jnp.maximum(f, F_MIN))
    k = (1.0 - lb) * (1.0 - sig)
    return sig, f, logf, k


def _row_group(jg, anti):
    if anti:
        return 0, 8 * jg + 8
    return 8 * jg, SUB


def _sub_order(anti):
    return range(N_SUB - 1, -1, -1) if anti else range(N_SUB)


def _block_columns(b, anti):
    tt = lax.broadcasted_iota(jnp.int32, (SUB, LANES), 0)
    cols = []
    for jg in range(SUB // 8):
        r0, r1 = _row_group(jg, anti)
        br = b[r0:r1]
        tr = tt[r0:r1]
        for i in range(8):
            sc = 8 * jg + i
            mask = (tr <= sc) if anti else (tr >= sc)
            cols.append((r0, r1, sc, jnp.where(mask, jnp.exp(jnp.minimum(br - b[sc:sc + 1], 0.0)), 0.0)))
    return cols


def _lockstep(gens):
    results = [None] * len(gens)
    live = list(range(len(gens)))
    while live:
        for i in list(live):
            try:
                next(gens[i])
            except StopIteration as stop:
                results[i] = stop.value
                live.remove(i)
    return results


def _scatter_rows(base, accs):
    pieces = []
    for g in range(SUB // 8):
        tot = base[8 * g:8 * g + 8]
        for (r0, r1), acc in accs.items():
            if r0 <= 8 * g and 8 * g + 8 <= r1:
                tot = tot + acc[8 * g - r0:8 * g - r0 + 8]
        pieces.append(tot)
    return jnp.concatenate(pieces, axis=0)


def _chunk_fwd(q, k, v, b, st, bones, bmask, anti):
    rs = [slice(SUB * i, SUB * i + SUB) for i in range(N_SUB)]
    decay, update, prods, spans, qbs = [], [], [], [], []
    for i in range(N_SUB):
        qi, ki, vi, bi = q[rs[i]], k[rs[i]], v[rs[i]], b[rs[i]]
        b_last = bi[0:1] if anti else bi[SUB - 1:SUB]
        decay.append(jnp.exp(b_last))
        update.append(_dot_tn(vi.astype(BF16), (ki * jnp.exp(b_last - bi)).astype(BF16)) * bmask)
        qbs.append((qi * jnp.exp(bi)).astype(BF16))
        for r0, r1, sc, e in _block_columns(bi, anti):
            prods.append(qi[r0:r1] * e * ki[sc:sc + 1])
            spans.append((i, r0, r1, sc))
    pb = _dot(jnp.concatenate(prods, axis=0).astype(BF16), bones)
    yield
    entered = [None] * N_SUB
    for i in _sub_order(anti):
        entered[i] = st
        st = st * decay[i] + update[i]
    yield
    accs = [dict() for _ in range(N_SUB)]
    off = 0
    for i, r0, r1, sc in spans:
        term = pb[off:off + r1 - r0] * v[rs[i]][sc:sc + 1]
        off += r1 - r0
        accs[i][(r0, r1)] = term if (r0, r1) not in accs[i] else accs[i][(r0, r1)] + term
    outs = [_scatter_rows(_dot_nt(qbs[i], entered[i].astype(BF16)), accs[i]) for i in range(N_SUB)]
    return jnp.concatenate(outs, axis=0), st, entered


def _hgrn_fwd(proj, lb, gw, nb, s, ride=None):
    t = proj.shape[0]
    nc = s // CHUNK
    tri = _tri_mats()
    bones = _block_ones(LANES, HEAD_DIM)
    rd = _ride_plan(ride)

    def body(*refs):
        q_ref, zf_ref, zb_ref, v_ref, g_ref, lb_ref, gw_ref, tri_ref, bones_ref = refs[:9]
        y_ref, os_ref, sts_ref = refs[9 + rd.n:12 + rd.n]
        st_ref = refs[12 + 2 * rd.n]
        copies = rd.copies(refs[9:9 + rd.n], refs[12 + rd.n:12 + 2 * rd.n], refs[13 + 2 * rd.n:])
        step_id = pl.program_id(0) * 2 + pl.program_id(1)
        _ride_start(copies, step_id == 0)
        bones_m = bones_ref[...]
        bmask = bones_m.astype(F32)
        st_ref[...] = jnp.zeros_like(st_ref)

        def one_direction(n, anti):
            side = 1 if anti else 0
            z_ref = zb_ref if anti else zf_ref
            cn = (nc - 1 - n) if anti else n
            rows = pl.ds(pl.multiple_of(cn * CHUNK, CHUNK), CHUNK)
            q = q_ref[rows, :]
            v = v_ref[rows, :]
            _, _, logf, k = _gates(z_ref[rows, :], lb_ref[side:side + 1])
            b = _running_sum(tri_ref[side], logf)
            yield
            o, st_new, entered = yield from _chunk_fwd(q, k, v, b, st_ref[side], bones_m, bmask, anti)
            for i in range(N_SUB):
                sts_ref[0, 0, side, cn * N_SUB + i] = entered[i].astype(BF16)
            st_ref[side] = st_new
            (y_ref if anti else os_ref)[rows, :] = o

        def step(n, carry):
            _lockstep([one_direction(n, False), one_direction(n, True)])
            return carry

        lax.fori_loop(0, nc, step, 0)

        def join(n, carry):
            rows = pl.ds(pl.multiple_of(n * CHUNK, CHUNK), CHUNK)
            osum = os_ref[rows, :] + y_ref[rows, :]
            os_ref[rows, :] = osum
            r = lax.rsqrt(_dot_precise(osum * osum, bones_m, 2) * (1.0 / HEAD_DIM) + EPS)
            hg = g_ref[rows, :]
            y_ref[rows, :] = osum * r * gw_ref[...] * (hg * _sigmoid(hg))
            return carry

        lax.fori_loop(0, nc, join, 0)
        _ride_wait(copies, step_id == nb * 2 - 1)

    def col(c):
        return pl.BlockSpec((s, LANES), lambda b, p, c=c: (b, c + p))

    return pl.pallas_call(
        body, name="hgrn_fwd", grid=(nb, 2),
        in_specs=[col(COL_HQ), col(COL_ZFW), col(COL_ZBW), col(COL_HI), col(COL_HG),
                  pl.BlockSpec((2, LANES), lambda b, p: (0, p)),
                  pl.BlockSpec((1, LANES), lambda b, p: (0, 0)),
                  pl.BlockSpec((2, CHUNK, CHUNK), lambda b, p: (0, 0, 0)),
                  pl.BlockSpec((LANES, LANES), lambda b, p: (0, 0))] + rd.in_specs,
        out_specs=[pl.BlockSpec((s, LANES), lambda b, p: (b, p)),
                   pl.BlockSpec((s, LANES), lambda b, p: (b, p)),
                   pl.BlockSpec((1, 1, 2, nc * N_SUB, LANES, LANES), lambda b, p: (b, p, 0, 0, 0, 0))]
        + rd.out_specs,
        out_shape=[jax.ShapeDtypeStruct((t, D_HGRN), F32), jax.ShapeDtypeStruct((t, D_HGRN), F32),
                   jax.ShapeDtypeStruct((nb, 2, 2, nc * N_SUB, LANES, LANES), BF16)] + rd.out_shape,
        scratch_shapes=[pltpu.VMEM((2, LANES, LANES), F32)] + rd.scratch,
        compiler_params=_cparams(("arbitrary", "arbitrary")),
    )(proj, proj, proj, proj, proj, lb, gw, tri, bones, *rd.srcs)


def _chunk_bwd(q, k, v, b, do, states, rt, bones, bmask, anti):
    rs = [slice(SUB * i, SUB * i + SUB) for i in range(N_SUB)]
    r8 = lax.broadcasted_iota(jnp.int32, (8, LANES), 0)
    decay, update, dq_inter, ebls, prods_p, prods_d, spans, qes, kes = [], [], [], [], [], [], [], [], []
    for i in range(N_SUB):
        qi, ki, vi, bi, doi = q[rs[i]], k[rs[i]], v[rs[i]], b[rs[i]], do[rs[i]]
        b_last = bi[0:1] if anti else bi[SUB - 1:SUB]
        eb = jnp.exp(bi)
        dob = doi.astype(BF16)
        decay.append(jnp.exp(b_last))
        ebls.append(jnp.exp(b_last - bi))
        update.append(_dot_tn(dob, (qi * eb).astype(BF16)) * bmask)
        dq_inter.append(eb * _dot(dob, states[i]))
        for r0, r1, sc, e in _block_columns(bi, anti):
            qe = qi[r0:r1] * e
            qes.append(qe)
            kes.append(e * ki[sc:sc + 1])
            prods_p.append(qe * ki[sc:sc + 1])
            prods_d.append(doi[r0:r1] * vi[sc:sc + 1])
            spans.append((i, r0, r1, sc))
    sums = _dot(jnp.concatenate(prods_p + prods_d, axis=0).astype(BF16), bones)
    half = sum(r1 - r0 for _, r0, r1, _ in spans)
    yield
    entered = [None] * N_SUB
    for i in reversed(list(_sub_order(anti))):
        entered[i] = rt
        rt = rt * decay[i] + update[i]
    yield
    accs = [dict() for _ in range(N_SUB)]
    dk_blks = [[jnp.zeros((8, LANES), F32) for _ in range(SUB // 8)] for _ in range(N_SUB)]
    dv_blks = [[jnp.zeros((8, LANES), F32) for _ in range(SUB // 8)] for _ in range(N_SUB)]
    off = 0
    for n, (i, r0, r1, sc) in enumerate(spans):
        nr = r1 - r0
        pb = sums[off:off + nr]
        dpb = sums[half + off:half + off + nr]
        off += nr
        term = dpb * kes[n]
        accs[i][(r0, r1)] = term if (r0, r1) not in accs[i] else accs[i][(r0, r1)] + term
        dk_s = jnp.sum(dpb * qes[n], axis=0, keepdims=True)
        dv_s = jnp.sum(pb * do[rs[i]][r0:r1], axis=0, keepdims=True)
        dk_blks[i][sc // 8] = jnp.where(r8 == sc % 8, dk_s, dk_blks[i][sc // 8])
        dv_blks[i][sc // 8] = jnp.where(r8 == sc % 8, dv_s, dv_blks[i][sc // 8])
    dqs, dks, dvs, dbs = [], [], [], []
    for i in range(N_SUB):
        ki, vi = k[rs[i]], v[rs[i]]
        rtb = entered[i].astype(BF16)
        dk_inter = ebls[i] * _dot(vi.astype(BF16), rtb)
        dv_inter = _dot_nt((ki * ebls[i]).astype(BF16), rtb)
        dqs.append(_scatter_rows(dq_inter[i], accs[i]))
        dks.append(dk_inter + jnp.concatenate(dk_blks[i], axis=0))
        dvs.append(dv_inter + jnp.concatenate(dv_blks[i], axis=0))
        db_last = (jnp.sum(ki * dk_inter, axis=0, keepdims=True)
                   + decay[i] * jnp.sum(entered[i] * states[i].astype(F32), axis=0, keepdims=True))
        dbs.append(jnp.broadcast_to(db_last, (SUB, LANES)))
    cat = lambda xs: jnp.concatenate(xs, axis=0)
    return cat(dqs), cat(dks), cat(dvs), rt, cat(dbs)


def _hgrn_bwd(proj, lb, gw, osum, states, dy, nb, s, ride=None):
    t = proj.shape[0]
    nc = s // CHUNK
    assert nc % 2 == 0
    tri = _tri_mats()
    bones = _block_ones(LANES, HEAD_DIM)
    rd = _ride_plan(ride)

    def body(*refs):
        (q_ref, zf_ref, zb_ref, v_ref, g_ref, lb_ref, gw_ref, os_ref, sts_ref, dy_ref, tri_ref,
         bones_ref) = refs[:12]
        dq_ref, dzf_ref, dzb_ref, dv_ref, dg_ref, dgw_ref, dlb_ref = refs[12 + rd.n:19 + rd.n]
        do_sc, dq_sc, dv_sc, rt_cur = refs[19 + 2 * rd.n:23 + 2 * rd.n]
        copies = rd.copies(refs[12:12 + rd.n], refs[19 + rd.n:19 + 2 * rd.n], refs[23 + 2 * rd.n:])
        step_id = pl.program_id(0) * 2 + pl.program_id(1)
        _ride_start(copies, step_id == 0)
        bones_m = bones_ref[...]
        bmask = bones_m.astype(F32)
        gwv = gw_ref[...]

        def head(n, acc):
            rows = pl.ds(pl.multiple_of(n * CHUNK, CHUNK), CHUNK)
            o = os_ref[rows, :]
            hg = g_ref[rows, :]
            dyv = dy_ref[rows, :]
            sg = _sigmoid(hg)
            r = lax.rsqrt(_dot_precise(o * o, bones_m, 2) * (1.0 / HEAD_DIM) + EPS)
            nrm = o * r * gwv
            dn = dyv * (hg * sg)
            dg_ref[rows, :] = (dyv * nrm * (sg * (1.0 + hg * (1.0 - sg)))).astype(BF16)
            g = dn * gwv
            mean_go = _dot_precise(g * o, bones_m, 2) * (1.0 / HEAD_DIM)
            do_sc[rows, :] = r * (g - o * (r * r) * mean_go)
            return acc + jnp.sum(dn * o * r, axis=0, keepdims=True)

        dgw_ref[0] = lax.fori_loop(0, nc, head, jnp.zeros((1, LANES), F32))
        dq_sc[...] = jnp.zeros_like(dq_sc)
        dv_sc[...] = jnp.zeros_like(dv_sc)

        rt_cur[...] = jnp.zeros_like(rt_cur)

        def one_direction(n, anti):
            side = 1 if anti else 0
            z_ref = zb_ref if anti else zf_ref
            dz_ref = dzb_ref if anti else dzf_ref
            lbv = lb_ref[side:side + 1]
            cn = n if anti else (nc - 1 - n)
            rows = pl.ds(pl.multiple_of(cn * CHUNK, CHUNK), CHUNK)
            q = q_ref[rows, :]
            v = v_ref[rows, :]
            sig, f, logf, k = _gates(z_ref[rows, :], lbv)
            b = _running_sum(tri_ref[side], logf)
            yield
            do = do_sc[rows, :]
            entered = [sts_ref[0, 0, side, cn * N_SUB + i] for i in range(N_SUB)]
            dq, dk, dv, rt_new, db_last = yield from _chunk_bwd(q, k, v, b, do, entered, rt_cur[side], bones_m,
                                                                bmask, anti)
            rt_cur[side] = rt_new
            dq_sc[rows, :] += dq
            dv_sc[rows, :] += dv
            dlogf = _running_sum(tri_ref[1 - side], q * dq - k * dk) + db_last
            dfl = jnp.where(f > F_MIN, dlogf / f, 0.0)
            dz_ref[rows, :] = ((dfl - dk) * (1.0 - lbv) * sig * (1.0 - sig)).astype(BF16)
            return jnp.sum((dfl - dk) * (1.0 - sig), axis=0, keepdims=True)

        def back(n, dlb):
            d0, d1 = _lockstep([one_direction(n, False), one_direction(n, True)])
            return dlb[0] + d0, dlb[1] + d1

        zero = jnp.zeros((1, LANES), F32)
        dlb0, dlb1 = lax.fori_loop(0, nc, back, (zero, zero))
        dlb_ref[0, 0:1, :] = dlb0
        dlb_ref[0, 1:2, :] = dlb1

        dq_ref[...] = dq_sc[...].astype(BF16)
        dv_ref[...] = dv_sc[...].astype(BF16)
        _ride_wait(copies, step_id == nb * 2 - 1)

    def col(c):
        return pl.BlockSpec((s, LANES), lambda b, p, c=c: (b, c + p))

    sl = pl.BlockSpec((s, LANES), lambda b, p: (b, p))
    out_t = jax.ShapeDtypeStruct((t, D_HGRN), BF16)
    return pl.pallas_call(
        body, name="hgrn_bwd", grid=(nb, 2),
        in_specs=[col(COL_HQ), col(COL_ZFW), col(COL_ZBW), col(COL_HI), col(COL_HG),
                  pl.BlockSpec((2, LANES), lambda b, p: (0, p)),
                  pl.BlockSpec((1, LANES), lambda b, p: (0, 0)),
                  sl,
                  pl.BlockSpec((1, 1, 2, nc * N_SUB, LANES, LANES), lambda b, p: (b, p, 0, 0, 0, 0)),
                  sl,
                  pl.BlockSpec((2, CHUNK, CHUNK), lambda b, p: (0, 0, 0)),
                  pl.BlockSpec((LANES, LANES), lambda b, p: (0, 0))] + rd.in_specs,
        out_specs=[sl, sl, sl, sl, sl,
                   pl.BlockSpec((1, 1, LANES), lambda b, p: (b, 0, p)),
                   pl.BlockSpec((1, 2, LANES), lambda b, p: (b, 0, p))] + rd.out_specs,
        out_shape=[out_t, out_t, out_t, out_t, out_t,
                   jax.ShapeDtypeStruct((nb, 1, D_HGRN), F32),
                   jax.ShapeDtypeStruct((nb, 2, D_HGRN), F32)] + rd.out_shape,
        scratch_shapes=[pltpu.VMEM((s, LANES), F32), pltpu.VMEM((s, LANES), F32), pltpu.VMEM((s, LANES), F32),
                        pltpu.VMEM((2, LANES, LANES), F32)] + rd.scratch,
        compiler_params=_cparams(("arbitrary", "arbitrary")),
    )(proj, proj, proj, proj, proj, lb, gw, osum, states, dy, tri, bones, *rd.srcs)


def _lower_bounds(logits):
    def body(lg_ref, lb_ref):
        rows = [lg_ref[l:l + 1, :] for l in range(DEPTH)]
        m = functools.reduce(jnp.maximum, rows)
        ex = [jnp.exp(r - m) for r in rows]
        den = functools.reduce(jnp.add, ex)
        run = jnp.zeros_like(m)
        for l in range(DEPTH):
            if l > 0:
                run = run + ex[l] / den
            lb_ref[l:l + 1, :] = run

    return pl.pallas_call(body, name="lower_bounds", out_shape=jax.ShapeDtypeStruct(logits.shape, F32))(logits)


def _lower_bounds_bwd(logits, dlb):
    def body(lg_ref, dlb_ref, dlg_ref):
        rows = [lg_ref[l:l + 1, :] for l in range(DEPTH)]
        m = functools.reduce(jnp.maximum, rows)
        ex = [jnp.exp(r - m) for r in rows]
        den = functools.reduce(jnp.add, ex)
        sm = [e / den for e in ex]
        dsm = [jnp.zeros_like(m) for _ in range(DEPTH)]
        for i in range(1, DEPTH):
            for l in range(i, DEPTH):
                dsm[i] = dsm[i] + dlb_ref[l:l + 1, :]
        dot = functools.reduce(jnp.add, [sm[i] * dsm[i] for i in range(DEPTH)])
        for i in range(DEPTH):
            dlg_ref[i:i + 1, :] = sm[i] * (dsm[i] - dot)

    return pl.pallas_call(body, name="lower_bounds_bwd", out_shape=jax.ShapeDtypeStruct(logits.shape, F32))(logits, dlb)


CONV_ROWS = 128


def _conv_core(a, bg, dww, dwb, lnw, lnb, upad_ref, s):
    sb = _sigmoid(bg)
    u = a * sb
    upad_ref[0:16, :] = jnp.zeros((16, D_CONV), F32)
    upad_ref[16:16 + s, :] = u
    upad_ref[16 + s:32 + s, :] = jnp.zeros((16, D_CONV), F32)
    rows = min(s, CONV_ROWS)
    pieces = []
    for r0 in range(0, s, rows):
        acc = None
        for j in range(CONV_W):
            term = upad_ref[r0 + 1 + j:r0 + 1 + j + rows, :] * dww[j:j + 1, :]
            acc = term if acc is None else acc + term
        pieces.append(acc)
    c = jnp.concatenate(pieces, axis=0) + dwb
    mu = jnp.mean(c, axis=-1, keepdims=True)
    xc = c - mu
    rstd = lax.rsqrt(jnp.mean(xc * xc, axis=-1, keepdims=True) + LN_EPS)
    nh = xc * rstd
    l = nh * lnw + lnb
    sl = _sigmoid(l)
    return sb, nh, rstd, l, sl


def _conv_fwd(proj, dww, dwb, lnw, lnb, pww, pwb, nb, s):
    t = proj.shape[0]
    assert s % min(s, CONV_ROWS) == 0

    def body(a_ref, b_ref, dww_ref, dwb_ref, lnw_ref, lnb_ref, pww_ref, pwb_ref, y_ref, upad_ref):
        _, _, _, l, sl = _conv_core(a_ref[...], b_ref[...], dww_ref[...], dwb_ref[...], lnw_ref[...],
                                    lnb_ref[...], upad_ref, s)
        y_ref[...] = _dot((l * sl).astype(BF16), pww_ref[...]) + pwb_ref[...]

    vec = pl.BlockSpec((1, D_CONV), lambda b: (0, 0))
    return pl.pallas_call(
        body, name="conv_fwd", grid=(nb,),
        in_specs=[pl.BlockSpec((s, D_CONV), lambda b: (b, COL_CA)),
                  pl.BlockSpec((s, D_CONV), lambda b: (b, COL_CB)),
                  pl.BlockSpec((32, D_CONV), lambda b: (0, 0)), vec, vec, vec,
                  pl.BlockSpec((D_CONV, D_CONV), lambda b: (0, 0)), vec],
        out_specs=pl.BlockSpec((s, D_CONV), lambda b: (b, 0)),
        out_shape=jax.ShapeDtypeStruct((t, D_CONV), F32),
        scratch_shapes=[pltpu.VMEM((s + 32, D_CONV), F32)],
        compiler_params=_cparams(("parallel",)),
    )(proj, proj, dww, dwb, lnw, lnb, pww, pwb)


def _conv_bwd(proj, dy, dww, dwb, lnw, lnb, pww, nb, s):
    t = proj.shape[0]

    def body(a_ref, b_ref, dy_ref, dww_ref, dwb_ref, lnw_ref, lnb_ref, pww_ref,
             dab_ref, ddww_ref, ddwb_ref, dlnw_ref, dlnb_ref, dpww_ref, dpwb_ref, upad_ref, dcpad_ref):
        a = a_ref[...]
        dww = dww_ref[...]
        sb, nh, rstd, l, sl = _conv_core(a, b_ref[...], dww, dwb_ref[...], lnw_ref[...], lnb_ref[...],
                                         upad_ref, s)
        dyv = dy_ref[...]
        dyb = dyv.astype(BF16)
        ds = _dot_nt(dyb, pww_ref[...])
        dl = ds * (sl * (1.0 + l * (1.0 - sl)))
        dn = dl * lnw_ref[...]
        dc = rstd * (dn - jnp.mean(dn, axis=-1, keepdims=True)
                     - nh * jnp.mean(dn * nh, axis=-1, keepdims=True))

        @pl.when(pl.program_id(0) == 0)
        def _():
            for r in (ddww_ref, ddwb_ref, dlnw_ref, dlnb_ref, dpww_ref, dpwb_ref):
                r[...] = jnp.zeros_like(r)

        dpww_ref[...] += _dot_tn((l * sl).astype(BF16), dyb)
        dpwb_ref[...] += jnp.sum(dyv, axis=0, keepdims=True)
        dlnw_ref[...] += jnp.sum(dl * nh, axis=0, keepdims=True)
        dlnb_ref[...] += jnp.sum(dl, axis=0, keepdims=True)
        ddwb_ref[...] += jnp.sum(dc, axis=0, keepdims=True)

        dcpad_ref[0:16, :] = jnp.zeros((16, D_CONV), F32)
        dcpad_ref[16:16 + s, :] = dc
        dcpad_ref[16 + s:32 + s, :] = jnp.zeros((16, D_CONV), F32)
        rows = min(s, CONV_ROWS)
        r8 = lax.broadcasted_iota(jnp.int32, (32, D_CONV), 0)
        ddww = jnp.zeros((32, D_CONV), F32)
        pieces = []
        for r0 in range(0, s, rows):
            acc = None
            dcr = dcpad_ref[16 + r0:16 + r0 + rows, :]
            for j in range(CONV_W):
                term = dcpad_ref[r0 + 31 - j:r0 + 31 - j + rows, :] * dww[j:j + 1, :]
                acc = term if acc is None else acc + term
                wj = jnp.sum(dcr * upad_ref[r0 + 1 + j:r0 + 1 + j + rows, :], axis=0, keepdims=True)
                ddww = ddww + jnp.where(r8 == j, wj, 0.0)
            pieces.append(acc)
        du = jnp.concatenate(pieces, axis=0)
        ddww_ref[...] += ddww
        dab_ref[:, 0:D_CONV] = (du * sb).astype(BF16)
        dab_ref[:, D_CONV:2 * D_CONV] = (du * a * sb * (1.0 - sb)).astype(BF16)

    vec = pl.BlockSpec((1, D_CONV), lambda b: (0, 0))
    mat = pl.BlockSpec((D_CONV, D_CONV), lambda b: (0, 0))
    w32 = pl.BlockSpec((32, D_CONV), lambda b: (0, 0))
    vshape = jax.ShapeDtypeStruct((1, D_CONV), F32)
    return pl.pallas_call(
        body, name="conv_bwd", grid=(nb,),
        in_specs=[pl.BlockSpec((s, D_CONV), lambda b: (b, COL_CA)),
                  pl.BlockSpec((s, D_CONV), lambda b: (b, COL_CB)),
                  pl.BlockSpec((s, D_CONV), lambda b: (b, 0)),
                  w32, vec, vec, vec, mat],
        out_specs=[pl.BlockSpec((s, 2 * D_CONV), lambda b: (b, 0)), w32, vec, vec, vec, mat, vec],
        out_shape=[jax.ShapeDtypeStruct((t, 2 * D_CONV), BF16),
                   jax.ShapeDtypeStruct((32, D_CONV), F32), vshape, vshape, vshape,
                   jax.ShapeDtypeStruct((D_CONV, D_CONV), F32), vshape],
        scratch_shapes=[pltpu.VMEM((s + 32, D_CONV), F32), pltpu.VMEM((s + 32, D_CONV), F32)],
        compiler_params=_cparams(("arbitrary",)),
    )(proj, proj, dy, dww, dwb, lnw, lnb, pww)


def _mix_out(o_attn, y_hgrn, y_conv, x, aw, cw, w_out, tm):
    t = x.shape[0]

    def body(o_ref, h_ref, c_ref, x_ref, aw_ref, cw_ref, w_ref, mixed_ref, x1_ref):
        o = o_ref[...]
        a = o * lax.rsqrt(jnp.mean(o * o, axis=-1, keepdims=True) + EPS) * aw_ref[...]
        yc = c_ref[...]
        c = yc * lax.rsqrt(jnp.mean(yc * yc, axis=-1, keepdims=True) + EPS) * cw_ref[...]
        ab, hb, cb = a.astype(BF16), h_ref[...].astype(BF16), c.astype(BF16)
        mixed_ref[:, 0:512] = ab
        mixed_ref[:, 512:768] = hb
        mixed_ref[:, 768:1024] = cb
        x1_ref[...] = (x_ref[...] + _dot(ab, w_ref[0:512, :]) + _dot(hb, w_ref[512:768, :])
                       + _dot(cb, w_ref[768:1024, :]))

    def tok(w):
        return pl.BlockSpec((tm, w), lambda i: (i, 0))

    return pl.pallas_call(
        body, name="mix_out", grid=(t // tm,),
        in_specs=[tok(512), tok(256), tok(256), tok(D_MODEL),
                  pl.BlockSpec((1, 512), lambda i: (0, 0)), pl.BlockSpec((1, 256), lambda i: (0, 0)),
                  pl.BlockSpec((D_MODEL, D_MODEL), lambda i: (0, 0))],
        out_specs=[tok(D_MODEL), tok(D_MODEL)],
        out_shape=[jax.ShapeDtypeStruct((t, D_MODEL), BF16), jax.ShapeDtypeStruct((t, D_MODEL), F32)],
        compiler_params=_cparams(("parallel",)),
    )(o_attn, y_hgrn, y_conv, x, aw, cw, w_out)


def _mix_out_bwd(dx1, w_out, o_attn, y_conv, aw, cw, tm):
    t = dx1.shape[0]

    def body(dx_ref, w_ref, o_ref, c_ref, aw_ref, cw_ref, do_ref, dh_ref, dc_ref, daw_ref, dcw_ref):
        dm = _dot_nt(dx_ref[...].astype(BF16), w_ref[...])
        do, daw = _rms_bwd(dm[:, 0:512], o_ref[...], aw_ref[...])
        dc, dcw = _rms_bwd(dm[:, 768:1024], c_ref[...], cw_ref[...])
        do_ref[...] = do
        dh_ref[...] = dm[:, 512:768]
        dc_ref[...] = dc

        @pl.when(pl.program_id(0) == 0)
        def _():
            daw_ref[...] = jnp.zeros_like(daw_ref)
            dcw_ref[...] = jnp.zeros_like(dcw_ref)

        daw_ref[...] += jnp.sum(daw, axis=0, keepdims=True)
        dcw_ref[...] += jnp.sum(dcw, axis=0, keepdims=True)

    def tok(w):
        return pl.BlockSpec((tm, w), lambda i: (i, 0))

    v512 = pl.BlockSpec((1, 512), lambda i: (0, 0))
    v256 = pl.BlockSpec((1, 256), lambda i: (0, 0))
    return pl.pallas_call(
        body, name="mix_out_bwd", grid=(t // tm,),
        in_specs=[tok(D_MODEL), pl.BlockSpec((D_MODEL, D_MODEL), lambda i: (0, 0)), tok(512), tok(256),
                  v512, v256],
        out_specs=[tok(512), tok(256), tok(256), v512, v256],
        out_shape=[jax.ShapeDtypeStruct((t, 512), F32), jax.ShapeDtypeStruct((t, 256), F32),
                   jax.ShapeDtypeStruct((t, 256), F32), jax.ShapeDtypeStruct((1, 512), F32),
                   jax.ShapeDtypeStruct((1, 256), F32)],
        compiler_params=_cparams(("arbitrary",)),
    )(dx1, w_out, o_attn, y_conv, aw, cw)


FF_BLOCKS = 4


def _ffn_fwd(x1, fw, wg, wu, wd, tm, ride=None):
    t = x1.shape[0]
    fb = wg.shape[1]
    nf = N_DEV // FF_BLOCKS
    rd = _ride_plan(ride)
    grid = (t // tm, nf)

    def body(*refs):
        x_ref, fw_ref, wg_ref, wu_ref, wd_ref = refs[:5]
        h_ref, g_ref, u_ref, a_ref, x2_ref = refs[5 + rd.n:10 + rd.n]
        acc_ref = refs[10 + 2 * rd.n]
        copies = rd.copies(refs[5:5 + rd.n], refs[10 + rd.n:10 + 2 * rd.n], refs[11 + 2 * rd.n:])
        step_id = _grid_step_id(grid)
        _ride_start(copies, step_id == 0)
        j = pl.program_id(1)

        @pl.when(j == 0)
        def _():
            xv = x_ref[...]
            r = lax.rsqrt(jnp.mean(xv * xv, axis=-1, keepdims=True) + EPS)
            h_ref[...] = (xv * r * fw_ref[...]).astype(BF16)
            acc_ref[...] = xv

        h = h_ref[...]
        out = None
        for c in range(FF_BLOCKS):
            g = _dot_nt(h, wg_ref[c])
            u = _dot_nt(h, wu_ref[c])
            a = (g * _sigmoid(g) * u).astype(BF16)
            g_ref[c] = g.astype(BF16)
            u_ref[c] = u.astype(BF16)
            a_ref[c] = a
            part = _dot(a, wd_ref[c])
            out = part if out is None else out + part
        acc_ref[...] += out

        @pl.when(j == nf - 1)
        def _():
            x2_ref[...] = acc_ref[...]

        _ride_wait(copies, step_id == (t // tm) * nf - 1)

    tok = pl.BlockSpec((tm, D_MODEL), lambda i, j: (i, 0))
    ffb = pl.BlockSpec((FF_BLOCKS, tm, fb), lambda i, j: (j, i, 0))
    ffs = jax.ShapeDtypeStruct((N_DEV, t, fb), BF16)
    return pl.pallas_call(
        body, name="ffn_fwd", grid=grid,
        in_specs=[tok, pl.BlockSpec((1, D_MODEL), lambda i, j: (0, 0)),
                  pl.BlockSpec((FF_BLOCKS, fb, D_MODEL), lambda i, j: (j, 0, 0)),
                  pl.BlockSpec((FF_BLOCKS, fb, D_MODEL), lambda i, j: (j, 0, 0)),
                  pl.BlockSpec((FF_BLOCKS, fb, D_MODEL), lambda i, j: (j, 0, 0))] + rd.in_specs,
        out_specs=[tok, ffb, ffb, ffb, tok] + rd.out_specs,
        out_shape=[jax.ShapeDtypeStruct((t, D_MODEL), BF16), ffs, ffs, ffs,
                   jax.ShapeDtypeStruct((t, D_MODEL), F32)] + rd.out_shape,
        scratch_shapes=[pltpu.VMEM((tm, D_MODEL), F32)] + rd.scratch,
        compiler_params=_cparams(("arbitrary", "arbitrary")),
    )(x1, fw, wg, wu, wd, *rd.srcs)


def _ffn_bwd(dx2, g, u, wg, wu, wd, x1, fw, tm, ride=None):
    t = dx2.shape[0]
    fb = wg.shape[1]
    nf = N_DEV // FF_BLOCKS
    rd = _ride_plan(ride)
    grid = (t // tm, nf)

    def body(*refs):
        dx_ref, g_ref, u_ref, wg_ref, wu_ref, wd_ref, x_ref, fw_ref = refs[:8]
        dg_ref, du_ref, dx1_ref, dfw_ref = refs[8 + rd.n:12 + rd.n]
        acc_ref = refs[12 + 2 * rd.n]
        copies = rd.copies(refs[8:8 + rd.n], refs[12 + rd.n:12 + 2 * rd.n], refs[13 + 2 * rd.n:])
        step_id = _grid_step_id(grid)
        _ride_start(copies, step_id == 0)
        i = pl.program_id(0)
        j = pl.program_id(1)
        dxb = dx_ref[...].astype(BF16)
        dh = None
        for c in range(FF_BLOCKS):
            da = _dot_nt(dxb, wd_ref[c])
            gv = g_ref[c].astype(F32)
            uv = u_ref[c].astype(F32)
            sg = _sigmoid(gv)
            dg = (da * uv * (sg * (1.0 + gv * (1.0 - sg)))).astype(BF16)
            du = (da * gv * sg).astype(BF16)
            dg_ref[c] = dg
            du_ref[c] = du
            part = _dot(dg, wg_ref[c]) + _dot(du, wu_ref[c])
            dh = part if dh is None else dh + part

        @pl.when(j == 0)
        def _():
            acc_ref[...] = dh

        @pl.when(j > 0)
        def _():
            acc_ref[...] += dh

        @pl.when((i == 0) & (j == 0))
        def _():
            dfw_ref[...] = jnp.zeros_like(dfw_ref)

        @pl.when(j == nf - 1)
        def _():
            dx, dfw = _rms_bwd(acc_ref[...], x_ref[...], fw_ref[...])
            dx1_ref[...] = dx_ref[...] + dx
            dfw_ref[...] += jnp.sum(dfw, axis=0, keepdims=True)

        _ride_wait(copies, step_id == (t // tm) * nf - 1)

    tok = pl.BlockSpec((tm, D_MODEL), lambda i, j: (i, 0))
    ffb = pl.BlockSpec((FF_BLOCKS, tm, fb), lambda i, j: (j, i, 0))
    ffs = jax.ShapeDtypeStruct((N_DEV, t, fb), BF16)
    vec = pl.BlockSpec((1, D_MODEL), lambda i, j: (0, 0))
    return pl.pallas_call(
        body, name="ffn_bwd", grid=grid,
        in_specs=[tok, ffb, ffb,
                  pl.BlockSpec((FF_BLOCKS, fb, D_MODEL), lambda i, j: (j, 0, 0)),
                  pl.BlockSpec((FF_BLOCKS, fb, D_MODEL), lambda i, j: (j, 0, 0)),
                  pl.BlockSpec((FF_BLOCKS, fb, D_MODEL), lambda i, j: (j, 0, 0)),
                  tok, vec] + rd.in_specs,
        out_specs=[ffb, ffb, tok, vec] + rd.out_specs,
        out_shape=[ffs, ffs, jax.ShapeDtypeStruct((t, D_MODEL), F32),
                   jax.ShapeDtypeStruct((1, D_MODEL), F32)] + rd.out_shape,
        scratch_shapes=[pltpu.VMEM((tm, D_MODEL), F32)] + rd.scratch,
        compiler_params=_cparams(("arbitrary", "arbitrary")),
    )(dx2, g, u, wg, wu, wd, x1, fw, *rd.srcs)


def _loss_grad(y, target, tm):
    t, d = y.shape

    def body(y_ref, t_ref, dy_ref, loss_ref):
        err = y_ref[...] - t_ref[...]
        dy_ref[...] = err * (1.0 / d)

        @pl.when(pl.program_id(0) == 0)
        def _():
            loss_ref[...] = jnp.zeros_like(loss_ref)

        part = jnp.sum(jnp.sum(err * err, axis=-1, keepdims=True), axis=0, keepdims=True)
        loss_ref[...] += part * (0.5 / d)

    tok = pl.BlockSpec((tm, d), lambda i: (i, 0))
    return pl.pallas_call(
        body, name="loss_grad", grid=(t // tm,),
        in_specs=[tok, tok],
        out_specs=[tok, pl.BlockSpec((1, 1), lambda i: (0, 0))],
        out_shape=[jax.ShapeDtypeStruct((t, d), F32), jax.ShapeDtypeStruct((1, 1), F32)],
        compiler_params=_cparams(("arbitrary",)),
    )(y, target)


def _tile(v, reps):
    return jnp.tile(v.reshape(1, -1), (1, reps))


class _LocalPlan:
    def __init__(self, wb):
        self.w = [{n: wb[n][l] for n in BIG_AXIS} for l in range(DEPTH)]

    def ride(self, kernel_name, l, grads=None):
        return None

    def done(self, kernel_name, l, outs):
        pass


def _local_step(x, target, p, plan):
    nb, s, d = x.shape
    t = nb * s
    tm = min(512, s)
    tq = min(1024, s)
    xf = x.reshape(t, d)
    cosq, sinq = _rope_tables(s)
    ones512 = _block_ones(512, HEAD_DIM)
    lbs = _lower_bounds(p["hgrn_lb_logits"].reshape(DEPTH, 2 * D_HGRN)).reshape(DEPTH, 2, D_HGRN)

    saved = []
    cur = xf
    wb = plan.w
    for l in range(DEPTH):
        qw = _tile(p["q_norm_w"][l], N_HEADS)
        kw = _tile(p["k_norm_w"][l], N_KV)
        gw = _tile(p["hgrn_gnorm_w"][l], 2)
        dww = jnp.pad(p["conv_dw_w"][l], ((0, 1), (0, 0)))
        pww = p["conv_pw_w"][l].astype(BF16)
        h0, proj = _rms_proj(cur, _row(p["mix_norm_w"][l]), wb[l]["w_in"], tm)
        qr, kd, vd, kdt, vdt = _qkv_prep(proj, cosq, sinq, qw, kw, ones512, s, tm)
        o_attn, lse, *rode = _attn_fwd(qr, kd, vdt, nb, s, tq, plan.ride("attn_fwd", l))
        plan.done("attn_fwd", l, rode)
        y_hgrn, osum, states, *rode = _hgrn_fwd(proj, lbs[l], gw, nb, s, plan.ride("hgrn_fwd", l))
        plan.done("hgrn_fwd", l, rode)
        y_conv = _conv_fwd(proj, dww, _row(p["conv_dw_b"][l]), _row(p["conv_ln_w"][l]),
                           _row(p["conv_ln_b"][l]), pww, _row(p["conv_pw_b"][l]), nb, s)
        mixed, x1 = _mix_out(o_attn, y_hgrn, y_conv, cur, _row(p["attn_out_norm_w"][l]),
                             _row(p["conv_out_norm_w"][l]), wb[l]["w_out"], tm)
        hf, g, u, a, x2, *rode = _ffn_fwd(x1, _row(p["ffn_norm_w"][l]), wb[l]["w_gate"], wb[l]["w_up"],
                                          wb[l]["w_down"], tm, plan.ride("ffn_fwd", l))
        plan.done("ffn_fwd", l, rode)
        saved.append(dict(x=cur, h0=h0, proj=proj, qr=qr, kd=kd, vd=vd, kdt=kdt, o_attn=o_attn, lse=lse,
                          osum=osum, states=states, y_conv=y_conv, mixed=mixed, x1=x1, hf=hf, g=g, u=u, a=a,
                          qw=qw, kw=kw, gw=gw, dww=dww, pww=pww))
        cur = x2

    dcur, loss = _loss_grad(cur, target.reshape(t, d), tm)

    grads = {k: [None] * DEPTH for k in WEIGHTS}
    dlb = [None] * DEPTH
    for l in reversed(range(DEPTH)):
        sv = saved[l]
        dg, du, dx1, dfw, *rode = _ffn_bwd(dcur, sv["g"], sv["u"], wb[l]["w_gate"], wb[l]["w_up"],
                                           wb[l]["w_down"], sv["x1"], _row(p["ffn_norm_w"][l]), tm,
                                           plan.ride("ffn_bwd", l, grads))
        plan.done("ffn_bwd", l, rode)
        grads["ffn_norm_w"][l] = dfw[0]
        grads["w_gate"][l] = _dw_ff(dg, sv["hf"], "dw_gate", tm)
        grads["w_up"][l] = _dw_ff(du, sv["hf"], "dw_up", tm)
        grads["w_down"][l] = _dw_ff(sv["a"], dcur, "dw_down", tm)
        do_attn, dy_hgrn, dy_conv, daw, dcw = _mix_out_bwd(
            dx1, wb[l]["w_out"], sv["o_attn"], sv["y_conv"], _row(p["attn_out_norm_w"][l]),
            _row(p["conv_out_norm_w"][l]), tm)
        grads["attn_out_norm_w"][l] = daw[0]
        grads["conv_out_norm_w"][l] = dcw[0]
        grads["w_out"][l] = _mm_tn(sv["mixed"], dx1, D_MODEL, "dw_out", tm)
        dq, dkd, dvd, *rode = _attn_bwd(sv["qr"], sv["kd"], sv["vd"], sv["kdt"], sv["o_attn"], sv["lse"], do_attn,
                                        nb, s, tq, plan.ride("attn_bwd", l, grads))
        plan.done("attn_bwd", l, rode)
        dqkv, dqw, dkw = _qkv_bwd(sv["proj"], dq, dkd, dvd, cosq, sinq, sv["qw"], sv["kw"], ones512, s, tm)
        grads["q_norm_w"][l] = dqw.reshape(N_HEADS, HEAD_DIM).sum(0)
        grads["k_norm_w"][l] = dkw.reshape(N_KV, HEAD_DIM).sum(0)
        dhq, dzf, dzb, dhi, dhg, dgw, dlb_l, *rode = _hgrn_bwd(sv["proj"], lbs[l], sv["gw"], sv["osum"],
                                                               sv["states"], dy_hgrn, nb, s,
                                                               plan.ride("hgrn_bwd", l, grads))
        plan.done("hgrn_bwd", l, rode)
        grads["hgrn_gnorm_w"][l] = dgw.reshape(nb * D_HGRN // HEAD_DIM, HEAD_DIM).sum(0)
        dlb[l] = dlb_l.sum(0)
        dab, ddww, ddwb, dlnw, dlnb, dpww, dpwb = _conv_bwd(
            sv["proj"], dy_conv, sv["dww"], _row(p["conv_dw_b"][l]), _row(p["conv_ln_w"][l]),
            _row(p["conv_ln_b"][l]), sv["pww"], nb, s)
        grads["conv_dw_w"][l] = ddww[:CONV_W]
        grads["conv_dw_b"][l] = ddwb[0]
        grads["conv_ln_w"][l] = dlnw[0]
        grads["conv_ln_b"][l] = dlnb[0]
        grads["conv_pw_w"][l] = dpww
        grads["conv_pw_b"][l] = dpwb[0]
        pieces = [dqkv, dhq, dzf, dzb, dhi, dhg, dab]
        grads["w_in"][l] = _dw_in(sv["h0"], pieces, tm)
        dcur, dnw, *rode = _proj_bwd(pieces, wb[l]["w_in"], sv["x"], _row(p["mix_norm_w"][l]), dx1, tm,
                                     plan.ride("proj_bwd", l, grads))
        plan.done("proj_bwd", l, rode)
        grads["mix_norm_w"][l] = dnw[0]

    dlog = _lower_bounds_bwd(p["hgrn_lb_logits"].reshape(DEPTH, 2 * D_HGRN),
                             jnp.stack(dlb).reshape(DEPTH, 2 * D_HGRN))
    out = {k: (v if k in BIG_AXIS else jnp.stack(v)) for k, v in grads.items() if k != "hgrn_lb_logits"}
    out["hgrn_lb_logits"] = dlog.reshape(DEPTH, 2, D_HGRN)
    return loss, dcur.reshape(nb, s, d), out


BIG_AXIS = {"w_in": 2, "w_out": 1, "w_gate": 2, "w_up": 2, "w_down": 1}
SMALL_SHARD_AXIS = {"hgrn_lb_logits": 2, "conv_dw_w": 2, "conv_pw_w": 1}
WEIGHTS = ("mix_norm_w", "w_in", "q_norm_w", "k_norm_w", "hgrn_lb_logits", "hgrn_gnorm_w", "conv_dw_w",
           "conv_dw_b", "conv_ln_w", "conv_ln_b", "conv_pw_w", "conv_pw_b", "attn_out_norm_w",
           "conv_out_norm_w", "w_out", "ffn_norm_w", "w_gate", "w_up", "w_down")
SMALL = tuple(n for n in WEIGHTS if n not in BIG_AXIS)


def _my_index():
    return 4 * lax.axis_index("x") + 2 * lax.axis_index("y") + lax.axis_index("c")


class _RidePlan:
    def __init__(self, srcs, gather):
        self.srcs = list(srcs)
        self.n = len(self.srcs)
        self.gather = list(gather) if isinstance(gather, (list, tuple)) else [gather] * self.n
        any_spec = pl.BlockSpec(memory_space=pl.ANY)
        self.in_specs = [any_spec] * self.n
        self.out_specs = [any_spec] * self.n
        self.out_shape = [jax.ShapeDtypeStruct(((N_DEV,) + s.shape) if g else s.shape, s.dtype)
                          for s, g in zip(self.srcs, self.gather)]
        npeer = N_DEV - 1
        self.scratch = [pltpu.SemaphoreType.DMA((self.n * npeer,)), pltpu.SemaphoreType.DMA((self.n * npeer,)),
                        pltpu.SemaphoreType.DMA((self.n,))] if self.n else []

    def copies(self, src_refs, out_refs, sems):
        if not self.n:
            return [], [], []
        send_sems, recv_sems, local_sems = sems
        npeer = N_DEV - 1
        x, y, c = lax.axis_index("x"), lax.axis_index("y"), lax.axis_index("c")
        me = 4 * x + 2 * y + c
        locals_, sends, recvs = [], [], []
        for a in range(self.n):
            src_ref, out_ref = src_refs[a], out_refs[a]

            def rows_for(j, src_ref=src_ref, gather=self.gather[a]):
                return src_ref if gather else src_ref.at[j]

            locals_.append(pltpu.make_async_copy(rows_for(me), out_ref.at[me], local_sems.at[a]))
            for k in range(1, N_DEV):
                px = (1 - x) if (k & 4) else x
                py = (1 - y) if (k & 2) else y
                pc = (1 - c) if (k & 1) else c
                pidx = 4 * px + 2 * py + pc
                common = dict(send_sem=send_sems.at[a * npeer + k - 1], recv_sem=recv_sems.at[a * npeer + k - 1],
                              device_id=(px, py, pc), device_id_type=pl.DeviceIdType.MESH)
                sends.append(pltpu.make_async_remote_copy(src_ref=rows_for(pidx), dst_ref=out_ref.at[me], **common))
                recvs.append(pltpu.make_async_remote_copy(src_ref=rows_for(pidx), dst_ref=out_ref.at[pidx],
                                                          **common))
        return locals_, sends, recvs


def _ride_plan(ride):
    return _RidePlan(*ride) if ride else _RidePlan([], True)


def _ride_start(copies, when=None):
    locals_, sends, _ = copies

    def go():
        for cp in locals_ + sends:
            cp.start()

    if locals_:
        go() if when is None else pl.when(when)(go)


def _ride_wait(copies, when=None):
    locals_, sends, recvs = copies

    def go():
        for cp in recvs:
            cp.wait_recv()
        for cp in sends:
            cp.wait_send()
        for cp in locals_:
            cp.wait()

    if locals_:
        go() if when is None else pl.when(when)(go)


def _exchange(srcs, gather, name):
    rd = _RidePlan(srcs, gather)

    def body(*refs):
        copies = rd.copies(refs[:rd.n], refs[rd.n:2 * rd.n], refs[2 * rd.n:])
        _ride_start(copies)
        _ride_wait(copies)

    return pl.pallas_call(body, name=name, in_specs=rd.in_specs, out_specs=rd.out_specs,
                          out_shape=rd.out_shape, scratch_shapes=rd.scratch)(*srcs)


def _adamw_math(w, g, m, v):
    m = ADAM_B1 * m + (1.0 - ADAM_B1) * g
    v = ADAM_B2 * v + (1.0 - ADAM_B2) * (g * g)
    m_hat = m / (1.0 - ADAM_B1 ** ADAM_STEP)
    v_hat = v / (1.0 - ADAM_B2 ** ADAM_STEP)
    delta = -ADAM_LR * (m_hat / (jnp.sqrt(v_hat) + ADAM_EPS) + ADAM_WD * w)
    return delta, m, v


def _sum_adamw(parts, w, m, v, name):
    _, k, n = w.shape
    tk = k
    for cand in (256, 176, 160, 128):
        if k % cand == 0:
            tk = cand
            break

    def body(*refs):
        p_refs = refs[:DEPTH]
        w_ref, m_ref, v_ref, g_ref, d_ref, mo_ref, vo_ref = refs[DEPTH:]
        for l in range(DEPTH):
            @pl.when(pl.program_id(0) == l)
            def _(p_ref=p_refs[l]):
                g = p_ref[0].astype(F32)
                for i in range(1, N_DEV):
                    g = g + p_ref[i].astype(F32)
                g_ref[...] = g
                d_ref[...], mo_ref[...], vo_ref[...] = _adamw_math(w_ref[...], g, m_ref[...], v_ref[...])

    row = pl.BlockSpec((None, tk, n), lambda l, i: (l, i, 0))
    shp = jax.ShapeDtypeStruct(w.shape, F32)
    return pl.pallas_call(
        body, name=name, grid=(DEPTH, k // tk),
        in_specs=[pl.BlockSpec((N_DEV, tk, n), lambda l, i: (0, i, 0))] * DEPTH + [row, row, row],
        out_specs=[row, row, row, row],
        out_shape=[shp, shp, shp, shp],
        compiler_params=_cparams(("parallel", "parallel")),
    )(*parts, w, m, v)


def _sum8(parts, name):
    r = parts.shape[1]

    def body(p_ref, g_ref):
        g = p_ref[0]
        for i in range(1, N_DEV):
            g = g + p_ref[i]
        g_ref[...] = g

    return pl.pallas_call(body, name=name, out_shape=jax.ShapeDtypeStruct((r, LANES), F32))(parts)


def _adamw(w, g, m, v):
    def body(w_ref, g_ref, m_ref, v_ref, d_ref, mo_ref, vo_ref):
        d_ref[...], mo_ref[...], vo_ref[...] = _adamw_math(w_ref[...], g_ref[...], m_ref[...], v_ref[...])

    shp = jax.ShapeDtypeStruct(w.shape, F32)
    return pl.pallas_call(body, name="adamw_small", out_shape=[shp, shp, shp])(w, g, m, v)


def _pack(arrays, dtype, row_multiple):
    flat = jnp.concatenate([a.reshape(-1).astype(dtype) for a in arrays])
    n = flat.shape[0]
    unit = row_multiple * LANES
    total = -(-n // unit) * unit
    return jnp.pad(flat, (0, total - n)).reshape(total // LANES, LANES)


def _unpack(flat2d, shapes, lead=()):
    flat = flat2d.reshape(lead + (-1,))
    out, off = [], 0
    for shp in shapes:
        n = int(np.prod(shp))
        out.append(flat[..., off:off + n].reshape(lead + tuple(shp)))
        off += n
    return out


def _shard_to_rows(full, axis):
    shp = full.shape
    k = shp[axis] // N_DEV
    r = full.reshape(shp[:axis] + (N_DEV, k) + shp[axis + 1:])
    return jnp.moveaxis(r, axis, 0)


def _rows_to_full(rows, axis):
    r = jnp.moveaxis(rows, 0, axis)
    shp = r.shape
    return r.reshape(shp[:axis] + (shp[axis] * shp[axis + 1],) + shp[axis + 2:])


def kernel(x, mix_norm_w, w_in, q_norm_w, k_norm_w, hgrn_lb_logits, hgrn_gnorm_w, conv_dw_w, conv_dw_b, conv_ln_w, conv_ln_b, conv_pw_w, conv_pw_b, attn_out_norm_w, conv_out_norm_w, w_out, ffn_norm_w, w_gate, w_up, w_down, loss_target, m_mix_norm_w, m_w_in, m_q_norm_w, m_k_norm_w, m_hgrn_lb_logits, m_hgrn_gnorm_w, m_conv_dw_w, m_conv_dw_b, m_conv_ln_w, m_conv_ln_b, m_conv_pw_w, m_conv_pw_b, m_attn_out_norm_w, m_conv_out_norm_w, m_w_out, m_ffn_norm_w, m_w_gate, m_w_up, m_w_down, v_mix_norm_w, v_w_in, v_q_norm_w, v_k_norm_w, v_hgrn_lb_logits, v_hgrn_gnorm_w, v_conv_dw_w, v_conv_dw_b, v_conv_ln_w, v_conv_ln_b, v_conv_pw_w, v_conv_pw_b, v_attn_out_norm_w, v_conv_out_norm_w, v_w_out, v_ffn_norm_w, v_w_gate, v_w_up, v_w_down):
    w_loc = dict(zip(WEIGHTS, (mix_norm_w, w_in, q_norm_w, k_norm_w, hgrn_lb_logits, hgrn_gnorm_w, conv_dw_w,
                               conv_dw_b, conv_ln_w, conv_ln_b, conv_pw_w, conv_pw_b, attn_out_norm_w,
                               conv_out_norm_w, w_out, ffn_norm_w, w_gate, w_up, w_down)))
    m_loc = dict(zip(WEIGHTS, (m_mix_norm_w, m_w_in, m_q_norm_w, m_k_norm_w, m_hgrn_lb_logits, m_hgrn_gnorm_w,
                               m_conv_dw_w, m_conv_dw_b, m_conv_ln_w, m_conv_ln_b, m_conv_pw_w, m_conv_pw_b,
                               m_attn_out_norm_w, m_conv_out_norm_w, m_w_out, m_ffn_norm_w, m_w_gate, m_w_up,
                               m_w_down)))
    v_loc = dict(zip(WEIGHTS, (v_mix_norm_w, v_w_in, v_q_norm_w, v_k_norm_w, v_hgrn_lb_logits, v_hgrn_gnorm_w,
                               v_conv_dw_w, v_conv_dw_b, v_conv_ln_w, v_conv_ln_b, v_conv_pw_w, v_conv_pw_b,
                               v_attn_out_norm_w, v_conv_out_norm_w, v_w_out, v_ffn_norm_w, v_w_gate, v_w_up,
                               v_w_down)))
    me = _my_index()
    big = tuple(BIG_AXIS)
    sms = tuple(SMALL_SHARD_AXIS)

    col_sharded = tuple(n for n in big if BIG_AXIS[n] == 2)

    def shard_t(n, a):
        return jnp.swapaxes(a, 1, 2) if n in col_sharded else a

    w_send = {n: shard_t(n, w_loc[n]).astype(BF16) for n in big}

    sm_shapes = [w_loc[n].shape for n in sms]
    got_s, got_w_in0 = _exchange([_pack([w_loc[n] for n in sms], F32, 8), w_send["w_in"][0]], True,
                                 "gather_first")
    p_full = {n: w_loc[n] for n in SMALL if n not in SMALL_SHARD_AXIS}
    for n, a in zip(sms, _unpack(got_s, sm_shapes, (N_DEV,))):
        p_full[n] = _rows_to_full(a, SMALL_SHARD_AXIS[n])

    def natural(n, gathered):
        if n in ("w_in", "w_out"):
            return gathered.reshape(-1, gathered.shape[-1])
        return gathered

    def to_send(n, gl):
        if n in ("w_in", "w_out"):
            return gl.reshape(N_DEV, gl.shape[0] // N_DEV, gl.shape[1]).astype(BF16)
        return gl

    class StepPlan:
        def __init__(self):
            self.w = [dict() for _ in range(DEPTH)]
            self.parts = [dict() for _ in range(DEPTH)]
            self.pending = {}
            self.w[0]["w_in"] = natural("w_in", got_w_in0)

        def ride(self, kernel_name, l, grads=None):
            want = []
            if kernel_name == "attn_fwd":
                want = [("w_out", l), ("w_gate", l)]
            elif kernel_name == "hgrn_fwd":
                want = [("w_up", l), ("w_down", l)]
            elif kernel_name == "ffn_fwd" and l + 1 < DEPTH:
                want = [("w_in", l + 1)]
            elif kernel_name == "ffn_bwd" and l + 1 < DEPTH:
                want = [("w_gate", l + 1), ("w_up", l + 1)]
            elif kernel_name == "attn_bwd" and l + 1 < DEPTH:
                want = [("w_in", l + 1), ("w_out", l + 1), ("w_down", l + 1)]
                if l == 0:
                    want += [("w_out", 0)]
            elif kernel_name == "hgrn_bwd" and l == 0:
                want = [(n, 0) for n in ("w_gate", "w_up", "w_down")]
            elif kernel_name == "proj_bwd" and l == 0:
                want = [("w_in", 0)]
            if not want:
                return None
            self.pending[(kernel_name, l)] = want
            if grads is None:
                return [w_send[n][wl] for n, wl in want], True
            return [to_send(n, grads[n][wl]) for n, wl in want], False

        def done(self, kernel_name, l, outs):
            want = self.pending.pop((kernel_name, l), [])
            for (n, wl), out in zip(want, outs):
                if kernel_name.endswith("_fwd"):
                    self.w[wl][n] = natural(n, out)
                else:
                    self.parts[wl][n] = out

    plan = StepPlan()
    loss_part, grad_x, g = _local_step(x, loss_target, p_full, plan)
    loss = lax.psum(loss_part[0, 0], MESH_AXES)

    pw = g["conv_pw_w"]
    k_pw = w_loc["conv_pw_w"].shape[1]
    pw_send = jnp.moveaxis(pw.reshape(DEPTH, N_DEV, k_pw, pw.shape[-1]), 1, 0).reshape(N_DEV, -1, LANES)
    gathered_small = [n for n in SMALL if n != "conv_pw_w"]
    small_shapes = [g[n].shape for n in gathered_small]
    small_parts, pw_parts = _exchange([_pack([g[n] for n in gathered_small], F32, 8), pw_send],
                                      [True, False], "exchange_small_grads")
    big_out = {}
    for n in big:
        res = _sum_adamw([plan.parts[l][n] for l in range(DEPTH)], shard_t(n, w_loc[n]), shard_t(n, m_loc[n]),
                         shard_t(n, v_loc[n]), "sum_adamw_" + n)
        big_out[n] = [shard_t(n, r) for r in res]

    g_small = dict(zip(gathered_small, _unpack(_sum8(small_parts, "sum_small_grads"), small_shapes)))
    g_small["conv_pw_w"] = _sum8(pw_parts, "sum_conv_pw_grads").reshape(w_loc["conv_pw_w"].shape)
    for n in sms:
        if n == "conv_pw_w":
            continue
        ax = SMALL_SHARD_AXIS[n]
        k = w_loc[n].shape[ax]
        g_small[n] = lax.dynamic_slice_in_dim(g_small[n], me * k, k, axis=ax)
    loc_shapes = [w_loc[n].shape for n in SMALL]
    packed = [_pack([d[n] for n in SMALL], F32, 8) for d in (w_loc, g_small, m_loc, v_loc)]
    res = _adamw(*packed)
    small_out = [g_small] + [dict(zip(SMALL, _unpack(r, loc_shapes))) for r in res]

    def pick(i, n):
        return big_out[n][i] if n in BIG_AXIS else small_out[i][n]

    return (loss, grad_x) + tuple(pick(i, n) for i in range(4) for n in WEIGHTS)
```

```python
import functools

import jax
import jax.numpy as jnp
import numpy as np
from jax import lax
from jax.experimental import pallas as pl
from jax.experimental.pallas import tpu as pltpu

F32 = jnp.float32
BF16 = jnp.bfloat16

D_MODEL = 1024
D_ATTN = 512
D_HGRN = 256
D_CONV = 256
HEAD_DIM = 64
N_HEADS = 8
N_KV = 2
GRID_W = 64
ROPE_THETA = 10000.0
CHUNK = 64
F_MIN = 1e-6
CONV_W = 31
CONV_PAD = 15
D_FF = 2816
D_PROJ = 2560
EPS = 1e-6
LN_EPS = 1e-5
DEPTH = 2
ADAM_LR = 0.001
ADAM_B1 = 0.9
ADAM_B2 = 0.999
ADAM_EPS = 1e-08
ADAM_WD = 0.01
ADAM_STEP = 10
N_DEV = 8
MESH_AXES = ("x", "y", "c")

COL_HQ, COL_ZFW, COL_ZBW, COL_HI, COL_HG = 6, 8, 10, 12, 14
COL_CA, COL_CB = 8, 9

LANES = 128
VMEM_LIMIT_MB = 56


def _cparams(dims=None):
    return pltpu.CompilerParams(dimension_semantics=dims, vmem_limit_bytes=VMEM_LIMIT_MB * 2 ** 20)


def _dot(a, b):
    return jnp.dot(a, b, preferred_element_type=F32)


def _dot_nt(a, b):
    return lax.dot_general(a, b, (((1,), (1,)), ((), ())), preferred_element_type=F32)


def _dot_tn(a, b):
    return lax.dot_general(a, b, (((0,), (0,)), ((), ())), preferred_element_type=F32)


def _split_bf16(x, parts):
    out = []
    r = x
    for _ in range(parts):
        p = r.astype(BF16)
        out.append(p)
        r = r - p.astype(F32)
    return out


def _dot_precise(x, m_bf16, parts=3):
    acc = None
    for p in _split_bf16(x, parts):
        t = _dot(p, m_bf16)
        acc = t if acc is None else acc + t
    return acc


def _block_ones(width, group):
    i = np.arange(width)
    return jnp.asarray((i[:, None] // group) == (i[None, :] // group), dtype=BF16)


def _sigmoid(x):
    return 1.0 / (1.0 + jnp.exp(-x))


def _rot(x):
    w = x.shape[1]
    lane = lax.broadcasted_iota(jnp.int32, x.shape, 1)
    first = (lane % 32) < 16
    return jnp.where(first, -pltpu.roll(x, w - 16, 1), pltpu.roll(x, 16, 1))


def _rope(x, cos, sin):
    return x * cos + _rot(x) * sin


def _rope_t(dy, cos, sin):
    return dy * cos - _rot(dy * sin)


def _row(v):
    return v.reshape(1, -1)


def _rms_proj(x, wn, wt, tm):
    t, d = x.shape
    n = wt.shape[0]

    def body(x_ref, wn_ref, w_ref, h_ref, y_ref):
        xv = x_ref[...]
        r = lax.rsqrt(jnp.mean(xv * xv, axis=-1, keepdims=True) + EPS)
        h = (xv * r * wn_ref[...]).astype(BF16)
        h_ref[...] = h
        y_ref[...] = _dot_nt(h, w_ref[...])

    return pl.pallas_call(
        body, name="rms_proj", grid=(t // tm,),
        in_specs=[pl.BlockSpec((tm, d), lambda i: (i, 0)),
                  pl.BlockSpec((1, d), lambda i: (0, 0)),
                  pl.BlockSpec((n, d), lambda i: (0, 0))],
        out_specs=[pl.BlockSpec((tm, d), lambda i: (i, 0)),
                   pl.BlockSpec((tm, n), lambda i: (i, 0))],
        out_shape=[jax.ShapeDtypeStruct((t, d), BF16), jax.ShapeDtypeStruct((t, n), F32)],
        compiler_params=_cparams(("parallel",)),
    )(x, wn, wt)


def _rms_bwd(dh, x, wn):
    r = lax.rsqrt(jnp.mean(x * x, axis=-1, keepdims=True) + EPS)
    g = dh * wn
    dx = r * (g - x * (r * r) * jnp.mean(g * x, axis=-1, keepdims=True))
    return dx, dh * x * r


def _proj_bwd(pieces, wt, x, wn, dres, tm, ride=None):
    t = x.shape[0]
    d = x.shape[1]
    n = wt.shape[0]
    widths = [p.shape[1] for p in pieces]
    offs = [sum(widths[:i]) for i in range(len(widths))]
    assert sum(widths) == n
    npc = len(pieces)
    rd = _ride_plan(ride)

    def body(*refs):
        p_refs = refs[:npc]
        w_ref, x_ref, wn_ref, dr_ref = refs[npc:npc + 4]
        dx_ref, dwn_ref = refs[npc + 4 + rd.n:npc + 6 + rd.n]
        copies = rd.copies(refs[npc + 4:npc + 4 + rd.n], refs[npc + 6 + rd.n:npc + 6 + 2 * rd.n],
                           refs[npc + 6 + 2 * rd.n:])
        _ride_start(copies, pl.program_id(0) == 0)
        dh = None
        for p_ref, o, wd in zip(p_refs, offs, widths):
            part = _dot(p_ref[...], w_ref[o:o + wd, :])
            dh = part if dh is None else dh + part
        dx, dwn = _rms_bwd(dh, x_ref[...], wn_ref[...])
        dx_ref[...] = dr_ref[...] + dx

        @pl.when(pl.program_id(0) == 0)
        def _():
            dwn_ref[...] = jnp.zeros_like(dwn_ref)

        dwn_ref[...] += jnp.sum(dwn, axis=0, keepdims=True)
        _ride_wait(copies, pl.program_id(0) == t // tm - 1)

    return pl.pallas_call(
        body, name="proj_bwd", grid=(t // tm,),
        in_specs=[pl.BlockSpec((tm, wd), lambda i: (i, 0)) for wd in widths]
        + [pl.BlockSpec((n, d), lambda i: (0, 0)),
           pl.BlockSpec((tm, d), lambda i: (i, 0)),
           pl.BlockSpec((1, d), lambda i: (0, 0)),
           pl.BlockSpec((tm, d), lambda i: (i, 0))] + rd.in_specs,
        out_specs=[pl.BlockSpec((tm, d), lambda i: (i, 0)),
                   pl.BlockSpec((1, d), lambda i: (0, 0))] + rd.out_specs,
        out_shape=[jax.ShapeDtypeStruct((t, d), F32), jax.ShapeDtypeStruct((1, d), F32)] + rd.out_shape,
        scratch_shapes=rd.scratch,
        compiler_params=_cparams(("arbitrary",)),
    )(*pieces, wt, x, wn, dres, *rd.srcs)


def _dw_in(h0, pieces, tm):
    t, k = h0.shape
    widths = [p.shape[1] for p in pieces]
    offs = [sum(widths[:i]) for i in range(len(widths))]
    n = sum(widths)
    npc = len(pieces)

    def body(*refs):
        h_ref = refs[0]
        p_refs = refs[1:1 + npc]
        o_ref, acc_ref = refs[1 + npc:]
        i = pl.program_id(0)

        @pl.when(i == 0)
        def _():
            acc_ref[...] = jnp.zeros_like(acc_ref)

        h = h_ref[...]
        for p_ref, o, wd in zip(p_refs, offs, widths):
            acc_ref[o:o + wd, :] += _dot_tn(p_ref[...], h)

        @pl.when(i == t // tm - 1)
        def _():
            o_ref[...] = acc_ref[...].astype(BF16)

    return pl.pallas_call(
        body, name="dw_in", grid=(t // tm,),
        in_specs=[pl.BlockSpec((tm, k), lambda i: (i, 0))]
        + [pl.BlockSpec((tm, wd), lambda i: (i, 0)) for wd in widths],
        out_specs=pl.BlockSpec((n, k), lambda i: (0, 0)),
        out_shape=jax.ShapeDtypeStruct((n, k), BF16),
        scratch_shapes=[pltpu.VMEM((n, k), F32)],
        compiler_params=_cparams(("arbitrary",)),
    )(h0, *pieces)


def _mm_tn(a, b, tn, name, tm):
    t, k = a.shape
    n = b.shape[1]

    def body(a_ref, b_ref, o_ref):
        @pl.when(pl.program_id(1) == 0)
        def _():
            o_ref[...] = jnp.zeros_like(o_ref)

        o_ref[...] += _dot_tn(a_ref[...].astype(BF16), b_ref[...].astype(BF16))

    return pl.pallas_call(
        body, name=name, grid=(n // tn, t // tm),
        in_specs=[pl.BlockSpec((tm, k), lambda j, i: (i, 0)),
                  pl.BlockSpec((tm, tn), lambda j, i: (i, j))],
        out_specs=pl.BlockSpec((k, tn), lambda j, i: (0, j)),
        out_shape=jax.ShapeDtypeStruct((k, n), F32),
        compiler_params=_cparams(("parallel", "arbitrary")),
    )(a, b)


def _dw_ff(blocked, flat, name, tm):
    t, dm = flat.shape
    fb = blocked.shape[2]
    out_blk = (N_DEV, fb, dm)

    def body(b_ref, f_ref, o_ref, acc_ref):
        i = pl.program_id(0)

        @pl.when(i == 0)
        def _():
            acc_ref[...] = jnp.zeros_like(acc_ref)

        fv = f_ref[...].astype(BF16)
        for j in range(N_DEV):
            acc_ref[j] += _dot_tn(b_ref[j], fv)

        @pl.when(i == t // tm - 1)
        def _():
            o_ref[...] = acc_ref[...].astype(BF16)

    return pl.pallas_call(
        body, name=name, grid=(t // tm,),
        in_specs=[pl.BlockSpec((N_DEV, tm, fb), lambda i: (0, i, 0)),
                  pl.BlockSpec((tm, dm), lambda i: (i, 0))],
        out_specs=pl.BlockSpec(out_blk, lambda i: (0, 0, 0)),
        out_shape=jax.ShapeDtypeStruct(out_blk, BF16),
        scratch_shapes=[pltpu.VMEM(out_blk, F32)],
        compiler_params=_cparams(("arbitrary",)),
    )(blocked, flat)


def _rope_tables(s):
    rows = s // GRID_W
    row_id = jnp.repeat(jnp.arange(rows, dtype=F32), GRID_W)
    col_id = jnp.tile(jnp.arange(GRID_W, dtype=F32), rows)
    half = HEAD_DIM // 2
    inv_freq = ROPE_THETA ** (-jnp.arange(0, half, 2, dtype=F32) / half)
    ang_r = row_id[:, None] * inv_freq[None, :]
    ang_c = col_id[:, None] * inv_freq[None, :]
    ang = jnp.concatenate([ang_r, ang_r, ang_c, ang_c], axis=-1)
    cos, sin = jnp.cos(ang), jnp.sin(ang)
    return jnp.tile(cos, (1, LANES // HEAD_DIM)), jnp.tile(sin, (1, LANES // HEAD_DIM))


def _head_rms(x, w, ones):
    r = lax.rsqrt(_dot_precise(x * x, ones, 2) * (1.0 / HEAD_DIM) + EPS)
    return x * r * w, r


def _dup_half(x, kv):
    lane = lax.broadcasted_iota(jnp.int32, x.shape, 1)
    sel = (lane < 64) if kv == 0 else (lane >= 64)
    return jnp.where(sel, x, pltpu.roll(x, 64, 1))


def _qkv_prep(proj, cosq, sinq, qw, kw, ones, s, tm):
    t = proj.shape[0]
    ns = s // tm

    def body(p_ref, cos_ref, sin_ref, qw_ref, kw_ref, ones_ref, q_out, kd_out, vd_out, kdt_out, vdt_out):
        cos = jnp.tile(cos_ref[...], (1, D_ATTN // LANES))
        sin = jnp.tile(sin_ref[...], (1, D_ATTN // LANES))
        ones_m = ones_ref[...]
        qn, _ = _head_rms(p_ref[:, 0:512], qw_ref[...], ones_m)
        q_out[...] = (_rope(qn, cos, sin) * (HEAD_DIM ** -0.5)).astype(BF16)
        kn, _ = _head_rms(p_ref[:, 512:640], kw_ref[...], ones_m[0:128, 0:128])
        kr = _rope(kn, cos[:, 0:128], sin[:, 0:128])
        v = p_ref[:, 640:768]
        for kv in range(N_KV):
            kd = _dup_half(kr, kv)
            vd = _dup_half(v, kv)
            kd_out[kv] = kd.astype(BF16)
            vd_out[kv] = vd.astype(BF16)
            kdt_out[kv] = kd.T.astype(BF16)
            vdt_out[kv] = vd.T.astype(BF16)

    return pl.pallas_call(
        body, name="qkv_prep", grid=(t // tm,),
        in_specs=[pl.BlockSpec((tm, 768), lambda i: (i, 0)),
                  pl.BlockSpec((tm, LANES), lambda i: (i % ns, 0)),
                  pl.BlockSpec((tm, LANES), lambda i: (i % ns, 0)),
                  pl.BlockSpec((1, 512), lambda i: (0, 0)),
                  pl.BlockSpec((1, 128), lambda i: (0, 0)),
                  pl.BlockSpec((512, 512), lambda i: (0, 0))],
        out_specs=[pl.BlockSpec((tm, 512), lambda i: (i, 0)),
                   pl.BlockSpec((N_KV, tm, 128), lambda i: (0, i, 0)),
                   pl.BlockSpec((N_KV, tm, 128), lambda i: (0, i, 0)),
                   pl.BlockSpec((N_KV, 128, tm), lambda i: (0, 0, i)),
                   pl.BlockSpec((N_KV, 128, tm), lambda i: (0, 0, i))],
        out_shape=[jax.ShapeDtypeStruct((t, 512), BF16),
                   jax.ShapeDtypeStruct((N_KV, t, 128), BF16),
                   jax.ShapeDtypeStruct((N_KV, t, 128), BF16),
                   jax.ShapeDtypeStruct((N_KV, 128, t), BF16),
                   jax.ShapeDtypeStruct((N_KV, 128, t), BF16)],
        compiler_params=_cparams(("parallel",)),
    )(proj, cosq, sinq, qw, kw, ones)


def _qkv_bwd(proj, dq, dkd, dvd, cosq, sinq, qw, kw, ones, s, tm):
    t = proj.shape[0]
    ns = s // tm

    def body(p_ref, dq_ref, dkd_ref, dvd_ref, cos_ref, sin_ref, qw_ref, kw_ref, ones_ref,
             out_ref, dqw_ref, dkw_ref):
        cos = jnp.tile(cos_ref[...], (1, D_ATTN // LANES))
        sin = jnp.tile(sin_ref[...], (1, D_ATTN // LANES))
        ones_m = ones_ref[...]
        ones_k = ones_m[0:128, 0:128]

        def norm_bwd(x, w, dn, om):
            r = lax.rsqrt(_dot_precise(x * x, om, 2) * (1.0 / HEAD_DIM) + EPS)
            g = dn * w
            dx = r * (g - x * (r * r) * (_dot_precise(g * x, om, 2) * (1.0 / HEAD_DIM)))
            return dx, jnp.sum(dn * x * r, axis=0, keepdims=True)

        q = p_ref[:, 0:512]
        dqn = _rope_t(dq_ref[...], cos, sin) * (HEAD_DIM ** -0.5)
        dq_raw, dqw = norm_bwd(q, qw_ref[...], dqn, ones_m)
        out_ref[:, 0:512] = dq_raw.astype(BF16)

        lane = lax.broadcasted_iota(jnp.int32, (tm, 128), 1)

        def fold(ref):
            a0 = ref[0]
            a1 = ref[1]
            f0 = a0 + pltpu.roll(a0, 64, 1)
            f1 = a1 + pltpu.roll(a1, 64, 1)
            return jnp.where(lane < 64, f0, f1)

        k = p_ref[:, 512:640]
        dkn = _rope_t(fold(dkd_ref), cos[:, 0:128], sin[:, 0:128])
        dk_raw, dkw = norm_bwd(k, kw_ref[...], dkn, ones_k)
        out_ref[:, 512:640] = dk_raw.astype(BF16)
        out_ref[:, 640:768] = fold(dvd_ref).astype(BF16)

        @pl.when(pl.program_id(0) == 0)
        def _():
            dqw_ref[...] = jnp.zeros_like(dqw_ref)
            dkw_ref[...] = jnp.zeros_like(dkw_ref)

        dqw_ref[...] += dqw
        dkw_ref[...] += dkw

    return pl.pallas_call(
        body, name="qkv_bwd", grid=(t // tm,),
        in_specs=[pl.BlockSpec((tm, 768), lambda i: (i, 0)),
                  pl.BlockSpec((tm, 512), lambda i: (i, 0)),
                  pl.BlockSpec((N_KV, tm, 128), lambda i: (0, i, 0)),
                  pl.BlockSpec((N_KV, tm, 128), lambda i: (0, i, 0)),
                  pl.BlockSpec((tm, LANES), lambda i: (i % ns, 0)),
                  pl.BlockSpec((tm, LANES), lambda i: (i % ns, 0)),
                  pl.BlockSpec((1, 512), lambda i: (0, 0)),
                  pl.BlockSpec((1, 128), lambda i: (0, 0)),
                  pl.BlockSpec((512, 512), lambda i: (0, 0))],
        out_specs=[pl.BlockSpec((tm, 768), lambda i: (i, 0)),
                   pl.BlockSpec((1, 512), lambda i: (0, 0)),
                   pl.BlockSpec((1, 128), lambda i: (0, 0))],
        out_shape=[jax.ShapeDtypeStruct((t, 768), BF16),
                   jax.ShapeDtypeStruct((1, 512), F32),
                   jax.ShapeDtypeStruct((1, 128), F32)],
        compiler_params=_cparams(("arbitrary",)),
    )(proj, dq, dkd, dvd, cosq, sinq, qw, kw, ones)


def _grid_step_id(grid):
    idx = pl.program_id(0)
    for ax in range(1, len(grid)):
        idx = idx * grid[ax] + pl.program_id(ax)
    return idx


def _attn_fwd(q, kd, vdt, nb, s, tq, ride=None):
    t = q.shape[0]
    nq = s // tq
    rd = _ride_plan(ride)
    grid = (nb, N_HEADS // 2, nq)
    nsteps = nb * (N_HEADS // 2) * nq

    def body(*refs):
        q_ref, k_ref, vt_ref = refs[:3]
        o_ref, lse_ref = refs[3 + rd.n:5 + rd.n]
        copies = rd.copies(refs[3:3 + rd.n], refs[5 + rd.n:5 + 2 * rd.n], refs[5 + 2 * rd.n:])
        step_id = _grid_step_id(grid)
        _ride_start(copies, step_id == 0)
        qv = q_ref[...].astype(F32)
        lane = lax.broadcasted_iota(jnp.int32, qv.shape, 1)
        k = k_ref[0]
        vt = vt_ref[0]
        outs = []
        scores = [_dot_nt(k, jnp.where((lane < 64) if half == 0 else (lane >= 64), qv, 0.0).astype(BF16))
                  for half in range(2)]
        for half in range(2):
            st = scores[half]
            m = jnp.max(st, axis=0, keepdims=True)
            p = jnp.exp(st - m)
            l = jnp.sum(p, axis=0, keepdims=True)
            ot = _dot(vt, p.astype(BF16)) / l
            lse_ref[0, half] = m + jnp.log(l)
            outs.append(ot)
        row = lax.broadcasted_iota(jnp.int32, outs[0].shape, 0)
        o_ref[...] = jnp.where(row < 64, outs[0], outs[1]).T
        _ride_wait(copies, step_id == nsteps - 1)

    return pl.pallas_call(
        body, name="attn_fwd", grid=grid,
        in_specs=[pl.BlockSpec((tq, 128), lambda b, p, i: (b * nq + i, p)),
                  pl.BlockSpec((1, s, 128), lambda b, p, i: (p // 2, b, 0)),
                  pl.BlockSpec((1, 128, s), lambda b, p, i: (p // 2, 0, b))] + rd.in_specs,
        out_specs=[pl.BlockSpec((tq, 128), lambda b, p, i: (b * nq + i, p)),
                   pl.BlockSpec((1, 2, 1, tq), lambda b, p, i: (b, p, 0, i))] + rd.out_specs,
        out_shape=[jax.ShapeDtypeStruct((t, D_ATTN), F32),
                   jax.ShapeDtypeStruct((nb, N_HEADS, 1, s), F32)] + rd.out_shape,
        scratch_shapes=rd.scratch,
        compiler_params=_cparams(("arbitrary", "arbitrary", "arbitrary")),
    )(q, kd, vdt, *rd.srcs)


def _attn_bwd(q, kd, vd, kdt, o, lse, do, nb, s, tq, ride=None):
    t = q.shape[0]
    nq = s // tq
    ones8 = jnp.ones((8, 128), BF16)
    rd = _ride_plan(ride)
    grid = (nb, N_KV, 2, nq)
    nsteps = nb * N_KV * 2 * nq

    def body(*refs):
        q_ref, k_ref, v_ref, kt_ref, o_ref, lse_ref, do_ref, ones_ref = refs[:8]
        dq_ref, dk_ref, dv_ref = refs[8 + rd.n:11 + rd.n]
        copies = rd.copies(refs[8:8 + rd.n], refs[11 + rd.n:11 + 2 * rd.n], refs[11 + 2 * rd.n:])
        step_id = _grid_step_id(grid)
        _ride_start(copies, step_id == 0)

        @pl.when((pl.program_id(2) == 0) & (pl.program_id(3) == 0))
        def _():
            dk_ref[...] = jnp.zeros_like(dk_ref)
            dv_ref[...] = jnp.zeros_like(dv_ref)

        qv = q_ref[...].astype(F32)
        dov = do_ref[...]
        ov = o_ref[...]
        lane = lax.broadcasted_iota(jnp.int32, qv.shape, 1)
        k = k_ref[0]
        v = v_ref[0]
        kt = kt_ref[0]
        dqs = []
        dk_acc = None
        dv_acc = None
        for half in range(2):
            sel = (lane < 64) if half == 0 else (lane >= 64)
            qh = jnp.where(sel, qv, 0.0).astype(BF16)
            doh = jnp.where(sel, dov, 0.0)
            dob = doh.astype(BF16)
            delta = None
            for part in _split_bf16(doh * ov, 3):
                d8 = _dot_nt(ones_ref[...], part)
                delta = d8 if delta is None else delta + d8
            delta = delta[0:1, :]
            st = _dot_nt(k, qh)
            pt = jnp.exp(st - lse_ref[0, half])
            dpt = _dot_nt(v, dob)
            dst = (pt * (dpt - delta)).astype(BF16)
            dkh = _dot(dst, qh)
            dvh = _dot(pt.astype(BF16), dob)
            dk_acc = dkh if dk_acc is None else dk_acc + dkh
            dv_acc = dvh if dv_acc is None else dv_acc + dvh
            dqs.append(_dot(kt, dst))
        dk_ref[0] += dk_acc
        dv_ref[0] += dv_acc
        row = lax.broadcasted_iota(jnp.int32, dqs[0].shape, 0)
        dq_ref[...] = jnp.where(row < 64, dqs[0], dqs[1]).T
        _ride_wait(copies, step_id == nsteps - 1)

    qmap = lambda b, g, p, i: (b * nq + i, g * 2 + p)
    kvmap = lambda b, g, p, i: (g, b, 0)
    return pl.pallas_call(
        body, name="attn_bwd", grid=grid,
        in_specs=[pl.BlockSpec((tq, 128), qmap),
                  pl.BlockSpec((1, s, 128), kvmap),
                  pl.BlockSpec((1, s, 128), kvmap),
                  pl.BlockSpec((1, 128, s), lambda b, g, p, i: (g, 0, b)),
                  pl.BlockSpec((tq, 128), qmap),
                  pl.BlockSpec((1, 2, 1, tq), lambda b, g, p, i: (b, g * 2 + p, 0, i)),
                  pl.BlockSpec((tq, 128), qmap),
                  pl.BlockSpec((8, 128), lambda b, g, p, i: (0, 0))] + rd.in_specs,
        out_specs=[pl.BlockSpec((tq, 128), qmap),
                   pl.BlockSpec((1, s, 128), kvmap),
                   pl.BlockSpec((1, s, 128), kvmap)] + rd.out_specs,
        out_shape=[jax.ShapeDtypeStruct((t, D_ATTN), F32),
                   jax.ShapeDtypeStruct((N_KV, t, 128), F32),
                   jax.ShapeDtypeStruct((N_KV, t, 128), F32)] + rd.out_shape,
        scratch_shapes=rd.scratch,
        compiler_params=_cparams(("arbitrary", "arbitrary", "arbitrary", "arbitrary")),
    )(q, kd, vd, kdt, o, lse, do, ones8, *rd.srcs)


SUB = 16
N_SUB = CHUNK // SUB


def _tri_mats():
    i = np.arange(CHUNK)
    same = (i[:, None] // SUB) == (i[None, :] // SUB)
    lower = jnp.asarray(same & (i[:, None] >= i[None, :]), dtype=BF16)
    upper = jnp.asarray(same & (i[:, None] <= i[None, :]), dtype=BF16)
    return jnp.stack([lower, upper])


def _running_sum(tri, x):
    acc = None
    for part in _split_bf16(x, 3):
        t = _dot(tri, part)
        acc = t if acc is None else acc + t
    return acc


def _gates(z, lb):
    sig = _sigmoid(z)
    f = lb + (1.0 - lb) * sig
    logf = jnp.log(jnp.maximum(f, F_MIN))
    k = (1.0 - lb) * (1.0 - sig)
    return sig, f, logf, k


def _row_group(jg, anti):
    if anti:
        return 0, 8 * jg + 8
    return 8 * jg, SUB


def _sub_order(anti):
    return range(N_SUB - 1, -1, -1) if anti else range(N_SUB)


def _block_columns(b, anti):
    tt = lax.broadcasted_iota(jnp.int32, (SUB, LANES), 0)
    cols = []
    for jg in range(SUB // 8):
        r0, r1 = _row_group(jg, anti)
        br = b[r0:r1]
        tr = tt[r0:r1]
        for i in range(8):
            sc = 8 * jg + i
            mask = (tr <= sc) if anti else (tr >= sc)
            cols.append((r0, r1, sc, jnp.where(mask, jnp.exp(jnp.minimum(br - b[sc:sc + 1], 0.0)), 0.0)))
    return cols


def _lockstep(gens):
    results = [None] * len(gens)
    live = list(range(len(gens)))
    while live:
        for i in list(live):
            try:
                next(gens[i])
            except StopIteration as stop:
                results[i] = stop.value
                live.remove(i)
    return results


def _scatter_rows(base, accs):
    pieces = []
    for g in range(SUB // 8):
        tot = base[8 * g:8 * g + 8]
        for (r0, r1), acc in accs.items():
            if r0 <= 8 * g and 8 * g + 8 <= r1:
                tot = tot + acc[8 * g - r0:8 * g - r0 + 8]
        pieces.append(tot)
    return jnp.concatenate(pieces, axis=0)


def _chunk_fwd(q, k, v, b, st, bones, bmask, anti):
    rs = [slice(SUB * i, SUB * i + SUB) for i in range(N_SUB)]
    decay, update, prods, spans, qbs = [], [], [], [], []
    for i in range(N_SUB):
        qi, ki, vi, bi = q[rs[i]], k[rs[i]], v[rs[i]], b[rs[i]]
        b_last = bi[0:1] if anti else bi[SUB - 1:SUB]
        decay.append(jnp.exp(b_last))
        update.append(_dot_tn(vi.astype(BF16), (ki * jnp.exp(b_last - bi)).astype(BF16)) * bmask)
        qbs.append((qi * jnp.exp(bi)).astype(BF16))
        for r0, r1, sc, e in _block_columns(bi, anti):
            prods.append(qi[r0:r1] * e * ki[sc:sc + 1])
            spans.append((i, r0, r1, sc))
    pb = _dot(jnp.concatenate(prods, axis=0).astype(BF16), bones)
    yield
    entered = [None] * N_SUB
    for i in _sub_order(anti):
        entered[i] = st
        st = st * decay[i] + update[i]
    yield
    accs = [dict() for _ in range(N_SUB)]
    off = 0
    for i, r0, r1, sc in spans:
        term = pb[off:off + r1 - r0] * v[rs[i]][sc:sc + 1]
        off += r1 - r0
        accs[i][(r0, r1)] = term if (r0, r1) not in accs[i] else accs[i][(r0, r1)] + term
    outs = [_scatter_rows(_dot_nt(qbs[i], entered[i].astype(BF16)), accs[i]) for i in range(N_SUB)]
    return jnp.concatenate(outs, axis=0), st, entered


def _hgrn_fwd(proj, lb, gw, nb, s, ride=None):
    t = proj.shape[0]
    nc = s // CHUNK
    assert nc % 2 == 0
    tri = _tri_mats()
    bones = _block_ones(LANES, HEAD_DIM)
    rd = _ride_plan(ride)

    def body(*refs):
        q_ref, zf_ref, zb_ref, v_ref, g_ref, lb_ref, gw_ref, tri_ref, bones_ref = refs[:9]
        y_ref, os_ref, sts_ref = refs[9 + rd.n:12 + rd.n]
        st_ref = refs[12 + 2 * rd.n]
        copies = rd.copies(refs[9:9 + rd.n], refs[12 + rd.n:12 + 2 * rd.n], refs[13 + 2 * rd.n:])
        step_id = pl.program_id(0) * 2 + pl.program_id(1)
        _ride_start(copies, step_id == 0)
        bones_m = bones_ref[...]
        bmask = bones_m.astype(F32)
        st_ref[...] = jnp.zeros_like(st_ref)

        def one_direction(n, anti):
            side = 1 if anti else 0
            z_ref = zb_ref if anti else zf_ref
            cn = (nc - 1 - n) if anti else n
            rows = pl.ds(pl.multiple_of(cn * CHUNK, CHUNK), CHUNK)
            q = q_ref[rows, :]
            v = v_ref[rows, :]
            _, _, logf, k = _gates(z_ref[rows, :], lb_ref[side:side + 1])
            b = _running_sum(tri_ref[side], logf)
            yield
            o, st_new, entered = yield from _chunk_fwd(q, k, v, b, st_ref[side], bones_m, bmask, anti)
            for i in range(N_SUB):
                sts_ref[0, 0, side, cn * N_SUB + i] = entered[i].astype(BF16)
            st_ref[side] = st_new
            (y_ref if anti else os_ref)[rows, :] = o

        def step(n, carry):
            _lockstep([one_direction(n, False), one_direction(n, True)])
            return carry

        lax.fori_loop(0, nc, step, 0)

        def join(n, carry):
            rows = pl.ds(pl.multiple_of(n * (2 * CHUNK), 2 * CHUNK), 2 * CHUNK)
            osum = os_ref[rows, :] + y_ref[rows, :]
            os_ref[rows, :] = osum
            r = lax.rsqrt(_dot_precise(osum * osum, bones_m, 2) * (1.0 / HEAD_DIM) + EPS)
            hg = g_ref[rows, :]
            y_ref[rows, :] = osum * r * gw_ref[...] * (hg * _sigmoid(hg))
            return carry

        lax.fori_loop(0, nc // 2, join, 0)
        _ride_wait(copies, step_id == nb * 2 - 1)

    def col(c):
        return pl.BlockSpec((s, LANES), lambda b, p, c=c: (b, c + p))

    return pl.pallas_call(
        body, name="hgrn_fwd", grid=(nb, 2),
        in_specs=[col(COL_HQ), col(COL_ZFW), col(COL_ZBW), col(COL_HI), col(COL_HG),
                  pl.BlockSpec((2, LANES), lambda b, p: (0, p)),
                  pl.BlockSpec((1, LANES), lambda b, p: (0, 0)),
                  pl.BlockSpec((2, CHUNK, CHUNK), lambda b, p: (0, 0, 0)),
                  pl.BlockSpec((LANES, LANES), lambda b, p: (0, 0))] + rd.in_specs,
        out_specs=[pl.BlockSpec((s, LANES), lambda b, p: (b, p)),
                   pl.BlockSpec((s, LANES), lambda b, p: (b, p)),
                   pl.BlockSpec((1, 1, 2, nc * N_SUB, LANES, LANES), lambda b, p: (b, p, 0, 0, 0, 0))]
        + rd.out_specs,
        out_shape=[jax.ShapeDtypeStruct((t, D_HGRN), F32), jax.ShapeDtypeStruct((t, D_HGRN), F32),
                   jax.ShapeDtypeStruct((nb, 2, 2, nc * N_SUB, LANES, LANES), BF16)] + rd.out_shape,
        scratch_shapes=[pltpu.VMEM((2, LANES, LANES), F32)] + rd.scratch,
        compiler_params=_cparams(("arbitrary", "arbitrary")),
    )(proj, proj, proj, proj, proj, lb, gw, tri, bones, *rd.srcs)


def _chunk_bwd(q, k, v, b, do, states, rt, bones, bmask, anti):
    rs = [slice(SUB * i, SUB * i + SUB) for i in range(N_SUB)]
    r8 = lax.broadcasted_iota(jnp.int32, (8, LANES), 0)
    decay, update, dq_inter, ebls, prods_p, prods_d, spans, qes, kes = [], [], [], [], [], [], [], [], []
    for i in range(N_SUB):
        qi, ki, vi, bi, doi = q[rs[i]], k[rs[i]], v[rs[i]], b[rs[i]], do[rs[i]]
        b_last = bi[0:1] if anti else bi[SUB - 1:SUB]
        eb = jnp.exp(bi)
        dob = doi.astype(BF16)
        decay.append(jnp.exp(b_last))
        ebls.append(jnp.exp(b_last - bi))
        update.append(_dot_tn(dob, (qi * eb).astype(BF16)) * bmask)
        dq_inter.append(eb * _dot(dob, states[i]))
        for r0, r1, sc, e in _block_columns(bi, anti):
            qe = qi[r0:r1] * e
            qes.append(qe)
            kes.append(e * ki[sc:sc + 1])
            prods_p.append(qe * ki[sc:sc + 1])
            prods_d.append(doi[r0:r1] * vi[sc:sc + 1])
            spans.append((i, r0, r1, sc))
    sums = _dot(jnp.concatenate(prods_p + prods_d, axis=0).astype(BF16), bones)
    half = sum(r1 - r0 for _, r0, r1, _ in spans)
    yield
    entered = [None] * N_SUB
    for i in reversed(list(_sub_order(anti))):
        entered[i] = rt
        rt = rt * decay[i] + update[i]
    yield
    accs = [dict() for _ in range(N_SUB)]
    dk_blks = [[jnp.zeros((8, LANES), F32) for _ in range(SUB // 8)] for _ in range(N_SUB)]
    dv_blks = [[jnp.zeros((8, LANES), F32) for _ in range(SUB // 8)] for _ in range(N_SUB)]
    off = 0
    for n, (i, r0, r1, sc) in enumerate(spans):
        nr = r1 - r0
        pb = sums[off:off + nr]
        dpb = sums[half + off:half + off + nr]
        off += nr
        term = dpb * kes[n]
        accs[i][(r0, r1)] = term if (r0, r1) not in accs[i] else accs[i][(r0, r1)] + term
        dk_s = jnp.sum(dpb * qes[n], axis=0, keepdims=True)
        dv_s = jnp.sum(pb * do[rs[i]][r0:r1], axis=0, keepdims=True)
        dk_blks[i][sc // 8] = jnp.where(r8 == sc % 8, dk_s, dk_blks[i][sc // 8])
        dv_blks[i][sc // 8] = jnp.where(r8 == sc % 8, dv_s, dv_blks[i][sc // 8])
    dqs, dks, dvs, dbs = [], [], [], []
    for i in range(N_SUB):
        ki, vi = k[rs[i]], v[rs[i]]
        rtb = entered[i].astype(BF16)
        dk_inter = ebls[i] * _dot(vi.astype(BF16), rtb)
        dv_inter = _dot_nt((ki * ebls[i]).astype(BF16), rtb)
        dqs.append(_scatter_rows(dq_inter[i], accs[i]))
        dks.append(dk_inter + jnp.concatenate(dk_blks[i], axis=0))
        dvs.append(dv_inter + jnp.concatenate(dv_blks[i], axis=0))
        db_last = (jnp.sum(ki * dk_inter, axis=0, keepdims=True)
                   + decay[i] * jnp.sum(entered[i] * states[i].astype(F32), axis=0, keepdims=True))
        dbs.append(jnp.broadcast_to(db_last, (SUB, LANES)))
    cat = lambda xs: jnp.concatenate(xs, axis=0)
    return cat(dqs), cat(dks), cat(dvs), rt, cat(dbs)


def _hgrn_bwd(proj, lb, gw, osum, states, dy, nb, s, ride=None):
    t = proj.shape[0]
    nc = s // CHUNK
    assert nc % 2 == 0
    tri = _tri_mats()
    bones = _block_ones(LANES, HEAD_DIM)
    rd = _ride_plan(ride)

    def body(*refs):
        (q_ref, zf_ref, zb_ref, v_ref, g_ref, lb_ref, gw_ref, os_ref, sts_ref, dy_ref, tri_ref,
         bones_ref) = refs[:12]
        dq_ref, dzf_ref, dzb_ref, dv_ref, dg_ref, dgw_ref, dlb_ref = refs[12 + rd.n:19 + rd.n]
        do_sc, dq_sc, dv_sc, rt_cur = refs[19 + 2 * rd.n:23 + 2 * rd.n]
        copies = rd.copies(refs[12:12 + rd.n], refs[19 + rd.n:19 + 2 * rd.n], refs[23 + 2 * rd.n:])
        step_id = pl.program_id(0) * 2 + pl.program_id(1)
        _ride_start(copies, step_id == 0)
        bones_m = bones_ref[...]
        bmask = bones_m.astype(F32)
        gwv = gw_ref[...]

        def head(n, acc):
            rows = pl.ds(pl.multiple_of(n * (2 * CHUNK), 2 * CHUNK), 2 * CHUNK)
            o = os_ref[rows, :]
            hg = g_ref[rows, :]
            dyv = dy_ref[rows, :]
            sg = _sigmoid(hg)
            r = lax.rsqrt(_dot_precise(o * o, bones_m, 2) * (1.0 / HEAD_DIM) + EPS)
            nrm = o * r * gwv
            dn = dyv * (hg * sg)
            dg_ref[rows, :] = (dyv * nrm * (sg * (1.0 + hg * (1.0 - sg)))).astype(BF16)
            g = dn * gwv
            mean_go = _dot_precise(g * o, bones_m, 2) * (1.0 / HEAD_DIM)
            do_sc[rows, :] = r * (g - o * (r * r) * mean_go)
            return acc + jnp.sum(dn * o * r, axis=0, keepdims=True)

        dgw_ref[0] = lax.fori_loop(0, nc // 2, head, jnp.zeros((1, LANES), F32))
        dq_sc[...] = jnp.zeros_like(dq_sc)
        dv_sc[...] = jnp.zeros_like(dv_sc)

        rt_cur[...] = jnp.zeros_like(rt_cur)

        def one_direction(n, anti):
            side = 1 if anti else 0
            z_ref = zb_ref if anti else zf_ref
            dz_ref = dzb_ref if anti else dzf_ref
            lbv = lb_ref[side:side + 1]
            cn = n if anti else (nc - 1 - n)
            rows = pl.ds(pl.multiple_of(cn * CHUNK, CHUNK), CHUNK)
            q = q_ref[rows, :]
            v = v_ref[rows, :]
            sig, f, logf, k = _gates(z_ref[rows, :], lbv)
            b = _running_sum(tri_ref[side], logf)
            yield
            do = do_sc[rows, :]
            entered = [sts_ref[0, 0, side, cn * N_SUB + i] for i in range(N_SUB)]
            dq, dk, dv, rt_new, db_last = yield from _chunk_bwd(q, k, v, b, do, entered, rt_cur[side], bones_m,
                                                                bmask, anti)
            rt_cur[side] = rt_new
            dq_sc[rows, :] += dq
            dv_sc[rows, :] += dv
            dlogf = _running_sum(tri_ref[1 - side], q * dq - k * dk) + db_last
            dfl = jnp.where(f > F_MIN, dlogf / f, 0.0)
            dz_ref[rows, :] = ((dfl - dk) * (1.0 - lbv) * sig * (1.0 - sig)).astype(BF16)
            return jnp.sum((dfl - dk) * (1.0 - sig), axis=0, keepdims=True)

        def back(n, dlb):
            d0, d1 = _lockstep([one_direction(n, False), one_direction(n, True)])
            return dlb[0] + d0, dlb[1] + d1

        zero = jnp.zeros((1, LANES), F32)
        dlb0, dlb1 = lax.fori_loop(0, nc, back, (zero, zero))
        dlb_ref[0, 0:1, :] = dlb0
        dlb_ref[0, 1:2, :] = dlb1

        dq_ref[...] = dq_sc[...].astype(BF16)
        dv_ref[...] = dv_sc[...].astype(BF16)
        _ride_wait(copies, step_id == nb * 2 - 1)

    def col(c):
        return pl.BlockSpec((s, LANES), lambda b, p, c=c: (b, c + p))

    sl = pl.BlockSpec((s, LANES), lambda b, p: (b, p))
    out_t = jax.ShapeDtypeStruct((t, D_HGRN), BF16)
    return pl.pallas_call(
        body, name="hgrn_bwd", grid=(nb, 2),
        in_specs=[col(COL_HQ), col(COL_ZFW), col(COL_ZBW), col(COL_HI), col(COL_HG),
                  pl.BlockSpec((2, LANES), lambda b, p: (0, p)),
                  pl.BlockSpec((1, LANES), lambda b, p: (0, 0)),
                  sl,
                  pl.BlockSpec((1, 1, 2, nc * N_SUB, LANES, LANES), lambda b, p: (b, p, 0, 0, 0, 0)),
                  sl,
                  pl.BlockSpec((2, CHUNK, CHUNK), lambda b, p: (0, 0, 0)),
                  pl.BlockSpec((LANES, LANES), lambda b, p: (0, 0))] + rd.in_specs,
        out_specs=[sl, sl, sl, sl, sl,
                   pl.BlockSpec((1, 1, LANES), lambda b, p: (b, 0, p)),
                   pl.BlockSpec((1, 2, LANES), lambda b, p: (b, 0, p))] + rd.out_specs,
        out_shape=[out_t, out_t, out_t, out_t, out_t,
                   jax.ShapeDtypeStruct((nb, 1, D_HGRN), F32),
                   jax.ShapeDtypeStruct((nb, 2, D_HGRN), F32)] + rd.out_shape,
        scratch_shapes=[pltpu.VMEM((s, LANES), F32), pltpu.VMEM((s, LANES), F32), pltpu.VMEM((s, LANES), F32),
                        pltpu.VMEM((2, LANES, LANES), F32)] + rd.scratch,
        compiler_params=_cparams(("arbitrary", "arbitrary")),
    )(proj, proj, proj, proj, proj, lb, gw, osum, states, dy, tri, bones, *rd.srcs)


def _lower_bounds(logits):
    def body(lg_ref, lb_ref):
        rows = [lg_ref[l:l + 1, :] for l in range(DEPTH)]
        m = functools.reduce(jnp.maximum, rows)
        ex = [jnp.exp(r - m) for r in rows]
        den = functools.reduce(jnp.add, ex)
        run = jnp.zeros_like(m)
        for l in range(DEPTH):
            if l > 0:
                run = run + ex[l] / den
            lb_ref[l:l + 1, :] = run

    return pl.pallas_call(body, name="lower_bounds", out_shape=jax.ShapeDtypeStruct(logits.shape, F32))(logits)


def _lower_bounds_bwd(logits, dlb):
    def body(lg_ref, dlb_ref, dlg_ref):
        rows = [lg_ref[l:l + 1, :] for l in range(DEPTH)]
        m = functools.reduce(jnp.maximum, rows)
        ex = [jnp.exp(r - m) for r in rows]
        den = functools.reduce(jnp.add, ex)
        sm = [e / den for e in ex]
        dsm = [jnp.zeros_like(m) for _ in range(DEPTH)]
        for i in range(1, DEPTH):
            for l in range(i, DEPTH):
                dsm[i] = dsm[i] + dlb_ref[l:l + 1, :]
        dot = functools.reduce(jnp.add, [sm[i] * dsm[i] for i in range(DEPTH)])
        for i in range(DEPTH):
            dlg_ref[i:i + 1, :] = sm[i] * (dsm[i] - dot)

    return pl.pallas_call(body, name="lower_bounds_bwd", out_shape=jax.ShapeDtypeStruct(logits.shape, F32))(logits, dlb)


CONV_ROWS = 128


def _conv_core(a, bg, dww, dwb, lnw, lnb, upad_ref, s):
    sb = _sigmoid(bg)
    u = a * sb
    upad_ref[0:16, :] = jnp.zeros((16, D_CONV), F32)
    upad_ref[16:16 + s, :] = u
    upad_ref[16 + s:32 + s, :] = jnp.zeros((16, D_CONV), F32)
    rows = min(s, CONV_ROWS)
    pieces = []
    for r0 in range(0, s, rows):
        acc = None
        for j in range(CONV_W):
            term = upad_ref[r0 + 1 + j:r0 + 1 + j + rows, :] * dww[j:j + 1, :]
            acc = term if acc is None else acc + term
        pieces.append(acc)
    c = jnp.concatenate(pieces, axis=0) + dwb
    mu = jnp.mean(c, axis=-1, keepdims=True)
    xc = c - mu
    rstd = lax.rsqrt(jnp.mean(xc * xc, axis=-1, keepdims=True) + LN_EPS)
    nh = xc * rstd
    l = nh * lnw + lnb
    sl = _sigmoid(l)
    return sb, nh, rstd, l, sl


def _conv_fwd(proj, dww, dwb, lnw, lnb, pww, pwb, nb, s):
    t = proj.shape[0]
    assert s % min(s, CONV_ROWS) == 0

    def body(a_ref, b_ref, dww_ref, dwb_ref, lnw_ref, lnb_ref, pww_ref, pwb_ref, y_ref, upad_ref):
        _, _, _, l, sl = _conv_core(a_ref[...], b_ref[...], dww_ref[...], dwb_ref[...], lnw_ref[...],
                                    lnb_ref[...], upad_ref, s)
        y_ref[...] = _dot((l * sl).astype(BF16), pww_ref[...]) + pwb_ref[...]

    vec = pl.BlockSpec((1, D_CONV), lambda b: (0, 0))
    return pl.pallas_call(
        body, name="conv_fwd", grid=(nb,),
        in_specs=[pl.BlockSpec((s, D_CONV), lambda b: (b, COL_CA)),
                  pl.BlockSpec((s, D_CONV), lambda b: (b, COL_CB)),
                  pl.BlockSpec((32, D_CONV), lambda b: (0, 0)), vec, vec, vec,
                  pl.BlockSpec((D_CONV, D_CONV), lambda b: (0, 0)), vec],
        out_specs=pl.BlockSpec((s, D_CONV), lambda b: (b, 0)),
        out_shape=jax.ShapeDtypeStruct((t, D_CONV), F32),
        scratch_shapes=[pltpu.VMEM((s + 32, D_CONV), F32)],
        compiler_params=_cparams(("parallel",)),
    )(proj, proj, dww, dwb, lnw, lnb, pww, pwb)


def _conv_bwd(proj, dy, dww, dwb, lnw, lnb, pww, nb, s):
    t = proj.shape[0]

    def body(a_ref, b_ref, dy_ref, dww_ref, dwb_ref, lnw_ref, lnb_ref, pww_ref,
             dab_ref, ddww_ref, ddwb_ref, dlnw_ref, dlnb_ref, dpww_ref, dpwb_ref, upad_ref, dcpad_ref):
        a = a_ref[...]
        dww = dww_ref[...]
        sb, nh, rstd, l, sl = _conv_core(a, b_ref[...], dww, dwb_ref[...], lnw_ref[...], lnb_ref[...],
                                         upad_ref, s)
        dyv = dy_ref[...]
        dyb = dyv.astype(BF16)
        ds = _dot_nt(dyb, pww_ref[...])
        dl = ds * (sl * (1.0 + l * (1.0 - sl)))
        dn = dl * lnw_ref[...]
        dc = rstd * (dn - jnp.mean(dn, axis=-1, keepdims=True)
                     - nh * jnp.mean(dn * nh, axis=-1, keepdims=True))

        @pl.when(pl.program_id(0) == 0)
        def _():
            for r in (ddww_ref, ddwb_ref, dlnw_ref, dlnb_ref, dpww_ref, dpwb_ref):
                r[...] = jnp.zeros_like(r)

        dpww_ref[...] += _dot_tn((l * sl).astype(BF16), dyb)
        dpwb_ref[...] += jnp.sum(dyv, axis=0, keepdims=True)
        dlnw_ref[...] += jnp.sum(dl * nh, axis=0, keepdims=True)
        dlnb_ref[...] += jnp.sum(dl, axis=0, keepdims=True)
        ddwb_ref[...] += jnp.sum(dc, axis=0, keepdims=True)

        dcpad_ref[0:16, :] = jnp.zeros((16, D_CONV), F32)
        dcpad_ref[16:16 + s, :] = dc
        dcpad_ref[16 + s:32 + s, :] = jnp.zeros((16, D_CONV), F32)
        rows = min(s, CONV_ROWS)
        r8 = lax.broadcasted_iota(jnp.int32, (32, D_CONV), 0)
        ddww = jnp.zeros((32, D_CONV), F32)
        pieces = []
        for r0 in range(0, s, rows):
            acc = None
            dcr = dcpad_ref[16 + r0:16 + r0 + rows, :]
            for j in range(CONV_W):
                term = dcpad_ref[r0 + 31 - j:r0 + 31 - j + rows, :] * dww[j:j + 1, :]
                acc = term if acc is None else acc + term
                wj = jnp.sum(dcr * upad_ref[r0 + 1 + j:r0 + 1 + j + rows, :], axis=0, keepdims=True)
                ddww = ddww + jnp.where(r8 == j, wj, 0.0)
            pieces.append(acc)
        du = jnp.concatenate(pieces, axis=0)
        ddww_ref[...] += ddww
        dab_ref[:, 0:D_CONV] = (du * sb).astype(BF16)
        dab_ref[:, D_CONV:2 * D_CONV] = (du * a * sb * (1.0 - sb)).astype(BF16)

    vec = pl.BlockSpec((1, D_CONV), lambda b: (0, 0))
    mat = pl.BlockSpec((D_CONV, D_CONV), lambda b: (0, 0))
    w32 = pl.BlockSpec((32, D_CONV), lambda b: (0, 0))
    vshape = jax.ShapeDtypeStruct((1, D_CONV), F32)
    return pl.pallas_call(
        body, name="conv_bwd", grid=(nb,),
        in_specs=[pl.BlockSpec((s, D_CONV), lambda b: (b, COL_CA)),
                  pl.BlockSpec((s, D_CONV), lambda b: (b, COL_CB)),
                  pl.BlockSpec((s, D_CONV), lambda b: (b, 0)),
                  w32, vec, vec, vec, mat],
        out_specs=[pl.BlockSpec((s, 2 * D_CONV), lambda b: (b, 0)), w32, vec, vec, vec, mat, vec],
        out_shape=[jax.ShapeDtypeStruct((t, 2 * D_CONV), BF16),
                   jax.ShapeDtypeStruct((32, D_CONV), F32), vshape, vshape, vshape,
                   jax.ShapeDtypeStruct((D_CONV, D_CONV), F32), vshape],
        scratch_shapes=[pltpu.VMEM((s + 32, D_CONV), F32), pltpu.VMEM((s + 32, D_CONV), F32)],
        compiler_params=_cparams(("arbitrary",)),
    )(proj, proj, dy, dww, dwb, lnw, lnb, pww)


def _mix_out(o_attn, y_hgrn, y_conv, x, aw, cw, w_out, tm):
    t = x.shape[0]

    def body(o_ref, h_ref, c_ref, x_ref, aw_ref, cw_ref, w_ref, mixed_ref, x1_ref):
        o = o_ref[...]
        a = o * lax.rsqrt(jnp.mean(o * o, axis=-1, keepdims=True) + EPS) * aw_ref[...]
        yc = c_ref[...]
        c = yc * lax.rsqrt(jnp.mean(yc * yc, axis=-1, keepdims=True) + EPS) * cw_ref[...]
        ab, hb, cb = a.astype(BF16), h_ref[...].astype(BF16), c.astype(BF16)
        mixed_ref[:, 0:512] = ab
        mixed_ref[:, 512:768] = hb
        mixed_ref[:, 768:1024] = cb
        x1_ref[...] = (x_ref[...] + _dot(ab, w_ref[0:512, :]) + _dot(hb, w_ref[512:768, :])
                       + _dot(cb, w_ref[768:1024, :]))

    def tok(w):
        return pl.BlockSpec((tm, w), lambda i: (i, 0))

    return pl.pallas_call(
        body, name="mix_out", grid=(t // tm,),
        in_specs=[tok(512), tok(256), tok(256), tok(D_MODEL),
                  pl.BlockSpec((1, 512), lambda i: (0, 0)), pl.BlockSpec((1, 256), lambda i: (0, 0)),
                  pl.BlockSpec((D_MODEL, D_MODEL), lambda i: (0, 0))],
        out_specs=[tok(D_MODEL), tok(D_MODEL)],
        out_shape=[jax.ShapeDtypeStruct((t, D_MODEL), BF16), jax.ShapeDtypeStruct((t, D_MODEL), F32)],
        compiler_params=_cparams(("parallel",)),
    )(o_attn, y_hgrn, y_conv, x, aw, cw, w_out)


def _mix_out_bwd(dx1, w_out, o_attn, y_conv, aw, cw, tm):
    t = dx1.shape[0]

    def body(dx_ref, w_ref, o_ref, c_ref, aw_ref, cw_ref, do_ref, dh_ref, dc_ref, daw_ref, dcw_ref):
        dm = _dot_nt(dx_ref[...].astype(BF16), w_ref[...])
        do, daw = _rms_bwd(dm[:, 0:512], o_ref[...], aw_ref[...])
        dc, dcw = _rms_bwd(dm[:, 768:1024], c_ref[...], cw_ref[...])
        do_ref[...] = do
        dh_ref[...] = dm[:, 512:768]
        dc_ref[...] = dc

        @pl.when(pl.program_id(0) == 0)
        def _():
            daw_ref[...] = jnp.zeros_like(daw_ref)
            dcw_ref[...] = jnp.zeros_like(dcw_ref)

        daw_ref[...] += jnp.sum(daw, axis=0, keepdims=True)
        dcw_ref[...] += jnp.sum(dcw, axis=0, keepdims=True)

    def tok(w):
        return pl.BlockSpec((tm, w), lambda i: (i, 0))

    v512 = pl.BlockSpec((1, 512), lambda i: (0, 0))
    v256 = pl.BlockSpec((1, 256), lambda i: (0, 0))
    return pl.pallas_call(
        body, name="mix_out_bwd", grid=(t // tm,),
        in_specs=[tok(D_MODEL), pl.BlockSpec((D_MODEL, D_MODEL), lambda i: (0, 0)), tok(512), tok(256),
                  v512, v256],
        out_specs=[tok(512), tok(256), tok(256), v512, v256],
        out_shape=[jax.ShapeDtypeStruct((t, 512), F32), jax.ShapeDtypeStruct((t, 256), F32),
                   jax.ShapeDtypeStruct((t, 256), F32), jax.ShapeDtypeStruct((1, 512), F32),
                   jax.ShapeDtypeStruct((1, 256), F32)],
        compiler_params=_cparams(("arbitrary",)),
    )(dx1, w_out, o_attn, y_conv, aw, cw)


FF_BLOCKS = 4


def _ffn_fwd(x1, fw, wg, wu, wd, tm, ride=None):
    t = x1.shape[0]
    fb = wg.shape[1]
    nf = N_DEV // FF_BLOCKS
    rd = _ride_plan(ride)
    grid = (t // tm, nf)

    def body(*refs):
        x_ref, fw_ref, wg_ref, wu_ref, wd_ref = refs[:5]
        h_ref, g_ref, u_ref, a_ref, x2_ref = refs[5 + rd.n:10 + rd.n]
        acc_ref = refs[10 + 2 * rd.n]
        copies = rd.copies(refs[5:5 + rd.n], refs[10 + rd.n:10 + 2 * rd.n], refs[11 + 2 * rd.n:])
        step_id = _grid_step_id(grid)
        _ride_start(copies, step_id == 0)
        j = pl.program_id(1)

        @pl.when(j == 0)
        def _():
            xv = x_ref[...]
            r = lax.rsqrt(jnp.mean(xv * xv, axis=-1, keepdims=True) + EPS)
            h_ref[...] = (xv * r * fw_ref[...]).astype(BF16)
            acc_ref[...] = xv

        h = h_ref[...]
        out = None
        for c in range(FF_BLOCKS):
            g = _dot_nt(h, wg_ref[c])
            u = _dot_nt(h, wu_ref[c])
            a = (g * _sigmoid(g) * u).astype(BF16)
            g_ref[c] = g.astype(BF16)
            u_ref[c] = u.astype(BF16)
            a_ref[c] = a
            part = _dot(a, wd_ref[c])
            out = part if out is None else out + part
        acc_ref[...] += out

        @pl.when(j == nf - 1)
        def _():
            x2_ref[...] = acc_ref[...]

        _ride_wait(copies, step_id == (t // tm) * nf - 1)

    tok = pl.BlockSpec((tm, D_MODEL), lambda i, j: (i, 0))
    ffb = pl.BlockSpec((FF_BLOCKS, tm, fb), lambda i, j: (j, i, 0))
    ffs = jax.ShapeDtypeStruct((N_DEV, t, fb), BF16)
    return pl.pallas_call(
        body, name="ffn_fwd", grid=grid,
        in_specs=[tok, pl.BlockSpec((1, D_MODEL), lambda i, j: (0, 0)),
                  pl.BlockSpec((FF_BLOCKS, fb, D_MODEL), lambda i, j: (j, 0, 0)),
                  pl.BlockSpec((FF_BLOCKS, fb, D_MODEL), lambda i, j: (j, 0, 0)),
                  pl.BlockSpec((FF_BLOCKS, fb, D_MODEL), lambda i, j: (j, 0, 0))] + rd.in_specs,
        out_specs=[tok, ffb, ffb, ffb, tok] + rd.out_specs,
        out_shape=[jax.ShapeDtypeStruct((t, D_MODEL), BF16), ffs, ffs, ffs,
                   jax.ShapeDtypeStruct((t, D_MODEL), F32)] + rd.out_shape,
        scratch_shapes=[pltpu.VMEM((tm, D_MODEL), F32)] + rd.scratch,
        compiler_params=_cparams(("arbitrary", "arbitrary")),
    )(x1, fw, wg, wu, wd, *rd.srcs)


def _ffn_bwd(dx2, g, u, wg, wu, wd, x1, fw, tm, ride=None):
    t = dx2.shape[0]
    fb = wg.shape[1]
    nf = N_DEV // FF_BLOCKS
    rd = _ride_plan(ride)
    grid = (t // tm, nf)

    def body(*refs):
        dx_ref, g_ref, u_ref, wg_ref, wu_ref, wd_ref, x_ref, fw_ref = refs[:8]
        dg_ref, du_ref, dx1_ref, dfw_ref = refs[8 + rd.n:12 + rd.n]
        acc_ref = refs[12 + 2 * rd.n]
        copies = rd.copies(refs[8:8 + rd.n], refs[12 + rd.n:12 + 2 * rd.n], refs[13 + 2 * rd.n:])
        step_id = _grid_step_id(grid)
        _ride_start(copies, step_id == 0)
        i = pl.program_id(0)
        j = pl.program_id(1)
        dxb = dx_ref[...].astype(BF16)
        dh = None
        for c in range(FF_BLOCKS):
            da = _dot_nt(dxb, wd_ref[c])
            gv = g_ref[c].astype(F32)
            uv = u_ref[c].astype(F32)
            sg = _sigmoid(gv)
            dg = (da * uv * (sg * (1.0 + gv * (1.0 - sg)))).astype(BF16)
            du = (da * gv * sg).astype(BF16)
            dg_ref[c] = dg
            du_ref[c] = du
            part = _dot(dg, wg_ref[c]) + _dot(du, wu_ref[c])
            dh = part if dh is None else dh + part

        @pl.when(j == 0)
        def _():
            acc_ref[...] = dh

        @pl.when(j > 0)
        def _():
            acc_ref[...] += dh

        @pl.when((i == 0) & (j == 0))
        def _():
            dfw_ref[...] = jnp.zeros_like(dfw_ref)

        @pl.when(j == nf - 1)
        def _():
            dx, dfw = _rms_bwd(acc_ref[...], x_ref[...], fw_ref[...])
            dx1_ref[...] = dx_ref[...] + dx
            dfw_ref[...] += jnp.sum(dfw, axis=0, keepdims=True)

        _ride_wait(copies, step_id == (t // tm) * nf - 1)

    tok = pl.BlockSpec((tm, D_MODEL), lambda i, j: (i, 0))
    ffb = pl.BlockSpec((FF_BLOCKS, tm, fb), lambda i, j: (j, i, 0))
    ffs = jax.ShapeDtypeStruct((N_DEV, t, fb), BF16)
    vec = pl.BlockSpec((1, D_MODEL), lambda i, j: (0, 0))
    return pl.pallas_call(
        body, name="ffn_bwd", grid=grid,
        in_specs=[tok, ffb, ffb,
                  pl.BlockSpec((FF_BLOCKS, fb, D_MODEL), lambda i, j: (j, 0, 0)),
                  pl.BlockSpec((FF_BLOCKS, fb, D_MODEL), lambda i, j: (j, 0, 0)),
                  pl.BlockSpec((FF_BLOCKS, fb, D_MODEL), lambda i, j: (j, 0, 0)),
                  tok, vec] + rd.in_specs,
        out_specs=[ffb, ffb, tok, vec] + rd.out_specs,
        out_shape=[ffs, ffs, jax.ShapeDtypeStruct((t, D_MODEL), F32),
                   jax.ShapeDtypeStruct((1, D_MODEL), F32)] + rd.out_shape,
        scratch_shapes=[pltpu.VMEM((tm, D_MODEL), F32)] + rd.scratch,
        compiler_params=_cparams(("arbitrary", "arbitrary")),
    )(dx2, g, u, wg, wu, wd, x1, fw, *rd.srcs)


def _loss_grad(y, target, tm):
    t, d = y.shape

    def body(y_ref, t_ref, dy_ref, loss_ref):
        err = y_ref[...] - t_ref[...]
        dy_ref[...] = err * (1.0 / d)

        @pl.when(pl.program_id(0) == 0)
        def _():
            loss_ref[...] = jnp.zeros_like(loss_ref)

        part = jnp.sum(jnp.sum(err * err, axis=-1, keepdims=True), axis=0, keepdims=True)
        loss_ref[...] += part * (0.5 / d)

    tok = pl.BlockSpec((tm, d), lambda i: (i, 0))
    return pl.pallas_call(
        body, name="loss_grad", grid=(t // tm,),
        in_specs=[tok, tok],
        out_specs=[tok, pl.BlockSpec((1, 1), lambda i: (0, 0))],
        out_shape=[jax.ShapeDtypeStruct((t, d), F32), jax.ShapeDtypeStruct((1, 1), F32)],
        compiler_params=_cparams(("arbitrary",)),
    )(y, target)


def _tile(v, reps):
    return jnp.tile(v.reshape(1, -1), (1, reps))


class _LocalPlan:
    def __init__(self, wb):
        self.w = [{n: wb[n][l] for n in BIG_AXIS} for l in range(DEPTH)]

    def ride(self, kernel_name, l, grads=None):
        return None

    def done(self, kernel_name, l, outs):
        pass


def _local_step(x, target, p, plan):
    nb, s, d = x.shape
    t = nb * s
    tm = min(512, s)
    tq = min(1024, s)
    xf = x.reshape(t, d)
    cosq, sinq = _rope_tables(s)
    ones512 = _block_ones(512, HEAD_DIM)
    lbs = _lower_bounds(p["hgrn_lb_logits"].reshape(DEPTH, 2 * D_HGRN)).reshape(DEPTH, 2, D_HGRN)

    saved = []
    cur = xf
    wb = plan.w
    for l in range(DEPTH):
        qw = _tile(p["q_norm_w"][l], N_HEADS)
        kw = _tile(p["k_norm_w"][l], N_KV)
        gw = _tile(p["hgrn_gnorm_w"][l], 2)
        dww = jnp.pad(p["conv_dw_w"][l], ((0, 1), (0, 0)))
        pww = p["conv_pw_w"][l].astype(BF16)
        h0, proj = _rms_proj(cur, _row(p["mix_norm_w"][l]), wb[l]["w_in"], tm)
        qr, kd, vd, kdt, vdt = _qkv_prep(proj, cosq, sinq, qw, kw, ones512, s, tm)
        o_attn, lse, *rode = _attn_fwd(qr, kd, vdt, nb, s, tq, plan.ride("attn_fwd", l))
        plan.done("attn_fwd", l, rode)
        y_hgrn, osum, states, *rode = _hgrn_fwd(proj, lbs[l], gw, nb, s, plan.ride("hgrn_fwd", l))
        plan.done("hgrn_fwd", l, rode)
        y_conv = _conv_fwd(proj, dww, _row(p["conv_dw_b"][l]), _row(p["conv_ln_w"][l]),
                           _row(p["conv_ln_b"][l]), pww, _row(p["conv_pw_b"][l]), nb, s)
        mixed, x1 = _mix_out(o_attn, y_hgrn, y_conv, cur, _row(p["attn_out_norm_w"][l]),
                             _row(p["conv_out_norm_w"][l]), wb[l]["w_out"], tm)
        hf, g, u, a, x2, *rode = _ffn_fwd(x1, _row(p["ffn_norm_w"][l]), wb[l]["w_gate"], wb[l]["w_up"],
                                          wb[l]["w_down"], tm, plan.ride("ffn_fwd", l))
        plan.done("ffn_fwd", l, rode)
        saved.append(dict(x=cur, h0=h0, proj=proj, qr=qr, kd=kd, vd=vd, kdt=kdt, o_attn=o_attn, lse=lse,
                          osum=osum, states=states, y_conv=y_conv, mixed=mixed, x1=x1, hf=hf, g=g, u=u, a=a,
                          qw=qw, kw=kw, gw=gw, dww=dww, pww=pww))
        cur = x2

    dcur, loss = _loss_grad(cur, target.reshape(t, d), tm)

    grads = {k: [None] * DEPTH for k in WEIGHTS}
    dlb = [None] * DEPTH
    for l in reversed(range(DEPTH)):
        sv = saved[l]
        dg, du, dx1, dfw, *rode = _ffn_bwd(dcur, sv["g"], sv["u"], wb[l]["w_gate"], wb[l]["w_up"],
                                           wb[l]["w_down"], sv["x1"], _row(p["ffn_norm_w"][l]), tm,
                                           plan.ride("ffn_bwd", l, grads))
        plan.done("ffn_bwd", l, rode)
        grads["ffn_norm_w"][l] = dfw[0]
        grads["w_gate"][l] = _dw_ff(dg, sv["hf"], "dw_gate", tm)
        grads["w_up"][l] = _dw_ff(du, sv["hf"], "dw_up", tm)
        grads["w_down"][l] = _dw_ff(sv["a"], dcur, "dw_down", tm)
        do_attn, dy_hgrn, dy_conv, daw, dcw = _mix_out_bwd(
            dx1, wb[l]["w_out"], sv["o_attn"], sv["y_conv"], _row(p["attn_out_norm_w"][l]),
            _row(p["conv_out_norm_w"][l]), tm)
        grads["attn_out_norm_w"][l] = daw[0]
        grads["conv_out_norm_w"][l] = dcw[0]
        grads["w_out"][l] = _mm_tn(sv["mixed"], dx1, D_MODEL, "dw_out", tm)
        dq, dkd, dvd, *rode = _attn_bwd(sv["qr"], sv["kd"], sv["vd"], sv["kdt"], sv["o_attn"], sv["lse"], do_attn,
                                        nb, s, tq, plan.ride("attn_bwd", l, grads))
        plan.done("attn_bwd", l, rode)
        dqkv, dqw, dkw = _qkv_bwd(sv["proj"], dq, dkd, dvd, cosq, sinq, sv["qw"], sv["kw"], ones512, s, tm)
        grads["q_norm_w"][l] = dqw.reshape(N_HEADS, HEAD_DIM).sum(0)
        grads["k_norm_w"][l] = dkw.reshape(N_KV, HEAD_DIM).sum(0)
        dhq, dzf, dzb, dhi, dhg, dgw, dlb_l, *rode = _hgrn_bwd(sv["proj"], lbs[l], sv["gw"], sv["osum"],
                                                               sv["states"], dy_hgrn, nb, s,
                                                               plan.ride("hgrn_bwd", l, grads))
        plan.done("hgrn_bwd", l, rode)
        grads["hgrn_gnorm_w"][l] = dgw.reshape(nb * D_HGRN // HEAD_DIM, HEAD_DIM).sum(0)
        dlb[l] = dlb_l.sum(0)
        dab, ddww, ddwb, dlnw, dlnb, dpww, dpwb = _conv_bwd(
            sv["proj"], dy_conv, sv["dww"], _row(p["conv_dw_b"][l]), _row(p["conv_ln_w"][l]),
            _row(p["conv_ln_b"][l]), sv["pww"], nb, s)
        grads["conv_dw_w"][l] = ddww[:CONV_W]
        grads["conv_dw_b"][l] = ddwb[0]
        grads["conv_ln_w"][l] = dlnw[0]
        grads["conv_ln_b"][l] = dlnb[0]
        grads["conv_pw_w"][l] = dpww
        grads["conv_pw_b"][l] = dpwb[0]
        pieces = [dqkv, dhq, dzf, dzb, dhi, dhg, dab]
        grads["w_in"][l] = _dw_in(sv["h0"], pieces, tm)
        dcur, dnw, *rode = _proj_bwd(pieces, wb[l]["w_in"], sv["x"], _row(p["mix_norm_w"][l]), dx1, tm,
                                     plan.ride("proj_bwd", l, grads))
        plan.done("proj_bwd", l, rode)
        grads["mix_norm_w"][l] = dnw[0]

    dlog = _lower_bounds_bwd(p["hgrn_lb_logits"].reshape(DEPTH, 2 * D_HGRN),
                             jnp.stack(dlb).reshape(DEPTH, 2 * D_HGRN))
    out = {k: (v if k in BIG_AXIS else jnp.stack(v)) for k, v in grads.items() if k != "hgrn_lb_logits"}
    out["hgrn_lb_logits"] = dlog.reshape(DEPTH, 2, D_HGRN)
    return loss, dcur.reshape(nb, s, d), out


BIG_AXIS = {"w_in": 2, "w_out": 1, "w_gate": 2, "w_up": 2, "w_down": 1}
SMALL_SHARD_AXIS = {"hgrn_lb_logits": 2, "conv_dw_w": 2, "conv_pw_w": 1}
WEIGHTS = ("mix_norm_w", "w_in", "q_norm_w", "k_norm_w", "hgrn_lb_logits", "hgrn_gnorm_w", "conv_dw_w",
           "conv_dw_b", "conv_ln_w", "conv_ln_b", "conv_pw_w", "conv_pw_b", "attn_out_norm_w",
           "conv_out_norm_w", "w_out", "ffn_norm_w", "w_gate", "w_up", "w_down")
SMALL = tuple(n for n in WEIGHTS if n not in BIG_AXIS)


def _my_index():
    return 4 * lax.axis_index("x") + 2 * lax.axis_index("y") + lax.axis_index("c")


class _RidePlan:
    def __init__(self, srcs, gather):
        self.srcs = list(srcs)
        self.n = len(self.srcs)
        self.gather = list(gather) if isinstance(gather, (list, tuple)) else [gather] * self.n
        any_spec = pl.BlockSpec(memory_space=pl.ANY)
        self.in_specs = [any_spec] * self.n
        self.out_specs = [any_spec] * self.n
        self.out_shape = [jax.ShapeDtypeStruct(((N_DEV,) + s.shape) if g else s.shape, s.dtype)
                          for s, g in zip(self.srcs, self.gather)]
        npeer = N_DEV - 1
        self.scratch = [pltpu.SemaphoreType.DMA((self.n * npeer,)), pltpu.SemaphoreType.DMA((self.n * npeer,)),
                        pltpu.SemaphoreType.DMA((self.n,))] if self.n else []

    def copies(self, src_refs, out_refs, sems):
        if not self.n:
            return [], [], []
        send_sems, recv_sems, local_sems = sems
        npeer = N_DEV - 1
        x, y, c = lax.axis_index("x"), lax.axis_index("y"), lax.axis_index("c")
        me = 4 * x + 2 * y + c
        locals_, sends, recvs = [], [], []
        for a in range(self.n):
            src_ref, out_ref = src_refs[a], out_refs[a]

            def rows_for(j, src_ref=src_ref, gather=self.gather[a]):
                return src_ref if gather else src_ref.at[j]

            locals_.append(pltpu.make_async_copy(rows_for(me), out_ref.at[me], local_sems.at[a]))
            for k in range(1, N_DEV):
                px = (1 - x) if (k & 4) else x
                py = (1 - y) if (k & 2) else y
                pc = (1 - c) if (k & 1) else c
                pidx = 4 * px + 2 * py + pc
                common = dict(send_sem=send_sems.at[a * npeer + k - 1], recv_sem=recv_sems.at[a * npeer + k - 1],
                              device_id=(px, py, pc), device_id_type=pl.DeviceIdType.MESH)
                sends.append(pltpu.make_async_remote_copy(src_ref=rows_for(pidx), dst_ref=out_ref.at[me], **common))
                recvs.append(pltpu.make_async_remote_copy(src_ref=rows_for(pidx), dst_ref=out_ref.at[pidx],
                                                          **common))
        return locals_, sends, recvs


def _ride_plan(ride):
    return _RidePlan(*ride) if ride else _RidePlan([], True)


def _ride_start(copies, when=None):
    locals_, sends, _ = copies

    def go():
        for cp in locals_ + sends:
            cp.start()

    if locals_:
        go() if when is None else pl.when(when)(go)


def _ride_wait(copies, when=None):
    locals_, sends, recvs = copies

    def go():
        for cp in recvs:
            cp.wait_recv()
        for cp in sends:
            cp.wait_send()
        for cp in locals_:
            cp.wait()

    if locals_:
        go() if when is None else pl.when(when)(go)


def _exchange(srcs, gather, name):
    rd = _RidePlan(srcs, gather)

    def body(*refs):
        copies = rd.copies(refs[:rd.n], refs[rd.n:2 * rd.n], refs[2 * rd.n:])
        _ride_start(copies)
        _ride_wait(copies)

    return pl.pallas_call(body, name=name, in_specs=rd.in_specs, out_specs=rd.out_specs,
                          out_shape=rd.out_shape, scratch_shapes=rd.scratch)(*srcs)


def _adamw_math(w, g, m, v):
    m = ADAM_B1 * m + (1.0 - ADAM_B1) * g
    v = ADAM_B2 * v + (1.0 - ADAM_B2) * (g * g)
    m_hat = m / (1.0 - ADAM_B1 ** ADAM_STEP)
    v_hat = v / (1.0 - ADAM_B2 ** ADAM_STEP)
    delta = -ADAM_LR * (m_hat / (jnp.sqrt(v_hat) + ADAM_EPS) + ADAM_WD * w)
    return delta, m, v


def _sum_adamw(parts, w, m, v, name):
    _, k, n = w.shape
    tk = k
    for cand in (256, 176, 160, 128):
        if k % cand == 0:
            tk = cand
            break

    def body(*refs):
        p_refs = refs[:DEPTH]
        w_ref, m_ref, v_ref, g_ref, d_ref, mo_ref, vo_ref = refs[DEPTH:]
        for l in range(DEPTH):
            @pl.when(pl.program_id(0) == l)
            def _(p_ref=p_refs[l]):
                g = p_ref[0].astype(F32)
                for i in range(1, N_DEV):
                    g = g + p_ref[i].astype(F32)
                g_ref[...] = g
                d_ref[...], mo_ref[...], vo_ref[...] = _adamw_math(w_ref[...], g, m_ref[...], v_ref[...])

    row = pl.BlockSpec((None, tk, n), lambda l, i: (l, i, 0))
    shp = jax.ShapeDtypeStruct(w.shape, F32)
    return pl.pallas_call(
        body, name=name, grid=(DEPTH, k // tk),
        in_specs=[pl.BlockSpec((N_DEV, tk, n), lambda l, i: (0, i, 0))] * DEPTH + [row, row, row],
        out_specs=[row, row, row, row],
        out_shape=[shp, shp, shp, shp],
        compiler_params=_cparams(("parallel", "parallel")),
    )(*parts, w, m, v)


def _sum8(parts, name):
    r = parts.shape[1]

    def body(p_ref, g_ref):
        g = p_ref[0]
        for i in range(1, N_DEV):
            g = g + p_ref[i]
        g_ref[...] = g

    return pl.pallas_call(body, name=name, out_shape=jax.ShapeDtypeStruct((r, LANES), F32))(parts)


def _adamw(w, g, m, v):
    def body(w_ref, g_ref, m_ref, v_ref, d_ref, mo_ref, vo_ref):
        d_ref[...], mo_ref[...], vo_ref[...] = _adamw_math(w_ref[...], g_ref[...], m_ref[...], v_ref[...])

    shp = jax.ShapeDtypeStruct(w.shape, F32)
    return pl.pallas_call(body, name="adamw_small", out_shape=[shp, shp, shp])(w, g, m, v)


def _pack(arrays, dtype, row_multiple):
    flat = jnp.concatenate([a.reshape(-1).astype(dtype) for a in arrays])
    n = flat.shape[0]
    unit = row_multiple * LANES
    total = -(-n // unit) * unit
    return jnp.pad(flat, (0, total - n)).reshape(total // LANES, LANES)


def _unpack(flat2d, shapes, lead=()):
    flat = flat2d.reshape(lead + (-1,))
    out, off = [], 0
    for shp in shapes:
        n = int(np.prod(shp))
        out.append(flat[..., off:off + n].reshape(lead + tuple(shp)))
        off += n
    return out


def _shard_to_rows(full, axis):
    shp = full.shape
    k = shp[axis] // N_DEV
    r = full.reshape(shp[:axis] + (N_DEV, k) + shp[axis + 1:])
    return jnp.moveaxis(r, axis, 0)


def _rows_to_full(rows, axis):
    r = jnp.moveaxis(rows, 0, axis)
    shp = r.shape
    return r.reshape(shp[:axis] + (shp[axis] * shp[axis + 1],) + shp[axis + 2:])


def kernel(x, mix_norm_w, w_in, q_norm_w, k_norm_w, hgrn_lb_logits, hgrn_gnorm_w, conv_dw_w, conv_dw_b, conv_ln_w, conv_ln_b, conv_pw_w, conv_pw_b, attn_out_norm_w, conv_out_norm_w, w_out, ffn_norm_w, w_gate, w_up, w_down, loss_target, m_mix_norm_w, m_w_in, m_q_norm_w, m_k_norm_w, m_hgrn_lb_logits, m_hgrn_gnorm_w, m_conv_dw_w, m_conv_dw_b, m_conv_ln_w, m_conv_ln_b, m_conv_pw_w, m_conv_pw_b, m_attn_out_norm_w, m_conv_out_norm_w, m_w_out, m_ffn_norm_w, m_w_gate, m_w_up, m_w_down, v_mix_norm_w, v_w_in, v_q_norm_w, v_k_norm_w, v_hgrn_lb_logits, v_hgrn_gnorm_w, v_conv_dw_w, v_conv_dw_b, v_conv_ln_w, v_conv_ln_b, v_conv_pw_w, v_conv_pw_b, v_attn_out_norm_w, v_conv_out_norm_w, v_w_out, v_ffn_norm_w, v_w_gate, v_w_up, v_w_down):
    w_loc = dict(zip(WEIGHTS, (mix_norm_w, w_in, q_norm_w, k_norm_w, hgrn_lb_logits, hgrn_gnorm_w, conv_dw_w,
                               conv_dw_b, conv_ln_w, conv_ln_b, conv_pw_w, conv_pw_b, attn_out_norm_w,
                               conv_out_norm_w, w_out, ffn_norm_w, w_gate, w_up, w_down)))
    m_loc = dict(zip(WEIGHTS, (m_mix_norm_w, m_w_in, m_q_norm_w, m_k_norm_w, m_hgrn_lb_logits, m_hgrn_gnorm_w,
                               m_conv_dw_w, m_conv_dw_b, m_conv_ln_w, m_conv_ln_b, m_conv_pw_w, m_conv_pw_b,
                               m_attn_out_norm_w, m_conv_out_norm_w, m_w_out, m_ffn_norm_w, m_w_gate, m_w_up,
                               m_w_down)))
    v_loc = dict(zip(WEIGHTS, (v_mix_norm_w, v_w_in, v_q_norm_w, v_k_norm_w, v_hgrn_lb_logits, v_hgrn_gnorm_w,
                               v_conv_dw_w, v_conv_dw_b, v_conv_ln_w, v_conv_ln_b, v_conv_pw_w, v_conv_pw_b,
                               v_attn_out_norm_w, v_conv_out_norm_w, v_w_out, v_ffn_norm_w, v_w_gate, v_w_up,
                               v_w_down)))
    me = _my_index()
    big = tuple(BIG_AXIS)
    sms = tuple(SMALL_SHARD_AXIS)

    col_sharded = tuple(n for n in big if BIG_AXIS[n] == 2)

    def shard_t(n, a):
        return jnp.swapaxes(a, 1, 2) if n in col_sharded else a

    w_send = {n: shard_t(n, w_loc[n]).astype(BF16) for n in big}

    sm_shapes = [w_loc[n].shape for n in sms]
    got_s, got_w_in0 = _exchange([_pack([w_loc[n] for n in sms], F32, 8), w_send["w_in"][0]], True,
                                 "gather_first")
    p_full = {n: w_loc[n] for n in SMALL if n not in SMALL_SHARD_AXIS}
    for n, a in zip(sms, _unpack(got_s, sm_shapes, (N_DEV,))):
        p_full[n] = _rows_to_full(a, SMALL_SHARD_AXIS[n])

    def natural(n, gathered):
        if n in ("w_in", "w_out"):
            return gathered.reshape(-1, gathered.shape[-1])
        return gathered

    def to_send(n, gl):
        if n in ("w_in", "w_out"):
            return gl.reshape(N_DEV, gl.shape[0] // N_DEV, gl.shape[1]).astype(BF16)
        return gl

    class StepPlan:
        def __init__(self):
            self.w = [dict() for _ in range(DEPTH)]
            self.parts = [dict() for _ in range(DEPTH)]
            self.pending = {}
            self.w[0]["w_in"] = natural("w_in", got_w_in0)

        def ride(self, kernel_name, l, grads=None):
            want = []
            if kernel_name == "attn_fwd":
                want = [("w_out", l), ("w_gate", l)]
            elif kernel_name == "hgrn_fwd":
                want = [("w_up", l), ("w_down", l)]
            elif kernel_name == "ffn_fwd" and l + 1 < DEPTH:
                want = [("w_in", l + 1)]
            elif kernel_name == "ffn_bwd" and l + 1 < DEPTH:
                want = [("w_gate", l + 1), ("w_up", l + 1)]
            elif kernel_name == "attn_bwd" and l + 1 < DEPTH:
                want = [("w_in", l + 1), ("w_out", l + 1), ("w_down", l + 1)]
                if l == 0:
                    want += [("w_out", 0)]
            elif kernel_name == "hgrn_bwd" and l == 0:
                want = [(n, 0) for n in ("w_gate", "w_up", "w_down")]
            elif kernel_name == "proj_bwd" and l == 0:
                want = [("w_in", 0)]
            if not want:
                return None
            self.pending[(kernel_name, l)] = want
            if grads is None:
                return [w_send[n][wl] for n, wl in want], True
            return [to_send(n, grads[n][wl]) for n, wl in want], False

        def done(self, kernel_name, l, outs):
            want = self.pending.pop((kernel_name, l), [])
            for (n, wl), out in zip(want, outs):
                if kernel_name.endswith("_fwd"):
                    self.w[wl][n] = natural(n, out)
                else:
                    self.parts[wl][n] = out

    plan = StepPlan()
    loss_part, grad_x, g = _local_step(x, loss_target, p_full, plan)
    loss = lax.psum(loss_part[0, 0], MESH_AXES)

    pw = g["conv_pw_w"]
    k_pw = w_loc["conv_pw_w"].shape[1]
    pw_send = jnp.moveaxis(pw.reshape(DEPTH, N_DEV, k_pw, pw.shape[-1]), 1, 0).reshape(N_DEV, -1, LANES)
    gathered_small = [n for n in SMALL if n != "conv_pw_w"]
    small_shapes = [g[n].shape for n in gathered_small]
    small_parts, pw_parts = _exchange([_pack([g[n] for n in gathered_small], F32, 8), pw_send],
                                      [True, False], "exchange_small_grads")
    big_out = {}
    for n in big:
        res = _sum_adamw([plan.parts[l][n] for l in range(DEPTH)], shard_t(n, w_loc[n]), shard_t(n, m_loc[n]),
                         shard_t(n, v_loc[n]), "sum_adamw_" + n)
        big_out[n] = [shard_t(n, r) for r in res]

    g_small = dict(zip(gathered_small, _unpack(_sum8(small_parts, "sum_small_grads"), small_shapes)))
    g_small["conv_pw_w"] = _sum8(pw_parts, "sum_conv_pw_grads").reshape(w_loc["conv_pw_w"].shape)
    for n in sms:
        if n == "conv_pw_w":
            continue
        ax = SMALL_SHARD_AXIS[n]
        k = w_loc[n].shape[ax]
        g_small[n] = lax.dynamic_slice_in_dim(g_small[n], me * k, k, axis=ax)
    loc_shapes = [w_loc[n].shape for n in SMALL]
    packed = [_pack([d[n] for n in SMALL], F32, 8) for d in (w_loc, g_small, m_loc, v_loc)]
    res = _adamw(*packed)
    small_out = [g_small] + [dict(zip(SMALL, _unpack(r, loc_shapes))) for r in res]

    def pick(i, n):
        return big_out[n][i] if n in BIG_AXIS else small_out[i][n]

    return (loss, grad_x) + tuple(pick(i, n) for i in range(4) for n in WEIGHTS)
```

```python
import functools

import jax
import jax.numpy as jnp
import numpy as np
from jax import lax
from jax.experimental import pallas as pl
from jax.experimental.pallas import tpu as pltpu

F32 = jnp.float32
BF16 = jnp.bfloat16

D_MODEL = 1024
D_ATTN = 512
D_HGRN = 256
D_CONV = 256
HEAD_DIM = 64
N_HEADS = 8
N_KV = 2
GRID_W = 64
ROPE_THETA = 10000.0
CHUNK = 64
F_MIN = 1e-6
CONV_W = 31
CONV_PAD = 15
D_FF = 2816
D_PROJ = 2560
EPS = 1e-6
LN_EPS = 1e-5
DEPTH = 2
ADAM_LR = 0.001
ADAM_B1 = 0.9
ADAM_B2 = 0.999
ADAM_EPS = 1e-08
ADAM_WD = 0.01
ADAM_STEP = 10
N_DEV = 8
MESH_AXES = ("x", "y", "c")

COL_HQ, COL_ZFW, COL_ZBW, COL_HI, COL_HG = 6, 8, 10, 12, 14
COL_CA, COL_CB = 8, 9

LANES = 128
VMEM_LIMIT_MB = 56


def _cparams(dims=None):
    return pltpu.CompilerParams(dimension_semantics=dims, vmem_limit_bytes=VMEM_LIMIT_MB * 2 ** 20)


def _dot(a, b):
    return jnp.dot(a, b, preferred_element_type=F32)


def _dot_nt(a, b):
    return lax.dot_general(a, b, (((1,), (1,)), ((), ())), preferred_element_type=F32)


def _dot_tn(a, b):
    return lax.dot_general(a, b, (((0,), (0,)), ((), ())), preferred_element_type=F32)


def _split_bf16(x, parts):
    out = []
    r = x
    for _ in range(parts):
        p = r.astype(BF16)
        out.append(p)
        r = r - p.astype(F32)
    return out


def _dot_precise(x, m_bf16, parts=3):
    acc = None
    for p in _split_bf16(x, parts):
        t = _dot(p, m_bf16)
        acc = t if acc is None else acc + t
    return acc


def _block_ones(width, group):
    i = np.arange(width)
    return jnp.asarray((i[:, None] // group) == (i[None, :] // group), dtype=BF16)


def _sigmoid(x):
    return 1.0 / (1.0 + jnp.exp(-x))


def _rot(x):
    w = x.shape[1]
    lane = lax.broadcasted_iota(jnp.int32, x.shape, 1)
    first = (lane % 32) < 16
    return jnp.where(first, -pltpu.roll(x, w - 16, 1), pltpu.roll(x, 16, 1))


def _rope(x, cos, sin):
    return x * cos + _rot(x) * sin


def _rope_t(dy, cos, sin):
    return dy * cos - _rot(dy * sin)


def _row(v):
    return v.reshape(1, -1)


def _rms_proj(x, wn, wt, tm):
    t, d = x.shape
    n = wt.shape[0]

    def body(x_ref, wn_ref, w_ref, h_ref, y_ref):
        xv = x_ref[...]
        r = lax.rsqrt(jnp.mean(xv * xv, axis=-1, keepdims=True) + EPS)
        h = (xv * r * wn_ref[...]).astype(BF16)
        h_ref[...] = h
        y_ref[...] = _dot_nt(h, w_ref[...])

    return pl.pallas_call(
        body, name="rms_proj", grid=(t // tm,),
        in_specs=[pl.BlockSpec((tm, d), lambda i: (i, 0)),
                  pl.BlockSpec((1, d), lambda i: (0, 0)),
                  pl.BlockSpec((n, d), lambda i: (0, 0))],
        out_specs=[pl.BlockSpec((tm, d), lambda i: (i, 0)),
                   pl.BlockSpec((tm, n), lambda i: (i, 0))],
        out_shape=[jax.ShapeDtypeStruct((t, d), BF16), jax.ShapeDtypeStruct((t, n), F32)],
        compiler_params=_cparams(("parallel",)),
    )(x, wn, wt)


def _rms_bwd(dh, x, wn):
    r = lax.rsqrt(jnp.mean(x * x, axis=-1, keepdims=True) + EPS)
    g = dh * wn
    dx = r * (g - x * (r * r) * jnp.mean(g * x, axis=-1, keepdims=True))
    return dx, dh * x * r


def _proj_bwd(pieces, wt, x, wn, dres, tm, ride=None):
    t = x.shape[0]
    d = x.shape[1]
    n = wt.shape[0]
    widths = [p.shape[1] for p in pieces]
    offs = [sum(widths[:i]) for i in range(len(widths))]
    assert sum(widths) == n
    npc = len(pieces)
    rd = _ride_plan(ride)

    def body(*refs):
        p_refs = refs[:npc]
        w_ref, x_ref, wn_ref, dr_ref = refs[npc:npc + 4]
        dx_ref, dwn_ref = refs[npc + 4 + rd.n:npc + 6 + rd.n]
        copies = rd.copies(refs[npc + 4:npc + 4 + rd.n], refs[npc + 6 + rd.n:npc + 6 + 2 * rd.n],
                           refs[npc + 6 + 2 * rd.n:])
        _ride_start(copies, pl.program_id(0) == 0)
        dh = None
        for p_ref, o, wd in zip(p_refs, offs, widths):
            part = _dot(p_ref[...], w_ref[o:o + wd, :])
            dh = part if dh is None else dh + part
        dx, dwn = _rms_bwd(dh, x_ref[...], wn_ref[...])
        dx_ref[...] = dr_ref[...] + dx

        @pl.when(pl.program_id(0) == 0)
        def _():
            dwn_ref[...] = jnp.zeros_like(dwn_ref)

        dwn_ref[...] += jnp.sum(dwn, axis=0, keepdims=True)
        _ride_wait(copies, pl.program_id(0) == t // tm - 1)

    return pl.pallas_call(
        body, name="proj_bwd", grid=(t // tm,),
        in_specs=[pl.BlockSpec((tm, wd), lambda i: (i, 0)) for wd in widths]
        + [pl.BlockSpec((n, d), lambda i: (0, 0)),
           pl.BlockSpec((tm, d), lambda i: (i, 0)),
           pl.BlockSpec((1, d), lambda i: (0, 0)),
           pl.BlockSpec((tm, d), lambda i: (i, 0))] + rd.in_specs,
        out_specs=[pl.BlockSpec((tm, d), lambda i: (i, 0)),
                   pl.BlockSpec((1, d), lambda i: (0, 0))] + rd.out_specs,
        out_shape=[jax.ShapeDtypeStruct((t, d), F32), jax.ShapeDtypeStruct((1, d), F32)] + rd.out_shape,
        scratch_shapes=rd.scratch,
        compiler_params=_cparams(("arbitrary",)),
    )(*pieces, wt, x, wn, dres, *rd.srcs)


def _dw_in(h0, pieces, tm):
    t, k = h0.shape
    widths = [p.shape[1] for p in pieces]
    offs = [sum(widths[:i]) for i in range(len(widths))]
    n = sum(widths)
    npc = len(pieces)

    def body(*refs):
        h_ref = refs[0]
        p_refs = refs[1:1 + npc]
        o_ref, acc_ref = refs[1 + npc:]
        i = pl.program_id(0)

        @pl.when(i == 0)
        def _():
            acc_ref[...] = jnp.zeros_like(acc_ref)

        h = h_ref[...]
        for p_ref, o, wd in zip(p_refs, offs, widths):
            acc_ref[o:o + wd, :] += _dot_tn(p_ref[...], h)

        @pl.when(i == t // tm - 1)
        def _():
            o_ref[...] = acc_ref[...].astype(BF16)

    return pl.pallas_call(
        body, name="dw_in", grid=(t // tm,),
        in_specs=[pl.BlockSpec((tm, k), lambda i: (i, 0))]
        + [pl.BlockSpec((tm, wd), lambda i: (i, 0)) for wd in widths],
        out_specs=pl.BlockSpec((n, k), lambda i: (0, 0)),
        out_shape=jax.ShapeDtypeStruct((n, k), BF16),
        scratch_shapes=[pltpu.VMEM((n, k), F32)],
        compiler_params=_cparams(("arbitrary",)),
    )(h0, *pieces)


def _mm_tn(a, b, tn, name, tm):
    t, k = a.shape
    n = b.shape[1]

    def body(a_ref, b_ref, o_ref):
        @pl.when(pl.program_id(1) == 0)
        def _():
            o_ref[...] = jnp.zeros_like(o_ref)

        o_ref[...] += _dot_tn(a_ref[...].astype(BF16), b_ref[...].astype(BF16))

    return pl.pallas_call(
        body, name=name, grid=(n // tn, t // tm),
        in_specs=[pl.BlockSpec((tm, k), lambda j, i: (i, 0)),
                  pl.BlockSpec((tm, tn), lambda j, i: (i, j))],
        out_specs=pl.BlockSpec((k, tn), lambda j, i: (0, j)),
        out_shape=jax.ShapeDtypeStruct((k, n), F32),
        compiler_params=_cparams(("parallel", "arbitrary")),
    )(a, b)


def _dw_ff(blocked, flat, name, tm):
    t, dm = flat.shape
    fb = blocked.shape[2]
    out_blk = (N_DEV, fb, dm)

    def body(b_ref, f_ref, o_ref, acc_ref):
        i = pl.program_id(0)

        @pl.when(i == 0)
        def _():
            acc_ref[...] = jnp.zeros_like(acc_ref)

        fv = f_ref[...].astype(BF16)
        for j in range(N_DEV):
            acc_ref[j] += _dot_tn(b_ref[j], fv)

        @pl.when(i == t // tm - 1)
        def _():
            o_ref[...] = acc_ref[...].astype(BF16)

    return pl.pallas_call(
        body, name=name, grid=(t // tm,),
        in_specs=[pl.BlockSpec((N_DEV, tm, fb), lambda i: (0, i, 0)),
                  pl.BlockSpec((tm, dm), lambda i: (i, 0))],
        out_specs=pl.BlockSpec(out_blk, lambda i: (0, 0, 0)),
        out_shape=jax.ShapeDtypeStruct(out_blk, BF16),
        scratch_shapes=[pltpu.VMEM(out_blk, F32)],
        compiler_params=_cparams(("arbitrary",)),
    )(blocked, flat)


def _rope_tables(s):
    rows = s // GRID_W
    row_id = jnp.repeat(jnp.arange(rows, dtype=F32), GRID_W)
    col_id = jnp.tile(jnp.arange(GRID_W, dtype=F32), rows)
    half = HEAD_DIM // 2
    inv_freq = ROPE_THETA ** (-jnp.arange(0, half, 2, dtype=F32) / half)
    ang_r = row_id[:, None] * inv_freq[None, :]
    ang_c = col_id[:, None] * inv_freq[None, :]
    ang = jnp.concatenate([ang_r, ang_r, ang_c, ang_c], axis=-1)
    cos, sin = jnp.cos(ang), jnp.sin(ang)
    return jnp.tile(cos, (1, LANES // HEAD_DIM)), jnp.tile(sin, (1, LANES // HEAD_DIM))


def _head_rms(x, w, ones):
    r = lax.rsqrt(_dot_precise(x * x, ones, 2) * (1.0 / HEAD_DIM) + EPS)
    return x * r * w, r


def _dup_half(x, kv):
    lane = lax.broadcasted_iota(jnp.int32, x.shape, 1)
    sel = (lane < 64) if kv == 0 else (lane >= 64)
    return jnp.where(sel, x, pltpu.roll(x, 64, 1))


def _qkv_prep(proj, cosq, sinq, qw, kw, ones, s, tm):
    t = proj.shape[0]
    ns = s // tm

    def body(p_ref, cos_ref, sin_ref, qw_ref, kw_ref, ones_ref, q_out, kd_out, vd_out, kdt_out, vdt_out):
        cos = jnp.tile(cos_ref[...], (1, D_ATTN // LANES))
        sin = jnp.tile(sin_ref[...], (1, D_ATTN // LANES))
        ones_m = ones_ref[...]
        qn, _ = _head_rms(p_ref[:, 0:512], qw_ref[...], ones_m)
        q_out[...] = (_rope(qn, cos, sin) * (HEAD_DIM ** -0.5)).astype(BF16)
        kn, _ = _head_rms(p_ref[:, 512:640], kw_ref[...], ones_m[0:128, 0:128])
        kr = _rope(kn, cos[:, 0:128], sin[:, 0:128])
        v = p_ref[:, 640:768]
        for kv in range(N_KV):
            kd = _dup_half(kr, kv)
            vd = _dup_half(v, kv)
            kd_out[kv] = kd.astype(BF16)
            vd_out[kv] = vd.astype(BF16)
            kdt_out[kv] = kd.T.astype(BF16)
            vdt_out[kv] = vd.T.astype(BF16)

    return pl.pallas_call(
        body, name="qkv_prep", grid=(t // tm,),
        in_specs=[pl.BlockSpec((tm, 768), lambda i: (i, 0)),
                  pl.BlockSpec((tm, LANES), lambda i: (i % ns, 0)),
                  pl.BlockSpec((tm, LANES), lambda i: (i % ns, 0)),
                  pl.BlockSpec((1, 512), lambda i: (0, 0)),
                  pl.BlockSpec((1, 128), lambda i: (0, 0)),
                  pl.BlockSpec((512, 512), lambda i: (0, 0))],
        out_specs=[pl.BlockSpec((tm, 512), lambda i: (i, 0)),
                   pl.BlockSpec((N_KV, tm, 128), lambda i: (0, i, 0)),
                   pl.BlockSpec((N_KV, tm, 128), lambda i: (0, i, 0)),
                   pl.BlockSpec((N_KV, 128, tm), lambda i: (0, 0, i)),
                   pl.BlockSpec((N_KV, 128, tm), lambda i: (0, 0, i))],
        out_shape=[jax.ShapeDtypeStruct((t, 512), BF16),
                   jax.ShapeDtypeStruct((N_KV, t, 128), BF16),
                   jax.ShapeDtypeStruct((N_KV, t, 128), BF16),
                   jax.ShapeDtypeStruct((N_KV, 128, t), BF16),
                   jax.ShapeDtypeStruct((N_KV, 128, t), BF16)],
        compiler_params=_cparams(("parallel",)),
    )(proj, cosq, sinq, qw, kw, ones)


def _qkv_bwd(proj, dq, dkd, dvd, cosq, sinq, qw, kw, ones, s, tm):
    t = proj.shape[0]
    ns = s // tm

    def body(p_ref, dq_ref, dkd_ref, dvd_ref, cos_ref, sin_ref, qw_ref, kw_ref, ones_ref,
             out_ref, dqw_ref, dkw_ref):
        cos = jnp.tile(cos_ref[...], (1, D_ATTN // LANES))
        sin = jnp.tile(sin_ref[...], (1, D_ATTN // LANES))
        ones_m = ones_ref[...]
        ones_k = ones_m[0:128, 0:128]

        def norm_bwd(x, w, dn, om):
            r = lax.rsqrt(_dot_precise(x * x, om, 2) * (1.0 / HEAD_DIM) + EPS)
            g = dn * w
            dx = r * (g - x * (r * r) * (_dot_precise(g * x, om, 2) * (1.0 / HEAD_DIM)))
            return dx, jnp.sum(dn * x * r, axis=0, keepdims=True)

        q = p_ref[:, 0:512]
        dqn = _rope_t(dq_ref[...], cos, sin) * (HEAD_DIM ** -0.5)
        dq_raw, dqw = norm_bwd(q, qw_ref[...], dqn, ones_m)
        out_ref[:, 0:512] = dq_raw.astype(BF16)

        lane = lax.broadcasted_iota(jnp.int32, (tm, 128), 1)

        def fold(ref):
            a0 = ref[0]
            a1 = ref[1]
            f0 = a0 + pltpu.roll(a0, 64, 1)
            f1 = a1 + pltpu.roll(a1, 64, 1)
            return jnp.where(lane < 64, f0, f1)

        k = p_ref[:, 512:640]
        dkn = _rope_t(fold(dkd_ref), cos[:, 0:128], sin[:, 0:128])
        dk_raw, dkw = norm_bwd(k, kw_ref[...], dkn, ones_k)
        out_ref[:, 512:640] = dk_raw.astype(BF16)
        out_ref[:, 640:768] = fold(dvd_ref).astype(BF16)

        @pl.when(pl.program_id(0) == 0)
        def _():
            dqw_ref[...] = jnp.zeros_like(dqw_ref)
            dkw_ref[...] = jnp.zeros_like(dkw_ref)

        dqw_ref[...] += dqw
        dkw_ref[...] += dkw

    return pl.pallas_call(
        body, name="qkv_bwd", grid=(t // tm,),
        in_specs=[pl.BlockSpec((tm, 768), lambda i: (i, 0)),
                  pl.BlockSpec((tm, 512), lambda i: (i, 0)),
                  pl.BlockSpec((N_KV, tm, 128), lambda i: (0, i, 0)),
                  pl.BlockSpec((N_KV, tm, 128), lambda i: (0, i, 0)),
                  pl.BlockSpec((tm, LANES), lambda i: (i % ns, 0)),
                  pl.BlockSpec((tm, LANES), lambda i: (i % ns, 0)),
                  pl.BlockSpec((1, 512), lambda i: (0, 0)),
                  pl.BlockSpec((1, 128), lambda i: (0, 0)),
                  pl.BlockSpec((512, 512), lambda i: (0, 0))],
        out_specs=[pl.BlockSpec((tm, 768), lambda i: (i, 0)),
                   pl.BlockSpec((1, 512), lambda i: (0, 0)),
                   pl.BlockSpec((1, 128), lambda i: (0, 0))],
        out_shape=[jax.ShapeDtypeStruct((t, 768), BF16),
                   jax.ShapeDtypeStruct((1, 512), F32),
                   jax.ShapeDtypeStruct((1, 128), F32)],
        compiler_params=_cparams(("arbitrary",)),
    )(proj, dq, dkd, dvd, cosq, sinq, qw, kw, ones)


def _grid_step_id(grid):
    idx = pl.program_id(0)
    for ax in range(1, len(grid)):
        idx = idx * grid[ax] + pl.program_id(ax)
    return idx


def _attn_fwd(q, kd, vdt, nb, s, tq, ride=None):
    t = q.shape[0]
    nq = s // tq
    rd = _ride_plan(ride)
    grid = (nb, N_HEADS // 2, nq)
    nsteps = nb * (N_HEADS // 2) * nq

    def body(*refs):
        q_ref, k_ref, vt_ref = refs[:3]
        o_ref, lse_ref = refs[3 + rd.n:5 + rd.n]
        copies = rd.copies(refs[3:3 + rd.n], refs[5 + rd.n:5 + 2 * rd.n], refs[5 + 2 * rd.n:])
        step_id = _grid_step_id(grid)
        _ride_start(copies, step_id == 0)
        qv = q_ref[...].astype(F32)
        lane = lax.broadcasted_iota(jnp.int32, qv.shape, 1)
        k = k_ref[0]
        vt = vt_ref[0]
        outs = []
        scores = [_dot_nt(k, jnp.where((lane < 64) if half == 0 else (lane >= 64), qv, 0.0).astype(BF16))
                  for half in range(2)]
        for half in range(2):
            st = scores[half]
            m = jnp.max(st, axis=0, keepdims=True)
            p = jnp.exp(st - m)
            l = jnp.sum(p, axis=0, keepdims=True)
            ot = _dot(vt, p.astype(BF16)) / l
            lse_ref[0, half] = m + jnp.log(l)
            outs.append(ot)
        row = lax.broadcasted_iota(jnp.int32, outs[0].shape, 0)
        o_ref[...] = jnp.where(row < 64, outs[0], outs[1]).T
        _ride_wait(copies, step_id == nsteps - 1)

    return pl.pallas_call(
        body, name="attn_fwd", grid=grid,
        in_specs=[pl.BlockSpec((tq, 128), lambda b, p, i: (b * nq + i, p)),
                  pl.BlockSpec((1, s, 128), lambda b, p, i: (p // 2, b, 0)),
                  pl.BlockSpec((1, 128, s), lambda b, p, i: (p // 2, 0, b))] + rd.in_specs,
        out_specs=[pl.BlockSpec((tq, 128), lambda b, p, i: (b * nq + i, p)),
                   pl.BlockSpec((1, 2, 1, tq), lambda b, p, i: (b, p, 0, i))] + rd.out_specs,
        out_shape=[jax.ShapeDtypeStruct((t, D_ATTN), F32),
                   jax.ShapeDtypeStruct((nb, N_HEADS, 1, s), F32)] + rd.out_shape,
        scratch_shapes=rd.scratch,
        compiler_params=_cparams(("arbitrary", "arbitrary", "arbitrary")),
    )(q, kd, vdt, *rd.srcs)


KEY_BLOCKS = 2


def _attn_bwd(q, kd, vd, kdt, o, lse, do, nb, s, tq, ride=None):
    t = q.shape[0]
    nq = s // tq
    ones8 = jnp.ones((8, 128), BF16)
    rd = _ride_plan(ride)
    grid = (nb, N_KV, 2, nq)
    nsteps = nb * N_KV * 2 * nq

    def body(*refs):
        q_ref, k_ref, v_ref, kt_ref, o_ref, lse_ref, do_ref, ones_ref = refs[:8]
        dq_ref, dk_ref, dv_ref = refs[8 + rd.n:11 + rd.n]
        copies = rd.copies(refs[8:8 + rd.n], refs[11 + rd.n:11 + 2 * rd.n], refs[11 + 2 * rd.n:])
        step_id = _grid_step_id(grid)
        _ride_start(copies, step_id == 0)

        @pl.when((pl.program_id(2) == 0) & (pl.program_id(3) == 0))
        def _():
            dk_ref[...] = jnp.zeros_like(dk_ref)
            dv_ref[...] = jnp.zeros_like(dv_ref)

        qv = q_ref[...].astype(F32)
        dov = do_ref[...]
        ov = o_ref[...]
        lane = lax.broadcasted_iota(jnp.int32, qv.shape, 1)
        kb = s // KEY_BLOCKS
        dqs = []
        for half in range(2):
            sel = (lane < 64) if half == 0 else (lane >= 64)
            qh = jnp.where(sel, qv, 0.0).astype(BF16)
            doh = jnp.where(sel, dov, 0.0)
            dob = doh.astype(BF16)
            delta = None
            for part in _split_bf16(doh * ov, 3):
                d8 = _dot_nt(ones_ref[...], part)
                delta = d8 if delta is None else delta + d8
            delta = delta[0:1, :]
            dq_h = None
            for blk in range(KEY_BLOCKS):
                ks = slice(blk * kb, (blk + 1) * kb)
                st = _dot_nt(k_ref[0, ks, :], qh)
                pt = jnp.exp(st - lse_ref[0, half])
                dpt = _dot_nt(v_ref[0, ks, :], dob)
                dst = (pt * (dpt - delta)).astype(BF16)
                dk_ref[0, ks, :] += _dot(dst, qh)
                dv_ref[0, ks, :] += _dot(pt.astype(BF16), dob)
                part = _dot(kt_ref[0, :, ks], dst)
                dq_h = part if dq_h is None else dq_h + part
            dqs.append(dq_h)
        row = lax.broadcasted_iota(jnp.int32, dqs[0].shape, 0)
        dq_ref[...] = jnp.where(row < 64, dqs[0], dqs[1]).T
        _ride_wait(copies, step_id == nsteps - 1)

    qmap = lambda b, g, p, i: (b * nq + i, g * 2 + p)
    kvmap = lambda b, g, p, i: (g, b, 0)
    return pl.pallas_call(
        body, name="attn_bwd", grid=grid,
        in_specs=[pl.BlockSpec((tq, 128), qmap),
                  pl.BlockSpec((1, s, 128), kvmap),
                  pl.BlockSpec((1, s, 128), kvmap),
                  pl.BlockSpec((1, 128, s), lambda b, g, p, i: (g, 0, b)),
                  pl.BlockSpec((tq, 128), qmap),
                  pl.BlockSpec((1, 2, 1, tq), lambda b, g, p, i: (b, g * 2 + p, 0, i)),
                  pl.BlockSpec((tq, 128), qmap),
                  pl.BlockSpec((8, 128), lambda b, g, p, i: (0, 0))] + rd.in_specs,
        out_specs=[pl.BlockSpec((tq, 128), qmap),
                   pl.BlockSpec((1, s, 128), kvmap),
                   pl.BlockSpec((1, s, 128), kvmap)] + rd.out_specs,
        out_shape=[jax.ShapeDtypeStruct((t, D_ATTN), F32),
                   jax.ShapeDtypeStruct((N_KV, t, 128), F32),
                   jax.ShapeDtypeStruct((N_KV, t, 128), F32)] + rd.out_shape,
        scratch_shapes=rd.scratch,
        compiler_params=_cparams(("arbitrary", "arbitrary", "arbitrary", "arbitrary")),
    )(q, kd, vd, kdt, o, lse, do, ones8, *rd.srcs)


SUB = 16
N_SUB = CHUNK // SUB


def _tri_mats():
    i = np.arange(CHUNK)
    same = (i[:, None] // SUB) == (i[None, :] // SUB)
    lower = jnp.asarray(same & (i[:, None] >= i[None, :]), dtype=BF16)
    upper = jnp.asarray(same & (i[:, None] <= i[None, :]), dtype=BF16)
    return jnp.stack([lower, upper])


def _running_sum(tri, x):
    acc = None
    for part in _split_bf16(x, 3):
        t = _dot(tri, part)
        acc = t if acc is None else acc + t
    return acc


def _gates(z, lb):
    sig = _sigmoid(z)
    f = lb + (1.0 - lb) * sig
    logf = jnp.log(jnp.maximum(f, F_MIN))
    k = (1.0 - lb) * (1.0 - sig)
    return sig, f, logf, k


def _row_group(jg, anti):
    if anti:
        return 0, 8 * jg + 8
    return 8 * jg, SUB


def _sub_order(anti):
    return range(N_SUB - 1, -1, -1) if anti else range(N_SUB)


def _block_columns(b, anti):
    tt = lax.broadcasted_iota(jnp.int32, (SUB, LANES), 0)
    cols = []
    for jg in range(SUB // 8):
        r0, r1 = _row_group(jg, anti)
        br = b[r0:r1]
        tr = tt[r0:r1]
        for i in range(8):
            sc = 8 * jg + i
            mask = (tr <= sc) if anti else (tr >= sc)
            cols.append((r0, r1, sc, jnp.where(mask, jnp.exp(jnp.minimum(br - b[sc:sc + 1], 0.0)), 0.0)))
    return cols


def _lockstep(gens):
    results = [None] * len(gens)
    live = list(range(len(gens)))
    while live:
        for i in list(live):
            try:
                next(gens[i])
            except StopIteration as stop:
                results[i] = stop.value
                live.remove(i)
    return results


def _scatter_rows(base, accs):
    pieces = []
    for g in range(SUB // 8):
        tot = base[8 * g:8 * g + 8]
        for (r0, r1), acc in accs.items():
            if r0 <= 8 * g and 8 * g + 8 <= r1:
                tot = tot + acc[8 * g - r0:8 * g - r0 + 8]
        pieces.append(tot)
    return jnp.concatenate(pieces, axis=0)


def _chunk_fwd(q, k, v, b, st, bones, bmask, anti):
    rs = [slice(SUB * i, SUB * i + SUB) for i in range(N_SUB)]
    decay, update, prods, spans, qbs = [], [], [], [], []
    for i in range(N_SUB):
        qi, ki, vi, bi = q[rs[i]], k[rs[i]], v[rs[i]], b[rs[i]]
        b_last = bi[0:1] if anti else bi[SUB - 1:SUB]
        decay.append(jnp.exp(b_last))
        update.append(_dot_tn(vi.astype(BF16), (ki * jnp.exp(b_last - bi)).astype(BF16)) * bmask)
        qbs.append((qi * jnp.exp(bi)).astype(BF16))
        for r0, r1, sc, e in _block_columns(bi, anti):
            prods.append(qi[r0:r1] * e * ki[sc:sc + 1])
            spans.append((i, r0, r1, sc))
    pb = _dot(jnp.concatenate(prods, axis=0).astype(BF16), bones)
    yield
    entered = [None] * N_SUB
    for i in _sub_order(anti):
        entered[i] = st
        st = st * decay[i] + update[i]
    yield
    accs = [dict() for _ in range(N_SUB)]
    off = 0
    for i, r0, r1, sc in spans:
        term = pb[off:off + r1 - r0] * v[rs[i]][sc:sc + 1]
        off += r1 - r0
        accs[i][(r0, r1)] = term if (r0, r1) not in accs[i] else accs[i][(r0, r1)] + term
    outs = [_scatter_rows(_dot_nt(qbs[i], entered[i].astype(BF16)), accs[i]) for i in range(N_SUB)]
    return jnp.concatenate(outs, axis=0), st, entered


def _hgrn_fwd(proj, lb, gw, nb, s, ride=None):
    t = proj.shape[0]
    nc = s // CHUNK
    assert nc % 2 == 0
    tri = _tri_mats()
    bones = _block_ones(LANES, HEAD_DIM)
    rd = _ride_plan(ride)

    def body(*refs):
        q_ref, zf_ref, zb_ref, v_ref, g_ref, lb_ref, gw_ref, tri_ref, bones_ref = refs[:9]
        y_ref, os_ref, sts_ref = refs[9 + rd.n:12 + rd.n]
        st_ref = refs[12 + 2 * rd.n]
        copies = rd.copies(refs[9:9 + rd.n], refs[12 + rd.n:12 + 2 * rd.n], refs[13 + 2 * rd.n:])
        step_id = pl.program_id(0) * 2 + pl.program_id(1)
        _ride_start(copies, step_id == 0)
        bones_m = bones_ref[...]
        bmask = bones_m.astype(F32)
        st_ref[...] = jnp.zeros_like(st_ref)

        def one_direction(n, anti):
            side = 1 if anti else 0
            z_ref = zb_ref if anti else zf_ref
            cn = (nc - 1 - n) if anti else n
            rows = pl.ds(pl.multiple_of(cn * CHUNK, CHUNK), CHUNK)
            q = q_ref[rows, :]
            v = v_ref[rows, :]
            _, _, logf, k = _gates(z_ref[rows, :], lb_ref[side:side + 1])
            b = _running_sum(tri_ref[side], logf)
            yield
            o, st_new, entered = yield from _chunk_fwd(q, k, v, b, st_ref[side], bones_m, bmask, anti)
            for i in range(N_SUB):
                sts_ref[0, 0, side, cn * N_SUB + i] = entered[i].astype(BF16)
            st_ref[side] = st_new
            (y_ref if anti else os_ref)[rows, :] = o

        def step(n, carry):
            _lockstep([one_direction(n, False), one_direction(n, True)])
            return carry

        lax.fori_loop(0, nc, step, 0)

        def join(n, carry):
            rows = pl.ds(pl.multiple_of(n * (2 * CHUNK), 2 * CHUNK), 2 * CHUNK)
            osum = os_ref[rows, :] + y_ref[rows, :]
            os_ref[rows, :] = osum
            r = lax.rsqrt(_dot_precise(osum * osum, bones_m, 2) * (1.0 / HEAD_DIM) + EPS)
            hg = g_ref[rows, :]
            y_ref[rows, :] = osum * r * gw_ref[...] * (hg * _sigmoid(hg))
            return carry

        lax.fori_loop(0, nc // 2, join, 0)
        _ride_wait(copies, step_id == nb * 2 - 1)

    def col(c):
        return pl.BlockSpec((s, LANES), lambda b, p, c=c: (b, c + p))

    return pl.pallas_call(
        body, name="hgrn_fwd", grid=(nb, 2),
        in_specs=[col(COL_HQ), col(COL_ZFW), col(COL_ZBW), col(COL_HI), col(COL_HG),
                  pl.BlockSpec((2, LANES), lambda b, p: (0, p)),
                  pl.BlockSpec((1, LANES), lambda b, p: (0, 0)),
                  pl.BlockSpec((2, CHUNK, CHUNK), lambda b, p: (0, 0, 0)),
                  pl.BlockSpec((LANES, LANES), lambda b, p: (0, 0))] + rd.in_specs,
        out_specs=[pl.BlockSpec((s, LANES), lambda b, p: (b, p)),
                   pl.BlockSpec((s, LANES), lambda b, p: (b, p)),
                   pl.BlockSpec((1, 1, 2, nc * N_SUB, LANES, LANES), lambda b, p: (b, p, 0, 0, 0, 0))]
        + rd.out_specs,
        out_shape=[jax.ShapeDtypeStruct((t, D_HGRN), F32), jax.ShapeDtypeStruct((t, D_HGRN), F32),
                   jax.ShapeDtypeStruct((nb, 2, 2, nc * N_SUB, LANES, LANES), BF16)] + rd.out_shape,
        scratch_shapes=[pltpu.VMEM((2, LANES, LANES), F32)] + rd.scratch,
        compiler_params=_cparams(("arbitrary", "arbitrary")),
    )(proj, proj, proj, proj, proj, lb, gw, tri, bones, *rd.srcs)


def _chunk_bwd(q, k, v, b, do, states, rt, bones, bmask, anti):
    rs = [slice(SUB * i, SUB * i + SUB) for i in range(N_SUB)]
    r8 = lax.broadcasted_iota(jnp.int32, (8, LANES), 0)
    decay, update, dq_inter, ebls, prods_p, prods_d, spans, qes, kes = [], [], [], [], [], [], [], [], []
    for i in range(N_SUB):
        qi, ki, vi, bi, doi = q[rs[i]], k[rs[i]], v[rs[i]], b[rs[i]], do[rs[i]]
        b_last = bi[0:1] if anti else bi[SUB - 1:SUB]
        eb = jnp.exp(bi)
        dob = doi.astype(BF16)
        decay.append(jnp.exp(b_last))
        ebls.append(jnp.exp(b_last - bi))
        update.append(_dot_tn(dob, (qi * eb).astype(BF16)) * bmask)
        dq_inter.append(eb * _dot(dob, states[i]))
        for r0, r1, sc, e in _block_columns(bi, anti):
            qe = qi[r0:r1] * e
            qes.append(qe)
            kes.append(e * ki[sc:sc + 1])
            prods_p.append(qe * ki[sc:sc + 1])
            prods_d.append(doi[r0:r1] * vi[sc:sc + 1])
            spans.append((i, r0, r1, sc))
    sums = _dot(jnp.concatenate(prods_p + prods_d, axis=0).astype(BF16), bones)
    half = sum(r1 - r0 for _, r0, r1, _ in spans)
    yield
    entered = [None] * N_SUB
    for i in reversed(list(_sub_order(anti))):
        entered[i] = rt
        rt = rt * decay[i] + update[i]
    yield
    accs = [dict() for _ in range(N_SUB)]
    dk_blks = [[jnp.zeros((8, LANES), F32) for _ in range(SUB // 8)] for _ in range(N_SUB)]
    dv_blks = [[jnp.zeros((8, LANES), F32) for _ in range(SUB // 8)] for _ in range(N_SUB)]
    off = 0
    for n, (i, r0, r1, sc) in enumerate(spans):
        nr = r1 - r0
        pb = sums[off:off + nr]
        dpb = sums[half + off:half + off + nr]
        off += nr
        term = dpb * kes[n]
        accs[i][(r0, r1)] = term if (r0, r1) not in accs[i] else accs[i][(r0, r1)] + term
        dk_s = jnp.sum(dpb * qes[n], axis=0, keepdims=True)
        dv_s = jnp.sum(pb * do[rs[i]][r0:r1], axis=0, keepdims=True)
        dk_blks[i][sc // 8] = jnp.where(r8 == sc % 8, dk_s, dk_blks[i][sc // 8])
        dv_blks[i][sc // 8] = jnp.where(r8 == sc % 8, dv_s, dv_blks[i][sc // 8])
    dqs, dks, dvs, dbs = [], [], [], []
    for i in range(N_SUB):
        ki, vi = k[rs[i]], v[rs[i]]
        rtb = entered[i].astype(BF16)
        dk_inter = ebls[i] * _dot(vi.astype(BF16), rtb)
        dv_inter = _dot_nt((ki * ebls[i]).astype(BF16), rtb)
        dqs.append(_scatter_rows(dq_inter[i], accs[i]))
        dks.append(dk_inter + jnp.concatenate(dk_blks[i], axis=0))
        dvs.append(dv_inter + jnp.concatenate(dv_blks[i], axis=0))
        db_last = (jnp.sum(ki * dk_inter, axis=0, keepdims=True)
                   + decay[i] * jnp.sum(entered[i] * states[i].astype(F32), axis=0, keepdims=True))
        dbs.append(jnp.broadcast_to(db_last, (SUB, LANES)))
    cat = lambda xs: jnp.concatenate(xs, axis=0)
    return cat(dqs), cat(dks), cat(dvs), rt, cat(dbs)


def _hgrn_bwd(proj, lb, gw, osum, states, dy, nb, s, ride=None):
    t = proj.shape[0]
    nc = s // CHUNK
    assert nc % 2 == 0
    tri = _tri_mats()
    bones = _block_ones(LANES, HEAD_DIM)
    rd = _ride_plan(ride)

    def body(*refs):
        (q_ref, zf_ref, zb_ref, v_ref, g_ref, lb_ref, gw_ref, os_ref, sts_ref, dy_ref, tri_ref,
         bones_ref) = refs[:12]
        dq_ref, dzf_ref, dzb_ref, dv_ref, dg_ref, dgw_ref, dlb_ref = refs[12 + rd.n:19 + rd.n]
        do_sc, dq_sc, dv_sc, rt_cur = refs[19 + 2 * rd.n:23 + 2 * rd.n]
        copies = rd.copies(refs[12:12 + rd.n], refs[19 + rd.n:19 + 2 * rd.n], refs[23 + 2 * rd.n:])
        step_id = pl.program_id(0) * 2 + pl.program_id(1)
        _ride_start(copies, step_id == 0)
        bones_m = bones_ref[...]
        bmask = bones_m.astype(F32)
        gwv = gw_ref[...]

        def head(n, acc):
            rows = pl.ds(pl.multiple_of(n * (2 * CHUNK), 2 * CHUNK), 2 * CHUNK)
            o = os_ref[rows, :]
            hg = g_ref[rows, :]
            dyv = dy_ref[rows, :]
            sg = _sigmoid(hg)
            r = lax.rsqrt(_dot_precise(o * o, bones_m, 2) * (1.0 / HEAD_DIM) + EPS)
            nrm = o * r * gwv
            dn = dyv * (hg * sg)
            dg_ref[rows, :] = (dyv * nrm * (sg * (1.0 + hg * (1.0 - sg)))).astype(BF16)
            g = dn * gwv
            mean_go = _dot_precise(g * o, bones_m, 2) * (1.0 / HEAD_DIM)
            do_sc[rows, :] = r * (g - o * (r * r) * mean_go)
            return acc + jnp.sum(dn * o * r, axis=0, keepdims=True)

        dgw_ref[0] = lax.fori_loop(0, nc // 2, head, jnp.zeros((1, LANES), F32))
        dq_sc[...] = jnp.zeros_like(dq_sc)
        dv_sc[...] = jnp.zeros_like(dv_sc)

        rt_cur[...] = jnp.zeros_like(rt_cur)

        def one_direction(n, anti):
            side = 1 if anti else 0
            z_ref = zb_ref if anti else zf_ref
            dz_ref = dzb_ref if anti else dzf_ref
            lbv = lb_ref[side:side + 1]
            cn = n if anti else (nc - 1 - n)
            rows = pl.ds(pl.multiple_of(cn * CHUNK, CHUNK), CHUNK)
            q = q_ref[rows, :]
            v = v_ref[rows, :]
            sig, f, logf, k = _gates(z_ref[rows, :], lbv)
            b = _running_sum(tri_ref[side], logf)
            yield
            do = do_sc[rows, :]
            entered = [sts_ref[0, 0, side, cn * N_SUB + i] for i in range(N_SUB)]
            dq, dk, dv, rt_new, db_last = yield from _chunk_bwd(q, k, v, b, do, entered, rt_cur[side], bones_m,
                                                                bmask, anti)
            rt_cur[side] = rt_new
            dq_sc[rows, :] += dq
            dv_sc[rows, :] += dv
            dlogf = _running_sum(tri_ref[1 - side], q * dq - k * dk) + db_last
            dfl = jnp.where(f > F_MIN, dlogf / f, 0.0)
            dz_ref[rows, :] = ((dfl - dk) * (1.0 - lbv) * sig * (1.0 - sig)).astype(BF16)
            return jnp.sum((dfl - dk) * (1.0 - sig), axis=0, keepdims=True)

        def back(n, dlb):
            d0, d1 = _lockstep([one_direction(n, False), one_direction(n, True)])
            return dlb[0] + d0, dlb[1] + d1

        zero = jnp.zeros((1, LANES), F32)
        dlb0, dlb1 = lax.fori_loop(0, nc, back, (zero, zero))
        dlb_ref[0, 0:1, :] = dlb0
        dlb_ref[0, 1:2, :] = dlb1

        dq_ref[...] = dq_sc[...].astype(BF16)
        dv_ref[...] = dv_sc[...].astype(BF16)
        _ride_wait(copies, step_id == nb * 2 - 1)

    def col(c):
        return pl.BlockSpec((s, LANES), lambda b, p, c=c: (b, c + p))

    sl = pl.BlockSpec((s, LANES), lambda b, p: (b, p))
    out_t = jax.ShapeDtypeStruct((t, D_HGRN), BF16)
    return pl.pallas_call(
        body, name="hgrn_bwd", grid=(nb, 2),
        in_specs=[col(COL_HQ), col(COL_ZFW), col(COL_ZBW), col(COL_HI), col(COL_HG),
                  pl.BlockSpec((2, LANES), lambda b, p: (0, p)),
                  pl.BlockSpec((1, LANES), lambda b, p: (0, 0)),
                  sl,
                  pl.BlockSpec((1, 1, 2, nc * N_SUB, LANES, LANES), lambda b, p: (b, p, 0, 0, 0, 0)),
                  sl,
                  pl.BlockSpec((2, CHUNK, CHUNK), lambda b, p: (0, 0, 0)),
                  pl.BlockSpec((LANES, LANES), lambda b, p: (0, 0))] + rd.in_specs,
        out_specs=[sl, sl, sl, sl, sl,
                   pl.BlockSpec((1, 1, LANES), lambda b, p: (b, 0, p)),
                   pl.BlockSpec((1, 2, LANES), lambda b, p: (b, 0, p))] + rd.out_specs,
        out_shape=[out_t, out_t, out_t, out_t, out_t,
                   jax.ShapeDtypeStruct((nb, 1, D_HGRN), F32),
                   jax.ShapeDtypeStruct((nb, 2, D_HGRN), F32)] + rd.out_shape,
        scratch_shapes=[pltpu.VMEM((s, LANES), F32), pltpu.VMEM((s, LANES), F32), pltpu.VMEM((s, LANES), F32),
                        pltpu.VMEM((2, LANES, LANES), F32)] + rd.scratch,
        compiler_params=_cparams(("arbitrary", "arbitrary")),
    )(proj, proj, proj, proj, proj, lb, gw, osum, states, dy, tri, bones, *rd.srcs)


def _lower_bounds(logits):
    def body(lg_ref, lb_ref):
        rows = [lg_ref[l:l + 1, :] for l in range(DEPTH)]
        m = functools.reduce(jnp.maximum, rows)
        ex = [jnp.exp(r - m) for r in rows]
        den = functools.reduce(jnp.add, ex)
        run = jnp.zeros_like(m)
        for l in range(DEPTH):
            if l > 0:
                run = run + ex[l] / den
            lb_ref[l:l + 1, :] = run

    return pl.pallas_call(body, name="lower_bounds", out_shape=jax.ShapeDtypeStruct(logits.shape, F32))(logits)


def _lower_bounds_bwd(logits, dlb):
    def body(lg_ref, dlb_ref, dlg_ref):
        rows = [lg_ref[l:l + 1, :] for l in range(DEPTH)]
        m = functools.reduce(jnp.maximum, rows)
        ex = [jnp.exp(r - m) for r in rows]
        den = functools.reduce(jnp.add, ex)
        sm = [e / den for e in ex]
        dsm = [jnp.zeros_like(m) for _ in range(DEPTH)]
        for i in range(1, DEPTH):
            for l in range(i, DEPTH):
                dsm[i] = dsm[i] + dlb_ref[l:l + 1, :]
        dot = functools.reduce(jnp.add, [sm[i] * dsm[i] for i in range(DEPTH)])
        for i in range(DEPTH):
            dlg_ref[i:i + 1, :] = sm[i] * (dsm[i] - dot)

    return pl.pallas_call(body, name="lower_bounds_bwd", out_shape=jax.ShapeDtypeStruct(logits.shape, F32))(logits, dlb)


CONV_ROWS = 128


def _conv_core(a, bg, dww, dwb, lnw, lnb, upad_ref, s):
    sb = _sigmoid(bg)
    u = a * sb
    upad_ref[0:16, :] = jnp.zeros((16, D_CONV), F32)
    upad_ref[16:16 + s, :] = u
    upad_ref[16 + s:32 + s, :] = jnp.zeros((16, D_CONV), F32)
    rows = min(s, CONV_ROWS)
    pieces = []
    for r0 in range(0, s, rows):
        acc = None
        for j in range(CONV_W):
            term = upad_ref[r0 + 1 + j:r0 + 1 + j + rows, :] * dww[j:j + 1, :]
            acc = term if acc is None else acc + term
        pieces.append(acc)
    c = jnp.concatenate(pieces, axis=0) + dwb
    mu = jnp.mean(c, axis=-1, keepdims=True)
    xc = c - mu
    rstd = lax.rsqrt(jnp.mean(xc * xc, axis=-1, keepdims=True) + LN_EPS)
    nh = xc * rstd
    l = nh * lnw + lnb
    sl = _sigmoid(l)
    return sb, nh, rstd, l, sl


def _conv_fwd(proj, dww, dwb, lnw, lnb, pww, pwb, nb, s):
    t = proj.shape[0]
    assert s % min(s, CONV_ROWS) == 0

    def body(a_ref, b_ref, dww_ref, dwb_ref, lnw_ref, lnb_ref, pww_ref, pwb_ref, y_ref, upad_ref):
        _, _, _, l, sl = _conv_core(a_ref[...], b_ref[...], dww_ref[...], dwb_ref[...], lnw_ref[...],
                                    lnb_ref[...], upad_ref, s)
        y_ref[...] = _dot((l * sl).astype(BF16), pww_ref[...]) + pwb_ref[...]

    vec = pl.BlockSpec((1, D_CONV), lambda b: (0, 0))
    return pl.pallas_call(
        body, name="conv_fwd", grid=(nb,),
        in_specs=[pl.BlockSpec((s, D_CONV), lambda b: (b, COL_CA)),
                  pl.BlockSpec((s, D_CONV), lambda b: (b, COL_CB)),
                  pl.BlockSpec((32, D_CONV), lambda b: (0, 0)), vec, vec, vec,
                  pl.BlockSpec((D_CONV, D_CONV), lambda b: (0, 0)), vec],
        out_specs=pl.BlockSpec((s, D_CONV), lambda b: (b, 0)),
        out_shape=jax.ShapeDtypeStruct((t, D_CONV), F32),
        scratch_shapes=[pltpu.VMEM((s + 32, D_CONV), F32)],
        compiler_params=_cparams(("parallel",)),
    )(proj, proj, dww, dwb, lnw, lnb, pww, pwb)


def _conv_bwd(proj, dy, dww, dwb, lnw, lnb, pww, nb, s):
    t = proj.shape[0]

    def body(a_ref, b_ref, dy_ref, dww_ref, dwb_ref, lnw_ref, lnb_ref, pww_ref,
             dab_ref, ddww_ref, ddwb_ref, dlnw_ref, dlnb_ref, dpww_ref, dpwb_ref, upad_ref, dcpad_ref):
        a = a_ref[...]
        dww = dww_ref[...]
        sb, nh, rstd, l, sl = _conv_core(a, b_ref[...], dww, dwb_ref[...], lnw_ref[...], lnb_ref[...],
                                         upad_ref, s)
        dyv = dy_ref[...]
        dyb = dyv.astype(BF16)
        ds = _dot_nt(dyb, pww_ref[...])
        dl = ds * (sl * (1.0 + l * (1.0 - sl)))
        dn = dl * lnw_ref[...]
        dc = rstd * (dn - jnp.mean(dn, axis=-1, keepdims=True)
                     - nh * jnp.mean(dn * nh, axis=-1, keepdims=True))

        @pl.when(pl.program_id(0) == 0)
        def _():
            for r in (ddww_ref, ddwb_ref, dlnw_ref, dlnb_ref, dpww_ref, dpwb_ref):
                r[...] = jnp.zeros_like(r)

        dpww_ref[...] += _dot_tn((l * sl).astype(BF16), dyb)
        dpwb_ref[...] += jnp.sum(dyv, axis=0, keepdims=True)
        dlnw_ref[...] += jnp.sum(dl * nh, axis=0, keepdims=True)
        dlnb_ref[...] += jnp.sum(dl, axis=0, keepdims=True)
        ddwb_ref[...] += jnp.sum(dc, axis=0, keepdims=True)

        dcpad_ref[0:16, :] = jnp.zeros((16, D_CONV), F32)
        dcpad_ref[16:16 + s, :] = dc
        dcpad_ref[16 + s:32 + s, :] = jnp.zeros((16, D_CONV), F32)
        rows = min(s, CONV_ROWS)
        r8 = lax.broadcasted_iota(jnp.int32, (32, D_CONV), 0)
        ddww = jnp.zeros((32, D_CONV), F32)
        pieces = []
        for r0 in range(0, s, rows):
            acc = None
            dcr = dcpad_ref[16 + r0:16 + r0 + rows, :]
            for j in range(CONV_W):
                term = dcpad_ref[r0 + 31 - j:r0 + 31 - j + rows, :] * dww[j:j + 1, :]
                acc = term if acc is None else acc + term
                wj = jnp.sum(dcr * upad_ref[r0 + 1 + j:r0 + 1 + j + rows, :], axis=0, keepdims=True)
                ddww = ddww + jnp.where(r8 == j, wj, 0.0)
            pieces.append(acc)
        du = jnp.concatenate(pieces, axis=0)
        ddww_ref[...] += ddww
        dab_ref[:, 0:D_CONV] = (du * sb).astype(BF16)
        dab_ref[:, D_CONV:2 * D_CONV] = (du * a * sb * (1.0 - sb)).astype(BF16)

    vec = pl.BlockSpec((1, D_CONV), lambda b: (0, 0))
    mat = pl.BlockSpec((D_CONV, D_CONV), lambda b: (0, 0))
    w32 = pl.BlockSpec((32, D_CONV), lambda b: (0, 0))
    vshape = jax.ShapeDtypeStruct((1, D_CONV), F32)
    return pl.pallas_call(
        body, name="conv_bwd", grid=(nb,),
        in_specs=[pl.BlockSpec((s, D_CONV), lambda b: (b, COL_CA)),
                  pl.BlockSpec((s, D_CONV), lambda b: (b, COL_CB)),
                  pl.BlockSpec((s, D_CONV), lambda b: (b, 0)),
                  w32, vec, vec, vec, mat],
        out_specs=[pl.BlockSpec((s, 2 * D_CONV), lambda b: (b, 0)), w32, vec, vec, vec, mat, vec],
        out_shape=[jax.ShapeDtypeStruct((t, 2 * D_CONV), BF16),
                   jax.ShapeDtypeStruct((32, D_CONV), F32), vshape, vshape, vshape,
                   jax.ShapeDtypeStruct((D_CONV, D_CONV), F32), vshape],
        scratch_shapes=[pltpu.VMEM((s + 32, D_CONV), F32), pltpu.VMEM((s + 32, D_CONV), F32)],
        compiler_params=_cparams(("arbitrary",)),
    )(proj, proj, dy, dww, dwb, lnw, lnb, pww)


def _mix_out(o_attn, y_hgrn, y_conv, x, aw, cw, w_out, tm):
    t = x.shape[0]

    def body(o_ref, h_ref, c_ref, x_ref, aw_ref, cw_ref, w_ref, mixed_ref, x1_ref):
        o = o_ref[...]
        a = o * lax.rsqrt(jnp.mean(o * o, axis=-1, keepdims=True) + EPS) * aw_ref[...]
        yc = c_ref[...]
        c = yc * lax.rsqrt(jnp.mean(yc * yc, axis=-1, keepdims=True) + EPS) * cw_ref[...]
        ab, hb, cb = a.astype(BF16), h_ref[...].astype(BF16), c.astype(BF16)
        mixed_ref[:, 0:512] = ab
        mixed_ref[:, 512:768] = hb
        mixed_ref[:, 768:1024] = cb
        x1_ref[...] = (x_ref[...] + _dot(ab, w_ref[0:512, :]) + _dot(hb, w_ref[512:768, :])
                       + _dot(cb, w_ref[768:1024, :]))

    def tok(w):
        return pl.BlockSpec((tm, w), lambda i: (i, 0))

    return pl.pallas_call(
        body, name="mix_out", grid=(t // tm,),
        in_specs=[tok(512), tok(256), tok(256), tok(D_MODEL),
                  pl.BlockSpec((1, 512), lambda i: (0, 0)), pl.BlockSpec((1, 256), lambda i: (0, 0)),
                  pl.BlockSpec((D_MODEL, D_MODEL), lambda i: (0, 0))],
        out_specs=[tok(D_MODEL), tok(D_MODEL)],
        out_shape=[jax.ShapeDtypeStruct((t, D_MODEL), BF16), jax.ShapeDtypeStruct((t, D_MODEL), F32)],
        compiler_params=_cparams(("parallel",)),
    )(o_attn, y_hgrn, y_conv, x, aw, cw, w_out)


def _mix_out_bwd(dx1, w_out, o_attn, y_conv, aw, cw, tm):
    t = dx1.shape[0]

    def body(dx_ref, w_ref, o_ref, c_ref, aw_ref, cw_ref, do_ref, dh_ref, dc_ref, daw_ref, dcw_ref):
        dm = _dot_nt(dx_ref[...].astype(BF16), w_ref[...])
        do, daw = _rms_bwd(dm[:, 0:512], o_ref[...], aw_ref[...])
        dc, dcw = _rms_bwd(dm[:, 768:1024], c_ref[...], cw_ref[...])
        do_ref[...] = do
        dh_ref[...] = dm[:, 512:768]
        dc_ref[...] = dc

        @pl.when(pl.program_id(0) == 0)
        def _():
            daw_ref[...] = jnp.zeros_like(daw_ref)
            dcw_ref[...] = jnp.zeros_like(dcw_ref)

        daw_ref[...] += jnp.sum(daw, axis=0, keepdims=True)
        dcw_ref[...] += jnp.sum(dcw, axis=0, keepdims=True)

    def tok(w):
        return pl.BlockSpec((tm, w), lambda i: (i, 0))

    v512 = pl.BlockSpec((1, 512), lambda i: (0, 0))
    v256 = pl.BlockSpec((1, 256), lambda i: (0, 0))
    return pl.pallas_call(
        body, name="mix_out_bwd", grid=(t // tm,),
        in_specs=[tok(D_MODEL), pl.BlockSpec((D_MODEL, D_MODEL), lambda i: (0, 0)), tok(512), tok(256),
                  v512, v256],
        out_specs=[tok(512), tok(256), tok(256), v512, v256],
        out_shape=[jax.ShapeDtypeStruct((t, 512), F32), jax.ShapeDtypeStruct((t, 256), F32),
                   jax.ShapeDtypeStruct((t, 256), F32), jax.ShapeDtypeStruct((1, 512), F32),
                   jax.ShapeDtypeStruct((1, 256), F32)],
        compiler_params=_cparams(("arbitrary",)),
    )(dx1, w_out, o_attn, y_conv, aw, cw)


FF_BLOCKS = 4


def _ffn_fwd(x1, fw, wg, wu, wd, tm, ride=None):
    t = x1.shape[0]
    fb = wg.shape[1]
    nf = N_DEV // FF_BLOCKS
    rd = _ride_plan(ride)
    grid = (t // tm, nf)

    def body(*refs):
        x_ref, fw_ref, wg_ref, wu_ref, wd_ref = refs[:5]
        h_ref, g_ref, u_ref, a_ref, x2_ref = refs[5 + rd.n:10 + rd.n]
        acc_ref = refs[10 + 2 * rd.n]
        copies = rd.copies(refs[5:5 + rd.n], refs[10 + rd.n:10 + 2 * rd.n], refs[11 + 2 * rd.n:])
        step_id = _grid_step_id(grid)
        _ride_start(copies, step_id == 0)
        j = pl.program_id(1)

        @pl.when(j == 0)
        def _():
            xv = x_ref[...]
            r = lax.rsqrt(jnp.mean(xv * xv, axis=-1, keepdims=True) + EPS)
            h_ref[...] = (xv * r * fw_ref[...]).astype(BF16)
            acc_ref[...] = xv

        h = h_ref[...]
        out = None
        for c in range(FF_BLOCKS):
            g = _dot_nt(h, wg_ref[c])
            u = _dot_nt(h, wu_ref[c])
            a = (g * _sigmoid(g) * u).astype(BF16)
            g_ref[c] = g.astype(BF16)
            u_ref[c] = u.astype(BF16)
            a_ref[c] = a
            part = _dot(a, wd_ref[c])
            out = part if out is None else out + part
        acc_ref[...] += out

        @pl.when(j == nf - 1)
        def _():
            x2_ref[...] = acc_ref[...]

        _ride_wait(copies, step_id == (t // tm) * nf - 1)

    tok = pl.BlockSpec((tm, D_MODEL), lambda i, j: (i, 0))
    ffb = pl.BlockSpec((FF_BLOCKS, tm, fb), lambda i, j: (j, i, 0))
    ffs = jax.ShapeDtypeStruct((N_DEV, t, fb), BF16)
    return pl.pallas_call(
        body, name="ffn_fwd", grid=grid,
        in_specs=[tok, pl.BlockSpec((1, D_MODEL), lambda i, j: (0, 0)),
                  pl.BlockSpec((FF_BLOCKS, fb, D_MODEL), lambda i, j: (j, 0, 0)),
                  pl.BlockSpec((FF_BLOCKS, fb, D_MODEL), lambda i, j: (j, 0, 0)),
                  pl.BlockSpec((FF_BLOCKS, fb, D_MODEL), lambda i, j: (j, 0, 0))] + rd.in_specs,
        out_specs=[tok, ffb, ffb, ffb, tok] + rd.out_specs,
        out_shape=[jax.ShapeDtypeStruct((t, D_MODEL), BF16), ffs, ffs, ffs,
                   jax.ShapeDtypeStruct((t, D_MODEL), F32)] + rd.out_shape,
        scratch_shapes=[pltpu.VMEM((tm, D_MODEL), F32)] + rd.scratch,
        compiler_params=_cparams(("arbitrary", "arbitrary")),
    )(x1, fw, wg, wu, wd, *rd.srcs)


def _ffn_bwd(dx2, g, u, wg, wu, wd, x1, fw, tm, ride=None):
    t = dx2.shape[0]
    fb = wg.shape[1]
    nf = N_DEV // FF_BLOCKS
    rd = _ride_plan(ride)
    grid = (t // tm, nf)

    def body(*refs):
        dx_ref, g_ref, u_ref, wg_ref, wu_ref, wd_ref, x_ref, fw_ref = refs[:8]
        dg_ref, du_ref, dx1_ref, dfw_ref = refs[8 + rd.n:12 + rd.n]
        acc_ref = refs[12 + 2 * rd.n]
        copies = rd.copies(refs[8:8 + rd.n], refs[12 + rd.n:12 + 2 * rd.n], refs[13 + 2 * rd.n:])
        step_id = _grid_step_id(grid)
        _ride_start(copies, step_id == 0)
        i = pl.program_id(0)
        j = pl.program_id(1)
        dxb = dx_ref[...].astype(BF16)
        dh = None
        for c in range(FF_BLOCKS):
            da = _dot_nt(dxb, wd_ref[c])
            gv = g_ref[c].astype(F32)
            uv = u_ref[c].astype(F32)
            sg = _sigmoid(gv)
            dg = (da * uv * (sg * (1.0 + gv * (1.0 - sg)))).astype(BF16)
            du = (da * gv * sg).astype(BF16)
            dg_ref[c] = dg
            du_ref[c] = du
            part = _dot(dg, wg_ref[c]) + _dot(du, wu_ref[c])
            dh = part if dh is None else dh + part

        @pl.when(j == 0)
        def _():
            acc_ref[...] = dh

        @pl.when(j > 0)
        def _():
            acc_ref[...] += dh

        @pl.when((i == 0) & (j == 0))
        def _():
            dfw_ref[...] = jnp.zeros_like(dfw_ref)

        @pl.when(j == nf - 1)
        def _():
            dx, dfw = _rms_bwd(acc_ref[...], x_ref[...], fw_ref[...])
            dx1_ref[...] = dx_ref[...] + dx
            dfw_ref[...] += jnp.sum(dfw, axis=0, keepdims=True)

        _ride_wait(copies, step_id == (t // tm) * nf - 1)

    tok = pl.BlockSpec((tm, D_MODEL), lambda i, j: (i, 0))
    ffb = pl.BlockSpec((FF_BLOCKS, tm, fb), lambda i, j: (j, i, 0))
    ffs = jax.ShapeDtypeStruct((N_DEV, t, fb), BF16)
    vec = pl.BlockSpec((1, D_MODEL), lambda i, j: (0, 0))
    return pl.pallas_call(
        body, name="ffn_bwd", grid=grid,
        in_specs=[tok, ffb, ffb,
                  pl.BlockSpec((FF_BLOCKS, fb, D_MODEL), lambda i, j: (j, 0, 0)),
                  pl.BlockSpec((FF_BLOCKS, fb, D_MODEL), lambda i, j: (j, 0, 0)),
                  pl.BlockSpec((FF_BLOCKS, fb, D_MODEL), lambda i, j: (j, 0, 0)),
                  tok, vec] + rd.in_specs,
        out_specs=[ffb, ffb, tok, vec] + rd.out_specs,
        out_shape=[ffs, ffs, jax.ShapeDtypeStruct((t, D_MODEL), F32),
                   jax.ShapeDtypeStruct((1, D_MODEL), F32)] + rd.out_shape,
        scratch_shapes=[pltpu.VMEM((tm, D_MODEL), F32)] + rd.scratch,
        compiler_params=_cparams(("arbitrary", "arbitrary")),
    )(dx2, g, u, wg, wu, wd, x1, fw, *rd.srcs)


def _loss_grad(y, target, tm):
    t, d = y.shape

    def body(y_ref, t_ref, dy_ref, loss_ref):
        err = y_ref[...] - t_ref[...]
        dy_ref[...] = err * (1.0 / d)

        @pl.when(pl.program_id(0) == 0)
        def _():
            loss_ref[...] = jnp.zeros_like(loss_ref)

        part = jnp.sum(jnp.sum(err * err, axis=-1, keepdims=True), axis=0, keepdims=True)
        loss_ref[...] += part * (0.5 / d)

    tok = pl.BlockSpec((tm, d), lambda i: (i, 0))
    return pl.pallas_call(
        body, name="loss_grad", grid=(t // tm,),
        in_specs=[tok, tok],
        out_specs=[tok, pl.BlockSpec((1, 1), lambda i: (0, 0))],
        out_shape=[jax.ShapeDtypeStruct((t, d), F32), jax.ShapeDtypeStruct((1, 1), F32)],
        compiler_params=_cparams(("arbitrary",)),
    )(y, target)


def _tile(v, reps):
    return jnp.tile(v.reshape(1, -1), (1, reps))


class _LocalPlan:
    def __init__(self, wb):
        self.w = [{n: wb[n][l] for n in BIG_AXIS} for l in range(DEPTH)]

    def ride(self, kernel_name, l, grads=None):
        return None

    def done(self, kernel_name, l, outs):
        pass


def _local_step(x, target, p, plan):
    nb, s, d = x.shape
    t = nb * s
    tm = min(512, s)
    tq = min(1024, s)
    xf = x.reshape(t, d)
    cosq, sinq = _rope_tables(s)
    ones512 = _block_ones(512, HEAD_DIM)
    lbs = _lower_bounds(p["hgrn_lb_logits"].reshape(DEPTH, 2 * D_HGRN)).reshape(DEPTH, 2, D_HGRN)

    saved = []
    cur = xf
    wb = plan.w
    for l in range(DEPTH):
        qw = _tile(p["q_norm_w"][l], N_HEADS)
        kw = _tile(p["k_norm_w"][l], N_KV)
        gw = _tile(p["hgrn_gnorm_w"][l], 2)
        dww = jnp.pad(p["conv_dw_w"][l], ((0, 1), (0, 0)))
        pww = p["conv_pw_w"][l].astype(BF16)
        h0, proj = _rms_proj(cur, _row(p["mix_norm_w"][l]), wb[l]["w_in"], tm)
        qr, kd, vd, kdt, vdt = _qkv_prep(proj, cosq, sinq, qw, kw, ones512, s, tm)
        o_attn, lse, *rode = _attn_fwd(qr, kd, vdt, nb, s, tq, plan.ride("attn_fwd", l))
        plan.done("attn_fwd", l, rode)
        y_hgrn, osum, states, *rode = _hgrn_fwd(proj, lbs[l], gw, nb, s, plan.ride("hgrn_fwd", l))
        plan.done("hgrn_fwd", l, rode)
        y_conv = _conv_fwd(proj, dww, _row(p["conv_dw_b"][l]), _row(p["conv_ln_w"][l]),
                           _row(p["conv_ln_b"][l]), pww, _row(p["conv_pw_b"][l]), nb, s)
        mixed, x1 = _mix_out(o_attn, y_hgrn, y_conv, cur, _row(p["attn_out_norm_w"][l]),
                             _row(p["conv_out_norm_w"][l]), wb[l]["w_out"], tm)
        hf, g, u, a, x2, *rode = _ffn_fwd(x1, _row(p["ffn_norm_w"][l]), wb[l]["w_gate"], wb[l]["w_up"],
                                          wb[l]["w_down"], tm, plan.ride("ffn_fwd", l))
        plan.done("ffn_fwd", l, rode)
        saved.append(dict(x=cur, h0=h0, proj=proj, qr=qr, kd=kd, vd=vd, kdt=kdt, o_attn=o_attn, lse=lse,
                          osum=osum, states=states, y_conv=y_conv, mixed=mixed, x1=x1, hf=hf, g=g, u=u, a=a,
                          qw=qw, kw=kw, gw=gw, dww=dww, pww=pww))
        cur = x2

    dcur, loss = _loss_grad(cur, target.reshape(t, d), tm)

    grads = {k: [None] * DEPTH for k in WEIGHTS}
    dlb = [None] * DEPTH
    for l in reversed(range(DEPTH)):
        sv = saved[l]
        dg, du, dx1, dfw, *rode = _ffn_bwd(dcur, sv["g"], sv["u"], wb[l]["w_gate"], wb[l]["w_up"],
                                           wb[l]["w_down"], sv["x1"], _row(p["ffn_norm_w"][l]), tm,
                                           plan.ride("ffn_bwd", l, grads))
        plan.done("ffn_bwd", l, rode)
        grads["ffn_norm_w"][l] = dfw[0]
        grads["w_gate"][l] = _dw_ff(dg, sv["hf"], "dw_gate", tm)
        grads["w_up"][l] = _dw_ff(du, sv["hf"], "dw_up", tm)
        grads["w_down"][l] = _dw_ff(sv["a"], dcur, "dw_down", tm)
        do_attn, dy_hgrn, dy_conv, daw, dcw = _mix_out_bwd(
            dx1, wb[l]["w_out"], sv["o_attn"], sv["y_conv"], _row(p["attn_out_norm_w"][l]),
            _row(p["conv_out_norm_w"][l]), tm)
        grads["attn_out_norm_w"][l] = daw[0]
        grads["conv_out_norm_w"][l] = dcw[0]
        grads["w_out"][l] = _mm_tn(sv["mixed"], dx1, D_MODEL, "dw_out", tm)
        dq, dkd, dvd, *rode = _attn_bwd(sv["qr"], sv["kd"], sv["vd"], sv["kdt"], sv["o_attn"], sv["lse"], do_attn,
                                        nb, s, tq, plan.ride("attn_bwd", l, grads))
        plan.done("attn_bwd", l, rode)
        dqkv, dqw, dkw = _qkv_bwd(sv["proj"], dq, dkd, dvd, cosq, sinq, sv["qw"], sv["kw"], ones512, s, tm)
        grads["q_norm_w"][l] = dqw.reshape(N_HEADS, HEAD_DIM).sum(0)
        grads["k_norm_w"][l] = dkw.reshape(N_KV, HEAD_DIM).sum(0)
        dhq, dzf, dzb, dhi, dhg, dgw, dlb_l, *rode = _hgrn_bwd(sv["proj"], lbs[l], sv["gw"], sv["osum"],
                                                               sv["states"], dy_hgrn, nb, s,
                                                               plan.ride("hgrn_bwd", l, grads))
        plan.done("hgrn_bwd", l, rode)
        grads["hgrn_gnorm_w"][l] = dgw.reshape(nb * D_HGRN // HEAD_DIM, HEAD_DIM).sum(0)
        dlb[l] = dlb_l.sum(0)
        dab, ddww, ddwb, dlnw, dlnb, dpww, dpwb = _conv_bwd(
            sv["proj"], dy_conv, sv["dww"], _row(p["conv_dw_b"][l]), _row(p["conv_ln_w"][l]),
            _row(p["conv_ln_b"][l]), sv["pww"], nb, s)
        grads["conv_dw_w"][l] = ddww[:CONV_W]
        grads["conv_dw_b"][l] = ddwb[0]
        grads["conv_ln_w"][l] = dlnw[0]
        grads["conv_ln_b"][l] = dlnb[0]
        grads["conv_pw_w"][l] = dpww
        grads["conv_pw_b"][l] = dpwb[0]
        pieces = [dqkv, dhq, dzf, dzb, dhi, dhg, dab]
        grads["w_in"][l] = _dw_in(sv["h0"], pieces, tm)
        dcur, dnw, *rode = _proj_bwd(pieces, wb[l]["w_in"], sv["x"], _row(p["mix_norm_w"][l]), dx1, tm,
                                     plan.ride("proj_bwd", l, grads))
        plan.done("proj_bwd", l, rode)
        grads["mix_norm_w"][l] = dnw[0]

    dlog = _lower_bounds_bwd(p["hgrn_lb_logits"].reshape(DEPTH, 2 * D_HGRN),
                             jnp.stack(dlb).reshape(DEPTH, 2 * D_HGRN))
    out = {k: (v if k in BIG_AXIS else jnp.stack(v)) for k, v in grads.items() if k != "hgrn_lb_logits"}
    out["hgrn_lb_logits"] = dlog.reshape(DEPTH, 2, D_HGRN)
    return loss, dcur.reshape(nb, s, d), out


BIG_AXIS = {"w_in": 2, "w_out": 1, "w_gate": 2, "w_up": 2, "w_down": 1}
SMALL_SHARD_AXIS = {"hgrn_lb_logits": 2, "conv_dw_w": 2, "conv_pw_w": 1}
WEIGHTS = ("mix_norm_w", "w_in", "q_norm_w", "k_norm_w", "hgrn_lb_logits", "hgrn_gnorm_w", "conv_dw_w",
           "conv_dw_b", "conv_ln_w", "conv_ln_b", "conv_pw_w", "conv_pw_b", "attn_out_norm_w",
           "conv_out_norm_w", "w_out", "ffn_norm_w", "w_gate", "w_up", "w_down")
SMALL = tuple(n for n in WEIGHTS if n not in BIG_AXIS)


def _my_index():
    return 4 * lax.axis_index("x") + 2 * lax.axis_index("y") + lax.axis_index("c")


class _RidePlan:
    def __init__(self, srcs, gather):
        self.srcs = list(srcs)
        self.n = len(self.srcs)
        self.gather = list(gather) if isinstance(gather, (list, tuple)) else [gather] * self.n
        any_spec = pl.BlockSpec(memory_space=pl.ANY)
        self.in_specs = [any_spec] * self.n
        self.out_specs = [any_spec] * self.n
        self.out_shape = [jax.ShapeDtypeStruct(((N_DEV,) + s.shape) if g else s.shape, s.dtype)
                          for s, g in zip(self.srcs, self.gather)]
        npeer = N_DEV - 1
        self.scratch = [pltpu.SemaphoreType.DMA((self.n * npeer,)), pltpu.SemaphoreType.DMA((self.n * npeer,)),
                        pltpu.SemaphoreType.DMA((self.n,))] if self.n else []

    def copies(self, src_refs, out_refs, sems):
        if not self.n:
            return [], [], []
        send_sems, recv_sems, local_sems = sems
        npeer = N_DEV - 1
        x, y, c = lax.axis_index("x"), lax.axis_index("y"), lax.axis_index("c")
        me = 4 * x + 2 * y + c
        locals_, sends, recvs = [], [], []
        for a in range(self.n):
            src_ref, out_ref = src_refs[a], out_refs[a]

            def rows_for(j, src_ref=src_ref, gather=self.gather[a]):
                return src_ref if gather else src_ref.at[j]

            locals_.append(pltpu.make_async_copy(rows_for(me), out_ref.at[me], local_sems.at[a]))
            for k in range(1, N_DEV):
                px = (1 - x) if (k & 4) else x
                py = (1 - y) if (k & 2) else y
                pc = (1 - c) if (k & 1) else c
                pidx = 4 * px + 2 * py + pc
                common = dict(send_sem=send_sems.at[a * npeer + k - 1], recv_sem=recv_sems.at[a * npeer + k - 1],
                              device_id=(px, py, pc), device_id_type=pl.DeviceIdType.MESH)
                sends.append(pltpu.make_async_remote_copy(src_ref=rows_for(pidx), dst_ref=out_ref.at[me], **common))
                recvs.append(pltpu.make_async_remote_copy(src_ref=rows_for(pidx), dst_ref=out_ref.at[pidx],
                                                          **common))
        return locals_, sends, recvs


def _ride_plan(ride):
    return _RidePlan(*ride) if ride else _RidePlan([], True)


def _ride_start(copies, when=None):
    locals_, sends, _ = copies

    def go():
        for cp in locals_ + sends:
            cp.start()

    if locals_:
        go() if when is None else pl.when(when)(go)


def _ride_wait(copies, when=None):
    locals_, sends, recvs = copies

    def go():
        for cp in recvs:
            cp.wait_recv()
        for cp in sends:
            cp.wait_send()
        for cp in locals_:
            cp.wait()

    if locals_:
        go() if when is None else pl.when(when)(go)


def _exchange(srcs, gather, name):
    rd = _RidePlan(srcs, gather)

    def body(*refs):
        copies = rd.copies(refs[:rd.n], refs[rd.n:2 * rd.n], refs[2 * rd.n:])
        _ride_start(copies)
        _ride_wait(copies)

    return pl.pallas_call(body, name=name, in_specs=rd.in_specs, out_specs=rd.out_specs,
                          out_shape=rd.out_shape, scratch_shapes=rd.scratch)(*srcs)


def _adamw_math(w, g, m, v):
    m = ADAM_B1 * m + (1.0 - ADAM_B1) * g
    v = ADAM_B2 * v + (1.0 - ADAM_B2) * (g * g)
    m_hat = m / (1.0 - ADAM_B1 ** ADAM_STEP)
    v_hat = v / (1.0 - ADAM_B2 ** ADAM_STEP)
    delta = -ADAM_LR * (m_hat / (jnp.sqrt(v_hat) + ADAM_EPS) + ADAM_WD * w)
    return delta, m, v


def _sum_adamw(parts, w, m, v, name):
    _, k, n = w.shape
    tk = k
    for cand in (256, 176, 160, 128):
        if k % cand == 0:
            tk = cand
            break

    def body(*refs):
        p_refs = refs[:DEPTH]
        w_ref, m_ref, v_ref, g_ref, d_ref, mo_ref, vo_ref = refs[DEPTH:]
        for l in range(DEPTH):
            @pl.when(pl.program_id(0) == l)
            def _(p_ref=p_refs[l]):
                g = p_ref[0].astype(F32)
                for i in range(1, N_DEV):
                    g = g + p_ref[i].astype(F32)
                g_ref[...] = g
                d_ref[...], mo_ref[...], vo_ref[...] = _adamw_math(w_ref[...], g, m_ref[...], v_ref[...])

    row = pl.BlockSpec((None, tk, n), lambda l, i: (l, i, 0))
    shp = jax.ShapeDtypeStruct(w.shape, F32)
    return pl.pallas_call(
        body, name=name, grid=(DEPTH, k // tk),
        in_specs=[pl.BlockSpec((N_DEV, tk, n), lambda l, i: (0, i, 0))] * DEPTH + [row, row, row],
        out_specs=[row, row, row, row],
        out_shape=[shp, shp, shp, shp],
        compiler_params=_cparams(("parallel", "parallel")),
    )(*parts, w, m, v)


def _sum8(parts, name):
    r = parts.shape[1]

    def body(p_ref, g_ref):
        g = p_ref[0]
        for i in range(1, N_DEV):
            g = g + p_ref[i]
        g_ref[...] = g

    return pl.pallas_call(body, name=name, out_shape=jax.ShapeDtypeStruct((r, LANES), F32))(parts)


def _adamw(w, g, m, v):
    def body(w_ref, g_ref, m_ref, v_ref, d_ref, mo_ref, vo_ref):
        d_ref[...], mo_ref[...], vo_ref[...] = _adamw_math(w_ref[...], g_ref[...], m_ref[...], v_ref[...])

    shp = jax.ShapeDtypeStruct(w.shape, F32)
    return pl.pallas_call(body, name="adamw_small", out_shape=[shp, shp, shp])(w, g, m, v)


def _pack(arrays, dtype, row_multiple):
    flat = jnp.concatenate([a.reshape(-1).astype(dtype) for a in arrays])
    n = flat.shape[0]
    unit = row_multiple * LANES
    total = -(-n // unit) * unit
    return jnp.pad(flat, (0, total - n)).reshape(total // LANES, LANES)


def _unpack(flat2d, shapes, lead=()):
    flat = flat2d.reshape(lead + (-1,))
    out, off = [], 0
    for shp in shapes:
        n = int(np.prod(shp))
        out.append(flat[..., off:off + n].reshape(lead + tuple(shp)))
        off += n
    return out


def _shard_to_rows(full, axis):
    shp = full.shape
    k = shp[axis] // N_DEV
    r = full.reshape(shp[:axis] + (N_DEV, k) + shp[axis + 1:])
    return jnp.moveaxis(r, axis, 0)


def _rows_to_full(rows, axis):
    r = jnp.moveaxis(rows, 0, axis)
    shp = r.shape
    return r.reshape(shp[:axis] + (shp[axis] * shp[axis + 1],) + shp[axis + 2:])


def kernel(x, mix_norm_w, w_in, q_norm_w, k_norm_w, hgrn_lb_logits, hgrn_gnorm_w, conv_dw_w, conv_dw_b, conv_ln_w, conv_ln_b, conv_pw_w, conv_pw_b, attn_out_norm_w, conv_out_norm_w, w_out, ffn_norm_w, w_gate, w_up, w_down, loss_target, m_mix_norm_w, m_w_in, m_q_norm_w, m_k_norm_w, m_hgrn_lb_logits, m_hgrn_gnorm_w, m_conv_dw_w, m_conv_dw_b, m_conv_ln_w, m_conv_ln_b, m_conv_pw_w, m_conv_pw_b, m_attn_out_norm_w, m_conv_out_norm_w, m_w_out, m_ffn_norm_w, m_w_gate, m_w_up, m_w_down, v_mix_norm_w, v_w_in, v_q_norm_w, v_k_norm_w, v_hgrn_lb_logits, v_hgrn_gnorm_w, v_conv_dw_w, v_conv_dw_b, v_conv_ln_w, v_conv_ln_b, v_conv_pw_w, v_conv_pw_b, v_attn_out_norm_w, v_conv_out_norm_w, v_w_out, v_ffn_norm_w, v_w_gate, v_w_up, v_w_down):
    w_loc = dict(zip(WEIGHTS, (mix_norm_w, w_in, q_norm_w, k_norm_w, hgrn_lb_logits, hgrn_gnorm_w, conv_dw_w,
                               conv_dw_b, conv_ln_w, conv_ln_b, conv_pw_w, conv_pw_b, attn_out_norm_w,
                               conv_out_norm_w, w_out, ffn_norm_w, w_gate, w_up, w_down)))
    m_loc = dict(zip(WEIGHTS, (m_mix_norm_w, m_w_in, m_q_norm_w, m_k_norm_w, m_hgrn_lb_logits, m_hgrn_gnorm_w,
                               m_conv_dw_w, m_conv_dw_b, m_conv_ln_w, m_conv_ln_b, m_conv_pw_w, m_conv_pw_b,
                               m_attn_out_norm_w, m_conv_out_norm_w, m_w_out, m_ffn_norm_w, m_w_gate, m_w_up,
                               m_w_down)))
    v_loc = dict(zip(WEIGHTS, (v_mix_norm_w, v_w_in, v_q_norm_w, v_k_norm_w, v_hgrn_lb_logits, v_hgrn_gnorm_w,
                               v_conv_dw_w, v_conv_dw_b, v_conv_ln_w, v_conv_ln_b, v_conv_pw_w, v_conv_pw_b,
                               v_attn_out_norm_w, v_conv_out_norm_w, v_w_out, v_ffn_norm_w, v_w_gate, v_w_up,
                               v_w_down)))
    me = _my_index()
    big = tuple(BIG_AXIS)
    sms = tuple(SMALL_SHARD_AXIS)

    col_sharded = tuple(n for n in big if BIG_AXIS[n] == 2)

    def shard_t(n, a):
        return jnp.swapaxes(a, 1, 2) if n in col_sharded else a

    w_send = {n: shard_t(n, w_loc[n]).astype(BF16) for n in big}

    sm_shapes = [w_loc[n].shape for n in sms]
    got_s, got_w_in0 = _exchange([_pack([w_loc[n] for n in sms], F32, 8), w_send["w_in"][0]], True,
                                 "gather_first")
    p_full = {n: w_loc[n] for n in SMALL if n not in SMALL_SHARD_AXIS}
    for n, a in zip(sms, _unpack(got_s, sm_shapes, (N_DEV,))):
        p_full[n] = _rows_to_full(a, SMALL_SHARD_AXIS[n])

    def natural(n, gathered):
        if n in ("w_in", "w_out"):
            return gathered.reshape(-1, gathered.shape[-1])
        return gathered

    def to_send(n, gl):
        if n in ("w_in", "w_out"):
            return gl.reshape(N_DEV, gl.shape[0] // N_DEV, gl.shape[1]).astype(BF16)
        return gl

    class StepPlan:
        def __init__(self):
            self.w = [dict() for _ in range(DEPTH)]
            self.parts = [dict() for _ in range(DEPTH)]
            self.pending = {}
            self.w[0]["w_in"] = natural("w_in", got_w_in0)

        def ride(self, kernel_name, l, grads=None):
            want = []
            if kernel_name == "attn_fwd":
                want = [("w_out", l), ("w_gate", l)]
            elif kernel_name == "hgrn_fwd":
                want = [("w_up", l), ("w_down", l)]
            elif kernel_name == "ffn_fwd" and l + 1 < DEPTH:
                want = [("w_in", l + 1)]
            elif kernel_name == "ffn_bwd" and l + 1 < DEPTH:
                want = [("w_gate", l + 1), ("w_up", l + 1)]
            elif kernel_name == "attn_bwd" and l + 1 < DEPTH:
                want = [("w_in", l + 1), ("w_out", l + 1), ("w_down", l + 1)]
                if l == 0:
                    want += [("w_out", 0)]
            elif kernel_name == "hgrn_bwd" and l == 0:
                want = [(n, 0) for n in ("w_gate", "w_up", "w_down")]
            elif kernel_name == "proj_bwd" and l == 0:
                want = [("w_in", 0)]
            if not want:
                return None
            self.pending[(kernel_name, l)] = want
            if grads is None:
                return [w_send[n][wl] for n, wl in want], True
            return [to_send(n, grads[n][wl]) for n, wl in want], False

        def done(self, kernel_name, l, outs):
            want = self.pending.pop((kernel_name, l), [])
            for (n, wl), out in zip(want, outs):
                if kernel_name.endswith("_fwd"):
                    self.w[wl][n] = natural(n, out)
                else:
                    self.parts[wl][n] = out

    plan = StepPlan()
    loss_part, grad_x, g = _local_step(x, loss_target, p_full, plan)
    loss = lax.psum(loss_part[0, 0], MESH_AXES)

    pw = g["conv_pw_w"]
    k_pw = w_loc["conv_pw_w"].shape[1]
    pw_send = jnp.moveaxis(pw.reshape(DEPTH, N_DEV, k_pw, pw.shape[-1]), 1, 0).reshape(N_DEV, -1, LANES)
    gathered_small = [n for n in SMALL if n != "conv_pw_w"]
    small_shapes = [g[n].shape for n in gathered_small]
    small_parts, pw_parts = _exchange([_pack([g[n] for n in gathered_small], F32, 8), pw_send],
                                      [True, False], "exchange_small_grads")
    big_out = {}
    for n in big:
        res = _sum_adamw([plan.parts[l][n] for l in range(DEPTH)], shard_t(n, w_loc[n]), shard_t(n, m_loc[n]),
                         shard_t(n, v_loc[n]), "sum_adamw_" + n)
        big_out[n] = [shard_t(n, r) for r in res]

    g_small = dict(zip(gathered_small, _unpack(_sum8(small_parts, "sum_small_grads"), small_shapes)))
    g_small["conv_pw_w"] = _sum8(pw_parts, "sum_conv_pw_grads").reshape(w_loc["conv_pw_w"].shape)
    for n in sms:
        if n == "conv_pw_w":
            continue
        ax = SMALL_SHARD_AXIS[n]
        k = w_loc[n].shape[ax]
        g_small[n] = lax.dynamic_slice_in_dim(g_small[n], me * k, k, axis=ax)
    loc_shapes = [w_loc[n].shape for n in SMALL]
    packed = [_pack([d[n] for n in SMALL], F32, 8) for d in (w_loc, g_small, m_loc, v_loc)]
    res = _adamw(*packed)
    small_out = [g_small] + [dict(zip(SMALL, _unpack(r, loc_shapes))) for r in res]

    def pick(i, n):
        return big_out[n][i] if n in BIG_AXIS else small_out[i][n]

    return (loss, grad_x) + tuple(pick(i, n) for i in range(4) for n in WEIGHTS)
```
